```python
import jax, jax.numpy as jnp
from jax import lax
import numpy as np

D_MODEL = 1024
BATCH = 8
SEQ = 4096
DEPTH = 4

N_MIXERS = 2
N_MLA_LAYERS = (DEPTH + 1) // 2
N_SGU_LAYERS = DEPTH // 2
MLA_HEADS = 8
QK_NOPE_DIM = 128
QK_ROPE_DIM = 64
QK_HEAD_DIM = QK_NOPE_DIM + QK_ROPE_DIM
V_HEAD_DIM = 128
Q_LORA_RANK = 256
KV_LORA_RANK = 128
ROPE_THETA = 10000.0
Q_BLOCK = 128
SGU_CHUNK = 128
SGU_WIDTH = 2 * D_MODEL
SGU_GROUPS = 8
SGU_GROUP_DIM = SGU_WIDTH // SGU_GROUPS
FFN_HIDDEN = 4 * D_MODEL
NORM_EPS = 1e-6
LN_EPS = 1e-5

kernel_name = "hybrid_mla_chunked_sgu_trunk"


def rms_norm(x, g):
    xf = x.astype(jnp.float32)
    y = xf * lax.rsqrt(jnp.mean(xf * xf, axis=-1, keepdims=True) + NORM_EPS)
    return (y * g.astype(jnp.float32)).astype(x.dtype)


def layer_norm(x, g, b):
    xf = x.astype(jnp.float32)
    mu = jnp.mean(xf, axis=-1, keepdims=True)
    var = jnp.mean(jnp.square(xf - mu), axis=-1, keepdims=True)
    y = (xf - mu) * lax.rsqrt(var + LN_EPS)
    return (y * g.astype(jnp.float32) + b.astype(jnp.float32)).astype(x.dtype)


def apply_rope(x, cos, sin):
    x1, x2 = jnp.split(x.astype(jnp.float32), 2, axis=-1)
    out = jnp.concatenate([x1 * cos - x2 * sin, x2 * cos + x1 * sin], axis=-1)
    return out.astype(x.dtype)


def mla_mixer(h, positions, w_dkv, q_norm, kv_norm, w_uq, w_ukv, w_o):
    B, S, _ = h.shape
    lat = h @ w_dkv
    c_q, c_kv, k_rope = jnp.split(lat, [Q_LORA_RANK, Q_LORA_RANK + KV_LORA_RANK], axis=-1)
    c_q = rms_norm(c_q, q_norm)
    c_kv = rms_norm(c_kv, kv_norm)
    q = (c_q @ w_uq).reshape(B, S, MLA_HEADS, QK_HEAD_DIM)
    q_nope, q_rope = jnp.split(q, [QK_NOPE_DIM], axis=-1)
    kv = (c_kv @ w_ukv).reshape(B, S, MLA_HEADS, QK_NOPE_DIM + V_HEAD_DIM)
    k_nope, v = jnp.split(kv, [QK_NOPE_DIM], axis=-1)

    inv_freq = ROPE_THETA ** (-jnp.arange(0, QK_ROPE_DIM, 2, dtype=jnp.float32) / QK_ROPE_DIM)
    ang = positions.astype(jnp.float32)[..., None] * inv_freq
    cos, sin = jnp.cos(ang), jnp.sin(ang)
    q_rope = apply_rope(q_rope, cos[:, :, None, :], sin[:, :, None, :])
    k_rope = apply_rope(k_rope, cos, sin)

    nb = S // Q_BLOCK
    qn_b = q_nope.reshape(B, nb, Q_BLOCK, MLA_HEADS, QK_NOPE_DIM).transpose(1, 0, 2, 3, 4)
    qr_b = q_rope.reshape(B, nb, Q_BLOCK, MLA_HEADS, QK_ROPE_DIM).transpose(1, 0, 2, 3, 4)
    key_idx = jnp.arange(S)
    scale = QK_HEAD_DIM ** -0.5

    def attend(args):
        qn, qr, blk = args
        s = (jnp.einsum('bqhd,bkhd->bhqk', qn, k_nope)
             + jnp.einsum('bqhr,bkr->bhqk', qr, k_rope))
        s = s.astype(jnp.float32) * scale
        q_idx = blk * Q_BLOCK + jnp.arange(Q_BLOCK)
        causal = key_idx[None, :] <= q_idx[:, None]
        s = jnp.where(causal[None, None], s, -jnp.inf)
        p = jax.nn.softmax(s, axis=-1).astype(v.dtype)
        return jnp.einsum('bhqk,bkhd->bqhd', p, v)

    o = lax.map(attend, (qn_b, qr_b, jnp.arange(nb)))
    o = o.transpose(1, 0, 2, 3, 4).reshape(B, S, MLA_HEADS * V_HEAD_DIM)
    return o @ w_o


def chunked_sgu_mixer(h, w_in, ln_g, ln_b, w_spatial, b_spatial, w_out):
    B, S, _ = h.shape
    z = jax.nn.gelu(h @ w_in, approximate=False)
    u, v = jnp.split(z, 2, axis=-1)
    v = layer_norm(v, ln_g, ln_b)
    nc = S // SGU_CHUNK
    vg = v.reshape(B, nc, SGU_CHUNK, SGU_GROUPS, SGU_GROUP_DIM)
    w_causal = jnp.tril(w_spatial)
    mixed = (jnp.einsum('gts,bcsgd->bctgd', w_causal, vg)
             + b_spatial.T[None, None, :, :, None])
    v = mixed.reshape(B, S, SGU_WIDTH)
    return (u * v) @ w_out


def sq_relu_mlp(h, w_up, w_down):
    return jnp.square(jax.nn.relu(h @ w_up)) @ w_down


def _fwd_setup_inputs(seed: int = 0) -> dict:
    key = jax.random.key(seed)
    ks = jax.random.split(key, 24)
    f32 = jnp.float32

    def nrm(k, shape, fan_in, gain=1.0):
        return jax.random.normal(k, shape, f32) * (gain * fan_in ** -0.5)

    def gain(k, shape):
        return 1.0 + 0.02 * jax.random.normal(k, shape, f32)

    x = jax.random.normal(ks[0], (BATCH, SEQ, D_MODEL), f32)
    offset = jax.random.randint(ks[1], (BATCH, 1), 0, 1024, dtype=jnp.int32)
    positions = offset + jnp.arange(SEQ, dtype=jnp.int32)[None, :]

    return {
        "x": x,
        "positions": positions,
        "norm_mix": gain(ks[2], (DEPTH, D_MODEL)),
        "norm_ffn": gain(ks[3], (DEPTH, D_MODEL)),
        "final_norm": gain(ks[4], (D_MODEL,)),
        "mla_w_dkv": nrm(ks[5], (N_MLA_LAYERS, D_MODEL, Q_LORA_RANK + KV_LORA_RANK + QK_ROPE_DIM), D_MODEL),
        "mla_q_norm": gain(ks[6], (N_MLA_LAYERS, Q_LORA_RANK)),
        "mla_kv_norm": gain(ks[7], (N_MLA_LAYERS, KV_LORA_RANK)),
        "mla_w_uq": nrm(ks[8], (N_MLA_LAYERS, Q_LORA_RANK, MLA_HEADS * QK_HEAD_DIM), Q_LORA_RANK),
        "mla_w_ukv": nrm(ks[9], (N_MLA_LAYERS, KV_LORA_RANK, MLA_HEADS * (QK_NOPE_DIM + V_HEAD_DIM)), KV_LORA_RANK),
        "mla_w_o": nrm(ks[10], (N_MLA_LAYERS, MLA_HEADS * V_HEAD_DIM, D_MODEL), MLA_HEADS * V_HEAD_DIM),
        "sgu_w_in": nrm(ks[11], (N_SGU_LAYERS, D_MODEL, 2 * SGU_WIDTH), D_MODEL),
        "sgu_ln_g": gain(ks[12], (N_SGU_LAYERS, SGU_WIDTH)),
        "sgu_ln_b": 0.02 * jax.random.normal(ks[13], (N_SGU_LAYERS, SGU_WIDTH), f32),
        "sgu_w_spatial": nrm(ks[14], (N_SGU_LAYERS, SGU_GROUPS, SGU_CHUNK, SGU_CHUNK), SGU_CHUNK, 0.5),
        "sgu_b_spatial": gain(ks[15], (N_SGU_LAYERS, SGU_GROUPS, SGU_CHUNK)),
        "sgu_w_out": nrm(ks[16], (N_SGU_LAYERS, SGU_WIDTH, D_MODEL), SGU_WIDTH),
        "ffn_w_up": nrm(ks[17], (DEPTH, D_MODEL, FFN_HIDDEN), D_MODEL),
        "ffn_w_down": nrm(ks[18], (DEPTH, FFN_HIDDEN, D_MODEL), FFN_HIDDEN),
    }


def _fwd_reference(x, positions, norm_mix, norm_ffn, final_norm,
              mla_w_dkv, mla_q_norm, mla_kv_norm, mla_w_uq, mla_w_ukv, mla_w_o,
              sgu_w_in, sgu_ln_g, sgu_ln_b, sgu_w_spatial, sgu_b_spatial, sgu_w_out,
              ffn_w_up, ffn_w_down):
    for i in range(DEPTH):
        h = rms_norm(x, norm_mix[i])
        j = i // N_MIXERS
        if i % N_MIXERS == 0:
            x = x + mla_mixer(h, positions, mla_w_dkv[j], mla_q_norm[j], mla_kv_norm[j],
                              mla_w_uq[j], mla_w_ukv[j], mla_w_o[j])
        else:
            x = x + chunked_sgu_mixer(h, sgu_w_in[j], sgu_ln_g[j], sgu_ln_b[j],
                                      sgu_w_spatial[j], sgu_b_spatial[j], sgu_w_out[j])
        h = rms_norm(x, norm_ffn[i])
        x = x + sq_relu_mlp(h, ffn_w_up[i], ffn_w_down[i])
    return rms_norm(x, final_norm)


import jax as _jax
import jax.numpy as _jnp

TWIN_FORMAT = 'train_step'
FWD_PARAMS = ['x', 'positions', 'norm_mix', 'norm_ffn', 'final_norm', 'mla_w_dkv', 'mla_q_norm', 'mla_kv_norm', 'mla_w_uq', 'mla_w_ukv', 'mla_w_o', 'sgu_w_in', 'sgu_ln_g', 'sgu_ln_b', 'sgu_w_spatial', 'sgu_b_spatial', 'sgu_w_out', 'ffn_w_up', 'ffn_w_down']
TWIN_WEIGHTS = ['norm_mix', 'norm_ffn', 'final_norm', 'mla_w_dkv', 'mla_q_norm', 'mla_kv_norm', 'mla_w_uq', 'mla_w_ukv', 'mla_w_o', 'sgu_w_in', 'sgu_ln_g', 'sgu_ln_b', 'sgu_w_spatial', 'sgu_b_spatial', 'sgu_w_out', 'ffn_w_up', 'ffn_w_down']
TWIN_DIFF_INPUT = 'x'
TWIN_INPUTS = ['x', 'positions', 'norm_mix', 'norm_ffn', 'final_norm', 'mla_w_dkv', 'mla_q_norm', 'mla_kv_norm', 'mla_w_uq', 'mla_w_ukv', 'mla_w_o', 'sgu_w_in', 'sgu_ln_g', 'sgu_ln_b', 'sgu_w_spatial', 'sgu_b_spatial', 'sgu_w_out', 'ffn_w_up', 'ffn_w_down', 'loss_target', 'm_norm_mix', 'm_norm_ffn', 'm_final_norm', 'm_mla_w_dkv', 'm_mla_q_norm', 'm_mla_kv_norm', 'm_mla_w_uq', 'm_mla_w_ukv', 'm_mla_w_o', 'm_sgu_w_in', 'm_sgu_ln_g', 'm_sgu_ln_b', 'm_sgu_w_spatial', 'm_sgu_b_spatial', 'm_sgu_w_out', 'm_ffn_w_up', 'm_ffn_w_down', 'v_norm_mix', 'v_norm_ffn', 'v_final_norm', 'v_mla_w_dkv', 'v_mla_q_norm', 'v_mla_kv_norm', 'v_mla_w_uq', 'v_mla_w_ukv', 'v_mla_w_o', 'v_sgu_w_in', 'v_sgu_ln_g', 'v_sgu_ln_b', 'v_sgu_w_spatial', 'v_sgu_b_spatial', 'v_sgu_w_out', 'v_ffn_w_up', 'v_ffn_w_down']
TWIN_OUTPUTS = ['loss', 'grad_x', 'grad_norm_mix', 'grad_norm_ffn', 'grad_final_norm', 'grad_mla_w_dkv', 'grad_mla_q_norm', 'grad_mla_kv_norm', 'grad_mla_w_uq', 'grad_mla_w_ukv', 'grad_mla_w_o', 'grad_sgu_w_in', 'grad_sgu_ln_g', 'grad_sgu_ln_b', 'grad_sgu_w_spatial', 'grad_sgu_b_spatial', 'grad_sgu_w_out', 'grad_ffn_w_up', 'grad_ffn_w_down', 'delta_norm_mix', 'delta_norm_ffn', 'delta_final_norm', 'delta_mla_w_dkv', 'delta_mla_q_norm', 'delta_mla_kv_norm', 'delta_mla_w_uq', 'delta_mla_w_ukv', 'delta_mla_w_o', 'delta_sgu_w_in', 'delta_sgu_ln_g', 'delta_sgu_ln_b', 'delta_sgu_w_spatial', 'delta_sgu_b_spatial', 'delta_sgu_w_out', 'delta_ffn_w_up', 'delta_ffn_w_down', 'new_m_norm_mix', 'new_m_norm_ffn', 'new_m_final_norm', 'new_m_mla_w_dkv', 'new_m_mla_q_norm', 'new_m_mla_kv_norm', 'new_m_mla_w_uq', 'new_m_mla_w_ukv', 'new_m_mla_w_o', 'new_m_sgu_w_in', 'new_m_sgu_ln_g', 'new_m_sgu_ln_b', 'new_m_sgu_w_spatial', 'new_m_sgu_b_spatial', 'new_m_sgu_w_out', 'new_m_ffn_w_up', 'new_m_ffn_w_down', 'new_v_norm_mix', 'new_v_norm_ffn', 'new_v_final_norm', 'new_v_mla_w_dkv', 'new_v_mla_q_norm', 'new_v_mla_kv_norm', 'new_v_mla_w_uq', 'new_v_mla_w_ukv', 'new_v_mla_w_o', 'new_v_sgu_w_in', 'new_v_sgu_ln_g', 'new_v_sgu_ln_b', 'new_v_sgu_w_spatial', 'new_v_sgu_b_spatial', 'new_v_sgu_w_out', 'new_v_ffn_w_up', 'new_v_ffn_w_down']
TWIN_LEAF_KINDS = {'loss': 'loss', 'grad_x': 'grad_x', 'grad_norm_mix': 'grad_w', 'grad_norm_ffn': 'grad_w', 'grad_final_norm': 'grad_w', 'grad_mla_w_dkv': 'grad_w', 'grad_mla_q_norm': 'grad_w', 'grad_mla_kv_norm': 'grad_w', 'grad_mla_w_uq': 'grad_w', 'grad_mla_w_ukv': 'grad_w', 'grad_mla_w_o': 'grad_w', 'grad_sgu_w_in': 'grad_w', 'grad_sgu_ln_g': 'grad_w', 'grad_sgu_ln_b': 'grad_w', 'grad_sgu_w_spatial': 'grad_w', 'grad_sgu_b_spatial': 'grad_w', 'grad_sgu_w_out': 'grad_w', 'grad_ffn_w_up': 'grad_w', 'grad_ffn_w_down': 'grad_w', 'delta_norm_mix': 'delta_w', 'delta_norm_ffn': 'delta_w', 'delta_final_norm': 'delta_w', 'delta_mla_w_dkv': 'delta_w', 'delta_mla_q_norm': 'delta_w', 'delta_mla_kv_norm': 'delta_w', 'delta_mla_w_uq': 'delta_w', 'delta_mla_w_ukv': 'delta_w', 'delta_mla_w_o': 'delta_w', 'delta_sgu_w_in': 'delta_w', 'delta_sgu_ln_g': 'delta_w', 'delta_sgu_ln_b': 'delta_w', 'delta_sgu_w_spatial': 'delta_w', 'delta_sgu_b_spatial': 'delta_w', 'delta_sgu_w_out': 'delta_w', 'delta_ffn_w_up': 'delta_w', 'delta_ffn_w_down': 'delta_w', 'new_m_norm_mix': 'new_m', 'new_m_norm_ffn': 'new_m', 'new_m_final_norm': 'new_m', 'new_m_mla_w_dkv': 'new_m', 'new_m_mla_q_norm': 'new_m', 'new_m_mla_kv_norm': 'new_m', 'new_m_mla_w_uq': 'new_m', 'new_m_mla_w_ukv': 'new_m', 'new_m_mla_w_o': 'new_m', 'new_m_sgu_w_in': 'new_m', 'new_m_sgu_ln_g': 'new_m', 'new_m_sgu_ln_b': 'new_m', 'new_m_sgu_w_spatial': 'new_m', 'new_m_sgu_b_spatial': 'new_m', 'new_m_sgu_w_out': 'new_m', 'new_m_ffn_w_up': 'new_m', 'new_m_ffn_w_down': 'new_m', 'new_v_norm_mix': 'new_v', 'new_v_norm_ffn': 'new_v', 'new_v_final_norm': 'new_v', 'new_v_mla_w_dkv': 'new_v', 'new_v_mla_q_norm': 'new_v', 'new_v_mla_kv_norm': 'new_v', 'new_v_mla_w_uq': 'new_v', 'new_v_mla_w_ukv': 'new_v', 'new_v_mla_w_o': 'new_v', 'new_v_sgu_w_in': 'new_v', 'new_v_sgu_ln_g': 'new_v', 'new_v_sgu_ln_b': 'new_v', 'new_v_sgu_w_spatial': 'new_v', 'new_v_sgu_b_spatial': 'new_v', 'new_v_sgu_w_out': 'new_v', 'new_v_ffn_w_up': 'new_v', 'new_v_ffn_w_down': 'new_v'}


def _forward(args):
    return _fwd_reference(*[args[k] for k in FWD_PARAMS])


def _output_shape():
    def fwd():
        inp = _fwd_setup_inputs(0)
        return _fwd_reference(*[inp[k] for k in FWD_PARAMS])
    out = _jax.eval_shape(fwd)
    return out.shape, out.dtype

N_MICROBATCH = 1
ADAM_LR = 0.001
ADAM_B1 = 0.9
ADAM_B2 = 0.999
ADAM_EPS = 1e-08
ADAM_WD = 0.01
ADAM_STEP = 10
PER_EXAMPLE_BATCH_AXIS = {'x': 0, 'positions': 0, 'loss_target': 0}
SHARED_INPUTS = []
_WEIGHT_DTYPES = {'norm_mix': _jnp.float32, 'norm_ffn': _jnp.float32, 'final_norm': _jnp.float32, 'mla_w_dkv': _jnp.float32, 'mla_q_norm': _jnp.float32, 'mla_kv_norm': _jnp.float32, 'mla_w_uq': _jnp.float32, 'mla_w_ukv': _jnp.float32, 'mla_w_o': _jnp.float32, 'sgu_w_in': _jnp.float32, 'sgu_ln_g': _jnp.float32, 'sgu_ln_b': _jnp.float32, 'sgu_w_spatial': _jnp.float32, 'sgu_b_spatial': _jnp.float32, 'sgu_w_out': _jnp.float32, 'ffn_w_up': _jnp.float32, 'ffn_w_down': _jnp.float32}
MOMENT_SCALE = {'norm_mix': 6.858138e-02, 'norm_ffn': 1.398832e-01, 'final_norm': 3.374961e+01, 'mla_w_dkv': 9.056148e-02, 'mla_q_norm': 5.754625e-02, 'mla_kv_norm': 1.499459e-01, 'mla_w_uq': 2.386956e-02, 'mla_w_ukv': 3.838065e-02, 'mla_w_o': 4.920051e-02, 'sgu_w_in': 4.121727e-02, 'sgu_ln_g': 1.548235e-02, 'sgu_ln_b': 1.487150e-02, 'sgu_w_spatial': 4.372329e-02, 'sgu_b_spatial': 6.110036e-02, 'sgu_w_out': 9.821652e-02, 'ffn_w_up': 7.068752e-02, 'ffn_w_down': 1.607675e-01}


def _to_microbatches(a, axis):
    t = _jnp.moveaxis(a, axis, 0)
    t = t.reshape((N_MICROBATCH, t.shape[0] // N_MICROBATCH) + t.shape[1:])
    return _jnp.moveaxis(t, 1, axis + 1)


def setup_inputs(seed: int = 0) -> dict:
    inp = _fwd_setup_inputs(seed)
    key = _jax.random.fold_in(_jax.random.key(seed), 7919)
    shape, _ = _output_shape()
    out = dict(inp)
    out["loss_target"] = _jax.random.normal(_jax.random.fold_in(key, 0), shape, _jnp.float32)
    for i, name in enumerate(TWIN_WEIGHTS):
        w = inp[name].astype(_jnp.float32)
        if MOMENT_SCALE is None:
            s = _jnp.sqrt(_jnp.mean(_jnp.square(w)) + 1e-30)
        else:
            s = MOMENT_SCALE[name]
        km, kv = _jax.random.split(_jax.random.fold_in(key, i + 1))
        out[name] = w
        out["m_" + name] = s * _jax.random.normal(km, w.shape, _jnp.float32)
        out["v_" + name] = (s * s) * _jax.random.uniform(kv, w.shape, _jnp.float32, 0.5, 1.5)
    if N_MICROBATCH > 1:
        for name, axis in PER_EXAMPLE_BATCH_AXIS.items():
            out[name] = _to_microbatches(out[name], axis)
    return {'x': out['x'], 'positions': out['positions'], 'norm_mix': out['norm_mix'], 'norm_ffn': out['norm_ffn'], 'final_norm': out['final_norm'], 'mla_w_dkv': out['mla_w_dkv'], 'mla_q_norm': out['mla_q_norm'], 'mla_kv_norm': out['mla_kv_norm'], 'mla_w_uq': out['mla_w_uq'], 'mla_w_ukv': out['mla_w_ukv'], 'mla_w_o': out['mla_w_o'], 'sgu_w_in': out['sgu_w_in'], 'sgu_ln_g': out['sgu_ln_g'], 'sgu_ln_b': out['sgu_ln_b'], 'sgu_w_spatial': out['sgu_w_spatial'], 'sgu_b_spatial': out['sgu_b_spatial'], 'sgu_w_out': out['sgu_w_out'], 'ffn_w_up': out['ffn_w_up'], 'ffn_w_down': out['ffn_w_down'], 'loss_target': out['loss_target'], 'm_norm_mix': out['m_norm_mix'], 'm_norm_ffn': out['m_norm_ffn'], 'm_final_norm': out['m_final_norm'], 'm_mla_w_dkv': out['m_mla_w_dkv'], 'm_mla_q_norm': out['m_mla_q_norm'], 'm_mla_kv_norm': out['m_mla_kv_norm'], 'm_mla_w_uq': out['m_mla_w_uq'], 'm_mla_w_ukv': out['m_mla_w_ukv'], 'm_mla_w_o': out['m_mla_w_o'], 'm_sgu_w_in': out['m_sgu_w_in'], 'm_sgu_ln_g': out['m_sgu_ln_g'], 'm_sgu_ln_b': out['m_sgu_ln_b'], 'm_sgu_w_spatial': out['m_sgu_w_spatial'], 'm_sgu_b_spatial': out['m_sgu_b_spatial'], 'm_sgu_w_out': out['m_sgu_w_out'], 'm_ffn_w_up': out['m_ffn_w_up'], 'm_ffn_w_down': out['m_ffn_w_down'], 'v_norm_mix': out['v_norm_mix'], 'v_norm_ffn': out['v_norm_ffn'], 'v_final_norm': out['v_final_norm'], 'v_mla_w_dkv': out['v_mla_w_dkv'], 'v_mla_q_norm': out['v_mla_q_norm'], 'v_mla_kv_norm': out['v_mla_kv_norm'], 'v_mla_w_uq': out['v_mla_w_uq'], 'v_mla_w_ukv': out['v_mla_w_ukv'], 'v_mla_w_o': out['v_mla_w_o'], 'v_sgu_w_in': out['v_sgu_w_in'], 'v_sgu_ln_g': out['v_sgu_ln_g'], 'v_sgu_ln_b': out['v_sgu_ln_b'], 'v_sgu_w_spatial': out['v_sgu_w_spatial'], 'v_sgu_b_spatial': out['v_sgu_b_spatial'], 'v_sgu_w_out': out['v_sgu_w_out'], 'v_ffn_w_up': out['v_ffn_w_up'], 'v_ffn_w_down': out['v_ffn_w_down']}


def _loss(weights, diff, rest, loss_target):
    with _jax.named_scope("forward"):
        args = {**rest, TWIN_DIFF_INPUT: diff, **{k: w.astype(_WEIGHT_DTYPES[k]) for k, w in weights.items()}}
        y = _forward(args)
    with _jax.named_scope("loss_head"):
        err = _jnp.square(y.astype(_jnp.float32) - loss_target)
        return 0.5 * _jnp.sum(_jnp.mean(err, axis=-1)) if err.ndim else 0.5 * err


def _adamw(w, g, m, v):
    m = ADAM_B1 * m + (1.0 - ADAM_B1) * g
    v = ADAM_B2 * v + (1.0 - ADAM_B2) * _jnp.square(g)
    m_hat = m / (1.0 - ADAM_B1 ** ADAM_STEP)
    v_hat = v / (1.0 - ADAM_B2 ** ADAM_STEP)
    delta = -ADAM_LR * (m_hat / (_jnp.sqrt(v_hat) + ADAM_EPS) + ADAM_WD * w)
    return delta, m, v


def reference(x, positions, norm_mix, norm_ffn, final_norm, mla_w_dkv, mla_q_norm, mla_kv_norm, mla_w_uq, mla_w_ukv, mla_w_o, sgu_w_in, sgu_ln_g, sgu_ln_b, sgu_w_spatial, sgu_b_spatial, sgu_w_out, ffn_w_up, ffn_w_down, loss_target, m_norm_mix, m_norm_ffn, m_final_norm, m_mla_w_dkv, m_mla_q_norm, m_mla_kv_norm, m_mla_w_uq, m_mla_w_ukv, m_mla_w_o, m_sgu_w_in, m_sgu_ln_g, m_sgu_ln_b, m_sgu_w_spatial, m_sgu_b_spatial, m_sgu_w_out, m_ffn_w_up, m_ffn_w_down, v_norm_mix, v_norm_ffn, v_final_norm, v_mla_w_dkv, v_mla_q_norm, v_mla_kv_norm, v_mla_w_uq, v_mla_w_ukv, v_mla_w_o, v_sgu_w_in, v_sgu_ln_g, v_sgu_ln_b, v_sgu_w_spatial, v_sgu_b_spatial, v_sgu_w_out, v_ffn_w_up, v_ffn_w_down):
    given = dict(x=x, positions=positions, norm_mix=norm_mix, norm_ffn=norm_ffn, final_norm=final_norm, mla_w_dkv=mla_w_dkv, mla_q_norm=mla_q_norm, mla_kv_norm=mla_kv_norm, mla_w_uq=mla_w_uq, mla_w_ukv=mla_w_ukv, mla_w_o=mla_w_o, sgu_w_in=sgu_w_in, sgu_ln_g=sgu_ln_g, sgu_ln_b=sgu_ln_b, sgu_w_spatial=sgu_w_spatial, sgu_b_spatial=sgu_b_spatial, sgu_w_out=sgu_w_out, ffn_w_up=ffn_w_up, ffn_w_down=ffn_w_down, loss_target=loss_target, m_norm_mix=m_norm_mix, m_norm_ffn=m_norm_ffn, m_final_norm=m_final_norm, m_mla_w_dkv=m_mla_w_dkv, m_mla_q_norm=m_mla_q_norm, m_mla_kv_norm=m_mla_kv_norm, m_mla_w_uq=m_mla_w_uq, m_mla_w_ukv=m_mla_w_ukv, m_mla_w_o=m_mla_w_o, m_sgu_w_in=m_sgu_w_in, m_sgu_ln_g=m_sgu_ln_g, m_sgu_ln_b=m_sgu_ln_b, m_sgu_w_spatial=m_sgu_w_spatial, m_sgu_b_spatial=m_sgu_b_spatial, m_sgu_w_out=m_sgu_w_out, m_ffn_w_up=m_ffn_w_up, m_ffn_w_down=m_ffn_w_down, v_norm_mix=v_norm_mix, v_norm_ffn=v_norm_ffn, v_final_norm=v_final_norm, v_mla_w_dkv=v_mla_w_dkv, v_mla_q_norm=v_mla_q_norm, v_mla_kv_norm=v_mla_kv_norm, v_mla_w_uq=v_mla_w_uq, v_mla_w_ukv=v_mla_w_ukv, v_mla_w_o=v_mla_w_o, v_sgu_w_in=v_sgu_w_in, v_sgu_ln_g=v_sgu_ln_g, v_sgu_ln_b=v_sgu_ln_b, v_sgu_w_spatial=v_sgu_w_spatial, v_sgu_b_spatial=v_sgu_b_spatial, v_sgu_w_out=v_sgu_w_out, v_ffn_w_up=v_ffn_w_up, v_ffn_w_down=v_ffn_w_down)
    weights = {n: given[n] for n in TWIN_WEIGHTS}
    shared = {n: given[n] for n in SHARED_INPUTS}
    per_example = {n: given[n] for n in ['x', 'positions']}
    grad_fn = _jax.value_and_grad(_loss, argnums=(0, 1))

    def one_microbatch(ex, loss_target):
        ex = dict(ex)
        diff = ex.pop(TWIN_DIFF_INPUT)
        return grad_fn(weights, diff, {**shared, **ex}, loss_target)

    if N_MICROBATCH == 1:
        loss, (grad_w, grad_x) = one_microbatch(per_example, given["loss_target"])
    else:
        def body(carry, xs):
            loss_sum, grad_sum = carry
            l_k, (gw_k, gx_k) = one_microbatch(xs[0], xs[1])
            with _jax.named_scope("update"):
                return (loss_sum + l_k, _jax.tree.map(_jnp.add, grad_sum, gw_k)), gx_k

        init = (_jnp.zeros((), _jnp.float32), _jax.tree.map(_jnp.zeros_like, weights))
        (loss, grad_w), grad_x = _jax.lax.scan(body, init, (per_example, given["loss_target"]))
    with _jax.named_scope("update"):
        delta_w, new_m, new_v = {}, {}, {}
        for n in TWIN_WEIGHTS:
            delta_w[n], new_m[n], new_v[n] = _adamw(weights[n], grad_w[n], given["m_" + n], given["v_" + n])
    return (loss, grad_x, *[grad_w[n] for n in TWIN_WEIGHTS], *[delta_w[n] for n in TWIN_WEIGHTS],
            *[new_m[n] for n in TWIN_WEIGHTS], *[new_v[n] for n in TWIN_WEIGHTS])
```

```python
import functools
import math

import jax
import jax.numpy as jnp
from jax import lax
from jax.experimental import pallas as pl
from jax.experimental.pallas import tpu as pltpu

F32 = jnp.float32
BF16 = jnp.bfloat16
MESH = pl.DeviceIdType.MESH

DEPTH = 4
HEADS = 8
NOPE = 128
ROPE = 64
VHEAD = 128
QK_HEAD = NOPE + ROPE
Q_RANK = 256
KV_RANK = 128
HEAD_PAD = 256
LAT_PAD = 512
ROPE_THETA = 10000.0
SGU_CHUNK = 128
SGU_GROUPS = 8
NORM_EPS = 1e-6
LN_EPS = 1e-5
ADAM_LR, ADAM_B1, ADAM_B2, ADAM_EPS, ADAM_WD, ADAM_STEP = 0.001, 0.9, 0.999, 1e-08, 0.01, 10

N_SHARDS = 4
LANE = 128
VMEM_LIMIT_BYTES = 48 * 1024 * 1024
ATT_TILE = 512

NN = (((1,), (0,)), ((), ()))
NT = (((1,), (1,)), ((), ()))
TN = (((0,), (0,)), ((), ()))


def _params(sem):
    return pltpu.CompilerParams(dimension_semantics=sem, vmem_limit_bytes=VMEM_LIMIT_BYTES)


def _tile(n, pref):
    t = min(n, pref)
    while n % t:
        t //= 2
    return t


def _matmul(name, a, b, a_spec, b_spec, dims, grid, tile, outs, extras=(), epilogue=None, aliased=()):
    nk, ne, no = grid[2], len(extras), len(outs)

    def body(a_ref, b_ref, *rest):
        e_refs, o_refs, acc_ref = rest[:ne], rest[ne + len(aliased):ne + len(aliased) + no], rest[-1]
        k = pl.program_id(2)

        @pl.when(k == 0)
        def _():
            acc_ref[...] = jnp.zeros_like(acc_ref)

        acc_ref[...] += lax.dot_general(a_ref[...].astype(BF16), b_ref[...].astype(BF16), dims,
                                        preferred_element_type=F32)

        @pl.when(k == nk - 1)
        def _():
            acc = acc_ref[...]
            vals = (acc,) if epilogue is None else epilogue(acc, *[e[...] for e in e_refs])
            for o_ref, v in zip(o_refs, vals):
                o_ref[...] = v.astype(o_ref.dtype)

    n_in = 2 + ne
    return pl.pallas_call(
        body, name=name, grid=grid,
        in_specs=[a_spec, b_spec] + [s for _, s in extras] + [pl.BlockSpec(memory_space=pl.ANY)] * len(aliased),
        out_specs=[s for _, s in outs], out_shape=[s for s, _ in outs],
        scratch_shapes=[pltpu.VMEM(tile, F32)],
        input_output_aliases={n_in + i: o for i, (_, o) in enumerate(aliased)},
        compiler_params=_params(("parallel", "parallel", "arbitrary")),
    )(a, b, *[e for e, _ in extras], *[arr for arr, _ in aliased])


def _mm(name, a, b, out_dtypes=(F32,), epilogue=None, extras=(), tm=512, tn=512, tk=512, nt=False):
    m, kd = a.shape
    n = b.shape[0] if nt else b.shape[1]
    tm, tn, tk = _tile(m, tm), _tile(n, tn), _tile(kd, tk)
    o_spec = pl.BlockSpec((tm, tn), lambda i, j, k: (i, j))
    b_spec = pl.BlockSpec((tn, tk), lambda i, j, k: (j, k)) if nt else pl.BlockSpec((tk, tn), lambda i, j, k: (k, j))
    return _matmul(name, a, b, pl.BlockSpec((tm, tk), lambda i, j, k: (i, k)), b_spec, NT if nt else NN,
                   (m // tm, n // tn, kd // tk), (tm, tn),
                   [(jax.ShapeDtypeStruct((m, n), d), o_spec) for d in out_dtypes],
                   [(e, o_spec) for e in extras], epilogue)


def _mm_tn(name, a, b, out_dtypes=(F32,), tm=512, tn=512, tk=512):
    s, m = a.shape
    n = b.shape[1]
    tm, tn, tk = _tile(m, tm), _tile(n, tn), _tile(s, tk)
    o_spec = pl.BlockSpec((tm, tn), lambda i, j, k: (i, j))
    return _matmul(name, a, b, pl.BlockSpec((tk, tm), lambda i, j, k: (k, i)),
                   pl.BlockSpec((tk, tn), lambda i, j, k: (k, j)), TN, (m // tm, n // tn, s // tk), (tm, tn),
                   [(jax.ShapeDtypeStruct((m, n), d), o_spec) for d in out_dtypes])


def _mm_stacked(name, a, w4, layer, mode, out_dtypes=(F32,), epilogue=None, extras=(), tm=512, tn=512, tk=512):
    m, kd = a.shape
    _, _, r, c = w4.shape
    n = c if mode == "row" else N_SHARDS * c
    tm = _tile(m, tm)
    if mode == "row":
        tn, tk = _tile(n, tn), _tile(r, tk)
        per = r // tk
        b_spec = pl.BlockSpec((None, None, tk, tn), lambda i, j, k: (k // per, layer, k % per, j))
    else:
        tn, tk = _tile(c, tn), _tile(kd, tk)
        per = c // tn
        b_spec = pl.BlockSpec((None, None, tk, tn), lambda i, j, k: (j // per, layer, k, j % per))
    o_spec = pl.BlockSpec((tm, tn), lambda i, j, k: (i, j))
    return _matmul(name, a, w4, pl.BlockSpec((tm, tk), lambda i, j, k: (i, k)), b_spec, NN,
                   (m // tm, n // tn, kd // tk), (tm, tn),
                   [(jax.ShapeDtypeStruct((m, n), d), o_spec) for d in out_dtypes],
                   [(e, o_spec) for e in extras], epilogue)


def _mm_stacked_nt(name, a, w4, layer, mode, out_dtypes=(F32,), epilogue=None, extras=(), tm=512, tn=512, tk=512):
    m, nd = a.shape
    _, _, r, c = w4.shape
    kout = N_SHARDS * r if mode == "row" else r
    tm = _tile(m, tm)
    if mode == "row":
        tn, tk = _tile(r, tn), _tile(c, tk)
        per = r // tn
        b_spec = pl.BlockSpec((None, None, tn, tk), lambda i, j, k: (j // per, layer, j % per, k))
    else:
        tn, tk = _tile(r, tn), _tile(c, tk)
        per = c // tk
        b_spec = pl.BlockSpec((None, None, tn, tk), lambda i, j, k: (k // per, layer, j, k % per))
    o_spec = pl.BlockSpec((tm, tn), lambda i, j, k: (i, j))
    return _matmul(name, a, w4, pl.BlockSpec((tm, tk), lambda i, j, k: (i, k)), b_spec, NT,
                   (m // tm, kout // tn, nd // tk), (tm, tn),
                   [(jax.ShapeDtypeStruct((m, kout), d), o_spec) for d in out_dtypes],
                   [(e, o_spec) for e in extras], epilogue)


def _mm_tn_stacked(name, a, b, shape4, layer, mode, prev, tm=512, tn=512, tk=512):
    s, m = a.shape
    n = b.shape[1]
    _, _, r, c = shape4
    tk = _tile(s, tk)
    if mode == "row":
        tm, tn = _tile(r, tm), _tile(n, tn)
        per = r // tm
        o_spec = pl.BlockSpec((None, None, tm, tn), lambda i, j, k: (i // per, layer, i % per, j))
    else:
        tm, tn = _tile(m, tm), _tile(c, tn)
        per = c // tn
        o_spec = pl.BlockSpec((None, None, tm, tn), lambda i, j, k: (j // per, layer, i, j % per))
    outs = [(jax.ShapeDtypeStruct(shape4, F32), o_spec), (jax.ShapeDtypeStruct(shape4, BF16), o_spec)]
    aliased = [] if prev is None else [(prev[0], 0), (prev[1], 1)]
    return _matmul(name, a, b, pl.BlockSpec((tk, tm), lambda i, j, k: (k, i)),
                   pl.BlockSpec((tk, tn), lambda i, j, k: (k, j)), TN, (m // tm, n // tn, s // tk), (tm, tn),
                   outs, epilogue=lambda acc: (acc, acc), aliased=aliased)


def _rowwise(name, fn, rows, consts, out_rows, out_accs=(), tr=256):
    nr, nc, no = len(rows), len(consts), len(out_rows)
    n_rows = rows[0].shape[0]
    tr = _tile(n_rows, tr)

    def body(*refs):
        vals = fn(*[r[...] for r in refs[:nr + nc]])
        o_refs, a_refs = refs[nr + nc:nr + nc + no], refs[nr + nc + no:]
        for ref, v in zip(o_refs, vals[:no]):
            ref[...] = v.astype(ref.dtype)
        first = pl.program_id(0) == 0

        @pl.when(first)
        def _():
            for ref, v in zip(a_refs, vals[no:]):
                ref[...] = v

        @pl.when(jnp.logical_not(first))
        def _():
            for ref, v in zip(a_refs, vals[no:]):
                ref[...] += v

    def whole(shape):
        return pl.BlockSpec(shape, lambda i: (0,) * len(shape))

    return pl.pallas_call(
        body, name=name, grid=(n_rows // tr,),
        in_specs=[pl.BlockSpec((tr, a.shape[1]), lambda i: (i, 0)) for a in rows] + [whole(c.shape) for c in consts],
        out_specs=[pl.BlockSpec((tr, f), lambda i: (i, 0)) for f, _ in out_rows] + [whole(s) for s in out_accs],
        out_shape=[jax.ShapeDtypeStruct((n_rows, f), d) for f, d in out_rows]
        + [jax.ShapeDtypeStruct(s, F32) for s in out_accs],
        compiler_params=_params(("arbitrary",)),
    )(*rows, *consts)


def _rms_fwd(x, g):
    return x * lax.rsqrt(jnp.mean(x * x, axis=-1, keepdims=True) + NORM_EPS) * g


def _rms_bwd(dy, x, g):
    rstd = lax.rsqrt(jnp.mean(x * x, axis=-1, keepdims=True) + NORM_EPS)
    n = x * rstd
    dn = dy * g
    dx = rstd * (dn - n * jnp.mean(dn * n, axis=-1, keepdims=True))
    return dx, jnp.sum(dy * n, axis=0, keepdims=True)


def _rope(x, cs, s1, s2):
    return x * cs + pltpu.roll(x, 32, 1) * s1 + pltpu.roll(x, 96, 1) * s2


def _rope_t(dy, cs, s1, s2):
    return dy * cs + pltpu.roll(dy * s1, 96, 1) + pltpu.roll(dy * s2, 32, 1)


def _gelu(z):
    return 0.5 * z * (1.0 + lax.erf(z * (1.0 / math.sqrt(2.0))))


def _gelu_grad(z):
    return 0.5 * (1.0 + lax.erf(z * (1.0 / math.sqrt(2.0)))) + z * jnp.exp(-0.5 * z * z) * (1.0 / math.sqrt(2.0 * math.pi))


def _att_scores(q, kv, kr, scale, masked, transposed):
    k = jnp.concatenate([kv[:, :NOPE], kr], axis=1)
    if transposed:
        s = lax.dot_general(k, q, NT, preferred_element_type=F32) * scale
    else:
        s = lax.dot_general(q, k, NT, preferred_element_type=F32) * scale
    if masked:
        r = lax.broadcasted_iota(jnp.int32, s.shape, 0)
        c = lax.broadcasted_iota(jnp.int32, s.shape, 1)
        s = jnp.where((r <= c) if transposed else (c <= r), s, -jnp.inf)
    return s, k


def _flash_fwd(qb, kvb, krb):
    s_len = qb.shape[0]
    t = ATT_TILE
    nq = s_len // t
    scale = QK_HEAD ** -0.5

    def body(q_ref, kv_ref, kr_ref, o_ref, lse_ref, m_s, l_s, acc_s):
        qi, ki = pl.program_id(1), pl.program_id(2)

        @pl.when(ki == 0)
        def _():
            m_s[...] = jnp.full_like(m_s, -jnp.inf)
            l_s[...] = jnp.zeros_like(l_s)
            acc_s[...] = jnp.zeros_like(acc_s)

        def step(masked):
            kv = kv_ref[...]
            s, _ = _att_scores(q_ref[...], kv, kr_ref[...], scale, masked, False)
            m_prev = m_s[...]
            m_new = jnp.maximum(m_prev, jnp.max(s, axis=1, keepdims=True))
            alpha = jnp.exp(m_prev - m_new)
            p = jnp.exp(s - m_new)
            l_s[...] = alpha * l_s[...] + jnp.sum(p, axis=1, keepdims=True)
            acc_s[...] = alpha * acc_s[...] + jnp.dot(p.astype(BF16), kv[:, NOPE:], preferred_element_type=F32)
            m_s[...] = m_new

        @pl.when(ki < qi)
        def _():
            step(False)

        @pl.when(ki == qi)
        def _():
            step(True)
            o_ref[...] = (acc_s[...] / l_s[...]).astype(o_ref.dtype)
            lse_ref[...] = m_s[...] + jnp.log(l_s[...])

    return pl.pallas_call(
        body, name="flash_fwd", grid=(HEADS, nq, nq),
        in_specs=[pl.BlockSpec((t, HEAD_PAD), lambda h, qi, ki: (qi, h)),
                  pl.BlockSpec((t, HEAD_PAD), lambda h, qi, ki: (jnp.minimum(ki, qi), h)),
                  pl.BlockSpec((t, LANE), lambda h, qi, ki: (jnp.minimum(ki, qi), 0))],
        out_specs=[pl.BlockSpec((t, VHEAD), lambda h, qi, ki: (qi, h)),
                   pl.BlockSpec((None, t, 1), lambda h, qi, ki: (h, qi, 0))],
        out_shape=[jax.ShapeDtypeStruct((s_len, HEADS * VHEAD), BF16),
                   jax.ShapeDtypeStruct((HEADS, s_len, 1), F32)],
        scratch_shapes=[pltpu.VMEM((t, 1), F32), pltpu.VMEM((t, 1), F32), pltpu.VMEM((t, VHEAD), F32)],
        compiler_params=_params(("parallel", "parallel", "arbitrary")),
    )(qb, kvb, krb)


def _flash_bwd_dq(qb, kvb, krb, dob, lse, delta):
    s_len = qb.shape[0]
    t = ATT_TILE
    nq = s_len // t
    scale = QK_HEAD ** -0.5

    def body(q_ref, kv_ref, kr_ref, do_ref, lse_ref, dl_ref, dq_ref, acc_s):
        qi, ki = pl.program_id(1), pl.program_id(2)

        @pl.when(ki == 0)
        def _():
            acc_s[...] = jnp.zeros_like(acc_s)

        def step(masked):
            kv = kv_ref[...]
            s, k = _att_scores(q_ref[...], kv, kr_ref[...], scale, masked, False)
            p = jnp.exp(s - lse_ref[...])
            dp = lax.dot_general(do_ref[...], kv[:, NOPE:], NT, preferred_element_type=F32)
            ds = (p * (dp - dl_ref[...]) * scale).astype(BF16)
            acc_s[...] += jnp.dot(ds, k, preferred_element_type=F32)

        @pl.when(ki < qi)
        def _():
            step(False)

        @pl.when(ki == qi)
        def _():
            step(True)
            dq_ref[...] = acc_s[...]

    col = pl.BlockSpec((None, t, 1), lambda h, qi, ki: (h, qi, 0))
    return pl.pallas_call(
        body, name="flash_bwd_dq", grid=(HEADS, nq, nq),
        in_specs=[pl.BlockSpec((t, HEAD_PAD), lambda h, qi, ki: (qi, h)),
                  pl.BlockSpec((t, HEAD_PAD), lambda h, qi, ki: (jnp.minimum(ki, qi), h)),
                  pl.BlockSpec((t, LANE), lambda h, qi, ki: (jnp.minimum(ki, qi), 0)),
                  pl.BlockSpec((t, VHEAD), lambda h, qi, ki: (qi, h)), col, col],
        out_specs=pl.BlockSpec((t, HEAD_PAD), lambda h, qi, ki: (qi, h)),
        out_shape=jax.ShapeDtypeStruct((s_len, HEADS * HEAD_PAD), F32),
        scratch_shapes=[pltpu.VMEM((t, HEAD_PAD), F32)],
        compiler_params=_params(("parallel", "parallel", "arbitrary")),
    )(qb, kvb, krb, dob, lse, delta)


def _flash_bwd_dkv(qb, kvb, krb, dob, lse_row, delta_row):
    s_len = qb.shape[0]
    t = ATT_TILE
    nq = s_len // t
    scale = QK_HEAD ** -0.5

    def body(q_ref, kv_ref, kr_ref, do_ref, lse_ref, dl_ref, dkv_ref, dkr_ref, dk_s, dv_s, dkr_s):
        h, qi = pl.program_id(1), pl.program_id(2)
        ki = pl.program_id(0)

        @pl.when(qi == 0)
        def _():
            dk_s[...] = jnp.zeros_like(dk_s)
            dv_s[...] = jnp.zeros_like(dv_s)

        @pl.when(jnp.logical_and(qi == 0, h == 0))
        def _():
            dkr_s[...] = jnp.zeros_like(dkr_s)

        def step(masked):
            kv = kv_ref[...]
            q = q_ref[...]
            do = do_ref[...]
            st, _ = _att_scores(q, kv, kr_ref[...], scale, masked, True)
            pt = jnp.exp(st - lse_ref[...])
            dv_s[...] += jnp.dot(pt.astype(BF16), do, preferred_element_type=F32)
            dpt = lax.dot_general(kv[:, NOPE:], do, NT, preferred_element_type=F32)
            dst = (pt * (dpt - dl_ref[...]) * scale).astype(BF16)
            dk_s[...] += jnp.dot(dst, q, preferred_element_type=F32)

        @pl.when(qi > ki)
        def _():
            step(False)

        @pl.when(qi == ki)
        def _():
            step(True)

        @pl.when(qi == nq - 1)
        def _():
            dk = dk_s[...]
            dkv_ref[...] = jnp.concatenate([dk[:, :NOPE], dv_s[...]], axis=1).astype(dkv_ref.dtype)
            dkr_s[...] += dk[:, NOPE:]

        @pl.when(jnp.logical_and(qi == nq - 1, h == HEADS - 1))
        def _():
            dkr_ref[...] = dkr_s[...]

    row = pl.BlockSpec((None, 1, t), lambda ki, h, qi: (h, 0, jnp.maximum(qi, ki)))
    return pl.pallas_call(
        body, name="flash_bwd_dkv", grid=(nq, HEADS, nq),
        in_specs=[pl.BlockSpec((t, HEAD_PAD), lambda ki, h, qi: (jnp.maximum(qi, ki), h)),
                  pl.BlockSpec((t, HEAD_PAD), lambda ki, h, qi: (ki, h)),
                  pl.BlockSpec((t, LANE), lambda ki, h, qi: (ki, 0)),
                  pl.BlockSpec((t, VHEAD), lambda ki, h, qi: (jnp.maximum(qi, ki), h)), row, row],
        out_specs=[pl.BlockSpec((t, HEAD_PAD), lambda ki, h, qi: (ki, h)),
                   pl.BlockSpec((t, LANE), lambda ki, h, qi: (ki, 0))],
        out_shape=[jax.ShapeDtypeStruct((s_len, HEADS * HEAD_PAD), BF16),
                   jax.ShapeDtypeStruct((s_len, LANE), F32)],
        scratch_shapes=[pltpu.VMEM((t, HEAD_PAD), F32), pltpu.VMEM((t, VHEAD), F32), pltpu.VMEM((t, LANE), F32)],
        compiler_params=_params(("parallel", "arbitrary", "arbitrary")),
    )(qb, kvb, krb, dob, lse_row, delta_row)


def _att_delta(do, ob):
    s_len = do.shape[0]
    t = ATT_TILE

    def body(do_ref, o_ref, dob_ref, dl_ref):
        d = do_ref[...]
        dob_ref[...] = d.astype(dob_ref.dtype)
        dl_ref[...] = jnp.sum(d * o_ref[...].astype(F32), axis=1, keepdims=True)

    blk = pl.BlockSpec((t, VHEAD), lambda i, h: (i, h))
    return pl.pallas_call(
        body, name="att_delta", grid=(s_len // t, HEADS), in_specs=[blk, blk],
        out_specs=[blk, pl.BlockSpec((None, t, 1), lambda i, h: (h, i, 0))],
        out_shape=[jax.ShapeDtypeStruct(do.shape, BF16), jax.ShapeDtypeStruct((HEADS, s_len, 1), F32)],
        compiler_params=_params(("parallel", "parallel")),
    )(do, ob)


def _tril(w):
    r = lax.broadcasted_iota(jnp.int32, w.shape, 0)
    c = lax.broadcasted_iota(jnp.int32, w.shape, 1)
    return jnp.where(c <= r, w, 0.0)


def _sgu_mix(w_ref, vln, gd):
    wcs = [_tril(w_ref[g]).astype(BF16) for g in range(SGU_GROUPS)]
    mixed = jnp.concatenate(
        [jnp.dot(wcs[g], vln[:, g * gd:(g + 1) * gd], preferred_element_type=F32) for g in range(SGU_GROUPS)], axis=1)
    return wcs, mixed


def _sgu_fwd(zpre, ln_g, ln_b, w_sp, bias_full):
    s_len, two_w = zpre.shape
    width = two_w // 2
    gd = width // SGU_GROUPS
    t = SGU_CHUNK

    def body(z_ref, g_ref, b_ref, w_ref, bias_ref, uv_ref):
        u = _gelu(z_ref[:, :width])
        v = _gelu(z_ref[:, width:])
        d = v - jnp.mean(v, axis=-1, keepdims=True)
        vhat = d * lax.rsqrt(jnp.mean(d * d, axis=-1, keepdims=True) + LN_EPS)
        vln = (vhat * g_ref[...] + b_ref[...]).astype(BF16)
        _, mixed = _sgu_mix(w_ref, vln, gd)
        uv_ref[...] = (u * (mixed + bias_ref[...])).astype(uv_ref.dtype)

    return pl.pallas_call(
        body, name="sgu_fwd", grid=(s_len // t,),
        in_specs=[pl.BlockSpec((t, two_w), lambda i: (i, 0)), pl.BlockSpec((1, width), lambda i: (0, 0)),
                  pl.BlockSpec((1, width), lambda i: (0, 0)), pl.BlockSpec(w_sp.shape, lambda i: (0, 0, 0)),
                  pl.BlockSpec((t, width), lambda i: (0, 0))],
        out_specs=pl.BlockSpec((t, width), lambda i: (i, 0)),
        out_shape=jax.ShapeDtypeStruct((s_len, width), BF16),
        compiler_params=_params(("parallel",)),
    )(zpre, ln_g, ln_b, w_sp, bias_full)


def _sgu_bwd(zpre, duv, ln_g, ln_b, w_sp, bias_full):
    s_len, two_w = zpre.shape
    width = two_w // 2
    gd = width // SGU_GROUPS
    t = SGU_CHUNK

    def body(z_ref, duv_ref, g_ref, b_ref, w_ref, bias_ref, dz_ref, dg_ref, db_ref, dw_ref, dbias_ref):
        first = pl.program_id(0) == 0

        def accumulate(ref, val):
            @pl.when(first)
            def _():
                ref[...] = val

            @pl.when(jnp.logical_not(first))
            def _():
                ref[...] += val

        zu, zv = z_ref[:, :width], z_ref[:, width:]
        u = _gelu(zu)
        v = _gelu(zv)
        d = v - jnp.mean(v, axis=-1, keepdims=True)
        rstd = lax.rsqrt(jnp.mean(d * d, axis=-1, keepdims=True) + LN_EPS)
        vhat = d * rstd
        vln = (vhat * g_ref[...] + b_ref[...]).astype(BF16)
        wcs, mixed = _sgu_mix(w_ref, vln, gd)
        duv_v = duv_ref[...]
        du = duv_v * (mixed + bias_ref[...])
        dmixed = duv_v * u
        dmb = dmixed.astype(BF16)
        dvln = jnp.concatenate(
            [lax.dot_general(wcs[g], dmb[:, g * gd:(g + 1) * gd], TN, preferred_element_type=F32)
             for g in range(SGU_GROUPS)], axis=1)
        for g in range(SGU_GROUPS):
            dw = lax.dot_general(dmb[:, g * gd:(g + 1) * gd], vln[:, g * gd:(g + 1) * gd], NT, preferred_element_type=F32)
            accumulate(dw_ref.at[g], _tril(dw))
        dvhat = dvln * g_ref[...]
        dv0 = rstd * (dvhat - jnp.mean(dvhat, axis=-1, keepdims=True)
                      - vhat * jnp.mean(dvhat * vhat, axis=-1, keepdims=True))
        dz_ref[:, :width] = (du * _gelu_grad(zu)).astype(dz_ref.dtype)
        dz_ref[:, width:] = (dv0 * _gelu_grad(zv)).astype(dz_ref.dtype)
        accumulate(dg_ref, jnp.sum(dvln * vhat, axis=0, keepdims=True))
        accumulate(db_ref, jnp.sum(dvln, axis=0, keepdims=True))
        accumulate(dbias_ref, dmixed)

    vec = pl.BlockSpec((1, width), lambda i: (0, 0))
    return pl.pallas_call(
        body, name="sgu_bwd", grid=(s_len // t,),
        in_specs=[pl.BlockSpec((t, two_w), lambda i: (i, 0)), pl.BlockSpec((t, width), lambda i: (i, 0)), vec, vec,
                  pl.BlockSpec(w_sp.shape, lambda i: (0, 0, 0)), pl.BlockSpec((t, width), lambda i: (0, 0))],
        out_specs=[pl.BlockSpec((t, two_w), lambda i: (i, 0)), vec, vec,
                   pl.BlockSpec(w_sp.shape, lambda i: (0, 0, 0)), pl.BlockSpec((t, width), lambda i: (0, 0))],
        out_shape=[jax.ShapeDtypeStruct((s_len, two_w), BF16), jax.ShapeDtypeStruct((1, width), F32),
                   jax.ShapeDtypeStruct((1, width), F32), jax.ShapeDtypeStruct(w_sp.shape, F32),
                   jax.ShapeDtypeStruct((t, width), F32)],
        compiler_params=_params(("arbitrary",)),
    )(zpre, duv, ln_g, ln_b, w_sp, bias_full)


def _rope_tables(positions):
    inv_freq = ROPE_THETA ** (-jnp.arange(0, ROPE, 2, dtype=F32) / ROPE)
    ang = positions.astype(F32)[:, None] * inv_freq
    cos, sin = jnp.cos(ang), jnp.sin(ang)
    z32, z64 = jnp.zeros_like(cos), jnp.zeros((cos.shape[0], LANE - ROPE), F32)
    return (jnp.concatenate([cos, cos, z64], axis=1), jnp.concatenate([z32, sin, z64], axis=1),
            jnp.concatenate([-sin, z32, z64], axis=1))


def _q_rope(name, q, tables, transpose):
    rot = _rope_t if transpose else _rope

    def fn(qv, cs, s1, s2):
        parts = []
        for h in range(HEADS):
            parts.append(qv[:, h * HEAD_PAD:h * HEAD_PAD + NOPE])
            parts.append(rot(qv[:, h * HEAD_PAD + NOPE:(h + 1) * HEAD_PAD], cs, s1, s2))
        return (jnp.concatenate(parts, axis=1),)

    return _rowwise(name, fn, [q, *tables], [], [(q.shape[1], BF16)])[0]


def _ffn_fwd(i, x, g, w_up4, w_down4):
    h2 = _rowwise("ffn_norm", lambda xv, gv: (_rms_fwd(xv, gv),), [x], [g], [(x.shape[1], BF16)])[0]

    def sq_relu(acc):
        r = jnp.maximum(acc, 0.0)
        return acc, r * r

    a, r = _mm_stacked("ffn_up", h2, w_up4, i, "col", (F32, BF16), sq_relu, tn=1024)
    x_out = _mm_stacked("ffn_down", r, w_down4, i, "row", (F32,), lambda acc, res: (acc + res,), [x], tn=1024)[0]
    return x_out, (x, h2, a, r)


def _ffn_bwd(i, dx, saved, g, w_up4, w_down4, g_up, g_down):
    x, h2, a, r = saved
    da = _mm_stacked_nt("ffn_down_dx", dx, w_down4, i, "row", (BF16,),
                        lambda acc, av: (acc * (2.0 * jnp.maximum(av, 0.0)),), [a], tn=1024)[0]
    g_down = _mm_tn_stacked("ffn_down_dw", r, dx, w_down4.shape, i, "row", g_down, tn=1024)
    dh2 = _mm_stacked_nt("ffn_up_dx", da, w_up4, i, "col", (F32,), tn=1024)[0]
    g_up = _mm_tn_stacked("ffn_up_dw", h2, da, w_up4.shape, i, "col", g_up, tn=1024)
    dx, dg = _norm_bwd("ffn_norm_bwd", dh2, x, g, dx)
    return dx, dg, g_up, g_down


def _norm_bwd(name, dh, x, g, dres):
    def fn(dhv, xv, rv, gv):
        dxv, dg = _rms_bwd(dhv, xv, gv)
        return dxv + rv, dg

    return _rowwise(name, fn, [dh, x, dres], [g], [(x.shape[1], F32)], [g.shape])


def _mla_fwd(x, g, wdkv, q_norm, kv_norm, wq, wkv, wo, tables):
    d = x.shape[1]
    h = _rowwise("mla_norm", lambda xv, gv: (_rms_fwd(xv, gv),), [x], [g], [(d, BF16)])[0]
    lat = _mm("mla_dkv", h, wdkv, tk=1024)[0]

    def lat_post(lv, cs, s1, s2, qg, kg):
        return (_rms_fwd(lv[:, :Q_RANK], qg), _rms_fwd(lv[:, Q_RANK:Q_RANK + KV_RANK], kg),
                _rope(lv[:, Q_RANK + KV_RANK:], cs, s1, s2))

    cqn, ckvn, krb = _rowwise("mla_lat", lat_post, [lat, *tables], [q_norm, kv_norm],
                              [(Q_RANK, BF16), (KV_RANK, BF16), (LANE, BF16)])
    q = _mm("mla_uq", cqn, wq, tn=1024)[0]
    kvb = _mm("mla_ukv", ckvn, wkv, (BF16,), tn=1024)[0]
    qb = _q_rope("mla_q_rope", q, tables, False)
    ob, lse = _flash_fwd(qb, kvb, krb)
    x_mid = _mm("mla_o", ob, wo, (F32,), lambda acc, res: (acc + res,), [x], tn=1024)[0]
    return x_mid, (x, h, lat, cqn, ckvn, krb, qb, kvb, ob, lse)


def _mla_bwd(dx, saved, g, wdkv, q_norm, kv_norm, wq, wkv, wo, tables):
    x, h, lat, cqn, ckvn, krb, qb, kvb, ob, lse = saved
    s_len = x.shape[0]
    do = _mm("mla_o_dx", dx, wo, nt=True, tn=1024)[0]
    g_wo = _mm_tn("mla_o_dw", ob, dx, tn=1024)[0]
    dob, delta = _att_delta(do, ob)
    dq = _flash_bwd_dq(qb, kvb, krb, dob, lse, delta)
    dkvb, dkr = _flash_bwd_dkv(qb, kvb, krb, dob, lse.reshape(HEADS, 1, s_len), delta.reshape(HEADS, 1, s_len))
    dqb = _q_rope("mla_q_rope_bwd", dq, tables, True)
    dcqn = _mm("mla_uq_dx", dqb, wq, nt=True, tk=1024)[0]
    g_wq = _mm_tn("mla_uq_dw", cqn, dqb, tn=1024)[0]
    dckvn = _mm("mla_ukv_dx", dkvb, wkv, nt=True, tk=1024)[0]
    g_wkv = _mm_tn("mla_ukv_dw", ckvn, dkvb, tn=1024)[0]

    def lat_bwd(dq_v, dkv_v, dkr_v, lv, cs, s1, s2, qg, kg):
        dcq, dqg = _rms_bwd(dq_v, lv[:, :Q_RANK], qg)
        dckv, dkg = _rms_bwd(dkv_v, lv[:, Q_RANK:Q_RANK + KV_RANK], kg)
        return jnp.concatenate([dcq, dckv, _rope_t(dkr_v, cs, s1, s2)], axis=1), dqg, dkg

    dlat, g_qn, g_kvn = _rowwise("mla_lat_bwd", lat_bwd, [dcqn, dckvn, dkr, lat, *tables], [q_norm, kv_norm],
                                 [(LAT_PAD, BF16)], [q_norm.shape, kv_norm.shape])
    dh = _mm("mla_dkv_dx", dlat, wdkv, nt=True, tn=1024)[0]
    g_wdkv = _mm_tn("mla_dkv_dw", h, dlat, tm=1024)[0]
    dx, dg = _norm_bwd("mla_norm_bwd", dh, x, g, dx)
    return dx, dg, g_wdkv, g_qn, g_kvn, g_wq, g_wkv, g_wo


def _sgu_layer_fwd(j, x, g, w_in4, ln_g, ln_b, w_sp, bias_full, w_out4):
    h = _rowwise("sgu_norm", lambda xv, gv: (_rms_fwd(xv, gv),), [x], [g], [(x.shape[1], BF16)])[0]
    zpre = _mm_stacked("sgu_in", h, w_in4, j, "col", tn=1024)[0]
    uv = _sgu_fwd(zpre, ln_g, ln_b, w_sp, bias_full)
    x_mid = _mm_stacked("sgu_out", uv, w_out4, j, "row", (F32,), lambda acc, res: (acc + res,), [x], tn=1024)[0]
    return x_mid, (x, h, zpre, uv)


def _sgu_layer_bwd(j, dx, saved, g, w_in4, ln_g, ln_b, w_sp, bias_full, w_out4, g_in, g_out):
    x, h, zpre, uv = saved
    duv = _mm_stacked_nt("sgu_out_dx", dx, w_out4, j, "row", tn=1024)[0]
    g_out = _mm_tn_stacked("sgu_out_dw", uv, dx, w_out4.shape, j, "row", g_out, tn=1024)
    dz, g_lng, g_lnb, g_wsp, g_bias = _sgu_bwd(zpre, duv, ln_g, ln_b, w_sp, bias_full)
    dh = _mm_stacked_nt("sgu_in_dx", dz, w_in4, j, "col", tn=1024)[0]
    g_in = _mm_tn_stacked("sgu_in_dw", h, dz, w_in4.shape, j, "col", g_in, tn=1024)
    dx, dg = _norm_bwd("sgu_norm_bwd", dh, x, g, dx)
    return dx, dg, g_in, g_out, g_lng, g_lnb, g_wsp, g_bias


def _loss_head(x, target, g):
    d = x.shape[1]

    def fn(xv, tv, gv):
        err = _rms_fwd(xv, gv) - tv
        dxv, dg = _rms_bwd(err * (1.0 / d), xv, gv)
        return dxv, dg, jnp.sum(err * err, axis=0, keepdims=True)

    return _rowwise("loss_head", fn, [x, target], [g], [(d, F32)], [g.shape, g.shape])


def _local_step(x, positions, target, norm_mix, norm_ffn, final_norm, wdkv, q_norm, kv_norm, wq, wkv, wo,
                w_in4, ln_g, ln_b, w_sp, b_sp, w_out4, w_up4, w_down4):
    tables = _rope_tables(positions)
    width = ln_g.shape[-1]
    gd = width // SGU_GROUPS
    bias_full = [jnp.repeat(b_sp[j].T, gd, axis=1) for j in range(DEPTH // 2)]
    saved = []
    for i in range(DEPTH):
        j = i // 2
        if i % 2 == 0:
            x, s_mix = _mla_fwd(x, norm_mix[i:i + 1], wdkv[j], q_norm[j:j + 1], kv_norm[j:j + 1], wq[j], wkv[j], wo[j], tables)
        else:
            x, s_mix = _sgu_layer_fwd(j, x, norm_mix[i:i + 1], w_in4, ln_g[j:j + 1], ln_b[j:j + 1], w_sp[j], bias_full[j], w_out4)
        x, s_ffn = _ffn_fwd(i, x, norm_ffn[i:i + 1], w_up4, w_down4)
        saved.append((s_mix, s_ffn))
    dx, g_final, sq_cols = _loss_head(x, target, final_norm[None, :])
    loss = 0.5 * jnp.sum(sq_cols) / x.shape[1]

    g_mix, g_ffn = [None] * DEPTH, [None] * DEPTH
    g_up = g_down = g_in = g_out = None
    mla_g, sgu_g = [None] * (DEPTH // 2), [None] * (DEPTH // 2)
    for i in reversed(range(DEPTH)):
        j = i // 2
        s_mix, s_ffn = saved[i]
        dx, g_ffn[i], g_up, g_down = _ffn_bwd(i, dx, s_ffn, norm_ffn[i:i + 1], w_up4, w_down4, g_up, g_down)
        if i % 2 == 0:
            dx, g_mix[i], *mla_g[j] = _mla_bwd(dx, s_mix, norm_mix[i:i + 1], wdkv[j], q_norm[j:j + 1], kv_norm[j:j + 1],
                                               wq[j], wkv[j], wo[j], tables)
        else:
            dx, g_mix[i], g_in, g_out, *sgu_g[j] = _sgu_layer_bwd(j, dx, s_mix, norm_mix[i:i + 1], w_in4, ln_g[j:j + 1],
                                                                  ln_b[j:j + 1], w_sp[j], bias_full[j], w_out4, g_in, g_out)
    small = dict(
        norm_mix=jnp.concatenate(g_mix, axis=0), norm_ffn=jnp.concatenate(g_ffn, axis=0), final_norm=g_final[0],
        q_norm=jnp.concatenate([m[1] for m in mla_g], axis=0), kv_norm=jnp.concatenate([m[2] for m in mla_g], axis=0),
        w_sp=jnp.stack([s[2] for s in sgu_g]),
        b_sp=jnp.stack([s[3].reshape(SGU_CHUNK, SGU_GROUPS, gd).sum(axis=-1).T for s in sgu_g]))
    mats = dict(wdkv=[m[0] for m in mla_g], wq=[m[3] for m in mla_g], wkv=[m[4] for m in mla_g], wo=[m[5] for m in mla_g],
                ln_g=[s[0] for s in sgu_g], ln_b=[s[1] for s in sgu_g])
    return loss, dx, small, mats, dict(w_in=g_in, w_out=g_out, w_up=g_up, w_down=g_down)


HBM_SPEC = pl.BlockSpec(memory_space=pltpu.HBM)


def _place():
    x, y, c = lax.axis_index("x"), lax.axis_index("y"), lax.axis_index("c")
    return x, y, c, [(1 - x, y), (x, 1 - y), (1 - x, 1 - y)]


def _remote(src, dst, send_sems, recv_sems, k, to):
    return pltpu.make_async_remote_copy(src_ref=src, dst_ref=dst, send_sem=send_sems.at[k], recv_sem=recv_sems.at[k],
                                        device_id=to, device_id_type=MESH)


def _all_gather(shards):
    n = len(shards)

    def body(*refs):
        ins, outs = refs[:n], refs[n:2 * n]
        send_sems, recv_sems, local_sems = refs[2 * n:]
        x, y, c, chips = _place()
        mine = 2 * x + y
        halves = [pl.ds(c * (s.shape[0] // 2), s.shape[0] // 2) for s in shards]
        other = [pl.ds((1 - c) * (s.shape[0] // 2), s.shape[0] // 2) for s in shards]
        local, sent = [], []
        for t in range(n):
            local.append(pltpu.make_async_copy(ins[t], outs[t].at[mine], local_sems.at[t]))
            local[-1].start()
            for j, chip in enumerate(chips):
                cp = _remote(ins[t].at[halves[t]], outs[t].at[mine, halves[t]], send_sems, recv_sems, 3 * t + j, (*chip, c))
                cp.start()
                sent.append(cp)
        for j, chip in enumerate(chips):
            theirs = 2 * chip[0] + chip[1]
            for t in range(n):
                piece = outs[t].at[theirs, halves[t]]
                _remote(piece, piece, send_sems, recv_sems, 3 * t + j, (x, y, c)).wait_recv()
                cp = _remote(piece, piece, send_sems, recv_sems, 3 * n + 3 * t + j, (x, y, 1 - c))
                cp.start()
                sent.append(cp)
        for j, chip in enumerate(chips):
            theirs = 2 * chip[0] + chip[1]
            for t in range(n):
                piece = outs[t].at[theirs, other[t]]
                _remote(piece, piece, send_sems, recv_sems, 3 * n + 3 * t + j, (x, y, c)).wait_recv()
        for cp in sent:
            cp.wait_send()
        for cp in local:
            cp.wait()

    return pl.pallas_call(
        body, name="gather_weights", in_specs=[HBM_SPEC] * n, out_specs=[HBM_SPEC] * n,
        out_shape=[jax.ShapeDtypeStruct((N_SHARDS, *s.shape), s.dtype) for s in shards],
        scratch_shapes=[pltpu.SemaphoreType.DMA((6 * n,)), pltpu.SemaphoreType.DMA((6 * n,)), pltpu.SemaphoreType.DMA((n,))],
    )(*shards)


def _sibling_swap(stacks):
    n = len(stacks)

    def body(*refs):
        ins, outs, send_sems, recv_sems = refs[:n], refs[n:2 * n], refs[2 * n], refs[2 * n + 1]
        x, y, c, _ = _place()
        sent = []
        for t in range(n):
            lh = stacks[t].shape[1] // 2
            sent.append(_remote(ins[t].at[:, pl.ds((1 - c) * lh, lh)], outs[t], send_sems, recv_sems, t, (x, y, 1 - c)))
            sent[-1].start()
        for cp in sent:
            cp.wait()

    return pl.pallas_call(
        body, name="reduce_sibling_swap", in_specs=[HBM_SPEC] * n, out_specs=[HBM_SPEC] * n,
        out_shape=[jax.ShapeDtypeStruct((s.shape[0], s.shape[1] // 2, *s.shape[2:]), s.dtype) for s in stacks],
        scratch_shapes=[pltpu.SemaphoreType.DMA((n,)), pltpu.SemaphoreType.DMA((n,))],
    )(*stacks)


def _chip_exchange(partials):
    n = len(partials)

    def body(*refs):
        ins, outs, send_sems, recv_sems = refs[:n], refs[n:2 * n], refs[2 * n], refs[2 * n + 1]
        x, y, c, chips = _place()
        sent = []
        for t in range(n):
            for j, chip in enumerate(chips):
                sent.append(_remote(ins[t].at[2 * chip[0] + chip[1]], outs[t].at[j], send_sems, recv_sems, 3 * t + j, (*chip, c)))
                sent[-1].start()
        for cp in sent:
            cp.wait()

    return pl.pallas_call(
        body, name="reduce_chip_exchange", in_specs=[HBM_SPEC] * n, out_specs=[HBM_SPEC] * n,
        out_shape=[jax.ShapeDtypeStruct((3, *s.shape[1:]), s.dtype) for s in partials],
        scratch_shapes=[pltpu.SemaphoreType.DMA((3 * n,)), pltpu.SemaphoreType.DMA((3 * n,))],
    )(*partials)


def _sibling_share(grads):
    n = len(grads)

    def body(*refs):
        ins, outs, send_sems, recv_sems = refs[:n], refs[n:2 * n], refs[2 * n], refs[2 * n + 1]
        x, y, c, _ = _place()
        sent = []
        for t in range(n):
            lh = grads[t].shape[0] // 2
            half = pl.ds(c * lh, lh)
            sent.append(_remote(ins[t].at[half], outs[t].at[half], send_sems, recv_sems, t, (x, y, 1 - c)))
            sent[-1].start()
        for t, cp in enumerate(sent):
            cp.wait_send()
            lh = grads[t].shape[0] // 2
            theirs = outs[t].at[pl.ds((1 - c) * lh, lh)]
            _remote(theirs, theirs, send_sems, recv_sems, t, (x, y, c)).wait_recv()

    return pl.pallas_call(
        body, name="reduce_sibling_share", in_specs=[HBM_SPEC] * n, out_specs=[HBM_SPEC] * n,
        out_shape=[jax.ShapeDtypeStruct(g.shape, g.dtype) for g in grads],
        input_output_aliases={t: t for t in range(n)},
        scratch_shapes=[pltpu.SemaphoreType.DMA((n,)), pltpu.SemaphoreType.DMA((n,))],
    )(*grads)


def _all_reduce_small(part):
    rows = part.shape[0]

    def body(p_ref, out_ref, sib_buf, chip_sums, send_sems, recv_sems):
        x, y, c, chips = _place()
        mine = 2 * x + y
        swap = _remote(p_ref, sib_buf, send_sems, recv_sems, 0, (x, y, 1 - c))
        swap.start()
        swap.wait()
        chip_sums[mine] = p_ref[...] + sib_buf[...]
        sent = [_remote(chip_sums.at[mine], chip_sums.at[mine], send_sems, recv_sems, 1 + j, (*chip, c))
                for j, chip in enumerate(chips)]
        for cp in sent:
            cp.start()
        for j, chip in enumerate(chips):
            sent[j].wait_send()
            theirs = chip_sums.at[2 * chip[0] + chip[1]]
            _remote(theirs, theirs, send_sems, recv_sems, 1 + j, (x, y, c)).wait_recv()
        out_ref[...] = ((chip_sums[0] + chip_sums[1]) + chip_sums[2]) + chip_sums[3]

    vmem = pl.BlockSpec(memory_space=pltpu.VMEM)
    return pl.pallas_call(
        body, name="all_reduce_small", in_specs=[vmem], out_specs=vmem, out_shape=jax.ShapeDtypeStruct(part.shape, F32),
        scratch_shapes=[pltpu.VMEM((rows, LANE), F32), pltpu.VMEM((N_SHARDS, rows, LANE), F32),
                        pltpu.SemaphoreType.DMA((4,)), pltpu.SemaphoreType.DMA((4,))],
        compiler_params=pltpu.CompilerParams(vmem_limit_bytes=VMEM_LIMIT_BYTES),
    )(part)


def _chip_partial(g4, from_sibling, half):
    _, lh, r, c = from_sibling.shape
    tr = _tile(r, 512)

    def body(half_ref, g_ref, s_ref, o_ref):
        o_ref[...] = (g_ref[...] + s_ref[...].astype(F32)).astype(o_ref.dtype)

    blk = (None, None, tr, c)
    return pl.pallas_call(
        body, name="reduce_chip_partial",
        grid_spec=pltpu.PrefetchScalarGridSpec(
            num_scalar_prefetch=1, grid=(N_SHARDS, lh, r // tr),
            in_specs=[pl.BlockSpec(blk, lambda s, l, i, hr: (s, hr[0] * lh + l, i, 0)),
                      pl.BlockSpec(blk, lambda s, l, i, hr: (s, l, i, 0))],
            out_specs=pl.BlockSpec(blk, lambda s, l, i, hr: (s, l, i, 0))),
        out_shape=jax.ShapeDtypeStruct(from_sibling.shape, BF16),
        compiler_params=_params(("parallel", "parallel", "parallel")),
    )(half, g4, from_sibling)


def _reduce_own(g4, from_sibling, from_chips, sel):
    _, lh, r, c = from_sibling.shape

    tr = _tile(r, 512)

    def body(sel_ref, g_ref, s_ref, c0_ref, c1_ref, c2_ref, o_ref):
        acc = g_ref[...] + s_ref[...].astype(F32)
        for ref in (c0_ref, c1_ref, c2_ref):
            acc = acc + ref[...].astype(F32)
        o_ref[...] = acc

    blk = (None, None, tr, c)
    chip_specs = [pl.BlockSpec(blk, functools.partial(lambda l, i, sr, j: (j, l, i, 0), j=j)) for j in range(3)]
    return pl.pallas_call(
        body, name="reduce_own_shard",
        grid_spec=pltpu.PrefetchScalarGridSpec(
            num_scalar_prefetch=1, grid=(lh, r // tr),
            in_specs=[pl.BlockSpec(blk, lambda l, i, sr: (sr[0], sr[1] * lh + l, i, 0)),
                      pl.BlockSpec(blk, lambda l, i, sr: (sr[0], l, i, 0))] + chip_specs,
            out_specs=pl.BlockSpec((None, tr, c), lambda l, i, sr: (sr[1] * lh + l, i, 0))),
        out_shape=jax.ShapeDtypeStruct((2 * lh, r, c), F32),
        compiler_params=_params(("parallel", "parallel")),
    )(sel, g4, from_sibling, from_chips, from_chips, from_chips)


def _adamw(w, g, m, v):
    lyr, r, c = w.shape
    tr = _tile(r, 512)

    def body(w_ref, g_ref, m_ref, v_ref, d_ref, nm_ref, nv_ref):
        gv = g_ref[...]
        nm = ADAM_B1 * m_ref[...] + (1.0 - ADAM_B1) * gv
        nv = ADAM_B2 * v_ref[...] + (1.0 - ADAM_B2) * (gv * gv)
        m_hat = nm / (1.0 - ADAM_B1 ** ADAM_STEP)
        v_hat = nv / (1.0 - ADAM_B2 ** ADAM_STEP)
        d_ref[...] = -ADAM_LR * (m_hat / (jnp.sqrt(v_hat) + ADAM_EPS) + ADAM_WD * w_ref[...])
        nm_ref[...] = nm
        nv_ref[...] = nv

    blk = pl.BlockSpec((None, tr, c), lambda l, i: (l, i, 0))
    return pl.pallas_call(
        body, name="adamw", grid=(lyr, r // tr), in_specs=[blk] * 4, out_specs=[blk] * 3,
        out_shape=[jax.ShapeDtypeStruct(w.shape, F32)] * 3,
        compiler_params=_params(("parallel", "parallel")),
    )(w, g, m, v)


SHARDED = ("mla_w_dkv", "mla_w_uq", "mla_w_ukv", "mla_w_o", "sgu_w_in", "sgu_ln_g", "sgu_ln_b", "sgu_w_out",
           "ffn_w_up", "ffn_w_down")
REPLICATED = ("norm_mix", "norm_ffn", "final_norm", "mla_q_norm", "mla_kv_norm", "sgu_w_spatial", "sgu_b_spatial")
WEIGHTS = ("norm_mix", "norm_ffn", "final_norm", "mla_w_dkv", "mla_q_norm", "mla_kv_norm", "mla_w_uq", "mla_w_ukv",
           "mla_w_o", "sgu_w_in", "sgu_ln_g", "sgu_ln_b", "sgu_w_spatial", "sgu_b_spatial", "sgu_w_out", "ffn_w_up",
           "ffn_w_down")


def _as3d(name, a):
    return a.reshape(a.shape[0], -1, LANE) if name in ("sgu_ln_g", "sgu_ln_b") else a


def _pack(parts):
    flat = jnp.concatenate([p.reshape(-1) for p in parts])
    rows = -(-flat.shape[0] // (8 * LANE)) * 8
    return jnp.pad(flat, (0, rows * LANE - flat.shape[0])).reshape(rows, LANE)


def _unpack(packed, like):
    flat, out, at = packed.reshape(-1), [], 0
    for p in like:
        out.append(flat[at:at + p.size].reshape(p.shape))
        at += p.size
    return out


def kernel(x, positions, norm_mix, norm_ffn, final_norm, mla_w_dkv, mla_q_norm, mla_kv_norm, mla_w_uq, mla_w_ukv, mla_w_o, sgu_w_in, sgu_ln_g, sgu_ln_b, sgu_w_spatial, sgu_b_spatial, sgu_w_out, ffn_w_up, ffn_w_down, loss_target, m_norm_mix, m_norm_ffn, m_final_norm, m_mla_w_dkv, m_mla_q_norm, m_mla_kv_norm, m_mla_w_uq, m_mla_w_ukv, m_mla_w_o, m_sgu_w_in, m_sgu_ln_g, m_sgu_ln_b, m_sgu_w_spatial, m_sgu_b_spatial, m_sgu_w_out, m_ffn_w_up, m_ffn_w_down, v_norm_mix, v_norm_ffn, v_final_norm, v_mla_w_dkv, v_mla_q_norm, v_mla_kv_norm, v_mla_w_uq, v_mla_w_ukv, v_mla_w_o, v_sgu_w_in, v_sgu_ln_g, v_sgu_ln_b, v_sgu_w_spatial, v_sgu_b_spatial, v_sgu_w_out, v_ffn_w_up, v_ffn_w_down):
    given = dict(locals())
    w = {n: given[n] for n in WEIGHTS}
    mom = {n: given["m_" + n] for n in WEIGHTS}
    var = {n: given["v_" + n] for n in WEIGHTS}
    n_mla = mla_w_dkv.shape[0]

    shards = [_as3d(n, w[n]) if n in ("sgu_ln_g", "sgu_ln_b") else w[n].astype(BF16) for n in SHARDED]
    full = dict(zip(SHARDED, _all_gather(shards)))
    by_rows = lambda a: a.transpose(1, 0, 2, 3).reshape(a.shape[1], N_SHARDS * a.shape[2], a.shape[3])
    by_cols = lambda a: a.transpose(1, 2, 0, 3).reshape(a.shape[1], a.shape[2], N_SHARDS * a.shape[3])
    wdkv = jnp.pad(by_rows(full["mla_w_dkv"]), ((0, 0), (0, 0), (0, LAT_PAD - mla_w_dkv.shape[-1])))
    wq = by_cols(full["mla_w_uq"]).reshape(n_mla, Q_RANK, HEADS, QK_HEAD)
    wq = jnp.pad(wq, ((0, 0), (0, 0), (0, 0), (0, HEAD_PAD - QK_HEAD))).reshape(n_mla, Q_RANK, HEADS * HEAD_PAD)
    wkv = by_cols(full["mla_w_ukv"])
    wo = by_rows(full["mla_w_o"])
    ln_g = by_rows(full["sgu_ln_g"]).reshape(sgu_ln_g.shape[0], -1)
    ln_b = by_rows(full["sgu_ln_b"]).reshape(sgu_ln_b.shape[0], -1)

    loss, dx, small, mats, big = _local_step(
        x[0], positions[0], loss_target[0], norm_mix, norm_ffn, final_norm, wdkv, mla_q_norm, mla_kv_norm, wq, wkv, wo,
        full["sgu_w_in"], ln_g, ln_b, sgu_w_spatial, sgu_b_spatial, full["sgu_w_out"], full["ffn_w_up"], full["ffn_w_down"])
    loss = lax.psum(loss, ("x", "y", "c"))

    rows_of = lambda g: g.reshape(g.shape[0], N_SHARDS, g.shape[1] // N_SHARDS, g.shape[2]).transpose(1, 0, 2, 3)
    cols_of = lambda g: g.reshape(g.shape[0], g.shape[1], N_SHARDS, g.shape[2] // N_SHARDS).transpose(2, 0, 1, 3)
    g_wq = jnp.stack(mats["wq"]).reshape(n_mla, Q_RANK, HEADS, HEAD_PAD)[..., :QK_HEAD].reshape(n_mla, Q_RANK, HEADS * QK_HEAD)
    stacks = {
        "mla_w_dkv": rows_of(jnp.stack(mats["wdkv"])[:, :, :mla_w_dkv.shape[-1]]),
        "mla_w_uq": cols_of(g_wq),
        "mla_w_ukv": cols_of(jnp.stack(mats["wkv"])),
        "mla_w_o": rows_of(jnp.stack(mats["wo"])),
        "sgu_ln_g": rows_of(jnp.concatenate(mats["ln_g"]).reshape(sgu_ln_g.shape[0], -1, LANE)),
        "sgu_ln_b": rows_of(jnp.concatenate(mats["ln_b"]).reshape(sgu_ln_b.shape[0], -1, LANE)),
    }
    stacks = {n: (g, g.astype(BF16)) for n, g in stacks.items()}
    stacks.update({"sgu_w_in": big["w_in"], "sgu_w_out": big["w_out"], "ffn_w_up": big["w_up"], "ffn_w_down": big["w_down"]})

    x_i, y_i, c_i = lax.axis_index("x"), lax.axis_index("y"), lax.axis_index("c")
    half = jnp.reshape(c_i, (1,)).astype(jnp.int32)
    sel = jnp.stack([2 * x_i + y_i, c_i]).astype(jnp.int32)
    from_sibling = _sibling_swap([stacks[n][1] for n in SHARDED])
    partials = [_chip_partial(stacks[n][0], fs, half) for n, fs in zip(SHARDED, from_sibling)]
    from_chips = _chip_exchange(partials)
    halves = [_reduce_own(stacks[n][0], fs, fc, sel) for n, fs, fc in zip(SHARDED, from_sibling, from_chips)]
    grads = dict(zip(SHARDED, _sibling_share(halves)))

    small_g = [small["norm_mix"], small["norm_ffn"], small["final_norm"], small["q_norm"], small["kv_norm"],
               small["w_sp"], small["b_sp"]]
    like = [w[n] for n in REPLICATED]
    g_small = _all_reduce_small(_pack(small_g))
    packed = [_pack([d[n] for n in REPLICATED])[None] for d in (w, mom, var)]
    upd_small = _adamw(packed[0], g_small[None], packed[1], packed[2])
    grads.update(zip(REPLICATED, _unpack(g_small, like)))
    delta, new_m, new_v = ({n: a for n, a in zip(REPLICATED, _unpack(u[0], like))} for u in upd_small)

    for n in SHARDED:
        d, nm, nv = _adamw(_as3d(n, w[n]), grads[n], _as3d(n, mom[n]), _as3d(n, var[n]))
        grads[n], delta[n], new_m[n], new_v[n] = (a.reshape(w[n].shape) for a in (grads[n], d, nm, nv))

    return (loss, dx[None], *[grads[n] for n in WEIGHTS], *[delta[n] for n in WEIGHTS],
            *[new_m[n] for n in WEIGHTS], *[new_v[n] for n in WEIGHTS])
```

```python
import functools
import math

import jax
import jax.numpy as jnp
from jax import lax
from jax.experimental import pallas as pl
from jax.experimental.pallas import tpu as pltpu

F32 = jnp.float32
BF16 = jnp.bfloat16
MESH = pl.DeviceIdType.MESH

DEPTH = 4
HEADS = 8
NOPE = 128
ROPE = 64
VHEAD = 128
QK_HEAD = NOPE + ROPE
Q_RANK = 256
KV_RANK = 128
HEAD_PAD = 256
LAT_PAD = 512
ROPE_THETA = 10000.0
SGU_CHUNK = 128
SGU_GROUPS = 8
NORM_EPS = 1e-6
LN_EPS = 1e-5
ADAM_LR, ADAM_B1, ADAM_B2, ADAM_EPS, ADAM_WD, ADAM_STEP = 0.001, 0.9, 0.999, 1e-08, 0.01, 10

N_SHARDS = 4
LANE = 128
VMEM_LIMIT_BYTES = 56 * 1024 * 1024
ATT_TILE = 512
MM_TILE = 1024
ATT_SCALE = QK_HEAD ** -0.5
LOG2_SCALE = ATT_SCALE * math.log2(math.e)

NN = (((1,), (0,)), ((), ()))
NT = (((1,), (1,)), ((), ()))
TN = (((0,), (0,)), ((), ()))


def _params(sem):
    return pltpu.CompilerParams(dimension_semantics=sem, vmem_limit_bytes=VMEM_LIMIT_BYTES)


def _tile(n, pref):
    t = min(n, pref)
    while n % t:
        t //= 2
    return t


def _matmul(name, a, b, a_spec, b_spec, dims, grid, tile, outs, extras=(), epilogue=None, aliased=()):
    nk, ne, no = grid[2], len(extras), len(outs)

    def body(a_ref, b_ref, *rest):
        e_refs, o_refs = rest[:ne], rest[ne + len(aliased):ne + len(aliased) + no]
        part = lax.dot_general(a_ref[...].astype(BF16), b_ref[...].astype(BF16), dims, preferred_element_type=F32)

        def finish(acc):
            vals = (acc,) if epilogue is None else epilogue(acc, *[e[...] for e in e_refs])
            for o_ref, v in zip(o_refs, vals):
                o_ref[...] = v.astype(o_ref.dtype)

        if nk == 1:
            finish(part)
            return
        acc_ref, k = rest[-1], pl.program_id(2)

        @pl.when(k == 0)
        def _():
            acc_ref[...] = part

        @pl.when(jnp.logical_and(k > 0, k < nk - 1))
        def _():
            acc_ref[...] += part

        @pl.when(k == nk - 1)
        def _():
            finish(acc_ref[...] + part)

    n_in = 2 + ne
    return pl.pallas_call(
        body, name=name, grid=grid,
        in_specs=[a_spec, b_spec] + [s for _, s in extras] + [pl.BlockSpec(memory_space=pl.ANY)] * len(aliased),
        out_specs=[s for _, s in outs], out_shape=[s for s, _ in outs],
        scratch_shapes=[pltpu.VMEM(tile, F32)] if nk > 1 else [],
        input_output_aliases={n_in + i: o for i, (_, o) in enumerate(aliased)},
        compiler_params=_params(("parallel", "parallel", "arbitrary")),
    )(a, b, *[e for e, _ in extras], *[arr for arr, _ in aliased])


def _mm(name, a, b, out_dtypes=(F32,), epilogue=None, extras=(), tm=MM_TILE, tn=MM_TILE, tk=MM_TILE, nt=False):
    m, kd = a.shape
    n = b.shape[0] if nt else b.shape[1]
    tm, tn, tk = _tile(m, tm), _tile(n, tn), _tile(kd, tk)
    o_spec = pl.BlockSpec((tm, tn), lambda i, j, k: (i, j))
    b_spec = pl.BlockSpec((tn, tk), lambda i, j, k: (j, k)) if nt else pl.BlockSpec((tk, tn), lambda i, j, k: (k, j))
    return _matmul(name, a, b, pl.BlockSpec((tm, tk), lambda i, j, k: (i, k)), b_spec, NT if nt else NN,
                   (m // tm, n // tn, kd // tk), (tm, tn),
                   [(jax.ShapeDtypeStruct((m, n), d), o_spec) for d in out_dtypes],
                   [(e, o_spec) for e in extras], epilogue)


def _mm_tn(name, a, b, out_dtypes=(F32,), tm=MM_TILE, tn=MM_TILE, tk=MM_TILE):
    s, m = a.shape
    n = b.shape[1]
    tm, tn, tk = _tile(m, tm), _tile(n, tn), _tile(s, tk)
    o_spec = pl.BlockSpec((tm, tn), lambda i, j, k: (i, j))
    return _matmul(name, a, b, pl.BlockSpec((tk, tm), lambda i, j, k: (k, i)),
                   pl.BlockSpec((tk, tn), lambda i, j, k: (k, j)), TN, (m // tm, n // tn, s // tk), (tm, tn),
                   [(jax.ShapeDtypeStruct((m, n), d), o_spec) for d in out_dtypes])


def _mm_stacked(name, a, w4, layer, mode, out_dtypes=(F32,), epilogue=None, extras=(), tm=MM_TILE, tn=MM_TILE, tk=MM_TILE):
    m, kd = a.shape
    _, _, r, c = w4.shape
    n = c if mode == "row" else N_SHARDS * c
    tm = _tile(m, tm)
    if mode == "row":
        tn, tk = _tile(n, tn), _tile(r, tk)
        per = r // tk
        b_spec = pl.BlockSpec((None, None, tk, tn), lambda i, j, k: (k // per, layer, k % per, j))
    else:
        tn, tk = _tile(c, tn), _tile(kd, tk)
        per = c // tn
        b_spec = pl.BlockSpec((None, None, tk, tn), lambda i, j, k: (j // per, layer, k, j % per))
    o_spec = pl.BlockSpec((tm, tn), lambda i, j, k: (i, j))
    return _matmul(name, a, w4, pl.BlockSpec((tm, tk), lambda i, j, k: (i, k)), b_spec, NN,
                   (m // tm, n // tn, kd // tk), (tm, tn),
                   [(jax.ShapeDtypeStruct((m, n), d), o_spec) for d in out_dtypes],
                   [(e, o_spec) for e in extras], epilogue)


def _mm_stacked_nt(name, a, w4, layer, mode, out_dtypes=(F32,), epilogue=None, extras=(), tm=MM_TILE, tn=MM_TILE, tk=MM_TILE):
    m, nd = a.shape
    _, _, r, c = w4.shape
    kout = N_SHARDS * r if mode == "row" else r
    tm = _tile(m, tm)
    if mode == "row":
        tn, tk = _tile(r, tn), _tile(c, tk)
        per = r // tn
        b_spec = pl.BlockSpec((None, None, tn, tk), lambda i, j, k: (j // per, layer, j % per, k))
    else:
        tn, tk = _tile(r, tn), _tile(c, tk)
        per = c // tk
        b_spec = pl.BlockSpec((None, None, tn, tk), lambda i, j, k: (k // per, layer, j, k % per))
    o_spec = pl.BlockSpec((tm, tn), lambda i, j, k: (i, j))
    return _matmul(name, a, w4, pl.BlockSpec((tm, tk), lambda i, j, k: (i, k)), b_spec, NT,
                   (m // tm, kout // tn, nd // tk), (tm, tn),
                   [(jax.ShapeDtypeStruct((m, kout), d), o_spec) for d in out_dtypes],
                   [(e, o_spec) for e in extras], epilogue)


def _mm_tn_stacked(name, a, b, shape4, layer, mode, prev, tm=MM_TILE, tn=MM_TILE, tk=MM_TILE):
    s, m = a.shape
    n = b.shape[1]
    _, _, r, c = shape4
    tk = _tile(s, tk)
    if mode == "row":
        tm, tn = _tile(r, tm), _tile(n, tn)
        per = r // tm
        o_spec = pl.BlockSpec((None, None, tm, tn), lambda i, j, k: (i // per, layer, i % per, j))
    else:
        tm, tn = _tile(m, tm), _tile(c, tn)
        per = c // tn
        o_spec = pl.BlockSpec((None, None, tm, tn), lambda i, j, k: (j // per, layer, i, j % per))
    outs = [(jax.ShapeDtypeStruct(shape4, F32), o_spec), (jax.ShapeDtypeStruct(shape4, BF16), o_spec)]
    aliased = [] if prev is None else [(prev[0], 0), (prev[1], 1)]
    return _matmul(name, a, b, pl.BlockSpec((tk, tm), lambda i, j, k: (k, i)),
                   pl.BlockSpec((tk, tn), lambda i, j, k: (k, j)), TN, (m // tm, n // tn, s // tk), (tm, tn),
                   outs, epilogue=lambda acc: (acc, acc), aliased=aliased)


def _rowwise(name, fn, rows, consts, out_rows, out_accs=(), tr=256):
    nr, nc, no = len(rows), len(consts), len(out_rows)
    n_rows = rows[0].shape[0]
    tr = _tile(n_rows, tr)

    def body(*refs):
        vals = fn(*[r[...] for r in refs[:nr + nc]])
        o_refs, a_refs = refs[nr + nc:nr + nc + no], refs[nr + nc + no:]
        for ref, v in zip(o_refs, vals[:no]):
            ref[...] = v.astype(ref.dtype)
        first = pl.program_id(0) == 0

        @pl.when(first)
        def _():
            for ref, v in zip(a_refs, vals[no:]):
                ref[...] = v

        @pl.when(jnp.logical_not(first))
        def _():
            for ref, v in zip(a_refs, vals[no:]):
                ref[...] += v

    def whole(shape):
        return pl.BlockSpec(shape, lambda i: (0,) * len(shape))

    return pl.pallas_call(
        body, name=name, grid=(n_rows // tr,),
        in_specs=[pl.BlockSpec((tr, a.shape[1]), lambda i: (i, 0)) for a in rows] + [whole(c.shape) for c in consts],
        out_specs=[pl.BlockSpec((tr, f), lambda i: (i, 0)) for f, _ in out_rows] + [whole(s) for s in out_accs],
        out_shape=[jax.ShapeDtypeStruct((n_rows, f), d) for f, d in out_rows]
        + [jax.ShapeDtypeStruct(s, F32) for s in out_accs],
        compiler_params=_params(("arbitrary",)),
    )(*rows, *consts)


def _rms_fwd(x, g):
    return x * lax.rsqrt(jnp.mean(x * x, axis=-1, keepdims=True) + NORM_EPS) * g


def _rms_bwd(dy, x, g):
    rstd = lax.rsqrt(jnp.mean(x * x, axis=-1, keepdims=True) + NORM_EPS)
    n = x * rstd
    dn = dy * g
    dx = rstd * (dn - n * jnp.mean(dn * n, axis=-1, keepdims=True))
    return dx, jnp.sum(dy * n, axis=0, keepdims=True)


def _rope(x, cs, s1, s2):
    return x * cs + pltpu.roll(x, 32, 1) * s1 + pltpu.roll(x, 96, 1) * s2


def _rope_t(dy, cs, s1, s2):
    return dy * cs + pltpu.roll(dy * s1, 96, 1) + pltpu.roll(dy * s2, 32, 1)


def _gelu(z):
    return 0.5 * z * (1.0 + lax.erf(z * (1.0 / math.sqrt(2.0))))


def _gelu_grad(z):
    return 0.5 * (1.0 + lax.erf(z * (1.0 / math.sqrt(2.0)))) + z * jnp.exp(-0.5 * z * z) * (1.0 / math.sqrt(2.0 * math.pi))


def _att_scores(q, kv, kr, scale, masked, transposed):
    k = jnp.concatenate([kv[:, :NOPE], kr], axis=1)
    if transposed:
        s = lax.dot_general(k, q, NT, preferred_element_type=F32) * scale
    else:
        s = lax.dot_general(q, k, NT, preferred_element_type=F32) * scale
    if masked:
        r = lax.broadcasted_iota(jnp.int32, s.shape, 0)
        c = lax.broadcasted_iota(jnp.int32, s.shape, 1)
        s = jnp.where((r <= c) if transposed else (c <= r), s, -jnp.inf)
    return s, k


def _flash_fwd(qb, kvb, krb):
    s_len = qb.shape[0]
    t = ATT_TILE

    def body(q_ref, kv_ref, kr_ref, o_ref, lse_ref, m_s, l_s, acc_s):
        qi = pl.program_id(1)
        m_s[...] = jnp.full_like(m_s, -jnp.inf)
        l_s[...] = jnp.zeros_like(l_s)
        acc_s[...] = jnp.zeros_like(acc_s)
        q = q_ref[...]

        def step(ki, masked):
            rows = pl.ds(pl.multiple_of(ki * t, t), t)
            kv = kv_ref[rows, :]
            s, _ = _att_scores(q, kv, kr_ref[rows, :], LOG2_SCALE, masked, False)
            m_prev = m_s[...]
            m_new = jnp.maximum(m_prev, jnp.max(s, axis=1, keepdims=True))
            alpha = jnp.exp2(m_prev - m_new)
            p = jnp.exp2(s - jnp.tile(m_new, (1, t // LANE)))
            l_s[...] = alpha * l_s[...] + jnp.sum(p, axis=1, keepdims=True)
            acc_s[...] = alpha * acc_s[...] + jnp.dot(p.astype(BF16), kv[:, NOPE:], preferred_element_type=F32)
            m_s[...] = m_new

        def unmasked(ki, carry):
            step(ki, False)
            return carry

        lax.fori_loop(0, qi, unmasked, 0)
        step(qi, True)
        o_ref[...] = (acc_s[...] / l_s[...]).astype(o_ref.dtype)
        lse_ref[...] = (m_s[...] + jnp.log2(l_s[...]))[:, :1]

    return pl.pallas_call(
        body, name="flash_fwd", grid=(HEADS, s_len // t),
        in_specs=[pl.BlockSpec((t, HEAD_PAD), lambda h, qi: (qi, h)),
                  pl.BlockSpec((s_len, HEAD_PAD), lambda h, qi: (0, h)),
                  pl.BlockSpec((s_len, LANE), lambda h, qi: (0, 0))],
        out_specs=[pl.BlockSpec((t, VHEAD), lambda h, qi: (qi, h)),
                   pl.BlockSpec((None, t, 1), lambda h, qi: (h, qi, 0))],
        out_shape=[jax.ShapeDtypeStruct((s_len, HEADS * VHEAD), BF16),
                   jax.ShapeDtypeStruct((HEADS, s_len, 1), F32)],
        scratch_shapes=[pltpu.VMEM((t, LANE), F32), pltpu.VMEM((t, LANE), F32), pltpu.VMEM((t, VHEAD), F32)],
        compiler_params=_params(("parallel", "arbitrary")),
    )(qb, kvb, krb)


def _flash_bwd_dq(qb, kvb, krb, dob, lse, delta):
    s_len = qb.shape[0]
    t = ATT_TILE
    nq = s_len // t
    scale = QK_HEAD ** -0.5

    def body(q_ref, kv_ref, kr_ref, do_ref, lse_ref, dl_ref, dq_ref, acc_s):
        qi = pl.program_id(1)
        acc_s[...] = jnp.zeros_like(acc_s)
        q, do = q_ref[...], do_ref[...]
        lse = jnp.broadcast_to(lse_ref[...], (t, LANE))
        dl = jnp.broadcast_to(dl_ref[...], (t, LANE))

        def step(ki, masked):
            rows = pl.ds(pl.multiple_of(ki * t, t), t)
            kv = kv_ref[rows, :]
            s, k = _att_scores(q, kv, kr_ref[rows, :], LOG2_SCALE, masked, False)
            p = jnp.exp2(s - jnp.tile(lse, (1, t // LANE)))
            dp = lax.dot_general(do, kv[:, NOPE:], NT, preferred_element_type=F32)
            ds = (p * (dp - jnp.tile(dl, (1, t // LANE))) * scale).astype(BF16)
            acc_s[...] += jnp.dot(ds, k, preferred_element_type=F32)

        def unmasked(ki, carry):
            step(ki, False)
            return carry

        lax.fori_loop(0, qi, unmasked, 0)
        step(qi, True)
        dq_ref[...] = acc_s[...]

    col = pl.BlockSpec((None, t, 1), lambda h, qi: (h, qi, 0))
    return pl.pallas_call(
        body, name="flash_bwd_dq", grid=(HEADS, nq),
        in_specs=[pl.BlockSpec((t, HEAD_PAD), lambda h, qi: (qi, h)),
                  pl.BlockSpec((s_len, HEAD_PAD), lambda h, qi: (0, h)),
                  pl.BlockSpec((s_len, LANE), lambda h, qi: (0, 0)),
                  pl.BlockSpec((t, VHEAD), lambda h, qi: (qi, h)), col, col],
        out_specs=pl.BlockSpec((t, HEAD_PAD), lambda h, qi: (qi, h)),
        out_shape=jax.ShapeDtypeStruct((s_len, HEADS * HEAD_PAD), F32),
        scratch_shapes=[pltpu.VMEM((t, HEAD_PAD), F32)],
        compiler_params=_params(("parallel", "arbitrary")),
    )(qb, kvb, krb, dob, lse, delta)


def _flash_bwd_dkv(qb, kvb, krb, dob, lse_row, delta_row):
    s_len = qb.shape[0]
    t = ATT_TILE
    nq = s_len // t
    scale = QK_HEAD ** -0.5

    def body(q_ref, kv_ref, kr_ref, do_ref, lse_ref, dl_ref, dkv_ref, dkr_ref, dk_s, dv_s):
        ki = pl.program_id(1)
        dk_s[...] = jnp.zeros_like(dk_s)
        dv_s[...] = jnp.zeros_like(dv_s)
        kv, kr = kv_ref[...], kr_ref[...]

        def step(qi, masked):
            rows = pl.ds(pl.multiple_of(qi * t, t), t)
            q, do = q_ref[rows, :], do_ref[rows, :]
            st, _ = _att_scores(q, kv, kr, LOG2_SCALE, masked, True)
            pt = jnp.exp2(st - lse_ref[:, rows])
            dv_s[...] += jnp.dot(pt.astype(BF16), do, preferred_element_type=F32)
            dpt = lax.dot_general(kv[:, NOPE:], do, NT, preferred_element_type=F32)
            dst = (pt * (dpt - dl_ref[:, rows]) * scale).astype(BF16)
            dk_s[...] += jnp.dot(dst, q, preferred_element_type=F32)

        def unmasked(qi, carry):
            step(qi, False)
            return carry

        step(ki, True)
        lax.fori_loop(ki + 1, nq, unmasked, 0)
        dk = dk_s[...]
        dkv_ref[...] = jnp.concatenate([dk[:, :NOPE], dv_s[...]], axis=1).astype(dkv_ref.dtype)
        dkr_ref[...] = dk[:, NOPE:]

    row = pl.BlockSpec((None, 1, s_len), lambda h, ki: (h, 0, 0))
    return pl.pallas_call(
        body, name="flash_bwd_dkv", grid=(HEADS, nq),
        in_specs=[pl.BlockSpec((s_len, HEAD_PAD), lambda h, ki: (0, h)),
                  pl.BlockSpec((t, HEAD_PAD), lambda h, ki: (ki, h)),
                  pl.BlockSpec((t, LANE), lambda h, ki: (ki, 0)),
                  pl.BlockSpec((s_len, VHEAD), lambda h, ki: (0, h)), row, row],
        out_specs=[pl.BlockSpec((t, HEAD_PAD), lambda h, ki: (ki, h)),
                   pl.BlockSpec((t, LANE), lambda h, ki: (ki, h))],
        out_shape=[jax.ShapeDtypeStruct((s_len, HEADS * HEAD_PAD), BF16),
                   jax.ShapeDtypeStruct((s_len, HEADS * LANE), F32)],
        scratch_shapes=[pltpu.VMEM((t, HEAD_PAD), F32), pltpu.VMEM((t, VHEAD), F32)],
        compiler_params=_params(("parallel", "parallel")),
    )(qb, kvb, krb, dob, lse_row, delta_row)


def _att_delta(do, ob):
    s_len = do.shape[0]
    t = ATT_TILE

    def body(do_ref, o_ref, dob_ref, dl_ref):
        d = do_ref[...]
        dob_ref[...] = d.astype(dob_ref.dtype)
        dl_ref[...] = jnp.sum(d * o_ref[...].astype(F32), axis=1, keepdims=True)

    blk = pl.BlockSpec((t, VHEAD), lambda i, h: (i, h))
    return pl.pallas_call(
        body, name="att_delta", grid=(s_len // t, HEADS), in_specs=[blk, blk],
        out_specs=[blk, pl.BlockSpec((None, t, 1), lambda i, h: (h, i, 0))],
        out_shape=[jax.ShapeDtypeStruct(do.shape, BF16), jax.ShapeDtypeStruct((HEADS, s_len, 1), F32)],
        compiler_params=_params(("parallel", "parallel")),
    )(do, ob)


def _tril(w):
    r = lax.broadcasted_iota(jnp.int32, w.shape, 0)
    c = lax.broadcasted_iota(jnp.int32, w.shape, 1)
    return jnp.where(c <= r, w, 0.0)


def _sgu_mix(w_ref, vln, gd):
    wcs = [_tril(w_ref[g]).astype(BF16) for g in range(SGU_GROUPS)]
    mixed = jnp.concatenate(
        [jnp.dot(wcs[g], vln[:, g * gd:(g + 1) * gd], preferred_element_type=F32) for g in range(SGU_GROUPS)], axis=1)
    return wcs, mixed


def _sgu_fwd(zpre, ln_g, ln_b, w_sp, bias_full):
    s_len, two_w = zpre.shape
    width = two_w // 2
    gd = width // SGU_GROUPS
    t = SGU_CHUNK

    def body(z_ref, g_ref, b_ref, w_ref, bias_ref, uv_ref):
        u = _gelu(z_ref[:, :width])
        v = _gelu(z_ref[:, width:])
        d = v - jnp.mean(v, axis=-1, keepdims=True)
        vhat = d * lax.rsqrt(jnp.mean(d * d, axis=-1, keepdims=True) + LN_EPS)
        vln = (vhat * g_ref[...] + b_ref[...]).astype(BF16)
        _, mixed = _sgu_mix(w_ref, vln, gd)
        uv_ref[...] = (u * (mixed + bias_ref[...])).astype(uv_ref.dtype)

    return pl.pallas_call(
        body, name="sgu_fwd", grid=(s_len // t,),
        in_specs=[pl.BlockSpec((t, two_w), lambda i: (i, 0)), pl.BlockSpec((1, width), lambda i: (0, 0)),
                  pl.BlockSpec((1, width), lambda i: (0, 0)), pl.BlockSpec(w_sp.shape, lambda i: (0, 0, 0)),
                  pl.BlockSpec((t, width), lambda i: (0, 0))],
        out_specs=pl.BlockSpec((t, width), lambda i: (i, 0)),
        out_shape=jax.ShapeDtypeStruct((s_len, width), BF16),
        compiler_params=_params(("parallel",)),
    )(zpre, ln_g, ln_b, w_sp, bias_full)


def _sgu_bwd(zpre, duv, ln_g, ln_b, w_sp, bias_full):
    s_len, two_w = zpre.shape
    width = two_w // 2
    gd = width // SGU_GROUPS
    t = SGU_CHUNK

    def body(z_ref, duv_ref, g_ref, b_ref, w_ref, bias_ref, dz_ref, dg_ref, db_ref, dw_ref, dbias_ref):
        first = pl.program_id(0) == 0

        def accumulate(ref, val):
            @pl.when(first)
            def _():
                ref[...] = val

            @pl.when(jnp.logical_not(first))
            def _():
                ref[...] += val

        zu, zv = z_ref[:, :width], z_ref[:, width:]
        u = _gelu(zu)
        v = _gelu(zv)
        d = v - jnp.mean(v, axis=-1, keepdims=True)
        rstd = lax.rsqrt(jnp.mean(d * d, axis=-1, keepdims=True) + LN_EPS)
        vhat = d * rstd
        vln = (vhat * g_ref[...] + b_ref[...]).astype(BF16)
        wcs, mixed = _sgu_mix(w_ref, vln, gd)
        duv_v = duv_ref[...]
        du = duv_v * (mixed + bias_ref[...])
        dmixed = duv_v * u
        dmb = dmixed.astype(BF16)
        dvln = jnp.concatenate(
            [lax.dot_general(wcs[g], dmb[:, g * gd:(g + 1) * gd], TN, preferred_element_type=F32)
             for g in range(SGU_GROUPS)], axis=1)
        for g in range(SGU_GROUPS):
            dw = lax.dot_general(dmb[:, g * gd:(g + 1) * gd], vln[:, g * gd:(g + 1) * gd], NT, preferred_element_type=F32)
            accumulate(dw_ref.at[g], _tril(dw))
        dvhat = dvln * g_ref[...]
        dv0 = rstd * (dvhat - jnp.mean(dvhat, axis=-1, keepdims=True)
                      - vhat * jnp.mean(dvhat * vhat, axis=-1, keepdims=True))
        dz_ref[:, :width] = (du * _gelu_grad(zu)).astype(dz_ref.dtype)
        dz_ref[:, width:] = (dv0 * _gelu_grad(zv)).astype(dz_ref.dtype)
        accumulate(dg_ref, jnp.sum(dvln * vhat, axis=0, keepdims=True))
        accumulate(db_ref, jnp.sum(dvln, axis=0, keepdims=True))
        accumulate(dbias_ref, dmixed)

    vec = pl.BlockSpec((1, width), lambda i: (0, 0))
    return pl.pallas_call(
        body, name="sgu_bwd", grid=(s_len // t,),
        in_specs=[pl.BlockSpec((t, two_w), lambda i: (i, 0)), pl.BlockSpec((t, width), lambda i: (i, 0)), vec, vec,
                  pl.BlockSpec(w_sp.shape, lambda i: (0, 0, 0)), pl.BlockSpec((t, width), lambda i: (0, 0))],
        out_specs=[pl.BlockSpec((t, two_w), lambda i: (i, 0)), vec, vec,
                   pl.BlockSpec(w_sp.shape, lambda i: (0, 0, 0)), pl.BlockSpec((t, width), lambda i: (0, 0))],
        out_shape=[jax.ShapeDtypeStruct((s_len, two_w), BF16), jax.ShapeDtypeStruct((1, width), F32),
                   jax.ShapeDtypeStruct((1, width), F32), jax.ShapeDtypeStruct(w_sp.shape, F32),
                   jax.ShapeDtypeStruct((t, width), F32)],
        compiler_params=_params(("arbitrary",)),
    )(zpre, duv, ln_g, ln_b, w_sp, bias_full)


def _rope_tables(positions):
    inv_freq = ROPE_THETA ** (-jnp.arange(0, ROPE, 2, dtype=F32) / ROPE)
    ang = positions.astype(F32)[:, None] * inv_freq
    cos, sin = jnp.cos(ang), jnp.sin(ang)
    z32, z64 = jnp.zeros_like(cos), jnp.zeros((cos.shape[0], LANE - ROPE), F32)
    return (jnp.concatenate([cos, cos, z64], axis=1), jnp.concatenate([z32, sin, z64], axis=1),
            jnp.concatenate([-sin, z32, z64], axis=1))


def _q_rope(name, q, tables, transpose):
    rot = _rope_t if transpose else _rope

    def fn(qv, cs, s1, s2):
        parts = []
        for h in range(HEADS):
            parts.append(qv[:, h * HEAD_PAD:h * HEAD_PAD + NOPE])
            parts.append(rot(qv[:, h * HEAD_PAD + NOPE:(h + 1) * HEAD_PAD], cs, s1, s2))
        return (jnp.concatenate(parts, axis=1),)

    return _rowwise(name, fn, [q, *tables], [], [(q.shape[1], BF16)])[0]


def _ffn_fwd(i, x, g, w_up4, w_down4):
    h2 = _rowwise("ffn_norm", lambda xv, gv: (_rms_fwd(xv, gv),), [x], [g], [(x.shape[1], BF16)])[0]

    def sq_relu(acc):
        r = jnp.maximum(acc, 0.0)
        return (r * r,)

    r = _mm_stacked("ffn_up", h2, w_up4, i, "col", (BF16,), sq_relu)[0]
    x_out = _mm_stacked("ffn_down", r, w_down4, i, "row", (F32,), lambda acc, res: (acc + res,), [x])[0]
    return x_out, (x, h2, r)


def _ffn_bwd(i, dx, dxb, saved, g, w_up4, w_down4, g_up, g_down):
    x, h2, r = saved
    da = _mm_stacked_nt("ffn_down_dx", dxb, w_down4, i, "row", (BF16,),
                        lambda acc, rv: (acc * (2.0 * jnp.sqrt(rv.astype(F32))),), [r])[0]
    g_down = _mm_tn_stacked("ffn_down_dw", r, dxb, w_down4.shape, i, "row", g_down)
    dh2 = _mm_stacked_nt("ffn_up_dx", da, w_up4, i, "col", (F32,))[0]
    g_up = _mm_tn_stacked("ffn_up_dw", h2, da, w_up4.shape, i, "col", g_up)
    dx, dxb, dg = _norm_bwd("ffn_norm_bwd", dh2, x, g, dx)
    return dx, dxb, dg, g_up, g_down


def _norm_bwd(name, dh, x, g, dres):
    def fn(dhv, xv, rv, gv):
        dxv, dg = _rms_bwd(dhv, xv, gv)
        return dxv + rv, dxv + rv, dg

    return _rowwise(name, fn, [dh, x, dres], [g], [(x.shape[1], F32), (x.shape[1], BF16)], [g.shape])


def _mla_fwd(x, g, wdkv, q_norm, kv_norm, wq, wkv, wo, tables):
    d = x.shape[1]
    h = _rowwise("mla_norm", lambda xv, gv: (_rms_fwd(xv, gv),), [x], [g], [(d, BF16)])[0]
    lat = _mm("mla_dkv", h, wdkv)[0]

    def lat_post(lv, cs, s1, s2, qg, kg):
        return (_rms_fwd(lv[:, :Q_RANK], qg), _rms_fwd(lv[:, Q_RANK:Q_RANK + KV_RANK], kg),
                _rope(lv[:, Q_RANK + KV_RANK:], cs, s1, s2))

    cqn, ckvn, krb = _rowwise("mla_lat", lat_post, [lat, *tables], [q_norm, kv_norm],
                              [(Q_RANK, BF16), (KV_RANK, BF16), (LANE, BF16)])
    q = _mm("mla_uq", cqn, wq)[0]
    kvb = _mm("mla_ukv", ckvn, wkv, (BF16,))[0]
    qb = _q_rope("mla_q_rope", q, tables, False)
    ob, lse = _flash_fwd(qb, kvb, krb)
    x_mid = _mm("mla_o", ob, wo, (F32,), lambda acc, res: (acc + res,), [x])[0]
    return x_mid, (x, h, lat, cqn, ckvn, krb, qb, kvb, ob, lse)


def _mla_bwd(dx, dxb, saved, g, wdkv, q_norm, kv_norm, wq, wkv, wo, tables):
    x, h, lat, cqn, ckvn, krb, qb, kvb, ob, lse = saved
    s_len = x.shape[0]
    do = _mm("mla_o_dx", dxb, wo, nt=True)[0]
    g_wo = _mm_tn("mla_o_dw", ob, dxb)[0]
    dob, delta = _att_delta(do, ob)
    dq = _flash_bwd_dq(qb, kvb, krb, dob, lse, delta)
    dkvb, dkr = _flash_bwd_dkv(qb, kvb, krb, dob, lse.reshape(HEADS, 1, s_len), delta.reshape(HEADS, 1, s_len))
    dqb = _q_rope("mla_q_rope_bwd", dq, tables, True)
    dcqn = _mm("mla_uq_dx", dqb, wq, nt=True)[0]
    g_wq = _mm_tn("mla_uq_dw", cqn, dqb)[0]
    dckvn = _mm("mla_ukv_dx", dkvb, wkv, nt=True)[0]
    g_wkv = _mm_tn("mla_ukv_dw", ckvn, dkvb)[0]

    def lat_bwd(dq_v, dkv_v, dkr_v, lv, cs, s1, s2, qg, kg):
        dcq, dqg = _rms_bwd(dq_v, lv[:, :Q_RANK], qg)
        dckv, dkg = _rms_bwd(dkv_v, lv[:, Q_RANK:Q_RANK + KV_RANK], kg)
        dkr_sum = dkr_v[:, :LANE]
        for hd in range(1, HEADS):
            dkr_sum = dkr_sum + dkr_v[:, hd * LANE:(hd + 1) * LANE]
        return jnp.concatenate([dcq, dckv, _rope_t(dkr_sum, cs, s1, s2)], axis=1), dqg, dkg

    dlat, g_qn, g_kvn = _rowwise("mla_lat_bwd", lat_bwd, [dcqn, dckvn, dkr, lat, *tables], [q_norm, kv_norm],
                                 [(LAT_PAD, BF16)], [q_norm.shape, kv_norm.shape])
    dh = _mm("mla_dkv_dx", dlat, wdkv, nt=True)[0]
    g_wdkv = _mm_tn("mla_dkv_dw", h, dlat)[0]
    dx, dxb, dg = _norm_bwd("mla_norm_bwd", dh, x, g, dx)
    return dx, dxb, dg, g_wdkv, g_qn, g_kvn, g_wq, g_wkv, g_wo


def _sgu_layer_fwd(j, x, g, w_in4, ln_g, ln_b, w_sp, bias_full, w_out4):
    h = _rowwise("sgu_norm", lambda xv, gv: (_rms_fwd(xv, gv),), [x], [g], [(x.shape[1], BF16)])[0]
    zpre = _mm_stacked("sgu_in", h, w_in4, j, "col")[0]
    uv = _sgu_fwd(zpre, ln_g, ln_b, w_sp, bias_full)
    x_mid = _mm_stacked("sgu_out", uv, w_out4, j, "row", (F32,), lambda acc, res: (acc + res,), [x])[0]
    return x_mid, (x, h, zpre, uv)


def _sgu_layer_bwd(j, dx, dxb, saved, g, w_in4, ln_g, ln_b, w_sp, bias_full, w_out4, g_in, g_out):
    x, h, zpre, uv = saved
    duv = _mm_stacked_nt("sgu_out_dx", dxb, w_out4, j, "row")[0]
    g_out = _mm_tn_stacked("sgu_out_dw", uv, dxb, w_out4.shape, j, "row", g_out)
    dz, g_lng, g_lnb, g_wsp, g_bias = _sgu_bwd(zpre, duv, ln_g, ln_b, w_sp, bias_full)
    dh = _mm_stacked_nt("sgu_in_dx", dz, w_in4, j, "col")[0]
    g_in = _mm_tn_stacked("sgu_in_dw", h, dz, w_in4.shape, j, "col", g_in)
    dx, dxb, dg = _norm_bwd("sgu_norm_bwd", dh, x, g, dx)
    return dx, dxb, dg, g_in, g_out, g_lng, g_lnb, g_wsp, g_bias


def _loss_head(x, target, g):
    d = x.shape[1]

    def fn(xv, tv, gv):
        err = _rms_fwd(xv, gv) - tv
        dxv, dg = _rms_bwd(err * (1.0 / d), xv, gv)
        return dxv, dxv, dg, jnp.sum(err * err, axis=0, keepdims=True)

    return _rowwise("loss_head", fn, [x, target], [g], [(d, F32), (d, BF16)], [g.shape, g.shape])


def _local_step(x, positions, target, norm_mix, norm_ffn, final_norm, wdkv, q_norm, kv_norm, wq, wkv, wo,
                w_in4, ln_g, ln_b, w_sp, b_sp, w_out4, w_up4, w_down4):
    tables = _rope_tables(positions)
    width = ln_g.shape[-1]
    gd = width // SGU_GROUPS
    bias_full = [jnp.repeat(b_sp[j].T, gd, axis=1) for j in range(DEPTH // 2)]
    saved = []
    for i in range(DEPTH):
        j = i // 2
        if i % 2 == 0:
            x, s_mix = _mla_fwd(x, norm_mix[i:i + 1], wdkv[j], q_norm[j:j + 1], kv_norm[j:j + 1], wq[j], wkv[j], wo[j], tables)
        else:
            x, s_mix = _sgu_layer_fwd(j, x, norm_mix[i:i + 1], w_in4, ln_g[j:j + 1], ln_b[j:j + 1], w_sp[j], bias_full[j], w_out4)
        x, s_ffn = _ffn_fwd(i, x, norm_ffn[i:i + 1], w_up4, w_down4)
        saved.append((s_mix, s_ffn))
    dx, dxb, g_final, sq_cols = _loss_head(x, target, final_norm[None, :])
    loss = 0.5 * jnp.sum(sq_cols) / x.shape[1]

    g_mix, g_ffn = [None] * DEPTH, [None] * DEPTH
    g_up = g_down = g_in = g_out = None
    mla_g, sgu_g = [None] * (DEPTH // 2), [None] * (DEPTH // 2)
    for i in reversed(range(DEPTH)):
        j = i // 2
        s_mix, s_ffn = saved[i]
        dx, dxb, g_ffn[i], g_up, g_down = _ffn_bwd(i, dx, dxb, s_ffn, norm_ffn[i:i + 1], w_up4, w_down4, g_up, g_down)
        if i % 2 == 0:
            dx, dxb, g_mix[i], *mla_g[j] = _mla_bwd(dx, dxb, s_mix, norm_mix[i:i + 1], wdkv[j], q_norm[j:j + 1],
                                                    kv_norm[j:j + 1], wq[j], wkv[j], wo[j], tables)
        else:
            dx, dxb, g_mix[i], g_in, g_out, *sgu_g[j] = _sgu_layer_bwd(
                j, dx, dxb, s_mix, norm_mix[i:i + 1], w_in4, ln_g[j:j + 1], ln_b[j:j + 1], w_sp[j], bias_full[j], w_out4,
                g_in, g_out)
    small = dict(
        norm_mix=jnp.concatenate(g_mix, axis=0), norm_ffn=jnp.concatenate(g_ffn, axis=0), final_norm=g_final[0],
        q_norm=jnp.concatenate([m[1] for m in mla_g], axis=0), kv_norm=jnp.concatenate([m[2] for m in mla_g], axis=0),
        w_sp=jnp.stack([s[2] for s in sgu_g]),
        b_sp=jnp.stack([s[3].reshape(SGU_CHUNK, SGU_GROUPS, gd).sum(axis=-1).T for s in sgu_g]))
    mats = dict(wdkv=[m[0] for m in mla_g], wq=[m[3] for m in mla_g], wkv=[m[4] for m in mla_g], wo=[m[5] for m in mla_g],
                ln_g=[s[0] for s in sgu_g], ln_b=[s[1] for s in sgu_g])
    return loss, dx, small, mats, dict(w_in=g_in, w_out=g_out, w_up=g_up, w_down=g_down)


HBM_SPEC = pl.BlockSpec(memory_space=pltpu.HBM)


def _place():
    x, y, c = lax.axis_index("x"), lax.axis_index("y"), lax.axis_index("c")
    return x, y, c, [(1 - x, y), (x, 1 - y), (1 - x, 1 - y)]


def _remote(src, dst, send_sems, recv_sems, k, to):
    return pltpu.make_async_remote_copy(src_ref=src, dst_ref=dst, send_sem=send_sems.at[k], recv_sem=recv_sems.at[k],
                                        device_id=to, device_id_type=MESH)


def _all_gather(shards):
    n = len(shards)

    def body(*refs):
        ins, outs = refs[:n], refs[n:2 * n]
        send_sems, recv_sems, local_sems = refs[2 * n:]
        x, y, c, chips = _place()
        mine = 2 * x + y
        halves = [pl.ds(c * (s.shape[0] // 2), s.shape[0] // 2) for s in shards]
        other = [pl.ds((1 - c) * (s.shape[0] // 2), s.shape[0] // 2) for s in shards]
        local, sent = [], []
        for t in range(n):
            local.append(pltpu.make_async_copy(ins[t], outs[t].at[mine], local_sems.at[t]))
            local[-1].start()
            for j, chip in enumerate(chips):
                cp = _remote(ins[t].at[halves[t]], outs[t].at[mine, halves[t]], send_sems, recv_sems, 3 * t + j, (*chip, c))
                cp.start()
                sent.append(cp)
        for j, chip in enumerate(chips):
            theirs = 2 * chip[0] + chip[1]
            for t in range(n):
                piece = outs[t].at[theirs, halves[t]]
                _remote(piece, piece, send_sems, recv_sems, 3 * t + j, (x, y, c)).wait_recv()
                cp = _remote(piece, piece, send_sems, recv_sems, 3 * n + 3 * t + j, (x, y, 1 - c))
                cp.start()
                sent.append(cp)
        for j, chip in enumerate(chips):
            theirs = 2 * chip[0] + chip[1]
            for t in range(n):
                piece = outs[t].at[theirs, other[t]]
                _remote(piece, piece, send_sems, recv_sems, 3 * n + 3 * t + j, (x, y, c)).wait_recv()
        for cp in sent:
            cp.wait_send()
        for cp in local:
            cp.wait()

    return pl.pallas_call(
        body, name="gather_weights", in_specs=[HBM_SPEC] * n, out_specs=[HBM_SPEC] * n,
        out_shape=[jax.ShapeDtypeStruct((N_SHARDS, *s.shape), s.dtype) for s in shards],
        scratch_shapes=[pltpu.SemaphoreType.DMA((6 * n,)), pltpu.SemaphoreType.DMA((6 * n,)), pltpu.SemaphoreType.DMA((n,))],
    )(*shards)


def _sibling_swap(stacks):
    n = len(stacks)

    def body(*refs):
        ins, outs, send_sems, recv_sems = refs[:n], refs[n:2 * n], refs[2 * n], refs[2 * n + 1]
        x, y, c, _ = _place()
        sent = []
        for t in range(n):
            lh = stacks[t].shape[1] // 2
            sent.append(_remote(ins[t].at[:, pl.ds((1 - c) * lh, lh)], outs[t], send_sems, recv_sems, t, (x, y, 1 - c)))
            sent[-1].start()
        for cp in sent:
            cp.wait()

    return pl.pallas_call(
        body, name="reduce_sibling_swap", in_specs=[HBM_SPEC] * n, out_specs=[HBM_SPEC] * n,
        out_shape=[jax.ShapeDtypeStruct((s.shape[0], s.shape[1] // 2, *s.shape[2:]), s.dtype) for s in stacks],
        scratch_shapes=[pltpu.SemaphoreType.DMA((n,)), pltpu.SemaphoreType.DMA((n,))],
    )(*stacks)


def _chip_exchange(partials):
    n = len(partials)

    def body(*refs):
        ins, outs, send_sems, recv_sems = refs[:n], refs[n:2 * n], refs[2 * n], refs[2 * n + 1]
        x, y, c, chips = _place()
        sent = []
        for t in range(n):
            for j, chip in enumerate(chips):
                sent.append(_remote(ins[t].at[2 * chip[0] + chip[1]], outs[t].at[j], send_sems, recv_sems, 3 * t + j, (*chip, c)))
                sent[-1].start()
        for cp in sent:
            cp.wait()

    return pl.pallas_call(
        body, name="reduce_chip_exchange", in_specs=[HBM_SPEC] * n, out_specs=[HBM_SPEC] * n,
        out_shape=[jax.ShapeDtypeStruct((3, *s.shape[1:]), s.dtype) for s in partials],
        scratch_shapes=[pltpu.SemaphoreType.DMA((3 * n,)), pltpu.SemaphoreType.DMA((3 * n,))],
    )(*partials)


def _sibling_share(grads):
    n = len(grads)

    def body(*refs):
        ins, outs, send_sems, recv_sems = refs[:n], refs[n:2 * n], refs[2 * n], refs[2 * n + 1]
        x, y, c, _ = _place()
        sent = []
        for t in range(n):
            lh = grads[t].shape[0] // 2
            half = pl.ds(c * lh, lh)
            sent.append(_remote(ins[t].at[half], outs[t].at[half], send_sems, recv_sems, t, (x, y, 1 - c)))
            sent[-1].start()
        for t, cp in enumerate(sent):
            cp.wait_send()
            lh = grads[t].shape[0] // 2
            theirs = outs[t].at[pl.ds((1 - c) * lh, lh)]
            _remote(theirs, theirs, send_sems, recv_sems, t, (x, y, c)).wait_recv()

    return pl.pallas_call(
        body, name="reduce_sibling_share", in_specs=[HBM_SPEC] * n, out_specs=[HBM_SPEC] * n,
        out_shape=[jax.ShapeDtypeStruct(g.shape, g.dtype) for g in grads],
        input_output_aliases={t: t for t in range(n)},
        scratch_shapes=[pltpu.SemaphoreType.DMA((n,)), pltpu.SemaphoreType.DMA((n,))],
    )(*grads)


def _all_reduce_small(part):
    rows = part.shape[0]

    def body(p_ref, out_ref, sib_buf, chip_sums, send_sems, recv_sems):
        x, y, c, chips = _place()
        mine = 2 * x + y
        swap = _remote(p_ref, sib_buf, send_sems, recv_sems, 0, (x, y, 1 - c))
        swap.start()
        swap.wait()
        chip_sums[mine] = p_ref[...] + sib_buf[...]
        sent = [_remote(chip_sums.at[mine], chip_sums.at[mine], send_sems, recv_sems, 1 + j, (*chip, c))
                for j, chip in enumerate(chips)]
        for cp in sent:
            cp.start()
        for j, chip in enumerate(chips):
            sent[j].wait_send()
            theirs = chip_sums.at[2 * chip[0] + chip[1]]
            _remote(theirs, theirs, send_sems, recv_sems, 1 + j, (x, y, c)).wait_recv()
        out_ref[...] = ((chip_sums[0] + chip_sums[1]) + chip_sums[2]) + chip_sums[3]

    vmem = pl.BlockSpec(memory_space=pltpu.VMEM)
    return pl.pallas_call(
        body, name="all_reduce_small", in_specs=[vmem], out_specs=vmem, out_shape=jax.ShapeDtypeStruct(part.shape, F32),
        scratch_shapes=[pltpu.VMEM((rows, LANE), F32), pltpu.VMEM((N_SHARDS, rows, LANE), F32),
                        pltpu.SemaphoreType.DMA((4,)), pltpu.SemaphoreType.DMA((4,))],
        compiler_params=pltpu.CompilerParams(vmem_limit_bytes=VMEM_LIMIT_BYTES),
    )(part)


def _chip_partial(g4, from_sibling, half):
    _, lh, r, c = from_sibling.shape
    tr = _tile(r, 512)

    def body(half_ref, g_ref, s_ref, o_ref):
        o_ref[...] = (g_ref[...] + s_ref[...].astype(F32)).astype(o_ref.dtype)

    blk = (None, None, tr, c)
    return pl.pallas_call(
        body, name="reduce_chip_partial",
        grid_spec=pltpu.PrefetchScalarGridSpec(
            num_scalar_prefetch=1, grid=(N_SHARDS, lh, r // tr),
            in_specs=[pl.BlockSpec(blk, lambda s, l, i, hr: (s, hr[0] * lh + l, i, 0)),
                      pl.BlockSpec(blk, lambda s, l, i, hr: (s, l, i, 0))],
            out_specs=pl.BlockSpec(blk, lambda s, l, i, hr: (s, l, i, 0))),
        out_shape=jax.ShapeDtypeStruct(from_sibling.shape, BF16),
        compiler_params=_params(("parallel", "parallel", "parallel")),
    )(half, g4, from_sibling)


def _reduce_own(g4, from_sibling, from_chips, sel):
    _, lh, r, c = from_sibling.shape

    tr = _tile(r, 512)

    def body(sel_ref, g_ref, s_ref, c0_ref, c1_ref, c2_ref, o_ref):
        acc = g_ref[...] + s_ref[...].astype(F32)
        for ref in (c0_ref, c1_ref, c2_ref):
            acc = acc + ref[...].astype(F32)
        o_ref[...] = acc

    blk = (None, None, tr, c)
    chip_specs = [pl.BlockSpec(blk, functools.partial(lambda l, i, sr, j: (j, l, i, 0), j=j)) for j in range(3)]
    return pl.pallas_call(
        body, name="reduce_own_shard",
        grid_spec=pltpu.PrefetchScalarGridSpec(
            num_scalar_prefetch=1, grid=(lh, r // tr),
            in_specs=[pl.BlockSpec(blk, lambda l, i, sr: (sr[0], sr[1] * lh + l, i, 0)),
                      pl.BlockSpec(blk, lambda l, i, sr: (sr[0], l, i, 0))] + chip_specs,
            out_specs=pl.BlockSpec((None, tr, c), lambda l, i, sr: (sr[1] * lh + l, i, 0))),
        out_shape=jax.ShapeDtypeStruct((2 * lh, r, c), F32),
        compiler_params=_params(("parallel", "parallel")),
    )(sel, g4, from_sibling, from_chips, from_chips, from_chips)


def _adamw(w, g, m, v):
    lyr, r, c = w.shape
    tr = _tile(r, 512)

    def body(w_ref, g_ref, m_ref, v_ref, d_ref, nm_ref, nv_ref):
        gv = g_ref[...]
        nm = ADAM_B1 * m_ref[...] + (1.0 - ADAM_B1) * gv
        nv = ADAM_B2 * v_ref[...] + (1.0 - ADAM_B2) * (gv * gv)
        m_hat = nm / (1.0 - ADAM_B1 ** ADAM_STEP)
        v_hat = nv / (1.0 - ADAM_B2 ** ADAM_STEP)
        d_ref[...] = -ADAM_LR * (m_hat / (jnp.sqrt(v_hat) + ADAM_EPS) + ADAM_WD * w_ref[...])
        nm_ref[...] = nm
        nv_ref[...] = nv

    blk = pl.BlockSpec((None, tr, c), lambda l, i: (l, i, 0))
    return pl.pallas_call(
        body, name="adamw", grid=(lyr, r // tr), in_specs=[blk] * 4, out_specs=[blk] * 3,
        out_shape=[jax.ShapeDtypeStruct(w.shape, F32)] * 3,
        compiler_params=_params(("parallel", "parallel")),
    )(w, g, m, v)


SHARDED = ("mla_w_dkv", "mla_w_uq", "mla_w_ukv", "mla_w_o", "sgu_w_in", "sgu_ln_g", "sgu_ln_b", "sgu_w_out",
           "ffn_w_up", "ffn_w_down")
REPLICATED = ("norm_mix", "norm_ffn", "final_norm", "mla_q_norm", "mla_kv_norm", "sgu_w_spatial", "sgu_b_spatial")
WEIGHTS = ("norm_mix", "norm_ffn", "final_norm", "mla_w_dkv", "mla_q_norm", "mla_kv_norm", "mla_w_uq", "mla_w_ukv",
           "mla_w_o", "sgu_w_in", "sgu_ln_g", "sgu_ln_b", "sgu_w_spatial", "sgu_b_spatial", "sgu_w_out", "ffn_w_up",
           "ffn_w_down")


def _as3d(name, a):
    return a.reshape(a.shape[0], -1, LANE) if name in ("sgu_ln_g", "sgu_ln_b") else a


def _pack(parts):
    flat = jnp.concatenate([p.reshape(-1) for p in parts])
    rows = -(-flat.shape[0] // (8 * LANE)) * 8
    return jnp.pad(flat, (0, rows * LANE - flat.shape[0])).reshape(rows, LANE)


def _unpack(packed, like):
    flat, out, at = packed.reshape(-1), [], 0
    for p in like:
        out.append(flat[at:at + p.size].reshape(p.shape))
        at += p.size
    return out


def kernel(x, positions, norm_mix, norm_ffn, final_norm, mla_w_dkv, mla_q_norm, mla_kv_norm, mla_w_uq, mla_w_ukv, mla_w_o, sgu_w_in, sgu_ln_g, sgu_ln_b, sgu_w_spatial, sgu_b_spatial, sgu_w_out, ffn_w_up, ffn_w_down, loss_target, m_norm_mix, m_norm_ffn, m_final_norm, m_mla_w_dkv, m_mla_q_norm, m_mla_kv_norm, m_mla_w_uq, m_mla_w_ukv, m_mla_w_o, m_sgu_w_in, m_sgu_ln_g, m_sgu_ln_b, m_sgu_w_spatial, m_sgu_b_spatial, m_sgu_w_out, m_ffn_w_up, m_ffn_w_down, v_norm_mix, v_norm_ffn, v_final_norm, v_mla_w_dkv, v_mla_q_norm, v_mla_kv_norm, v_mla_w_uq, v_mla_w_ukv, v_mla_w_o, v_sgu_w_in, v_sgu_ln_g, v_sgu_ln_b, v_sgu_w_spatial, v_sgu_b_spatial, v_sgu_w_out, v_ffn_w_up, v_ffn_w_down):
    given = dict(locals())
    w = {n: given[n] for n in WEIGHTS}
    mom = {n: given["m_" + n] for n in WEIGHTS}
    var = {n: given["v_" + n] for n in WEIGHTS}
    n_mla = mla_w_dkv.shape[0]

    shards = [_as3d(n, w[n]) if n in ("sgu_ln_g", "sgu_ln_b") else w[n].astype(BF16) for n in SHARDED]
    full = dict(zip(SHARDED, _all_gather(shards)))
    by_rows = lambda a: a.transpose(1, 0, 2, 3).reshape(a.shape[1], N_SHARDS * a.shape[2], a.shape[3])
    by_cols = lambda a: a.transpose(1, 2, 0, 3).reshape(a.shape[1], a.shape[2], N_SHARDS * a.shape[3])
    wdkv = jnp.pad(by_rows(full["mla_w_dkv"]), ((0, 0), (0, 0), (0, LAT_PAD - mla_w_dkv.shape[-1])))
    wq = by_cols(full["mla_w_uq"]).reshape(n_mla, Q_RANK, HEADS, QK_HEAD)
    wq = jnp.pad(wq, ((0, 0), (0, 0), (0, 0), (0, HEAD_PAD - QK_HEAD))).reshape(n_mla, Q_RANK, HEADS * HEAD_PAD)
    wkv = by_cols(full["mla_w_ukv"])
    wo = by_rows(full["mla_w_o"])
    ln_g = by_rows(full["sgu_ln_g"]).reshape(sgu_ln_g.shape[0], -1)
    ln_b = by_rows(full["sgu_ln_b"]).reshape(sgu_ln_b.shape[0], -1)

    loss, dx, small, mats, big = _local_step(
        x[0], positions[0], loss_target[0], norm_mix, norm_ffn, final_norm, wdkv, mla_q_norm, mla_kv_norm, wq, wkv, wo,
        full["sgu_w_in"], ln_g, ln_b, sgu_w_spatial, sgu_b_spatial, full["sgu_w_out"], full["ffn_w_up"], full["ffn_w_down"])
    loss = lax.psum(loss, ("x", "y", "c"))

    rows_of = lambda g: g.reshape(g.shape[0], N_SHARDS, g.shape[1] // N_SHARDS, g.shape[2]).transpose(1, 0, 2, 3)
    cols_of = lambda g: g.reshape(g.shape[0], g.shape[1], N_SHARDS, g.shape[2] // N_SHARDS).transpose(2, 0, 1, 3)
    g_wq = jnp.stack(mats["wq"]).reshape(n_mla, Q_RANK, HEADS, HEAD_PAD)[..., :QK_HEAD].reshape(n_mla, Q_RANK, HEADS * QK_HEAD)
    stacks = {
        "mla_w_dkv": rows_of(jnp.stack(mats["wdkv"])[:, :, :mla_w_dkv.shape[-1]]),
        "mla_w_uq": cols_of(g_wq),
        "mla_w_ukv": cols_of(jnp.stack(mats["wkv"])),
        "mla_w_o": rows_of(jnp.stack(mats["wo"])),
        "sgu_ln_g": rows_of(jnp.concatenate(mats["ln_g"]).reshape(sgu_ln_g.shape[0], -1, LANE)),
        "sgu_ln_b": rows_of(jnp.concatenate(mats["ln_b"]).reshape(sgu_ln_b.shape[0], -1, LANE)),
    }
    stacks = {n: (g, g.astype(BF16)) for n, g in stacks.items()}
    stacks.update({"sgu_w_in": big["w_in"], "sgu_w_out": big["w_out"], "ffn_w_up": big["w_up"], "ffn_w_down": big["w_down"]})

    x_i, y_i, c_i = lax.axis_index("x"), lax.axis_index("y"), lax.axis_index("c")
    half = jnp.reshape(c_i, (1,)).astype(jnp.int32)
    sel = jnp.stack([2 * x_i + y_i, c_i]).astype(jnp.int32)
    from_sibling = _sibling_swap([stacks[n][1] for n in SHARDED])
    partials = [_chip_partial(stacks[n][0], fs, half) for n, fs in zip(SHARDED, from_sibling)]
    from_chips = _chip_exchange(partials)
    halves = [_reduce_own(stacks[n][0], fs, fc, sel) for n, fs, fc in zip(SHARDED, from_sibling, from_chips)]
    grads = dict(zip(SHARDED, _sibling_share(halves)))

    small_g = [small["norm_mix"], small["norm_ffn"], small["final_norm"], small["q_norm"], small["kv_norm"],
               small["w_sp"], small["b_sp"]]
    like = [w[n] for n in REPLICATED]
    g_small = _all_reduce_small(_pack(small_g))
    packed = [_pack([d[n] for n in REPLICATED])[None] for d in (w, mom, var)]
    upd_small = _adamw(packed[0], g_small[None], packed[1], packed[2])
    grads.update(zip(REPLICATED, _unpack(g_small, like)))
    delta, new_m, new_v = ({n: a for n, a in zip(REPLICATED, _unpack(u[0], like))} for u in upd_small)

    for n in SHARDED:
        d, nm, nv = _adamw(_as3d(n, w[n]), grads[n], _as3d(n, mom[n]), _as3d(n, var[n]))
        grads[n], delta[n], new_m[n], new_v[n] = (a.reshape(w[n].shape) for a in (grads[n], d, nm, nv))

    return (loss, dx[None], *[grads[n] for n in WEIGHTS], *[delta[n] for n in WEIGHTS],
            *[new_m[n] for n in WEIGHTS], *[new_v[n] for n in WEIGHTS])
```

```python
import functools
import math

import jax
import jax.numpy as jnp
from jax import lax
from jax.experimental import pallas as pl
from jax.experimental.pallas import tpu as pltpu
from jax.experimental.pallas import tpu_sc as plsc

F32 = jnp.float32
BF16 = jnp.bfloat16
MESH = pl.DeviceIdType.MESH

DEPTH = 4
HEADS = 8
NOPE = 128
ROPE = 64
VHEAD = 128
QK_HEAD = NOPE + ROPE
Q_RANK = 256
KV_RANK = 128
HEAD_PAD = 256
LAT_PAD = 512
ROPE_THETA = 10000.0
SGU_CHUNK = 128
SGU_GROUPS = 8
NORM_EPS = 1e-6
LN_EPS = 1e-5
ADAM_LR, ADAM_B1, ADAM_B2, ADAM_EPS, ADAM_WD, ADAM_STEP = 0.001, 0.9, 0.999, 1e-08, 0.01, 10

N_SHARDS = 4
LANE = 128
VMEM_LIMIT_BYTES = 56 * 1024 * 1024
ATT_TILE = 512
MM_TILE = 1024
ATT_SCALE = QK_HEAD ** -0.5
LOG2_SCALE = ATT_SCALE * math.log2(math.e)

NN = (((1,), (0,)), ((), ()))
NT = (((1,), (1,)), ((), ()))
TN = (((0,), (0,)), ((), ()))


def _params(sem):
    return pltpu.CompilerParams(dimension_semantics=sem, vmem_limit_bytes=VMEM_LIMIT_BYTES)


def _tile(n, pref):
    t = min(n, pref)
    while n % t:
        t //= 2
    return t


def _matmul(name, a, b, a_spec, b_spec, dims, grid, tile, outs, extras=(), epilogue=None, aliased=()):
    nk, ne, no = grid[2], len(extras), len(outs)

    def body(a_ref, b_ref, *rest):
        e_refs, o_refs = rest[:ne], rest[ne + len(aliased):ne + len(aliased) + no]
        part = lax.dot_general(a_ref[...].astype(BF16), b_ref[...].astype(BF16), dims, preferred_element_type=F32)

        def finish(acc):
            vals = (acc,) if epilogue is None else epilogue(acc, *[e[...] for e in e_refs])
            for o_ref, v in zip(o_refs, vals):
                o_ref[...] = v.astype(o_ref.dtype)

        if nk == 1:
            finish(part)
            return
        acc_ref, k = rest[-1], pl.program_id(2)

        @pl.when(k == 0)
        def _():
            acc_ref[...] = part

        @pl.when(jnp.logical_and(k > 0, k < nk - 1))
        def _():
            acc_ref[...] += part

        @pl.when(k == nk - 1)
        def _():
            finish(acc_ref[...] + part)

    n_in = 2 + ne
    return pl.pallas_call(
        body, name=name, grid=grid,
        in_specs=[a_spec, b_spec] + [s for _, s in extras] + [pl.BlockSpec(memory_space=pl.ANY)] * len(aliased),
        out_specs=[s for _, s in outs], out_shape=[s for s, _ in outs],
        scratch_shapes=[pltpu.VMEM(tile, F32)] if nk > 1 else [],
        input_output_aliases={n_in + i: o for i, (_, o) in enumerate(aliased)},
        compiler_params=_params(("parallel", "parallel", "arbitrary")),
    )(a, b, *[e for e, _ in extras], *[arr for arr, _ in aliased])


def _mm(name, a, b, out_dtypes=(F32,), epilogue=None, extras=(), tm=MM_TILE, tn=MM_TILE, tk=MM_TILE, nt=False):
    m, kd = a.shape
    n = b.shape[0] if nt else b.shape[1]
    tm, tn, tk = _tile(m, tm), _tile(n, tn), _tile(kd, tk)
    o_spec = pl.BlockSpec((tm, tn), lambda i, j, k: (i, j))
    b_spec = pl.BlockSpec((tn, tk), lambda i, j, k: (j, k)) if nt else pl.BlockSpec((tk, tn), lambda i, j, k: (k, j))
    return _matmul(name, a, b, pl.BlockSpec((tm, tk), lambda i, j, k: (i, k)), b_spec, NT if nt else NN,
                   (m // tm, n // tn, kd // tk), (tm, tn),
                   [(jax.ShapeDtypeStruct((m, n), d), o_spec) for d in out_dtypes],
                   [(e, o_spec) for e in extras], epilogue)


def _mm_tn(name, a, b, out_dtypes=(F32,), tm=MM_TILE, tn=MM_TILE, tk=MM_TILE):
    s, m = a.shape
    n = b.shape[1]
    tm, tn, tk = _tile(m, tm), _tile(n, tn), _tile(s, tk)
    o_spec = pl.BlockSpec((tm, tn), lambda i, j, k: (i, j))
    return _matmul(name, a, b, pl.BlockSpec((tk, tm), lambda i, j, k: (k, i)),
                   pl.BlockSpec((tk, tn), lambda i, j, k: (k, j)), TN, (m // tm, n // tn, s // tk), (tm, tn),
                   [(jax.ShapeDtypeStruct((m, n), d), o_spec) for d in out_dtypes])


def _mm_stacked(name, a, w3, mode, out_dtypes=(F32,), epilogue=None, extras=(), tm=MM_TILE, tn=MM_TILE, tk=MM_TILE):
    m, kd = a.shape
    _, r, c = w3.shape
    n = c if mode == "row" else N_SHARDS * c
    tm = _tile(m, tm)
    if mode == "row":
        tn, tk = _tile(n, tn), _tile(r, tk)
        per = r // tk
        b_spec = pl.BlockSpec((None, tk, tn), lambda i, j, k: (k // per, k % per, j))
    else:
        tn, tk = _tile(c, tn), _tile(kd, tk)
        per = c // tn
        b_spec = pl.BlockSpec((None, tk, tn), lambda i, j, k: (j // per, k, j % per))
    o_spec = pl.BlockSpec((tm, tn), lambda i, j, k: (i, j))
    return _matmul(name, a, w3, pl.BlockSpec((tm, tk), lambda i, j, k: (i, k)), b_spec, NN,
                   (m // tm, n // tn, kd // tk), (tm, tn),
                   [(jax.ShapeDtypeStruct((m, n), d), o_spec) for d in out_dtypes],
                   [(e, o_spec) for e in extras], epilogue)


def _mm_stacked_nt(name, a, w3, mode, out_dtypes=(F32,), epilogue=None, extras=(), tm=MM_TILE, tn=MM_TILE, tk=MM_TILE):
    m, nd = a.shape
    _, r, c = w3.shape
    kout = N_SHARDS * r if mode == "row" else r
    tm = _tile(m, tm)
    if mode == "row":
        tn, tk = _tile(r, tn), _tile(c, tk)
        per = r // tn
        b_spec = pl.BlockSpec((None, tn, tk), lambda i, j, k: (j // per, j % per, k))
    else:
        tn, tk = _tile(r, tn), _tile(c, tk)
        per = c // tk
        b_spec = pl.BlockSpec((None, tn, tk), lambda i, j, k: (k // per, j, k % per))
    o_spec = pl.BlockSpec((tm, tn), lambda i, j, k: (i, j))
    return _matmul(name, a, w3, pl.BlockSpec((tm, tk), lambda i, j, k: (i, k)), b_spec, NT,
                   (m // tm, kout // tn, nd // tk), (tm, tn),
                   [(jax.ShapeDtypeStruct((m, kout), d), o_spec) for d in out_dtypes],
                   [(e, o_spec) for e in extras], epilogue)


def _mm_tn_stacked(name, a, b, shape4, layer, mode, prev, tm=MM_TILE, tn=MM_TILE, tk=MM_TILE):
    s, m = a.shape
    n = b.shape[1]
    _, _, r, c = shape4
    tk = _tile(s, tk)
    if mode == "row":
        tm, tn = _tile(r, tm), _tile(n, tn)
        per = r // tm
        o_spec = pl.BlockSpec((None, None, tm, tn), lambda i, j, k: (i // per, layer, i % per, j))
    else:
        tm, tn = _tile(m, tm), _tile(c, tn)
        per = c // tn
        o_spec = pl.BlockSpec((None, None, tm, tn), lambda i, j, k: (j // per, layer, i, j % per))
    outs = [(jax.ShapeDtypeStruct(shape4, F32), o_spec), (jax.ShapeDtypeStruct(shape4, BF16), o_spec)]
    aliased = [] if prev is None else [(prev[0], 0), (prev[1], 1)]
    return _matmul(name, a, b, pl.BlockSpec((tk, tm), lambda i, j, k: (k, i)),
                   pl.BlockSpec((tk, tn), lambda i, j, k: (k, j)), TN, (m // tm, n // tn, s // tk), (tm, tn),
                   outs, epilogue=lambda acc: (acc, acc), aliased=aliased)


def _rowwise(name, fn, rows, consts, out_rows, out_accs=(), tr=256):
    nr, nc, no = len(rows), len(consts), len(out_rows)
    n_rows = rows[0].shape[0]
    tr = _tile(n_rows, tr)

    def body(*refs):
        vals = fn(*[r[...] for r in refs[:nr + nc]])
        o_refs, a_refs = refs[nr + nc:nr + nc + no], refs[nr + nc + no:]
        for ref, v in zip(o_refs, vals[:no]):
            ref[...] = v.astype(ref.dtype)
        first = pl.program_id(0) == 0

        @pl.when(first)
        def _():
            for ref, v in zip(a_refs, vals[no:]):
                ref[...] = v

        @pl.when(jnp.logical_not(first))
        def _():
            for ref, v in zip(a_refs, vals[no:]):
                ref[...] += v

    def whole(shape):
        return pl.BlockSpec(shape, lambda i: (0,) * len(shape))

    return pl.pallas_call(
        body, name=name, grid=(n_rows // tr,),
        in_specs=[pl.BlockSpec((tr, a.shape[1]), lambda i: (i, 0)) for a in rows] + [whole(c.shape) for c in consts],
        out_specs=[pl.BlockSpec((tr, f), lambda i: (i, 0)) for f, _ in out_rows] + [whole(s) for s in out_accs],
        out_shape=[jax.ShapeDtypeStruct((n_rows, f), d) for f, d in out_rows]
        + [jax.ShapeDtypeStruct(s, F32) for s in out_accs],
        compiler_params=_params(("arbitrary",)),
    )(*rows, *consts)


def _rms_fwd(x, g):
    return x * lax.rsqrt(jnp.mean(x * x, axis=-1, keepdims=True) + NORM_EPS) * g


def _rms_bwd(dy, x, g):
    rstd = lax.rsqrt(jnp.mean(x * x, axis=-1, keepdims=True) + NORM_EPS)
    n = x * rstd
    dn = dy * g
    dx = rstd * (dn - n * jnp.mean(dn * n, axis=-1, keepdims=True))
    return dx, jnp.sum(dy * n, axis=0, keepdims=True)


def _rope(x, cs, s1, s2):
    return x * cs + pltpu.roll(x, 32, 1) * s1 + pltpu.roll(x, 96, 1) * s2


def _rope_t(dy, cs, s1, s2):
    return dy * cs + pltpu.roll(dy * s1, 96, 1) + pltpu.roll(dy * s2, 32, 1)


def _gelu(z):
    return 0.5 * z * (1.0 + lax.erf(z * (1.0 / math.sqrt(2.0))))


def _gelu_grad(z):
    return 0.5 * (1.0 + lax.erf(z * (1.0 / math.sqrt(2.0)))) + z * jnp.exp(-0.5 * z * z) * (1.0 / math.sqrt(2.0 * math.pi))


def _att_scores(q, kv, kr, scale, masked, transposed):
    k = jnp.concatenate([kv[:, :NOPE], kr], axis=1)
    if transposed:
        s = lax.dot_general(k, q, NT, preferred_element_type=F32) * scale
    else:
        s = lax.dot_general(q, k, NT, preferred_element_type=F32) * scale
    if masked:
        r = lax.broadcasted_iota(jnp.int32, s.shape, 0)
        c = lax.broadcasted_iota(jnp.int32, s.shape, 1)
        s = jnp.where((r <= c) if transposed else (c <= r), s, -jnp.inf)
    return s, k


def _flash_fwd(qb, kvb, krb):
    s_len = qb.shape[0]
    t = ATT_TILE

    def body(q_ref, kv_ref, kr_ref, o_ref, lse_ref, m_s, l_s, acc_s):
        qi = pl.program_id(1)
        m_s[...] = jnp.full_like(m_s, -jnp.inf)
        l_s[...] = jnp.zeros_like(l_s)
        acc_s[...] = jnp.zeros_like(acc_s)
        q = q_ref[...]

        def step(ki, masked):
            rows = pl.ds(pl.multiple_of(ki * t, t), t)
            kv = kv_ref[rows, :]
            s, _ = _att_scores(q, kv, kr_ref[rows, :], LOG2_SCALE, masked, False)
            m_prev = m_s[...]
            m_new = jnp.maximum(m_prev, jnp.max(s, axis=1, keepdims=True))
            alpha = jnp.exp2(m_prev - m_new)
            p = jnp.exp2(s - jnp.tile(m_new, (1, t // LANE)))
            l_s[...] = alpha * l_s[...] + jnp.sum(p, axis=1, keepdims=True)
            acc_s[...] = alpha * acc_s[...] + jnp.dot(p.astype(BF16), kv[:, NOPE:], preferred_element_type=F32)
            m_s[...] = m_new

        def unmasked(ki, carry):
            step(ki, False)
            return carry

        lax.fori_loop(0, qi, unmasked, 0)
        step(qi, True)
        o_ref[...] = (acc_s[...] / l_s[...]).astype(o_ref.dtype)
        lse_ref[...] = (m_s[...] + jnp.log2(l_s[...]))[:, :1]

    return pl.pallas_call(
        body, name="flash_fwd", grid=(HEADS, s_len // t),
        in_specs=[pl.BlockSpec((t, HEAD_PAD), lambda h, qi: (qi, h)),
                  pl.BlockSpec((s_len, HEAD_PAD), lambda h, qi: (0, h)),
                  pl.BlockSpec((s_len, LANE), lambda h, qi: (0, 0))],
        out_specs=[pl.BlockSpec((t, VHEAD), lambda h, qi: (qi, h)),
                   pl.BlockSpec((None, t, 1), lambda h, qi: (h, qi, 0))],
        out_shape=[jax.ShapeDtypeStruct((s_len, HEADS * VHEAD), BF16),
                   jax.ShapeDtypeStruct((HEADS, s_len, 1), F32)],
        scratch_shapes=[pltpu.VMEM((t, LANE), F32), pltpu.VMEM((t, LANE), F32), pltpu.VMEM((t, VHEAD), F32)],
        compiler_params=_params(("parallel", "arbitrary")),
    )(qb, kvb, krb)


def _flash_bwd_dq(qb, kvb, krb, dob, lse, delta):
    s_len = qb.shape[0]
    t = ATT_TILE
    nq = s_len // t
    scale = QK_HEAD ** -0.5

    def body(q_ref, kv_ref, kr_ref, do_ref, lse_ref, dl_ref, dq_ref, acc_s):
        qi = pl.program_id(1)
        acc_s[...] = jnp.zeros_like(acc_s)
        q, do = q_ref[...], do_ref[...]
        lse = jnp.broadcast_to(lse_ref[...], (t, LANE))
        dl = jnp.broadcast_to(dl_ref[...], (t, LANE))

        def step(ki, masked):
            rows = pl.ds(pl.multiple_of(ki * t, t), t)
            kv = kv_ref[rows, :]
            s, k = _att_scores(q, kv, kr_ref[rows, :], LOG2_SCALE, masked, False)
            p = jnp.exp2(s - jnp.tile(lse, (1, t // LANE)))
            dp = lax.dot_general(do, kv[:, NOPE:], NT, preferred_element_type=F32)
            ds = (p * (dp - jnp.tile(dl, (1, t // LANE))) * scale).astype(BF16)
            acc_s[...] += jnp.dot(ds, k, preferred_element_type=F32)

        def unmasked(ki, carry):
            step(ki, False)
            return carry

        lax.fori_loop(0, qi, unmasked, 0)
        step(qi, True)
        dq_ref[...] = acc_s[...]

    col = pl.BlockSpec((None, t, 1), lambda h, qi: (h, qi, 0))
    return pl.pallas_call(
        body, name="flash_bwd_dq", grid=(HEADS, nq),
        in_specs=[pl.BlockSpec((t, HEAD_PAD), lambda h, qi: (qi, h)),
                  pl.BlockSpec((s_len, HEAD_PAD), lambda h, qi: (0, h)),
                  pl.BlockSpec((s_len, LANE), lambda h, qi: (0, 0)),
                  pl.BlockSpec((t, VHEAD), lambda h, qi: (qi, h)), col, col],
        out_specs=pl.BlockSpec((t, HEAD_PAD), lambda h, qi: (qi, h)),
        out_shape=jax.ShapeDtypeStruct((s_len, HEADS * HEAD_PAD), F32),
        scratch_shapes=[pltpu.VMEM((t, HEAD_PAD), F32)],
        compiler_params=_params(("parallel", "arbitrary")),
    )(qb, kvb, krb, dob, lse, delta)


def _flash_bwd_dkv(qb, kvb, krb, dob, lse_row, delta_row):
    s_len = qb.shape[0]
    t = ATT_TILE
    nq = s_len // t
    scale = QK_HEAD ** -0.5

    def body(q_ref, kv_ref, kr_ref, do_ref, lse_ref, dl_ref, dkv_ref, dkr_ref, dk_s, dv_s):
        ki = pl.program_id(1)
        dk_s[...] = jnp.zeros_like(dk_s)
        dv_s[...] = jnp.zeros_like(dv_s)
        kv, kr = kv_ref[...], kr_ref[...]

        def step(qi, masked):
            rows = pl.ds(pl.multiple_of(qi * t, t), t)
            q, do = q_ref[rows, :], do_ref[rows, :]
            st, _ = _att_scores(q, kv, kr, LOG2_SCALE, masked, True)
            pt = jnp.exp2(st - lse_ref[:, rows])
            dv_s[...] += jnp.dot(pt.astype(BF16), do, preferred_element_type=F32)
            dpt = lax.dot_general(kv[:, NOPE:], do, NT, preferred_element_type=F32)
            dst = (pt * (dpt - dl_ref[:, rows]) * scale).astype(BF16)
            dk_s[...] += jnp.dot(dst, q, preferred_element_type=F32)

        def unmasked(qi, carry):
            step(qi, False)
            return carry

        step(ki, True)
        lax.fori_loop(ki + 1, nq, unmasked, 0)
        dk = dk_s[...]
        dkv_ref[...] = jnp.concatenate([dk[:, :NOPE], dv_s[...]], axis=1).astype(dkv_ref.dtype)
        dkr_ref[...] = dk[:, NOPE:]

    row = pl.BlockSpec((None, 1, s_len), lambda h, ki: (h, 0, 0))
    return pl.pallas_call(
        body, name="flash_bwd_dkv", grid=(HEADS, nq),
        in_specs=[pl.BlockSpec((s_len, HEAD_PAD), lambda h, ki: (0, h)),
                  pl.BlockSpec((t, HEAD_PAD), lambda h, ki: (ki, h)),
                  pl.BlockSpec((t, LANE), lambda h, ki: (ki, 0)),
                  pl.BlockSpec((s_len, VHEAD), lambda h, ki: (0, h)), row, row],
        out_specs=[pl.BlockSpec((t, HEAD_PAD), lambda h, ki: (ki, h)),
                   pl.BlockSpec((t, LANE), lambda h, ki: (ki, h))],
        out_shape=[jax.ShapeDtypeStruct((s_len, HEADS * HEAD_PAD), BF16),
                   jax.ShapeDtypeStruct((s_len, HEADS * LANE), F32)],
        scratch_shapes=[pltpu.VMEM((t, HEAD_PAD), F32), pltpu.VMEM((t, VHEAD), F32)],
        compiler_params=_params(("parallel", "parallel")),
    )(qb, kvb, krb, dob, lse_row, delta_row)


def _att_delta(do, ob):
    s_len = do.shape[0]
    t = ATT_TILE

    def body(do_ref, o_ref, dob_ref, dl_ref):
        d = do_ref[...]
        dob_ref[...] = d.astype(dob_ref.dtype)
        dl_ref[...] = jnp.sum(d * o_ref[...].astype(F32), axis=1, keepdims=True)

    blk = pl.BlockSpec((t, VHEAD), lambda i, h: (i, h))
    return pl.pallas_call(
        body, name="att_delta", grid=(s_len // t, HEADS), in_specs=[blk, blk],
        out_specs=[blk, pl.BlockSpec((None, t, 1), lambda i, h: (h, i, 0))],
        out_shape=[jax.ShapeDtypeStruct(do.shape, BF16), jax.ShapeDtypeStruct((HEADS, s_len, 1), F32)],
        compiler_params=_params(("parallel", "parallel")),
    )(do, ob)


def _tril(w):
    r = lax.broadcasted_iota(jnp.int32, w.shape, 0)
    c = lax.broadcasted_iota(jnp.int32, w.shape, 1)
    return jnp.where(c <= r, w, 0.0)


def _sgu_mix(w_ref, vln, gd):
    wcs = [_tril(w_ref[g]).astype(BF16) for g in range(SGU_GROUPS)]
    mixed = jnp.concatenate(
        [jnp.dot(wcs[g], vln[:, g * gd:(g + 1) * gd], preferred_element_type=F32) for g in range(SGU_GROUPS)], axis=1)
    return wcs, mixed


def _sgu_fwd(zpre, ln_g, ln_b, w_sp, bias_full):
    s_len, two_w = zpre.shape
    width = two_w // 2
    gd = width // SGU_GROUPS
    t = SGU_CHUNK

    def body(z_ref, g_ref, b_ref, w_ref, bias_ref, uv_ref):
        u = _gelu(z_ref[:, :width])
        v = _gelu(z_ref[:, width:])
        d = v - jnp.mean(v, axis=-1, keepdims=True)
        vhat = d * lax.rsqrt(jnp.mean(d * d, axis=-1, keepdims=True) + LN_EPS)
        vln = (vhat * g_ref[...] + b_ref[...]).astype(BF16)
        _, mixed = _sgu_mix(w_ref, vln, gd)
        uv_ref[...] = (u * (mixed + bias_ref[...])).astype(uv_ref.dtype)

    return pl.pallas_call(
        body, name="sgu_fwd", grid=(s_len // t,),
        in_specs=[pl.BlockSpec((t, two_w), lambda i: (i, 0)), pl.BlockSpec((1, width), lambda i: (0, 0)),
                  pl.BlockSpec((1, width), lambda i: (0, 0)), pl.BlockSpec(w_sp.shape, lambda i: (0, 0, 0)),
                  pl.BlockSpec((t, width), lambda i: (0, 0))],
        out_specs=pl.BlockSpec((t, width), lambda i: (i, 0)),
        out_shape=jax.ShapeDtypeStruct((s_len, width), BF16),
        compiler_params=_params(("parallel",)),
    )(zpre, ln_g, ln_b, w_sp, bias_full)


def _sgu_bwd(zpre, duv, ln_g, ln_b, w_sp, bias_full):
    s_len, two_w = zpre.shape
    width = two_w // 2
    gd = width // SGU_GROUPS
    t = SGU_CHUNK

    def body(z_ref, duv_ref, g_ref, b_ref, w_ref, bias_ref, dz_ref, dg_ref, db_ref, dw_ref, dbias_ref):
        first = pl.program_id(0) == 0

        def accumulate(ref, val):
            @pl.when(first)
            def _():
                ref[...] = val

            @pl.when(jnp.logical_not(first))
            def _():
                ref[...] += val

        zu, zv = z_ref[:, :width], z_ref[:, width:]
        u = _gelu(zu)
        v = _gelu(zv)
        d = v - jnp.mean(v, axis=-1, keepdims=True)
        rstd = lax.rsqrt(jnp.mean(d * d, axis=-1, keepdims=True) + LN_EPS)
        vhat = d * rstd
        vln = (vhat * g_ref[...] + b_ref[...]).astype(BF16)
        wcs, mixed = _sgu_mix(w_ref, vln, gd)
        duv_v = duv_ref[...]
        du = duv_v * (mixed + bias_ref[...])
        dmixed = duv_v * u
        dmb = dmixed.astype(BF16)
        dvln = jnp.concatenate(
            [lax.dot_general(wcs[g], dmb[:, g * gd:(g + 1) * gd], TN, preferred_element_type=F32)
             for g in range(SGU_GROUPS)], axis=1)
        for g in range(SGU_GROUPS):
            dw = lax.dot_general(dmb[:, g * gd:(g + 1) * gd], vln[:, g * gd:(g + 1) * gd], NT, preferred_element_type=F32)
            accumulate(dw_ref.at[g], _tril(dw))
        dvhat = dvln * g_ref[...]
        dv0 = rstd * (dvhat - jnp.mean(dvhat, axis=-1, keepdims=True)
                      - vhat * jnp.mean(dvhat * vhat, axis=-1, keepdims=True))
        dz_ref[:, :width] = (du * _gelu_grad(zu)).astype(dz_ref.dtype)
        dz_ref[:, width:] = (dv0 * _gelu_grad(zv)).astype(dz_ref.dtype)
        accumulate(dg_ref, jnp.sum(dvln * vhat, axis=0, keepdims=True))
        accumulate(db_ref, jnp.sum(dvln, axis=0, keepdims=True))
        accumulate(dbias_ref, dmixed)

    vec = pl.BlockSpec((1, width), lambda i: (0, 0))
    return pl.pallas_call(
        body, name="sgu_bwd", grid=(s_len // t,),
        in_specs=[pl.BlockSpec((t, two_w), lambda i: (i, 0)), pl.BlockSpec((t, width), lambda i: (i, 0)), vec, vec,
                  pl.BlockSpec(w_sp.shape, lambda i: (0, 0, 0)), pl.BlockSpec((t, width), lambda i: (0, 0))],
        out_specs=[pl.BlockSpec((t, two_w), lambda i: (i, 0)), vec, vec,
                   pl.BlockSpec(w_sp.shape, lambda i: (0, 0, 0)), pl.BlockSpec((t, width), lambda i: (0, 0))],
        out_shape=[jax.ShapeDtypeStruct((s_len, two_w), BF16), jax.ShapeDtypeStruct((1, width), F32),
                   jax.ShapeDtypeStruct((1, width), F32), jax.ShapeDtypeStruct(w_sp.shape, F32),
                   jax.ShapeDtypeStruct((t, width), F32)],
        compiler_params=_params(("arbitrary",)),
    )(zpre, duv, ln_g, ln_b, w_sp, bias_full)


def _rope_tables(positions):
    inv_freq = ROPE_THETA ** (-jnp.arange(0, ROPE, 2, dtype=F32) / ROPE)
    ang = positions.astype(F32)[:, None] * inv_freq
    cos, sin = jnp.cos(ang), jnp.sin(ang)
    z32, z64 = jnp.zeros_like(cos), jnp.zeros((cos.shape[0], LANE - ROPE), F32)
    return (jnp.concatenate([cos, cos, z64], axis=1), jnp.concatenate([z32, sin, z64], axis=1),
            jnp.concatenate([-sin, z32, z64], axis=1))


def _q_rope(name, q, tables, transpose):
    rot = _rope_t if transpose else _rope

    def fn(qv, cs, s1, s2):
        parts = []
        for h in range(HEADS):
            parts.append(qv[:, h * HEAD_PAD:h * HEAD_PAD + NOPE])
            parts.append(rot(qv[:, h * HEAD_PAD + NOPE:(h + 1) * HEAD_PAD], cs, s1, s2))
        return (jnp.concatenate(parts, axis=1),)

    return _rowwise(name, fn, [q, *tables], [], [(q.shape[1], BF16)])[0]


def _ffn_fwd(x, g, w_up3, w_down3):
    h2 = _rowwise("ffn_norm", lambda xv, gv: (_rms_fwd(xv, gv),), [x], [g], [(x.shape[1], BF16)])[0]

    def sq_relu(acc):
        r = jnp.maximum(acc, 0.0)
        return (r * r,)

    r = _mm_stacked("ffn_up", h2, w_up3, "col", (BF16,), sq_relu)[0]
    x_out = _mm_stacked("ffn_down", r, w_down3, "row", (F32,), lambda acc, res: (acc + res,), [x])[0]
    return x_out, (x, h2, r)


def _stack_shape(w3, layers):
    return (N_SHARDS, layers, *w3.shape[1:])


def _ffn_bwd(i, dx, dxb, saved, g, w_up3, w_down3, g_up, g_down):
    x, h2, r = saved
    da = _mm_stacked_nt("ffn_down_dx", dxb, w_down3, "row", (BF16,),
                        lambda acc, rv: (acc * (2.0 * jnp.sqrt(rv.astype(F32))),), [r])[0]
    g_down = _mm_tn_stacked("ffn_down_dw", r, dxb, _stack_shape(w_down3, DEPTH), i, "row", g_down)
    dh2 = _mm_stacked_nt("ffn_up_dx", da, w_up3, "col", (F32,))[0]
    g_up = _mm_tn_stacked("ffn_up_dw", h2, da, _stack_shape(w_up3, DEPTH), i, "col", g_up)
    dx, dxb, dg = _norm_bwd("ffn_norm_bwd", dh2, x, g, dx)
    return dx, dxb, dg, g_up, g_down


def _norm_bwd(name, dh, x, g, dres):
    def fn(dhv, xv, rv, gv):
        dxv, dg = _rms_bwd(dhv, xv, gv)
        return dxv + rv, dxv + rv, dg

    return _rowwise(name, fn, [dh, x, dres], [g], [(x.shape[1], F32), (x.shape[1], BF16)], [g.shape])


def _mla_fwd(x, g, wdkv, q_norm, kv_norm, wq, wkv, wo, tables):
    d = x.shape[1]
    h = _rowwise("mla_norm", lambda xv, gv: (_rms_fwd(xv, gv),), [x], [g], [(d, BF16)])[0]
    lat = _mm("mla_dkv", h, wdkv)[0]

    def lat_post(lv, cs, s1, s2, qg, kg):
        return (_rms_fwd(lv[:, :Q_RANK], qg), _rms_fwd(lv[:, Q_RANK:Q_RANK + KV_RANK], kg),
                _rope(lv[:, Q_RANK + KV_RANK:], cs, s1, s2))

    cqn, ckvn, krb = _rowwise("mla_lat", lat_post, [lat, *tables], [q_norm, kv_norm],
                              [(Q_RANK, BF16), (KV_RANK, BF16), (LANE, BF16)])
    q = _mm("mla_uq", cqn, wq)[0]
    kvb = _mm("mla_ukv", ckvn, wkv, (BF16,))[0]
    qb = _q_rope("mla_q_rope", q, tables, False)
    ob, lse = _flash_fwd(qb, kvb, krb)
    x_mid = _mm("mla_o", ob, wo, (F32,), lambda acc, res: (acc + res,), [x])[0]
    return x_mid, (x, h, lat, cqn, ckvn, krb, qb, kvb, ob, lse)


def _mla_bwd(dx, dxb, saved, g, wdkv, q_norm, kv_norm, wq, wkv, wo, tables):
    x, h, lat, cqn, ckvn, krb, qb, kvb, ob, lse = saved
    s_len = x.shape[0]
    do = _mm("mla_o_dx", dxb, wo, nt=True)[0]
    g_wo = _mm_tn("mla_o_dw", ob, dxb)[0]
    dob, delta = _att_delta(do, ob)
    dq = _flash_bwd_dq(qb, kvb, krb, dob, lse, delta)
    dkvb, dkr = _flash_bwd_dkv(qb, kvb, krb, dob, lse.reshape(HEADS, 1, s_len), delta.reshape(HEADS, 1, s_len))
    dqb = _q_rope("mla_q_rope_bwd", dq, tables, True)
    dcqn = _mm("mla_uq_dx", dqb, wq, nt=True)[0]
    g_wq = _mm_tn("mla_uq_dw", cqn, dqb)[0]
    dckvn = _mm("mla_ukv_dx", dkvb, wkv, nt=True)[0]
    g_wkv = _mm_tn("mla_ukv_dw", ckvn, dkvb)[0]

    def lat_bwd(dq_v, dkv_v, dkr_v, lv, cs, s1, s2, qg, kg):
        dcq, dqg = _rms_bwd(dq_v, lv[:, :Q_RANK], qg)
        dckv, dkg = _rms_bwd(dkv_v, lv[:, Q_RANK:Q_RANK + KV_RANK], kg)
        dkr_sum = dkr_v[:, :LANE]
        for hd in range(1, HEADS):
            dkr_sum = dkr_sum + dkr_v[:, hd * LANE:(hd + 1) * LANE]
        return jnp.concatenate([dcq, dckv, _rope_t(dkr_sum, cs, s1, s2)], axis=1), dqg, dkg

    dlat, g_qn, g_kvn = _rowwise("mla_lat_bwd", lat_bwd, [dcqn, dckvn, dkr, lat, *tables], [q_norm, kv_norm],
                                 [(LAT_PAD, BF16)], [q_norm.shape, kv_norm.shape])
    dh = _mm("mla_dkv_dx", dlat, wdkv, nt=True)[0]
    g_wdkv = _mm_tn("mla_dkv_dw", h, dlat)[0]
    dx, dxb, dg = _norm_bwd("mla_norm_bwd", dh, x, g, dx)
    return dx, dxb, dg, g_wdkv, g_qn, g_kvn, g_wq, g_wkv, g_wo


def _sgu_layer_fwd(x, g, w_in3, ln_g, ln_b, w_sp, bias_full, w_out3):
    h = _rowwise("sgu_norm", lambda xv, gv: (_rms_fwd(xv, gv),), [x], [g], [(x.shape[1], BF16)])[0]
    zpre = _mm_stacked("sgu_in", h, w_in3, "col")[0]
    uv = _sgu_fwd(zpre, ln_g, ln_b, w_sp, bias_full)
    x_mid = _mm_stacked("sgu_out", uv, w_out3, "row", (F32,), lambda acc, res: (acc + res,), [x])[0]
    return x_mid, (x, h, zpre, uv)


def _sgu_layer_bwd(j, dx, dxb, saved, g, w_in3, ln_g, ln_b, w_sp, bias_full, w_out3, g_in, g_out):
    x, h, zpre, uv = saved
    duv = _mm_stacked_nt("sgu_out_dx", dxb, w_out3, "row")[0]
    g_out = _mm_tn_stacked("sgu_out_dw", uv, dxb, _stack_shape(w_out3, DEPTH // 2), j, "row", g_out)
    dz, g_lng, g_lnb, g_wsp, g_bias = _sgu_bwd(zpre, duv, ln_g, ln_b, w_sp, bias_full)
    dh = _mm_stacked_nt("sgu_in_dx", dz, w_in3, "col")[0]
    g_in = _mm_tn_stacked("sgu_in_dw", h, dz, _stack_shape(w_in3, DEPTH // 2), j, "col", g_in)
    dx, dxb, dg = _norm_bwd("sgu_norm_bwd", dh, x, g, dx)
    return dx, dxb, dg, g_in, g_out, g_lng, g_lnb, g_wsp, g_bias


def _loss_head(x, target, g):
    d = x.shape[1]

    def fn(xv, tv, gv):
        err = _rms_fwd(xv, gv) - tv
        dxv, dg = _rms_bwd(err * (1.0 / d), xv, gv)
        return dxv, dxv, dg, jnp.sum(err * err, axis=0, keepdims=True)

    return _rowwise("loss_head", fn, [x, target], [g], [(d, F32), (d, BF16)], [g.shape, g.shape])


def _local_step(x, positions, target, norm_mix, norm_ffn, final_norm, q_norm, kv_norm, w_sp, b_sp, mla, sgu, ffn):
    tables = _rope_tables(positions)
    width = sgu[0][1].shape[-1]
    gd = width // SGU_GROUPS
    bias_full = [jnp.repeat(b_sp[j].T, gd, axis=1) for j in range(DEPTH // 2)]
    saved = []
    for i in range(DEPTH):
        j = i // 2
        if i % 2 == 0:
            wdkv, wq, wkv, wo = mla[j]
            x, s_mix = _mla_fwd(x, norm_mix[i:i + 1], wdkv, q_norm[j:j + 1], kv_norm[j:j + 1], wq, wkv, wo, tables)
        else:
            w_in3, ln_g, ln_b, w_out3 = sgu[j]
            x, s_mix = _sgu_layer_fwd(x, norm_mix[i:i + 1], w_in3, ln_g, ln_b, w_sp[j], bias_full[j], w_out3)
        x, s_ffn = _ffn_fwd(x, norm_ffn[i:i + 1], *ffn[i])
        saved.append((s_mix, s_ffn))
    dx, dxb, g_final, sq_cols = _loss_head(x, target, final_norm[None, :])
    loss = 0.5 * jnp.sum(sq_cols) / x.shape[1]

    g_mix, g_ffn = [None] * DEPTH, [None] * DEPTH
    g_up = g_down = g_in = g_out = None
    mla_g, sgu_g = [None] * (DEPTH // 2), [None] * (DEPTH // 2)
    for i in reversed(range(DEPTH)):
        j = i // 2
        s_mix, s_ffn = saved[i]
        dx, dxb, g_ffn[i], g_up, g_down = _ffn_bwd(i, dx, dxb, s_ffn, norm_ffn[i:i + 1], *ffn[i], g_up, g_down)
        if i % 2 == 0:
            wdkv, wq, wkv, wo = mla[j]
            dx, dxb, g_mix[i], *mla_g[j] = _mla_bwd(dx, dxb, s_mix, norm_mix[i:i + 1], wdkv, q_norm[j:j + 1],
                                                    kv_norm[j:j + 1], wq, wkv, wo, tables)
        else:
            w_in3, ln_g, ln_b, w_out3 = sgu[j]
            dx, dxb, g_mix[i], g_in, g_out, *sgu_g[j] = _sgu_layer_bwd(
                j, dx, dxb, s_mix, norm_mix[i:i + 1], w_in3, ln_g, ln_b, w_sp[j], bias_full[j], w_out3, g_in, g_out)
    small = dict(
        norm_mix=jnp.concatenate(g_mix, axis=0), norm_ffn=jnp.concatenate(g_ffn, axis=0), final_norm=g_final[0],
        q_norm=jnp.concatenate([m[1] for m in mla_g], axis=0), kv_norm=jnp.concatenate([m[2] for m in mla_g], axis=0),
        w_sp=jnp.stack([s[2] for s in sgu_g]),
        b_sp=jnp.stack([s[3].reshape(SGU_CHUNK, SGU_GROUPS, gd).sum(axis=-1).T for s in sgu_g]))
    mats = dict(wdkv=[m[0] for m in mla_g], wq=[m[3] for m in mla_g], wkv=[m[4] for m in mla_g], wo=[m[5] for m in mla_g],
                ln_g=[s[0] for s in sgu_g], ln_b=[s[1] for s in sgu_g])
    return loss, dx, small, mats, dict(w_in=g_in, w_out=g_out, w_up=g_up, w_down=g_down)


HBM_SPEC = pl.BlockSpec(memory_space=pltpu.HBM)


def _place():
    x, y, c = lax.axis_index("x"), lax.axis_index("y"), lax.axis_index("c")
    return x, y, c, [(1 - x, y), (x, 1 - y), (1 - x, 1 - y)]


def _remote(src, dst, send_sems, recv_sems, k, to):
    return pltpu.make_async_remote_copy(src_ref=src, dst_ref=dst, send_sem=send_sems.at[k], recv_sem=recv_sems.at[k],
                                        device_id=to, device_id_type=MESH)


def _gather_layer(layer, shards):
    n = len(shards)
    split = [s.shape[0] >= 16 for s in shards]

    def body(*refs):
        ins, outs = refs[:n], refs[n:2 * n]
        send_sems, recv_sems, local_sems = refs[2 * n:]
        x, y, c, chips = _place()
        mine = 2 * x + y
        barrier = pltpu.get_barrier_semaphore()
        peers = [(x, y, 1 - c)] + [(*chip, c) for chip in chips]
        for peer in peers:
            pl.semaphore_signal(barrier, inc=1, device_id=peer, device_id_type=MESH)
        pl.semaphore_wait(barrier, len(peers))

        def rows(t, half):
            hr = shards[t].shape[0] // 2
            return pl.ds(half * hr, hr) if split[t] else pl.ds(0, shards[t].shape[0])

        local, sent = [], []
        for t in range(n):
            local.append(pltpu.make_async_copy(ins[t], outs[t].at[mine], local_sems.at[t]))
            local[-1].start()
            for j, chip in enumerate(chips):
                cp = _remote(ins[t].at[rows(t, c)], outs[t].at[mine, rows(t, c)], send_sems, recv_sems, 3 * t + j, (*chip, c))
                cp.start()
                sent.append(cp)
        for j, chip in enumerate(chips):
            theirs = 2 * chip[0] + chip[1]
            for t in range(n):
                piece = outs[t].at[theirs, rows(t, c)]
                _remote(piece, piece, send_sems, recv_sems, 3 * t + j, (x, y, c)).wait_recv()
                if split[t]:
                    cp = _remote(piece, piece, send_sems, recv_sems, 3 * n + 3 * t + j, (x, y, 1 - c))
                    cp.start()
                    sent.append(cp)
        for j, chip in enumerate(chips):
            theirs = 2 * chip[0] + chip[1]
            for t in range(n):
                if split[t]:
                    piece = outs[t].at[theirs, rows(t, 1 - c)]
                    _remote(piece, piece, send_sems, recv_sems, 3 * n + 3 * t + j, (x, y, c)).wait_recv()
        for cp in sent:
            cp.wait_send()
        for cp in local:
            cp.wait()

    return pl.kernel(
        body, name=f"gather_layer{layer}", mesh=plsc.ScalarSubcoreMesh(axis_name="sequencer", num_cores=1),
        out_type=[jax.ShapeDtypeStruct((N_SHARDS, *s.shape), s.dtype) for s in shards],
        scratch_types=[pltpu.SemaphoreType.DMA((6 * n,)), pltpu.SemaphoreType.DMA((6 * n,)), pltpu.SemaphoreType.DMA((n,))],
        compiler_params=pltpu.CompilerParams(collective_id=layer),
    )(*shards)


def _sibling_swap(stacks):
    n = len(stacks)

    def body(*refs):
        ins, outs, send_sems, recv_sems = refs[:n], refs[n:2 * n], refs[2 * n], refs[2 * n + 1]
        x, y, c, _ = _place()
        sent = []
        for t in range(n):
            lh = stacks[t].shape[1] // 2
            sent.append(_remote(ins[t].at[:, pl.ds((1 - c) * lh, lh)], outs[t], send_sems, recv_sems, t, (x, y, 1 - c)))
            sent[-1].start()
        for cp in sent:
            cp.wait()

    return pl.pallas_call(
        body, name="reduce_sibling_swap", in_specs=[HBM_SPEC] * n, out_specs=[HBM_SPEC] * n,
        out_shape=[jax.ShapeDtypeStruct((s.shape[0], s.shape[1] // 2, *s.shape[2:]), s.dtype) for s in stacks],
        scratch_shapes=[pltpu.SemaphoreType.DMA((n,)), pltpu.SemaphoreType.DMA((n,))],
    )(*stacks)


def _chip_exchange(partials):
    n = len(partials)

    def body(*refs):
        ins, outs, send_sems, recv_sems = refs[:n], refs[n:2 * n], refs[2 * n], refs[2 * n + 1]
        x, y, c, chips = _place()
        sent = []
        for t in range(n):
            for j, chip in enumerate(chips):
                sent.append(_remote(ins[t].at[2 * chip[0] + chip[1]], outs[t].at[j], send_sems, recv_sems, 3 * t + j, (*chip, c)))
                sent[-1].start()
        for cp in sent:
            cp.wait()

    return pl.pallas_call(
        body, name="reduce_chip_exchange", in_specs=[HBM_SPEC] * n, out_specs=[HBM_SPEC] * n,
        out_shape=[jax.ShapeDtypeStruct((3, *s.shape[1:]), s.dtype) for s in partials],
        scratch_shapes=[pltpu.SemaphoreType.DMA((3 * n,)), pltpu.SemaphoreType.DMA((3 * n,))],
    )(*partials)


def _sibling_share(grads):
    n = len(grads)

    def body(*refs):
        ins, outs, send_sems, recv_sems = refs[:n], refs[n:2 * n], refs[2 * n], refs[2 * n + 1]
        x, y, c, _ = _place()
        sent = []
        for t in range(n):
            lh = grads[t].shape[0] // 2
            half = pl.ds(c * lh, lh)
            sent.append(_remote(ins[t].at[half], outs[t].at[half], send_sems, recv_sems, t, (x, y, 1 - c)))
            sent[-1].start()
        for t, cp in enumerate(sent):
            cp.wait_send()
            lh = grads[t].shape[0] // 2
            theirs = outs[t].at[pl.ds((1 - c) * lh, lh)]
            _remote(theirs, theirs, send_sems, recv_sems, t, (x, y, c)).wait_recv()

    return pl.pallas_call(
        body, name="reduce_sibling_share", in_specs=[HBM_SPEC] * n, out_specs=[HBM_SPEC] * n,
        out_shape=[jax.ShapeDtypeStruct(g.shape, g.dtype) for g in grads],
        input_output_aliases={t: t for t in range(n)},
        scratch_shapes=[pltpu.SemaphoreType.DMA((n,)), pltpu.SemaphoreType.DMA((n,))],
    )(*grads)


def _all_reduce_small(part):
    rows = part.shape[0]

    def body(p_ref, out_ref, sib_buf, chip_sums, send_sems, recv_sems):
        x, y, c, chips = _place()
        mine = 2 * x + y
        swap = _remote(p_ref, sib_buf, send_sems, recv_sems, 0, (x, y, 1 - c))
        swap.start()
        swap.wait()
        chip_sums[mine] = p_ref[...] + sib_buf[...]
        sent = [_remote(chip_sums.at[mine], chip_sums.at[mine], send_sems, recv_sems, 1 + j, (*chip, c))
                for j, chip in enumerate(chips)]
        for cp in sent:
            cp.start()
        for j, chip in enumerate(chips):
            sent[j].wait_send()
            theirs = chip_sums.at[2 * chip[0] + chip[1]]
            _remote(theirs, theirs, send_sems, recv_sems, 1 + j, (x, y, c)).wait_recv()
        out_ref[...] = ((chip_sums[0] + chip_sums[1]) + chip_sums[2]) + chip_sums[3]

    vmem = pl.BlockSpec(memory_space=pltpu.VMEM)
    return pl.pallas_call(
        body, name="all_reduce_small", in_specs=[vmem], out_specs=vmem, out_shape=jax.ShapeDtypeStruct(part.shape, F32),
        scratch_shapes=[pltpu.VMEM((rows, LANE), F32), pltpu.VMEM((N_SHARDS, rows, LANE), F32),
                        pltpu.SemaphoreType.DMA((4,)), pltpu.SemaphoreType.DMA((4,))],
        compiler_params=pltpu.CompilerParams(vmem_limit_bytes=VMEM_LIMIT_BYTES),
    )(part)


def _chip_partial(g4, from_sibling, half):
    _, lh, r, c = from_sibling.shape
    tr = _tile(r, 512)

    def body(half_ref, g_ref, s_ref, o_ref):
        o_ref[...] = (g_ref[...] + s_ref[...].astype(F32)).astype(o_ref.dtype)

    blk = (None, None, tr, c)
    return pl.pallas_call(
        body, name="reduce_chip_partial",
        grid_spec=pltpu.PrefetchScalarGridSpec(
            num_scalar_prefetch=1, grid=(N_SHARDS, lh, r // tr),
            in_specs=[pl.BlockSpec(blk, lambda s, l, i, hr: (s, hr[0] * lh + l, i, 0)),
                      pl.BlockSpec(blk, lambda s, l, i, hr: (s, l, i, 0))],
            out_specs=pl.BlockSpec(blk, lambda s, l, i, hr: (s, l, i, 0))),
        out_shape=jax.ShapeDtypeStruct(from_sibling.shape, BF16),
        compiler_params=_params(("parallel", "parallel", "parallel")),
    )(half, g4, from_sibling)


def _reduce_own(g4, from_sibling, from_chips, sel):
    _, lh, r, c = from_sibling.shape

    tr = _tile(r, 512)

    def body(sel_ref, g_ref, s_ref, c0_ref, c1_ref, c2_ref, o_ref):
        acc = g_ref[...] + s_ref[...].astype(F32)
        for ref in (c0_ref, c1_ref, c2_ref):
            acc = acc + ref[...].astype(F32)
        o_ref[...] = acc

    blk = (None, None, tr, c)
    chip_specs = [pl.BlockSpec(blk, functools.partial(lambda l, i, sr, j: (j, l, i, 0), j=j)) for j in range(3)]
    return pl.pallas_call(
        body, name="reduce_own_shard",
        grid_spec=pltpu.PrefetchScalarGridSpec(
            num_scalar_prefetch=1, grid=(lh, r // tr),
            in_specs=[pl.BlockSpec(blk, lambda l, i, sr: (sr[0], sr[1] * lh + l, i, 0)),
                      pl.BlockSpec(blk, lambda l, i, sr: (sr[0], l, i, 0))] + chip_specs,
            out_specs=pl.BlockSpec((None, tr, c), lambda l, i, sr: (sr[1] * lh + l, i, 0))),
        out_shape=jax.ShapeDtypeStruct((2 * lh, r, c), F32),
        compiler_params=_params(("parallel", "parallel")),
    )(sel, g4, from_sibling, from_chips, from_chips, from_chips)


def _adamw(w, g, m, v):
    lyr, r, c = w.shape
    tr = _tile(r, 512)

    def body(w_ref, g_ref, m_ref, v_ref, d_ref, nm_ref, nv_ref):
        gv = g_ref[...]
        nm = ADAM_B1 * m_ref[...] + (1.0 - ADAM_B1) * gv
        nv = ADAM_B2 * v_ref[...] + (1.0 - ADAM_B2) * (gv * gv)
        m_hat = nm / (1.0 - ADAM_B1 ** ADAM_STEP)
        v_hat = nv / (1.0 - ADAM_B2 ** ADAM_STEP)
        d_ref[...] = -ADAM_LR * (m_hat / (jnp.sqrt(v_hat) + ADAM_EPS) + ADAM_WD * w_ref[...])
        nm_ref[...] = nm
        nv_ref[...] = nv

    blk = pl.BlockSpec((None, tr, c), lambda l, i: (l, i, 0))
    return pl.pallas_call(
        body, name="adamw", grid=(lyr, r // tr), in_specs=[blk] * 4, out_specs=[blk] * 3,
        out_shape=[jax.ShapeDtypeStruct(w.shape, F32)] * 3,
        compiler_params=_params(("parallel", "parallel")),
    )(w, g, m, v)


SHARDED = ("mla_w_dkv", "mla_w_uq", "mla_w_ukv", "mla_w_o", "sgu_w_in", "sgu_ln_g", "sgu_ln_b", "sgu_w_out",
           "ffn_w_up", "ffn_w_down")
REPLICATED = ("norm_mix", "norm_ffn", "final_norm", "mla_q_norm", "mla_kv_norm", "sgu_w_spatial", "sgu_b_spatial")
WEIGHTS = ("norm_mix", "norm_ffn", "final_norm", "mla_w_dkv", "mla_q_norm", "mla_kv_norm", "mla_w_uq", "mla_w_ukv",
           "mla_w_o", "sgu_w_in", "sgu_ln_g", "sgu_ln_b", "sgu_w_spatial", "sgu_b_spatial", "sgu_w_out", "ffn_w_up",
           "ffn_w_down")


def _as3d(name, a):
    return a.reshape(a.shape[0], -1, LANE) if name in ("sgu_ln_g", "sgu_ln_b") else a


def _pack(parts):
    flat = jnp.concatenate([p.reshape(-1) for p in parts])
    rows = -(-flat.shape[0] // (8 * LANE)) * 8
    return jnp.pad(flat, (0, rows * LANE - flat.shape[0])).reshape(rows, LANE)


def _unpack(packed, like):
    flat, out, at = packed.reshape(-1), [], 0
    for p in like:
        out.append(flat[at:at + p.size].reshape(p.shape))
        at += p.size
    return out


def kernel(x, positions, norm_mix, norm_ffn, final_norm, mla_w_dkv, mla_q_norm, mla_kv_norm, mla_w_uq, mla_w_ukv, mla_w_o, sgu_w_in, sgu_ln_g, sgu_ln_b, sgu_w_spatial, sgu_b_spatial, sgu_w_out, ffn_w_up, ffn_w_down, loss_target, m_norm_mix, m_norm_ffn, m_final_norm, m_mla_w_dkv, m_mla_q_norm, m_mla_kv_norm, m_mla_w_uq, m_mla_w_ukv, m_mla_w_o, m_sgu_w_in, m_sgu_ln_g, m_sgu_ln_b, m_sgu_w_spatial, m_sgu_b_spatial, m_sgu_w_out, m_ffn_w_up, m_ffn_w_down, v_norm_mix, v_norm_ffn, v_final_norm, v_mla_w_dkv, v_mla_q_norm, v_mla_kv_norm, v_mla_w_uq, v_mla_w_ukv, v_mla_w_o, v_sgu_w_in, v_sgu_ln_g, v_sgu_ln_b, v_sgu_w_spatial, v_sgu_b_spatial, v_sgu_w_out, v_ffn_w_up, v_ffn_w_down):
    given = dict(locals())
    w = {n: given[n] for n in WEIGHTS}
    mom = {n: given["m_" + n] for n in WEIGHTS}
    var = {n: given["v_" + n] for n in WEIGHTS}
    n_mla = mla_w_dkv.shape[0]

    by_rows = lambda a: a.reshape(N_SHARDS * a.shape[1], a.shape[2])
    by_cols = lambda a: a.transpose(1, 0, 2).reshape(a.shape[1], N_SHARDS * a.shape[2])
    mla, sgu, ffn = [], [], []
    for i in range(DEPTH):
        j = i // 2
        names = ("mla_w_dkv", "mla_w_uq", "mla_w_ukv", "mla_w_o") if i % 2 == 0 else ("sgu_w_in", "sgu_w_out")
        shards = [w[n][j].astype(BF16) for n in names] + [ffn_w_up[i].astype(BF16), ffn_w_down[i].astype(BF16)]
        if i % 2:
            shards += [sgu_ln_g[j].reshape(-1, LANE), sgu_ln_b[j].reshape(-1, LANE)]
        full = _gather_layer(i, shards)
        if i % 2 == 0:
            wdkv = jnp.pad(by_rows(full[0]), ((0, 0), (0, LAT_PAD - mla_w_dkv.shape[-1])))
            wq = jnp.pad(by_cols(full[1]).reshape(Q_RANK, HEADS, QK_HEAD), ((0, 0), (0, 0), (0, HEAD_PAD - QK_HEAD)))
            mla.append((wdkv, wq.reshape(Q_RANK, HEADS * HEAD_PAD), by_cols(full[2]), by_rows(full[3])))
            ffn.append((full[4], full[5]))
        else:
            sgu.append((full[0], full[4].reshape(1, -1), full[5].reshape(1, -1), full[1]))
            ffn.append((full[2], full[3]))

    loss, dx, small, mats, big = _local_step(
        x[0], positions[0], loss_target[0], norm_mix, norm_ffn, final_norm, mla_q_norm, mla_kv_norm, sgu_w_spatial,
        sgu_b_spatial, mla, sgu, ffn)
    loss = lax.psum(loss, ("x", "y", "c"))

    rows_of = lambda g: g.reshape(g.shape[0], N_SHARDS, g.shape[1] // N_SHARDS, g.shape[2]).transpose(1, 0, 2, 3)
    cols_of = lambda g: g.reshape(g.shape[0], g.shape[1], N_SHARDS, g.shape[2] // N_SHARDS).transpose(2, 0, 1, 3)
    g_wq = jnp.stack(mats["wq"]).reshape(n_mla, Q_RANK, HEADS, HEAD_PAD)[..., :QK_HEAD].reshape(n_mla, Q_RANK, HEADS * QK_HEAD)
    stacks = {
        "mla_w_dkv": rows_of(jnp.stack(mats["wdkv"])[:, :, :mla_w_dkv.shape[-1]]),
        "mla_w_uq": cols_of(g_wq),
        "mla_w_ukv": cols_of(jnp.stack(mats["wkv"])),
        "mla_w_o": rows_of(jnp.stack(mats["wo"])),
        "sgu_ln_g": rows_of(jnp.concatenate(mats["ln_g"]).reshape(sgu_ln_g.shape[0], -1, LANE)),
        "sgu_ln_b": rows_of(jnp.concatenate(mats["ln_b"]).reshape(sgu_ln_b.shape[0], -1, LANE)),
    }
    stacks = {n: (g, g.astype(BF16)) for n, g in stacks.items()}
    stacks.update({"sgu_w_in": big["w_in"], "sgu_w_out": big["w_out"], "ffn_w_up": big["w_up"], "ffn_w_down": big["w_down"]})

    x_i, y_i, c_i = lax.axis_index("x"), lax.axis_index("y"), lax.axis_index("c")
    half = jnp.reshape(c_i, (1,)).astype(jnp.int32)
    sel = jnp.stack([2 * x_i + y_i, c_i]).astype(jnp.int32)
    from_sibling = _sibling_swap([stacks[n][1] for n in SHARDED])
    partials = [_chip_partial(stacks[n][0], fs, half) for n, fs in zip(SHARDED, from_sibling)]
    from_chips = _chip_exchange(partials)
    halves = [_reduce_own(stacks[n][0], fs, fc, sel) for n, fs, fc in zip(SHARDED, from_sibling, from_chips)]
    grads = dict(zip(SHARDED, _sibling_share(halves)))

    small_g = [small["norm_mix"], small["norm_ffn"], small["final_norm"], small["q_norm"], small["kv_norm"],
               small["w_sp"], small["b_sp"]]
    like = [w[n] for n in REPLICATED]
    g_small = _all_reduce_small(_pack(small_g))
    packed = [_pack([d[n] for n in REPLICATED])[None] for d in (w, mom, var)]
    upd_small = _adamw(packed[0], g_small[None], packed[1], packed[2])
    grads.update(zip(REPLICATED, _unpack(g_small, like)))
    delta, new_m, new_v = ({n: a for n, a in zip(REPLICATED, _unpack(u[0], like))} for u in upd_small)

    for n in SHARDED:
        d, nm, nv = _adamw(_as3d(n, w[n]), grads[n], _as3d(n, mom[n]), _as3d(n, var[n]))
        grads[n], delta[n], new_m[n], new_v[n] = (a.reshape(w[n].shape) for a in (grads[n], d, nm, nv))

    return (loss, dx[None], *[grads[n] for n in WEIGHTS], *[delta[n] for n in WEIGHTS],
            *[new_m[n] for n in WEIGHTS], *[new_v[n] for n in WEIGHTS])
```

```python
import functools
import math

import jax
import jax.numpy as jnp
from jax import lax
from jax.experimental import pallas as pl
from jax.experimental.pallas import tpu as pltpu
from jax.experimental.pallas import tpu_sc as plsc

F32 = jnp.float32
BF16 = jnp.bfloat16
MESH = pl.DeviceIdType.MESH

DEPTH = 4
HEADS = 8
NOPE = 128
ROPE = 64
VHEAD = 128
QK_HEAD = NOPE + ROPE
Q_RANK = 256
KV_RANK = 128
HEAD_PAD = 256
LAT_PAD = 512
ROPE_THETA = 10000.0
SGU_CHUNK = 128
SGU_GROUPS = 8
NORM_EPS = 1e-6
LN_EPS = 1e-5
ADAM_LR, ADAM_B1, ADAM_B2, ADAM_EPS, ADAM_WD, ADAM_STEP = 0.001, 0.9, 0.999, 1e-08, 0.01, 10

N_SHARDS = 4
LANE = 128
VMEM_LIMIT_BYTES = 56 * 1024 * 1024
ATT_TILE = 512
MM_TILE = 1024
ATT_SCALE = QK_HEAD ** -0.5
LOG2_SCALE = ATT_SCALE * math.log2(math.e)

NN = (((1,), (0,)), ((), ()))
NT = (((1,), (1,)), ((), ()))
TN = (((0,), (0,)), ((), ()))


def _params(sem):
    return pltpu.CompilerParams(dimension_semantics=sem, vmem_limit_bytes=VMEM_LIMIT_BYTES)


def _tile(n, pref):
    t = min(n, pref)
    while n % t:
        t //= 2
    return t


def _matmul(name, a, b, a_spec, b_spec, dims, grid, tile, outs, extras=(), epilogue=None, aliased=()):
    nk, ne, no = grid[2], len(extras), len(outs)

    def body(a_ref, b_ref, *rest):
        e_refs, o_refs = rest[:ne], rest[ne + len(aliased):ne + len(aliased) + no]
        part = lax.dot_general(a_ref[...].astype(BF16), b_ref[...].astype(BF16), dims, preferred_element_type=F32)

        def finish(acc):
            vals = (acc,) if epilogue is None else epilogue(acc, *[e[...] for e in e_refs])
            for o_ref, v in zip(o_refs, vals):
                o_ref[...] = v.astype(o_ref.dtype)

        if nk == 1:
            finish(part)
            return
        acc_ref, k = rest[-1], pl.program_id(2)

        @pl.when(k == 0)
        def _():
            acc_ref[...] = part

        @pl.when(jnp.logical_and(k > 0, k < nk - 1))
        def _():
            acc_ref[...] += part

        @pl.when(k == nk - 1)
        def _():
            finish(acc_ref[...] + part)

    n_in = 2 + ne
    return pl.pallas_call(
        body, name=name, grid=grid,
        in_specs=[a_spec, b_spec] + [s for _, s in extras] + [pl.BlockSpec(memory_space=pl.ANY)] * len(aliased),
        out_specs=[s for _, s in outs], out_shape=[s for s, _ in outs],
        scratch_shapes=[pltpu.VMEM(tile, F32)] if nk > 1 else [],
        input_output_aliases={n_in + i: o for i, (_, o) in enumerate(aliased)},
        compiler_params=_params(("parallel", "parallel", "arbitrary")),
    )(a, b, *[e for e, _ in extras], *[arr for arr, _ in aliased])


def _mm(name, a, b, out_dtypes=(F32,), epilogue=None, extras=(), tm=MM_TILE, tn=MM_TILE, tk=MM_TILE, nt=False):
    m, kd = a.shape
    n = b.shape[0] if nt else b.shape[1]
    tm, tn, tk = _tile(m, tm), _tile(n, tn), _tile(kd, tk)
    o_spec = pl.BlockSpec((tm, tn), lambda i, j, k: (i, j))
    b_spec = pl.BlockSpec((tn, tk), lambda i, j, k: (j, k)) if nt else pl.BlockSpec((tk, tn), lambda i, j, k: (k, j))
    return _matmul(name, a, b, pl.BlockSpec((tm, tk), lambda i, j, k: (i, k)), b_spec, NT if nt else NN,
                   (m // tm, n // tn, kd // tk), (tm, tn),
                   [(jax.ShapeDtypeStruct((m, n), d), o_spec) for d in out_dtypes],
                   [(e, o_spec) for e in extras], epilogue)


def _mm_tn(name, a, b, out_dtypes=(F32,), tm=MM_TILE, tn=MM_TILE, tk=MM_TILE):
    s, m = a.shape
    n = b.shape[1]
    tm, tn, tk = _tile(m, tm), _tile(n, tn), _tile(s, tk)
    o_spec = pl.BlockSpec((tm, tn), lambda i, j, k: (i, j))
    return _matmul(name, a, b, pl.BlockSpec((tk, tm), lambda i, j, k: (k, i)),
                   pl.BlockSpec((tk, tn), lambda i, j, k: (k, j)), TN, (m // tm, n // tn, s // tk), (tm, tn),
                   [(jax.ShapeDtypeStruct((m, n), d), o_spec) for d in out_dtypes])


def _mm_stacked(name, a, w3, mode, out_dtypes=(F32,), epilogue=None, extras=(), tm=MM_TILE, tn=MM_TILE, tk=MM_TILE):
    m, kd = a.shape
    _, r, c = w3.shape
    n = c if mode == "row" else N_SHARDS * c
    tm = _tile(m, tm)
    if mode == "row":
        tn, tk = _tile(n, tn), _tile(r, tk)
        per = r // tk
        b_spec = pl.BlockSpec((None, tk, tn), lambda i, j, k: (k // per, k % per, j))
    else:
        tn, tk = _tile(c, tn), _tile(kd, tk)
        per = c // tn
        b_spec = pl.BlockSpec((None, tk, tn), lambda i, j, k: (j // per, k, j % per))
    o_spec = pl.BlockSpec((tm, tn), lambda i, j, k: (i, j))
    return _matmul(name, a, w3, pl.BlockSpec((tm, tk), lambda i, j, k: (i, k)), b_spec, NN,
                   (m // tm, n // tn, kd // tk), (tm, tn),
                   [(jax.ShapeDtypeStruct((m, n), d), o_spec) for d in out_dtypes],
                   [(e, o_spec) for e in extras], epilogue)


def _mm_stacked_nt(name, a, w3, mode, out_dtypes=(F32,), epilogue=None, extras=(), tm=MM_TILE, tn=MM_TILE, tk=MM_TILE):
    m, nd = a.shape
    _, r, c = w3.shape
    kout = N_SHARDS * r if mode == "row" else r
    tm = _tile(m, tm)
    if mode == "row":
        tn, tk = _tile(r, tn), _tile(c, tk)
        per = r // tn
        b_spec = pl.BlockSpec((None, tn, tk), lambda i, j, k: (j // per, j % per, k))
    else:
        tn, tk = _tile(r, tn), _tile(c, tk)
        per = c // tk
        b_spec = pl.BlockSpec((None, tn, tk), lambda i, j, k: (k // per, j, k % per))
    o_spec = pl.BlockSpec((tm, tn), lambda i, j, k: (i, j))
    return _matmul(name, a, w3, pl.BlockSpec((tm, tk), lambda i, j, k: (i, k)), b_spec, NT,
                   (m // tm, kout // tn, nd // tk), (tm, tn),
                   [(jax.ShapeDtypeStruct((m, kout), d), o_spec) for d in out_dtypes],
                   [(e, o_spec) for e in extras], epilogue)


def _mm_tn_stacked(name, a, b, shape3, mode, tm=MM_TILE, tn=MM_TILE, tk=MM_TILE):
    s, m = a.shape
    n = b.shape[1]
    _, r, c = shape3
    tk = _tile(s, tk)
    if mode == "row":
        tm, tn = _tile(r, tm), _tile(n, tn)
        per = r // tm
        o_spec = pl.BlockSpec((None, tm, tn), lambda i, j, k: (i // per, i % per, j))
    else:
        tm, tn = _tile(m, tm), _tile(c, tn)
        per = c // tn
        o_spec = pl.BlockSpec((None, tm, tn), lambda i, j, k: (j // per, i, j % per))
    outs = [(jax.ShapeDtypeStruct(shape3, F32), o_spec), (jax.ShapeDtypeStruct(shape3, BF16), o_spec)]
    return _matmul(name, a, b, pl.BlockSpec((tk, tm), lambda i, j, k: (k, i)),
                   pl.BlockSpec((tk, tn), lambda i, j, k: (k, j)), TN, (m // tm, n // tn, s // tk), (tm, tn),
                   outs, epilogue=lambda acc: (acc, acc))


def _rowwise(name, fn, rows, consts, out_rows, out_accs=(), tr=256):
    nr, nc, no = len(rows), len(consts), len(out_rows)
    n_rows = rows[0].shape[0]
    tr = _tile(n_rows, tr)

    def body(*refs):
        vals = fn(*[r[...] for r in refs[:nr + nc]])
        o_refs, a_refs = refs[nr + nc:nr + nc + no], refs[nr + nc + no:]
        for ref, v in zip(o_refs, vals[:no]):
            ref[...] = v.astype(ref.dtype)
        first = pl.program_id(0) == 0

        @pl.when(first)
        def _():
            for ref, v in zip(a_refs, vals[no:]):
                ref[...] = v

        @pl.when(jnp.logical_not(first))
        def _():
            for ref, v in zip(a_refs, vals[no:]):
                ref[...] += v

    def whole(shape):
        return pl.BlockSpec(shape, lambda i: (0,) * len(shape))

    return pl.pallas_call(
        body, name=name, grid=(n_rows // tr,),
        in_specs=[pl.BlockSpec((tr, a.shape[1]), lambda i: (i, 0)) for a in rows] + [whole(c.shape) for c in consts],
        out_specs=[pl.BlockSpec((tr, f), lambda i: (i, 0)) for f, _ in out_rows] + [whole(s) for s in out_accs],
        out_shape=[jax.ShapeDtypeStruct((n_rows, f), d) for f, d in out_rows]
        + [jax.ShapeDtypeStruct(s, F32) for s in out_accs],
        compiler_params=_params(("arbitrary",)),
    )(*rows, *consts)


def _rms_fwd(x, g):
    return x * lax.rsqrt(jnp.mean(x * x, axis=-1, keepdims=True) + NORM_EPS) * g


def _rms_bwd(dy, x, g):
    rstd = lax.rsqrt(jnp.mean(x * x, axis=-1, keepdims=True) + NORM_EPS)
    n = x * rstd
    dn = dy * g
    dx = rstd * (dn - n * jnp.mean(dn * n, axis=-1, keepdims=True))
    return dx, jnp.sum(dy * n, axis=0, keepdims=True)


def _rope(x, cs, s1, s2):
    return x * cs + pltpu.roll(x, 32, 1) * s1 + pltpu.roll(x, 96, 1) * s2


def _rope_t(dy, cs, s1, s2):
    return dy * cs + pltpu.roll(dy * s1, 96, 1) + pltpu.roll(dy * s2, 32, 1)


def _gelu(z):
    return 0.5 * z * (1.0 + lax.erf(z * (1.0 / math.sqrt(2.0))))


def _gelu_grad(z):
    return 0.5 * (1.0 + lax.erf(z * (1.0 / math.sqrt(2.0)))) + z * jnp.exp(-0.5 * z * z) * (1.0 / math.sqrt(2.0 * math.pi))


def _att_scores(q, kv, kr, scale, masked, transposed):
    k = jnp.concatenate([kv[:, :NOPE], kr], axis=1)
    if transposed:
        s = lax.dot_general(k, q, NT, preferred_element_type=F32) * scale
    else:
        s = lax.dot_general(q, k, NT, preferred_element_type=F32) * scale
    if masked:
        r = lax.broadcasted_iota(jnp.int32, s.shape, 0)
        c = lax.broadcasted_iota(jnp.int32, s.shape, 1)
        s = jnp.where((r <= c) if transposed else (c <= r), s, -jnp.inf)
    return s, k


def _flash_fwd(qb, kvb, krb):
    s_len = qb.shape[0]
    t = ATT_TILE

    def body(q_ref, kv_ref, kr_ref, o_ref, lse_ref, m_s, l_s, acc_s):
        qi = pl.program_id(1)
        m_s[...] = jnp.full_like(m_s, -jnp.inf)
        l_s[...] = jnp.zeros_like(l_s)
        acc_s[...] = jnp.zeros_like(acc_s)
        q = q_ref[...]

        def step(ki, masked):
            rows = pl.ds(pl.multiple_of(ki * t, t), t)
            kv = kv_ref[rows, :]
            s, _ = _att_scores(q, kv, kr_ref[rows, :], LOG2_SCALE, masked, False)
            m_prev = m_s[...]
            m_new = jnp.maximum(m_prev, jnp.max(s, axis=1, keepdims=True))
            alpha = jnp.exp2(m_prev - m_new)
            p = jnp.exp2(s - jnp.tile(m_new, (1, t // LANE)))
            l_s[...] = alpha * l_s[...] + jnp.sum(p, axis=1, keepdims=True)
            acc_s[...] = alpha * acc_s[...] + jnp.dot(p.astype(BF16), kv[:, NOPE:], preferred_element_type=F32)
            m_s[...] = m_new

        def unmasked(ki, carry):
            step(ki, False)
            return carry

        lax.fori_loop(0, qi, unmasked, 0)
        step(qi, True)
        o_ref[...] = (acc_s[...] / l_s[...]).astype(o_ref.dtype)
        lse_ref[...] = (m_s[...] + jnp.log2(l_s[...]))[:, :1]

    return pl.pallas_call(
        body, name="flash_fwd", grid=(HEADS, s_len // t),
        in_specs=[pl.BlockSpec((t, HEAD_PAD), lambda h, qi: (qi, h)),
                  pl.BlockSpec((s_len, HEAD_PAD), lambda h, qi: (0, h)),
                  pl.BlockSpec((s_len, LANE), lambda h, qi: (0, 0))],
        out_specs=[pl.BlockSpec((t, VHEAD), lambda h, qi: (qi, h)),
                   pl.BlockSpec((None, t, 1), lambda h, qi: (h, qi, 0))],
        out_shape=[jax.ShapeDtypeStruct((s_len, HEADS * VHEAD), BF16),
                   jax.ShapeDtypeStruct((HEADS, s_len, 1), F32)],
        scratch_shapes=[pltpu.VMEM((t, LANE), F32), pltpu.VMEM((t, LANE), F32), pltpu.VMEM((t, VHEAD), F32)],
        compiler_params=_params(("parallel", "arbitrary")),
    )(qb, kvb, krb)


def _flash_bwd_dq(qb, kvb, krb, dob, lse, delta):
    s_len = qb.shape[0]
    t = ATT_TILE
    nq = s_len // t
    scale = QK_HEAD ** -0.5

    def body(q_ref, kv_ref, kr_ref, do_ref, lse_ref, dl_ref, dq_ref, acc_s):
        qi = pl.program_id(1)
        acc_s[...] = jnp.zeros_like(acc_s)
        q, do = q_ref[...], do_ref[...]
        lse = jnp.broadcast_to(lse_ref[...], (t, LANE))
        dl = jnp.broadcast_to(dl_ref[...], (t, LANE))

        def step(ki, masked):
            rows = pl.ds(pl.multiple_of(ki * t, t), t)
            kv = kv_ref[rows, :]
            s, k = _att_scores(q, kv, kr_ref[rows, :], LOG2_SCALE, masked, False)
            p = jnp.exp2(s - jnp.tile(lse, (1, t // LANE)))
            dp = lax.dot_general(do, kv[:, NOPE:], NT, preferred_element_type=F32)
            ds = (p * (dp - jnp.tile(dl, (1, t // LANE))) * scale).astype(BF16)
            acc_s[...] += jnp.dot(ds, k, preferred_element_type=F32)

        def unmasked(ki, carry):
            step(ki, False)
            return carry

        lax.fori_loop(0, qi, unmasked, 0)
        step(qi, True)
        dq_ref[...] = acc_s[...]

    col = pl.BlockSpec((None, t, 1), lambda h, qi: (h, qi, 0))
    return pl.pallas_call(
        body, name="flash_bwd_dq", grid=(HEADS, nq),
        in_specs=[pl.BlockSpec((t, HEAD_PAD), lambda h, qi: (qi, h)),
                  pl.BlockSpec((s_len, HEAD_PAD), lambda h, qi: (0, h)),
                  pl.BlockSpec((s_len, LANE), lambda h, qi: (0, 0)),
                  pl.BlockSpec((t, VHEAD), lambda h, qi: (qi, h)), col, col],
        out_specs=pl.BlockSpec((t, HEAD_PAD), lambda h, qi: (qi, h)),
        out_shape=jax.ShapeDtypeStruct((s_len, HEADS * HEAD_PAD), F32),
        scratch_shapes=[pltpu.VMEM((t, HEAD_PAD), F32)],
        compiler_params=_params(("parallel", "arbitrary")),
    )(qb, kvb, krb, dob, lse, delta)


def _flash_bwd_dkv(qb, kvb, krb, dob, lse_row, delta_row):
    s_len = qb.shape[0]
    t = ATT_TILE
    nq = s_len // t
    scale = QK_HEAD ** -0.5

    def body(q_ref, kv_ref, kr_ref, do_ref, lse_ref, dl_ref, dkv_ref, dkr_ref, dk_s, dv_s):
        ki = pl.program_id(1)
        dk_s[...] = jnp.zeros_like(dk_s)
        dv_s[...] = jnp.zeros_like(dv_s)
        kv, kr = kv_ref[...], kr_ref[...]

        def step(qi, masked):
            rows = pl.ds(pl.multiple_of(qi * t, t), t)
            q, do = q_ref[rows, :], do_ref[rows, :]
            st, _ = _att_scores(q, kv, kr, LOG2_SCALE, masked, True)
            pt = jnp.exp2(st - lse_ref[:, rows])
            dv_s[...] += jnp.dot(pt.astype(BF16), do, preferred_element_type=F32)
            dpt = lax.dot_general(kv[:, NOPE:], do, NT, preferred_element_type=F32)
            dst = (pt * (dpt - dl_ref[:, rows]) * scale).astype(BF16)
            dk_s[...] += jnp.dot(dst, q, preferred_element_type=F32)

        def unmasked(qi, carry):
            step(qi, False)
            return carry

        step(ki, True)
        lax.fori_loop(ki + 1, nq, unmasked, 0)
        dk = dk_s[...]
        dkv_ref[...] = jnp.concatenate([dk[:, :NOPE], dv_s[...]], axis=1).astype(dkv_ref.dtype)
        dkr_ref[...] = dk[:, NOPE:]

    row = pl.BlockSpec((None, 1, s_len), lambda h, ki: (h, 0, 0))
    return pl.pallas_call(
        body, name="flash_bwd_dkv", grid=(HEADS, nq),
        in_specs=[pl.BlockSpec((s_len, HEAD_PAD), lambda h, ki: (0, h)),
                  pl.BlockSpec((t, HEAD_PAD), lambda h, ki: (ki, h)),
                  pl.BlockSpec((t, LANE), lambda h, ki: (ki, 0)),
                  pl.BlockSpec((s_len, VHEAD), lambda h, ki: (0, h)), row, row],
        out_specs=[pl.BlockSpec((t, HEAD_PAD), lambda h, ki: (ki, h)),
                   pl.BlockSpec((t, LANE), lambda h, ki: (ki, h))],
        out_shape=[jax.ShapeDtypeStruct((s_len, HEADS * HEAD_PAD), BF16),
                   jax.ShapeDtypeStruct((s_len, HEADS * LANE), F32)],
        scratch_shapes=[pltpu.VMEM((t, HEAD_PAD), F32), pltpu.VMEM((t, VHEAD), F32)],
        compiler_params=_params(("parallel", "parallel")),
    )(qb, kvb, krb, dob, lse_row, delta_row)


def _att_delta(do, ob):
    s_len = do.shape[0]
    t = ATT_TILE

    def body(do_ref, o_ref, dob_ref, dl_ref):
        d = do_ref[...]
        dob_ref[...] = d.astype(dob_ref.dtype)
        dl_ref[...] = jnp.sum(d * o_ref[...].astype(F32), axis=1, keepdims=True)

    blk = pl.BlockSpec((t, VHEAD), lambda i, h: (i, h))
    return pl.pallas_call(
        body, name="att_delta", grid=(s_len // t, HEADS), in_specs=[blk, blk],
        out_specs=[blk, pl.BlockSpec((None, t, 1), lambda i, h: (h, i, 0))],
        out_shape=[jax.ShapeDtypeStruct(do.shape, BF16), jax.ShapeDtypeStruct((HEADS, s_len, 1), F32)],
        compiler_params=_params(("parallel", "parallel")),
    )(do, ob)


def _tril(w):
    r = lax.broadcasted_iota(jnp.int32, w.shape, 0)
    c = lax.broadcasted_iota(jnp.int32, w.shape, 1)
    return jnp.where(c <= r, w, 0.0)


def _sgu_mix(w_ref, vln, gd):
    wcs = [_tril(w_ref[g]).astype(BF16) for g in range(SGU_GROUPS)]
    mixed = jnp.concatenate(
        [jnp.dot(wcs[g], vln[:, g * gd:(g + 1) * gd], preferred_element_type=F32) for g in range(SGU_GROUPS)], axis=1)
    return wcs, mixed


def _sgu_fwd(zpre, ln_g, ln_b, w_sp, bias_full):
    s_len, two_w = zpre.shape
    width = two_w // 2
    gd = width // SGU_GROUPS
    t = SGU_CHUNK

    def body(z_ref, g_ref, b_ref, w_ref, bias_ref, uv_ref):
        u = _gelu(z_ref[:, :width])
        v = _gelu(z_ref[:, width:])
        d = v - jnp.mean(v, axis=-1, keepdims=True)
        vhat = d * lax.rsqrt(jnp.mean(d * d, axis=-1, keepdims=True) + LN_EPS)
        vln = (vhat * g_ref[...] + b_ref[...]).astype(BF16)
        _, mixed = _sgu_mix(w_ref, vln, gd)
        uv_ref[...] = (u * (mixed + bias_ref[...])).astype(uv_ref.dtype)

    return pl.pallas_call(
        body, name="sgu_fwd", grid=(s_len // t,),
        in_specs=[pl.BlockSpec((t, two_w), lambda i: (i, 0)), pl.BlockSpec((1, width), lambda i: (0, 0)),
                  pl.BlockSpec((1, width), lambda i: (0, 0)), pl.BlockSpec(w_sp.shape, lambda i: (0, 0, 0)),
                  pl.BlockSpec((t, width), lambda i: (0, 0))],
        out_specs=pl.BlockSpec((t, width), lambda i: (i, 0)),
        out_shape=jax.ShapeDtypeStruct((s_len, width), BF16),
        compiler_params=_params(("parallel",)),
    )(zpre, ln_g, ln_b, w_sp, bias_full)


def _sgu_bwd(zpre, duv, ln_g, ln_b, w_sp, bias_full):
    s_len, two_w = zpre.shape
    width = two_w // 2
    gd = width // SGU_GROUPS
    t = SGU_CHUNK

    def body(z_ref, duv_ref, g_ref, b_ref, w_ref, bias_ref, dz_ref, dg_ref, db_ref, dw_ref, dbias_ref):
        first = pl.program_id(0) == 0

        def accumulate(ref, val):
            @pl.when(first)
            def _():
                ref[...] = val

            @pl.when(jnp.logical_not(first))
            def _():
                ref[...] += val

        zu, zv = z_ref[:, :width], z_ref[:, width:]
        u = _gelu(zu)
        v = _gelu(zv)
        d = v - jnp.mean(v, axis=-1, keepdims=True)
        rstd = lax.rsqrt(jnp.mean(d * d, axis=-1, keepdims=True) + LN_EPS)
        vhat = d * rstd
        vln = (vhat * g_ref[...] + b_ref[...]).astype(BF16)
        wcs, mixed = _sgu_mix(w_ref, vln, gd)
        duv_v = duv_ref[...]
        du = duv_v * (mixed + bias_ref[...])
        dmixed = duv_v * u
        dmb = dmixed.astype(BF16)
        dvln = jnp.concatenate(
            [lax.dot_general(wcs[g], dmb[:, g * gd:(g + 1) * gd], TN, preferred_element_type=F32)
             for g in range(SGU_GROUPS)], axis=1)
        for g in range(SGU_GROUPS):
            dw = lax.dot_general(dmb[:, g * gd:(g + 1) * gd], vln[:, g * gd:(g + 1) * gd], NT, preferred_element_type=F32)
            accumulate(dw_ref.at[g], _tril(dw))
        dvhat = dvln * g_ref[...]
        dv0 = rstd * (dvhat - jnp.mean(dvhat, axis=-1, keepdims=True)
                      - vhat * jnp.mean(dvhat * vhat, axis=-1, keepdims=True))
        dz_ref[:, :width] = (du * _gelu_grad(zu)).astype(dz_ref.dtype)
        dz_ref[:, width:] = (dv0 * _gelu_grad(zv)).astype(dz_ref.dtype)
        accumulate(dg_ref, jnp.sum(dvln * vhat, axis=0, keepdims=True))
        accumulate(db_ref, jnp.sum(dvln, axis=0, keepdims=True))
        accumulate(dbias_ref, dmixed)

    vec = pl.BlockSpec((1, width), lambda i: (0, 0))
    return pl.pallas_call(
        body, name="sgu_bwd", grid=(s_len // t,),
        in_specs=[pl.BlockSpec((t, two_w), lambda i: (i, 0)), pl.BlockSpec((t, width), lambda i: (i, 0)), vec, vec,
                  pl.BlockSpec(w_sp.shape, lambda i: (0, 0, 0)), pl.BlockSpec((t, width), lambda i: (0, 0))],
        out_specs=[pl.BlockSpec((t, two_w), lambda i: (i, 0)), vec, vec,
                   pl.BlockSpec(w_sp.shape, lambda i: (0, 0, 0)), pl.BlockSpec((t, width), lambda i: (0, 0))],
        out_shape=[jax.ShapeDtypeStruct((s_len, two_w), BF16), jax.ShapeDtypeStruct((1, width), F32),
                   jax.ShapeDtypeStruct((1, width), F32), jax.ShapeDtypeStruct(w_sp.shape, F32),
                   jax.ShapeDtypeStruct((t, width), F32)],
        compiler_params=_params(("arbitrary",)),
    )(zpre, duv, ln_g, ln_b, w_sp, bias_full)


def _rope_tables(positions):
    inv_freq = ROPE_THETA ** (-jnp.arange(0, ROPE, 2, dtype=F32) / ROPE)
    ang = positions.astype(F32)[:, None] * inv_freq
    cos, sin = jnp.cos(ang), jnp.sin(ang)
    z32, z64 = jnp.zeros_like(cos), jnp.zeros((cos.shape[0], LANE - ROPE), F32)
    return (jnp.concatenate([cos, cos, z64], axis=1), jnp.concatenate([z32, sin, z64], axis=1),
            jnp.concatenate([-sin, z32, z64], axis=1))


def _q_rope(name, q, tables, transpose):
    rot = _rope_t if transpose else _rope

    def fn(qv, cs, s1, s2):
        parts = []
        for h in range(HEADS):
            parts.append(qv[:, h * HEAD_PAD:h * HEAD_PAD + NOPE])
            parts.append(rot(qv[:, h * HEAD_PAD + NOPE:(h + 1) * HEAD_PAD], cs, s1, s2))
        return (jnp.concatenate(parts, axis=1),)

    return _rowwise(name, fn, [q, *tables], [], [(q.shape[1], BF16)])[0]


def _ffn_fwd(x, g, w_up3, w_down3):
    h2 = _rowwise("ffn_norm", lambda xv, gv: (_rms_fwd(xv, gv),), [x], [g], [(x.shape[1], BF16)])[0]

    def sq_relu(acc):
        r = jnp.maximum(acc, 0.0)
        return (r * r,)

    r = _mm_stacked("ffn_up", h2, w_up3, "col", (BF16,), sq_relu)[0]
    x_out = _mm_stacked("ffn_down", r, w_down3, "row", (F32,), lambda acc, res: (acc + res,), [x])[0]
    return x_out, (x, h2, r)


def _ffn_bwd(dx, dxb, saved, g, w_up3, w_down3):
    x, h2, r = saved
    da = _mm_stacked_nt("ffn_down_dx", dxb, w_down3, "row", (BF16,),
                        lambda acc, rv: (acc * (2.0 * jnp.sqrt(rv.astype(F32))),), [r])[0]
    g_down = _mm_tn_stacked("ffn_down_dw", r, dxb, w_down3.shape, "row")
    dh2 = _mm_stacked_nt("ffn_up_dx", da, w_up3, "col", (F32,))[0]
    g_up = _mm_tn_stacked("ffn_up_dw", h2, da, w_up3.shape, "col")
    dx, dxb, dg = _norm_bwd("ffn_norm_bwd", dh2, x, g, dx)
    return dx, dxb, dg, g_up, g_down


def _norm_bwd(name, dh, x, g, dres):
    def fn(dhv, xv, rv, gv):
        dxv, dg = _rms_bwd(dhv, xv, gv)
        return dxv + rv, dxv + rv, dg

    return _rowwise(name, fn, [dh, x, dres], [g], [(x.shape[1], F32), (x.shape[1], BF16)], [g.shape])


def _mla_fwd(x, g, wdkv, q_norm, kv_norm, wq, wkv, wo, tables):
    d = x.shape[1]
    h = _rowwise("mla_norm", lambda xv, gv: (_rms_fwd(xv, gv),), [x], [g], [(d, BF16)])[0]
    lat = _mm("mla_dkv", h, wdkv)[0]

    def lat_post(lv, cs, s1, s2, qg, kg):
        return (_rms_fwd(lv[:, :Q_RANK], qg), _rms_fwd(lv[:, Q_RANK:Q_RANK + KV_RANK], kg),
                _rope(lv[:, Q_RANK + KV_RANK:], cs, s1, s2))

    cqn, ckvn, krb = _rowwise("mla_lat", lat_post, [lat, *tables], [q_norm, kv_norm],
                              [(Q_RANK, BF16), (KV_RANK, BF16), (LANE, BF16)])
    q = _mm("mla_uq", cqn, wq)[0]
    kvb = _mm("mla_ukv", ckvn, wkv, (BF16,))[0]
    qb = _q_rope("mla_q_rope", q, tables, False)
    ob, lse = _flash_fwd(qb, kvb, krb)
    x_mid = _mm("mla_o", ob, wo, (F32,), lambda acc, res: (acc + res,), [x])[0]
    return x_mid, (x, h, lat, cqn, ckvn, krb, qb, kvb, ob, lse)


def _mla_bwd(dx, dxb, saved, g, wdkv, q_norm, kv_norm, wq, wkv, wo, tables):
    x, h, lat, cqn, ckvn, krb, qb, kvb, ob, lse = saved
    s_len = x.shape[0]
    do = _mm("mla_o_dx", dxb, wo, nt=True)[0]
    g_wo = _mm_tn("mla_o_dw", ob, dxb)[0]
    dob, delta = _att_delta(do, ob)
    dq = _flash_bwd_dq(qb, kvb, krb, dob, lse, delta)
    dkvb, dkr = _flash_bwd_dkv(qb, kvb, krb, dob, lse.reshape(HEADS, 1, s_len), delta.reshape(HEADS, 1, s_len))
    dqb = _q_rope("mla_q_rope_bwd", dq, tables, True)
    dcqn = _mm("mla_uq_dx", dqb, wq, nt=True)[0]
    g_wq = _mm_tn("mla_uq_dw", cqn, dqb)[0]
    dckvn = _mm("mla_ukv_dx", dkvb, wkv, nt=True)[0]
    g_wkv = _mm_tn("mla_ukv_dw", ckvn, dkvb)[0]

    def lat_bwd(dq_v, dkv_v, dkr_v, lv, cs, s1, s2, qg, kg):
        dcq, dqg = _rms_bwd(dq_v, lv[:, :Q_RANK], qg)
        dckv, dkg = _rms_bwd(dkv_v, lv[:, Q_RANK:Q_RANK + KV_RANK], kg)
        dkr_sum = dkr_v[:, :LANE]
        for hd in range(1, HEADS):
            dkr_sum = dkr_sum + dkr_v[:, hd * LANE:(hd + 1) * LANE]
        return jnp.concatenate([dcq, dckv, _rope_t(dkr_sum, cs, s1, s2)], axis=1), dqg, dkg

    dlat, g_qn, g_kvn = _rowwise("mla_lat_bwd", lat_bwd, [dcqn, dckvn, dkr, lat, *tables], [q_norm, kv_norm],
                                 [(LAT_PAD, BF16)], [q_norm.shape, kv_norm.shape])
    dh = _mm("mla_dkv_dx", dlat, wdkv, nt=True)[0]
    g_wdkv = _mm_tn("mla_dkv_dw", h, dlat)[0]
    dx, dxb, dg = _norm_bwd("mla_norm_bwd", dh, x, g, dx)
    return dx, dxb, dg, g_wdkv, g_qn, g_kvn, g_wq, g_wkv, g_wo


def _sgu_layer_fwd(x, g, w_in3, ln_g, ln_b, w_sp, bias_full, w_out3):
    h = _rowwise("sgu_norm", lambda xv, gv: (_rms_fwd(xv, gv),), [x], [g], [(x.shape[1], BF16)])[0]
    zpre = _mm_stacked("sgu_in", h, w_in3, "col")[0]
    uv = _sgu_fwd(zpre, ln_g, ln_b, w_sp, bias_full)
    x_mid = _mm_stacked("sgu_out", uv, w_out3, "row", (F32,), lambda acc, res: (acc + res,), [x])[0]
    return x_mid, (x, h, zpre, uv)


def _sgu_layer_bwd(dx, dxb, saved, g, w_in3, ln_g, ln_b, w_sp, bias_full, w_out3):
    x, h, zpre, uv = saved
    duv = _mm_stacked_nt("sgu_out_dx", dxb, w_out3, "row")[0]
    g_out = _mm_tn_stacked("sgu_out_dw", uv, dxb, w_out3.shape, "row")
    dz, g_lng, g_lnb, g_wsp, g_bias = _sgu_bwd(zpre, duv, ln_g, ln_b, w_sp, bias_full)
    dh = _mm_stacked_nt("sgu_in_dx", dz, w_in3, "col")[0]
    g_in = _mm_tn_stacked("sgu_in_dw", h, dz, w_in3.shape, "col")
    dx, dxb, dg = _norm_bwd("sgu_norm_bwd", dh, x, g, dx)
    return dx, dxb, dg, g_in, g_out, g_lng, g_lnb, g_wsp, g_bias


def _loss_head(x, target, g):
    d = x.shape[1]

    def fn(xv, tv, gv):
        err = _rms_fwd(xv, gv) - tv
        dxv, dg = _rms_bwd(err * (1.0 / d), xv, gv)
        return dxv, dxv, dg, jnp.sum(err * err, axis=0, keepdims=True)

    return _rowwise("loss_head", fn, [x, target], [g], [(d, F32), (d, BF16)], [g.shape, g.shape])


def _local_step(x, positions, target, norm_mix, norm_ffn, final_norm, q_norm, kv_norm, w_sp, b_sp, mla, sgu, ffn, reducer):
    tables = _rope_tables(positions)
    width = sgu[0][1].shape[-1]
    gd = width // SGU_GROUPS
    bias_full = [jnp.repeat(b_sp[j].T, gd, axis=1) for j in range(DEPTH // 2)]
    saved = []
    for i in range(DEPTH):
        j = i // 2
        if i % 2 == 0:
            wdkv, wq, wkv, wo = mla[j]
            x, s_mix = _mla_fwd(x, norm_mix[i:i + 1], wdkv, q_norm[j:j + 1], kv_norm[j:j + 1], wq, wkv, wo, tables)
        else:
            w_in3, ln_g, ln_b, w_out3 = sgu[j]
            x, s_mix = _sgu_layer_fwd(x, norm_mix[i:i + 1], w_in3, ln_g, ln_b, w_sp[j], bias_full[j], w_out3)
        x, s_ffn = _ffn_fwd(x, norm_ffn[i:i + 1], *ffn[i])
        saved.append((s_mix, s_ffn))
    dx, dxb, g_final, sq_cols = _loss_head(x, target, final_norm[None, :])
    loss = 0.5 * jnp.sum(sq_cols) / x.shape[1]

    def pair(g):
        return g, g.astype(BF16)

    g_mix, g_ffn = [None] * DEPTH, [None] * DEPTH
    mla_g, sgu_g = [None] * (DEPTH // 2), [None] * (DEPTH // 2)
    for i in reversed(range(DEPTH)):
        j = i // 2
        s_mix, s_ffn = saved[i]
        dx, dxb, g_ffn[i], g_up, g_down = _ffn_bwd(dx, dxb, s_ffn, norm_ffn[i:i + 1], *ffn[i])
        reducer.add(f"ffn{i}", i, {"ffn_w_up": g_up, "ffn_w_down": g_down})
        dxb = reducer.phase_end(dxb)
        if i % 2 == 0:
            wdkv, wq, wkv, wo = mla[j]
            dx, dxb, g_mix[i], g_wdkv, g_qn, g_kvn, g_wq, g_wkv, g_wo = _mla_bwd(
                dx, dxb, s_mix, norm_mix[i:i + 1], wdkv, q_norm[j:j + 1], kv_norm[j:j + 1], wq, wkv, wo, tables)
            mla_g[j] = (g_qn, g_kvn)
            g_wq = g_wq.reshape(Q_RANK, HEADS, HEAD_PAD)[..., :QK_HEAD].reshape(Q_RANK, N_SHARDS, -1)
            reducer.add(f"mla{j}", j, {
                "mla_w_dkv": pair(g_wdkv[:, :Q_RANK + KV_RANK + ROPE].reshape(N_SHARDS, -1, Q_RANK + KV_RANK + ROPE)),
                "mla_w_uq": pair(g_wq.transpose(1, 0, 2)),
                "mla_w_ukv": pair(g_wkv.reshape(KV_RANK, N_SHARDS, -1).transpose(1, 0, 2)),
                "mla_w_o": pair(g_wo.reshape(N_SHARDS, -1, g_wo.shape[1]))})
        else:
            w_in3, ln_g, ln_b, w_out3 = sgu[j]
            dx, dxb, g_mix[i], g_in, g_out, g_lng, g_lnb, g_wsp, g_bias = _sgu_layer_bwd(
                dx, dxb, s_mix, norm_mix[i:i + 1], w_in3, ln_g, ln_b, w_sp[j], bias_full[j], w_out3)
            sgu_g[j] = (g_wsp, g_bias.reshape(SGU_CHUNK, SGU_GROUPS, gd).sum(axis=-1).T)
            reducer.add(f"sgu{j}", j, {"sgu_w_in": g_in, "sgu_w_out": g_out,
                                       "sgu_ln_g": pair(g_lng.reshape(N_SHARDS, -1, LANE)),
                                       "sgu_ln_b": pair(g_lnb.reshape(N_SHARDS, -1, LANE))})
        dxb = reducer.phase_end(dxb)
    small = dict(
        norm_mix=jnp.concatenate(g_mix, axis=0), norm_ffn=jnp.concatenate(g_ffn, axis=0), final_norm=g_final[0],
        q_norm=jnp.concatenate([m[0] for m in mla_g], axis=0), kv_norm=jnp.concatenate([m[1] for m in mla_g], axis=0),
        w_sp=jnp.stack([s[0] for s in sgu_g]), b_sp=jnp.stack([s[1] for s in sgu_g]))
    return loss, dx, small


HBM_SPEC = pl.BlockSpec(memory_space=pltpu.HBM)


def _place():
    x, y, c = lax.axis_index("x"), lax.axis_index("y"), lax.axis_index("c")
    return x, y, c, [(1 - x, y), (x, 1 - y), (1 - x, 1 - y)]


def _remote(src, dst, send_sems, recv_sems, k, to):
    return pltpu.make_async_remote_copy(src_ref=src, dst_ref=dst, send_sem=send_sems.at[k], recv_sem=recv_sems.at[k],
                                        device_id=to, device_id_type=MESH)


def _gather_layer(layer, shards):
    n = len(shards)
    split = [s.shape[0] >= 16 for s in shards]

    def body(*refs):
        ins, outs = refs[:n], refs[n:2 * n]
        send_sems, recv_sems, local_sems = refs[2 * n:]
        x, y, c, chips = _place()
        mine = 2 * x + y
        barrier = pltpu.get_barrier_semaphore()
        peers = [(x, y, 1 - c)] + [(*chip, c) for chip in chips]
        for peer in peers:
            pl.semaphore_signal(barrier, inc=1, device_id=peer, device_id_type=MESH)
        pl.semaphore_wait(barrier, len(peers))

        def rows(t, half):
            hr = shards[t].shape[0] // 2
            return pl.ds(half * hr, hr) if split[t] else pl.ds(0, shards[t].shape[0])

        local, sent = [], []
        for t in range(n):
            local.append(pltpu.make_async_copy(ins[t], outs[t].at[mine], local_sems.at[t]))
            local[-1].start()
            for j, chip in enumerate(chips):
                cp = _remote(ins[t].at[rows(t, c)], outs[t].at[mine, rows(t, c)], send_sems, recv_sems, 3 * t + j, (*chip, c))
                cp.start()
                sent.append(cp)
        for j, chip in enumerate(chips):
            theirs = 2 * chip[0] + chip[1]
            for t in range(n):
                piece = outs[t].at[theirs, rows(t, c)]
                _remote(piece, piece, send_sems, recv_sems, 3 * t + j, (x, y, c)).wait_recv()
                if split[t]:
                    cp = _remote(piece, piece, send_sems, recv_sems, 3 * n + 3 * t + j, (x, y, 1 - c))
                    cp.start()
                    sent.append(cp)
        for j, chip in enumerate(chips):
            theirs = 2 * chip[0] + chip[1]
            for t in range(n):
                if split[t]:
                    piece = outs[t].at[theirs, rows(t, 1 - c)]
                    _remote(piece, piece, send_sems, recv_sems, 3 * n + 3 * t + j, (x, y, c)).wait_recv()
        for cp in sent:
            cp.wait_send()
        for cp in local:
            cp.wait()

    return pl.kernel(
        body, name=f"gather_layer{layer}", mesh=plsc.ScalarSubcoreMesh(axis_name="sequencer", num_cores=1),
        out_type=[jax.ShapeDtypeStruct((N_SHARDS, *s.shape), s.dtype) for s in shards],
        scratch_types=[pltpu.SemaphoreType.DMA((6 * n,)), pltpu.SemaphoreType.DMA((6 * n,)), pltpu.SemaphoreType.DMA((n,))],
        compiler_params=pltpu.CompilerParams(collective_id=ID_GATHER),
    )(*shards)


SEQUENCER = dict(axis_name="sequencer", num_cores=1)
ID_GATHER, ID_EXCHANGE, ID_SHARE = 0, 1, 2
MIN_SPLIT_ROWS = 16


def _handshake(peers):
    barrier = pltpu.get_barrier_semaphore()
    for peer in peers:
        pl.semaphore_signal(barrier, inc=1, device_id=peer, device_id_type=MESH)
    pl.semaphore_wait(barrier, len(peers))


def _half_rows(rows, half):
    return pl.ds(half * (rows // 2), rows // 2) if rows >= MIN_SPLIT_ROWS else pl.ds(0, rows)


def _exchange_partials(tag, stacks):
    n = len(stacks)

    def body(*refs):
        ins, outs, send_sems, recv_sems = refs[:n], refs[n:2 * n], refs[2 * n], refs[2 * n + 1]
        x, y, c, chips = _place()
        mine = 2 * x + y
        _handshake([(x, y, 1 - c)] + [(*chip, c) for chip in chips] + [(*chip, 1 - c) for chip in chips])
        sent = []
        for t in range(n):
            r = stacks[t].shape[1]
            sent.append(_remote(ins[t].at[mine, _half_rows(r, 1 - c)], outs[t].at[0], send_sems, recv_sems, 7 * t, (x, y, 1 - c)))
            for j, chip in enumerate(chips):
                theirs = 2 * chip[0] + chip[1]
                sent.append(_remote(ins[t].at[theirs, _half_rows(r, c)], outs[t].at[1 + j], send_sems, recv_sems,
                                    7 * t + 1 + j, (*chip, c)))
                sent.append(_remote(ins[t].at[theirs, _half_rows(r, 1 - c)], outs[t].at[4 + j], send_sems, recv_sems,
                                    7 * t + 4 + j, (*chip, 1 - c)))
        for cp in sent:
            cp.start()
        for cp in sent:
            cp.wait_send()
        for t in range(n):
            for k in range(7):
                _remote(outs[t].at[k], outs[t].at[k], send_sems, recv_sems, 7 * t + k, (x, y, c)).wait_recv()

    def landing(s):
        return (7, s.shape[1] // 2 if s.shape[1] >= MIN_SPLIT_ROWS else s.shape[1], s.shape[2])

    return pl.kernel(
        body, name=f"reduce_exchange_{tag}", mesh=plsc.ScalarSubcoreMesh(**SEQUENCER),
        out_type=[jax.ShapeDtypeStruct(landing(s), s.dtype) for s in stacks],
        scratch_types=[pltpu.SemaphoreType.DMA((7 * n,)), pltpu.SemaphoreType.DMA((7 * n,))],
        compiler_params=pltpu.CompilerParams(collective_id=ID_EXCHANGE),
    )(*stacks)


def _share_halves(tag, halves):
    n = len(halves)

    def body(*refs):
        ins, outs, send_sems, recv_sems = refs[:n], refs[n:2 * n], refs[2 * n], refs[2 * n + 1]
        x, y, c, _ = _place()
        _handshake([(x, y, 1 - c)])
        sent = [_remote(ins[t], outs[t], send_sems, recv_sems, t, (x, y, 1 - c)) for t in range(n)]
        for cp in sent:
            cp.start()
        for cp in sent:
            cp.wait()

    return pl.kernel(
        body, name=f"reduce_share_{tag}", mesh=plsc.ScalarSubcoreMesh(**SEQUENCER),
        out_type=[jax.ShapeDtypeStruct(h.shape, h.dtype) for h in halves],
        scratch_types=[pltpu.SemaphoreType.DMA((n,)), pltpu.SemaphoreType.DMA((n,))],
        compiler_params=pltpu.CompilerParams(collective_id=ID_SHARE),
    )(*halves)


def _all_reduce_small(part):
    rows = part.shape[0]

    def body(p_ref, out_ref, sib_buf, chip_sums, send_sems, recv_sems):
        x, y, c, chips = _place()
        mine = 2 * x + y
        swap = _remote(p_ref, sib_buf, send_sems, recv_sems, 0, (x, y, 1 - c))
        swap.start()
        swap.wait()
        chip_sums[mine] = p_ref[...] + sib_buf[...]
        sent = [_remote(chip_sums.at[mine], chip_sums.at[mine], send_sems, recv_sems, 1 + j, (*chip, c))
                for j, chip in enumerate(chips)]
        for cp in sent:
            cp.start()
        for j, chip in enumerate(chips):
            sent[j].wait_send()
            theirs = chip_sums.at[2 * chip[0] + chip[1]]
            _remote(theirs, theirs, send_sems, recv_sems, 1 + j, (x, y, c)).wait_recv()
        out_ref[...] = ((chip_sums[0] + chip_sums[1]) + chip_sums[2]) + chip_sums[3]

    vmem = pl.BlockSpec(memory_space=pltpu.VMEM)
    return pl.pallas_call(
        body, name="all_reduce_small", in_specs=[vmem], out_specs=vmem, out_shape=jax.ShapeDtypeStruct(part.shape, F32),
        scratch_shapes=[pltpu.VMEM((rows, LANE), F32), pltpu.VMEM((N_SHARDS, rows, LANE), F32),
                        pltpu.SemaphoreType.DMA((4,)), pltpu.SemaphoreType.DMA((4,))],
        compiler_params=pltpu.CompilerParams(vmem_limit_bytes=VMEM_LIMIT_BYTES),
    )(part)


def _sum_partials(g3, others, sel):
    _, rows, c = others.shape
    whole = g3.shape[1] == rows
    tr = _tile(rows, 512)
    nb = rows // tr

    def body(sel_ref, g_ref, *rest):
        same = g_ref[...].astype(F32)
        for ref in rest[1:4]:
            same = same + ref[...].astype(F32)
        other = rest[0][...].astype(F32)
        for ref in rest[4:7]:
            other = other + ref[...].astype(F32)
        rest[7][...] = same + other

    blk = (None, tr, c)
    slots = [pl.BlockSpec(blk, functools.partial(lambda i, sr, k: (k, i, 0), k=k)) for k in range(7)]
    return pl.pallas_call(
        body, name="reduce_sum_partials",
        grid_spec=pltpu.PrefetchScalarGridSpec(
            num_scalar_prefetch=1, grid=(nb,),
            in_specs=[pl.BlockSpec(blk, lambda i, sr: (sr[0], (0 if whole else sr[1] * nb) + i, 0))] + slots,
            out_specs=pl.BlockSpec((tr, c), lambda i, sr: (i, 0))),
        out_shape=jax.ShapeDtypeStruct((rows, c), F32),
        compiler_params=_params(("parallel",)),
    )(sel, g3, *[others] * 7)


def _adamw_math(w, g, m, v):
    nm = ADAM_B1 * m + (1.0 - ADAM_B1) * g
    nv = ADAM_B2 * v + (1.0 - ADAM_B2) * (g * g)
    m_hat = nm / (1.0 - ADAM_B1 ** ADAM_STEP)
    v_hat = nv / (1.0 - ADAM_B2 ** ADAM_STEP)
    return -ADAM_LR * (m_hat / (jnp.sqrt(v_hat) + ADAM_EPS) + ADAM_WD * w), nm, nv


def _adamw_layer(layer, w, m, v, g_mine, g_sibling, sel, prev):
    lyr, r, c = w.shape
    rows = g_mine.shape[0]
    halves = r // rows
    tr = _tile(rows, 512)
    nb = rows // tr
    n_g = 1 if g_sibling is None else 2

    def body(sel_ref, w_ref, m_ref, v_ref, *rest):
        g = rest[0][...]
        if n_g == 2:
            g = jnp.where(pl.program_id(0) == sel_ref[1], g, rest[1][...])
        outs = rest[n_g + (0 if prev is None else 4):]
        d, nm, nv = _adamw_math(w_ref[...], g, m_ref[...], v_ref[...])
        for ref, val in zip(outs, (g, d, nm, nv)):
            ref[...] = val

    full = pl.BlockSpec((None, tr, c), lambda h, i, sr: (layer, h * nb + i, 0))
    part = pl.BlockSpec((tr, c), lambda h, i, sr: (i, 0))
    n_in = 4 + n_g
    return pl.pallas_call(
        body, name="adamw_layer",
        grid_spec=pltpu.PrefetchScalarGridSpec(
            num_scalar_prefetch=1, grid=(halves, nb),
            in_specs=[full] * 3 + [part] * n_g + ([] if prev is None else [pl.BlockSpec(memory_space=pl.ANY)] * 4),
            out_specs=[full] * 4),
        out_shape=[jax.ShapeDtypeStruct(w.shape, F32)] * 4,
        input_output_aliases={} if prev is None else {n_in + k: k for k in range(4)},
        compiler_params=_params(("parallel", "parallel")),
    )(sel, w, m, v, g_mine, *([] if g_sibling is None else [g_sibling]), *([] if prev is None else prev))


def _adamw(w, g, m, v):
    lyr, r, c = w.shape
    tr = _tile(r, 512)

    def body(w_ref, g_ref, m_ref, v_ref, d_ref, nm_ref, nv_ref):
        gv = g_ref[...]
        nm = ADAM_B1 * m_ref[...] + (1.0 - ADAM_B1) * gv
        nv = ADAM_B2 * v_ref[...] + (1.0 - ADAM_B2) * (gv * gv)
        m_hat = nm / (1.0 - ADAM_B1 ** ADAM_STEP)
        v_hat = nv / (1.0 - ADAM_B2 ** ADAM_STEP)
        d_ref[...] = -ADAM_LR * (m_hat / (jnp.sqrt(v_hat) + ADAM_EPS) + ADAM_WD * w_ref[...])
        nm_ref[...] = nm
        nv_ref[...] = nv

    blk = pl.BlockSpec((None, tr, c), lambda l, i: (l, i, 0))
    return pl.pallas_call(
        body, name="adamw", grid=(lyr, r // tr), in_specs=[blk] * 4, out_specs=[blk] * 3,
        out_shape=[jax.ShapeDtypeStruct(w.shape, F32)] * 3,
        compiler_params=_params(("parallel", "parallel")),
    )(w, g, m, v)


SHARDED = ("mla_w_dkv", "mla_w_uq", "mla_w_ukv", "mla_w_o", "sgu_w_in", "sgu_ln_g", "sgu_ln_b", "sgu_w_out",
           "ffn_w_up", "ffn_w_down")
REPLICATED = ("norm_mix", "norm_ffn", "final_norm", "mla_q_norm", "mla_kv_norm", "sgu_w_spatial", "sgu_b_spatial")
WEIGHTS = ("norm_mix", "norm_ffn", "final_norm", "mla_w_dkv", "mla_q_norm", "mla_kv_norm", "mla_w_uq", "mla_w_ukv",
           "mla_w_o", "sgu_w_in", "sgu_ln_g", "sgu_ln_b", "sgu_w_spatial", "sgu_b_spatial", "sgu_w_out", "ffn_w_up",
           "ffn_w_down")


class _Reducer:
    def __init__(self, state, sel):
        self.state, self.sel = state, sel
        self.started, self.travelling, self.summed = [], [], []
        self.done = {}

    def add(self, tag, layer, grads):
        names = list(grads)
        received = _exchange_partials(tag, [grads[n][1] for n in names])
        own = [grads[n][0] if grads[n][0].shape[1] >= MIN_SPLIT_ROWS else grads[n][1] for n in names]
        self.started.append((tag, layer, names, own, received))

    def phase_end(self, token):
        for tag, layer, names, own, received in self.travelling:
            token, *received = lax.optimization_barrier((token, *received))
            mine = [_sum_partials(g, got, self.sel) for g, got in zip(own, received)]
            cut = [k for k, g in enumerate(own) if g.shape[1] >= MIN_SPLIT_ROWS]
            theirs = dict(zip(cut, _share_halves(tag, [mine[k] for k in cut])))
            self.summed.append((layer, names, mine, [theirs.get(k) for k in range(len(names))]))
            token = lax.optimization_barrier((token, *mine))[0]
        self.travelling, self.started = self.started, []
        return token

    def update(self, token):
        for layer, names, mine, theirs in self.summed:
            for name, g_mine, g_theirs in zip(names, mine, theirs):
                w, m, v = self.state[name]
                self.done[name] = _adamw_layer(layer, w, m, v, g_mine, g_theirs, self.sel, self.done.get(name))
                token = self.done[name][1]
        self.summed = []
        return token


def _as3d(name, a):
    return a.reshape(a.shape[0], -1, LANE) if name in ("sgu_ln_g", "sgu_ln_b") else a


def _pack(parts):
    flat = jnp.concatenate([p.reshape(-1) for p in parts])
    rows = -(-flat.shape[0] // (8 * LANE)) * 8
    return jnp.pad(flat, (0, rows * LANE - flat.shape[0])).reshape(rows, LANE)


def _unpack(packed, like):
    flat, out, at = packed.reshape(-1), [], 0
    for p in like:
        out.append(flat[at:at + p.size].reshape(p.shape))
        at += p.size
    return out


def kernel(x, positions, norm_mix, norm_ffn, final_norm, mla_w_dkv, mla_q_norm, mla_kv_norm, mla_w_uq, mla_w_ukv, mla_w_o, sgu_w_in, sgu_ln_g, sgu_ln_b, sgu_w_spatial, sgu_b_spatial, sgu_w_out, ffn_w_up, ffn_w_down, loss_target, m_norm_mix, m_norm_ffn, m_final_norm, m_mla_w_dkv, m_mla_q_norm, m_mla_kv_norm, m_mla_w_uq, m_mla_w_ukv, m_mla_w_o, m_sgu_w_in, m_sgu_ln_g, m_sgu_ln_b, m_sgu_w_spatial, m_sgu_b_spatial, m_sgu_w_out, m_ffn_w_up, m_ffn_w_down, v_norm_mix, v_norm_ffn, v_final_norm, v_mla_w_dkv, v_mla_q_norm, v_mla_kv_norm, v_mla_w_uq, v_mla_w_ukv, v_mla_w_o, v_sgu_w_in, v_sgu_ln_g, v_sgu_ln_b, v_sgu_w_spatial, v_sgu_b_spatial, v_sgu_w_out, v_ffn_w_up, v_ffn_w_down):
    given = dict(locals())
    w = {n: given[n] for n in WEIGHTS}
    mom = {n: given["m_" + n] for n in WEIGHTS}
    var = {n: given["v_" + n] for n in WEIGHTS}
    n_mla = mla_w_dkv.shape[0]

    by_rows = lambda a: a.reshape(N_SHARDS * a.shape[1], a.shape[2])
    by_cols = lambda a: a.transpose(1, 0, 2).reshape(a.shape[1], N_SHARDS * a.shape[2])
    mla, sgu, ffn = [], [], []
    for i in range(DEPTH):
        j = i // 2
        names = ("mla_w_dkv", "mla_w_uq", "mla_w_ukv", "mla_w_o") if i % 2 == 0 else ("sgu_w_in", "sgu_w_out")
        shards = [w[n][j].astype(BF16) for n in names] + [ffn_w_up[i].astype(BF16), ffn_w_down[i].astype(BF16)]
        if i % 2:
            shards += [sgu_ln_g[j].reshape(-1, LANE), sgu_ln_b[j].reshape(-1, LANE)]
        full = _gather_layer(i, shards)
        if i % 2 == 0:
            wdkv = jnp.pad(by_rows(full[0]), ((0, 0), (0, LAT_PAD - mla_w_dkv.shape[-1])))
            wq = jnp.pad(by_cols(full[1]).reshape(Q_RANK, HEADS, QK_HEAD), ((0, 0), (0, 0), (0, HEAD_PAD - QK_HEAD)))
            mla.append((wdkv, wq.reshape(Q_RANK, HEADS * HEAD_PAD), by_cols(full[2]), by_rows(full[3])))
            ffn.append((full[4], full[5]))
        else:
            sgu.append((full[0], full[4].reshape(1, -1), full[5].reshape(1, -1), full[1]))
            ffn.append((full[2], full[3]))

    x_i, y_i, c_i = lax.axis_index("x"), lax.axis_index("y"), lax.axis_index("c")
    sel = jnp.stack([2 * x_i + y_i, c_i]).astype(jnp.int32)
    reducer = _Reducer({n: tuple(_as3d(n, d[n]) for d in (w, mom, var)) for n in SHARDED}, sel)
    loss, dx, small = _local_step(
        x[0], positions[0], loss_target[0], norm_mix, norm_ffn, final_norm, mla_q_norm, mla_kv_norm, sgu_w_spatial,
        sgu_b_spatial, mla, sgu, ffn, reducer)
    loss = lax.psum(loss, ("x", "y", "c"))

    small_g = [small["norm_mix"], small["norm_ffn"], small["final_norm"], small["q_norm"], small["kv_norm"],
               small["w_sp"], small["b_sp"]]
    like = [w[n] for n in REPLICATED]
    g_small = _all_reduce_small(_pack(small_g))
    packed = [_pack([d[n] for n in REPLICATED])[None] for d in (w, mom, var)]
    upd_small = _adamw(packed[0], g_small[None], packed[1], packed[2])
    grads = dict(zip(REPLICATED, _unpack(g_small, like)))
    delta, new_m, new_v = ({n: a for n, a in zip(REPLICATED, _unpack(u[0], like))} for u in upd_small)

    reducer.phase_end(reducer.update(upd_small[0]))
    reducer.update(None)
    for n in SHARDED:
        grads[n], delta[n], new_m[n], new_v[n] = (a.reshape(w[n].shape) for a in reducer.done[n])

    return (loss, dx[None], *[grads[n] for n in WEIGHTS], *[delta[n] for n in WEIGHTS],
            *[new_m[n] for n in WEIGHTS], *[new_v[n] for n in WEIGHTS])
```

```python
import functools
import math

import jax
import jax.numpy as jnp
from jax import lax
from jax.experimental import pallas as pl
from jax.experimental.pallas import tpu as pltpu
from jax.experimental.pallas import tpu_sc as plsc

F32 = jnp.float32
BF16 = jnp.bfloat16
MESH = pl.DeviceIdType.MESH

DEPTH = 4
HEADS = 8
NOPE = 128
ROPE = 64
VHEAD = 128
QK_HEAD = NOPE + ROPE
Q_RANK = 256
KV_RANK = 128
HEAD_PAD = 256
LAT_PAD = 512
ROPE_THETA = 10000.0
SGU_CHUNK = 128
SGU_GROUPS = 8
NORM_EPS = 1e-6
LN_EPS = 1e-5
ADAM_LR, ADAM_B1, ADAM_B2, ADAM_EPS, ADAM_WD, ADAM_STEP = 0.001, 0.9, 0.999, 1e-08, 0.01, 10

N_SHARDS = 4
LANE = 128
VMEM_LIMIT_BYTES = 56 * 1024 * 1024
ATT_TILE = 512
MM_TILE = 1024
ATT_SCALE = QK_HEAD ** -0.5
LOG2_SCALE = ATT_SCALE * math.log2(math.e)

NN = (((1,), (0,)), ((), ()))
NT = (((1,), (1,)), ((), ()))
TN = (((0,), (0,)), ((), ()))


def _params(sem):
    return pltpu.CompilerParams(dimension_semantics=sem, vmem_limit_bytes=VMEM_LIMIT_BYTES)


def _tile(n, pref):
    t = min(n, pref)
    while n % t:
        t //= 2
    return t


def _matmul(name, a, b, a_spec, b_spec, dims, grid, tile, outs, extras=(), epilogue=None, aliased=()):
    nk, ne, no = grid[2], len(extras), len(outs)

    def body(a_ref, b_ref, *rest):
        e_refs, o_refs = rest[:ne], rest[ne + len(aliased):ne + len(aliased) + no]
        part = lax.dot_general(a_ref[...].astype(BF16), b_ref[...].astype(BF16), dims, preferred_element_type=F32)

        def finish(acc):
            vals = (acc,) if epilogue is None else epilogue(acc, *[e[...] for e in e_refs])
            for o_ref, v in zip(o_refs, vals):
                o_ref[...] = v.astype(o_ref.dtype)

        if nk == 1:
            finish(part)
            return
        acc_ref, k = rest[-1], pl.program_id(2)

        @pl.when(k == 0)
        def _():
            acc_ref[...] = part

        @pl.when(jnp.logical_and(k > 0, k < nk - 1))
        def _():
            acc_ref[...] += part

        @pl.when(k == nk - 1)
        def _():
            finish(acc_ref[...] + part)

    n_in = 2 + ne
    return pl.pallas_call(
        body, name=name, grid=grid,
        in_specs=[a_spec, b_spec] + [s for _, s in extras] + [pl.BlockSpec(memory_space=pl.ANY)] * len(aliased),
        out_specs=[s for _, s in outs], out_shape=[s for s, _ in outs],
        scratch_shapes=[pltpu.VMEM(tile, F32)] if nk > 1 else [],
        input_output_aliases={n_in + i: o for i, (_, o) in enumerate(aliased)},
        compiler_params=_params(("parallel", "parallel", "arbitrary")),
    )(a, b, *[e for e, _ in extras], *[arr for arr, _ in aliased])


def _mm(name, a, b, out_dtypes=(F32,), epilogue=None, extras=(), tm=MM_TILE, tn=MM_TILE, tk=MM_TILE, nt=False):
    m, kd = a.shape
    n = b.shape[0] if nt else b.shape[1]
    tm, tn, tk = _tile(m, tm), _tile(n, tn), _tile(kd, tk)
    o_spec = pl.BlockSpec((tm, tn), lambda i, j, k: (i, j))
    b_spec = pl.BlockSpec((tn, tk), lambda i, j, k: (j, k)) if nt else pl.BlockSpec((tk, tn), lambda i, j, k: (k, j))
    return _matmul(name, a, b, pl.BlockSpec((tm, tk), lambda i, j, k: (i, k)), b_spec, NT if nt else NN,
                   (m // tm, n // tn, kd // tk), (tm, tn),
                   [(jax.ShapeDtypeStruct((m, n), d), o_spec) for d in out_dtypes],
                   [(e, o_spec) for e in extras], epilogue)


def _mm_tn(name, a, b, out_dtypes=(F32,), tm=MM_TILE, tn=MM_TILE, tk=MM_TILE):
    s, m = a.shape
    n = b.shape[1]
    tm, tn, tk = _tile(m, tm), _tile(n, tn), _tile(s, tk)
    o_spec = pl.BlockSpec((tm, tn), lambda i, j, k: (i, j))
    return _matmul(name, a, b, pl.BlockSpec((tk, tm), lambda i, j, k: (k, i)),
                   pl.BlockSpec((tk, tn), lambda i, j, k: (k, j)), TN, (m // tm, n // tn, s // tk), (tm, tn),
                   [(jax.ShapeDtypeStruct((m, n), d), o_spec) for d in out_dtypes])


def _mm_stacked(name, a, w3, mode, out_dtypes=(F32,), epilogue=None, extras=(), tm=MM_TILE, tn=MM_TILE, tk=MM_TILE):
    m, kd = a.shape
    _, r, c = w3.shape
    n = c if mode == "row" else N_SHARDS * c
    tm = _tile(m, tm)
    if mode == "row":
        tn, tk = _tile(n, tn), _tile(r, tk)
        per = r // tk
        b_spec = pl.BlockSpec((None, tk, tn), lambda i, j, k: (k // per, k % per, j))
    else:
        tn, tk = _tile(c, tn), _tile(kd, tk)
        per = c // tn
        b_spec = pl.BlockSpec((None, tk, tn), lambda i, j, k: (j // per, k, j % per))
    o_spec = pl.BlockSpec((tm, tn), lambda i, j, k: (i, j))
    return _matmul(name, a, w3, pl.BlockSpec((tm, tk), lambda i, j, k: (i, k)), b_spec, NN,
                   (m // tm, n // tn, kd // tk), (tm, tn),
                   [(jax.ShapeDtypeStruct((m, n), d), o_spec) for d in out_dtypes],
                   [(e, o_spec) for e in extras], epilogue)


def _mm_stacked_nt(name, a, w3, mode, out_dtypes=(F32,), epilogue=None, extras=(), tm=MM_TILE, tn=MM_TILE, tk=MM_TILE):
    m, nd = a.shape
    _, r, c = w3.shape
    kout = N_SHARDS * r if mode == "row" else r
    tm = _tile(m, tm)
    if mode == "row":
        tn, tk = _tile(r, tn), _tile(c, tk)
        per = r // tn
        b_spec = pl.BlockSpec((None, tn, tk), lambda i, j, k: (j // per, j % per, k))
    else:
        tn, tk = _tile(r, tn), _tile(c, tk)
        per = c // tk
        b_spec = pl.BlockSpec((None, tn, tk), lambda i, j, k: (k // per, j, k % per))
    o_spec = pl.BlockSpec((tm, tn), lambda i, j, k: (i, j))
    return _matmul(name, a, w3, pl.BlockSpec((tm, tk), lambda i, j, k: (i, k)), b_spec, NT,
                   (m // tm, kout // tn, nd // tk), (tm, tn),
                   [(jax.ShapeDtypeStruct((m, kout), d), o_spec) for d in out_dtypes],
                   [(e, o_spec) for e in extras], epilogue)


def _mm_tn_stacked(name, a, b, shape3, mode, tm=MM_TILE, tn=MM_TILE, tk=MM_TILE):
    s, m = a.shape
    n = b.shape[1]
    _, r, c = shape3
    tk = _tile(s, tk)
    if mode == "row":
        tm, tn = _tile(r, tm), _tile(n, tn)
        per = r // tm
        o_spec = pl.BlockSpec((None, tm, tn), lambda i, j, k: (i // per, i % per, j))
    else:
        tm, tn = _tile(m, tm), _tile(c, tn)
        per = c // tn
        o_spec = pl.BlockSpec((None, tm, tn), lambda i, j, k: (j // per, i, j % per))
    outs = [(jax.ShapeDtypeStruct(shape3, F32), o_spec), (jax.ShapeDtypeStruct(shape3, BF16), o_spec)]
    return _matmul(name, a, b, pl.BlockSpec((tk, tm), lambda i, j, k: (k, i)),
                   pl.BlockSpec((tk, tn), lambda i, j, k: (k, j)), TN, (m // tm, n // tn, s // tk), (tm, tn),
                   outs, epilogue=lambda acc: (acc, acc))


def _rowwise(name, fn, rows, consts, out_rows, out_accs=(), tr=256):
    nr, nc, no = len(rows), len(consts), len(out_rows)
    n_rows = rows[0].shape[0]
    tr = _tile(n_rows, tr)

    def body(*refs):
        vals = fn(*[r[...] for r in refs[:nr + nc]])
        o_refs, a_refs = refs[nr + nc:nr + nc + no], refs[nr + nc + no:]
        for ref, v in zip(o_refs, vals[:no]):
            ref[...] = v.astype(ref.dtype)
        first = pl.program_id(0) == 0

        @pl.when(first)
        def _():
            for ref, v in zip(a_refs, vals[no:]):
                ref[...] = v

        @pl.when(jnp.logical_not(first))
        def _():
            for ref, v in zip(a_refs, vals[no:]):
                ref[...] += v

    def whole(shape):
        return pl.BlockSpec(shape, lambda i: (0,) * len(shape))

    return pl.pallas_call(
        body, name=name, grid=(n_rows // tr,),
        in_specs=[pl.BlockSpec((tr, a.shape[1]), lambda i: (i, 0)) for a in rows] + [whole(c.shape) for c in consts],
        out_specs=[pl.BlockSpec((tr, f), lambda i: (i, 0)) for f, _ in out_rows] + [whole(s) for s in out_accs],
        out_shape=[jax.ShapeDtypeStruct((n_rows, f), d) for f, d in out_rows]
        + [jax.ShapeDtypeStruct(s, F32) for s in out_accs],
        compiler_params=_params(("arbitrary",)),
    )(*rows, *consts)


def _rms_fwd(x, g):
    return x * lax.rsqrt(jnp.mean(x * x, axis=-1, keepdims=True) + NORM_EPS) * g


def _rms_bwd(dy, x, g):
    rstd = lax.rsqrt(jnp.mean(x * x, axis=-1, keepdims=True) + NORM_EPS)
    n = x * rstd
    dn = dy * g
    dx = rstd * (dn - n * jnp.mean(dn * n, axis=-1, keepdims=True))
    return dx, jnp.sum(dy * n, axis=0, keepdims=True)


def _rope(x, cs, s1, s2):
    return x * cs + pltpu.roll(x, 32, 1) * s1 + pltpu.roll(x, 96, 1) * s2


def _rope_t(dy, cs, s1, s2):
    return dy * cs + pltpu.roll(dy * s1, 96, 1) + pltpu.roll(dy * s2, 32, 1)


def _gelu(z):
    return 0.5 * z * (1.0 + lax.erf(z * (1.0 / math.sqrt(2.0))))


def _gelu_grad(z):
    return 0.5 * (1.0 + lax.erf(z * (1.0 / math.sqrt(2.0)))) + z * jnp.exp(-0.5 * z * z) * (1.0 / math.sqrt(2.0 * math.pi))


def _att_scores(q, kv, kr, scale, masked, transposed):
    k = jnp.concatenate([kv[:, :NOPE], kr], axis=1)
    if transposed:
        s = lax.dot_general(k, q, NT, preferred_element_type=F32) * scale
    else:
        s = lax.dot_general(q, k, NT, preferred_element_type=F32) * scale
    if masked:
        r = lax.broadcasted_iota(jnp.int32, s.shape, 0)
        c = lax.broadcasted_iota(jnp.int32, s.shape, 1)
        s = jnp.where((r <= c) if transposed else (c <= r), s, -jnp.inf)
    return s, k


def _flash_fwd(qb, kvb, krb):
    s_len = qb.shape[0]
    t = ATT_TILE

    def body(q_ref, kv_ref, kr_ref, o_ref, lse_ref, m_s, l_s, acc_s):
        qi = pl.program_id(1)
        m_s[...] = jnp.full_like(m_s, -jnp.inf)
        l_s[...] = jnp.zeros_like(l_s)
        acc_s[...] = jnp.zeros_like(acc_s)
        q = q_ref[...]

        def step(ki, masked):
            rows = pl.ds(pl.multiple_of(ki * t, t), t)
            kv = kv_ref[rows, :]
            s, _ = _att_scores(q, kv, kr_ref[rows, :], LOG2_SCALE, masked, False)
            m_prev = m_s[...]
            m_new = jnp.maximum(m_prev, jnp.max(s, axis=1, keepdims=True))
            alpha = jnp.exp2(m_prev - m_new)
            p = jnp.exp2(s - jnp.tile(m_new, (1, t // LANE)))
            l_s[...] = alpha * l_s[...] + jnp.sum(p, axis=1, keepdims=True)
            acc_s[...] = alpha * acc_s[...] + jnp.dot(p.astype(BF16), kv[:, NOPE:], preferred_element_type=F32)
            m_s[...] = m_new

        def unmasked(ki, carry):
            step(ki, False)
            return carry

        lax.fori_loop(0, qi, unmasked, 0)
        step(qi, True)
        o_ref[...] = (acc_s[...] / l_s[...]).astype(o_ref.dtype)
        lse_ref[...] = (m_s[...] + jnp.log2(l_s[...]))[:, :1]

    return pl.pallas_call(
        body, name="flash_fwd", grid=(HEADS, s_len // t),
        in_specs=[pl.BlockSpec((t, HEAD_PAD), lambda h, qi: (qi, h)),
                  pl.BlockSpec((s_len, HEAD_PAD), lambda h, qi: (0, h)),
                  pl.BlockSpec((s_len, LANE), lambda h, qi: (0, 0))],
        out_specs=[pl.BlockSpec((t, VHEAD), lambda h, qi: (qi, h)),
                   pl.BlockSpec((None, t, 1), lambda h, qi: (h, qi, 0))],
        out_shape=[jax.ShapeDtypeStruct((s_len, HEADS * VHEAD), BF16),
                   jax.ShapeDtypeStruct((HEADS, s_len, 1), F32)],
        scratch_shapes=[pltpu.VMEM((t, LANE), F32), pltpu.VMEM((t, LANE), F32), pltpu.VMEM((t, VHEAD), F32)],
        compiler_params=_params(("parallel", "arbitrary")),
    )(qb, kvb, krb)


def _flash_bwd_dq(qb, kvb, krb, dob, lse, delta):
    s_len = qb.shape[0]
    t = ATT_TILE
    nq = s_len // t
    scale = QK_HEAD ** -0.5

    def body(q_ref, kv_ref, kr_ref, do_ref, lse_ref, dl_ref, dq_ref, acc_s):
        qi = pl.program_id(1)
        acc_s[...] = jnp.zeros_like(acc_s)
        q, do = q_ref[...], do_ref[...]
        lse = jnp.broadcast_to(lse_ref[...], (t, LANE))
        dl = jnp.broadcast_to(dl_ref[...], (t, LANE))

        def step(ki, masked):
            rows = pl.ds(pl.multiple_of(ki * t, t), t)
            kv = kv_ref[rows, :]
            s, k = _att_scores(q, kv, kr_ref[rows, :], LOG2_SCALE, masked, False)
            p = jnp.exp2(s - jnp.tile(lse, (1, t // LANE)))
            dp = lax.dot_general(do, kv[:, NOPE:], NT, preferred_element_type=F32)
            ds = (p * (dp - jnp.tile(dl, (1, t // LANE))) * scale).astype(BF16)
            acc_s[...] += jnp.dot(ds, k, preferred_element_type=F32)

        def unmasked(ki, carry):
            step(ki, False)
            return carry

        lax.fori_loop(0, qi, unmasked, 0)
        step(qi, True)
        dq_ref[...] = acc_s[...]

    col = pl.BlockSpec((None, t, 1), lambda h, qi: (h, qi, 0))
    return pl.pallas_call(
        body, name="flash_bwd_dq", grid=(HEADS, nq),
        in_specs=[pl.BlockSpec((t, HEAD_PAD), lambda h, qi: (qi, h)),
                  pl.BlockSpec((s_len, HEAD_PAD), lambda h, qi: (0, h)),
                  pl.BlockSpec((s_len, LANE), lambda h, qi: (0, 0)),
                  pl.BlockSpec((t, VHEAD), lambda h, qi: (qi, h)), col, col],
        out_specs=pl.BlockSpec((t, HEAD_PAD), lambda h, qi: (qi, h)),
        out_shape=jax.ShapeDtypeStruct((s_len, HEADS * HEAD_PAD), F32),
        scratch_shapes=[pltpu.VMEM((t, HEAD_PAD), F32)],
        compiler_params=_params(("parallel", "arbitrary")),
    )(qb, kvb, krb, dob, lse, delta)


def _flash_bwd_dkv(qb, kvb, krb, dob, lse_row, delta_row):
    s_len = qb.shape[0]
    t = ATT_TILE
    nq = s_len // t
    scale = QK_HEAD ** -0.5

    def body(q_ref, kv_ref, kr_ref, do_ref, lse_ref, dl_ref, dkv_ref, dkr_ref, dk_s, dv_s):
        ki = pl.program_id(1)
        dk_s[...] = jnp.zeros_like(dk_s)
        dv_s[...] = jnp.zeros_like(dv_s)
        kv, kr = kv_ref[...], kr_ref[...]

        def step(qi, masked):
            rows = pl.ds(pl.multiple_of(qi * t, t), t)
            q, do = q_ref[rows, :], do_ref[rows, :]
            st, _ = _att_scores(q, kv, kr, LOG2_SCALE, masked, True)
            pt = jnp.exp2(st - lse_ref[:, rows])
            dv_s[...] += jnp.dot(pt.astype(BF16), do, preferred_element_type=F32)
            dpt = lax.dot_general(kv[:, NOPE:], do, NT, preferred_element_type=F32)
            dst = (pt * (dpt - dl_ref[:, rows]) * scale).astype(BF16)
            dk_s[...] += jnp.dot(dst, q, preferred_element_type=F32)

        def unmasked(qi, carry):
            step(qi, False)
            return carry

        step(ki, True)
        lax.fori_loop(ki + 1, nq, unmasked, 0)
        dk = dk_s[...]
        dkv_ref[...] = jnp.concatenate([dk[:, :NOPE], dv_s[...]], axis=1).astype(dkv_ref.dtype)
        dkr_ref[...] = dk[:, NOPE:]

    row = pl.BlockSpec((None, 1, s_len), lambda h, ki: (h, 0, 0))
    return pl.pallas_call(
        body, name="flash_bwd_dkv", grid=(HEADS, nq),
        in_specs=[pl.BlockSpec((s_len, HEAD_PAD), lambda h, ki: (0, h)),
                  pl.BlockSpec((t, HEAD_PAD), lambda h, ki: (ki, h)),
                  pl.BlockSpec((t, LANE), lambda h, ki: (ki, 0)),
                  pl.BlockSpec((s_len, VHEAD), lambda h, ki: (0, h)), row, row],
        out_specs=[pl.BlockSpec((t, HEAD_PAD), lambda h, ki: (ki, h)),
                   pl.BlockSpec((t, LANE), lambda h, ki: (ki, h))],
        out_shape=[jax.ShapeDtypeStruct((s_len, HEADS * HEAD_PAD), BF16),
                   jax.ShapeDtypeStruct((s_len, HEADS * LANE), F32)],
        scratch_shapes=[pltpu.VMEM((t, HEAD_PAD), F32), pltpu.VMEM((t, VHEAD), F32)],
        compiler_params=_params(("parallel", "parallel")),
    )(qb, kvb, krb, dob, lse_row, delta_row)


def _att_delta(do, ob):
    s_len = do.shape[0]
    t = ATT_TILE

    def body(do_ref, o_ref, dob_ref, dl_ref):
        d = do_ref[...]
        dob_ref[...] = d.astype(dob_ref.dtype)
        dl_ref[...] = jnp.sum(d * o_ref[...].astype(F32), axis=1, keepdims=True)

    blk = pl.BlockSpec((t, VHEAD), lambda i, h: (i, h))
    return pl.pallas_call(
        body, name="att_delta", grid=(s_len // t, HEADS), in_specs=[blk, blk],
        out_specs=[blk, pl.BlockSpec((None, t, 1), lambda i, h: (h, i, 0))],
        out_shape=[jax.ShapeDtypeStruct(do.shape, BF16), jax.ShapeDtypeStruct((HEADS, s_len, 1), F32)],
        compiler_params=_params(("parallel", "parallel")),
    )(do, ob)


def _tril(w):
    r = lax.broadcasted_iota(jnp.int32, w.shape, 0)
    c = lax.broadcasted_iota(jnp.int32, w.shape, 1)
    return jnp.where(c <= r, w, 0.0)


def _sgu_mix(w_ref, vln, gd):
    wcs = [_tril(w_ref[g]).astype(BF16) for g in range(SGU_GROUPS)]
    mixed = jnp.concatenate(
        [jnp.dot(wcs[g], vln[:, g * gd:(g + 1) * gd], preferred_element_type=F32) for g in range(SGU_GROUPS)], axis=1)
    return wcs, mixed


def _sgu_fwd(zpre, ln_g, ln_b, w_sp, bias_full):
    s_len, two_w = zpre.shape
    width = two_w // 2
    gd = width // SGU_GROUPS
    t = SGU_CHUNK

    def body(z_ref, g_ref, b_ref, w_ref, bias_ref, uv_ref):
        u = _gelu(z_ref[:, :width])
        v = _gelu(z_ref[:, width:])
        d = v - jnp.mean(v, axis=-1, keepdims=True)
        vhat = d * lax.rsqrt(jnp.mean(d * d, axis=-1, keepdims=True) + LN_EPS)
        vln = (vhat * g_ref[...] + b_ref[...]).astype(BF16)
        _, mixed = _sgu_mix(w_ref, vln, gd)
        uv_ref[...] = (u * (mixed + bias_ref[...])).astype(uv_ref.dtype)

    return pl.pallas_call(
        body, name="sgu_fwd", grid=(s_len // t,),
        in_specs=[pl.BlockSpec((t, two_w), lambda i: (i, 0)), pl.BlockSpec((1, width), lambda i: (0, 0)),
                  pl.BlockSpec((1, width), lambda i: (0, 0)), pl.BlockSpec(w_sp.shape, lambda i: (0, 0, 0)),
                  pl.BlockSpec((t, width), lambda i: (0, 0))],
        out_specs=pl.BlockSpec((t, width), lambda i: (i, 0)),
        out_shape=jax.ShapeDtypeStruct((s_len, width), BF16),
        compiler_params=_params(("parallel",)),
    )(zpre, ln_g, ln_b, w_sp, bias_full)


def _sgu_bwd(zpre, duv, ln_g, ln_b, w_sp, bias_full):
    s_len, two_w = zpre.shape
    width = two_w // 2
    gd = width // SGU_GROUPS
    t = SGU_CHUNK

    def body(z_ref, duv_ref, g_ref, b_ref, w_ref, bias_ref, dz_ref, dg_ref, db_ref, dw_ref, dbias_ref):
        first = pl.program_id(0) == 0

        def accumulate(ref, val):
            @pl.when(first)
            def _():
                ref[...] = val

            @pl.when(jnp.logical_not(first))
            def _():
                ref[...] += val

        zu, zv = z_ref[:, :width], z_ref[:, width:]
        u = _gelu(zu)
        v = _gelu(zv)
        d = v - jnp.mean(v, axis=-1, keepdims=True)
        rstd = lax.rsqrt(jnp.mean(d * d, axis=-1, keepdims=True) + LN_EPS)
        vhat = d * rstd
        vln = (vhat * g_ref[...] + b_ref[...]).astype(BF16)
        wcs, mixed = _sgu_mix(w_ref, vln, gd)
        duv_v = duv_ref[...]
        du = duv_v * (mixed + bias_ref[...])
        dmixed = duv_v * u
        dmb = dmixed.astype(BF16)
        dvln = jnp.concatenate(
            [lax.dot_general(wcs[g], dmb[:, g * gd:(g + 1) * gd], TN, preferred_element_type=F32)
             for g in range(SGU_GROUPS)], axis=1)
        for g in range(SGU_GROUPS):
            dw = lax.dot_general(dmb[:, g * gd:(g + 1) * gd], vln[:, g * gd:(g + 1) * gd], NT, preferred_element_type=F32)
            accumulate(dw_ref.at[g], _tril(dw))
        dvhat = dvln * g_ref[...]
        dv0 = rstd * (dvhat - jnp.mean(dvhat, axis=-1, keepdims=True)
                      - vhat * jnp.mean(dvhat * vhat, axis=-1, keepdims=True))
        dz_ref[:, :width] = (du * _gelu_grad(zu)).astype(dz_ref.dtype)
        dz_ref[:, width:] = (dv0 * _gelu_grad(zv)).astype(dz_ref.dtype)
        accumulate(dg_ref, jnp.sum(dvln * vhat, axis=0, keepdims=True))
        accumulate(db_ref, jnp.sum(dvln, axis=0, keepdims=True))
        accumulate(dbias_ref, dmixed)

    vec = pl.BlockSpec((1, width), lambda i: (0, 0))
    return pl.pallas_call(
        body, name="sgu_bwd", grid=(s_len // t,),
        in_specs=[pl.BlockSpec((t, two_w), lambda i: (i, 0)), pl.BlockSpec((t, width), lambda i: (i, 0)), vec, vec,
                  pl.BlockSpec(w_sp.shape, lambda i: (0, 0, 0)), pl.BlockSpec((t, width), lambda i: (0, 0))],
        out_specs=[pl.BlockSpec((t, two_w), lambda i: (i, 0)), vec, vec,
                   pl.BlockSpec(w_sp.shape, lambda i: (0, 0, 0)), pl.BlockSpec((t, width), lambda i: (0, 0))],
        out_shape=[jax.ShapeDtypeStruct((s_len, two_w), BF16), jax.ShapeDtypeStruct((1, width), F32),
                   jax.ShapeDtypeStruct((1, width), F32), jax.ShapeDtypeStruct(w_sp.shape, F32),
                   jax.ShapeDtypeStruct((t, width), F32)],
        compiler_params=_params(("arbitrary",)),
    )(zpre, duv, ln_g, ln_b, w_sp, bias_full)


def _rope_tables(positions):
    inv_freq = ROPE_THETA ** (-jnp.arange(0, ROPE, 2, dtype=F32) / ROPE)
    ang = positions.astype(F32)[:, None] * inv_freq
    cos, sin = jnp.cos(ang), jnp.sin(ang)
    z32, z64 = jnp.zeros_like(cos), jnp.zeros((cos.shape[0], LANE - ROPE), F32)
    return (jnp.concatenate([cos, cos, z64], axis=1), jnp.concatenate([z32, sin, z64], axis=1),
            jnp.concatenate([-sin, z32, z64], axis=1))


def _q_rope(name, q, tables, transpose):
    rot = _rope_t if transpose else _rope

    def fn(qv, cs, s1, s2):
        parts = []
        for h in range(HEADS):
            parts.append(qv[:, h * HEAD_PAD:h * HEAD_PAD + NOPE])
            parts.append(rot(qv[:, h * HEAD_PAD + NOPE:(h + 1) * HEAD_PAD], cs, s1, s2))
        return (jnp.concatenate(parts, axis=1),)

    return _rowwise(name, fn, [q, *tables], [], [(q.shape[1], BF16)])[0]


def _ffn_fwd(x, g, w_up3, w_down3):
    h2 = _rowwise("ffn_norm", lambda xv, gv: (_rms_fwd(xv, gv),), [x], [g], [(x.shape[1], BF16)])[0]

    def sq_relu(acc):
        r = jnp.maximum(acc, 0.0)
        return (r * r,)

    r = _mm_stacked("ffn_up", h2, w_up3, "col", (BF16,), sq_relu)[0]
    x_out = _mm_stacked("ffn_down", r, w_down3, "row", (F32,), lambda acc, res: (acc + res,), [x])[0]
    return x_out, (x, h2, r)


def _ffn_bwd(dx, dxb, saved, g, w_up3, w_down3):
    x, h2, r = saved
    da = _mm_stacked_nt("ffn_down_dx", dxb, w_down3, "row", (BF16,),
                        lambda acc, rv: (acc * (2.0 * jnp.sqrt(rv.astype(F32))),), [r])[0]
    g_down = _mm_tn_stacked("ffn_down_dw", r, dxb, w_down3.shape, "row")
    dh2 = _mm_stacked_nt("ffn_up_dx", da, w_up3, "col", (F32,))[0]
    g_up = _mm_tn_stacked("ffn_up_dw", h2, da, w_up3.shape, "col")
    dx, dxb, dg = _norm_bwd("ffn_norm_bwd", dh2, x, g, dx)
    return dx, dxb, dg, g_up, g_down


def _norm_bwd(name, dh, x, g, dres):
    def fn(dhv, xv, rv, gv):
        dxv, dg = _rms_bwd(dhv, xv, gv)
        return dxv + rv, dxv + rv, dg

    return _rowwise(name, fn, [dh, x, dres], [g], [(x.shape[1], F32), (x.shape[1], BF16)], [g.shape])


def _mla_fwd(x, g, wdkv, q_norm, kv_norm, wq, wkv, wo, tables):
    d = x.shape[1]
    h = _rowwise("mla_norm", lambda xv, gv: (_rms_fwd(xv, gv),), [x], [g], [(d, BF16)])[0]
    lat = _mm("mla_dkv", h, wdkv)[0]

    def lat_post(lv, cs, s1, s2, qg, kg):
        return (_rms_fwd(lv[:, :Q_RANK], qg), _rms_fwd(lv[:, Q_RANK:Q_RANK + KV_RANK], kg),
                _rope(lv[:, Q_RANK + KV_RANK:], cs, s1, s2))

    cqn, ckvn, krb = _rowwise("mla_lat", lat_post, [lat, *tables], [q_norm, kv_norm],
                              [(Q_RANK, BF16), (KV_RANK, BF16), (LANE, BF16)])
    q = _mm("mla_uq", cqn, wq)[0]
    kvb = _mm("mla_ukv", ckvn, wkv, (BF16,))[0]
    qb = _q_rope("mla_q_rope", q, tables, False)
    ob, lse = _flash_fwd(qb, kvb, krb)
    x_mid = _mm("mla_o", ob, wo, (F32,), lambda acc, res: (acc + res,), [x])[0]
    return x_mid, (x, h, lat, cqn, ckvn, krb, qb, kvb, ob, lse)


def _mla_bwd(dx, dxb, saved, g, wdkv, q_norm, kv_norm, wq, wkv, wo, tables):
    x, h, lat, cqn, ckvn, krb, qb, kvb, ob, lse = saved
    s_len = x.shape[0]
    do = _mm("mla_o_dx", dxb, wo, nt=True)[0]
    g_wo = _mm_tn("mla_o_dw", ob, dxb)[0]
    dob, delta = _att_delta(do, ob)
    dq = _flash_bwd_dq(qb, kvb, krb, dob, lse, delta)
    dkvb, dkr = _flash_bwd_dkv(qb, kvb, krb, dob, lse.reshape(HEADS, 1, s_len), delta.reshape(HEADS, 1, s_len))
    dqb = _q_rope("mla_q_rope_bwd", dq, tables, True)
    dcqn = _mm("mla_uq_dx", dqb, wq, nt=True)[0]
    g_wq = _mm_tn("mla_uq_dw", cqn, dqb)[0]
    dckvn = _mm("mla_ukv_dx", dkvb, wkv, nt=True)[0]
    g_wkv = _mm_tn("mla_ukv_dw", ckvn, dkvb)[0]

    def lat_bwd(dq_v, dkv_v, dkr_v, lv, cs, s1, s2, qg, kg):
        dcq, dqg = _rms_bwd(dq_v, lv[:, :Q_RANK], qg)
        dckv, dkg = _rms_bwd(dkv_v, lv[:, Q_RANK:Q_RANK + KV_RANK], kg)
        dkr_sum = dkr_v[:, :LANE]
        for hd in range(1, HEADS):
            dkr_sum = dkr_sum + dkr_v[:, hd * LANE:(hd + 1) * LANE]
        return jnp.concatenate([dcq, dckv, _rope_t(dkr_sum, cs, s1, s2)], axis=1), dqg, dkg

    dlat, g_qn, g_kvn = _rowwise("mla_lat_bwd", lat_bwd, [dcqn, dckvn, dkr, lat, *tables], [q_norm, kv_norm],
                                 [(LAT_PAD, BF16)], [q_norm.shape, kv_norm.shape])
    dh = _mm("mla_dkv_dx", dlat, wdkv, nt=True)[0]
    g_wdkv = _mm_tn("mla_dkv_dw", h, dlat)[0]
    dx, dxb, dg = _norm_bwd("mla_norm_bwd", dh, x, g, dx)
    return dx, dxb, dg, g_wdkv, g_qn, g_kvn, g_wq, g_wkv, g_wo


def _sgu_layer_fwd(x, g, w_in3, ln_g, ln_b, w_sp, bias_full, w_out3):
    h = _rowwise("sgu_norm", lambda xv, gv: (_rms_fwd(xv, gv),), [x], [g], [(x.shape[1], BF16)])[0]
    zpre = _mm_stacked("sgu_in", h, w_in3, "col")[0]
    uv = _sgu_fwd(zpre, ln_g, ln_b, w_sp, bias_full)
    x_mid = _mm_stacked("sgu_out", uv, w_out3, "row", (F32,), lambda acc, res: (acc + res,), [x])[0]
    return x_mid, (x, h, zpre, uv)


def _sgu_layer_bwd(dx, dxb, saved, g, w_in3, ln_g, ln_b, w_sp, bias_full, w_out3):
    x, h, zpre, uv = saved
    duv = _mm_stacked_nt("sgu_out_dx", dxb, w_out3, "row")[0]
    g_out = _mm_tn_stacked("sgu_out_dw", uv, dxb, w_out3.shape, "row")
    dz, g_lng, g_lnb, g_wsp, g_bias = _sgu_bwd(zpre, duv, ln_g, ln_b, w_sp, bias_full)
    dh = _mm_stacked_nt("sgu_in_dx", dz, w_in3, "col")[0]
    g_in = _mm_tn_stacked("sgu_in_dw", h, dz, w_in3.shape, "col")
    dx, dxb, dg = _norm_bwd("sgu_norm_bwd", dh, x, g, dx)
    return dx, dxb, dg, g_in, g_out, g_lng, g_lnb, g_wsp, g_bias


def _loss_head(x, target, g):
    d = x.shape[1]

    def fn(xv, tv, gv):
        err = _rms_fwd(xv, gv) - tv
        dxv, dg = _rms_bwd(err * (1.0 / d), xv, gv)
        return dxv, dxv, dg, jnp.sum(err * err, axis=0, keepdims=True)

    return _rowwise("loss_head", fn, [x, target], [g], [(d, F32), (d, BF16)], [g.shape, g.shape])


def _mixer_weights(i, stacks):
    by_rows = lambda a: a.reshape(N_SHARDS * a.shape[1], a.shape[2])
    by_cols = lambda a: a.transpose(1, 0, 2).reshape(a.shape[1], N_SHARDS * a.shape[2])
    if i % 2:
        w_in3, w_out3, ln_g, ln_b = stacks
        return w_in3, ln_g.reshape(1, -1), ln_b.reshape(1, -1), w_out3
    wdkv = by_rows(stacks[0])
    wdkv = jnp.pad(wdkv, ((0, 0), (0, LAT_PAD - wdkv.shape[1])))
    wq = jnp.pad(by_cols(stacks[1]).reshape(Q_RANK, HEADS, QK_HEAD), ((0, 0), (0, 0), (0, HEAD_PAD - QK_HEAD)))
    return wdkv, wq.reshape(Q_RANK, HEADS * HEAD_PAD), by_cols(stacks[2]), by_rows(stacks[3])


def _local_step(x, positions, target, norm_mix, norm_ffn, final_norm, q_norm, kv_norm, w_sp, b_sp, mixers, ffn, reducer):
    tables = _rope_tables(positions)
    gd = mixers[1][2].size // SGU_GROUPS
    bias_full = [jnp.repeat(b_sp[j].T, gd, axis=1) for j in range(DEPTH // 2)]
    saved, mla, sgu = [], [None] * (DEPTH // 2), [None] * (DEPTH // 2)
    for i in range(DEPTH):
        j = i // 2
        x, *stacks = lax.optimization_barrier((x, *mixers[i]))
        if i % 2 == 0:
            wdkv, wq, wkv, wo = mla[j] = _mixer_weights(i, stacks)
            x, s_mix = _mla_fwd(x, norm_mix[i:i + 1], wdkv, q_norm[j:j + 1], kv_norm[j:j + 1], wq, wkv, wo, tables)
        else:
            w_in3, ln_g, ln_b, w_out3 = sgu[j] = _mixer_weights(i, stacks)
            x, s_mix = _sgu_layer_fwd(x, norm_mix[i:i + 1], w_in3, ln_g, ln_b, w_sp[j], bias_full[j], w_out3)
        x, s_ffn = _ffn_fwd(x, norm_ffn[i:i + 1], *ffn[i])
        saved.append((s_mix, s_ffn))
    dx, dxb, g_final, sq_cols = _loss_head(x, target, final_norm[None, :])
    loss = 0.5 * jnp.sum(sq_cols) / x.shape[1]

    def pair(g):
        return g, g.astype(BF16)

    g_mix, g_ffn = [None] * DEPTH, [None] * DEPTH
    mla_g, sgu_g = [None] * (DEPTH // 2), [None] * (DEPTH // 2)
    for i in reversed(range(DEPTH)):
        j = i // 2
        s_mix, s_ffn = saved[i]
        dx, dxb, g_ffn[i], g_up, g_down = _ffn_bwd(dx, dxb, s_ffn, norm_ffn[i:i + 1], *ffn[i])
        dxb = reducer.add(f"ffn{i}", i, {"ffn_w_up": g_up, "ffn_w_down": g_down}, dxb)
        dxb = reducer.phase_end(dxb)
        if i % 2 == 0:
            wdkv, wq, wkv, wo = mla[j]
            dx, dxb, g_mix[i], g_wdkv, g_qn, g_kvn, g_wq, g_wkv, g_wo = _mla_bwd(
                dx, dxb, s_mix, norm_mix[i:i + 1], wdkv, q_norm[j:j + 1], kv_norm[j:j + 1], wq, wkv, wo, tables)
            mla_g[j] = (g_qn, g_kvn)
            g_wq = g_wq.reshape(Q_RANK, HEADS, HEAD_PAD)[..., :QK_HEAD].reshape(Q_RANK, N_SHARDS, -1)
            dxb = reducer.add(f"mla{j}", j, {
                "mla_w_dkv": pair(g_wdkv[:, :Q_RANK + KV_RANK + ROPE].reshape(N_SHARDS, -1, Q_RANK + KV_RANK + ROPE)),
                "mla_w_uq": pair(g_wq.transpose(1, 0, 2)),
                "mla_w_ukv": pair(g_wkv.reshape(KV_RANK, N_SHARDS, -1).transpose(1, 0, 2)),
                "mla_w_o": pair(g_wo.reshape(N_SHARDS, -1, g_wo.shape[1]))}, dxb)
        else:
            w_in3, ln_g, ln_b, w_out3 = sgu[j]
            dx, dxb, g_mix[i], g_in, g_out, g_lng, g_lnb, g_wsp, g_bias = _sgu_layer_bwd(
                dx, dxb, s_mix, norm_mix[i:i + 1], w_in3, ln_g, ln_b, w_sp[j], bias_full[j], w_out3)
            sgu_g[j] = (g_wsp, g_bias.reshape(SGU_CHUNK, SGU_GROUPS, gd).sum(axis=-1).T)
            dxb = reducer.add(f"sgu{j}", j, {"sgu_w_in": g_in, "sgu_w_out": g_out,
                                             "sgu_ln_g": pair(g_lng.reshape(N_SHARDS, -1, LANE)),
                                             "sgu_ln_b": pair(g_lnb.reshape(N_SHARDS, -1, LANE))}, dxb)
        dxb = reducer.phase_end(dxb)
    small = dict(
        norm_mix=jnp.concatenate(g_mix, axis=0), norm_ffn=jnp.concatenate(g_ffn, axis=0), final_norm=g_final[0],
        q_norm=jnp.concatenate([m[0] for m in mla_g], axis=0), kv_norm=jnp.concatenate([m[1] for m in mla_g], axis=0),
        w_sp=jnp.stack([s[0] for s in sgu_g]), b_sp=jnp.stack([s[1] for s in sgu_g]))
    return loss, dx, small


HBM_SPEC = pl.BlockSpec(memory_space=pltpu.HBM)


def _place():
    x, y, c = lax.axis_index("x"), lax.axis_index("y"), lax.axis_index("c")
    return x, y, c, [(1 - x, y), (x, 1 - y), (1 - x, 1 - y)]


def _remote(src, dst, send_sems, recv_sems, k, to):
    return pltpu.make_async_remote_copy(src_ref=src, dst_ref=dst, send_sem=send_sems.at[k], recv_sem=recv_sems.at[k],
                                        device_id=to, device_id_type=MESH)


def _gather_layer(tag, shards):
    n = len(shards)
    split = [s.shape[0] >= 16 for s in shards]

    def body(*refs):
        ins, outs = refs[:n], refs[n:2 * n]
        send_sems, recv_sems, local_sems = refs[2 * n:]
        x, y, c, chips = _place()
        mine = 2 * x + y
        barrier = pltpu.get_barrier_semaphore()
        peers = [(x, y, 1 - c)] + [(*chip, c) for chip in chips]
        for peer in peers:
            pl.semaphore_signal(barrier, inc=1, device_id=peer, device_id_type=MESH)
        pl.semaphore_wait(barrier, len(peers))

        def rows(t, half):
            hr = shards[t].shape[0] // 2
            return pl.ds(half * hr, hr) if split[t] else pl.ds(0, shards[t].shape[0])

        local, sent = [], []
        for t in range(n):
            local.append(pltpu.make_async_copy(ins[t], outs[t].at[mine], local_sems.at[t]))
            local[-1].start()
            for j, chip in enumerate(chips):
                cp = _remote(ins[t].at[rows(t, c)], outs[t].at[mine, rows(t, c)], send_sems, recv_sems, 3 * t + j, (*chip, c))
                cp.start()
                sent.append(cp)
        for j, chip in enumerate(chips):
            theirs = 2 * chip[0] + chip[1]
            for t in range(n):
                piece = outs[t].at[theirs, rows(t, c)]
                _remote(piece, piece, send_sems, recv_sems, 3 * t + j, (x, y, c)).wait_recv()
                if split[t]:
                    cp = _remote(piece, piece, send_sems, recv_sems, 3 * n + 3 * t + j, (x, y, 1 - c))
                    cp.start()
                    sent.append(cp)
        for j, chip in enumerate(chips):
            theirs = 2 * chip[0] + chip[1]
            for t in range(n):
                if split[t]:
                    piece = outs[t].at[theirs, rows(t, 1 - c)]
                    _remote(piece, piece, send_sems, recv_sems, 3 * n + 3 * t + j, (x, y, c)).wait_recv()
        for cp in sent:
            cp.wait_send()
        for cp in local:
            cp.wait()

    return pl.kernel(
        body, name=f"gather_{tag}", mesh=plsc.ScalarSubcoreMesh(axis_name="sequencer", num_cores=1),
        out_type=[jax.ShapeDtypeStruct((N_SHARDS, *s.shape), s.dtype) for s in shards],
        scratch_types=[pltpu.SemaphoreType.DMA((6 * n,)), pltpu.SemaphoreType.DMA((6 * n,)), pltpu.SemaphoreType.DMA((n,))],
        compiler_params=pltpu.CompilerParams(collective_id=ID_GATHER),
    )(*shards)


SEQUENCER = dict(axis_name="sequencer", num_cores=1)
ID_GATHER, ID_EXCHANGE, ID_SHARE = 0, 1, 2
MIN_SPLIT_ROWS = 16


def _handshake(peers):
    barrier = pltpu.get_barrier_semaphore()
    for peer in peers:
        pl.semaphore_signal(barrier, inc=1, device_id=peer, device_id_type=MESH)
    pl.semaphore_wait(barrier, len(peers))


def _half_rows(rows, half):
    return pl.ds(half * (rows // 2), rows // 2) if rows >= MIN_SPLIT_ROWS else pl.ds(0, rows)


def _exchange_partials(tag, stacks):
    n = len(stacks)

    def body(*refs):
        ins, outs, send_sems, recv_sems = refs[:n], refs[n:2 * n], refs[2 * n], refs[2 * n + 1]
        x, y, c, chips = _place()
        mine = 2 * x + y
        _handshake([(x, y, 1 - c)] + [(*chip, c) for chip in chips] + [(*chip, 1 - c) for chip in chips])
        sent = []
        for t in range(n):
            r = stacks[t].shape[1]
            sent.append(_remote(ins[t].at[mine, _half_rows(r, 1 - c)], outs[t].at[0], send_sems, recv_sems, 7 * t, (x, y, 1 - c)))
            for j, chip in enumerate(chips):
                theirs = 2 * chip[0] + chip[1]
                sent.append(_remote(ins[t].at[theirs, _half_rows(r, c)], outs[t].at[1 + j], send_sems, recv_sems,
                                    7 * t + 1 + j, (*chip, c)))
                sent.append(_remote(ins[t].at[theirs, _half_rows(r, 1 - c)], outs[t].at[4 + j], send_sems, recv_sems,
                                    7 * t + 4 + j, (*chip, 1 - c)))
        for cp in sent:
            cp.start()
        for cp in sent:
            cp.wait_send()
        for t in range(n):
            for k in range(7):
                _remote(outs[t].at[k], outs[t].at[k], send_sems, recv_sems, 7 * t + k, (x, y, c)).wait_recv()

    def landing(s):
        return (7, s.shape[1] // 2 if s.shape[1] >= MIN_SPLIT_ROWS else s.shape[1], s.shape[2])

    return pl.kernel(
        body, name=f"reduce_exchange_{tag}", mesh=plsc.ScalarSubcoreMesh(**SEQUENCER),
        out_type=[jax.ShapeDtypeStruct(landing(s), s.dtype) for s in stacks],
        scratch_types=[pltpu.SemaphoreType.DMA((7 * n,)), pltpu.SemaphoreType.DMA((7 * n,))],
        compiler_params=pltpu.CompilerParams(collective_id=ID_EXCHANGE),
    )(*stacks)


def _share_halves(tag, halves):
    n = len(halves)

    def body(*refs):
        ins, outs, send_sems, recv_sems = refs[:n], refs[n:2 * n], refs[2 * n], refs[2 * n + 1]
        x, y, c, _ = _place()
        _handshake([(x, y, 1 - c)])
        sent = [_remote(ins[t], outs[t], send_sems, recv_sems, t, (x, y, 1 - c)) for t in range(n)]
        for cp in sent:
            cp.start()
        for cp in sent:
            cp.wait()

    return pl.kernel(
        body, name=f"reduce_share_{tag}", mesh=plsc.ScalarSubcoreMesh(**SEQUENCER),
        out_type=[jax.ShapeDtypeStruct(h.shape, h.dtype) for h in halves],
        scratch_types=[pltpu.SemaphoreType.DMA((n,)), pltpu.SemaphoreType.DMA((n,))],
        compiler_params=pltpu.CompilerParams(collective_id=ID_SHARE),
    )(*halves)


def _all_reduce_small(part):
    rows = part.shape[0]

    def body(p_ref, out_ref, sib_buf, chip_sums, send_sems, recv_sems):
        x, y, c, chips = _place()
        mine = 2 * x + y
        swap = _remote(p_ref, sib_buf, send_sems, recv_sems, 0, (x, y, 1 - c))
        swap.start()
        swap.wait()
        chip_sums[mine] = p_ref[...] + sib_buf[...]
        sent = [_remote(chip_sums.at[mine], chip_sums.at[mine], send_sems, recv_sems, 1 + j, (*chip, c))
                for j, chip in enumerate(chips)]
        for cp in sent:
            cp.start()
        for j, chip in enumerate(chips):
            sent[j].wait_send()
            theirs = chip_sums.at[2 * chip[0] + chip[1]]
            _remote(theirs, theirs, send_sems, recv_sems, 1 + j, (x, y, c)).wait_recv()
        out_ref[...] = ((chip_sums[0] + chip_sums[1]) + chip_sums[2]) + chip_sums[3]

    vmem = pl.BlockSpec(memory_space=pltpu.VMEM)
    return pl.pallas_call(
        body, name="all_reduce_small", in_specs=[vmem], out_specs=vmem, out_shape=jax.ShapeDtypeStruct(part.shape, F32),
        scratch_shapes=[pltpu.VMEM((rows, LANE), F32), pltpu.VMEM((N_SHARDS, rows, LANE), F32),
                        pltpu.SemaphoreType.DMA((4,)), pltpu.SemaphoreType.DMA((4,))],
        compiler_params=pltpu.CompilerParams(vmem_limit_bytes=VMEM_LIMIT_BYTES),
    )(part)


def _sum_partials(g3, others, sel):
    _, rows, c = others.shape
    whole = g3.shape[1] == rows
    tr = _tile(rows, 512)
    nb = rows // tr

    def body(sel_ref, g_ref, *rest):
        same = g_ref[...].astype(F32)
        for ref in rest[1:4]:
            same = same + ref[...].astype(F32)
        other = rest[0][...].astype(F32)
        for ref in rest[4:7]:
            other = other + ref[...].astype(F32)
        rest[7][...] = same + other

    blk = (None, tr, c)
    slots = [pl.BlockSpec(blk, functools.partial(lambda i, sr, k: (k, i, 0), k=k)) for k in range(7)]
    return pl.pallas_call(
        body, name="reduce_sum_partials",
        grid_spec=pltpu.PrefetchScalarGridSpec(
            num_scalar_prefetch=1, grid=(nb,),
            in_specs=[pl.BlockSpec(blk, lambda i, sr: (sr[0], (0 if whole else sr[1] * nb) + i, 0))] + slots,
            out_specs=pl.BlockSpec((tr, c), lambda i, sr: (i, 0))),
        out_shape=jax.ShapeDtypeStruct((rows, c), F32),
        compiler_params=_params(("parallel",)),
    )(sel, g3, *[others] * 7)


def _adamw_math(w, g, m, v):
    nm = ADAM_B1 * m + (1.0 - ADAM_B1) * g
    nv = ADAM_B2 * v + (1.0 - ADAM_B2) * (g * g)
    m_hat = nm / (1.0 - ADAM_B1 ** ADAM_STEP)
    v_hat = nv / (1.0 - ADAM_B2 ** ADAM_STEP)
    return -ADAM_LR * (m_hat / (jnp.sqrt(v_hat) + ADAM_EPS) + ADAM_WD * w), nm, nv


def _adamw_layer(layer, w, m, v, g_mine, g_sibling, sel, prev):
    lyr, r, c = w.shape
    rows = g_mine.shape[0]
    halves = r // rows
    tr = _tile(rows, 512)
    nb = rows // tr
    n_g = 1 if g_sibling is None else 2

    def body(sel_ref, w_ref, m_ref, v_ref, *rest):
        g = rest[0][...]
        if n_g == 2:
            g = jnp.where(pl.program_id(0) == sel_ref[1], g, rest[1][...])
        outs = rest[n_g + (0 if prev is None else 4):]
        d, nm, nv = _adamw_math(w_ref[...], g, m_ref[...], v_ref[...])
        for ref, val in zip(outs, (g, d, nm, nv)):
            ref[...] = val

    full = pl.BlockSpec((None, tr, c), lambda h, i, sr: (layer, h * nb + i, 0))
    part = pl.BlockSpec((tr, c), lambda h, i, sr: (i, 0))
    n_in = 4 + n_g
    return pl.pallas_call(
        body, name="adamw_layer",
        grid_spec=pltpu.PrefetchScalarGridSpec(
            num_scalar_prefetch=1, grid=(halves, nb),
            in_specs=[full] * 3 + [part] * n_g + ([] if prev is None else [pl.BlockSpec(memory_space=pl.ANY)] * 4),
            out_specs=[full] * 4),
        out_shape=[jax.ShapeDtypeStruct(w.shape, F32)] * 4,
        input_output_aliases={} if prev is None else {n_in + k: k for k in range(4)},
        compiler_params=_params(("parallel", "parallel")),
    )(sel, w, m, v, g_mine, *([] if g_sibling is None else [g_sibling]), *([] if prev is None else prev))


def _adamw(w, g, m, v):
    lyr, r, c = w.shape
    tr = _tile(r, 512)

    def body(w_ref, g_ref, m_ref, v_ref, d_ref, nm_ref, nv_ref):
        gv = g_ref[...]
        nm = ADAM_B1 * m_ref[...] + (1.0 - ADAM_B1) * gv
        nv = ADAM_B2 * v_ref[...] + (1.0 - ADAM_B2) * (gv * gv)
        m_hat = nm / (1.0 - ADAM_B1 ** ADAM_STEP)
        v_hat = nv / (1.0 - ADAM_B2 ** ADAM_STEP)
        d_ref[...] = -ADAM_LR * (m_hat / (jnp.sqrt(v_hat) + ADAM_EPS) + ADAM_WD * w_ref[...])
        nm_ref[...] = nm
        nv_ref[...] = nv

    blk = pl.BlockSpec((None, tr, c), lambda l, i: (l, i, 0))
    return pl.pallas_call(
        body, name="adamw", grid=(lyr, r // tr), in_specs=[blk] * 4, out_specs=[blk] * 3,
        out_shape=[jax.ShapeDtypeStruct(w.shape, F32)] * 3,
        compiler_params=_params(("parallel", "parallel")),
    )(w, g, m, v)


SHARDED = ("mla_w_dkv", "mla_w_uq", "mla_w_ukv", "mla_w_o", "sgu_w_in", "sgu_ln_g", "sgu_ln_b", "sgu_w_out",
           "ffn_w_up", "ffn_w_down")
REPLICATED = ("norm_mix", "norm_ffn", "final_norm", "mla_q_norm", "mla_kv_norm", "sgu_w_spatial", "sgu_b_spatial")
WEIGHTS = ("norm_mix", "norm_ffn", "final_norm", "mla_w_dkv", "mla_q_norm", "mla_kv_norm", "mla_w_uq", "mla_w_ukv",
           "mla_w_o", "sgu_w_in", "sgu_ln_g", "sgu_ln_b", "sgu_w_spatial", "sgu_b_spatial", "sgu_w_out", "ffn_w_up",
           "ffn_w_down")


class _Reducer:
    def __init__(self, state, sel):
        self.state, self.sel = state, sel
        self.started, self.travelling, self.summed = [], [], []
        self.done = {}

    def add(self, tag, layer, grads, token):
        names = list(grads)
        token, *tied = lax.optimization_barrier((token, *[a for n in names for a in grads[n]]))
        f32s, bf16s = tied[0::2], tied[1::2]
        received = _exchange_partials(tag, bf16s)
        own = [g if g.shape[1] >= MIN_SPLIT_ROWS else gb for g, gb in zip(f32s, bf16s)]
        self.started.append((tag, layer, names, own, received))
        return token

    def phase_end(self, token):
        for tag, layer, names, own, received in self.travelling:
            token, *received = lax.optimization_barrier((token, *received))
            mine = [_sum_partials(g, got, self.sel) for g, got in zip(own, received)]
            cut = [k for k, g in enumerate(own) if g.shape[1] >= MIN_SPLIT_ROWS]
            theirs = dict(zip(cut, _share_halves(tag, [mine[k] for k in cut])))
            self.summed.append((layer, names, mine, [theirs.get(k) for k in range(len(names))]))
            token = lax.optimization_barrier((token, *mine))[0]
        self.travelling, self.started = self.started, []
        return token

    def update(self, token):
        for layer, names, mine, theirs in self.summed:
            for name, g_mine, g_theirs in zip(names, mine, theirs):
                w, m, v = self.state[name]
                self.done[name] = _adamw_layer(layer, w, m, v, g_mine, g_theirs, self.sel, self.done.get(name))
                token = self.done[name][1]
        self.summed = []
        return token


def _as3d(name, a):
    return a.reshape(a.shape[0], -1, LANE) if name in ("sgu_ln_g", "sgu_ln_b") else a


def _pack(parts):
    flat = jnp.concatenate([p.reshape(-1) for p in parts])
    rows = -(-flat.shape[0] // (8 * LANE)) * 8
    return jnp.pad(flat, (0, rows * LANE - flat.shape[0])).reshape(rows, LANE)


def _unpack(packed, like):
    flat, out, at = packed.reshape(-1), [], 0
    for p in like:
        out.append(flat[at:at + p.size].reshape(p.shape))
        at += p.size
    return out


def kernel(x, positions, norm_mix, norm_ffn, final_norm, mla_w_dkv, mla_q_norm, mla_kv_norm, mla_w_uq, mla_w_ukv, mla_w_o, sgu_w_in, sgu_ln_g, sgu_ln_b, sgu_w_spatial, sgu_b_spatial, sgu_w_out, ffn_w_up, ffn_w_down, loss_target, m_norm_mix, m_norm_ffn, m_final_norm, m_mla_w_dkv, m_mla_q_norm, m_mla_kv_norm, m_mla_w_uq, m_mla_w_ukv, m_mla_w_o, m_sgu_w_in, m_sgu_ln_g, m_sgu_ln_b, m_sgu_w_spatial, m_sgu_b_spatial, m_sgu_w_out, m_ffn_w_up, m_ffn_w_down, v_norm_mix, v_norm_ffn, v_final_norm, v_mla_w_dkv, v_mla_q_norm, v_mla_kv_norm, v_mla_w_uq, v_mla_w_ukv, v_mla_w_o, v_sgu_w_in, v_sgu_ln_g, v_sgu_ln_b, v_sgu_w_spatial, v_sgu_b_spatial, v_sgu_w_out, v_ffn_w_up, v_ffn_w_down):
    given = dict(locals())
    w = {n: given[n] for n in WEIGHTS}
    mom = {n: given["m_" + n] for n in WEIGHTS}
    var = {n: given["v_" + n] for n in WEIGHTS}
    mixers, ffn, token = [], [], None
    for i in range(DEPTH):
        j = i // 2
        if i % 2 == 0:
            mixer = [w[n][j].astype(BF16) for n in ("mla_w_dkv", "mla_w_uq", "mla_w_ukv", "mla_w_o")]
        else:
            mixer = [sgu_w_in[j].astype(BF16), sgu_w_out[j].astype(BF16), sgu_ln_g[j].reshape(-1, LANE),
                     sgu_ln_b[j].reshape(-1, LANE)]
        for tag, shards, into in ((f"mixer{i}", mixer, mixers), (f"ffn{i}", [ffn_w_up[i].astype(BF16), ffn_w_down[i].astype(BF16)], ffn)):
            if token is not None:
                token, *shards = lax.optimization_barrier((token, *shards))
            token = shards[0]
            into.append(_gather_layer(tag, shards))

    x_i, y_i, c_i = lax.axis_index("x"), lax.axis_index("y"), lax.axis_index("c")
    sel = jnp.stack([2 * x_i + y_i, c_i]).astype(jnp.int32)
    reducer = _Reducer({n: tuple(_as3d(n, d[n]) for d in (w, mom, var)) for n in SHARDED}, sel)
    loss, dx, small = _local_step(
        x[0], positions[0], loss_target[0], norm_mix, norm_ffn, final_norm, mla_q_norm, mla_kv_norm, sgu_w_spatial,
        sgu_b_spatial, mixers, ffn, reducer)
    loss = lax.psum(loss, ("x", "y", "c"))

    small_g = [small["norm_mix"], small["norm_ffn"], small["final_norm"], small["q_norm"], small["kv_norm"],
               small["w_sp"], small["b_sp"]]
    like = [w[n] for n in REPLICATED]
    g_small = _all_reduce_small(_pack(small_g))
    packed = [_pack([d[n] for n in REPLICATED])[None] for d in (w, mom, var)]
    upd_small = _adamw(packed[0], g_small[None], packed[1], packed[2])
    grads = dict(zip(REPLICATED, _unpack(g_small, like)))
    delta, new_m, new_v = ({n: a for n, a in zip(REPLICATED, _unpack(u[0], like))} for u in upd_small)

    reducer.phase_end(reducer.update(upd_small[0]))
    reducer.update(None)
    for n in SHARDED:
        grads[n], delta[n], new_m[n], new_v[n] = (a.reshape(w[n].shape) for a in reducer.done[n])

    return (loss, dx[None], *[grads[n] for n in WEIGHTS], *[delta[n] for n in WEIGHTS],
            *[new_m[n] for n in WEIGHTS], *[new_v[n] for n in WEIGHTS])
```

```python
import functools
import math

import jax
import jax.numpy as jnp
from jax import lax
from jax.experimental import pallas as pl
from jax.experimental.pallas import tpu as pltpu
from jax.experimental.pallas import tpu_sc as plsc

F32 = jnp.float32
BF16 = jnp.bfloat16
MESH = pl.DeviceIdType.MESH

DEPTH = 4
HEADS = 8
NOPE = 128
ROPE = 64
VHEAD = 128
QK_HEAD = NOPE + ROPE
Q_RANK = 256
KV_RANK = 128
HEAD_PAD = 256
LAT_PAD = 512
ROPE_THETA = 10000.0
SGU_CHUNK = 128
SGU_GROUPS = 8
NORM_EPS = 1e-6
LN_EPS = 1e-5
ADAM_LR, ADAM_B1, ADAM_B2, ADAM_EPS, ADAM_WD, ADAM_STEP = 0.001, 0.9, 0.999, 1e-08, 0.01, 10

N_SHARDS = 4
LANE = 128
VMEM_LIMIT_BYTES = 56 * 1024 * 1024
ATT_TILE = 512
MM_TILE = 1024
ATT_SCALE = QK_HEAD ** -0.5
LOG2_SCALE = ATT_SCALE * math.log2(math.e)

NN = (((1,), (0,)), ((), ()))
NT = (((1,), (1,)), ((), ()))
TN = (((0,), (0,)), ((), ()))


def _params(sem):
    return pltpu.CompilerParams(dimension_semantics=sem, vmem_limit_bytes=VMEM_LIMIT_BYTES)


def _tile(n, pref):
    t = min(n, pref)
    while n % t:
        t //= 2
    return t


def _matmul(name, a, b, a_spec, b_spec, dims, grid, tile, outs, extras=(), epilogue=None, aliased=()):
    nk, ne, no = grid[2], len(extras), len(outs)

    def body(a_ref, b_ref, *rest):
        e_refs, o_refs = rest[:ne], rest[ne + len(aliased):ne + len(aliased) + no]
        part = lax.dot_general(a_ref[...].astype(BF16), b_ref[...].astype(BF16), dims, preferred_element_type=F32)

        def finish(acc):
            vals = (acc,) if epilogue is None else epilogue(acc, *[e[...] for e in e_refs])
            for o_ref, v in zip(o_refs, vals):
                o_ref[...] = v.astype(o_ref.dtype)

        if nk == 1:
            finish(part)
            return
        acc_ref, k = rest[-1], pl.program_id(2)

        @pl.when(k == 0)
        def _():
            acc_ref[...] = part

        @pl.when(jnp.logical_and(k > 0, k < nk - 1))
        def _():
            acc_ref[...] += part

        @pl.when(k == nk - 1)
        def _():
            finish(acc_ref[...] + part)

    n_in = 2 + ne
    return pl.pallas_call(
        body, name=name, grid=grid,
        in_specs=[a_spec, b_spec] + [s for _, s in extras] + [pl.BlockSpec(memory_space=pl.ANY)] * len(aliased),
        out_specs=[s for _, s in outs], out_shape=[s for s, _ in outs],
        scratch_shapes=[pltpu.VMEM(tile, F32)] if nk > 1 else [],
        input_output_aliases={n_in + i: o for i, (_, o) in enumerate(aliased)},
        compiler_params=_params(("parallel", "parallel", "arbitrary")),
    )(a, b, *[e for e, _ in extras], *[arr for arr, _ in aliased])


def _mm(name, a, b, out_dtypes=(F32,), epilogue=None, extras=(), tm=MM_TILE, tn=MM_TILE, tk=MM_TILE, nt=False):
    m, kd = a.shape
    n = b.shape[0] if nt else b.shape[1]
    tm, tn, tk = _tile(m, tm), _tile(n, tn), _tile(kd, tk)
    o_spec = pl.BlockSpec((tm, tn), lambda i, j, k: (i, j))
    b_spec = pl.BlockSpec((tn, tk), lambda i, j, k: (j, k)) if nt else pl.BlockSpec((tk, tn), lambda i, j, k: (k, j))
    return _matmul(name, a, b, pl.BlockSpec((tm, tk), lambda i, j, k: (i, k)), b_spec, NT if nt else NN,
                   (m // tm, n // tn, kd // tk), (tm, tn),
                   [(jax.ShapeDtypeStruct((m, n), d), o_spec) for d in out_dtypes],
                   [(e, o_spec) for e in extras], epilogue)


def _mm_tn(name, a, b, out_dtypes=(F32,), tm=MM_TILE, tn=MM_TILE, tk=MM_TILE):
    s, m = a.shape
    n = b.shape[1]
    tm, tn, tk = _tile(m, tm), _tile(n, tn), _tile(s, tk)
    o_spec = pl.BlockSpec((tm, tn), lambda i, j, k: (i, j))
    return _matmul(name, a, b, pl.BlockSpec((tk, tm), lambda i, j, k: (k, i)),
                   pl.BlockSpec((tk, tn), lambda i, j, k: (k, j)), TN, (m // tm, n // tn, s // tk), (tm, tn),
                   [(jax.ShapeDtypeStruct((m, n), d), o_spec) for d in out_dtypes])


def _mm_stacked(name, a, w3, mode, out_dtypes=(F32,), epilogue=None, extras=(), tm=MM_TILE, tn=MM_TILE, tk=MM_TILE):
    m, kd = a.shape
    _, r, c = w3.shape
    n = c if mode == "row" else N_SHARDS * c
    tm = _tile(m, tm)
    if mode == "row":
        tn, tk = _tile(n, tn), _tile(r, tk)
        per = r // tk
        b_spec = pl.BlockSpec((None, tk, tn), lambda i, j, k: (k // per, k % per, j))
    else:
        tn, tk = _tile(c, tn), _tile(kd, tk)
        per = c // tn
        b_spec = pl.BlockSpec((None, tk, tn), lambda i, j, k: (j // per, k, j % per))
    o_spec = pl.BlockSpec((tm, tn), lambda i, j, k: (i, j))
    return _matmul(name, a, w3, pl.BlockSpec((tm, tk), lambda i, j, k: (i, k)), b_spec, NN,
                   (m // tm, n // tn, kd // tk), (tm, tn),
                   [(jax.ShapeDtypeStruct((m, n), d), o_spec) for d in out_dtypes],
                   [(e, o_spec) for e in extras], epilogue)


def _mm_stacked_nt(name, a, w3, mode, out_dtypes=(F32,), epilogue=None, extras=(), tm=MM_TILE, tn=MM_TILE, tk=MM_TILE):
    m, nd = a.shape
    _, r, c = w3.shape
    kout = N_SHARDS * r if mode == "row" else r
    tm = _tile(m, tm)
    if mode == "row":
        tn, tk = _tile(r, tn), _tile(c, tk)
        per = r // tn
        b_spec = pl.BlockSpec((None, tn, tk), lambda i, j, k: (j // per, j % per, k))
    else:
        tn, tk = _tile(r, tn), _tile(c, tk)
        per = c // tk
        b_spec = pl.BlockSpec((None, tn, tk), lambda i, j, k: (k // per, j, k % per))
    o_spec = pl.BlockSpec((tm, tn), lambda i, j, k: (i, j))
    return _matmul(name, a, w3, pl.BlockSpec((tm, tk), lambda i, j, k: (i, k)), b_spec, NT,
                   (m // tm, kout // tn, nd // tk), (tm, tn),
                   [(jax.ShapeDtypeStruct((m, kout), d), o_spec) for d in out_dtypes],
                   [(e, o_spec) for e in extras], epilogue)


def _mm_tn_stacked(name, a, b, shape3, mode, tm=MM_TILE, tn=MM_TILE, tk=MM_TILE):
    s, m = a.shape
    n = b.shape[1]
    _, r, c = shape3
    tk = _tile(s, tk)
    if mode == "row":
        tm, tn = _tile(r, tm), _tile(n, tn)
        per = r // tm
        o_spec = pl.BlockSpec((None, tm, tn), lambda i, j, k: (i // per, i % per, j))
    else:
        tm, tn = _tile(m, tm), _tile(c, tn)
        per = c // tn
        o_spec = pl.BlockSpec((None, tm, tn), lambda i, j, k: (j // per, i, j % per))
    outs = [(jax.ShapeDtypeStruct(shape3, F32), o_spec), (jax.ShapeDtypeStruct(shape3, BF16), o_spec)]
    return _matmul(name, a, b, pl.BlockSpec((tk, tm), lambda i, j, k: (k, i)),
                   pl.BlockSpec((tk, tn), lambda i, j, k: (k, j)), TN, (m // tm, n // tn, s // tk), (tm, tn),
                   outs, epilogue=lambda acc: (acc, acc))


def _rowwise(name, fn, rows, consts, out_rows, out_accs=(), tr=256):
    nr, nc, no = len(rows), len(consts), len(out_rows)
    n_rows = rows[0].shape[0]
    tr = _tile(n_rows, tr)

    def body(*refs):
        vals = fn(*[r[...] for r in refs[:nr + nc]])
        o_refs, a_refs = refs[nr + nc:nr + nc + no], refs[nr + nc + no:]
        for ref, v in zip(o_refs, vals[:no]):
            ref[...] = v.astype(ref.dtype)
        first = pl.program_id(0) == 0

        @pl.when(first)
        def _():
            for ref, v in zip(a_refs, vals[no:]):
                ref[...] = v

        @pl.when(jnp.logical_not(first))
        def _():
            for ref, v in zip(a_refs, vals[no:]):
                ref[...] += v

    def whole(shape):
        return pl.BlockSpec(shape, lambda i: (0,) * len(shape))

    return pl.pallas_call(
        body, name=name, grid=(n_rows // tr,),
        in_specs=[pl.BlockSpec((tr, a.shape[1]), lambda i: (i, 0)) for a in rows] + [whole(c.shape) for c in consts],
        out_specs=[pl.BlockSpec((tr, f), lambda i: (i, 0)) for f, _ in out_rows] + [whole(s) for s in out_accs],
        out_shape=[jax.ShapeDtypeStruct((n_rows, f), d) for f, d in out_rows]
        + [jax.ShapeDtypeStruct(s, F32) for s in out_accs],
        compiler_params=_params(("arbitrary",)),
    )(*rows, *consts)


def _rms_fwd(x, g):
    return x * lax.rsqrt(jnp.mean(x * x, axis=-1, keepdims=True) + NORM_EPS) * g


def _rms_bwd(dy, x, g):
    rstd = lax.rsqrt(jnp.mean(x * x, axis=-1, keepdims=True) + NORM_EPS)
    n = x * rstd
    dn = dy * g
    dx = rstd * (dn - n * jnp.mean(dn * n, axis=-1, keepdims=True))
    return dx, jnp.sum(dy * n, axis=0, keepdims=True)


def _rope(x, cs, s1, s2):
    return x * cs + pltpu.roll(x, 32, 1) * s1 + pltpu.roll(x, 96, 1) * s2


def _rope_t(dy, cs, s1, s2):
    return dy * cs + pltpu.roll(dy * s1, 96, 1) + pltpu.roll(dy * s2, 32, 1)


def _gelu(z):
    return 0.5 * z * (1.0 + lax.erf(z * (1.0 / math.sqrt(2.0))))


def _gelu_grad(z):
    return 0.5 * (1.0 + lax.erf(z * (1.0 / math.sqrt(2.0)))) + z * jnp.exp(-0.5 * z * z) * (1.0 / math.sqrt(2.0 * math.pi))


def _att_scores(q, kv, kr, scale, masked, transposed):
    k = jnp.concatenate([kv[:, :NOPE], kr], axis=1)
    if transposed:
        s = lax.dot_general(k, q, NT, preferred_element_type=F32) * scale
    else:
        s = lax.dot_general(q, k, NT, preferred_element_type=F32) * scale
    if masked:
        r = lax.broadcasted_iota(jnp.int32, s.shape, 0)
        c = lax.broadcasted_iota(jnp.int32, s.shape, 1)
        s = jnp.where((r <= c) if transposed else (c <= r), s, -jnp.inf)
    return s, k


def _flash_fwd(qb, kvb, krb):
    s_len = qb.shape[0]
    t = ATT_TILE

    def body(q_ref, kv_ref, kr_ref, o_ref, lse_ref, m_s, l_s, acc_s):
        qi = pl.program_id(1)
        m_s[...] = jnp.full_like(m_s, -jnp.inf)
        l_s[...] = jnp.zeros_like(l_s)
        acc_s[...] = jnp.zeros_like(acc_s)
        q = q_ref[...]

        def step(ki, masked):
            rows = pl.ds(pl.multiple_of(ki * t, t), t)
            kv = kv_ref[rows, :]
            s, _ = _att_scores(q, kv, kr_ref[rows, :], LOG2_SCALE, masked, False)
            m_prev = m_s[...]
            m_new = jnp.maximum(m_prev, jnp.max(s, axis=1, keepdims=True))
            alpha = jnp.exp2(m_prev - m_new)
            p = jnp.exp2(s - jnp.tile(m_new, (1, t // LANE)))
            l_s[...] = alpha * l_s[...] + jnp.sum(p, axis=1, keepdims=True)
            acc_s[...] = alpha * acc_s[...] + jnp.dot(p.astype(BF16), kv[:, NOPE:], preferred_element_type=F32)
            m_s[...] = m_new

        def unmasked(ki, carry):
            step(ki, False)
            return carry

        lax.fori_loop(0, qi, unmasked, 0)
        step(qi, True)
        o_ref[...] = (acc_s[...] / l_s[...]).astype(o_ref.dtype)
        lse_ref[...] = (m_s[...] + jnp.log2(l_s[...]))[:, :1]

    return pl.pallas_call(
        body, name="flash_fwd", grid=(HEADS, s_len // t),
        in_specs=[pl.BlockSpec((t, HEAD_PAD), lambda h, qi: (qi, h)),
                  pl.BlockSpec((s_len, HEAD_PAD), lambda h, qi: (0, h)),
                  pl.BlockSpec((s_len, LANE), lambda h, qi: (0, 0))],
        out_specs=[pl.BlockSpec((t, VHEAD), lambda h, qi: (qi, h)),
                   pl.BlockSpec((None, t, 1), lambda h, qi: (h, qi, 0))],
        out_shape=[jax.ShapeDtypeStruct((s_len, HEADS * VHEAD), BF16),
                   jax.ShapeDtypeStruct((HEADS, s_len, 1), F32)],
        scratch_shapes=[pltpu.VMEM((t, LANE), F32), pltpu.VMEM((t, LANE), F32), pltpu.VMEM((t, VHEAD), F32)],
        compiler_params=_params(("parallel", "arbitrary")),
    )(qb, kvb, krb)


def _flash_bwd_dq(qb, kvb, krb, dob, lse, delta):
    s_len = qb.shape[0]
    t = ATT_TILE
    nq = s_len // t
    scale = QK_HEAD ** -0.5

    def body(q_ref, kv_ref, kr_ref, do_ref, lse_ref, dl_ref, dq_ref, acc_s):
        qi = pl.program_id(1)
        acc_s[...] = jnp.zeros_like(acc_s)
        q, do = q_ref[...], do_ref[...]
        lse = jnp.broadcast_to(lse_ref[...], (t, LANE))
        dl = jnp.broadcast_to(dl_ref[...], (t, LANE))

        def step(ki, masked):
            rows = pl.ds(pl.multiple_of(ki * t, t), t)
            kv = kv_ref[rows, :]
            s, k = _att_scores(q, kv, kr_ref[rows, :], LOG2_SCALE, masked, False)
            p = jnp.exp2(s - jnp.tile(lse, (1, t // LANE)))
            dp = lax.dot_general(do, kv[:, NOPE:], NT, preferred_element_type=F32)
            ds = (p * (dp - jnp.tile(dl, (1, t // LANE))) * scale).astype(BF16)
            acc_s[...] += jnp.dot(ds, k, preferred_element_type=F32)

        def unmasked(ki, carry):
            step(ki, False)
            return carry

        lax.fori_loop(0, qi, unmasked, 0)
        step(qi, True)
        dq_ref[...] = acc_s[...]

    col = pl.BlockSpec((None, t, 1), lambda h, qi: (h, qi, 0))
    return pl.pallas_call(
        body, name="flash_bwd_dq", grid=(HEADS, nq),
        in_specs=[pl.BlockSpec((t, HEAD_PAD), lambda h, qi: (qi, h)),
                  pl.BlockSpec((s_len, HEAD_PAD), lambda h, qi: (0, h)),
                  pl.BlockSpec((s_len, LANE), lambda h, qi: (0, 0)),
                  pl.BlockSpec((t, VHEAD), lambda h, qi: (qi, h)), col, col],
        out_specs=pl.BlockSpec((t, HEAD_PAD), lambda h, qi: (qi, h)),
        out_shape=jax.ShapeDtypeStruct((s_len, HEADS * HEAD_PAD), F32),
        scratch_shapes=[pltpu.VMEM((t, HEAD_PAD), F32)],
        compiler_params=_params(("parallel", "arbitrary")),
    )(qb, kvb, krb, dob, lse, delta)


def _flash_bwd_dkv(qb, kvb, krb, dob, lse_row, delta_row):
    s_len = qb.shape[0]
    t = ATT_TILE
    nq = s_len // t
    scale = QK_HEAD ** -0.5

    def body(q_ref, kv_ref, kr_ref, do_ref, lse_ref, dl_ref, dkv_ref, dkr_ref, dk_s, dv_s):
        ki = pl.program_id(1)
        dk_s[...] = jnp.zeros_like(dk_s)
        dv_s[...] = jnp.zeros_like(dv_s)
        kv, kr = kv_ref[...], kr_ref[...]

        def step(qi, masked):
            rows = pl.ds(pl.multiple_of(qi * t, t), t)
            q, do = q_ref[rows, :], do_ref[rows, :]
            st, _ = _att_scores(q, kv, kr, LOG2_SCALE, masked, True)
            pt = jnp.exp2(st - lse_ref[:, rows])
            dv_s[...] += jnp.dot(pt.astype(BF16), do, preferred_element_type=F32)
            dpt = lax.dot_general(kv[:, NOPE:], do, NT, preferred_element_type=F32)
            dst = (pt * (dpt - dl_ref[:, rows]) * scale).astype(BF16)
            dk_s[...] += jnp.dot(dst, q, preferred_element_type=F32)

        def unmasked(qi, carry):
            step(qi, False)
            return carry

        step(ki, True)
        lax.fori_loop(ki + 1, nq, unmasked, 0)
        dk = dk_s[...]
        dkv_ref[...] = jnp.concatenate([dk[:, :NOPE], dv_s[...]], axis=1).astype(dkv_ref.dtype)
        dkr_ref[...] = dk[:, NOPE:]

    row = pl.BlockSpec((None, 1, s_len), lambda h, ki: (h, 0, 0))
    return pl.pallas_call(
        body, name="flash_bwd_dkv", grid=(HEADS, nq),
        in_specs=[pl.BlockSpec((s_len, HEAD_PAD), lambda h, ki: (0, h)),
                  pl.BlockSpec((t, HEAD_PAD), lambda h, ki: (ki, h)),
                  pl.BlockSpec((t, LANE), lambda h, ki: (ki, 0)),
                  pl.BlockSpec((s_len, VHEAD), lambda h, ki: (0, h)), row, row],
        out_specs=[pl.BlockSpec((t, HEAD_PAD), lambda h, ki: (ki, h)),
                   pl.BlockSpec((t, LANE), lambda h, ki: (ki, h))],
        out_shape=[jax.ShapeDtypeStruct((s_len, HEADS * HEAD_PAD), BF16),
                   jax.ShapeDtypeStruct((s_len, HEADS * LANE), F32)],
        scratch_shapes=[pltpu.VMEM((t, HEAD_PAD), F32), pltpu.VMEM((t, VHEAD), F32)],
        compiler_params=_params(("parallel", "parallel")),
    )(qb, kvb, krb, dob, lse_row, delta_row)


def _att_delta(do, ob):
    s_len = do.shape[0]
    t = ATT_TILE

    def body(do_ref, o_ref, dob_ref, dl_ref):
        d = do_ref[...]
        dob_ref[...] = d.astype(dob_ref.dtype)
        dl_ref[...] = jnp.sum(d * o_ref[...].astype(F32), axis=1, keepdims=True)

    blk = pl.BlockSpec((t, VHEAD), lambda i, h: (i, h))
    return pl.pallas_call(
        body, name="att_delta", grid=(s_len // t, HEADS), in_specs=[blk, blk],
        out_specs=[blk, pl.BlockSpec((None, t, 1), lambda i, h: (h, i, 0))],
        out_shape=[jax.ShapeDtypeStruct(do.shape, BF16), jax.ShapeDtypeStruct((HEADS, s_len, 1), F32)],
        compiler_params=_params(("parallel", "parallel")),
    )(do, ob)


def _tril(w):
    r = lax.broadcasted_iota(jnp.int32, w.shape, 0)
    c = lax.broadcasted_iota(jnp.int32, w.shape, 1)
    return jnp.where(c <= r, w, 0.0)


def _sgu_mix(w_ref, vln, gd):
    wcs = [_tril(w_ref[g]).astype(BF16) for g in range(SGU_GROUPS)]
    mixed = jnp.concatenate(
        [jnp.dot(wcs[g], vln[:, g * gd:(g + 1) * gd], preferred_element_type=F32) for g in range(SGU_GROUPS)], axis=1)
    return wcs, mixed


def _sgu_fwd(zpre, ln_g, ln_b, w_sp, bias_full):
    s_len, two_w = zpre.shape
    width = two_w // 2
    gd = width // SGU_GROUPS
    t = SGU_CHUNK

    def body(z_ref, g_ref, b_ref, w_ref, bias_ref, uv_ref):
        u = _gelu(z_ref[:, :width])
        v = _gelu(z_ref[:, width:])
        d = v - jnp.mean(v, axis=-1, keepdims=True)
        vhat = d * lax.rsqrt(jnp.mean(d * d, axis=-1, keepdims=True) + LN_EPS)
        vln = (vhat * g_ref[...] + b_ref[...]).astype(BF16)
        _, mixed = _sgu_mix(w_ref, vln, gd)
        uv_ref[...] = (u * (mixed + bias_ref[...])).astype(uv_ref.dtype)

    return pl.pallas_call(
        body, name="sgu_fwd", grid=(s_len // t,),
        in_specs=[pl.BlockSpec((t, two_w), lambda i: (i, 0)), pl.BlockSpec((1, width), lambda i: (0, 0)),
                  pl.BlockSpec((1, width), lambda i: (0, 0)), pl.BlockSpec(w_sp.shape, lambda i: (0, 0, 0)),
                  pl.BlockSpec((t, width), lambda i: (0, 0))],
        out_specs=pl.BlockSpec((t, width), lambda i: (i, 0)),
        out_shape=jax.ShapeDtypeStruct((s_len, width), BF16),
        compiler_params=_params(("parallel",)),
    )(zpre, ln_g, ln_b, w_sp, bias_full)


def _sgu_bwd(zpre, duv, ln_g, ln_b, w_sp, bias_full):
    s_len, two_w = zpre.shape
    width = two_w // 2
    gd = width // SGU_GROUPS
    t = SGU_CHUNK

    def body(z_ref, duv_ref, g_ref, b_ref, w_ref, bias_ref, dz_ref, dg_ref, db_ref, dw_ref, dbias_ref):
        first = pl.program_id(0) == 0

        def accumulate(ref, val):
            @pl.when(first)
            def _():
                ref[...] = val

            @pl.when(jnp.logical_not(first))
            def _():
                ref[...] += val

        zu, zv = z_ref[:, :width], z_ref[:, width:]
        u = _gelu(zu)
        v = _gelu(zv)
        d = v - jnp.mean(v, axis=-1, keepdims=True)
        rstd = lax.rsqrt(jnp.mean(d * d, axis=-1, keepdims=True) + LN_EPS)
        vhat = d * rstd
        vln = (vhat * g_ref[...] + b_ref[...]).astype(BF16)
        wcs, mixed = _sgu_mix(w_ref, vln, gd)
        duv_v = duv_ref[...]
        du = duv_v * (mixed + bias_ref[...])
        dmixed = duv_v * u
        dmb = dmixed.astype(BF16)
        dvln = jnp.concatenate(
            [lax.dot_general(wcs[g], dmb[:, g * gd:(g + 1) * gd], TN, preferred_element_type=F32)
             for g in range(SGU_GROUPS)], axis=1)
        for g in range(SGU_GROUPS):
            dw = lax.dot_general(dmb[:, g * gd:(g + 1) * gd], vln[:, g * gd:(g + 1) * gd], NT, preferred_element_type=F32)
            accumulate(dw_ref.at[g], _tril(dw))
        dvhat = dvln * g_ref[...]
        dv0 = rstd * (dvhat - jnp.mean(dvhat, axis=-1, keepdims=True)
                      - vhat * jnp.mean(dvhat * vhat, axis=-1, keepdims=True))
        dz_ref[:, :width] = (du * _gelu_grad(zu)).astype(dz_ref.dtype)
        dz_ref[:, width:] = (dv0 * _gelu_grad(zv)).astype(dz_ref.dtype)
        accumulate(dg_ref, jnp.sum(dvln * vhat, axis=0, keepdims=True))
        accumulate(db_ref, jnp.sum(dvln, axis=0, keepdims=True))
        accumulate(dbias_ref, dmixed)

    vec = pl.BlockSpec((1, width), lambda i: (0, 0))
    return pl.pallas_call(
        body, name="sgu_bwd", grid=(s_len // t,),
        in_specs=[pl.BlockSpec((t, two_w), lambda i: (i, 0)), pl.BlockSpec((t, width), lambda i: (i, 0)), vec, vec,
                  pl.BlockSpec(w_sp.shape, lambda i: (0, 0, 0)), pl.BlockSpec((t, width), lambda i: (0, 0))],
        out_specs=[pl.BlockSpec((t, two_w), lambda i: (i, 0)), vec, vec,
                   pl.BlockSpec(w_sp.shape, lambda i: (0, 0, 0)), pl.BlockSpec((t, width), lambda i: (0, 0))],
        out_shape=[jax.ShapeDtypeStruct((s_len, two_w), BF16), jax.ShapeDtypeStruct((1, width), F32),
                   jax.ShapeDtypeStruct((1, width), F32), jax.ShapeDtypeStruct(w_sp.shape, F32),
                   jax.ShapeDtypeStruct((t, width), F32)],
        compiler_params=_params(("arbitrary",)),
    )(zpre, duv, ln_g, ln_b, w_sp, bias_full)


def _rope_tables(positions):
    inv_freq = ROPE_THETA ** (-jnp.arange(0, ROPE, 2, dtype=F32) / ROPE)
    ang = positions.astype(F32)[:, None] * inv_freq
    cos, sin = jnp.cos(ang), jnp.sin(ang)
    z32, z64 = jnp.zeros_like(cos), jnp.zeros((cos.shape[0], LANE - ROPE), F32)
    return (jnp.concatenate([cos, cos, z64], axis=1), jnp.concatenate([z32, sin, z64], axis=1),
            jnp.concatenate([-sin, z32, z64], axis=1))


def _q_rope(name, q, tables, transpose):
    rot = _rope_t if transpose else _rope

    def fn(qv, cs, s1, s2):
        parts = []
        for h in range(HEADS):
            parts.append(qv[:, h * HEAD_PAD:h * HEAD_PAD + NOPE])
            parts.append(rot(qv[:, h * HEAD_PAD + NOPE:(h + 1) * HEAD_PAD], cs, s1, s2))
        return (jnp.concatenate(parts, axis=1),)

    return _rowwise(name, fn, [q, *tables], [], [(q.shape[1], BF16)])[0]


def _ffn_fwd(x, g, w_up3, w_down3):
    h2 = _rowwise("ffn_norm", lambda xv, gv: (_rms_fwd(xv, gv),), [x], [g], [(x.shape[1], BF16)])[0]

    def sq_relu(acc):
        r = jnp.maximum(acc, 0.0)
        return (r * r,)

    r = _mm_stacked("ffn_up", h2, w_up3, "col", (BF16,), sq_relu)[0]
    x_out = _mm_stacked("ffn_down", r, w_down3, "row", (F32,), lambda acc, res: (acc + res,), [x])[0]
    return x_out, (x, h2, r)


def _ffn_bwd(dx, dxb, saved, g, w_up3, w_down3):
    x, h2, r = saved
    da = _mm_stacked_nt("ffn_down_dx", dxb, w_down3, "row", (BF16,),
                        lambda acc, rv: (acc * (2.0 * jnp.sqrt(rv.astype(F32))),), [r])[0]
    g_down = _mm_tn_stacked("ffn_down_dw", r, dxb, w_down3.shape, "row")
    dh2 = _mm_stacked_nt("ffn_up_dx", da, w_up3, "col", (F32,))[0]
    g_up = _mm_tn_stacked("ffn_up_dw", h2, da, w_up3.shape, "col")
    dx, dxb, dg = _norm_bwd("ffn_norm_bwd", dh2, x, g, dx)
    return dx, dxb, dg, g_up, g_down


def _norm_bwd(name, dh, x, g, dres):
    def fn(dhv, xv, rv, gv):
        dxv, dg = _rms_bwd(dhv, xv, gv)
        return dxv + rv, dxv + rv, dg

    return _rowwise(name, fn, [dh, x, dres], [g], [(x.shape[1], F32), (x.shape[1], BF16)], [g.shape])


def _mla_fwd(x, g, wdkv, q_norm, kv_norm, wq, wkv, wo, tables):
    d = x.shape[1]
    h = _rowwise("mla_norm", lambda xv, gv: (_rms_fwd(xv, gv),), [x], [g], [(d, BF16)])[0]
    lat = _mm("mla_dkv", h, wdkv)[0]

    def lat_post(lv, cs, s1, s2, qg, kg):
        return (_rms_fwd(lv[:, :Q_RANK], qg), _rms_fwd(lv[:, Q_RANK:Q_RANK + KV_RANK], kg),
                _rope(lv[:, Q_RANK + KV_RANK:], cs, s1, s2))

    cqn, ckvn, krb = _rowwise("mla_lat", lat_post, [lat, *tables], [q_norm, kv_norm],
                              [(Q_RANK, BF16), (KV_RANK, BF16), (LANE, BF16)])
    q = _mm("mla_uq", cqn, wq)[0]
    kvb = _mm("mla_ukv", ckvn, wkv, (BF16,))[0]
    qb = _q_rope("mla_q_rope", q, tables, False)
    ob, lse = _flash_fwd(qb, kvb, krb)
    x_mid = _mm("mla_o", ob, wo, (F32,), lambda acc, res: (acc + res,), [x])[0]
    return x_mid, (x, h, lat, cqn, ckvn, krb, qb, kvb, ob, lse)


def _mla_bwd(dx, dxb, saved, g, wdkv, q_norm, kv_norm, wq, wkv, wo, tables):
    x, h, lat, cqn, ckvn, krb, qb, kvb, ob, lse = saved
    s_len = x.shape[0]
    do = _mm("mla_o_dx", dxb, wo, nt=True)[0]
    g_wo = _mm_tn("mla_o_dw", ob, dxb)[0]
    dob, delta = _att_delta(do, ob)
    dq = _flash_bwd_dq(qb, kvb, krb, dob, lse, delta)
    dkvb, dkr = _flash_bwd_dkv(qb, kvb, krb, dob, lse.reshape(HEADS, 1, s_len), delta.reshape(HEADS, 1, s_len))
    dqb = _q_rope("mla_q_rope_bwd", dq, tables, True)
    dcqn = _mm("mla_uq_dx", dqb, wq, nt=True)[0]
    g_wq = _mm_tn("mla_uq_dw", cqn, dqb)[0]
    dckvn = _mm("mla_ukv_dx", dkvb, wkv, nt=True)[0]
    g_wkv = _mm_tn("mla_ukv_dw", ckvn, dkvb)[0]

    def lat_bwd(dq_v, dkv_v, dkr_v, lv, cs, s1, s2, qg, kg):
        dcq, dqg = _rms_bwd(dq_v, lv[:, :Q_RANK], qg)
        dckv, dkg = _rms_bwd(dkv_v, lv[:, Q_RANK:Q_RANK + KV_RANK], kg)
        dkr_sum = dkr_v[:, :LANE]
        for hd in range(1, HEADS):
            dkr_sum = dkr_sum + dkr_v[:, hd * LANE:(hd + 1) * LANE]
        return jnp.concatenate([dcq, dckv, _rope_t(dkr_sum, cs, s1, s2)], axis=1), dqg, dkg

    dlat, g_qn, g_kvn = _rowwise("mla_lat_bwd", lat_bwd, [dcqn, dckvn, dkr, lat, *tables], [q_norm, kv_norm],
                                 [(LAT_PAD, BF16)], [q_norm.shape, kv_norm.shape])
    dh = _mm("mla_dkv_dx", dlat, wdkv, nt=True)[0]
    g_wdkv = _mm_tn("mla_dkv_dw", h, dlat)[0]
    dx, dxb, dg = _norm_bwd("mla_norm_bwd", dh, x, g, dx)
    return dx, dxb, dg, g_wdkv, g_qn, g_kvn, g_wq, g_wkv, g_wo


def _sgu_layer_fwd(x, g, w_in3, ln_g, ln_b, w_sp, bias_full, w_out3):
    h = _rowwise("sgu_norm", lambda xv, gv: (_rms_fwd(xv, gv),), [x], [g], [(x.shape[1], BF16)])[0]
    zpre = _mm_stacked("sgu_in", h, w_in3, "col")[0]
    uv = _sgu_fwd(zpre, ln_g, ln_b, w_sp, bias_full)
    x_mid = _mm_stacked("sgu_out", uv, w_out3, "row", (F32,), lambda acc, res: (acc + res,), [x])[0]
    return x_mid, (x, h, zpre, uv)


def _sgu_layer_bwd(dx, dxb, saved, g, w_in3, ln_g, ln_b, w_sp, bias_full, w_out3):
    x, h, zpre, uv = saved
    duv = _mm_stacked_nt("sgu_out_dx", dxb, w_out3, "row")[0]
    g_out = _mm_tn_stacked("sgu_out_dw", uv, dxb, w_out3.shape, "row")
    dz, g_lng, g_lnb, g_wsp, g_bias = _sgu_bwd(zpre, duv, ln_g, ln_b, w_sp, bias_full)
    dh = _mm_stacked_nt("sgu_in_dx", dz, w_in3, "col")[0]
    g_in = _mm_tn_stacked("sgu_in_dw", h, dz, w_in3.shape, "col")
    dx, dxb, dg = _norm_bwd("sgu_norm_bwd", dh, x, g, dx)
    return dx, dxb, dg, g_in, g_out, g_lng, g_lnb, g_wsp, g_bias


def _loss_head(x, target, g):
    d = x.shape[1]

    def fn(xv, tv, gv):
        err = _rms_fwd(xv, gv) - tv
        dxv, dg = _rms_bwd(err * (1.0 / d), xv, gv)
        return dxv, dxv, dg, jnp.sum(err * err, axis=0, keepdims=True)

    return _rowwise("loss_head", fn, [x, target], [g], [(d, F32), (d, BF16)], [g.shape, g.shape])


def _mixer_weights(i, stacks):
    by_rows = lambda a: a.reshape(N_SHARDS * a.shape[1], a.shape[2])
    by_cols = lambda a: a.transpose(1, 0, 2).reshape(a.shape[1], N_SHARDS * a.shape[2])
    if i % 2:
        w_in3, w_out3, ln_g, ln_b = stacks
        return w_in3, ln_g.reshape(1, -1), ln_b.reshape(1, -1), w_out3
    wdkv = by_rows(stacks[0])
    wdkv = jnp.pad(wdkv, ((0, 0), (0, LAT_PAD - wdkv.shape[1])))
    wq = jnp.pad(by_cols(stacks[1]).reshape(Q_RANK, HEADS, QK_HEAD), ((0, 0), (0, 0), (0, HEAD_PAD - QK_HEAD)))
    return wdkv, wq.reshape(Q_RANK, HEADS * HEAD_PAD), by_cols(stacks[2]), by_rows(stacks[3])


def _local_step(x, positions, target, norm_mix, norm_ffn, final_norm, q_norm, kv_norm, w_sp, b_sp, mixers, ffn, reducer):
    tables = _rope_tables(positions)
    gd = mixers[1][2].size // SGU_GROUPS
    bias_full = [jnp.repeat(b_sp[j].T, gd, axis=1) for j in range(DEPTH // 2)]
    saved, mla, sgu = [], [None] * (DEPTH // 2), [None] * (DEPTH // 2)
    for i in range(DEPTH):
        j = i // 2
        x, *stacks = lax.optimization_barrier((x, *mixers[i]))
        if i % 2 == 0:
            wdkv, wq, wkv, wo = mla[j] = _mixer_weights(i, stacks)
            x, s_mix = _mla_fwd(x, norm_mix[i:i + 1], wdkv, q_norm[j:j + 1], kv_norm[j:j + 1], wq, wkv, wo, tables)
        else:
            w_in3, ln_g, ln_b, w_out3 = sgu[j] = _mixer_weights(i, stacks)
            x, s_mix = _sgu_layer_fwd(x, norm_mix[i:i + 1], w_in3, ln_g, ln_b, w_sp[j], bias_full[j], w_out3)
        x, s_ffn = _ffn_fwd(x, norm_ffn[i:i + 1], *ffn[i])
        saved.append((s_mix, s_ffn))
    dx, dxb, g_final, sq_cols = _loss_head(x, target, final_norm[None, :])
    loss = 0.5 * jnp.sum(sq_cols) / x.shape[1]

    def pair(g):
        return g, g.astype(BF16)

    g_mix, g_ffn = [None] * DEPTH, [None] * DEPTH
    mla_g, sgu_g = [None] * (DEPTH // 2), [None] * (DEPTH // 2)
    for i in reversed(range(DEPTH)):
        j = i // 2
        s_mix, s_ffn = saved[i]
        dx, dxb, g_ffn[i], g_up, g_down = _ffn_bwd(dx, dxb, s_ffn, norm_ffn[i:i + 1], *ffn[i])
        dxb = reducer.add(f"ffn{i}", i, {"ffn_w_up": g_up, "ffn_w_down": g_down}, dxb)
        dxb = reducer.phase_end(dxb)
        if i % 2 == 0:
            wdkv, wq, wkv, wo = mla[j]
            dx, dxb, g_mix[i], g_wdkv, g_qn, g_kvn, g_wq, g_wkv, g_wo = _mla_bwd(
                dx, dxb, s_mix, norm_mix[i:i + 1], wdkv, q_norm[j:j + 1], kv_norm[j:j + 1], wq, wkv, wo, tables)
            mla_g[j] = (g_qn, g_kvn)
            g_wq = g_wq.reshape(Q_RANK, HEADS, HEAD_PAD)[..., :QK_HEAD].reshape(Q_RANK, N_SHARDS, -1)
            dxb = reducer.add(f"mla{j}", j, {
                "mla_w_dkv": pair(g_wdkv[:, :Q_RANK + KV_RANK + ROPE].reshape(N_SHARDS, -1, Q_RANK + KV_RANK + ROPE)),
                "mla_w_uq": pair(g_wq.transpose(1, 0, 2)),
                "mla_w_ukv": pair(g_wkv.reshape(KV_RANK, N_SHARDS, -1).transpose(1, 0, 2)),
                "mla_w_o": pair(g_wo.reshape(N_SHARDS, -1, g_wo.shape[1]))}, dxb)
        else:
            w_in3, ln_g, ln_b, w_out3 = sgu[j]
            dx, dxb, g_mix[i], g_in, g_out, g_lng, g_lnb, g_wsp, g_bias = _sgu_layer_bwd(
                dx, dxb, s_mix, norm_mix[i:i + 1], w_in3, ln_g, ln_b, w_sp[j], bias_full[j], w_out3)
            sgu_g[j] = (g_wsp, g_bias.reshape(SGU_CHUNK, SGU_GROUPS, gd).sum(axis=-1).T)
            dxb = reducer.add(f"sgu{j}", j, {"sgu_w_in": g_in, "sgu_w_out": g_out,
                                             "sgu_ln_g": pair(g_lng.reshape(N_SHARDS, -1, LANE)),
                                             "sgu_ln_b": pair(g_lnb.reshape(N_SHARDS, -1, LANE))}, dxb)
        dxb = reducer.phase_end(dxb)
    small = dict(
        norm_mix=jnp.concatenate(g_mix, axis=0), norm_ffn=jnp.concatenate(g_ffn, axis=0), final_norm=g_final[0],
        q_norm=jnp.concatenate([m[0] for m in mla_g], axis=0), kv_norm=jnp.concatenate([m[1] for m in mla_g], axis=0),
        w_sp=jnp.stack([s[0] for s in sgu_g]), b_sp=jnp.stack([s[1] for s in sgu_g]))
    return loss, dx, small, dxb


HBM_SPEC = pl.BlockSpec(memory_space=pltpu.HBM)


def _place():
    x, y, c = lax.axis_index("x"), lax.axis_index("y"), lax.axis_index("c")
    return x, y, c, [(1 - x, y), (x, 1 - y), (1 - x, 1 - y)]


def _remote(src, dst, send_sems, recv_sems, k, to):
    return pltpu.make_async_remote_copy(src_ref=src, dst_ref=dst, send_sem=send_sems.at[k], recv_sem=recv_sems.at[k],
                                        device_id=to, device_id_type=MESH)


def _gather_layer(tag, shards):
    n = len(shards)
    split = [s.shape[0] >= 16 for s in shards]

    def body(*refs):
        ins, outs = refs[:n], refs[n:2 * n]
        send_sems, recv_sems, local_sems = refs[2 * n:]
        x, y, c, chips = _place()
        mine = 2 * x + y
        barrier = pltpu.get_barrier_semaphore()
        peers = [(x, y, 1 - c)] + [(*chip, c) for chip in chips]
        for peer in peers:
            pl.semaphore_signal(barrier, inc=1, device_id=peer, device_id_type=MESH)
        pl.semaphore_wait(barrier, len(peers))

        def rows(t, half):
            hr = shards[t].shape[0] // 2
            return pl.ds(half * hr, hr) if split[t] else pl.ds(0, shards[t].shape[0])

        local, sent = [], []
        for t in range(n):
            local.append(pltpu.make_async_copy(ins[t], outs[t].at[mine], local_sems.at[t]))
            local[-1].start()
            for j, chip in enumerate(chips):
                cp = _remote(ins[t].at[rows(t, c)], outs[t].at[mine, rows(t, c)], send_sems, recv_sems, 3 * t + j, (*chip, c))
                cp.start()
                sent.append(cp)
        for j, chip in enumerate(chips):
            theirs = 2 * chip[0] + chip[1]
            for t in range(n):
                piece = outs[t].at[theirs, rows(t, c)]
                _remote(piece, piece, send_sems, recv_sems, 3 * t + j, (x, y, c)).wait_recv()
                if split[t]:
                    cp = _remote(piece, piece, send_sems, recv_sems, 3 * n + 3 * t + j, (x, y, 1 - c))
                    cp.start()
                    sent.append(cp)
        for j, chip in enumerate(chips):
            theirs = 2 * chip[0] + chip[1]
            for t in range(n):
                if split[t]:
                    piece = outs[t].at[theirs, rows(t, 1 - c)]
                    _remote(piece, piece, send_sems, recv_sems, 3 * n + 3 * t + j, (x, y, c)).wait_recv()
        for cp in sent:
            cp.wait_send()
        for cp in local:
            cp.wait()

    return pl.kernel(
        body, name=f"gather_{tag}", mesh=plsc.ScalarSubcoreMesh(axis_name="sequencer", num_cores=1),
        out_type=[jax.ShapeDtypeStruct((N_SHARDS, *s.shape), s.dtype) for s in shards],
        scratch_types=[pltpu.SemaphoreType.DMA((6 * n,)), pltpu.SemaphoreType.DMA((6 * n,)), pltpu.SemaphoreType.DMA((n,))],
        compiler_params=pltpu.CompilerParams(collective_id=ID_GATHER),
    )(*shards)


SEQUENCER = dict(axis_name="sequencer", num_cores=1)
ID_GATHER, ID_EXCHANGE, ID_SHARE = 0, 1, 2
MIN_SPLIT_ROWS = 16


def _handshake(peers):
    barrier = pltpu.get_barrier_semaphore()
    for peer in peers:
        pl.semaphore_signal(barrier, inc=1, device_id=peer, device_id_type=MESH)
    pl.semaphore_wait(barrier, len(peers))


def _half_rows(rows, half):
    return pl.ds(half * (rows // 2), rows // 2) if rows >= MIN_SPLIT_ROWS else pl.ds(0, rows)


def _exchange_partials(tag, stacks):
    n = len(stacks)

    def body(*refs):
        ins, outs, send_sems, recv_sems = refs[:n], refs[n:2 * n], refs[2 * n], refs[2 * n + 1]
        x, y, c, chips = _place()
        mine = 2 * x + y
        _handshake([(x, y, 1 - c)] + [(*chip, c) for chip in chips] + [(*chip, 1 - c) for chip in chips])
        sent = []
        for t in range(n):
            r = stacks[t].shape[1]
            sent.append(_remote(ins[t].at[mine, _half_rows(r, 1 - c)], outs[t].at[0], send_sems, recv_sems, 7 * t, (x, y, 1 - c)))
            for j, chip in enumerate(chips):
                theirs = 2 * chip[0] + chip[1]
                sent.append(_remote(ins[t].at[theirs, _half_rows(r, c)], outs[t].at[1 + j], send_sems, recv_sems,
                                    7 * t + 1 + j, (*chip, c)))
                sent.append(_remote(ins[t].at[theirs, _half_rows(r, 1 - c)], outs[t].at[4 + j], send_sems, recv_sems,
                                    7 * t + 4 + j, (*chip, 1 - c)))
        for cp in sent:
            cp.start()
        for cp in sent:
            cp.wait_send()
        for t in range(n):
            for k in range(7):
                _remote(outs[t].at[k], outs[t].at[k], send_sems, recv_sems, 7 * t + k, (x, y, c)).wait_recv()

    def landing(s):
        return (7, s.shape[1] // 2 if s.shape[1] >= MIN_SPLIT_ROWS else s.shape[1], s.shape[2])

    return pl.kernel(
        body, name=f"reduce_exchange_{tag}", mesh=plsc.ScalarSubcoreMesh(**SEQUENCER),
        out_type=[jax.ShapeDtypeStruct(landing(s), s.dtype) for s in stacks],
        scratch_types=[pltpu.SemaphoreType.DMA((7 * n,)), pltpu.SemaphoreType.DMA((7 * n,))],
        compiler_params=pltpu.CompilerParams(collective_id=ID_EXCHANGE),
    )(*stacks)


def _share_halves(tag, halves):
    n = len(halves)

    def body(*refs):
        ins, outs, send_sems, recv_sems = refs[:n], refs[n:2 * n], refs[2 * n], refs[2 * n + 1]
        x, y, c, _ = _place()
        _handshake([(x, y, 1 - c)])
        sent = [_remote(ins[t], outs[t], send_sems, recv_sems, t, (x, y, 1 - c)) for t in range(n)]
        for cp in sent:
            cp.start()
        for cp in sent:
            cp.wait()

    return pl.kernel(
        body, name=f"reduce_share_{tag}", mesh=plsc.ScalarSubcoreMesh(**SEQUENCER),
        out_type=[jax.ShapeDtypeStruct(h.shape, h.dtype) for h in halves],
        scratch_types=[pltpu.SemaphoreType.DMA((n,)), pltpu.SemaphoreType.DMA((n,))],
        compiler_params=pltpu.CompilerParams(collective_id=ID_SHARE),
    )(*halves)


def _all_reduce_small(part):
    rows = part.shape[0]

    def body(p_ref, out_ref, sib_buf, chip_sums, send_sems, recv_sems):
        x, y, c, chips = _place()
        mine = 2 * x + y
        swap = _remote(p_ref, sib_buf, send_sems, recv_sems, 0, (x, y, 1 - c))
        swap.start()
        swap.wait()
        chip_sums[mine] = p_ref[...] + sib_buf[...]
        sent = [_remote(chip_sums.at[mine], chip_sums.at[mine], send_sems, recv_sems, 1 + j, (*chip, c))
                for j, chip in enumerate(chips)]
        for cp in sent:
            cp.start()
        for j, chip in enumerate(chips):
            sent[j].wait_send()
            theirs = chip_sums.at[2 * chip[0] + chip[1]]
            _remote(theirs, theirs, send_sems, recv_sems, 1 + j, (x, y, c)).wait_recv()
        out_ref[...] = ((chip_sums[0] + chip_sums[1]) + chip_sums[2]) + chip_sums[3]

    vmem = pl.BlockSpec(memory_space=pltpu.VMEM)
    return pl.pallas_call(
        body, name="all_reduce_small", in_specs=[vmem], out_specs=vmem, out_shape=jax.ShapeDtypeStruct(part.shape, F32),
        scratch_shapes=[pltpu.VMEM((rows, LANE), F32), pltpu.VMEM((N_SHARDS, rows, LANE), F32),
                        pltpu.SemaphoreType.DMA((4,)), pltpu.SemaphoreType.DMA((4,))],
        compiler_params=pltpu.CompilerParams(vmem_limit_bytes=VMEM_LIMIT_BYTES),
    )(part)


def _sum_partials(g3, others, sel):
    _, rows, c = others.shape
    whole = g3.shape[1] == rows
    tr = _tile(rows, 512)
    nb = rows // tr

    def body(sel_ref, g_ref, *rest):
        same = g_ref[...].astype(F32)
        for ref in rest[1:4]:
            same = same + ref[...].astype(F32)
        other = rest[0][...].astype(F32)
        for ref in rest[4:7]:
            other = other + ref[...].astype(F32)
        rest[7][...] = same + other

    blk = (None, tr, c)
    slots = [pl.BlockSpec(blk, functools.partial(lambda i, sr, k: (k, i, 0), k=k)) for k in range(7)]
    return pl.pallas_call(
        body, name="reduce_sum_partials",
        grid_spec=pltpu.PrefetchScalarGridSpec(
            num_scalar_prefetch=1, grid=(nb,),
            in_specs=[pl.BlockSpec(blk, lambda i, sr: (sr[0], (0 if whole else sr[1] * nb) + i, 0))] + slots,
            out_specs=pl.BlockSpec((tr, c), lambda i, sr: (i, 0))),
        out_shape=jax.ShapeDtypeStruct((rows, c), F32),
        compiler_params=_params(("parallel",)),
    )(sel, g3, *[others] * 7)


def _adamw_math(w, g, m, v):
    nm = ADAM_B1 * m + (1.0 - ADAM_B1) * g
    nv = ADAM_B2 * v + (1.0 - ADAM_B2) * (g * g)
    m_hat = nm / (1.0 - ADAM_B1 ** ADAM_STEP)
    v_hat = nv / (1.0 - ADAM_B2 ** ADAM_STEP)
    return -ADAM_LR * (m_hat / (jnp.sqrt(v_hat) + ADAM_EPS) + ADAM_WD * w), nm, nv


def _adamw_layer(layer, w, m, v, g_mine, g_sibling, sel, prev):
    lyr, r, c = w.shape
    rows = g_mine.shape[0]
    halves = r // rows
    tr = _tile(rows, 512)
    nb = rows // tr
    n_g = 1 if g_sibling is None else 2

    def body(sel_ref, w_ref, m_ref, v_ref, *rest):
        g = rest[0][...]
        if n_g == 2:
            g = jnp.where(pl.program_id(0) == sel_ref[1], g, rest[1][...])
        outs = rest[n_g + (0 if prev is None else 4):]
        d, nm, nv = _adamw_math(w_ref[...], g, m_ref[...], v_ref[...])
        for ref, val in zip(outs, (g, d, nm, nv)):
            ref[...] = val

    full = pl.BlockSpec((None, tr, c), lambda h, i, sr: (layer, h * nb + i, 0))
    part = pl.BlockSpec((tr, c), lambda h, i, sr: (i, 0))
    n_in = 4 + n_g
    return pl.pallas_call(
        body, name="adamw_layer",
        grid_spec=pltpu.PrefetchScalarGridSpec(
            num_scalar_prefetch=1, grid=(halves, nb),
            in_specs=[full] * 3 + [part] * n_g + ([] if prev is None else [pl.BlockSpec(memory_space=pl.ANY)] * 4),
            out_specs=[full] * 4),
        out_shape=[jax.ShapeDtypeStruct(w.shape, F32)] * 4,
        input_output_aliases={} if prev is None else {n_in + k: k for k in range(4)},
        compiler_params=_params(("parallel", "parallel")),
    )(sel, w, m, v, g_mine, *([] if g_sibling is None else [g_sibling]), *([] if prev is None else prev))


def _adamw(w, g, m, v):
    lyr, r, c = w.shape
    tr = _tile(r, 512)

    def body(w_ref, g_ref, m_ref, v_ref, d_ref, nm_ref, nv_ref):
        gv = g_ref[...]
        nm = ADAM_B1 * m_ref[...] + (1.0 - ADAM_B1) * gv
        nv = ADAM_B2 * v_ref[...] + (1.0 - ADAM_B2) * (gv * gv)
        m_hat = nm / (1.0 - ADAM_B1 ** ADAM_STEP)
        v_hat = nv / (1.0 - ADAM_B2 ** ADAM_STEP)
        d_ref[...] = -ADAM_LR * (m_hat / (jnp.sqrt(v_hat) + ADAM_EPS) + ADAM_WD * w_ref[...])
        nm_ref[...] = nm
        nv_ref[...] = nv

    blk = pl.BlockSpec((None, tr, c), lambda l, i: (l, i, 0))
    return pl.pallas_call(
        body, name="adamw", grid=(lyr, r // tr), in_specs=[blk] * 4, out_specs=[blk] * 3,
        out_shape=[jax.ShapeDtypeStruct(w.shape, F32)] * 3,
        compiler_params=_params(("parallel", "parallel")),
    )(w, g, m, v)


SHARDED = ("mla_w_dkv", "mla_w_uq", "mla_w_ukv", "mla_w_o", "sgu_w_in", "sgu_ln_g", "sgu_ln_b", "sgu_w_out",
           "ffn_w_up", "ffn_w_down")
REPLICATED = ("norm_mix", "norm_ffn", "final_norm", "mla_q_norm", "mla_kv_norm", "sgu_w_spatial", "sgu_b_spatial")
WEIGHTS = ("norm_mix", "norm_ffn", "final_norm", "mla_w_dkv", "mla_q_norm", "mla_kv_norm", "mla_w_uq", "mla_w_ukv",
           "mla_w_o", "sgu_w_in", "sgu_ln_g", "sgu_ln_b", "sgu_w_spatial", "sgu_b_spatial", "sgu_w_out", "ffn_w_up",
           "ffn_w_down")


class _Reducer:
    def __init__(self, state, sel):
        self.state, self.sel = state, sel
        self.started, self.travelling, self.landed, self.summed = [], [], [], []
        self.done = {}

    def add(self, tag, layer, grads, token):
        names = list(grads)
        token, *tied = lax.optimization_barrier((token, *[a for n in names for a in grads[n]]))
        f32s, bf16s = tied[0::2], tied[1::2]
        received = _exchange_partials(tag, bf16s)
        own = [g if g.shape[1] >= MIN_SPLIT_ROWS else gb for g, gb in zip(f32s, bf16s)]
        self.started.append((tag, layer, names, own, received))
        return token

    def phase_end(self, token):
        for tag, layer, names, own, received in self.landed:
            token, *received = lax.optimization_barrier((token, *received))
            mine = [_sum_partials(g, got, self.sel) for g, got in zip(own, received)]
            cut = [k for k, g in enumerate(own) if g.shape[1] >= MIN_SPLIT_ROWS]
            theirs = dict(zip(cut, _share_halves(tag, [mine[k] for k in cut])))
            self.summed.append((layer, names, mine, [theirs.get(k) for k in range(len(names))]))
            token = lax.optimization_barrier((token, *mine))[0]
        self.landed, self.travelling, self.started = self.travelling, self.started, []
        return token

    def update(self, token):
        for layer, names, mine, theirs in self.summed:
            for name, g_mine, g_theirs in zip(names, mine, theirs):
                w, m, v = self.state[name]
                self.done[name] = _adamw_layer(layer, w, m, v, g_mine, g_theirs, self.sel, self.done.get(name))
                token = self.done[name][1]
        self.summed = []
        return token


def _as3d(name, a):
    return a.reshape(a.shape[0], -1, LANE) if name in ("sgu_ln_g", "sgu_ln_b") else a


def _pack(parts):
    flat = jnp.concatenate([p.reshape(-1) for p in parts])
    rows = -(-flat.shape[0] // (8 * LANE)) * 8
    return jnp.pad(flat, (0, rows * LANE - flat.shape[0])).reshape(rows, LANE)


def _unpack(packed, like):
    flat, out, at = packed.reshape(-1), [], 0
    for p in like:
        out.append(flat[at:at + p.size].reshape(p.shape))
        at += p.size
    return out


def kernel(x, positions, norm_mix, norm_ffn, final_norm, mla_w_dkv, mla_q_norm, mla_kv_norm, mla_w_uq, mla_w_ukv, mla_w_o, sgu_w_in, sgu_ln_g, sgu_ln_b, sgu_w_spatial, sgu_b_spatial, sgu_w_out, ffn_w_up, ffn_w_down, loss_target, m_norm_mix, m_norm_ffn, m_final_norm, m_mla_w_dkv, m_mla_q_norm, m_mla_kv_norm, m_mla_w_uq, m_mla_w_ukv, m_mla_w_o, m_sgu_w_in, m_sgu_ln_g, m_sgu_ln_b, m_sgu_w_spatial, m_sgu_b_spatial, m_sgu_w_out, m_ffn_w_up, m_ffn_w_down, v_norm_mix, v_norm_ffn, v_final_norm, v_mla_w_dkv, v_mla_q_norm, v_mla_kv_norm, v_mla_w_uq, v_mla_w_ukv, v_mla_w_o, v_sgu_w_in, v_sgu_ln_g, v_sgu_ln_b, v_sgu_w_spatial, v_sgu_b_spatial, v_sgu_w_out, v_ffn_w_up, v_ffn_w_down):
    given = dict(locals())
    w = {n: given[n] for n in WEIGHTS}
    mom = {n: given["m_" + n] for n in WEIGHTS}
    var = {n: given["v_" + n] for n in WEIGHTS}
    mixers, ffn, token = [], [], None
    for i in range(DEPTH):
        j = i // 2
        if i % 2 == 0:
            mixer = [w[n][j].astype(BF16) for n in ("mla_w_dkv", "mla_w_uq", "mla_w_ukv", "mla_w_o")]
        else:
            mixer = [sgu_w_in[j].astype(BF16), sgu_w_out[j].astype(BF16), sgu_ln_g[j].reshape(-1, LANE),
                     sgu_ln_b[j].reshape(-1, LANE)]
        for tag, shards, into in ((f"mixer{i}", mixer, mixers), (f"ffn{i}", [ffn_w_up[i].astype(BF16), ffn_w_down[i].astype(BF16)], ffn)):
            if token is not None:
                token, *shards = lax.optimization_barrier((token, *shards))
            token = shards[0]
            into.append(_gather_layer(tag, shards))

    x_i, y_i, c_i = lax.axis_index("x"), lax.axis_index("y"), lax.axis_index("c")
    sel = jnp.stack([2 * x_i + y_i, c_i]).astype(jnp.int32)
    reducer = _Reducer({n: tuple(_as3d(n, d[n]) for d in (w, mom, var)) for n in SHARDED}, sel)
    loss, dx, small, token = _local_step(
        x[0], positions[0], loss_target[0], norm_mix, norm_ffn, final_norm, mla_q_norm, mla_kv_norm, sgu_w_spatial,
        sgu_b_spatial, mixers, ffn, reducer)
    loss = lax.psum(loss, ("x", "y", "c"))

    small_g = [small["norm_mix"], small["norm_ffn"], small["final_norm"], small["q_norm"], small["kv_norm"],
               small["w_sp"], small["b_sp"]]
    like = [w[n] for n in REPLICATED]
    g_small = _all_reduce_small(lax.optimization_barrier((_pack(small_g), token))[0])
    packed = [_pack([d[n] for n in REPLICATED])[None] for d in (w, mom, var)]
    upd_small = _adamw(packed[0], g_small[None], packed[1], packed[2])
    grads = dict(zip(REPLICATED, _unpack(g_small, like)))
    delta, new_m, new_v = ({n: a for n, a in zip(REPLICATED, _unpack(u[0], like))} for u in upd_small)

    token = reducer.phase_end(reducer.phase_end(reducer.update(upd_small[0])))
    reducer.update(None)
    loss = lax.optimization_barrier((loss, token))[0]
    for n in SHARDED:
        grads[n], delta[n], new_m[n], new_v[n] = (a.reshape(w[n].shape) for a in reducer.done[n])

    return (loss, dx[None], *[grads[n] for n in WEIGHTS], *[delta[n] for n in WEIGHTS],
            *[new_m[n] for n in WEIGHTS], *[new_v[n] for n in WEIGHTS])
```

```python
import functools
import math

import jax
import jax.numpy as jnp
from jax import lax
from jax.experimental import pallas as pl
from jax.experimental.pallas import tpu as pltpu
from jax.experimental.pallas import tpu_sc as plsc

F32 = jnp.float32
BF16 = jnp.bfloat16
MESH = pl.DeviceIdType.MESH

DEPTH = 4
HEADS = 8
NOPE = 128
ROPE = 64
VHEAD = 128
QK_HEAD = NOPE + ROPE
Q_RANK = 256
KV_RANK = 128
HEAD_PAD = 256
LAT_PAD = 512
ROPE_THETA = 10000.0
SGU_CHUNK = 128
SGU_GROUPS = 8
NORM_EPS = 1e-6
LN_EPS = 1e-5
ADAM_LR, ADAM_B1, ADAM_B2, ADAM_EPS, ADAM_WD, ADAM_STEP = 0.001, 0.9, 0.999, 1e-08, 0.01, 10

N_SHARDS = 4
LANE = 128
VMEM_LIMIT_BYTES = 56 * 1024 * 1024
ATT_TILE = 512
MM_TILE = 1024
ATT_SCALE = QK_HEAD ** -0.5
LOG2_SCALE = ATT_SCALE * math.log2(math.e)

NN = (((1,), (0,)), ((), ()))
NT = (((1,), (1,)), ((), ()))
TN = (((0,), (0,)), ((), ()))


def _params(sem):
    return pltpu.CompilerParams(dimension_semantics=sem, vmem_limit_bytes=VMEM_LIMIT_BYTES)


def _tile(n, pref):
    t = min(n, pref)
    while n % t:
        t //= 2
    return t


def _matmul(name, a, b, a_spec, b_spec, dims, grid, tile, outs, extras=(), epilogue=None, aliased=()):
    nk, ne, no = grid[2], len(extras), len(outs)

    def body(a_ref, b_ref, *rest):
        e_refs, o_refs = rest[:ne], rest[ne + len(aliased):ne + len(aliased) + no]
        part = lax.dot_general(a_ref[...].astype(BF16), b_ref[...].astype(BF16), dims, preferred_element_type=F32)

        def finish(acc):
            vals = (acc,) if epilogue is None else epilogue(acc, *[e[...] for e in e_refs])
            for o_ref, v in zip(o_refs, vals):
                o_ref[...] = v.astype(o_ref.dtype)

        if nk == 1:
            finish(part)
            return
        acc_ref, k = rest[-1], pl.program_id(2)

        @pl.when(k == 0)
        def _():
            acc_ref[...] = part

        @pl.when(jnp.logical_and(k > 0, k < nk - 1))
        def _():
            acc_ref[...] += part

        @pl.when(k == nk - 1)
        def _():
            finish(acc_ref[...] + part)

    n_in = 2 + ne
    return pl.pallas_call(
        body, name=name, grid=grid,
        in_specs=[a_spec, b_spec] + [s for _, s in extras] + [pl.BlockSpec(memory_space=pl.ANY)] * len(aliased),
        out_specs=[s for _, s in outs], out_shape=[s for s, _ in outs],
        scratch_shapes=[pltpu.VMEM(tile, F32)] if nk > 1 else [],
        input_output_aliases={n_in + i: o for i, (_, o) in enumerate(aliased)},
        compiler_params=_params(("parallel", "parallel", "arbitrary")),
    )(a, b, *[e for e, _ in extras], *[arr for arr, _ in aliased])


def _mm(name, a, b, out_dtypes=(F32,), epilogue=None, extras=(), tm=MM_TILE, tn=MM_TILE, tk=MM_TILE, nt=False):
    m, kd = a.shape
    n = b.shape[0] if nt else b.shape[1]
    tm, tn, tk = _tile(m, tm), _tile(n, tn), _tile(kd, tk)
    o_spec = pl.BlockSpec((tm, tn), lambda i, j, k: (i, j))
    b_spec = pl.BlockSpec((tn, tk), lambda i, j, k: (j, k)) if nt else pl.BlockSpec((tk, tn), lambda i, j, k: (k, j))
    return _matmul(name, a, b, pl.BlockSpec((tm, tk), lambda i, j, k: (i, k)), b_spec, NT if nt else NN,
                   (m // tm, n // tn, kd // tk), (tm, tn),
                   [(jax.ShapeDtypeStruct((m, n), d), o_spec) for d in out_dtypes],
                   [(e, o_spec) for e in extras], epilogue)


def _mm_tn(name, a, b, out_dtypes=(F32,), tm=MM_TILE, tn=MM_TILE, tk=MM_TILE):
    s, m = a.shape
    n = b.shape[1]
    tm, tn, tk = _tile(m, tm), _tile(n, tn), _tile(s, tk)
    o_spec = pl.BlockSpec((tm, tn), lambda i, j, k: (i, j))
    return _matmul(name, a, b, pl.BlockSpec((tk, tm), lambda i, j, k: (k, i)),
                   pl.BlockSpec((tk, tn), lambda i, j, k: (k, j)), TN, (m // tm, n // tn, s // tk), (tm, tn),
                   [(jax.ShapeDtypeStruct((m, n), d), o_spec) for d in out_dtypes])


def _mm_stacked(name, a, w3, mode, out_dtypes=(F32,), epilogue=None, extras=(), tm=MM_TILE, tn=MM_TILE, tk=MM_TILE):
    m, kd = a.shape
    _, r, c = w3.shape
    n = c if mode == "row" else N_SHARDS * c
    tm = _tile(m, tm)
    if mode == "row":
        tn, tk = _tile(n, tn), _tile(r, tk)
        per = r // tk
        b_spec = pl.BlockSpec((None, tk, tn), lambda i, j, k: (k // per, k % per, j))
    else:
        tn, tk = _tile(c, tn), _tile(kd, tk)
        per = c // tn
        b_spec = pl.BlockSpec((None, tk, tn), lambda i, j, k: (j // per, k, j % per))
    o_spec = pl.BlockSpec((tm, tn), lambda i, j, k: (i, j))
    return _matmul(name, a, w3, pl.BlockSpec((tm, tk), lambda i, j, k: (i, k)), b_spec, NN,
                   (m // tm, n // tn, kd // tk), (tm, tn),
                   [(jax.ShapeDtypeStruct((m, n), d), o_spec) for d in out_dtypes],
                   [(e, o_spec) for e in extras], epilogue)


def _mm_stacked_nt(name, a, w3, mode, out_dtypes=(F32,), epilogue=None, extras=(), tm=MM_TILE, tn=MM_TILE, tk=MM_TILE):
    m, nd = a.shape
    _, r, c = w3.shape
    kout = N_SHARDS * r if mode == "row" else r
    tm = _tile(m, tm)
    if mode == "row":
        tn, tk = _tile(r, tn), _tile(c, tk)
        per = r // tn
        b_spec = pl.BlockSpec((None, tn, tk), lambda i, j, k: (j // per, j % per, k))
    else:
        tn, tk = _tile(r, tn), _tile(c, tk)
        per = c // tk
        b_spec = pl.BlockSpec((None, tn, tk), lambda i, j, k: (k // per, j, k % per))
    o_spec = pl.BlockSpec((tm, tn), lambda i, j, k: (i, j))
    return _matmul(name, a, w3, pl.BlockSpec((tm, tk), lambda i, j, k: (i, k)), b_spec, NT,
                   (m // tm, kout // tn, nd // tk), (tm, tn),
                   [(jax.ShapeDtypeStruct((m, kout), d), o_spec) for d in out_dtypes],
                   [(e, o_spec) for e in extras], epilogue)


def _mm_tn_stacked(name, a, b, shape3, mode, tm=MM_TILE, tn=MM_TILE, tk=MM_TILE):
    s, m = a.shape
    n = b.shape[1]
    _, r, c = shape3
    tk = _tile(s, tk)
    if mode == "row":
        tm, tn = _tile(r, tm), _tile(n, tn)
        per = r // tm
        o_spec = pl.BlockSpec((None, tm, tn), lambda i, j, k: (i // per, i % per, j))
    else:
        tm, tn = _tile(m, tm), _tile(c, tn)
        per = c // tn
        o_spec = pl.BlockSpec((None, tm, tn), lambda i, j, k: (j // per, i, j % per))
    outs = [(jax.ShapeDtypeStruct(shape3, F32), o_spec), (jax.ShapeDtypeStruct(shape3, BF16), o_spec)]
    return _matmul(name, a, b, pl.BlockSpec((tk, tm), lambda i, j, k: (k, i)),
                   pl.BlockSpec((tk, tn), lambda i, j, k: (k, j)), TN, (m // tm, n // tn, s // tk), (tm, tn),
                   outs, epilogue=lambda acc: (acc, acc))


def _rowwise(name, fn, rows, consts, out_rows, out_accs=(), tr=256):
    nr, nc, no = len(rows), len(consts), len(out_rows)
    n_rows = rows[0].shape[0]
    tr = _tile(n_rows, tr)

    def body(*refs):
        vals = fn(*[r[...] for r in refs[:nr + nc]])
        o_refs, a_refs = refs[nr + nc:nr + nc + no], refs[nr + nc + no:]
        for ref, v in zip(o_refs, vals[:no]):
            ref[...] = v.astype(ref.dtype)
        first = pl.program_id(0) == 0

        @pl.when(first)
        def _():
            for ref, v in zip(a_refs, vals[no:]):
                ref[...] = v

        @pl.when(jnp.logical_not(first))
        def _():
            for ref, v in zip(a_refs, vals[no:]):
                ref[...] += v

    def whole(shape):
        return pl.BlockSpec(shape, lambda i: (0,) * len(shape))

    return pl.pallas_call(
        body, name=name, grid=(n_rows // tr,),
        in_specs=[pl.BlockSpec((tr, a.shape[1]), lambda i: (i, 0)) for a in rows] + [whole(c.shape) for c in consts],
        out_specs=[pl.BlockSpec((tr, f), lambda i: (i, 0)) for f, _ in out_rows] + [whole(s) for s in out_accs],
        out_shape=[jax.ShapeDtypeStruct((n_rows, f), d) for f, d in out_rows]
        + [jax.ShapeDtypeStruct(s, F32) for s in out_accs],
        compiler_params=_params(("arbitrary",)),
    )(*rows, *consts)


def _rms_fwd(x, g):
    return x * lax.rsqrt(jnp.mean(x * x, axis=-1, keepdims=True) + NORM_EPS) * g


def _rms_bwd(dy, x, g):
    rstd = lax.rsqrt(jnp.mean(x * x, axis=-1, keepdims=True) + NORM_EPS)
    n = x * rstd
    dn = dy * g
    dx = rstd * (dn - n * jnp.mean(dn * n, axis=-1, keepdims=True))
    return dx, jnp.sum(dy * n, axis=0, keepdims=True)


def _rope(x, cs, s1, s2):
    return x * cs + pltpu.roll(x, 32, 1) * s1 + pltpu.roll(x, 96, 1) * s2


def _rope_t(dy, cs, s1, s2):
    return dy * cs + pltpu.roll(dy * s1, 96, 1) + pltpu.roll(dy * s2, 32, 1)


def _gelu(z):
    return 0.5 * z * (1.0 + lax.erf(z * (1.0 / math.sqrt(2.0))))


def _gelu_grad(z):
    return 0.5 * (1.0 + lax.erf(z * (1.0 / math.sqrt(2.0)))) + z * jnp.exp(-0.5 * z * z) * (1.0 / math.sqrt(2.0 * math.pi))


def _att_scores(q, kv, kr, scale, masked, transposed):
    k = jnp.concatenate([kv[:, :NOPE], kr], axis=1)
    if transposed:
        s = lax.dot_general(k, q, NT, preferred_element_type=F32) * scale
    else:
        s = lax.dot_general(q, k, NT, preferred_element_type=F32) * scale
    if masked:
        r = lax.broadcasted_iota(jnp.int32, s.shape, 0)
        c = lax.broadcasted_iota(jnp.int32, s.shape, 1)
        s = jnp.where((r <= c) if transposed else (c <= r), s, -jnp.inf)
    return s, k


def _flash_fwd(qb, kvb, krb):
    s_len = qb.shape[0]
    t = ATT_TILE

    def body(q_ref, kv_ref, kr_ref, o_ref, lse_ref, m_s, l_s, acc_s):
        qi = pl.program_id(1)
        m_s[...] = jnp.full_like(m_s, -jnp.inf)
        l_s[...] = jnp.zeros_like(l_s)
        acc_s[...] = jnp.zeros_like(acc_s)
        q = q_ref[...]

        def step(ki, masked):
            rows = pl.ds(pl.multiple_of(ki * t, t), t)
            kv = kv_ref[rows, :]
            s, _ = _att_scores(q, kv, kr_ref[rows, :], LOG2_SCALE, masked, False)
            m_prev = m_s[...]
            m_new = jnp.maximum(m_prev, jnp.max(s, axis=1, keepdims=True))
            alpha = jnp.exp2(m_prev - m_new)
            p = jnp.exp2(s - jnp.tile(m_new, (1, t // LANE)))
            l_s[...] = alpha * l_s[...] + jnp.sum(p, axis=1, keepdims=True)
            acc_s[...] = alpha * acc_s[...] + jnp.dot(p.astype(BF16), kv[:, NOPE:], preferred_element_type=F32)
            m_s[...] = m_new

        def unmasked(ki, carry):
            step(ki, False)
            return carry

        lax.fori_loop(0, qi, unmasked, 0)
        step(qi, True)
        o_ref[...] = (acc_s[...] / l_s[...]).astype(o_ref.dtype)
        lse_ref[...] = (m_s[...] + jnp.log2(l_s[...]))[:, :1]

    return pl.pallas_call(
        body, name="flash_fwd", grid=(HEADS, s_len // t),
        in_specs=[pl.BlockSpec((t, HEAD_PAD), lambda h, qi: (qi, h)),
                  pl.BlockSpec((s_len, HEAD_PAD), lambda h, qi: (0, h)),
                  pl.BlockSpec((s_len, LANE), lambda h, qi: (0, 0))],
        out_specs=[pl.BlockSpec((t, VHEAD), lambda h, qi: (qi, h)),
                   pl.BlockSpec((None, t, 1), lambda h, qi: (h, qi, 0))],
        out_shape=[jax.ShapeDtypeStruct((s_len, HEADS * VHEAD), BF16),
                   jax.ShapeDtypeStruct((HEADS, s_len, 1), F32)],
        scratch_shapes=[pltpu.VMEM((t, LANE), F32), pltpu.VMEM((t, LANE), F32), pltpu.VMEM((t, VHEAD), F32)],
        compiler_params=_params(("parallel", "arbitrary")),
    )(qb, kvb, krb)


def _flash_bwd_dq(qb, kvb, krb, dob, lse, delta):
    s_len = qb.shape[0]
    t = ATT_TILE
    nq = s_len // t
    scale = QK_HEAD ** -0.5

    def body(q_ref, kv_ref, kr_ref, do_ref, lse_ref, dl_ref, dq_ref, acc_s):
        qi = pl.program_id(1)
        acc_s[...] = jnp.zeros_like(acc_s)
        q, do = q_ref[...], do_ref[...]
        lse = jnp.broadcast_to(lse_ref[...], (t, LANE))
        dl = jnp.broadcast_to(dl_ref[...], (t, LANE))

        def step(ki, masked):
            rows = pl.ds(pl.multiple_of(ki * t, t), t)
            kv = kv_ref[rows, :]
            s, k = _att_scores(q, kv, kr_ref[rows, :], LOG2_SCALE, masked, False)
            p = jnp.exp2(s - jnp.tile(lse, (1, t // LANE)))
            dp = lax.dot_general(do, kv[:, NOPE:], NT, preferred_element_type=F32)
            ds = (p * (dp - jnp.tile(dl, (1, t // LANE))) * scale).astype(BF16)
            acc_s[...] += jnp.dot(ds, k, preferred_element_type=F32)

        def unmasked(ki, carry):
            step(ki, False)
            return carry

        lax.fori_loop(0, qi, unmasked, 0)
        step(qi, True)
        dq_ref[...] = acc_s[...]

    col = pl.BlockSpec((None, t, 1), lambda h, qi: (h, qi, 0))
    return pl.pallas_call(
        body, name="flash_bwd_dq", grid=(HEADS, nq),
        in_specs=[pl.BlockSpec((t, HEAD_PAD), lambda h, qi: (qi, h)),
                  pl.BlockSpec((s_len, HEAD_PAD), lambda h, qi: (0, h)),
                  pl.BlockSpec((s_len, LANE), lambda h, qi: (0, 0)),
                  pl.BlockSpec((t, VHEAD), lambda h, qi: (qi, h)), col, col],
        out_specs=pl.BlockSpec((t, HEAD_PAD), lambda h, qi: (qi, h)),
        out_shape=jax.ShapeDtypeStruct((s_len, HEADS * HEAD_PAD), F32),
        scratch_shapes=[pltpu.VMEM((t, HEAD_PAD), F32)],
        compiler_params=_params(("parallel", "arbitrary")),
    )(qb, kvb, krb, dob, lse, delta)


def _flash_bwd_dkv(qb, kvb, krb, dob, lse_row, delta_row):
    s_len = qb.shape[0]
    t = ATT_TILE
    nq = s_len // t
    scale = QK_HEAD ** -0.5

    def body(q_ref, kv_ref, kr_ref, do_ref, lse_ref, dl_ref, dkv_ref, dkr_ref, dk_s, dv_s):
        ki = pl.program_id(1)
        dk_s[...] = jnp.zeros_like(dk_s)
        dv_s[...] = jnp.zeros_like(dv_s)
        kv, kr = kv_ref[...], kr_ref[...]

        def step(qi, masked):
            rows = pl.ds(pl.multiple_of(qi * t, t), t)
            q, do = q_ref[rows, :], do_ref[rows, :]
            st, _ = _att_scores(q, kv, kr, LOG2_SCALE, masked, True)
            pt = jnp.exp2(st - lse_ref[:, rows])
            dv_s[...] += jnp.dot(pt.astype(BF16), do, preferred_element_type=F32)
            dpt = lax.dot_general(kv[:, NOPE:], do, NT, preferred_element_type=F32)
            dst = (pt * (dpt - dl_ref[:, rows]) * scale).astype(BF16)
            dk_s[...] += jnp.dot(dst, q, preferred_element_type=F32)

        def unmasked(qi, carry):
            step(qi, False)
            return carry

        step(ki, True)
        lax.fori_loop(ki + 1, nq, unmasked, 0)
        dk = dk_s[...]
        dkv_ref[...] = jnp.concatenate([dk[:, :NOPE], dv_s[...]], axis=1).astype(dkv_ref.dtype)
        dkr_ref[...] = dk[:, NOPE:]

    row = pl.BlockSpec((None, 1, s_len), lambda h, ki: (h, 0, 0))
    return pl.pallas_call(
        body, name="flash_bwd_dkv", grid=(HEADS, nq),
        in_specs=[pl.BlockSpec((s_len, HEAD_PAD), lambda h, ki: (0, h)),
                  pl.BlockSpec((t, HEAD_PAD), lambda h, ki: (ki, h)),
                  pl.BlockSpec((t, LANE), lambda h, ki: (ki, 0)),
                  pl.BlockSpec((s_len, VHEAD), lambda h, ki: (0, h)), row, row],
        out_specs=[pl.BlockSpec((t, HEAD_PAD), lambda h, ki: (ki, h)),
                   pl.BlockSpec((t, LANE), lambda h, ki: (ki, h))],
        out_shape=[jax.ShapeDtypeStruct((s_len, HEADS * HEAD_PAD), BF16),
                   jax.ShapeDtypeStruct((s_len, HEADS * LANE), F32)],
        scratch_shapes=[pltpu.VMEM((t, HEAD_PAD), F32), pltpu.VMEM((t, VHEAD), F32)],
        compiler_params=_params(("parallel", "parallel")),
    )(qb, kvb, krb, dob, lse_row, delta_row)


def _att_delta(do, ob):
    s_len = do.shape[0]
    t = ATT_TILE

    def body(do_ref, o_ref, dob_ref, dl_ref):
        d = do_ref[...]
        dob_ref[...] = d.astype(dob_ref.dtype)
        dl_ref[...] = jnp.sum(d * o_ref[...].astype(F32), axis=1, keepdims=True)

    blk = pl.BlockSpec((t, VHEAD), lambda i, h: (i, h))
    return pl.pallas_call(
        body, name="att_delta", grid=(s_len // t, HEADS), in_specs=[blk, blk],
        out_specs=[blk, pl.BlockSpec((None, t, 1), lambda i, h: (h, i, 0))],
        out_shape=[jax.ShapeDtypeStruct(do.shape, BF16), jax.ShapeDtypeStruct((HEADS, s_len, 1), F32)],
        compiler_params=_params(("parallel", "parallel")),
    )(do, ob)


def _tril(w):
    r = lax.broadcasted_iota(jnp.int32, w.shape, 0)
    c = lax.broadcasted_iota(jnp.int32, w.shape, 1)
    return jnp.where(c <= r, w, 0.0)


def _sgu_mix(w_ref, vln, gd):
    wcs = [_tril(w_ref[g]).astype(BF16) for g in range(SGU_GROUPS)]
    mixed = jnp.concatenate(
        [jnp.dot(wcs[g], vln[:, g * gd:(g + 1) * gd], preferred_element_type=F32) for g in range(SGU_GROUPS)], axis=1)
    return wcs, mixed


def _sgu_fwd(zpre, ln_g, ln_b, w_sp, bias_full):
    s_len, two_w = zpre.shape
    width = two_w // 2
    gd = width // SGU_GROUPS
    t = SGU_CHUNK

    def body(z_ref, g_ref, b_ref, w_ref, bias_ref, uv_ref):
        u = _gelu(z_ref[:, :width])
        v = _gelu(z_ref[:, width:])
        d = v - jnp.mean(v, axis=-1, keepdims=True)
        vhat = d * lax.rsqrt(jnp.mean(d * d, axis=-1, keepdims=True) + LN_EPS)
        vln = (vhat * g_ref[...] + b_ref[...]).astype(BF16)
        _, mixed = _sgu_mix(w_ref, vln, gd)
        uv_ref[...] = (u * (mixed + bias_ref[...])).astype(uv_ref.dtype)

    return pl.pallas_call(
        body, name="sgu_fwd", grid=(s_len // t,),
        in_specs=[pl.BlockSpec((t, two_w), lambda i: (i, 0)), pl.BlockSpec((1, width), lambda i: (0, 0)),
                  pl.BlockSpec((1, width), lambda i: (0, 0)), pl.BlockSpec(w_sp.shape, lambda i: (0, 0, 0)),
                  pl.BlockSpec((t, width), lambda i: (0, 0))],
        out_specs=pl.BlockSpec((t, width), lambda i: (i, 0)),
        out_shape=jax.ShapeDtypeStruct((s_len, width), BF16),
        compiler_params=_params(("parallel",)),
    )(zpre, ln_g, ln_b, w_sp, bias_full)


def _sgu_bwd(zpre, duv, ln_g, ln_b, w_sp, bias_full):
    s_len, two_w = zpre.shape
    width = two_w // 2
    gd = width // SGU_GROUPS
    t = SGU_CHUNK

    def body(z_ref, duv_ref, g_ref, b_ref, w_ref, bias_ref, dz_ref, dg_ref, db_ref, dw_ref, dbias_ref):
        first = pl.program_id(0) == 0

        def accumulate(ref, val):
            @pl.when(first)
            def _():
                ref[...] = val

            @pl.when(jnp.logical_not(first))
            def _():
                ref[...] += val

        zu, zv = z_ref[:, :width], z_ref[:, width:]
        u = _gelu(zu)
        v = _gelu(zv)
        d = v - jnp.mean(v, axis=-1, keepdims=True)
        rstd = lax.rsqrt(jnp.mean(d * d, axis=-1, keepdims=True) + LN_EPS)
        vhat = d * rstd
        vln = (vhat * g_ref[...] + b_ref[...]).astype(BF16)
        wcs, mixed = _sgu_mix(w_ref, vln, gd)
        duv_v = duv_ref[...]
        du = duv_v * (mixed + bias_ref[...])
        dmixed = duv_v * u
        dmb = dmixed.astype(BF16)
        dvln = jnp.concatenate(
            [lax.dot_general(wcs[g], dmb[:, g * gd:(g + 1) * gd], TN, preferred_element_type=F32)
             for g in range(SGU_GROUPS)], axis=1)
        for g in range(SGU_GROUPS):
            dw = lax.dot_general(dmb[:, g * gd:(g + 1) * gd], vln[:, g * gd:(g + 1) * gd], NT, preferred_element_type=F32)
            accumulate(dw_ref.at[g], _tril(dw))
        dvhat = dvln * g_ref[...]
        dv0 = rstd * (dvhat - jnp.mean(dvhat, axis=-1, keepdims=True)
                      - vhat * jnp.mean(dvhat * vhat, axis=-1, keepdims=True))
        dz_ref[:, :width] = (du * _gelu_grad(zu)).astype(dz_ref.dtype)
        dz_ref[:, width:] = (dv0 * _gelu_grad(zv)).astype(dz_ref.dtype)
        accumulate(dg_ref, jnp.sum(dvln * vhat, axis=0, keepdims=True))
        accumulate(db_ref, jnp.sum(dvln, axis=0, keepdims=True))
        accumulate(dbias_ref, dmixed)

    vec = pl.BlockSpec((1, width), lambda i: (0, 0))
    return pl.pallas_call(
        body, name="sgu_bwd", grid=(s_len // t,),
        in_specs=[pl.BlockSpec((t, two_w), lambda i: (i, 0)), pl.BlockSpec((t, width), lambda i: (i, 0)), vec, vec,
                  pl.BlockSpec(w_sp.shape, lambda i: (0, 0, 0)), pl.BlockSpec((t, width), lambda i: (0, 0))],
        out_specs=[pl.BlockSpec((t, two_w), lambda i: (i, 0)), vec, vec,
                   pl.BlockSpec(w_sp.shape, lambda i: (0, 0, 0)), pl.BlockSpec((t, width), lambda i: (0, 0))],
        out_shape=[jax.ShapeDtypeStruct((s_len, two_w), BF16), jax.ShapeDtypeStruct((1, width), F32),
                   jax.ShapeDtypeStruct((1, width), F32), jax.ShapeDtypeStruct(w_sp.shape, F32),
                   jax.ShapeDtypeStruct((t, width), F32)],
        compiler_params=_params(("arbitrary",)),
    )(zpre, duv, ln_g, ln_b, w_sp, bias_full)


def _rope_tables(positions):
    inv_freq = ROPE_THETA ** (-jnp.arange(0, ROPE, 2, dtype=F32) / ROPE)
    ang = positions.astype(F32)[:, None] * inv_freq
    cos, sin = jnp.cos(ang), jnp.sin(ang)
    z32, z64 = jnp.zeros_like(cos), jnp.zeros((cos.shape[0], LANE - ROPE), F32)
    return (jnp.concatenate([cos, cos, z64], axis=1), jnp.concatenate([z32, sin, z64], axis=1),
            jnp.concatenate([-sin, z32, z64], axis=1))


def _q_rope(name, q, tables, transpose):
    rot = _rope_t if transpose else _rope

    def fn(qv, cs, s1, s2):
        parts = []
        for h in range(HEADS):
            parts.append(qv[:, h * HEAD_PAD:h * HEAD_PAD + NOPE])
            parts.append(rot(qv[:, h * HEAD_PAD + NOPE:(h + 1) * HEAD_PAD], cs, s1, s2))
        return (jnp.concatenate(parts, axis=1),)

    return _rowwise(name, fn, [q, *tables], [], [(q.shape[1], BF16)])[0]


def _ffn_fwd(x, g, w_up3, w_down3):
    h2 = _rowwise("ffn_norm", lambda xv, gv: (_rms_fwd(xv, gv),), [x], [g], [(x.shape[1], BF16)])[0]

    def sq_relu(acc):
        r = jnp.maximum(acc, 0.0)
        return (r * r,)

    r = _mm_stacked("ffn_up", h2, w_up3, "col", (BF16,), sq_relu)[0]
    x_out = _mm_stacked("ffn_down", r, w_down3, "row", (F32,), lambda acc, res: (acc + res,), [x])[0]
    return x_out, (x, h2, r)


def _ffn_bwd(dx, dxb, saved, g, w_up3, w_down3):
    x, h2, r = saved
    da = _mm_stacked_nt("ffn_down_dx", dxb, w_down3, "row", (BF16,),
                        lambda acc, rv: (acc * (2.0 * jnp.sqrt(rv.astype(F32))),), [r])[0]
    g_down = _mm_tn_stacked("ffn_down_dw", r, dxb, w_down3.shape, "row")
    dh2 = _mm_stacked_nt("ffn_up_dx", da, w_up3, "col", (F32,))[0]
    g_up = _mm_tn_stacked("ffn_up_dw", h2, da, w_up3.shape, "col")
    dx, dxb, dg = _norm_bwd("ffn_norm_bwd", dh2, x, g, dx)
    return dx, dxb, dg, g_up, g_down


def _norm_bwd(name, dh, x, g, dres):
    def fn(dhv, xv, rv, gv):
        dxv, dg = _rms_bwd(dhv, xv, gv)
        return dxv + rv, dxv + rv, dg

    return _rowwise(name, fn, [dh, x, dres], [g], [(x.shape[1], F32), (x.shape[1], BF16)], [g.shape])


def _mla_fwd(x, g, wdkv, q_norm, kv_norm, wq, wkv, wo, tables):
    d = x.shape[1]
    h = _rowwise("mla_norm", lambda xv, gv: (_rms_fwd(xv, gv),), [x], [g], [(d, BF16)])[0]
    lat = _mm("mla_dkv", h, wdkv)[0]

    def lat_post(lv, cs, s1, s2, qg, kg):
        return (_rms_fwd(lv[:, :Q_RANK], qg), _rms_fwd(lv[:, Q_RANK:Q_RANK + KV_RANK], kg),
                _rope(lv[:, Q_RANK + KV_RANK:], cs, s1, s2))

    cqn, ckvn, krb = _rowwise("mla_lat", lat_post, [lat, *tables], [q_norm, kv_norm],
                              [(Q_RANK, BF16), (KV_RANK, BF16), (LANE, BF16)])
    q = _mm("mla_uq", cqn, wq)[0]
    kvb = _mm("mla_ukv", ckvn, wkv, (BF16,))[0]
    qb = _q_rope("mla_q_rope", q, tables, False)
    ob, lse = _flash_fwd(qb, kvb, krb)
    x_mid = _mm("mla_o", ob, wo, (F32,), lambda acc, res: (acc + res,), [x])[0]
    return x_mid, (x, h, lat, cqn, ckvn, krb, qb, kvb, ob, lse)


def _mla_bwd(dx, dxb, saved, g, wdkv, q_norm, kv_norm, wq, wkv, wo, tables):
    x, h, lat, cqn, ckvn, krb, qb, kvb, ob, lse = saved
    s_len = x.shape[0]
    do = _mm("mla_o_dx", dxb, wo, nt=True)[0]
    g_wo = _mm_tn("mla_o_dw", ob, dxb)[0]
    dob, delta = _att_delta(do, ob)
    dq = _flash_bwd_dq(qb, kvb, krb, dob, lse, delta)
    dkvb, dkr = _flash_bwd_dkv(qb, kvb, krb, dob, lse.reshape(HEADS, 1, s_len), delta.reshape(HEADS, 1, s_len))
    dqb = _q_rope("mla_q_rope_bwd", dq, tables, True)
    dcqn = _mm("mla_uq_dx", dqb, wq, nt=True)[0]
    g_wq = _mm_tn("mla_uq_dw", cqn, dqb)[0]
    dckvn = _mm("mla_ukv_dx", dkvb, wkv, nt=True)[0]
    g_wkv = _mm_tn("mla_ukv_dw", ckvn, dkvb)[0]

    def lat_bwd(dq_v, dkv_v, dkr_v, lv, cs, s1, s2, qg, kg):
        dcq, dqg = _rms_bwd(dq_v, lv[:, :Q_RANK], qg)
        dckv, dkg = _rms_bwd(dkv_v, lv[:, Q_RANK:Q_RANK + KV_RANK], kg)
        dkr_sum = dkr_v[:, :LANE]
        for hd in range(1, HEADS):
            dkr_sum = dkr_sum + dkr_v[:, hd * LANE:(hd + 1) * LANE]
        return jnp.concatenate([dcq, dckv, _rope_t(dkr_sum, cs, s1, s2)], axis=1), dqg, dkg

    dlat, g_qn, g_kvn = _rowwise("mla_lat_bwd", lat_bwd, [dcqn, dckvn, dkr, lat, *tables], [q_norm, kv_norm],
                                 [(LAT_PAD, BF16)], [q_norm.shape, kv_norm.shape])
    dh = _mm("mla_dkv_dx", dlat, wdkv, nt=True)[0]
    g_wdkv = _mm_tn("mla_dkv_dw", h, dlat)[0]
    dx, dxb, dg = _norm_bwd("mla_norm_bwd", dh, x, g, dx)
    return dx, dxb, dg, g_wdkv, g_qn, g_kvn, g_wq, g_wkv, g_wo


def _sgu_layer_fwd(x, g, w_in3, ln_g, ln_b, w_sp, bias_full, w_out3):
    h = _rowwise("sgu_norm", lambda xv, gv: (_rms_fwd(xv, gv),), [x], [g], [(x.shape[1], BF16)])[0]
    zpre = _mm_stacked("sgu_in", h, w_in3, "col")[0]
    uv = _sgu_fwd(zpre, ln_g, ln_b, w_sp, bias_full)
    x_mid = _mm_stacked("sgu_out", uv, w_out3, "row", (F32,), lambda acc, res: (acc + res,), [x])[0]
    return x_mid, (x, h, zpre, uv)


def _sgu_layer_bwd(dx, dxb, saved, g, w_in3, ln_g, ln_b, w_sp, bias_full, w_out3):
    x, h, zpre, uv = saved
    duv = _mm_stacked_nt("sgu_out_dx", dxb, w_out3, "row")[0]
    g_out = _mm_tn_stacked("sgu_out_dw", uv, dxb, w_out3.shape, "row")
    dz, g_lng, g_lnb, g_wsp, g_bias = _sgu_bwd(zpre, duv, ln_g, ln_b, w_sp, bias_full)
    dh = _mm_stacked_nt("sgu_in_dx", dz, w_in3, "col")[0]
    g_in = _mm_tn_stacked("sgu_in_dw", h, dz, w_in3.shape, "col")
    dx, dxb, dg = _norm_bwd("sgu_norm_bwd", dh, x, g, dx)
    return dx, dxb, dg, g_in, g_out, g_lng, g_lnb, g_wsp, g_bias


def _loss_head(x, target, g):
    d = x.shape[1]

    def fn(xv, tv, gv):
        err = _rms_fwd(xv, gv) - tv
        dxv, dg = _rms_bwd(err * (1.0 / d), xv, gv)
        return dxv, dxv, dg, jnp.sum(err * err, axis=0, keepdims=True)

    return _rowwise("loss_head", fn, [x, target], [g], [(d, F32), (d, BF16)], [g.shape, g.shape])


def _mixer_weights(i, stacks):
    by_rows = lambda a: a.reshape(N_SHARDS * a.shape[1], a.shape[2])
    by_cols = lambda a: a.transpose(1, 0, 2).reshape(a.shape[1], N_SHARDS * a.shape[2])
    if i % 2:
        w_in3, w_out3, ln_g, ln_b = stacks
        return w_in3, ln_g.reshape(1, -1), ln_b.reshape(1, -1), w_out3
    wdkv = by_rows(stacks[0])
    wdkv = jnp.pad(wdkv, ((0, 0), (0, LAT_PAD - wdkv.shape[1])))
    wq = jnp.pad(by_cols(stacks[1]).reshape(Q_RANK, HEADS, QK_HEAD), ((0, 0), (0, 0), (0, HEAD_PAD - QK_HEAD)))
    return wdkv, wq.reshape(Q_RANK, HEADS * HEAD_PAD), by_cols(stacks[2]), by_rows(stacks[3])


def _local_step(x, positions, target, norm_mix, norm_ffn, final_norm, q_norm, kv_norm, w_sp, b_sp, mixers, ffn, reducer):
    tables = _rope_tables(positions)
    gd = mixers[1][2].size // SGU_GROUPS
    bias_full = [jnp.repeat(b_sp[j].T, gd, axis=1) for j in range(DEPTH // 2)]
    saved, mla, sgu = [], [None] * (DEPTH // 2), [None] * (DEPTH // 2)
    for i in range(DEPTH):
        j = i // 2
        x, *stacks = lax.optimization_barrier((x, *mixers[i]))
        if i % 2 == 0:
            wdkv, wq, wkv, wo = mla[j] = _mixer_weights(i, stacks)
            x, s_mix = _mla_fwd(x, norm_mix[i:i + 1], wdkv, q_norm[j:j + 1], kv_norm[j:j + 1], wq, wkv, wo, tables)
        else:
            w_in3, ln_g, ln_b, w_out3 = sgu[j] = _mixer_weights(i, stacks)
            x, s_mix = _sgu_layer_fwd(x, norm_mix[i:i + 1], w_in3, ln_g, ln_b, w_sp[j], bias_full[j], w_out3)
        x, s_ffn = _ffn_fwd(x, norm_ffn[i:i + 1], *ffn[i])
        saved.append((s_mix, s_ffn))
    dx, dxb, g_final, sq_cols = _loss_head(x, target, final_norm[None, :])
    loss = 0.5 * jnp.sum(sq_cols) / x.shape[1]

    def pair(g):
        return g, g.astype(BF16)

    g_mix, g_ffn = [None] * DEPTH, [None] * DEPTH
    mla_g, sgu_g = [None] * (DEPTH // 2), [None] * (DEPTH // 2)
    for i in reversed(range(DEPTH)):
        j = i // 2
        s_mix, s_ffn = saved[i]
        dx, dxb, g_ffn[i], g_up, g_down = _ffn_bwd(dx, dxb, s_ffn, norm_ffn[i:i + 1], *ffn[i])
        dxb = reducer.add(f"ffn{i}", i, {"ffn_w_up": g_up, "ffn_w_down": g_down}, dxb)
        dxb = reducer.phase_end(dxb)
        if i % 2 == 0:
            wdkv, wq, wkv, wo = mla[j]
            dx, dxb, g_mix[i], g_wdkv, g_qn, g_kvn, g_wq, g_wkv, g_wo = _mla_bwd(
                dx, dxb, s_mix, norm_mix[i:i + 1], wdkv, q_norm[j:j + 1], kv_norm[j:j + 1], wq, wkv, wo, tables)
            mla_g[j] = (g_qn, g_kvn)
            g_wq = g_wq.reshape(Q_RANK, HEADS, HEAD_PAD)[..., :QK_HEAD].reshape(Q_RANK, N_SHARDS, -1)
            dxb = reducer.add(f"mla{j}", j, {
                "mla_w_dkv": pair(g_wdkv[:, :Q_RANK + KV_RANK + ROPE].reshape(N_SHARDS, -1, Q_RANK + KV_RANK + ROPE)),
                "mla_w_uq": pair(g_wq.transpose(1, 0, 2)),
                "mla_w_ukv": pair(g_wkv.reshape(KV_RANK, N_SHARDS, -1).transpose(1, 0, 2)),
                "mla_w_o": pair(g_wo.reshape(N_SHARDS, -1, g_wo.shape[1]))}, dxb)
        else:
            w_in3, ln_g, ln_b, w_out3 = sgu[j]
            dx, dxb, g_mix[i], g_in, g_out, g_lng, g_lnb, g_wsp, g_bias = _sgu_layer_bwd(
                dx, dxb, s_mix, norm_mix[i:i + 1], w_in3, ln_g, ln_b, w_sp[j], bias_full[j], w_out3)
            sgu_g[j] = (g_wsp, g_bias.reshape(SGU_CHUNK, SGU_GROUPS, gd).sum(axis=-1).T)
            dxb = reducer.add(f"sgu{j}", j, {"sgu_w_in": g_in, "sgu_w_out": g_out,
                                             "sgu_ln_g": pair(g_lng.reshape(N_SHARDS, -1, LANE)),
                                             "sgu_ln_b": pair(g_lnb.reshape(N_SHARDS, -1, LANE))}, dxb)
        dxb = reducer.phase_end(dxb)
    small = dict(
        norm_mix=jnp.concatenate(g_mix, axis=0), norm_ffn=jnp.concatenate(g_ffn, axis=0), final_norm=g_final[0],
        q_norm=jnp.concatenate([m[0] for m in mla_g], axis=0), kv_norm=jnp.concatenate([m[1] for m in mla_g], axis=0),
        w_sp=jnp.stack([s[0] for s in sgu_g]), b_sp=jnp.stack([s[1] for s in sgu_g]))
    return loss, dx, small, dxb


HBM_SPEC = pl.BlockSpec(memory_space=pltpu.HBM)


def _place():
    x, y, c = lax.axis_index("x"), lax.axis_index("y"), lax.axis_index("c")
    return x, y, c, [(1 - x, y), (x, 1 - y), (1 - x, 1 - y)]


def _remote(src, dst, send_sems, recv_sems, k, to):
    return pltpu.make_async_remote_copy(src_ref=src, dst_ref=dst, send_sem=send_sems.at[k], recv_sem=recv_sems.at[k],
                                        device_id=to, device_id_type=MESH)


def _gather_layer(tag, shards):
    n = len(shards)
    split = [s.shape[0] >= 16 for s in shards]

    def body(*refs):
        ins, outs = refs[:n], refs[n:2 * n]
        send_sems, recv_sems, local_sems = refs[2 * n:]
        x, y, c, chips = _place()
        mine = 2 * x + y
        barrier = pltpu.get_barrier_semaphore()
        peers = [(x, y, 1 - c)] + [(*chip, c) for chip in chips]
        for peer in peers:
            pl.semaphore_signal(barrier, inc=1, device_id=peer, device_id_type=MESH)
        pl.semaphore_wait(barrier, len(peers))

        def rows(t, half):
            hr = shards[t].shape[0] // 2
            return pl.ds(half * hr, hr) if split[t] else pl.ds(0, shards[t].shape[0])

        local, sent = [], []
        for t in range(n):
            local.append(pltpu.make_async_copy(ins[t], outs[t].at[mine], local_sems.at[t]))
            local[-1].start()
            for j, chip in enumerate(chips):
                cp = _remote(ins[t].at[rows(t, c)], outs[t].at[mine, rows(t, c)], send_sems, recv_sems, 3 * t + j, (*chip, c))
                cp.start()
                sent.append(cp)
        for j, chip in enumerate(chips):
            theirs = 2 * chip[0] + chip[1]
            for t in range(n):
                piece = outs[t].at[theirs, rows(t, c)]
                _remote(piece, piece, send_sems, recv_sems, 3 * t + j, (x, y, c)).wait_recv()
                if split[t]:
                    cp = _remote(piece, piece, send_sems, recv_sems, 3 * n + 3 * t + j, (x, y, 1 - c))
                    cp.start()
                    sent.append(cp)
        for j, chip in enumerate(chips):
            theirs = 2 * chip[0] + chip[1]
            for t in range(n):
                if split[t]:
                    piece = outs[t].at[theirs, rows(t, 1 - c)]
                    _remote(piece, piece, send_sems, recv_sems, 3 * n + 3 * t + j, (x, y, c)).wait_recv()
        for cp in sent:
            cp.wait_send()
        for cp in local:
            cp.wait()

    return pl.kernel(
        body, name=f"gather_{tag}", mesh=plsc.ScalarSubcoreMesh(axis_name="sequencer", num_cores=1),
        out_type=[jax.ShapeDtypeStruct((N_SHARDS, *s.shape), s.dtype) for s in shards],
        scratch_types=[pltpu.SemaphoreType.DMA((6 * n,)), pltpu.SemaphoreType.DMA((6 * n,)), pltpu.SemaphoreType.DMA((n,))],
        compiler_params=pltpu.CompilerParams(collective_id=ID_GATHER),
    )(*shards)


SEQUENCER = dict(axis_name="sequencer", num_cores=1)
ID_GATHER, ID_EXCHANGE, ID_SHARE = 0, 1, 2
MIN_SPLIT_ROWS = 16


def _handshake(peers):
    barrier = pltpu.get_barrier_semaphore()
    for peer in peers:
        pl.semaphore_signal(barrier, inc=1, device_id=peer, device_id_type=MESH)
    pl.semaphore_wait(barrier, len(peers))


def _half_rows(rows, half):
    return pl.ds(half * (rows // 2), rows // 2) if rows >= MIN_SPLIT_ROWS else pl.ds(0, rows)


def _exchange_partials(tag, stacks):
    n = len(stacks)

    def body(*refs):
        ins, outs, send_sems, recv_sems = refs[:n], refs[n:2 * n], refs[2 * n], refs[2 * n + 1]
        x, y, c, chips = _place()
        mine = 2 * x + y
        _handshake([(x, y, 1 - c)] + [(*chip, c) for chip in chips] + [(*chip, 1 - c) for chip in chips])
        sent = []
        for t in range(n):
            r = stacks[t].shape[1]
            sent.append(_remote(ins[t].at[mine, _half_rows(r, 1 - c)], outs[t].at[0], send_sems, recv_sems, 7 * t, (x, y, 1 - c)))
            for j, chip in enumerate(chips):
                theirs = 2 * chip[0] + chip[1]
                sent.append(_remote(ins[t].at[theirs, _half_rows(r, c)], outs[t].at[1 + j], send_sems, recv_sems,
                                    7 * t + 1 + j, (*chip, c)))
                sent.append(_remote(ins[t].at[theirs, _half_rows(r, 1 - c)], outs[t].at[4 + j], send_sems, recv_sems,
                                    7 * t + 4 + j, (*chip, 1 - c)))
        for cp in sent:
            cp.start()
        for cp in sent:
            cp.wait_send()
        for t in range(n):
            for k in range(7):
                _remote(outs[t].at[k], outs[t].at[k], send_sems, recv_sems, 7 * t + k, (x, y, c)).wait_recv()

    def landing(s):
        return (7, s.shape[1] // 2 if s.shape[1] >= MIN_SPLIT_ROWS else s.shape[1], s.shape[2])

    return pl.kernel(
        body, name=f"reduce_exchange_{tag}", mesh=plsc.ScalarSubcoreMesh(**SEQUENCER),
        out_type=[jax.ShapeDtypeStruct(landing(s), s.dtype) for s in stacks],
        scratch_types=[pltpu.SemaphoreType.DMA((7 * n,)), pltpu.SemaphoreType.DMA((7 * n,))],
        compiler_params=pltpu.CompilerParams(collective_id=ID_EXCHANGE),
    )(*stacks)


def _share_halves(tag, halves):
    n = len(halves)

    def body(*refs):
        ins, outs, send_sems, recv_sems = refs[:n], refs[n:2 * n], refs[2 * n], refs[2 * n + 1]
        x, y, c, _ = _place()
        _handshake([(x, y, 1 - c)])
        sent = [_remote(ins[t], outs[t], send_sems, recv_sems, t, (x, y, 1 - c)) for t in range(n)]
        for cp in sent:
            cp.start()
        for cp in sent:
            cp.wait()

    return pl.kernel(
        body, name=f"reduce_share_{tag}", mesh=plsc.ScalarSubcoreMesh(**SEQUENCER),
        out_type=[jax.ShapeDtypeStruct(h.shape, h.dtype) for h in halves],
        scratch_types=[pltpu.SemaphoreType.DMA((n,)), pltpu.SemaphoreType.DMA((n,))],
        compiler_params=pltpu.CompilerParams(collective_id=ID_SHARE),
    )(*halves)


def _all_reduce_small(part):
    rows = part.shape[0]

    def body(p_ref, out_ref, sib_buf, chip_sums, send_sems, recv_sems):
        x, y, c, chips = _place()
        mine = 2 * x + y
        swap = _remote(p_ref, sib_buf, send_sems, recv_sems, 0, (x, y, 1 - c))
        swap.start()
        swap.wait()
        chip_sums[mine] = p_ref[...] + sib_buf[...]
        sent = [_remote(chip_sums.at[mine], chip_sums.at[mine], send_sems, recv_sems, 1 + j, (*chip, c))
                for j, chip in enumerate(chips)]
        for cp in sent:
            cp.start()
        for j, chip in enumerate(chips):
            sent[j].wait_send()
            theirs = chip_sums.at[2 * chip[0] + chip[1]]
            _remote(theirs, theirs, send_sems, recv_sems, 1 + j, (x, y, c)).wait_recv()
        out_ref[...] = ((chip_sums[0] + chip_sums[1]) + chip_sums[2]) + chip_sums[3]

    vmem = pl.BlockSpec(memory_space=pltpu.VMEM)
    return pl.pallas_call(
        body, name="all_reduce_small", in_specs=[vmem], out_specs=vmem, out_shape=jax.ShapeDtypeStruct(part.shape, F32),
        scratch_shapes=[pltpu.VMEM((rows, LANE), F32), pltpu.VMEM((N_SHARDS, rows, LANE), F32),
                        pltpu.SemaphoreType.DMA((4,)), pltpu.SemaphoreType.DMA((4,))],
        compiler_params=pltpu.CompilerParams(vmem_limit_bytes=VMEM_LIMIT_BYTES),
    )(part)


def _sum_partials(g3, others, sel):
    _, rows, c = others.shape
    whole = g3.shape[1] == rows
    tr = _tile(rows, 512)
    nb = rows // tr

    def body(sel_ref, g_ref, *rest):
        same = g_ref[...].astype(F32)
        for ref in rest[1:4]:
            same = same + ref[...].astype(F32)
        other = rest[0][...].astype(F32)
        for ref in rest[4:7]:
            other = other + ref[...].astype(F32)
        rest[7][...] = same + other

    blk = (None, tr, c)
    slots = [pl.BlockSpec(blk, functools.partial(lambda i, sr, k: (k, i, 0), k=k)) for k in range(7)]
    return pl.pallas_call(
        body, name="reduce_sum_partials",
        grid_spec=pltpu.PrefetchScalarGridSpec(
            num_scalar_prefetch=1, grid=(nb,),
            in_specs=[pl.BlockSpec(blk, lambda i, sr: (sr[0], (0 if whole else sr[1] * nb) + i, 0))] + slots,
            out_specs=pl.BlockSpec((tr, c), lambda i, sr: (i, 0))),
        out_shape=jax.ShapeDtypeStruct((rows, c), F32),
        compiler_params=_params(("parallel",)),
    )(sel, g3, *[others] * 7)


def _adamw_math(w, g, m, v):
    nm = ADAM_B1 * m + (1.0 - ADAM_B1) * g
    nv = ADAM_B2 * v + (1.0 - ADAM_B2) * (g * g)
    m_hat = nm / (1.0 - ADAM_B1 ** ADAM_STEP)
    v_hat = nv / (1.0 - ADAM_B2 ** ADAM_STEP)
    return -ADAM_LR * (m_hat / (jnp.sqrt(v_hat) + ADAM_EPS) + ADAM_WD * w), nm, nv


def _adamw_layer(layer, w, m, v, g_mine, g_sibling, sel, prev):
    lyr, r, c = w.shape
    rows = g_mine.shape[0]
    halves = r // rows
    tr = _tile(rows, 512)
    nb = rows // tr
    n_g = 1 if g_sibling is None else 2

    def body(sel_ref, w_ref, m_ref, v_ref, *rest):
        g = rest[0][...]
        if n_g == 2:
            g = jnp.where(pl.program_id(0) == sel_ref[1], g, rest[1][...])
        outs = rest[n_g + (0 if prev is None else 4):]
        d, nm, nv = _adamw_math(w_ref[...], g, m_ref[...], v_ref[...])
        for ref, val in zip(outs, (g, d, nm, nv)):
            ref[...] = val

    full = pl.BlockSpec((None, tr, c), lambda h, i, sr: (layer, h * nb + i, 0))
    part = pl.BlockSpec((tr, c), lambda h, i, sr: (i, 0))
    n_in = 4 + n_g
    return pl.pallas_call(
        body, name="adamw_layer",
        grid_spec=pltpu.PrefetchScalarGridSpec(
            num_scalar_prefetch=1, grid=(halves, nb),
            in_specs=[full] * 3 + [part] * n_g + ([] if prev is None else [pl.BlockSpec(memory_space=pl.ANY)] * 4),
            out_specs=[full] * 4),
        out_shape=[jax.ShapeDtypeStruct(w.shape, F32)] * 4,
        input_output_aliases={} if prev is None else {n_in + k: k for k in range(4)},
        compiler_params=_params(("parallel", "parallel")),
    )(sel, w, m, v, g_mine, *([] if g_sibling is None else [g_sibling]), *([] if prev is None else prev))


def _adamw(w, g, m, v):
    lyr, r, c = w.shape
    tr = _tile(r, 512)

    def body(w_ref, g_ref, m_ref, v_ref, d_ref, nm_ref, nv_ref):
        gv = g_ref[...]
        nm = ADAM_B1 * m_ref[...] + (1.0 - ADAM_B1) * gv
        nv = ADAM_B2 * v_ref[...] + (1.0 - ADAM_B2) * (gv * gv)
        m_hat = nm / (1.0 - ADAM_B1 ** ADAM_STEP)
        v_hat = nv / (1.0 - ADAM_B2 ** ADAM_STEP)
        d_ref[...] = -ADAM_LR * (m_hat / (jnp.sqrt(v_hat) + ADAM_EPS) + ADAM_WD * w_ref[...])
        nm_ref[...] = nm
        nv_ref[...] = nv

    blk = pl.BlockSpec((None, tr, c), lambda l, i: (l, i, 0))
    return pl.pallas_call(
        body, name="adamw", grid=(lyr, r // tr), in_specs=[blk] * 4, out_specs=[blk] * 3,
        out_shape=[jax.ShapeDtypeStruct(w.shape, F32)] * 3,
        compiler_params=_params(("parallel", "parallel")),
    )(w, g, m, v)


SHARDED = ("mla_w_dkv", "mla_w_uq", "mla_w_ukv", "mla_w_o", "sgu_w_in", "sgu_ln_g", "sgu_ln_b", "sgu_w_out",
           "ffn_w_up", "ffn_w_down")
REPLICATED = ("norm_mix", "norm_ffn", "final_norm", "mla_q_norm", "mla_kv_norm", "sgu_w_spatial", "sgu_b_spatial")
WEIGHTS = ("norm_mix", "norm_ffn", "final_norm", "mla_w_dkv", "mla_q_norm", "mla_kv_norm", "mla_w_uq", "mla_w_ukv",
           "mla_w_o", "sgu_w_in", "sgu_ln_g", "sgu_ln_b", "sgu_w_spatial", "sgu_b_spatial", "sgu_w_out", "ffn_w_up",
           "ffn_w_down")


class _Reducer:
    def __init__(self, state, sel):
        self.state, self.sel = state, sel
        self.started, self.travelling, self.summed = [], [], []
        self.done = {}

    def add(self, tag, layer, grads, token):
        names = list(grads)
        token, *tied = lax.optimization_barrier((token, *[a for n in names for a in grads[n]]))
        f32s, bf16s = tied[0::2], tied[1::2]
        received = _exchange_partials(tag, bf16s)
        own = [g if g.shape[1] >= MIN_SPLIT_ROWS else gb for g, gb in zip(f32s, bf16s)]
        self.started.append((tag, layer, names, own, received))
        return token

    def phase_end(self, token):
        for tag, layer, names, own, received in self.travelling:
            token, *received = lax.optimization_barrier((token, *received))
            mine = [_sum_partials(g, got, self.sel) for g, got in zip(own, received)]
            cut = [k for k, g in enumerate(own) if g.shape[1] >= MIN_SPLIT_ROWS]
            theirs = dict(zip(cut, _share_halves(tag, [mine[k] for k in cut])))
            self.summed.append((layer, names, mine, [theirs.get(k) for k in range(len(names))]))
            token = lax.optimization_barrier((token, *mine))[0]
        self.travelling, self.started = self.started, []
        return token

    def update(self, token):
        for layer, names, mine, theirs in self.summed:
            for name, g_mine, g_theirs in zip(names, mine, theirs):
                w, m, v = self.state[name]
                self.done[name] = _adamw_layer(layer, w, m, v, g_mine, g_theirs, self.sel, self.done.get(name))
                token = self.done[name][1]
        self.summed = []
        return token


def _as3d(name, a):
    return a.reshape(a.shape[0], -1, LANE) if name in ("sgu_ln_g", "sgu_ln_b") else a


def _pack(parts):
    flat = jnp.concatenate([p.reshape(-1) for p in parts])
    rows = -(-flat.shape[0] // (8 * LANE)) * 8
    return jnp.pad(flat, (0, rows * LANE - flat.shape[0])).reshape(rows, LANE)


def _unpack(packed, like):
    flat, out, at = packed.reshape(-1), [], 0
    for p in like:
        out.append(flat[at:at + p.size].reshape(p.shape))
        at += p.size
    return out


def kernel(x, positions, norm_mix, norm_ffn, final_norm, mla_w_dkv, mla_q_norm, mla_kv_norm, mla_w_uq, mla_w_ukv, mla_w_o, sgu_w_in, sgu_ln_g, sgu_ln_b, sgu_w_spatial, sgu_b_spatial, sgu_w_out, ffn_w_up, ffn_w_down, loss_target, m_norm_mix, m_norm_ffn, m_final_norm, m_mla_w_dkv, m_mla_q_norm, m_mla_kv_norm, m_mla_w_uq, m_mla_w_ukv, m_mla_w_o, m_sgu_w_in, m_sgu_ln_g, m_sgu_ln_b, m_sgu_w_spatial, m_sgu_b_spatial, m_sgu_w_out, m_ffn_w_up, m_ffn_w_down, v_norm_mix, v_norm_ffn, v_final_norm, v_mla_w_dkv, v_mla_q_norm, v_mla_kv_norm, v_mla_w_uq, v_mla_w_ukv, v_mla_w_o, v_sgu_w_in, v_sgu_ln_g, v_sgu_ln_b, v_sgu_w_spatial, v_sgu_b_spatial, v_sgu_w_out, v_ffn_w_up, v_ffn_w_down):
    given = dict(locals())
    w = {n: given[n] for n in WEIGHTS}
    mom = {n: given["m_" + n] for n in WEIGHTS}
    var = {n: given["v_" + n] for n in WEIGHTS}
    mixers, ffn, token = [], [], None
    for i in range(DEPTH):
        j = i // 2
        if i % 2 == 0:
            mixer = [w[n][j].astype(BF16) for n in ("mla_w_dkv", "mla_w_uq", "mla_w_ukv", "mla_w_o")]
        else:
            mixer = [sgu_w_in[j].astype(BF16), sgu_w_out[j].astype(BF16), sgu_ln_g[j].reshape(-1, LANE),
                     sgu_ln_b[j].reshape(-1, LANE)]
        for tag, shards, into in ((f"mixer{i}", mixer, mixers), (f"ffn{i}", [ffn_w_up[i].astype(BF16), ffn_w_down[i].astype(BF16)], ffn)):
            if token is not None:
                token, *shards = lax.optimization_barrier((token, *shards))
            token = shards[0]
            into.append(_gather_layer(tag, shards))

    x_i, y_i, c_i = lax.axis_index("x"), lax.axis_index("y"), lax.axis_index("c")
    sel = jnp.stack([2 * x_i + y_i, c_i]).astype(jnp.int32)
    reducer = _Reducer({n: tuple(_as3d(n, d[n]) for d in (w, mom, var)) for n in SHARDED}, sel)
    loss, dx, small, token = _local_step(
        x[0], positions[0], loss_target[0], norm_mix, norm_ffn, final_norm, mla_q_norm, mla_kv_norm, sgu_w_spatial,
        sgu_b_spatial, mixers, ffn, reducer)
    loss = lax.psum(loss, ("x", "y", "c"))

    small_g = [small["norm_mix"], small["norm_ffn"], small["final_norm"], small["q_norm"], small["kv_norm"],
               small["w_sp"], small["b_sp"]]
    like = [w[n] for n in REPLICATED]
    g_small = _all_reduce_small(lax.optimization_barrier((_pack(small_g), token))[0])
    packed = [_pack([d[n] for n in REPLICATED])[None] for d in (w, mom, var)]
    upd_small = _adamw(packed[0], g_small[None], packed[1], packed[2])
    grads = dict(zip(REPLICATED, _unpack(g_small, like)))
    delta, new_m, new_v = ({n: a for n, a in zip(REPLICATED, _unpack(u[0], like))} for u in upd_small)

    token = reducer.phase_end(reducer.update(upd_small[0]))
    reducer.update(None)
    loss = lax.optimization_barrier((loss, token))[0]
    for n in SHARDED:
        grads[n], delta[n], new_m[n], new_v[n] = (a.reshape(w[n].shape) for a in reducer.done[n])

    return (loss, dx[None], *[grads[n] for n in WEIGHTS], *[delta[n] for n in WEIGHTS],
            *[new_m[n] for n in WEIGHTS], *[new_v[n] for n in WEIGHTS])
```

```python
import functools
import math

import jax
import jax.numpy as jnp
from jax import lax
from jax.experimental import pallas as pl
from jax.experimental.pallas import tpu as pltpu
from jax.experimental.pallas import tpu_sc as plsc

F32 = jnp.float32
BF16 = jnp.bfloat16
MESH = pl.DeviceIdType.MESH

DEPTH = 4
HEADS = 8
NOPE = 128
ROPE = 64
VHEAD = 128
QK_HEAD = NOPE + ROPE
Q_RANK = 256
KV_RANK = 128
HEAD_PAD = 256
LAT_PAD = 512
ROPE_THETA = 10000.0
SGU_CHUNK = 128
SGU_GROUPS = 8
NORM_EPS = 1e-6
LN_EPS = 1e-5
ADAM_LR, ADAM_B1, ADAM_B2, ADAM_EPS, ADAM_WD, ADAM_STEP = 0.001, 0.9, 0.999, 1e-08, 0.01, 10

N_SHARDS = 4
LANE = 128
VMEM_LIMIT_BYTES = 56 * 1024 * 1024
ATT_TILE = 512
MM_TILE = 1024
ATT_SCALE = QK_HEAD ** -0.5
LOG2_SCALE = ATT_SCALE * math.log2(math.e)

NN = (((1,), (0,)), ((), ()))
NT = (((1,), (1,)), ((), ()))
TN = (((0,), (0,)), ((), ()))


def _params(sem):
    return pltpu.CompilerParams(dimension_semantics=sem, vmem_limit_bytes=VMEM_LIMIT_BYTES)


def _tile(n, pref):
    t = min(n, pref)
    while n % t:
        t //= 2
    return t


def _matmul(name, a, b, a_spec, b_spec, dims, grid, tile, outs, extras=(), epilogue=None, aliased=()):
    nk, ne, no = grid[2], len(extras), len(outs)

    def body(a_ref, b_ref, *rest):
        e_refs, o_refs = rest[:ne], rest[ne + len(aliased):ne + len(aliased) + no]
        part = lax.dot_general(a_ref[...].astype(BF16), b_ref[...].astype(BF16), dims, preferred_element_type=F32)

        def finish(acc):
            vals = (acc,) if epilogue is None else epilogue(acc, *[e[...] for e in e_refs])
            for o_ref, v in zip(o_refs, vals):
                o_ref[...] = v.astype(o_ref.dtype)

        if nk == 1:
            finish(part)
            return
        acc_ref, k = rest[-1], pl.program_id(2)

        @pl.when(k == 0)
        def _():
            acc_ref[...] = part

        @pl.when(jnp.logical_and(k > 0, k < nk - 1))
        def _():
            acc_ref[...] += part

        @pl.when(k == nk - 1)
        def _():
            finish(acc_ref[...] + part)

    n_in = 2 + ne
    return pl.pallas_call(
        body, name=name, grid=grid,
        in_specs=[a_spec, b_spec] + [s for _, s in extras] + [pl.BlockSpec(memory_space=pl.ANY)] * len(aliased),
        out_specs=[s for _, s in outs], out_shape=[s for s, _ in outs],
        scratch_shapes=[pltpu.VMEM(tile, F32)] if nk > 1 else [],
        input_output_aliases={n_in + i: o for i, (_, o) in enumerate(aliased)},
        compiler_params=_params(("parallel", "parallel", "arbitrary")),
    )(a, b, *[e for e, _ in extras], *[arr for arr, _ in aliased])


def _mm(name, a, b, out_dtypes=(F32,), epilogue=None, extras=(), tm=MM_TILE, tn=MM_TILE, tk=MM_TILE, nt=False):
    m, kd = a.shape
    n = b.shape[0] if nt else b.shape[1]
    tm, tn, tk = _tile(m, tm), _tile(n, tn), _tile(kd, tk)
    o_spec = pl.BlockSpec((tm, tn), lambda i, j, k: (i, j))
    b_spec = pl.BlockSpec((tn, tk), lambda i, j, k: (j, k)) if nt else pl.BlockSpec((tk, tn), lambda i, j, k: (k, j))
    return _matmul(name, a, b, pl.BlockSpec((tm, tk), lambda i, j, k: (i, k)), b_spec, NT if nt else NN,
                   (m // tm, n // tn, kd // tk), (tm, tn),
                   [(jax.ShapeDtypeStruct((m, n), d), o_spec) for d in out_dtypes],
                   [(e, o_spec) for e in extras], epilogue)


def _mm_tn(name, a, b, out_dtypes=(F32,), tm=MM_TILE, tn=MM_TILE, tk=MM_TILE):
    s, m = a.shape
    n = b.shape[1]
    tm, tn, tk = _tile(m, tm), _tile(n, tn), _tile(s, tk)
    o_spec = pl.BlockSpec((tm, tn), lambda i, j, k: (i, j))
    return _matmul(name, a, b, pl.BlockSpec((tk, tm), lambda i, j, k: (k, i)),
                   pl.BlockSpec((tk, tn), lambda i, j, k: (k, j)), TN, (m // tm, n // tn, s // tk), (tm, tn),
                   [(jax.ShapeDtypeStruct((m, n), d), o_spec) for d in out_dtypes])


def _mm_stacked(name, a, w3, mode, out_dtypes=(F32,), epilogue=None, extras=(), tm=MM_TILE, tn=MM_TILE, tk=MM_TILE):
    m, kd = a.shape
    _, r, c = w3.shape
    n = c if mode == "row" else N_SHARDS * c
    tm = _tile(m, tm)
    if mode == "row":
        tn, tk = _tile(n, tn), _tile(r, tk)
        per = r // tk
        b_spec = pl.BlockSpec((None, tk, tn), lambda i, j, k: (k // per, k % per, j))
    else:
        tn, tk = _tile(c, tn), _tile(kd, tk)
        per = c // tn
        b_spec = pl.BlockSpec((None, tk, tn), lambda i, j, k: (j // per, k, j % per))
    o_spec = pl.BlockSpec((tm, tn), lambda i, j, k: (i, j))
    return _matmul(name, a, w3, pl.BlockSpec((tm, tk), lambda i, j, k: (i, k)), b_spec, NN,
                   (m // tm, n // tn, kd // tk), (tm, tn),
                   [(jax.ShapeDtypeStruct((m, n), d), o_spec) for d in out_dtypes],
                   [(e, o_spec) for e in extras], epilogue)


def _mm_stacked_nt(name, a, w3, mode, out_dtypes=(F32,), epilogue=None, extras=(), tm=MM_TILE, tn=MM_TILE, tk=MM_TILE):
    m, nd = a.shape
    _, r, c = w3.shape
    kout = N_SHARDS * r if mode == "row" else r
    tm = _tile(m, tm)
    if mode == "row":
        tn, tk = _tile(r, tn), _tile(c, tk)
        per = r // tn
        b_spec = pl.BlockSpec((None, tn, tk), lambda i, j, k: (j // per, j % per, k))
    else:
        tn, tk = _tile(r, tn), _tile(c, tk)
        per = c // tk
        b_spec = pl.BlockSpec((None, tn, tk), lambda i, j, k: (k // per, j, k % per))
    o_spec = pl.BlockSpec((tm, tn), lambda i, j, k: (i, j))
    return _matmul(name, a, w3, pl.BlockSpec((tm, tk), lambda i, j, k: (i, k)), b_spec, NT,
                   (m // tm, kout // tn, nd // tk), (tm, tn),
                   [(jax.ShapeDtypeStruct((m, kout), d), o_spec) for d in out_dtypes],
                   [(e, o_spec) for e in extras], epilogue)


def _mm_tn_stacked(name, a, b, shape3, mode, tm=MM_TILE, tn=MM_TILE, tk=MM_TILE):
    s, m = a.shape
    n = b.shape[1]
    _, r, c = shape3
    tk = _tile(s, tk)
    if mode == "row":
        tm, tn = _tile(r, tm), _tile(n, tn)
        per = r // tm
        o_spec = pl.BlockSpec((None, tm, tn), lambda i, j, k: (i // per, i % per, j))
    else:
        tm, tn = _tile(m, tm), _tile(c, tn)
        per = c // tn
        o_spec = pl.BlockSpec((None, tm, tn), lambda i, j, k: (j // per, i, j % per))
    outs = [(jax.ShapeDtypeStruct(shape3, F32), o_spec), (jax.ShapeDtypeStruct(shape3, BF16), o_spec)]
    return _matmul(name, a, b, pl.BlockSpec((tk, tm), lambda i, j, k: (k, i)),
                   pl.BlockSpec((tk, tn), lambda i, j, k: (k, j)), TN, (m // tm, n // tn, s // tk), (tm, tn),
                   outs, epilogue=lambda acc: (acc, acc))


def _rowwise(name, fn, rows, consts, out_rows, out_accs=(), tr=256):
    nr, nc, no = len(rows), len(consts), len(out_rows)
    n_rows = rows[0].shape[0]
    tr = _tile(n_rows, tr)

    def body(*refs):
        vals = fn(*[r[...] for r in refs[:nr + nc]])
        o_refs, a_refs = refs[nr + nc:nr + nc + no], refs[nr + nc + no:]
        for ref, v in zip(o_refs, vals[:no]):
            ref[...] = v.astype(ref.dtype)
        first = pl.program_id(0) == 0

        @pl.when(first)
        def _():
            for ref, v in zip(a_refs, vals[no:]):
                ref[...] = v

        @pl.when(jnp.logical_not(first))
        def _():
            for ref, v in zip(a_refs, vals[no:]):
                ref[...] += v

    def whole(shape):
        return pl.BlockSpec(shape, lambda i: (0,) * len(shape))

    return pl.pallas_call(
        body, name=name, grid=(n_rows // tr,),
        in_specs=[pl.BlockSpec((tr, a.shape[1]), lambda i: (i, 0)) for a in rows] + [whole(c.shape) for c in consts],
        out_specs=[pl.BlockSpec((tr, f), lambda i: (i, 0)) for f, _ in out_rows] + [whole(s) for s in out_accs],
        out_shape=[jax.ShapeDtypeStruct((n_rows, f), d) for f, d in out_rows]
        + [jax.ShapeDtypeStruct(s, F32) for s in out_accs],
        compiler_params=_params(("arbitrary",)),
    )(*rows, *consts)


def _rms_fwd(x, g):
    return x * lax.rsqrt(jnp.mean(x * x, axis=-1, keepdims=True) + NORM_EPS) * g


def _rms_bwd(dy, x, g):
    rstd = lax.rsqrt(jnp.mean(x * x, axis=-1, keepdims=True) + NORM_EPS)
    n = x * rstd
    dn = dy * g
    dx = rstd * (dn - n * jnp.mean(dn * n, axis=-1, keepdims=True))
    return dx, jnp.sum(dy * n, axis=0, keepdims=True)


def _rope(x, cs, s1, s2):
    return x * cs + pltpu.roll(x, 32, 1) * s1 + pltpu.roll(x, 96, 1) * s2


def _rope_t(dy, cs, s1, s2):
    return dy * cs + pltpu.roll(dy * s1, 96, 1) + pltpu.roll(dy * s2, 32, 1)


def _gelu(z):
    return 0.5 * z * (1.0 + lax.erf(z * (1.0 / math.sqrt(2.0))))


def _gelu_grad(z):
    return 0.5 * (1.0 + lax.erf(z * (1.0 / math.sqrt(2.0)))) + z * jnp.exp(-0.5 * z * z) * (1.0 / math.sqrt(2.0 * math.pi))


def _att_scores(q, kv, kr, scale, masked, transposed):
    k = jnp.concatenate([kv[:, :NOPE], kr], axis=1)
    if transposed:
        s = lax.dot_general(k, q, NT, preferred_element_type=F32) * scale
    else:
        s = lax.dot_general(q, k, NT, preferred_element_type=F32) * scale
    if masked:
        r = lax.broadcasted_iota(jnp.int32, s.shape, 0)
        c = lax.broadcasted_iota(jnp.int32, s.shape, 1)
        s = jnp.where((r <= c) if transposed else (c <= r), s, -jnp.inf)
    return s, k


def _flash_fwd(qb, kvb, krb):
    s_len = qb.shape[0]
    t = ATT_TILE

    def body(q_ref, kv_ref, kr_ref, o_ref, lse_ref, m_s, l_s, acc_s):
        qi = pl.program_id(1)
        m_s[...] = jnp.full_like(m_s, -jnp.inf)
        l_s[...] = jnp.zeros_like(l_s)
        acc_s[...] = jnp.zeros_like(acc_s)
        q = q_ref[...]

        def step(ki, masked):
            rows = pl.ds(pl.multiple_of(ki * t, t), t)
            kv = kv_ref[rows, :]
            s, _ = _att_scores(q, kv, kr_ref[rows, :], LOG2_SCALE, masked, False)
            m_prev = m_s[...]
            m_new = jnp.maximum(m_prev, jnp.max(s, axis=1, keepdims=True))
            alpha = jnp.exp2(m_prev - m_new)
            p = jnp.exp2(s - jnp.tile(m_new, (1, t // LANE)))
            l_s[...] = alpha * l_s[...] + jnp.sum(p, axis=1, keepdims=True)
            acc_s[...] = alpha * acc_s[...] + jnp.dot(p.astype(BF16), kv[:, NOPE:], preferred_element_type=F32)
            m_s[...] = m_new

        def unmasked(ki, carry):
            step(ki, False)
            return carry

        lax.fori_loop(0, qi, unmasked, 0)
        step(qi, True)
        o_ref[...] = (acc_s[...] / l_s[...]).astype(o_ref.dtype)
        lse_ref[...] = (m_s[...] + jnp.log2(l_s[...]))[:, :1]

    return pl.pallas_call(
        body, name="flash_fwd", grid=(HEADS, s_len // t),
        in_specs=[pl.BlockSpec((t, HEAD_PAD), lambda h, qi: (qi, h)),
                  pl.BlockSpec((s_len, HEAD_PAD), lambda h, qi: (0, h)),
                  pl.BlockSpec((s_len, LANE), lambda h, qi: (0, 0))],
        out_specs=[pl.BlockSpec((t, VHEAD), lambda h, qi: (qi, h)),
                   pl.BlockSpec((None, t, 1), lambda h, qi: (h, qi, 0))],
        out_shape=[jax.ShapeDtypeStruct((s_len, HEADS * VHEAD), BF16),
                   jax.ShapeDtypeStruct((HEADS, s_len, 1), F32)],
        scratch_shapes=[pltpu.VMEM((t, LANE), F32), pltpu.VMEM((t, LANE), F32), pltpu.VMEM((t, VHEAD), F32)],
        compiler_params=_params(("parallel", "arbitrary")),
    )(qb, kvb, krb)


def _flash_bwd_dq(qb, kvb, krb, dob, lse, delta):
    s_len = qb.shape[0]
    t = ATT_TILE
    nq = s_len // t
    scale = QK_HEAD ** -0.5

    def body(q_ref, kv_ref, kr_ref, do_ref, lse_ref, dl_ref, dq_ref, acc_s):
        qi = pl.program_id(1)
        acc_s[...] = jnp.zeros_like(acc_s)
        q, do = q_ref[...], do_ref[...]
        lse = jnp.broadcast_to(lse_ref[...], (t, LANE))
        dl = jnp.broadcast_to(dl_ref[...], (t, LANE))

        def step(ki, masked):
            rows = pl.ds(pl.multiple_of(ki * t, t), t)
            kv = kv_ref[rows, :]
            s, k = _att_scores(q, kv, kr_ref[rows, :], LOG2_SCALE, masked, False)
            p = jnp.exp2(s - jnp.tile(lse, (1, t // LANE)))
            dp = lax.dot_general(do, kv[:, NOPE:], NT, preferred_element_type=F32)
            ds = (p * (dp - jnp.tile(dl, (1, t // LANE))) * scale).astype(BF16)
            acc_s[...] += jnp.dot(ds, k, preferred_element_type=F32)

        def unmasked(ki, carry):
            step(ki, False)
            return carry

        lax.fori_loop(0, qi, unmasked, 0)
        step(qi, True)
        dq_ref[...] = acc_s[...]

    col = pl.BlockSpec((None, t, 1), lambda h, qi: (h, qi, 0))
    return pl.pallas_call(
        body, name="flash_bwd_dq", grid=(HEADS, nq),
        in_specs=[pl.BlockSpec((t, HEAD_PAD), lambda h, qi: (qi, h)),
                  pl.BlockSpec((s_len, HEAD_PAD), lambda h, qi: (0, h)),
                  pl.BlockSpec((s_len, LANE), lambda h, qi: (0, 0)),
                  pl.BlockSpec((t, VHEAD), lambda h, qi: (qi, h)), col, col],
        out_specs=pl.BlockSpec((t, HEAD_PAD), lambda h, qi: (qi, h)),
        out_shape=jax.ShapeDtypeStruct((s_len, HEADS * HEAD_PAD), F32),
        scratch_shapes=[pltpu.VMEM((t, HEAD_PAD), F32)],
        compiler_params=_params(("parallel", "arbitrary")),
    )(qb, kvb, krb, dob, lse, delta)


def _flash_bwd_dkv(qb, kvb, krb, dob, lse_row, delta_row):
    s_len = qb.shape[0]
    t = ATT_TILE
    nq = s_len // t
    scale = QK_HEAD ** -0.5

    def body(q_ref, kv_ref, kr_ref, do_ref, lse_ref, dl_ref, dkv_ref, dkr_ref, dk_s, dv_s):
        ki = pl.program_id(1)
        dk_s[...] = jnp.zeros_like(dk_s)
        dv_s[...] = jnp.zeros_like(dv_s)
        kv, kr = kv_ref[...], kr_ref[...]

        def step(qi, masked):
            rows = pl.ds(pl.multiple_of(qi * t, t), t)
            q, do = q_ref[rows, :], do_ref[rows, :]
            st, _ = _att_scores(q, kv, kr, LOG2_SCALE, masked, True)
            pt = jnp.exp2(st - lse_ref[:, rows])
            dv_s[...] += jnp.dot(pt.astype(BF16), do, preferred_element_type=F32)
            dpt = lax.dot_general(kv[:, NOPE:], do, NT, preferred_element_type=F32)
            dst = (pt * (dpt - dl_ref[:, rows]) * scale).astype(BF16)
            dk_s[...] += jnp.dot(dst, q, preferred_element_type=F32)

        def unmasked(qi, carry):
            step(qi, False)
            return carry

        step(ki, True)
        lax.fori_loop(ki + 1, nq, unmasked, 0)
        dk = dk_s[...]
        dkv_ref[...] = jnp.concatenate([dk[:, :NOPE], dv_s[...]], axis=1).astype(dkv_ref.dtype)
        dkr_ref[...] = dk[:, NOPE:]

    row = pl.BlockSpec((None, 1, s_len), lambda h, ki: (h, 0, 0))
    return pl.pallas_call(
        body, name="flash_bwd_dkv", grid=(HEADS, nq),
        in_specs=[pl.BlockSpec((s_len, HEAD_PAD), lambda h, ki: (0, h)),
                  pl.BlockSpec((t, HEAD_PAD), lambda h, ki: (ki, h)),
                  pl.BlockSpec((t, LANE), lambda h, ki: (ki, 0)),
                  pl.BlockSpec((s_len, VHEAD), lambda h, ki: (0, h)), row, row],
        out_specs=[pl.BlockSpec((t, HEAD_PAD), lambda h, ki: (ki, h)),
                   pl.BlockSpec((t, LANE), lambda h, ki: (ki, h))],
        out_shape=[jax.ShapeDtypeStruct((s_len, HEADS * HEAD_PAD), BF16),
                   jax.ShapeDtypeStruct((s_len, HEADS * LANE), F32)],
        scratch_shapes=[pltpu.VMEM((t, HEAD_PAD), F32), pltpu.VMEM((t, VHEAD), F32)],
        compiler_params=_params(("parallel", "parallel")),
    )(qb, kvb, krb, dob, lse_row, delta_row)


def _att_delta(do, ob):
    s_len = do.shape[0]
    t = ATT_TILE

    def body(do_ref, o_ref, dob_ref, dl_ref):
        d = do_ref[...]
        dob_ref[...] = d.astype(dob_ref.dtype)
        dl_ref[...] = jnp.sum(d * o_ref[...].astype(F32), axis=1, keepdims=True)

    blk = pl.BlockSpec((t, VHEAD), lambda i, h: (i, h))
    return pl.pallas_call(
        body, name="att_delta", grid=(s_len // t, HEADS), in_specs=[blk, blk],
        out_specs=[blk, pl.BlockSpec((None, t, 1), lambda i, h: (h, i, 0))],
        out_shape=[jax.ShapeDtypeStruct(do.shape, BF16), jax.ShapeDtypeStruct((HEADS, s_len, 1), F32)],
        compiler_params=_params(("parallel", "parallel")),
    )(do, ob)


def _tril(w):
    r = lax.broadcasted_iota(jnp.int32, w.shape, 0)
    c = lax.broadcasted_iota(jnp.int32, w.shape, 1)
    return jnp.where(c <= r, w, 0.0)


def _sgu_mix(w_ref, vln, gd):
    wcs = [_tril(w_ref[g]).astype(BF16) for g in range(SGU_GROUPS)]
    mixed = jnp.concatenate(
        [jnp.dot(wcs[g], vln[:, g * gd:(g + 1) * gd], preferred_element_type=F32) for g in range(SGU_GROUPS)], axis=1)
    return wcs, mixed


def _sgu_fwd(zpre, ln_g, ln_b, w_sp, bias_full):
    s_len, two_w = zpre.shape
    width = two_w // 2
    gd = width // SGU_GROUPS
    t = SGU_CHUNK

    def body(z_ref, g_ref, b_ref, w_ref, bias_ref, uv_ref):
        u = _gelu(z_ref[:, :width])
        v = _gelu(z_ref[:, width:])
        d = v - jnp.mean(v, axis=-1, keepdims=True)
        vhat = d * lax.rsqrt(jnp.mean(d * d, axis=-1, keepdims=True) + LN_EPS)
        vln = (vhat * g_ref[...] + b_ref[...]).astype(BF16)
        _, mixed = _sgu_mix(w_ref, vln, gd)
        uv_ref[...] = (u * (mixed + bias_ref[...])).astype(uv_ref.dtype)

    return pl.pallas_call(
        body, name="sgu_fwd", grid=(s_len // t,),
        in_specs=[pl.BlockSpec((t, two_w), lambda i: (i, 0)), pl.BlockSpec((1, width), lambda i: (0, 0)),
                  pl.BlockSpec((1, width), lambda i: (0, 0)), pl.BlockSpec(w_sp.shape, lambda i: (0, 0, 0)),
                  pl.BlockSpec((t, width), lambda i: (0, 0))],
        out_specs=pl.BlockSpec((t, width), lambda i: (i, 0)),
        out_shape=jax.ShapeDtypeStruct((s_len, width), BF16),
        compiler_params=_params(("parallel",)),
    )(zpre, ln_g, ln_b, w_sp, bias_full)


def _sgu_bwd(zpre, duv, ln_g, ln_b, w_sp, bias_full):
    s_len, two_w = zpre.shape
    width = two_w // 2
    gd = width // SGU_GROUPS
    t = SGU_CHUNK

    def body(z_ref, duv_ref, g_ref, b_ref, w_ref, bias_ref, dz_ref, dg_ref, db_ref, dw_ref, dbias_ref):
        first = pl.program_id(0) == 0

        def accumulate(ref, val):
            @pl.when(first)
            def _():
                ref[...] = val

            @pl.when(jnp.logical_not(first))
            def _():
                ref[...] += val

        zu, zv = z_ref[:, :width], z_ref[:, width:]
        u = _gelu(zu)
        v = _gelu(zv)
        d = v - jnp.mean(v, axis=-1, keepdims=True)
        rstd = lax.rsqrt(jnp.mean(d * d, axis=-1, keepdims=True) + LN_EPS)
        vhat = d * rstd
        vln = (vhat * g_ref[...] + b_ref[...]).astype(BF16)
        wcs, mixed = _sgu_mix(w_ref, vln, gd)
        duv_v = duv_ref[...]
        du = duv_v * (mixed + bias_ref[...])
        dmixed = duv_v * u
        dmb = dmixed.astype(BF16)
        dvln = jnp.concatenate(
            [lax.dot_general(wcs[g], dmb[:, g * gd:(g + 1) * gd], TN, preferred_element_type=F32)
             for g in range(SGU_GROUPS)], axis=1)
        for g in range(SGU_GROUPS):
            dw = lax.dot_general(dmb[:, g * gd:(g + 1) * gd], vln[:, g * gd:(g + 1) * gd], NT, preferred_element_type=F32)
            accumulate(dw_ref.at[g], _tril(dw))
        dvhat = dvln * g_ref[...]
        dv0 = rstd * (dvhat - jnp.mean(dvhat, axis=-1, keepdims=True)
                      - vhat * jnp.mean(dvhat * vhat, axis=-1, keepdims=True))
        dz_ref[:, :width] = (du * _gelu_grad(zu)).astype(dz_ref.dtype)
        dz_ref[:, width:] = (dv0 * _gelu_grad(zv)).astype(dz_ref.dtype)
        accumulate(dg_ref, jnp.sum(dvln * vhat, axis=0, keepdims=True))
        accumulate(db_ref, jnp.sum(dvln, axis=0, keepdims=True))
        accumulate(dbias_ref, dmixed)

    vec = pl.BlockSpec((1, width), lambda i: (0, 0))
    return pl.pallas_call(
        body, name="sgu_bwd", grid=(s_len // t,),
        in_specs=[pl.BlockSpec((t, two_w), lambda i: (i, 0)), pl.BlockSpec((t, width), lambda i: (i, 0)), vec, vec,
                  pl.BlockSpec(w_sp.shape, lambda i: (0, 0, 0)), pl.BlockSpec((t, width), lambda i: (0, 0))],
        out_specs=[pl.BlockSpec((t, two_w), lambda i: (i, 0)), vec, vec,
                   pl.BlockSpec(w_sp.shape, lambda i: (0, 0, 0)), pl.BlockSpec((t, width), lambda i: (0, 0))],
        out_shape=[jax.ShapeDtypeStruct((s_len, two_w), BF16), jax.ShapeDtypeStruct((1, width), F32),
                   jax.ShapeDtypeStruct((1, width), F32), jax.ShapeDtypeStruct(w_sp.shape, F32),
                   jax.ShapeDtypeStruct((t, width), F32)],
        compiler_params=_params(("arbitrary",)),
    )(zpre, duv, ln_g, ln_b, w_sp, bias_full)


def _rope_tables(positions):
    inv_freq = ROPE_THETA ** (-jnp.arange(0, ROPE, 2, dtype=F32) / ROPE)
    ang = positions.astype(F32)[:, None] * inv_freq
    cos, sin = jnp.cos(ang), jnp.sin(ang)
    z32, z64 = jnp.zeros_like(cos), jnp.zeros((cos.shape[0], LANE - ROPE), F32)
    return (jnp.concatenate([cos, cos, z64], axis=1), jnp.concatenate([z32, sin, z64], axis=1),
            jnp.concatenate([-sin, z32, z64], axis=1))


def _q_rope(name, q, tables, transpose):
    rot = _rope_t if transpose else _rope

    def fn(qv, cs, s1, s2):
        parts = []
        for h in range(HEADS):
            parts.append(qv[:, h * HEAD_PAD:h * HEAD_PAD + NOPE])
            parts.append(rot(qv[:, h * HEAD_PAD + NOPE:(h + 1) * HEAD_PAD], cs, s1, s2))
        return (jnp.concatenate(parts, axis=1),)

    return _rowwise(name, fn, [q, *tables], [], [(q.shape[1], BF16)])[0]


def _ffn_fwd(x, g, w_up3, w_down3):
    h2 = _rowwise("ffn_norm", lambda xv, gv: (_rms_fwd(xv, gv),), [x], [g], [(x.shape[1], BF16)])[0]

    def sq_relu(acc):
        r = jnp.maximum(acc, 0.0)
        return (r * r,)

    r = _mm_stacked("ffn_up", h2, w_up3, "col", (BF16,), sq_relu)[0]
    x_out = _mm_stacked("ffn_down", r, w_down3, "row", (F32,), lambda acc, res: (acc + res,), [x])[0]
    return x_out, (x, h2, r)


def _ffn_bwd(dx, dxb, saved, g, w_up3, w_down3):
    x, h2, r = saved
    da = _mm_stacked_nt("ffn_down_dx", dxb, w_down3, "row", (BF16,),
                        lambda acc, rv: (acc * (2.0 * jnp.sqrt(rv.astype(F32))),), [r])[0]
    g_down = _mm_tn_stacked("ffn_down_dw", r, dxb, w_down3.shape, "row")
    dh2 = _mm_stacked_nt("ffn_up_dx", da, w_up3, "col", (F32,))[0]
    g_up = _mm_tn_stacked("ffn_up_dw", h2, da, w_up3.shape, "col")
    dx, dxb, dg = _norm_bwd("ffn_norm_bwd", dh2, x, g, dx)
    return dx, dxb, dg, g_up, g_down


def _norm_bwd(name, dh, x, g, dres):
    def fn(dhv, xv, rv, gv):
        dxv, dg = _rms_bwd(dhv, xv, gv)
        return dxv + rv, dxv + rv, dg

    return _rowwise(name, fn, [dh, x, dres], [g], [(x.shape[1], F32), (x.shape[1], BF16)], [g.shape])


def _mla_fwd(x, g, wdkv, q_norm, kv_norm, wq, wkv, wo, tables):
    d = x.shape[1]
    h = _rowwise("mla_norm", lambda xv, gv: (_rms_fwd(xv, gv),), [x], [g], [(d, BF16)])[0]
    lat = _mm("mla_dkv", h, wdkv)[0]

    def lat_post(lv, cs, s1, s2, qg, kg):
        return (_rms_fwd(lv[:, :Q_RANK], qg), _rms_fwd(lv[:, Q_RANK:Q_RANK + KV_RANK], kg),
                _rope(lv[:, Q_RANK + KV_RANK:], cs, s1, s2))

    cqn, ckvn, krb = _rowwise("mla_lat", lat_post, [lat, *tables], [q_norm, kv_norm],
                              [(Q_RANK, BF16), (KV_RANK, BF16), (LANE, BF16)])
    q = _mm("mla_uq", cqn, wq)[0]
    kvb = _mm("mla_ukv", ckvn, wkv, (BF16,))[0]
    qb = _q_rope("mla_q_rope", q, tables, False)
    ob, lse = _flash_fwd(qb, kvb, krb)
    x_mid = _mm("mla_o", ob, wo, (F32,), lambda acc, res: (acc + res,), [x])[0]
    return x_mid, (x, h, lat, cqn, ckvn, krb, qb, kvb, ob, lse)


def _mla_bwd(dx, dxb, saved, g, wdkv, q_norm, kv_norm, wq, wkv, wo, tables):
    x, h, lat, cqn, ckvn, krb, qb, kvb, ob, lse = saved
    s_len = x.shape[0]
    do = _mm("mla_o_dx", dxb, wo, nt=True)[0]
    g_wo = _mm_tn("mla_o_dw", ob, dxb)[0]
    dob, delta = _att_delta(do, ob)
    dq = _flash_bwd_dq(qb, kvb, krb, dob, lse, delta)
    dkvb, dkr = _flash_bwd_dkv(qb, kvb, krb, dob, lse.reshape(HEADS, 1, s_len), delta.reshape(HEADS, 1, s_len))
    dqb = _q_rope("mla_q_rope_bwd", dq, tables, True)
    dcqn = _mm("mla_uq_dx", dqb, wq, nt=True)[0]
    g_wq = _mm_tn("mla_uq_dw", cqn, dqb)[0]
    dckvn = _mm("mla_ukv_dx", dkvb, wkv, nt=True)[0]
    g_wkv = _mm_tn("mla_ukv_dw", ckvn, dkvb)[0]

    def lat_bwd(dq_v, dkv_v, dkr_v, lv, cs, s1, s2, qg, kg):
        dcq, dqg = _rms_bwd(dq_v, lv[:, :Q_RANK], qg)
        dckv, dkg = _rms_bwd(dkv_v, lv[:, Q_RANK:Q_RANK + KV_RANK], kg)
        dkr_sum = dkr_v[:, :LANE]
        for hd in range(1, HEADS):
            dkr_sum = dkr_sum + dkr_v[:, hd * LANE:(hd + 1) * LANE]
        return jnp.concatenate([dcq, dckv, _rope_t(dkr_sum, cs, s1, s2)], axis=1), dqg, dkg

    dlat, g_qn, g_kvn = _rowwise("mla_lat_bwd", lat_bwd, [dcqn, dckvn, dkr, lat, *tables], [q_norm, kv_norm],
                                 [(LAT_PAD, BF16)], [q_norm.shape, kv_norm.shape])
    dh = _mm("mla_dkv_dx", dlat, wdkv, nt=True)[0]
    g_wdkv = _mm_tn("mla_dkv_dw", h, dlat)[0]
    dx, dxb, dg = _norm_bwd("mla_norm_bwd", dh, x, g, dx)
    return dx, dxb, dg, g_wdkv, g_qn, g_kvn, g_wq, g_wkv, g_wo


def _sgu_layer_fwd(x, g, w_in3, ln_g, ln_b, w_sp, bias_full, w_out3):
    h = _rowwise("sgu_norm", lambda xv, gv: (_rms_fwd(xv, gv),), [x], [g], [(x.shape[1], BF16)])[0]
    zpre = _mm_stacked("sgu_in", h, w_in3, "col")[0]
    uv = _sgu_fwd(zpre, ln_g, ln_b, w_sp, bias_full)
    x_mid = _mm_stacked("sgu_out", uv, w_out3, "row", (F32,), lambda acc, res: (acc + res,), [x])[0]
    return x_mid, (x, h, zpre, uv)


def _sgu_layer_bwd(dx, dxb, saved, g, w_in3, ln_g, ln_b, w_sp, bias_full, w_out3):
    x, h, zpre, uv = saved
    duv = _mm_stacked_nt("sgu_out_dx", dxb, w_out3, "row")[0]
    g_out = _mm_tn_stacked("sgu_out_dw", uv, dxb, w_out3.shape, "row")
    dz, g_lng, g_lnb, g_wsp, g_bias = _sgu_bwd(zpre, duv, ln_g, ln_b, w_sp, bias_full)
    dh = _mm_stacked_nt("sgu_in_dx", dz, w_in3, "col")[0]
    g_in = _mm_tn_stacked("sgu_in_dw", h, dz, w_in3.shape, "col")
    dx, dxb, dg = _norm_bwd("sgu_norm_bwd", dh, x, g, dx)
    return dx, dxb, dg, g_in, g_out, g_lng, g_lnb, g_wsp, g_bias


def _loss_head(x, target, g):
    d = x.shape[1]

    def fn(xv, tv, gv):
        err = _rms_fwd(xv, gv) - tv
        dxv, dg = _rms_bwd(err * (1.0 / d), xv, gv)
        return dxv, dxv, dg, jnp.sum(err * err, axis=0, keepdims=True)

    return _rowwise("loss_head", fn, [x, target], [g], [(d, F32), (d, BF16)], [g.shape, g.shape])


def _mixer_weights(i, stacks):
    by_rows = lambda a: a.reshape(N_SHARDS * a.shape[1], a.shape[2])
    by_cols = lambda a: a.transpose(1, 0, 2).reshape(a.shape[1], N_SHARDS * a.shape[2])
    if i % 2:
        w_in3, w_out3, ln_g, ln_b = stacks
        return w_in3, ln_g.reshape(1, -1), ln_b.reshape(1, -1), w_out3
    wdkv = by_rows(stacks[0])
    wdkv = jnp.pad(wdkv, ((0, 0), (0, LAT_PAD - wdkv.shape[1])))
    wq = jnp.pad(by_cols(stacks[1]).reshape(Q_RANK, HEADS, QK_HEAD), ((0, 0), (0, 0), (0, HEAD_PAD - QK_HEAD)))
    return wdkv, wq.reshape(Q_RANK, HEADS * HEAD_PAD), by_cols(stacks[2]), by_rows(stacks[3])


def _local_step(x, positions, target, norm_mix, norm_ffn, final_norm, q_norm, kv_norm, w_sp, b_sp, mixers, ffn, reducer):
    tables = _rope_tables(positions)
    gd = mixers[1][2].size // SGU_GROUPS
    bias_full = [jnp.repeat(b_sp[j].T, gd, axis=1) for j in range(DEPTH // 2)]
    saved, mla, sgu = [], [None] * (DEPTH // 2), [None] * (DEPTH // 2)
    for i in range(DEPTH):
        j = i // 2
        x, *stacks = lax.optimization_barrier((x, *mixers[i]))
        if i % 2 == 0:
            wdkv, wq, wkv, wo = mla[j] = _mixer_weights(i, stacks)
            x, s_mix = _mla_fwd(x, norm_mix[i:i + 1], wdkv, q_norm[j:j + 1], kv_norm[j:j + 1], wq, wkv, wo, tables)
        else:
            w_in3, ln_g, ln_b, w_out3 = sgu[j] = _mixer_weights(i, stacks)
            x, s_mix = _sgu_layer_fwd(x, norm_mix[i:i + 1], w_in3, ln_g, ln_b, w_sp[j], bias_full[j], w_out3)
        x, s_ffn = _ffn_fwd(x, norm_ffn[i:i + 1], *ffn[i])
        saved.append((s_mix, s_ffn))
    dx, dxb, g_final, sq_cols = _loss_head(x, target, final_norm[None, :])
    loss = 0.5 * jnp.sum(sq_cols) / x.shape[1]

    def pair(g):
        return g, g.astype(BF16)

    g_mix, g_ffn = [None] * DEPTH, [None] * DEPTH
    mla_g, sgu_g = [None] * (DEPTH // 2), [None] * (DEPTH // 2)
    for i in reversed(range(DEPTH)):
        j = i // 2
        s_mix, s_ffn = saved[i]
        dx, dxb, g_ffn[i], g_up, g_down = _ffn_bwd(dx, dxb, s_ffn, norm_ffn[i:i + 1], *ffn[i])
        dxb = reducer.add(f"ffn{i}", i, {"ffn_w_up": g_up, "ffn_w_down": g_down}, dxb)
        dxb = reducer.phase_end(dxb)
        if i % 2 == 0:
            wdkv, wq, wkv, wo = mla[j]
            dx, dxb, g_mix[i], g_wdkv, g_qn, g_kvn, g_wq, g_wkv, g_wo = _mla_bwd(
                dx, dxb, s_mix, norm_mix[i:i + 1], wdkv, q_norm[j:j + 1], kv_norm[j:j + 1], wq, wkv, wo, tables)
            mla_g[j] = (g_qn, g_kvn)
            g_wq = g_wq.reshape(Q_RANK, HEADS, HEAD_PAD)[..., :QK_HEAD].reshape(Q_RANK, N_SHARDS, -1)
            dxb = reducer.add(f"mla{j}", j, {
                "mla_w_dkv": pair(g_wdkv[:, :Q_RANK + KV_RANK + ROPE].reshape(N_SHARDS, -1, Q_RANK + KV_RANK + ROPE)),
                "mla_w_uq": pair(g_wq.transpose(1, 0, 2)),
                "mla_w_ukv": pair(g_wkv.reshape(KV_RANK, N_SHARDS, -1).transpose(1, 0, 2)),
                "mla_w_o": pair(g_wo.reshape(N_SHARDS, -1, g_wo.shape[1]))}, dxb)
        else:
            w_in3, ln_g, ln_b, w_out3 = sgu[j]
            dx, dxb, g_mix[i], g_in, g_out, g_lng, g_lnb, g_wsp, g_bias = _sgu_layer_bwd(
                dx, dxb, s_mix, norm_mix[i:i + 1], w_in3, ln_g, ln_b, w_sp[j], bias_full[j], w_out3)
            sgu_g[j] = (g_wsp, g_bias.reshape(SGU_CHUNK, SGU_GROUPS, gd).sum(axis=-1).T)
            dxb = reducer.add(f"sgu{j}", j, {"sgu_w_in": g_in, "sgu_w_out": g_out,
                                             "sgu_ln_g": pair(g_lng.reshape(N_SHARDS, -1, LANE)),
                                             "sgu_ln_b": pair(g_lnb.reshape(N_SHARDS, -1, LANE))}, dxb)
        dxb = reducer.phase_end(dxb)
    small = dict(
        norm_mix=jnp.concatenate(g_mix, axis=0), norm_ffn=jnp.concatenate(g_ffn, axis=0), final_norm=g_final[0],
        q_norm=jnp.concatenate([m[0] for m in mla_g], axis=0), kv_norm=jnp.concatenate([m[1] for m in mla_g], axis=0),
        w_sp=jnp.stack([s[0] for s in sgu_g]), b_sp=jnp.stack([s[1] for s in sgu_g]))
    return loss, dx, small, dxb


HBM_SPEC = pl.BlockSpec(memory_space=pltpu.HBM)


def _place():
    x, y, c = lax.axis_index("x"), lax.axis_index("y"), lax.axis_index("c")
    return x, y, c, [(1 - x, y), (x, 1 - y), (1 - x, 1 - y)]


def _remote(src, dst, send_sems, recv_sems, k, to):
    return pltpu.make_async_remote_copy(src_ref=src, dst_ref=dst, send_sem=send_sems.at[k], recv_sem=recv_sems.at[k],
                                        device_id=to, device_id_type=MESH)


def _gather_layer(tag, shards):
    n = len(shards)
    split = [s.shape[0] >= 16 for s in shards]

    def body(*refs):
        ins, outs = refs[:n], refs[n:2 * n]
        send_sems, recv_sems, local_sems = refs[2 * n:]
        x, y, c, chips = _place()
        mine = 2 * x + y
        barrier = pltpu.get_barrier_semaphore()
        peers = [(x, y, 1 - c)] + [(*chip, c) for chip in chips]
        for peer in peers:
            pl.semaphore_signal(barrier, inc=1, device_id=peer, device_id_type=MESH)
        pl.semaphore_wait(barrier, len(peers))

        def rows(t, half):
            hr = shards[t].shape[0] // 2
            return pl.ds(half * hr, hr) if split[t] else pl.ds(0, shards[t].shape[0])

        local, sent = [], []
        for t in range(n):
            local.append(pltpu.make_async_copy(ins[t], outs[t].at[mine], local_sems.at[t]))
            local[-1].start()
            for j, chip in enumerate(chips):
                cp = _remote(ins[t].at[rows(t, c)], outs[t].at[mine, rows(t, c)], send_sems, recv_sems, 3 * t + j, (*chip, c))
                cp.start()
                sent.append(cp)
        for j, chip in enumerate(chips):
            theirs = 2 * chip[0] + chip[1]
            for t in range(n):
                piece = outs[t].at[theirs, rows(t, c)]
                _remote(piece, piece, send_sems, recv_sems, 3 * t + j, (x, y, c)).wait_recv()
                if split[t]:
                    cp = _remote(piece, piece, send_sems, recv_sems, 3 * n + 3 * t + j, (x, y, 1 - c))
                    cp.start()
                    sent.append(cp)
        for j, chip in enumerate(chips):
            theirs = 2 * chip[0] + chip[1]
            for t in range(n):
                if split[t]:
                    piece = outs[t].at[theirs, rows(t, 1 - c)]
                    _remote(piece, piece, send_sems, recv_sems, 3 * n + 3 * t + j, (x, y, c)).wait_recv()
        for cp in sent:
            cp.wait_send()
        for cp in local:
            cp.wait()

    return pl.kernel(
        body, name=f"gather_{tag}", mesh=plsc.ScalarSubcoreMesh(axis_name="sequencer", num_cores=1),
        out_type=[jax.ShapeDtypeStruct((N_SHARDS, *s.shape), s.dtype) for s in shards],
        scratch_types=[pltpu.SemaphoreType.DMA((6 * n,)), pltpu.SemaphoreType.DMA((6 * n,)), pltpu.SemaphoreType.DMA((n,))],
        compiler_params=pltpu.CompilerParams(collective_id=ID_GATHER),
    )(*shards)


SEQUENCER = dict(axis_name="sequencer", num_cores=1)
ID_GATHER, ID_EXCHANGE, ID_SHARE = 0, 1, 2
MIN_SPLIT_ROWS = 16


def _handshake(peers):
    barrier = pltpu.get_barrier_semaphore()
    for peer in peers:
        pl.semaphore_signal(barrier, inc=1, device_id=peer, device_id_type=MESH)
    pl.semaphore_wait(barrier, len(peers))


def _half_rows(rows, half):
    return pl.ds(half * (rows // 2), rows // 2) if rows >= MIN_SPLIT_ROWS else pl.ds(0, rows)


SEM_SPEC = pl.BlockSpec(memory_space=pltpu.SEMAPHORE)
DATAFLOW = pltpu.SideEffectType.DATAFLOW_SIDE_EFFECTING


def _exchange_copies(shapes, stacks, lands, send_sems, recv_sems):
    x, y, c, chips = _place()
    mine = 2 * x + y
    copies = []
    for t, shape in enumerate(shapes):
        r = shape[1]
        copies.append(_remote(stacks[t].at[mine, _half_rows(r, 1 - c)], lands[t].at[0], send_sems, recv_sems, 7 * t, (x, y, 1 - c)))
        for j, chip in enumerate(chips):
            theirs = 2 * chip[0] + chip[1]
            copies.append(_remote(stacks[t].at[theirs, _half_rows(r, c)], lands[t].at[1 + j], send_sems, recv_sems,
                                  7 * t + 1 + j, (*chip, c)))
            copies.append(_remote(stacks[t].at[theirs, _half_rows(r, 1 - c)], lands[t].at[4 + j], send_sems, recv_sems,
                                  7 * t + 4 + j, (*chip, 1 - c)))
    return copies


def _exchange_start(tag, stacks):
    n = len(stacks)
    shapes = [s.shape for s in stacks]
    lands = [lax.empty((7, s.shape[1] // 2 if s.shape[1] >= MIN_SPLIT_ROWS else s.shape[1], s.shape[2]), s.dtype) for s in stacks]

    def body(*refs):
        send_sems, recv_sems, token = refs[2 * n], refs[2 * n + 1], refs[-1]
        for cp in _exchange_copies(shapes, refs[:n], refs[n:2 * n], send_sems, recv_sems):
            cp.start()
        token[...] = jnp.zeros_like(token)

    out = pl.pallas_call(
        body, name=f"reduce_exchange_start_{tag}",
        out_shape=(pltpu.SemaphoreType.DMA((7 * n,)), pltpu.SemaphoreType.DMA((7 * n,)),
                   *[pltpu.HBM(a.shape, a.dtype) for a in (*stacks, *lands)], jax.ShapeDtypeStruct((8, LANE), F32)),
        in_specs=[HBM_SPEC] * (2 * n),
        out_specs=(SEM_SPEC, SEM_SPEC, *[HBM_SPEC] * (2 * n), pl.BlockSpec(memory_space=pltpu.VMEM)),
        input_output_aliases={t: 2 + t for t in range(2 * n)},
        compiler_params=pltpu.CompilerParams(has_side_effects=DATAFLOW),
    )(*[pltpu.with_memory_space_constraint(a, pltpu.HBM) for a in (*stacks, *lands)])
    return out[0], out[1], out[2:2 + n], out[2 + n:2 + 2 * n], out[-1]


def _exchange_wait(tag, send_sems, recv_sems, stacks, lands, after):
    n = len(stacks)
    shapes = [s.shape for s in stacks]

    def body(*refs):
        for cp in _exchange_copies(shapes, refs[:n], refs[n:2 * n], refs[2 * n], refs[2 * n + 1]):
            cp.wait()

    out = pl.pallas_call(
        body, name=f"reduce_exchange_wait_{tag}",
        out_shape=tuple(pltpu.HBM(a.shape, a.dtype) for a in (*stacks, *lands)),
        in_specs=[HBM_SPEC] * (2 * n) + [SEM_SPEC, SEM_SPEC, pl.BlockSpec(memory_space=pl.ANY)],
        out_specs=tuple([HBM_SPEC] * (2 * n)),
        input_output_aliases={t: t for t in range(2 * n)},
        compiler_params=pltpu.CompilerParams(has_side_effects=DATAFLOW),
    )(*stacks, *lands, send_sems, recv_sems, after)
    return out[:n], out[n:]


def _share_halves(tag, halves):
    n = len(halves)

    def body(*refs):
        ins, outs, send_sems, recv_sems = refs[:n], refs[n:2 * n], refs[2 * n], refs[2 * n + 1]
        x, y, c, _ = _place()
        _handshake([(x, y, 1 - c)])
        sent = [_remote(ins[t], outs[t], send_sems, recv_sems, t, (x, y, 1 - c)) for t in range(n)]
        for cp in sent:
            cp.start()
        for cp in sent:
            cp.wait()

    return pl.kernel(
        body, name=f"reduce_share_{tag}", mesh=plsc.ScalarSubcoreMesh(**SEQUENCER),
        out_type=[jax.ShapeDtypeStruct(h.shape, h.dtype) for h in halves],
        scratch_types=[pltpu.SemaphoreType.DMA((n,)), pltpu.SemaphoreType.DMA((n,))],
        compiler_params=pltpu.CompilerParams(collective_id=ID_SHARE),
    )(*halves)


def _all_reduce_small(part):
    rows = part.shape[0]

    def body(p_ref, out_ref, sib_buf, chip_sums, send_sems, recv_sems):
        x, y, c, chips = _place()
        mine = 2 * x + y
        swap = _remote(p_ref, sib_buf, send_sems, recv_sems, 0, (x, y, 1 - c))
        swap.start()
        swap.wait()
        chip_sums[mine] = p_ref[...] + sib_buf[...]
        sent = [_remote(chip_sums.at[mine], chip_sums.at[mine], send_sems, recv_sems, 1 + j, (*chip, c))
                for j, chip in enumerate(chips)]
        for cp in sent:
            cp.start()
        for j, chip in enumerate(chips):
            sent[j].wait_send()
            theirs = chip_sums.at[2 * chip[0] + chip[1]]
            _remote(theirs, theirs, send_sems, recv_sems, 1 + j, (x, y, c)).wait_recv()
        out_ref[...] = ((chip_sums[0] + chip_sums[1]) + chip_sums[2]) + chip_sums[3]

    vmem = pl.BlockSpec(memory_space=pltpu.VMEM)
    return pl.pallas_call(
        body, name="all_reduce_small", in_specs=[vmem], out_specs=vmem, out_shape=jax.ShapeDtypeStruct(part.shape, F32),
        scratch_shapes=[pltpu.VMEM((rows, LANE), F32), pltpu.VMEM((N_SHARDS, rows, LANE), F32),
                        pltpu.SemaphoreType.DMA((4,)), pltpu.SemaphoreType.DMA((4,))],
        compiler_params=pltpu.CompilerParams(vmem_limit_bytes=VMEM_LIMIT_BYTES),
    )(part)


def _sum_partials(g3, others, sel):
    _, rows, c = others.shape
    whole = g3.shape[1] == rows
    tr = _tile(rows, 512)
    nb = rows // tr

    def body(sel_ref, g_ref, *rest):
        same = g_ref[...].astype(F32)
        for ref in rest[1:4]:
            same = same + ref[...].astype(F32)
        other = rest[0][...].astype(F32)
        for ref in rest[4:7]:
            other = other + ref[...].astype(F32)
        rest[7][...] = same + other

    blk = (None, tr, c)
    slots = [pl.BlockSpec(blk, functools.partial(lambda i, sr, k: (k, i, 0), k=k)) for k in range(7)]
    return pl.pallas_call(
        body, name="reduce_sum_partials",
        grid_spec=pltpu.PrefetchScalarGridSpec(
            num_scalar_prefetch=1, grid=(nb,),
            in_specs=[pl.BlockSpec(blk, lambda i, sr: (sr[0], (0 if whole else sr[1] * nb) + i, 0))] + slots,
            out_specs=pl.BlockSpec((tr, c), lambda i, sr: (i, 0))),
        out_shape=jax.ShapeDtypeStruct((rows, c), F32),
        compiler_params=_params(("parallel",)),
    )(sel, g3, *[others] * 7)


def _adamw_math(w, g, m, v):
    nm = ADAM_B1 * m + (1.0 - ADAM_B1) * g
    nv = ADAM_B2 * v + (1.0 - ADAM_B2) * (g * g)
    m_hat = nm / (1.0 - ADAM_B1 ** ADAM_STEP)
    v_hat = nv / (1.0 - ADAM_B2 ** ADAM_STEP)
    return -ADAM_LR * (m_hat / (jnp.sqrt(v_hat) + ADAM_EPS) + ADAM_WD * w), nm, nv


def _adamw_layer(layer, w, m, v, g_mine, g_sibling, sel, prev):
    lyr, r, c = w.shape
    rows = g_mine.shape[0]
    halves = r // rows
    tr = _tile(rows, 512)
    nb = rows // tr
    n_g = 1 if g_sibling is None else 2

    def body(sel_ref, w_ref, m_ref, v_ref, *rest):
        g = rest[0][...]
        if n_g == 2:
            g = jnp.where(pl.program_id(0) == sel_ref[1], g, rest[1][...])
        outs = rest[n_g + (0 if prev is None else 4):]
        d, nm, nv = _adamw_math(w_ref[...], g, m_ref[...], v_ref[...])
        for ref, val in zip(outs, (g, d, nm, nv)):
            ref[...] = val

    full = pl.BlockSpec((None, tr, c), lambda h, i, sr: (layer, h * nb + i, 0))
    part = pl.BlockSpec((tr, c), lambda h, i, sr: (i, 0))
    n_in = 4 + n_g
    return pl.pallas_call(
        body, name="adamw_layer",
        grid_spec=pltpu.PrefetchScalarGridSpec(
            num_scalar_prefetch=1, grid=(halves, nb),
            in_specs=[full] * 3 + [part] * n_g + ([] if prev is None else [pl.BlockSpec(memory_space=pl.ANY)] * 4),
            out_specs=[full] * 4),
        out_shape=[jax.ShapeDtypeStruct(w.shape, F32)] * 4,
        input_output_aliases={} if prev is None else {n_in + k: k for k in range(4)},
        compiler_params=_params(("parallel", "parallel")),
    )(sel, w, m, v, g_mine, *([] if g_sibling is None else [g_sibling]), *([] if prev is None else prev))


def _adamw(w, g, m, v):
    lyr, r, c = w.shape
    tr = _tile(r, 512)

    def body(w_ref, g_ref, m_ref, v_ref, d_ref, nm_ref, nv_ref):
        gv = g_ref[...]
        nm = ADAM_B1 * m_ref[...] + (1.0 - ADAM_B1) * gv
        nv = ADAM_B2 * v_ref[...] + (1.0 - ADAM_B2) * (gv * gv)
        m_hat = nm / (1.0 - ADAM_B1 ** ADAM_STEP)
        v_hat = nv / (1.0 - ADAM_B2 ** ADAM_STEP)
        d_ref[...] = -ADAM_LR * (m_hat / (jnp.sqrt(v_hat) + ADAM_EPS) + ADAM_WD * w_ref[...])
        nm_ref[...] = nm
        nv_ref[...] = nv

    blk = pl.BlockSpec((None, tr, c), lambda l, i: (l, i, 0))
    return pl.pallas_call(
        body, name="adamw", grid=(lyr, r // tr), in_specs=[blk] * 4, out_specs=[blk] * 3,
        out_shape=[jax.ShapeDtypeStruct(w.shape, F32)] * 3,
        compiler_params=_params(("parallel", "parallel")),
    )(w, g, m, v)


SHARDED = ("mla_w_dkv", "mla_w_uq", "mla_w_ukv", "mla_w_o", "sgu_w_in", "sgu_ln_g", "sgu_ln_b", "sgu_w_out",
           "ffn_w_up", "ffn_w_down")
REPLICATED = ("norm_mix", "norm_ffn", "final_norm", "mla_q_norm", "mla_kv_norm", "sgu_w_spatial", "sgu_b_spatial")
WEIGHTS = ("norm_mix", "norm_ffn", "final_norm", "mla_w_dkv", "mla_q_norm", "mla_kv_norm", "mla_w_uq", "mla_w_ukv",
           "mla_w_o", "sgu_w_in", "sgu_ln_g", "sgu_ln_b", "sgu_w_spatial", "sgu_b_spatial", "sgu_w_out", "ffn_w_up",
           "ffn_w_down")


class _Reducer:
    def __init__(self, state, sel):
        self.state, self.sel = state, sel
        self.started, self.travelling, self.summed = [], [], []
        self.done = {}

    def add(self, tag, layer, grads, token):
        names = list(grads)
        token, *tied = lax.optimization_barrier((token, *[a for n in names for a in grads[n]]))
        f32s, bf16s = tied[0::2], tied[1::2]
        *flying, started = _exchange_start(tag, bf16s)
        self.started.append((tag, layer, names, f32s, flying))
        return lax.optimization_barrier((token, started))[0]

    def phase_end(self, token):
        for tag, layer, names, f32s, flying in self.travelling:
            bf16s, received = _exchange_wait(tag, *flying, token)
            own = [g if g.shape[1] >= MIN_SPLIT_ROWS else gb for g, gb in zip(f32s, bf16s)]
            mine = [_sum_partials(g, got, self.sel) for g, got in zip(own, received)]
            cut = [k for k, g in enumerate(own) if g.shape[1] >= MIN_SPLIT_ROWS]
            theirs = dict(zip(cut, _share_halves(tag, [mine[k] for k in cut])))
            self.summed.append((layer, names, mine, [theirs.get(k) for k in range(len(names))]))
            token = lax.optimization_barrier((token, *mine))[0]
        self.travelling, self.started = self.started, []
        return token

    def update(self, token):
        for layer, names, mine, theirs in self.summed:
            for name, g_mine, g_theirs in zip(names, mine, theirs):
                w, m, v = self.state[name]
                self.done[name] = _adamw_layer(layer, w, m, v, g_mine, g_theirs, self.sel, self.done.get(name))
                token = self.done[name][1]
        self.summed = []
        return token


def _as3d(name, a):
    return a.reshape(a.shape[0], -1, LANE) if name in ("sgu_ln_g", "sgu_ln_b") else a


def _pack(parts):
    flat = jnp.concatenate([p.reshape(-1) for p in parts])
    rows = -(-flat.shape[0] // (8 * LANE)) * 8
    return jnp.pad(flat, (0, rows * LANE - flat.shape[0])).reshape(rows, LANE)


def _unpack(packed, like):
    flat, out, at = packed.reshape(-1), [], 0
    for p in like:
        out.append(flat[at:at + p.size].reshape(p.shape))
        at += p.size
    return out


def kernel(x, positions, norm_mix, norm_ffn, final_norm, mla_w_dkv, mla_q_norm, mla_kv_norm, mla_w_uq, mla_w_ukv, mla_w_o, sgu_w_in, sgu_ln_g, sgu_ln_b, sgu_w_spatial, sgu_b_spatial, sgu_w_out, ffn_w_up, ffn_w_down, loss_target, m_norm_mix, m_norm_ffn, m_final_norm, m_mla_w_dkv, m_mla_q_norm, m_mla_kv_norm, m_mla_w_uq, m_mla_w_ukv, m_mla_w_o, m_sgu_w_in, m_sgu_ln_g, m_sgu_ln_b, m_sgu_w_spatial, m_sgu_b_spatial, m_sgu_w_out, m_ffn_w_up, m_ffn_w_down, v_norm_mix, v_norm_ffn, v_final_norm, v_mla_w_dkv, v_mla_q_norm, v_mla_kv_norm, v_mla_w_uq, v_mla_w_ukv, v_mla_w_o, v_sgu_w_in, v_sgu_ln_g, v_sgu_ln_b, v_sgu_w_spatial, v_sgu_b_spatial, v_sgu_w_out, v_ffn_w_up, v_ffn_w_down):
    given = dict(locals())
    w = {n: given[n] for n in WEIGHTS}
    mom = {n: given["m_" + n] for n in WEIGHTS}
    var = {n: given["v_" + n] for n in WEIGHTS}
    mixers, ffn, token = [], [], None
    for i in range(DEPTH):
        j = i // 2
        if i % 2 == 0:
            mixer = [w[n][j].astype(BF16) for n in ("mla_w_dkv", "mla_w_uq", "mla_w_ukv", "mla_w_o")]
        else:
            mixer = [sgu_w_in[j].astype(BF16), sgu_w_out[j].astype(BF16), sgu_ln_g[j].reshape(-1, LANE),
                     sgu_ln_b[j].reshape(-1, LANE)]
        for tag, shards, into in ((f"mixer{i}", mixer, mixers), (f"ffn{i}", [ffn_w_up[i].astype(BF16), ffn_w_down[i].astype(BF16)], ffn)):
            if token is not None:
                token, *shards = lax.optimization_barrier((token, *shards))
            token = shards[0]
            into.append(_gather_layer(tag, shards))

    x_i, y_i, c_i = lax.axis_index("x"), lax.axis_index("y"), lax.axis_index("c")
    sel = jnp.stack([2 * x_i + y_i, c_i]).astype(jnp.int32)
    reducer = _Reducer({n: tuple(_as3d(n, d[n]) for d in (w, mom, var)) for n in SHARDED}, sel)
    loss, dx, small, token = _local_step(
        x[0], positions[0], loss_target[0], norm_mix, norm_ffn, final_norm, mla_q_norm, mla_kv_norm, sgu_w_spatial,
        sgu_b_spatial, mixers, ffn, reducer)
    loss = lax.psum(loss, ("x", "y", "c"))

    small_g = [small["norm_mix"], small["norm_ffn"], small["final_norm"], small["q_norm"], small["kv_norm"],
               small["w_sp"], small["b_sp"]]
    like = [w[n] for n in REPLICATED]
    g_small = _all_reduce_small(lax.optimization_barrier((_pack(small_g), token))[0])
    packed = [_pack([d[n] for n in REPLICATED])[None] for d in (w, mom, var)]
    upd_small = _adamw(packed[0], g_small[None], packed[1], packed[2])
    grads = dict(zip(REPLICATED, _unpack(g_small, like)))
    delta, new_m, new_v = ({n: a for n, a in zip(REPLICATED, _unpack(u[0], like))} for u in upd_small)

    token = reducer.phase_end(reducer.update(upd_small[0]))
    reducer.update(None)
    loss = lax.optimization_barrier((loss, token))[0]
    for n in SHARDED:
        grads[n], delta[n], new_m[n], new_v[n] = (a.reshape(w[n].shape) for a in reducer.done[n])

    return (loss, dx[None], *[grads[n] for n in WEIGHTS], *[delta[n] for n in WEIGHTS],
            *[new_m[n] for n in WEIGHTS], *[new_v[n] for n in WEIGHTS])
```

```python
import functools
import math

import jax
import jax.numpy as jnp
from jax import lax
from jax.experimental import pallas as pl
from jax.experimental.pallas import tpu as pltpu
from jax.experimental.pallas import tpu_sc as plsc

F32 = jnp.float32
BF16 = jnp.bfloat16
MESH = pl.DeviceIdType.MESH

DEPTH = 4
HEADS = 8
NOPE = 128
ROPE = 64
VHEAD = 128
QK_HEAD = NOPE + ROPE
Q_RANK = 256
KV_RANK = 128
HEAD_PAD = 256
LAT_PAD = 512
ROPE_THETA = 10000.0
SGU_CHUNK = 128
SGU_GROUPS = 8
NORM_EPS = 1e-6
LN_EPS = 1e-5
ADAM_LR, ADAM_B1, ADAM_B2, ADAM_EPS, ADAM_WD, ADAM_STEP = 0.001, 0.9, 0.999, 1e-08, 0.01, 10

N_SHARDS = 4
LANE = 128
VMEM_LIMIT_BYTES = 56 * 1024 * 1024
ATT_TILE = 512
MM_TILE = 1024
ATT_SCALE = QK_HEAD ** -0.5
LOG2_SCALE = ATT_SCALE * math.log2(math.e)

NN = (((1,), (0,)), ((), ()))
NT = (((1,), (1,)), ((), ()))
TN = (((0,), (0,)), ((), ()))


def _params(sem):
    return pltpu.CompilerParams(dimension_semantics=sem, vmem_limit_bytes=VMEM_LIMIT_BYTES)


def _tile(n, pref):
    t = min(n, pref)
    while n % t:
        t //= 2
    return t


def _matmul(name, a, b, a_spec, b_spec, dims, grid, tile, outs, extras=(), epilogue=None, aliased=()):
    nk, ne, no = grid[2], len(extras), len(outs)

    def body(a_ref, b_ref, *rest):
        e_refs, o_refs = rest[:ne], rest[ne + len(aliased):ne + len(aliased) + no]
        part = lax.dot_general(a_ref[...].astype(BF16), b_ref[...].astype(BF16), dims, preferred_element_type=F32)

        def finish(acc):
            vals = (acc,) if epilogue is None else epilogue(acc, *[e[...] for e in e_refs])
            for o_ref, v in zip(o_refs, vals):
                o_ref[...] = v.astype(o_ref.dtype)

        if nk == 1:
            finish(part)
            return
        acc_ref, k = rest[-1], pl.program_id(2)

        @pl.when(k == 0)
        def _():
            acc_ref[...] = part

        @pl.when(jnp.logical_and(k > 0, k < nk - 1))
        def _():
            acc_ref[...] += part

        @pl.when(k == nk - 1)
        def _():
            finish(acc_ref[...] + part)

    n_in = 2 + ne
    return pl.pallas_call(
        body, name=name, grid=grid,
        in_specs=[a_spec, b_spec] + [s for _, s in extras] + [pl.BlockSpec(memory_space=pl.ANY)] * len(aliased),
        out_specs=[s for _, s in outs], out_shape=[s for s, _ in outs],
        scratch_shapes=[pltpu.VMEM(tile, F32)] if nk > 1 else [],
        input_output_aliases={n_in + i: o for i, (_, o) in enumerate(aliased)},
        compiler_params=_params(("parallel", "parallel", "arbitrary")),
    )(a, b, *[e for e, _ in extras], *[arr for arr, _ in aliased])


def _mm(name, a, b, out_dtypes=(F32,), epilogue=None, extras=(), tm=MM_TILE, tn=MM_TILE, tk=MM_TILE, nt=False):
    m, kd = a.shape
    n = b.shape[0] if nt else b.shape[1]
    tm, tn, tk = _tile(m, tm), _tile(n, tn), _tile(kd, tk)
    o_spec = pl.BlockSpec((tm, tn), lambda i, j, k: (i, j))
    b_spec = pl.BlockSpec((tn, tk), lambda i, j, k: (j, k)) if nt else pl.BlockSpec((tk, tn), lambda i, j, k: (k, j))
    return _matmul(name, a, b, pl.BlockSpec((tm, tk), lambda i, j, k: (i, k)), b_spec, NT if nt else NN,
                   (m // tm, n // tn, kd // tk), (tm, tn),
                   [(jax.ShapeDtypeStruct((m, n), d), o_spec) for d in out_dtypes],
                   [(e, o_spec) for e in extras], epilogue)


def _mm_tn(name, a, b, out_dtypes=(F32,), tm=MM_TILE, tn=MM_TILE, tk=MM_TILE):
    s, m = a.shape
    n = b.shape[1]
    tm, tn, tk = _tile(m, tm), _tile(n, tn), _tile(s, tk)
    o_spec = pl.BlockSpec((tm, tn), lambda i, j, k: (i, j))
    return _matmul(name, a, b, pl.BlockSpec((tk, tm), lambda i, j, k: (k, i)),
                   pl.BlockSpec((tk, tn), lambda i, j, k: (k, j)), TN, (m // tm, n // tn, s // tk), (tm, tn),
                   [(jax.ShapeDtypeStruct((m, n), d), o_spec) for d in out_dtypes])


def _mm_stacked(name, a, w3, mode, out_dtypes=(F32,), epilogue=None, extras=(), tm=MM_TILE, tn=MM_TILE, tk=MM_TILE):
    m, kd = a.shape
    _, r, c = w3.shape
    n = c if mode == "row" else N_SHARDS * c
    tm = _tile(m, tm)
    if mode == "row":
        tn, tk = _tile(n, tn), _tile(r, tk)
        per = r // tk
        b_spec = pl.BlockSpec((None, tk, tn), lambda i, j, k: (k // per, k % per, j))
    else:
        tn, tk = _tile(c, tn), _tile(kd, tk)
        per = c // tn
        b_spec = pl.BlockSpec((None, tk, tn), lambda i, j, k: (j // per, k, j % per))
    o_spec = pl.BlockSpec((tm, tn), lambda i, j, k: (i, j))
    return _matmul(name, a, w3, pl.BlockSpec((tm, tk), lambda i, j, k: (i, k)), b_spec, NN,
                   (m // tm, n // tn, kd // tk), (tm, tn),
                   [(jax.ShapeDtypeStruct((m, n), d), o_spec) for d in out_dtypes],
                   [(e, o_spec) for e in extras], epilogue)


def _mm_stacked_nt(name, a, w3, mode, out_dtypes=(F32,), epilogue=None, extras=(), tm=MM_TILE, tn=MM_TILE, tk=MM_TILE):
    m, nd = a.shape
    _, r, c = w3.shape
    kout = N_SHARDS * r if mode == "row" else r
    tm = _tile(m, tm)
    if mode == "row":
        tn, tk = _tile(r, tn), _tile(c, tk)
        per = r // tn
        b_spec = pl.BlockSpec((None, tn, tk), lambda i, j, k: (j // per, j % per, k))
    else:
        tn, tk = _tile(r, tn), _tile(c, tk)
        per = c // tk
        b_spec = pl.BlockSpec((None, tn, tk), lambda i, j, k: (k // per, j, k % per))
    o_spec = pl.BlockSpec((tm, tn), lambda i, j, k: (i, j))
    return _matmul(name, a, w3, pl.BlockSpec((tm, tk), lambda i, j, k: (i, k)), b_spec, NT,
                   (m // tm, kout // tn, nd // tk), (tm, tn),
                   [(jax.ShapeDtypeStruct((m, kout), d), o_spec) for d in out_dtypes],
                   [(e, o_spec) for e in extras], epilogue)


def _mm_tn_stacked(name, a, b, shape3, mode, tm=MM_TILE, tn=MM_TILE, tk=MM_TILE):
    s, m = a.shape
    n = b.shape[1]
    _, r, c = shape3
    tk = _tile(s, tk)
    if mode == "row":
        tm, tn = _tile(r, tm), _tile(n, tn)
        per = r // tm
        o_spec = pl.BlockSpec((None, tm, tn), lambda i, j, k: (i // per, i % per, j))
    else:
        tm, tn = _tile(m, tm), _tile(c, tn)
        per = c // tn
        o_spec = pl.BlockSpec((None, tm, tn), lambda i, j, k: (j // per, i, j % per))
    outs = [(jax.ShapeDtypeStruct(shape3, F32), o_spec), (jax.ShapeDtypeStruct(shape3, BF16), o_spec)]
    return _matmul(name, a, b, pl.BlockSpec((tk, tm), lambda i, j, k: (k, i)),
                   pl.BlockSpec((tk, tn), lambda i, j, k: (k, j)), TN, (m // tm, n // tn, s // tk), (tm, tn),
                   outs, epilogue=lambda acc: (acc, acc))


def _rowwise(name, fn, rows, consts, out_rows, out_accs=(), tr=256):
    nr, nc, no = len(rows), len(consts), len(out_rows)
    n_rows = rows[0].shape[0]
    tr = _tile(n_rows, tr)

    def body(*refs):
        vals = fn(*[r[...] for r in refs[:nr + nc]])
        o_refs, a_refs = refs[nr + nc:nr + nc + no], refs[nr + nc + no:]
        for ref, v in zip(o_refs, vals[:no]):
            ref[...] = v.astype(ref.dtype)
        first = pl.program_id(0) == 0

        @pl.when(first)
        def _():
            for ref, v in zip(a_refs, vals[no:]):
                ref[...] = v

        @pl.when(jnp.logical_not(first))
        def _():
            for ref, v in zip(a_refs, vals[no:]):
                ref[...] += v

    def whole(shape):
        return pl.BlockSpec(shape, lambda i: (0,) * len(shape))

    return pl.pallas_call(
        body, name=name, grid=(n_rows // tr,),
        in_specs=[pl.BlockSpec((tr, a.shape[1]), lambda i: (i, 0)) for a in rows] + [whole(c.shape) for c in consts],
        out_specs=[pl.BlockSpec((tr, f), lambda i: (i, 0)) for f, _ in out_rows] + [whole(s) for s in out_accs],
        out_shape=[jax.ShapeDtypeStruct((n_rows, f), d) for f, d in out_rows]
        + [jax.ShapeDtypeStruct(s, F32) for s in out_accs],
        compiler_params=_params(("arbitrary",)),
    )(*rows, *consts)


def _rms_fwd(x, g):
    return x * lax.rsqrt(jnp.mean(x * x, axis=-1, keepdims=True) + NORM_EPS) * g


def _rms_bwd(dy, x, g):
    rstd = lax.rsqrt(jnp.mean(x * x, axis=-1, keepdims=True) + NORM_EPS)
    n = x * rstd
    dn = dy * g
    dx = rstd * (dn - n * jnp.mean(dn * n, axis=-1, keepdims=True))
    return dx, jnp.sum(dy * n, axis=0, keepdims=True)


def _rope(x, cs, s1, s2):
    return x * cs + pltpu.roll(x, 32, 1) * s1 + pltpu.roll(x, 96, 1) * s2


def _rope_t(dy, cs, s1, s2):
    return dy * cs + pltpu.roll(dy * s1, 96, 1) + pltpu.roll(dy * s2, 32, 1)


def _gelu(z):
    return 0.5 * z * (1.0 + lax.erf(z * (1.0 / math.sqrt(2.0))))


def _gelu_grad(z):
    return 0.5 * (1.0 + lax.erf(z * (1.0 / math.sqrt(2.0)))) + z * jnp.exp(-0.5 * z * z) * (1.0 / math.sqrt(2.0 * math.pi))


def _att_scores(q, kv, kr, scale, masked, transposed):
    k = jnp.concatenate([kv[:, :NOPE], kr], axis=1)
    if transposed:
        s = lax.dot_general(k, q, NT, preferred_element_type=F32) * scale
    else:
        s = lax.dot_general(q, k, NT, preferred_element_type=F32) * scale
    if masked:
        r = lax.broadcasted_iota(jnp.int32, s.shape, 0)
        c = lax.broadcasted_iota(jnp.int32, s.shape, 1)
        s = jnp.where((r <= c) if transposed else (c <= r), s, -jnp.inf)
    return s, k


def _flash_fwd(qb, kvb, krb):
    s_len = qb.shape[0]
    t = ATT_TILE

    def body(q_ref, kv_ref, kr_ref, o_ref, lse_ref, m_s, l_s, acc_s):
        qi = pl.program_id(1)
        m_s[...] = jnp.full_like(m_s, -jnp.inf)
        l_s[...] = jnp.zeros_like(l_s)
        acc_s[...] = jnp.zeros_like(acc_s)
        q = q_ref[...]

        def step(ki, masked):
            rows = pl.ds(pl.multiple_of(ki * t, t), t)
            kv = kv_ref[rows, :]
            s, _ = _att_scores(q, kv, kr_ref[rows, :], LOG2_SCALE, masked, False)
            m_prev = m_s[...]
            m_new = jnp.maximum(m_prev, jnp.max(s, axis=1, keepdims=True))
            alpha = jnp.exp2(m_prev - m_new)
            p = jnp.exp2(s - jnp.tile(m_new, (1, t // LANE)))
            l_s[...] = alpha * l_s[...] + jnp.sum(p, axis=1, keepdims=True)
            acc_s[...] = alpha * acc_s[...] + jnp.dot(p.astype(BF16), kv[:, NOPE:], preferred_element_type=F32)
            m_s[...] = m_new

        def unmasked(ki, carry):
            step(ki, False)
            return carry

        lax.fori_loop(0, qi, unmasked, 0)
        step(qi, True)
        o_ref[...] = (acc_s[...] / l_s[...]).astype(o_ref.dtype)
        lse_ref[...] = (m_s[...] + jnp.log2(l_s[...]))[:, :1]

    return pl.pallas_call(
        body, name="flash_fwd", grid=(HEADS, s_len // t),
        in_specs=[pl.BlockSpec((t, HEAD_PAD), lambda h, qi: (qi, h)),
                  pl.BlockSpec((s_len, HEAD_PAD), lambda h, qi: (0, h)),
                  pl.BlockSpec((s_len, LANE), lambda h, qi: (0, 0))],
        out_specs=[pl.BlockSpec((t, VHEAD), lambda h, qi: (qi, h)),
                   pl.BlockSpec((None, t, 1), lambda h, qi: (h, qi, 0))],
        out_shape=[jax.ShapeDtypeStruct((s_len, HEADS * VHEAD), BF16),
                   jax.ShapeDtypeStruct((HEADS, s_len, 1), F32)],
        scratch_shapes=[pltpu.VMEM((t, LANE), F32), pltpu.VMEM((t, LANE), F32), pltpu.VMEM((t, VHEAD), F32)],
        compiler_params=_params(("parallel", "arbitrary")),
    )(qb, kvb, krb)


def _flash_bwd_dq(qb, kvb, krb, dob, lse, delta):
    s_len = qb.shape[0]
    t = ATT_TILE
    nq = s_len // t
    scale = QK_HEAD ** -0.5

    def body(q_ref, kv_ref, kr_ref, do_ref, lse_ref, dl_ref, dq_ref, acc_s):
        qi = pl.program_id(1)
        acc_s[...] = jnp.zeros_like(acc_s)
        q, do = q_ref[...], do_ref[...]
        lse = jnp.broadcast_to(lse_ref[...], (t, LANE))
        dl = jnp.broadcast_to(dl_ref[...], (t, LANE))

        def step(ki, masked):
            rows = pl.ds(pl.multiple_of(ki * t, t), t)
            kv = kv_ref[rows, :]
            s, k = _att_scores(q, kv, kr_ref[rows, :], LOG2_SCALE, masked, False)
            p = jnp.exp2(s - jnp.tile(lse, (1, t // LANE)))
            dp = lax.dot_general(do, kv[:, NOPE:], NT, preferred_element_type=F32)
            ds = (p * (dp - jnp.tile(dl, (1, t // LANE))) * scale).astype(BF16)
            acc_s[...] += jnp.dot(ds, k, preferred_element_type=F32)

        def unmasked(ki, carry):
            step(ki, False)
            return carry

        lax.fori_loop(0, qi, unmasked, 0)
        step(qi, True)
        dq_ref[...] = acc_s[...]

    col = pl.BlockSpec((None, t, 1), lambda h, qi: (h, qi, 0))
    return pl.pallas_call(
        body, name="flash_bwd_dq", grid=(HEADS, nq),
        in_specs=[pl.BlockSpec((t, HEAD_PAD), lambda h, qi: (qi, h)),
                  pl.BlockSpec((s_len, HEAD_PAD), lambda h, qi: (0, h)),
                  pl.BlockSpec((s_len, LANE), lambda h, qi: (0, 0)),
                  pl.BlockSpec((t, VHEAD), lambda h, qi: (qi, h)), col, col],
        out_specs=pl.BlockSpec((t, HEAD_PAD), lambda h, qi: (qi, h)),
        out_shape=jax.ShapeDtypeStruct((s_len, HEADS * HEAD_PAD), F32),
        scratch_shapes=[pltpu.VMEM((t, HEAD_PAD), F32)],
        compiler_params=_params(("parallel", "arbitrary")),
    )(qb, kvb, krb, dob, lse, delta)


def _flash_bwd_dkv(qb, kvb, krb, dob, lse_row, delta_row):
    s_len = qb.shape[0]
    t = ATT_TILE
    nq = s_len // t
    scale = QK_HEAD ** -0.5

    def body(q_ref, kv_ref, kr_ref, do_ref, lse_ref, dl_ref, dkv_ref, dkr_ref, dk_s, dv_s):
        ki = pl.program_id(1)
        dk_s[...] = jnp.zeros_like(dk_s)
        dv_s[...] = jnp.zeros_like(dv_s)
        kv, kr = kv_ref[...], kr_ref[...]

        def step(qi, masked):
            rows = pl.ds(pl.multiple_of(qi * t, t), t)
            q, do = q_ref[rows, :], do_ref[rows, :]
            st, _ = _att_scores(q, kv, kr, LOG2_SCALE, masked, True)
            pt = jnp.exp2(st - lse_ref[:, rows])
            dv_s[...] += jnp.dot(pt.astype(BF16), do, preferred_element_type=F32)
            dpt = lax.dot_general(kv[:, NOPE:], do, NT, preferred_element_type=F32)
            dst = (pt * (dpt - dl_ref[:, rows]) * scale).astype(BF16)
            dk_s[...] += jnp.dot(dst, q, preferred_element_type=F32)

        def unmasked(qi, carry):
            step(qi, False)
            return carry

        step(ki, True)
        lax.fori_loop(ki + 1, nq, unmasked, 0)
        dk = dk_s[...]
        dkv_ref[...] = jnp.concatenate([dk[:, :NOPE], dv_s[...]], axis=1).astype(dkv_ref.dtype)
        dkr_ref[...] = dk[:, NOPE:]

    row = pl.BlockSpec((None, 1, s_len), lambda h, ki: (h, 0, 0))
    return pl.pallas_call(
        body, name="flash_bwd_dkv", grid=(HEADS, nq),
        in_specs=[pl.BlockSpec((s_len, HEAD_PAD), lambda h, ki: (0, h)),
                  pl.BlockSpec((t, HEAD_PAD), lambda h, ki: (ki, h)),
                  pl.BlockSpec((t, LANE), lambda h, ki: (ki, 0)),
                  pl.BlockSpec((s_len, VHEAD), lambda h, ki: (0, h)), row, row],
        out_specs=[pl.BlockSpec((t, HEAD_PAD), lambda h, ki: (ki, h)),
                   pl.BlockSpec((t, LANE), lambda h, ki: (ki, h))],
        out_shape=[jax.ShapeDtypeStruct((s_len, HEADS * HEAD_PAD), BF16),
                   jax.ShapeDtypeStruct((s_len, HEADS * LANE), F32)],
        scratch_shapes=[pltpu.VMEM((t, HEAD_PAD), F32), pltpu.VMEM((t, VHEAD), F32)],
        compiler_params=_params(("parallel", "parallel")),
    )(qb, kvb, krb, dob, lse_row, delta_row)


def _att_delta(do, ob):
    s_len = do.shape[0]
    t = ATT_TILE

    def body(do_ref, o_ref, dob_ref, dl_ref):
        d = do_ref[...]
        dob_ref[...] = d.astype(dob_ref.dtype)
        dl_ref[...] = jnp.sum(d * o_ref[...].astype(F32), axis=1, keepdims=True)

    blk = pl.BlockSpec((t, VHEAD), lambda i, h: (i, h))
    return pl.pallas_call(
        body, name="att_delta", grid=(s_len // t, HEADS), in_specs=[blk, blk],
        out_specs=[blk, pl.BlockSpec((None, t, 1), lambda i, h: (h, i, 0))],
        out_shape=[jax.ShapeDtypeStruct(do.shape, BF16), jax.ShapeDtypeStruct((HEADS, s_len, 1), F32)],
        compiler_params=_params(("parallel", "parallel")),
    )(do, ob)


def _tril(w):
    r = lax.broadcasted_iota(jnp.int32, w.shape, 0)
    c = lax.broadcasted_iota(jnp.int32, w.shape, 1)
    return jnp.where(c <= r, w, 0.0)


def _sgu_mix(w_ref, vln, gd):
    wcs = [_tril(w_ref[g]).astype(BF16) for g in range(SGU_GROUPS)]
    mixed = jnp.concatenate(
        [jnp.dot(wcs[g], vln[:, g * gd:(g + 1) * gd], preferred_element_type=F32) for g in range(SGU_GROUPS)], axis=1)
    return wcs, mixed


def _sgu_fwd(zpre, ln_g, ln_b, w_sp, bias_full):
    s_len, two_w = zpre.shape
    width = two_w // 2
    gd = width // SGU_GROUPS
    t = SGU_CHUNK

    def body(z_ref, g_ref, b_ref, w_ref, bias_ref, uv_ref):
        u = _gelu(z_ref[:, :width])
        v = _gelu(z_ref[:, width:])
        d = v - jnp.mean(v, axis=-1, keepdims=True)
        vhat = d * lax.rsqrt(jnp.mean(d * d, axis=-1, keepdims=True) + LN_EPS)
        vln = (vhat * g_ref[...] + b_ref[...]).astype(BF16)
        _, mixed = _sgu_mix(w_ref, vln, gd)
        uv_ref[...] = (u * (mixed + bias_ref[...])).astype(uv_ref.dtype)

    return pl.pallas_call(
        body, name="sgu_fwd", grid=(s_len // t,),
        in_specs=[pl.BlockSpec((t, two_w), lambda i: (i, 0)), pl.BlockSpec((1, width), lambda i: (0, 0)),
                  pl.BlockSpec((1, width), lambda i: (0, 0)), pl.BlockSpec(w_sp.shape, lambda i: (0, 0, 0)),
                  pl.BlockSpec((t, width), lambda i: (0, 0))],
        out_specs=pl.BlockSpec((t, width), lambda i: (i, 0)),
        out_shape=jax.ShapeDtypeStruct((s_len, width), BF16),
        compiler_params=_params(("parallel",)),
    )(zpre, ln_g, ln_b, w_sp, bias_full)


def _sgu_bwd(zpre, duv, ln_g, ln_b, w_sp, bias_full):
    s_len, two_w = zpre.shape
    width = two_w // 2
    gd = width // SGU_GROUPS
    t = SGU_CHUNK

    def body(z_ref, duv_ref, g_ref, b_ref, w_ref, bias_ref, dz_ref, dg_ref, db_ref, dw_ref, dbias_ref):
        first = pl.program_id(0) == 0

        def accumulate(ref, val):
            @pl.when(first)
            def _():
                ref[...] = val

            @pl.when(jnp.logical_not(first))
            def _():
                ref[...] += val

        zu, zv = z_ref[:, :width], z_ref[:, width:]
        u = _gelu(zu)
        v = _gelu(zv)
        d = v - jnp.mean(v, axis=-1, keepdims=True)
        rstd = lax.rsqrt(jnp.mean(d * d, axis=-1, keepdims=True) + LN_EPS)
        vhat = d * rstd
        vln = (vhat * g_ref[...] + b_ref[...]).astype(BF16)
        wcs, mixed = _sgu_mix(w_ref, vln, gd)
        duv_v = duv_ref[...]
        du = duv_v * (mixed + bias_ref[...])
        dmixed = duv_v * u
        dmb = dmixed.astype(BF16)
        dvln = jnp.concatenate(
            [lax.dot_general(wcs[g], dmb[:, g * gd:(g + 1) * gd], TN, preferred_element_type=F32)
             for g in range(SGU_GROUPS)], axis=1)
        for g in range(SGU_GROUPS):
            dw = lax.dot_general(dmb[:, g * gd:(g + 1) * gd], vln[:, g * gd:(g + 1) * gd], NT, preferred_element_type=F32)
            accumulate(dw_ref.at[g], _tril(dw))
        dvhat = dvln * g_ref[...]
        dv0 = rstd * (dvhat - jnp.mean(dvhat, axis=-1, keepdims=True)
                      - vhat * jnp.mean(dvhat * vhat, axis=-1, keepdims=True))
        dz_ref[:, :width] = (du * _gelu_grad(zu)).astype(dz_ref.dtype)
        dz_ref[:, width:] = (dv0 * _gelu_grad(zv)).astype(dz_ref.dtype)
        accumulate(dg_ref, jnp.sum(dvln * vhat, axis=0, keepdims=True))
        accumulate(db_ref, jnp.sum(dvln, axis=0, keepdims=True))
        accumulate(dbias_ref, dmixed)

    vec = pl.BlockSpec((1, width), lambda i: (0, 0))
    return pl.pallas_call(
        body, name="sgu_bwd", grid=(s_len // t,),
        in_specs=[pl.BlockSpec((t, two_w), lambda i: (i, 0)), pl.BlockSpec((t, width), lambda i: (i, 0)), vec, vec,
                  pl.BlockSpec(w_sp.shape, lambda i: (0, 0, 0)), pl.BlockSpec((t, width), lambda i: (0, 0))],
        out_specs=[pl.BlockSpec((t, two_w), lambda i: (i, 0)), vec, vec,
                   pl.BlockSpec(w_sp.shape, lambda i: (0, 0, 0)), pl.BlockSpec((t, width), lambda i: (0, 0))],
        out_shape=[jax.ShapeDtypeStruct((s_len, two_w), BF16), jax.ShapeDtypeStruct((1, width), F32),
                   jax.ShapeDtypeStruct((1, width), F32), jax.ShapeDtypeStruct(w_sp.shape, F32),
                   jax.ShapeDtypeStruct((t, width), F32)],
        compiler_params=_params(("arbitrary",)),
    )(zpre, duv, ln_g, ln_b, w_sp, bias_full)


def _rope_tables(positions):
    inv_freq = ROPE_THETA ** (-jnp.arange(0, ROPE, 2, dtype=F32) / ROPE)
    ang = positions.astype(F32)[:, None] * inv_freq
    cos, sin = jnp.cos(ang), jnp.sin(ang)
    z32, z64 = jnp.zeros_like(cos), jnp.zeros((cos.shape[0], LANE - ROPE), F32)
    return (jnp.concatenate([cos, cos, z64], axis=1), jnp.concatenate([z32, sin, z64], axis=1),
            jnp.concatenate([-sin, z32, z64], axis=1))


def _q_rope(name, q, tables, transpose):
    rot = _rope_t if transpose else _rope

    def fn(qv, cs, s1, s2):
        parts = []
        for h in range(HEADS):
            parts.append(qv[:, h * HEAD_PAD:h * HEAD_PAD + NOPE])
            parts.append(rot(qv[:, h * HEAD_PAD + NOPE:(h + 1) * HEAD_PAD], cs, s1, s2))
        return (jnp.concatenate(parts, axis=1),)

    return _rowwise(name, fn, [q, *tables], [], [(q.shape[1], BF16)])[0]


def _ffn_fwd(x, g, w_up3, w_down3):
    h2 = _rowwise("ffn_norm", lambda xv, gv: (_rms_fwd(xv, gv),), [x], [g], [(x.shape[1], BF16)])[0]

    def sq_relu(acc):
        r = jnp.maximum(acc, 0.0)
        return (r * r,)

    r = _mm_stacked("ffn_up", h2, w_up3, "col", (BF16,), sq_relu)[0]
    x_out = _mm_stacked("ffn_down", r, w_down3, "row", (F32,), lambda acc, res: (acc + res,), [x])[0]
    return x_out, (x, h2, r)


def _ffn_bwd(dx, dxb, saved, g, w_up3, w_down3):
    x, h2, r = saved
    da = _mm_stacked_nt("ffn_down_dx", dxb, w_down3, "row", (BF16,),
                        lambda acc, rv: (acc * (2.0 * jnp.sqrt(rv.astype(F32))),), [r])[0]
    g_down = _mm_tn_stacked("ffn_down_dw", r, dxb, w_down3.shape, "row")
    dh2 = _mm_stacked_nt("ffn_up_dx", da, w_up3, "col", (F32,))[0]
    g_up = _mm_tn_stacked("ffn_up_dw", h2, da, w_up3.shape, "col")
    dx, dxb, dg = _norm_bwd("ffn_norm_bwd", dh2, x, g, dx)
    return dx, dxb, dg, g_up, g_down


def _norm_bwd(name, dh, x, g, dres):
    def fn(dhv, xv, rv, gv):
        dxv, dg = _rms_bwd(dhv, xv, gv)
        return dxv + rv, dxv + rv, dg

    return _rowwise(name, fn, [dh, x, dres], [g], [(x.shape[1], F32), (x.shape[1], BF16)], [g.shape])


def _mla_fwd(x, g, wdkv, q_norm, kv_norm, wq, wkv, wo, tables):
    d = x.shape[1]
    h = _rowwise("mla_norm", lambda xv, gv: (_rms_fwd(xv, gv),), [x], [g], [(d, BF16)])[0]
    lat = _mm("mla_dkv", h, wdkv)[0]

    def lat_post(lv, cs, s1, s2, qg, kg):
        return (_rms_fwd(lv[:, :Q_RANK], qg), _rms_fwd(lv[:, Q_RANK:Q_RANK + KV_RANK], kg),
                _rope(lv[:, Q_RANK + KV_RANK:], cs, s1, s2))

    cqn, ckvn, krb = _rowwise("mla_lat", lat_post, [lat, *tables], [q_norm, kv_norm],
                              [(Q_RANK, BF16), (KV_RANK, BF16), (LANE, BF16)])
    q = _mm("mla_uq", cqn, wq)[0]
    kvb = _mm("mla_ukv", ckvn, wkv, (BF16,))[0]
    qb = _q_rope("mla_q_rope", q, tables, False)
    ob, lse = _flash_fwd(qb, kvb, krb)
    x_mid = _mm("mla_o", ob, wo, (F32,), lambda acc, res: (acc + res,), [x])[0]
    return x_mid, (x, h, lat, cqn, ckvn, krb, qb, kvb, ob, lse)


def _mla_bwd(dx, dxb, saved, g, wdkv, q_norm, kv_norm, wq, wkv, wo, tables):
    x, h, lat, cqn, ckvn, krb, qb, kvb, ob, lse = saved
    s_len = x.shape[0]
    do = _mm("mla_o_dx", dxb, wo, nt=True)[0]
    g_wo = _mm_tn("mla_o_dw", ob, dxb)[0]
    dob, delta = _att_delta(do, ob)
    dq = _flash_bwd_dq(qb, kvb, krb, dob, lse, delta)
    dkvb, dkr = _flash_bwd_dkv(qb, kvb, krb, dob, lse.reshape(HEADS, 1, s_len), delta.reshape(HEADS, 1, s_len))
    dqb = _q_rope("mla_q_rope_bwd", dq, tables, True)
    dcqn = _mm("mla_uq_dx", dqb, wq, nt=True)[0]
    g_wq = _mm_tn("mla_uq_dw", cqn, dqb)[0]
    dckvn = _mm("mla_ukv_dx", dkvb, wkv, nt=True)[0]
    g_wkv = _mm_tn("mla_ukv_dw", ckvn, dkvb)[0]

    def lat_bwd(dq_v, dkv_v, dkr_v, lv, cs, s1, s2, qg, kg):
        dcq, dqg = _rms_bwd(dq_v, lv[:, :Q_RANK], qg)
        dckv, dkg = _rms_bwd(dkv_v, lv[:, Q_RANK:Q_RANK + KV_RANK], kg)
        dkr_sum = dkr_v[:, :LANE]
        for hd in range(1, HEADS):
            dkr_sum = dkr_sum + dkr_v[:, hd * LANE:(hd + 1) * LANE]
        return jnp.concatenate([dcq, dckv, _rope_t(dkr_sum, cs, s1, s2)], axis=1), dqg, dkg

    dlat, g_qn, g_kvn = _rowwise("mla_lat_bwd", lat_bwd, [dcqn, dckvn, dkr, lat, *tables], [q_norm, kv_norm],
                                 [(LAT_PAD, BF16)], [q_norm.shape, kv_norm.shape])
    dh = _mm("mla_dkv_dx", dlat, wdkv, nt=True)[0]
    g_wdkv = _mm_tn("mla_dkv_dw", h, dlat)[0]
    dx, dxb, dg = _norm_bwd("mla_norm_bwd", dh, x, g, dx)
    return dx, dxb, dg, g_wdkv, g_qn, g_kvn, g_wq, g_wkv, g_wo


def _sgu_layer_fwd(x, g, w_in3, ln_g, ln_b, w_sp, bias_full, w_out3):
    h = _rowwise("sgu_norm", lambda xv, gv: (_rms_fwd(xv, gv),), [x], [g], [(x.shape[1], BF16)])[0]
    zpre = _mm_stacked("sgu_in", h, w_in3, "col")[0]
    uv = _sgu_fwd(zpre, ln_g, ln_b, w_sp, bias_full)
    x_mid = _mm_stacked("sgu_out", uv, w_out3, "row", (F32,), lambda acc, res: (acc + res,), [x])[0]
    return x_mid, (x, h, zpre, uv)


def _sgu_layer_bwd(dx, dxb, saved, g, w_in3, ln_g, ln_b, w_sp, bias_full, w_out3):
    x, h, zpre, uv = saved
    duv = _mm_stacked_nt("sgu_out_dx", dxb, w_out3, "row")[0]
    g_out = _mm_tn_stacked("sgu_out_dw", uv, dxb, w_out3.shape, "row")
    dz, g_lng, g_lnb, g_wsp, g_bias = _sgu_bwd(zpre, duv, ln_g, ln_b, w_sp, bias_full)
    dh = _mm_stacked_nt("sgu_in_dx", dz, w_in3, "col")[0]
    g_in = _mm_tn_stacked("sgu_in_dw", h, dz, w_in3.shape, "col")
    dx, dxb, dg = _norm_bwd("sgu_norm_bwd", dh, x, g, dx)
    return dx, dxb, dg, g_in, g_out, g_lng, g_lnb, g_wsp, g_bias


def _loss_head(x, target, g):
    d = x.shape[1]

    def fn(xv, tv, gv):
        err = _rms_fwd(xv, gv) - tv
        dxv, dg = _rms_bwd(err * (1.0 / d), xv, gv)
        return dxv, dxv, dg, jnp.sum(err * err, axis=0, keepdims=True)

    return _rowwise("loss_head", fn, [x, target], [g], [(d, F32), (d, BF16)], [g.shape, g.shape])


def _mixer_weights(i, stacks):
    by_rows = lambda a: a.reshape(N_SHARDS * a.shape[1], a.shape[2])
    by_cols = lambda a: a.transpose(1, 0, 2).reshape(a.shape[1], N_SHARDS * a.shape[2])
    if i % 2:
        w_in3, w_out3, ln_g, ln_b = stacks
        return w_in3, ln_g.reshape(1, -1), ln_b.reshape(1, -1), w_out3
    wdkv = by_rows(stacks[0])
    wdkv = jnp.pad(wdkv, ((0, 0), (0, LAT_PAD - wdkv.shape[1])))
    wq = jnp.pad(by_cols(stacks[1]).reshape(Q_RANK, HEADS, QK_HEAD), ((0, 0), (0, 0), (0, HEAD_PAD - QK_HEAD)))
    return wdkv, wq.reshape(Q_RANK, HEADS * HEAD_PAD), by_cols(stacks[2]), by_rows(stacks[3])


def _local_step(x, positions, target, norm_mix, norm_ffn, final_norm, q_norm, kv_norm, w_sp, b_sp, mixers, ffn, reducer):
    tables = _rope_tables(positions)
    gd = mixers[1][2].size // SGU_GROUPS
    bias_full = [jnp.repeat(b_sp[j].T, gd, axis=1) for j in range(DEPTH // 2)]
    saved, mla, sgu = [], [None] * (DEPTH // 2), [None] * (DEPTH // 2)
    for i in range(DEPTH):
        j = i // 2
        x, *stacks = lax.optimization_barrier((x, *mixers[i]))
        if i % 2 == 0:
            wdkv, wq, wkv, wo = mla[j] = _mixer_weights(i, stacks)
            x, s_mix = _mla_fwd(x, norm_mix[i:i + 1], wdkv, q_norm[j:j + 1], kv_norm[j:j + 1], wq, wkv, wo, tables)
        else:
            w_in3, ln_g, ln_b, w_out3 = sgu[j] = _mixer_weights(i, stacks)
            x, s_mix = _sgu_layer_fwd(x, norm_mix[i:i + 1], w_in3, ln_g, ln_b, w_sp[j], bias_full[j], w_out3)
        x, s_ffn = _ffn_fwd(x, norm_ffn[i:i + 1], *ffn[i])
        saved.append((s_mix, s_ffn))
    dx, dxb, g_final, sq_cols = _loss_head(x, target, final_norm[None, :])
    loss = 0.5 * jnp.sum(sq_cols) / x.shape[1]

    def pair(g):
        return g, g.astype(BF16)

    g_mix, g_ffn = [None] * DEPTH, [None] * DEPTH
    mla_g, sgu_g = [None] * (DEPTH // 2), [None] * (DEPTH // 2)
    for i in reversed(range(DEPTH)):
        j = i // 2
        s_mix, s_ffn = saved[i]
        dx, dxb, g_ffn[i], g_up, g_down = _ffn_bwd(dx, dxb, s_ffn, norm_ffn[i:i + 1], *ffn[i])
        dxb = reducer.add(f"ffn{i}", i, {"ffn_w_up": g_up, "ffn_w_down": g_down}, dxb)
        dxb = reducer.phase_end(dxb)
        if i % 2 == 0:
            wdkv, wq, wkv, wo = mla[j]
            dx, dxb, g_mix[i], g_wdkv, g_qn, g_kvn, g_wq, g_wkv, g_wo = _mla_bwd(
                dx, dxb, s_mix, norm_mix[i:i + 1], wdkv, q_norm[j:j + 1], kv_norm[j:j + 1], wq, wkv, wo, tables)
            mla_g[j] = (g_qn, g_kvn)
            g_wq = g_wq.reshape(Q_RANK, HEADS, HEAD_PAD)[..., :QK_HEAD].reshape(Q_RANK, N_SHARDS, -1)
            dxb = reducer.add(f"mla{j}", j, {
                "mla_w_dkv": pair(g_wdkv[:, :Q_RANK + KV_RANK + ROPE].reshape(N_SHARDS, -1, Q_RANK + KV_RANK + ROPE)),
                "mla_w_uq": pair(g_wq.transpose(1, 0, 2)),
                "mla_w_ukv": pair(g_wkv.reshape(KV_RANK, N_SHARDS, -1).transpose(1, 0, 2)),
                "mla_w_o": pair(g_wo.reshape(N_SHARDS, -1, g_wo.shape[1]))}, dxb)
        else:
            w_in3, ln_g, ln_b, w_out3 = sgu[j]
            dx, dxb, g_mix[i], g_in, g_out, g_lng, g_lnb, g_wsp, g_bias = _sgu_layer_bwd(
                dx, dxb, s_mix, norm_mix[i:i + 1], w_in3, ln_g, ln_b, w_sp[j], bias_full[j], w_out3)
            sgu_g[j] = (g_wsp, g_bias.reshape(SGU_CHUNK, SGU_GROUPS, gd).sum(axis=-1).T)
            dxb = reducer.add(f"sgu{j}", j, {"sgu_w_in": g_in, "sgu_w_out": g_out,
                                             "sgu_ln_g": pair(g_lng.reshape(N_SHARDS, -1, LANE)),
                                             "sgu_ln_b": pair(g_lnb.reshape(N_SHARDS, -1, LANE))}, dxb)
        dxb = reducer.phase_end(dxb)
    small = dict(
        norm_mix=jnp.concatenate(g_mix, axis=0), norm_ffn=jnp.concatenate(g_ffn, axis=0), final_norm=g_final[0],
        q_norm=jnp.concatenate([m[0] for m in mla_g], axis=0), kv_norm=jnp.concatenate([m[1] for m in mla_g], axis=0),
        w_sp=jnp.stack([s[0] for s in sgu_g]), b_sp=jnp.stack([s[1] for s in sgu_g]))
    return loss, dx, small


HBM_SPEC = pl.BlockSpec(memory_space=pltpu.HBM)


def _place():
    x, y, c = lax.axis_index("x"), lax.axis_index("y"), lax.axis_index("c")
    return x, y, c, [(1 - x, y), (x, 1 - y), (1 - x, 1 - y)]


def _remote(src, dst, send_sems, recv_sems, k, to):
    return pltpu.make_async_remote_copy(src_ref=src, dst_ref=dst, send_sem=send_sems.at[k], recv_sem=recv_sems.at[k],
                                        device_id=to, device_id_type=MESH)


def _gather_layer(tag, shards):
    n = len(shards)
    split = [s.shape[0] >= 16 for s in shards]

    def body(*refs):
        ins, outs = refs[:n], refs[n:2 * n]
        send_sems, recv_sems, local_sems = refs[2 * n:]
        x, y, c, chips = _place()
        mine = 2 * x + y
        barrier = pltpu.get_barrier_semaphore()
        peers = [(x, y, 1 - c)] + [(*chip, c) for chip in chips]
        for peer in peers:
            pl.semaphore_signal(barrier, inc=1, device_id=peer, device_id_type=MESH)
        pl.semaphore_wait(barrier, len(peers))

        def rows(t, half):
            hr = shards[t].shape[0] // 2
            return pl.ds(half * hr, hr) if split[t] else pl.ds(0, shards[t].shape[0])

        local, sent = [], []
        for t in range(n):
            local.append(pltpu.make_async_copy(ins[t], outs[t].at[mine], local_sems.at[t]))
            local[-1].start()
            for j, chip in enumerate(chips):
                cp = _remote(ins[t].at[rows(t, c)], outs[t].at[mine, rows(t, c)], send_sems, recv_sems, 3 * t + j, (*chip, c))
                cp.start()
                sent.append(cp)
        for j, chip in enumerate(chips):
            theirs = 2 * chip[0] + chip[1]
            for t in range(n):
                piece = outs[t].at[theirs, rows(t, c)]
                _remote(piece, piece, send_sems, recv_sems, 3 * t + j, (x, y, c)).wait_recv()
                if split[t]:
                    cp = _remote(piece, piece, send_sems, recv_sems, 3 * n + 3 * t + j, (x, y, 1 - c))
                    cp.start()
                    sent.append(cp)
        for j, chip in enumerate(chips):
            theirs = 2 * chip[0] + chip[1]
            for t in range(n):
                if split[t]:
                    piece = outs[t].at[theirs, rows(t, 1 - c)]
                    _remote(piece, piece, send_sems, recv_sems, 3 * n + 3 * t + j, (x, y, c)).wait_recv()
        for cp in sent:
            cp.wait_send()
        for cp in local:
            cp.wait()

    return pl.kernel(
        body, name=f"gather_{tag}", mesh=plsc.ScalarSubcoreMesh(axis_name="sequencer", num_cores=1),
        out_type=[jax.ShapeDtypeStruct((N_SHARDS, *s.shape), s.dtype) for s in shards],
        scratch_types=[pltpu.SemaphoreType.DMA((6 * n,)), pltpu.SemaphoreType.DMA((6 * n,)), pltpu.SemaphoreType.DMA((n,))],
        compiler_params=pltpu.CompilerParams(collective_id=ID_GATHER),
    )(*shards)


SEQUENCER = dict(axis_name="sequencer", num_cores=1)
ID_GATHER, ID_EXCHANGE, ID_SHARE = 0, 1, 2
MIN_SPLIT_ROWS = 16


def _handshake(peers):
    barrier = pltpu.get_barrier_semaphore()
    for peer in peers:
        pl.semaphore_signal(barrier, inc=1, device_id=peer, device_id_type=MESH)
    pl.semaphore_wait(barrier, len(peers))


def _half_rows(rows, half):
    return pl.ds(half * (rows // 2), rows // 2) if rows >= MIN_SPLIT_ROWS else pl.ds(0, rows)


SEM_SPEC = pl.BlockSpec(memory_space=pltpu.SEMAPHORE)
DATAFLOW = pltpu.SideEffectType.DATAFLOW_SIDE_EFFECTING


def _exchange_copies(shapes, stacks, lands, send_sems, recv_sems):
    x, y, c, chips = _place()
    mine = 2 * x + y
    copies = []
    for t, shape in enumerate(shapes):
        r = shape[1]
        copies.append(_remote(stacks[t].at[mine, _half_rows(r, 1 - c)], lands[t].at[0], send_sems, recv_sems, 7 * t, (x, y, 1 - c)))
        for j, chip in enumerate(chips):
            theirs = 2 * chip[0] + chip[1]
            copies.append(_remote(stacks[t].at[theirs, _half_rows(r, c)], lands[t].at[1 + j], send_sems, recv_sems,
                                  7 * t + 1 + j, (*chip, c)))
            copies.append(_remote(stacks[t].at[theirs, _half_rows(r, 1 - c)], lands[t].at[4 + j], send_sems, recv_sems,
                                  7 * t + 4 + j, (*chip, 1 - c)))
    return copies


def _exchange_start(tag, stacks, carry):
    n = len(stacks)
    shapes = [s.shape for s in stacks]
    lands = [lax.empty((7, s.shape[1] // 2 if s.shape[1] >= MIN_SPLIT_ROWS else s.shape[1], s.shape[2]), s.dtype) for s in stacks]

    def body(*refs):
        for cp in _exchange_copies(shapes, refs[:n], refs[n:2 * n], refs[2 * n + 1], refs[2 * n + 2]):
            cp.start()

    through = (*stacks, *lands, carry)
    out = pl.pallas_call(
        body, name=f"reduce_exchange_start_{tag}",
        out_shape=(pltpu.SemaphoreType.DMA((7 * n,)), pltpu.SemaphoreType.DMA((7 * n,)),
                   *[pltpu.HBM(a.shape, a.dtype) for a in through]),
        in_specs=[HBM_SPEC] * (2 * n + 1),
        out_specs=(SEM_SPEC, SEM_SPEC, *[HBM_SPEC] * (2 * n + 1)),
        input_output_aliases={t: 2 + t for t in range(2 * n + 1)},
        compiler_params=pltpu.CompilerParams(has_side_effects=DATAFLOW),
    )(*[pltpu.with_memory_space_constraint(a, pltpu.HBM) for a in through])
    return out[0], out[1], out[2:2 + n], out[2 + n:2 + 2 * n], out[-1]


def _exchange_wait(tag, send_sems, recv_sems, stacks, lands, after):
    n = len(stacks)
    shapes = [s.shape for s in stacks]

    def body(*refs):
        for cp in _exchange_copies(shapes, refs[:n], refs[n:2 * n], refs[2 * n], refs[2 * n + 1]):
            cp.wait()

    out = pl.pallas_call(
        body, name=f"reduce_exchange_wait_{tag}",
        out_shape=tuple(pltpu.HBM(a.shape, a.dtype) for a in (*stacks, *lands)),
        in_specs=[HBM_SPEC] * (2 * n) + [SEM_SPEC, SEM_SPEC, pl.BlockSpec(memory_space=pl.ANY)],
        out_specs=tuple([HBM_SPEC] * (2 * n)),
        input_output_aliases={t: t for t in range(2 * n)},
        compiler_params=pltpu.CompilerParams(has_side_effects=DATAFLOW),
    )(*stacks, *lands, send_sems, recv_sems, after)
    return out[:n], out[n:]


def _share_halves(tag, halves):
    n = len(halves)

    def body(*refs):
        ins, outs, send_sems, recv_sems = refs[:n], refs[n:2 * n], refs[2 * n], refs[2 * n + 1]
        x, y, c, _ = _place()
        _handshake([(x, y, 1 - c)])
        sent = [_remote(ins[t], outs[t], send_sems, recv_sems, t, (x, y, 1 - c)) for t in range(n)]
        for cp in sent:
            cp.start()
        for cp in sent:
            cp.wait()

    return pl.kernel(
        body, name=f"reduce_share_{tag}", mesh=plsc.ScalarSubcoreMesh(**SEQUENCER),
        out_type=[jax.ShapeDtypeStruct(h.shape, h.dtype) for h in halves],
        scratch_types=[pltpu.SemaphoreType.DMA((n,)), pltpu.SemaphoreType.DMA((n,))],
        compiler_params=pltpu.CompilerParams(collective_id=ID_SHARE),
    )(*halves)


def _all_reduce_small(part):
    rows = part.shape[0]

    def body(p_ref, out_ref, sib_buf, chip_sums, send_sems, recv_sems):
        x, y, c, chips = _place()
        mine = 2 * x + y
        swap = _remote(p_ref, sib_buf, send_sems, recv_sems, 0, (x, y, 1 - c))
        swap.start()
        swap.wait()
        chip_sums[mine] = p_ref[...] + sib_buf[...]
        sent = [_remote(chip_sums.at[mine], chip_sums.at[mine], send_sems, recv_sems, 1 + j, (*chip, c))
                for j, chip in enumerate(chips)]
        for cp in sent:
            cp.start()
        for j, chip in enumerate(chips):
            sent[j].wait_send()
            theirs = chip_sums.at[2 * chip[0] + chip[1]]
            _remote(theirs, theirs, send_sems, recv_sems, 1 + j, (x, y, c)).wait_recv()
        out_ref[...] = ((chip_sums[0] + chip_sums[1]) + chip_sums[2]) + chip_sums[3]

    vmem = pl.BlockSpec(memory_space=pltpu.VMEM)
    return pl.pallas_call(
        body, name="all_reduce_small", in_specs=[vmem], out_specs=vmem, out_shape=jax.ShapeDtypeStruct(part.shape, F32),
        scratch_shapes=[pltpu.VMEM((rows, LANE), F32), pltpu.VMEM((N_SHARDS, rows, LANE), F32),
                        pltpu.SemaphoreType.DMA((4,)), pltpu.SemaphoreType.DMA((4,))],
        compiler_params=pltpu.CompilerParams(vmem_limit_bytes=VMEM_LIMIT_BYTES),
    )(part)


def _sum_partials(g3, others, sel):
    _, rows, c = others.shape
    whole = g3.shape[1] == rows
    tr = _tile(rows, 512)
    nb = rows // tr

    def body(sel_ref, g_ref, *rest):
        same = g_ref[...].astype(F32)
        for ref in rest[1:4]:
            same = same + ref[...].astype(F32)
        other = rest[0][...].astype(F32)
        for ref in rest[4:7]:
            other = other + ref[...].astype(F32)
        rest[7][...] = same + other

    blk = (None, tr, c)
    slots = [pl.BlockSpec(blk, functools.partial(lambda i, sr, k: (k, i, 0), k=k)) for k in range(7)]
    return pl.pallas_call(
        body, name="reduce_sum_partials",
        grid_spec=pltpu.PrefetchScalarGridSpec(
            num_scalar_prefetch=1, grid=(nb,),
            in_specs=[pl.BlockSpec(blk, lambda i, sr: (sr[0], (0 if whole else sr[1] * nb) + i, 0))] + slots,
            out_specs=pl.BlockSpec((tr, c), lambda i, sr: (i, 0))),
        out_shape=jax.ShapeDtypeStruct((rows, c), F32),
        compiler_params=_params(("parallel",)),
    )(sel, g3, *[others] * 7)


def _adamw_math(w, g, m, v):
    nm = ADAM_B1 * m + (1.0 - ADAM_B1) * g
    nv = ADAM_B2 * v + (1.0 - ADAM_B2) * (g * g)
    m_hat = nm / (1.0 - ADAM_B1 ** ADAM_STEP)
    v_hat = nv / (1.0 - ADAM_B2 ** ADAM_STEP)
    return -ADAM_LR * (m_hat / (jnp.sqrt(v_hat) + ADAM_EPS) + ADAM_WD * w), nm, nv


def _adamw_layer(layer, w, m, v, g_mine, g_sibling, sel, prev):
    lyr, r, c = w.shape
    rows = g_mine.shape[0]
    halves = r // rows
    tr = _tile(rows, 512)
    nb = rows // tr
    n_g = 1 if g_sibling is None else 2

    def body(sel_ref, w_ref, m_ref, v_ref, *rest):
        g = rest[0][...]
        if n_g == 2:
            g = jnp.where(pl.program_id(0) == sel_ref[1], g, rest[1][...])
        outs = rest[n_g + (0 if prev is None else 4):]
        d, nm, nv = _adamw_math(w_ref[...], g, m_ref[...], v_ref[...])
        for ref, val in zip(outs, (g, d, nm, nv)):
            ref[...] = val

    full = pl.BlockSpec((None, tr, c), lambda h, i, sr: (layer, h * nb + i, 0))
    part = pl.BlockSpec((tr, c), lambda h, i, sr: (i, 0))
    n_in = 4 + n_g
    return pl.pallas_call(
        body, name="adamw_layer",
        grid_spec=pltpu.PrefetchScalarGridSpec(
            num_scalar_prefetch=1, grid=(halves, nb),
            in_specs=[full] * 3 + [part] * n_g + ([] if prev is None else [pl.BlockSpec(memory_space=pl.ANY)] * 4),
            out_specs=[full] * 4),
        out_shape=[jax.ShapeDtypeStruct(w.shape, F32)] * 4,
        input_output_aliases={} if prev is None else {n_in + k: k for k in range(4)},
        compiler_params=_params(("parallel", "parallel")),
    )(sel, w, m, v, g_mine, *([] if g_sibling is None else [g_sibling]), *([] if prev is None else prev))


def _adamw(w, g, m, v):
    lyr, r, c = w.shape
    tr = _tile(r, 512)

    def body(w_ref, g_ref, m_ref, v_ref, d_ref, nm_ref, nv_ref):
        gv = g_ref[...]
        nm = ADAM_B1 * m_ref[...] + (1.0 - ADAM_B1) * gv
        nv = ADAM_B2 * v_ref[...] + (1.0 - ADAM_B2) * (gv * gv)
        m_hat = nm / (1.0 - ADAM_B1 ** ADAM_STEP)
        v_hat = nv / (1.0 - ADAM_B2 ** ADAM_STEP)
        d_ref[...] = -ADAM_LR * (m_hat / (jnp.sqrt(v_hat) + ADAM_EPS) + ADAM_WD * w_ref[...])
        nm_ref[...] = nm
        nv_ref[...] = nv

    blk = pl.BlockSpec((None, tr, c), lambda l, i: (l, i, 0))
    return pl.pallas_call(
        body, name="adamw", grid=(lyr, r // tr), in_specs=[blk] * 4, out_specs=[blk] * 3,
        out_shape=[jax.ShapeDtypeStruct(w.shape, F32)] * 3,
        compiler_params=_params(("parallel", "parallel")),
    )(w, g, m, v)


SHARDED = ("mla_w_dkv", "mla_w_uq", "mla_w_ukv", "mla_w_o", "sgu_w_in", "sgu_ln_g", "sgu_ln_b", "sgu_w_out",
           "ffn_w_up", "ffn_w_down")
REPLICATED = ("norm_mix", "norm_ffn", "final_norm", "mla_q_norm", "mla_kv_norm", "sgu_w_spatial", "sgu_b_spatial")
WEIGHTS = ("norm_mix", "norm_ffn", "final_norm", "mla_w_dkv", "mla_q_norm", "mla_kv_norm", "mla_w_uq", "mla_w_ukv",
           "mla_w_o", "sgu_w_in", "sgu_ln_g", "sgu_ln_b", "sgu_w_spatial", "sgu_b_spatial", "sgu_w_out", "ffn_w_up",
           "ffn_w_down")


class _Reducer:
    def __init__(self, state, sel):
        self.state, self.sel = state, sel
        self.started, self.travelling, self.summed = [], [], []
        self.done = {}

    def add(self, tag, layer, grads, token):
        names = list(grads)
        token, *tied = lax.optimization_barrier((token, *[a for n in names for a in grads[n]]))
        f32s, bf16s = tied[0::2], tied[1::2]
        *flying, token = _exchange_start(tag, bf16s, token)
        self.started.append((tag, layer, names, f32s, flying))
        return token

    def phase_end(self, token):
        for tag, layer, names, f32s, flying in self.travelling:
            bf16s, received = _exchange_wait(tag, *flying, token)
            own = [g if g.shape[1] >= MIN_SPLIT_ROWS else gb for g, gb in zip(f32s, bf16s)]
            mine = [_sum_partials(g, got, self.sel) for g, got in zip(own, received)]
            token, *mine = lax.optimization_barrier((token, *mine))
            cut = [k for k, g in enumerate(own) if g.shape[1] >= MIN_SPLIT_ROWS]
            theirs = dict(zip(cut, _share_halves(tag, [mine[k] for k in cut])))
            self.summed.append((layer, names, mine, [theirs.get(k) for k in range(len(names))]))
        self.travelling, self.started = self.started, []
        return token

    def update(self, token):
        for layer, names, mine, theirs in self.summed:
            for name, g_mine, g_theirs in zip(names, mine, theirs):
                w, m, v = self.state[name]
                self.done[name] = _adamw_layer(layer, w, m, v, g_mine, g_theirs, self.sel, self.done.get(name))
                token = self.done[name][1]
        self.summed = []
        return token


def _as3d(name, a):
    return a.reshape(a.shape[0], -1, LANE) if name in ("sgu_ln_g", "sgu_ln_b") else a


def _pack(parts):
    flat = jnp.concatenate([p.reshape(-1) for p in parts])
    rows = -(-flat.shape[0] // (8 * LANE)) * 8
    return jnp.pad(flat, (0, rows * LANE - flat.shape[0])).reshape(rows, LANE)


def _unpack(packed, like):
    flat, out, at = packed.reshape(-1), [], 0
    for p in like:
        out.append(flat[at:at + p.size].reshape(p.shape))
        at += p.size
    return out


def kernel(x, positions, norm_mix, norm_ffn, final_norm, mla_w_dkv, mla_q_norm, mla_kv_norm, mla_w_uq, mla_w_ukv, mla_w_o, sgu_w_in, sgu_ln_g, sgu_ln_b, sgu_w_spatial, sgu_b_spatial, sgu_w_out, ffn_w_up, ffn_w_down, loss_target, m_norm_mix, m_norm_ffn, m_final_norm, m_mla_w_dkv, m_mla_q_norm, m_mla_kv_norm, m_mla_w_uq, m_mla_w_ukv, m_mla_w_o, m_sgu_w_in, m_sgu_ln_g, m_sgu_ln_b, m_sgu_w_spatial, m_sgu_b_spatial, m_sgu_w_out, m_ffn_w_up, m_ffn_w_down, v_norm_mix, v_norm_ffn, v_final_norm, v_mla_w_dkv, v_mla_q_norm, v_mla_kv_norm, v_mla_w_uq, v_mla_w_ukv, v_mla_w_o, v_sgu_w_in, v_sgu_ln_g, v_sgu_ln_b, v_sgu_w_spatial, v_sgu_b_spatial, v_sgu_w_out, v_ffn_w_up, v_ffn_w_down):
    given = dict(locals())
    w = {n: given[n] for n in WEIGHTS}
    mom = {n: given["m_" + n] for n in WEIGHTS}
    var = {n: given["v_" + n] for n in WEIGHTS}
    mixers, ffn, token = [], [], None
    for i in range(DEPTH):
        j = i // 2
        if i % 2 == 0:
            mixer = [w[n][j].astype(BF16) for n in ("mla_w_dkv", "mla_w_uq", "mla_w_ukv", "mla_w_o")]
        else:
            mixer = [sgu_w_in[j].astype(BF16), sgu_w_out[j].astype(BF16), sgu_ln_g[j].reshape(-1, LANE),
                     sgu_ln_b[j].reshape(-1, LANE)]
        for tag, shards, into in ((f"mixer{i}", mixer, mixers), (f"ffn{i}", [ffn_w_up[i].astype(BF16), ffn_w_down[i].astype(BF16)], ffn)):
            if token is None:
                token = shards[0]
            else:
                token, *shards = lax.optimization_barrier((token, *shards))
            into.append(_gather_layer(tag, shards))

    x_i, y_i, c_i = lax.axis_index("x"), lax.axis_index("y"), lax.axis_index("c")
    sel = jnp.stack([2 * x_i + y_i, c_i]).astype(jnp.int32)
    reducer = _Reducer({n: tuple(_as3d(n, d[n]) for d in (w, mom, var)) for n in SHARDED}, sel)
    loss, dx, small = _local_step(
        x[0], positions[0], loss_target[0], norm_mix, norm_ffn, final_norm, mla_q_norm, mla_kv_norm, sgu_w_spatial,
        sgu_b_spatial, mixers, ffn, reducer)
    loss = lax.psum(loss, ("x", "y", "c"))

    small_g = [small["norm_mix"], small["norm_ffn"], small["final_norm"], small["q_norm"], small["kv_norm"],
               small["w_sp"], small["b_sp"]]
    like = [w[n] for n in REPLICATED]
    g_small = _all_reduce_small(_pack(small_g))
    packed = [_pack([d[n] for n in REPLICATED])[None] for d in (w, mom, var)]
    upd_small = _adamw(packed[0], g_small[None], packed[1], packed[2])
    grads = dict(zip(REPLICATED, _unpack(g_small, like)))
    delta, new_m, new_v = ({n: a for n, a in zip(REPLICATED, _unpack(u[0], like))} for u in upd_small)

    reducer.phase_end(reducer.update(upd_small[0]))
    reducer.update(None)
    for n in SHARDED:
        grads[n], delta[n], new_m[n], new_v[n] = (a.reshape(w[n].shape) for a in reducer.done[n])

    return (loss, dx[None], *[grads[n] for n in WEIGHTS], *[delta[n] for n in WEIGHTS],
            *[new_m[n] for n in WEIGHTS], *[new_v[n] for n in WEIGHTS])
```

```python
import functools
import math

import jax
import jax.numpy as jnp
from jax import lax
from jax.experimental import pallas as pl
from jax.experimental.pallas import tpu as pltpu
from jax.experimental.pallas import tpu_sc as plsc

F32 = jnp.float32
BF16 = jnp.bfloat16
MESH = pl.DeviceIdType.MESH

DEPTH = 4
HEADS = 8
NOPE = 128
ROPE = 64
VHEAD = 128
QK_HEAD = NOPE + ROPE
Q_RANK = 256
KV_RANK = 128
HEAD_PAD = 256
LAT_PAD = 512
ROPE_THETA = 10000.0
SGU_CHUNK = 128
SGU_GROUPS = 8
NORM_EPS = 1e-6
LN_EPS = 1e-5
ADAM_LR, ADAM_B1, ADAM_B2, ADAM_EPS, ADAM_WD, ADAM_STEP = 0.001, 0.9, 0.999, 1e-08, 0.01, 10

N_SHARDS = 4
LANE = 128
VMEM_LIMIT_BYTES = 56 * 1024 * 1024
ATT_TILE = 512
MM_TILE = 1024
ATT_SCALE = QK_HEAD ** -0.5
LOG2_SCALE = ATT_SCALE * math.log2(math.e)

NN = (((1,), (0,)), ((), ()))
NT = (((1,), (1,)), ((), ()))
TN = (((0,), (0,)), ((), ()))


def _params(sem):
    return pltpu.CompilerParams(dimension_semantics=sem, vmem_limit_bytes=VMEM_LIMIT_BYTES)


def _tile(n, pref):
    t = min(n, pref)
    while n % t:
        t //= 2
    return t


def _matmul(name, a, b, a_spec, b_spec, dims, grid, tile, outs, extras=(), epilogue=None):
    nk, ne, no = grid[2], len(extras), len(outs)
    b_specs = list(b_spec) if isinstance(b_spec, (list, tuple)) else [b_spec]
    nb = len(b_specs)

    def body(a_ref, *rest):
        b_refs, e_refs, o_refs = rest[:nb], rest[nb:nb + ne], rest[nb + ne:nb + ne + no]
        kw = a_ref.shape[1] // nb
        part = None
        for p, b_ref in enumerate(b_refs):
            a_tile = a_ref[...] if nb == 1 else a_ref[:, p * kw:(p + 1) * kw]
            d = lax.dot_general(a_tile.astype(BF16), b_ref[...].astype(BF16), dims, preferred_element_type=F32)
            part = d if part is None else part + d

        def finish(acc):
            vals = (acc,) if epilogue is None else epilogue(acc, *[e[...] for e in e_refs])
            for o_ref, v in zip(o_refs, vals):
                o_ref[...] = v.astype(o_ref.dtype)

        if nk == 1:
            finish(part)
            return
        acc_ref, k = rest[-1], pl.program_id(2)

        @pl.when(k == 0)
        def _():
            acc_ref[...] = part

        @pl.when(jnp.logical_and(k > 0, k < nk - 1))
        def _():
            acc_ref[...] += part

        @pl.when(k == nk - 1)
        def _():
            finish(acc_ref[...] + part)

    return pl.pallas_call(
        body, name=name, grid=grid,
        in_specs=[a_spec] + b_specs + [s for _, s in extras],
        out_specs=[s for _, s in outs], out_shape=[s for s, _ in outs],
        scratch_shapes=[pltpu.VMEM(tile, F32)] if nk > 1 else [],
        compiler_params=_params(("parallel", "parallel", "arbitrary")),
    )(a, *[b] * nb, *[e for e, _ in extras])


def _mm(name, a, b, out_dtypes=(F32,), epilogue=None, extras=(), tm=MM_TILE, tn=MM_TILE, tk=MM_TILE, nt=False):
    m, kd = a.shape
    n = b.shape[0] if nt else b.shape[1]
    tm, tn, tk = _tile(m, tm), _tile(n, tn), _tile(kd, tk)
    o_spec = pl.BlockSpec((tm, tn), lambda i, j, k: (i, j))
    b_spec = pl.BlockSpec((tn, tk), lambda i, j, k: (j, k)) if nt else pl.BlockSpec((tk, tn), lambda i, j, k: (k, j))
    return _matmul(name, a, b, pl.BlockSpec((tm, tk), lambda i, j, k: (i, k)), b_spec, NT if nt else NN,
                   (m // tm, n // tn, kd // tk), (tm, tn),
                   [(jax.ShapeDtypeStruct((m, n), d), o_spec) for d in out_dtypes],
                   [(e, o_spec) for e in extras], epilogue)


def _mm_tn(name, a, b, out_dtypes=(F32,), tm=MM_TILE, tn=MM_TILE, tk=MM_TILE):
    s, m = a.shape
    n = b.shape[1]
    tm, tn, tk = _tile(m, tm), _tile(n, tn), _tile(s, tk)
    o_spec = pl.BlockSpec((tm, tn), lambda i, j, k: (i, j))
    return _matmul(name, a, b, pl.BlockSpec((tk, tm), lambda i, j, k: (k, i)),
                   pl.BlockSpec((tk, tn), lambda i, j, k: (k, j)), TN, (m // tm, n // tn, s // tk), (tm, tn),
                   [(jax.ShapeDtypeStruct((m, n), d), o_spec) for d in out_dtypes])


def _mm_stacked(name, a, w3, mode, out_dtypes=(F32,), epilogue=None, extras=(), tm=MM_TILE, tn=MM_TILE, tk=MM_TILE):
    m, kd = a.shape
    _, r, c = w3.shape
    n = c if mode == "row" else N_SHARDS * c
    if mode == "row":
        tm, tn, tk = _tile(m, tm // 2), _tile(n, tn), kd
        b_spec = [pl.BlockSpec((None, r, tn), functools.partial(lambda i, j, k, p: (p, 0, j), p=p)) for p in range(N_SHARDS)]
    else:
        tm, tn, tk = _tile(m, tm), _tile(c, tn), _tile(kd, tk)
        per = c // tn
        b_spec = pl.BlockSpec((None, tk, tn), lambda i, j, k: (j // per, k, j % per))
    o_spec = pl.BlockSpec((tm, tn), lambda i, j, k: (i, j))
    return _matmul(name, a, w3, pl.BlockSpec((tm, tk), lambda i, j, k: (i, k)), b_spec, NN,
                   (m // tm, n // tn, kd // tk), (tm, tn),
                   [(jax.ShapeDtypeStruct((m, n), d), o_spec) for d in out_dtypes],
                   [(e, o_spec) for e in extras], epilogue)


def _mm_stacked_nt(name, a, w3, mode, out_dtypes=(F32,), epilogue=None, extras=(), tm=MM_TILE, tn=MM_TILE, tk=MM_TILE):
    m, nd = a.shape
    _, r, c = w3.shape
    kout = N_SHARDS * r if mode == "row" else r
    if mode == "row":
        tm, tn, tk = _tile(m, tm), _tile(r, tn), _tile(c, tk)
        per = r // tn
        b_spec = pl.BlockSpec((None, tn, tk), lambda i, j, k: (j // per, j % per, k))
    else:
        tm, tn, tk = _tile(m, tm // 2), _tile(r, tn), nd
        b_spec = [pl.BlockSpec((None, tn, c), functools.partial(lambda i, j, k, p: (p, j, 0), p=p)) for p in range(N_SHARDS)]
    o_spec = pl.BlockSpec((tm, tn), lambda i, j, k: (i, j))
    return _matmul(name, a, w3, pl.BlockSpec((tm, tk), lambda i, j, k: (i, k)), b_spec, NT,
                   (m // tm, kout // tn, nd // tk), (tm, tn),
                   [(jax.ShapeDtypeStruct((m, kout), d), o_spec) for d in out_dtypes],
                   [(e, o_spec) for e in extras], epilogue)


def _mm_tn_stacked(name, a, b, shape3, mode, tm=MM_TILE, tn=MM_TILE, tk=MM_TILE):
    s, m = a.shape
    n = b.shape[1]
    _, r, c = shape3
    tk, tn = s, tn // 2
    if mode == "row":
        tm, tn = _tile(r, tm), _tile(n, tn)
        per = r // tm
        o_spec = pl.BlockSpec((None, tm, tn), lambda i, j, k: (i // per, i % per, j))
    else:
        tm, tn = _tile(m, tm), _tile(c, tn)
        per = c // tn
        o_spec = pl.BlockSpec((None, tm, tn), lambda i, j, k: (j // per, i, j % per))
    outs = [(jax.ShapeDtypeStruct(shape3, F32), o_spec), (jax.ShapeDtypeStruct(shape3, BF16), o_spec)]
    return _matmul(name, a, b, pl.BlockSpec((tk, tm), lambda i, j, k: (k, i)),
                   pl.BlockSpec((tk, tn), lambda i, j, k: (k, j)), TN, (m // tm, n // tn, s // tk), (tm, tn),
                   outs, epilogue=lambda acc: (acc, acc))


def _rowwise(name, fn, rows, consts, out_rows, out_accs=(), tr=256):
    nr, nc, no = len(rows), len(consts), len(out_rows)
    n_rows = rows[0].shape[0]
    tr = _tile(n_rows, tr)

    def body(*refs):
        vals = fn(*[r[...] for r in refs[:nr + nc]])
        o_refs, a_refs = refs[nr + nc:nr + nc + no], refs[nr + nc + no:]
        for ref, v in zip(o_refs, vals[:no]):
            ref[...] = v.astype(ref.dtype)
        first = pl.program_id(0) == 0

        @pl.when(first)
        def _():
            for ref, v in zip(a_refs, vals[no:]):
                ref[...] = v

        @pl.when(jnp.logical_not(first))
        def _():
            for ref, v in zip(a_refs, vals[no:]):
                ref[...] += v

    def whole(shape):
        return pl.BlockSpec(shape, lambda i: (0,) * len(shape))

    return pl.pallas_call(
        body, name=name, grid=(n_rows // tr,),
        in_specs=[pl.BlockSpec((tr, a.shape[1]), lambda i: (i, 0)) for a in rows] + [whole(c.shape) for c in consts],
        out_specs=[pl.BlockSpec((tr, f), lambda i: (i, 0)) for f, _ in out_rows] + [whole(s) for s in out_accs],
        out_shape=[jax.ShapeDtypeStruct((n_rows, f), d) for f, d in out_rows]
        + [jax.ShapeDtypeStruct(s, F32) for s in out_accs],
        compiler_params=_params(("arbitrary",)),
    )(*rows, *consts)


def _rms_fwd(x, g):
    return x * lax.rsqrt(jnp.mean(x * x, axis=-1, keepdims=True) + NORM_EPS) * g


def _rms_bwd(dy, x, g):
    rstd = lax.rsqrt(jnp.mean(x * x, axis=-1, keepdims=True) + NORM_EPS)
    n = x * rstd
    dn = dy * g
    dx = rstd * (dn - n * jnp.mean(dn * n, axis=-1, keepdims=True))
    return dx, jnp.sum(dy * n, axis=0, keepdims=True)


def _rope(x, cs, s1, s2):
    return x * cs + pltpu.roll(x, 32, 1) * s1 + pltpu.roll(x, 96, 1) * s2


def _rope_t(dy, cs, s1, s2):
    return dy * cs + pltpu.roll(dy * s1, 96, 1) + pltpu.roll(dy * s2, 32, 1)


def _gelu(z):
    return 0.5 * z * (1.0 + lax.erf(z * (1.0 / math.sqrt(2.0))))


def _gelu_and_grad(z):
    cdf = 0.5 * (1.0 + lax.erf(z * (1.0 / math.sqrt(2.0))))
    return z * cdf, cdf + z * jnp.exp(-0.5 * z * z) * (1.0 / math.sqrt(2.0 * math.pi))


def _att_scores(q, kv, kr, scale, masked, transposed):
    k = jnp.concatenate([kv[:, :NOPE], kr], axis=1)
    if transposed:
        s = lax.dot_general(k, q, NT, preferred_element_type=F32) * scale
    else:
        s = lax.dot_general(q, k, NT, preferred_element_type=F32) * scale
    if masked:
        r = lax.broadcasted_iota(jnp.int32, s.shape, 0)
        c = lax.broadcasted_iota(jnp.int32, s.shape, 1)
        s = jnp.where((r <= c) if transposed else (c <= r), s, -jnp.inf)
    return s, k


def _in_pairs(lo, hi, pair, single):
    n = hi - lo

    def body(p, carry):
        pair(lo + 2 * p, lo + 2 * p + 1)
        return carry

    lax.fori_loop(0, n // 2, body, 0)

    @pl.when(n % 2 == 1)
    def _():
        single(hi - 1)


def _flash_fwd(qb, kvb, krb):
    s_len = qb.shape[0]
    t = ATT_TILE

    def body(q_ref, kv_ref, kr_ref, o_ref, lse_ref, m_s, l_s, acc_s):
        qi = pl.program_id(1)
        m_s[...] = jnp.full_like(m_s, -jnp.inf)
        l_s[...] = jnp.zeros_like(l_s)
        acc_s[...] = jnp.zeros_like(acc_s)
        q = q_ref[...]

        def scores(ki, masked):
            rows = pl.ds(pl.multiple_of(ki * t, t), t)
            kv = kv_ref[rows, :]
            return _att_scores(q, kv, kr_ref[rows, :], LOG2_SCALE, masked, False)[0], kv

        def update(s, kv):
            m_prev = m_s[...]
            m_new = jnp.maximum(m_prev, jnp.max(s, axis=1, keepdims=True))
            alpha = jnp.exp2(m_prev - m_new)
            p = jnp.exp2(s - jnp.tile(m_new, (1, t // LANE)))
            l_s[...] = alpha * l_s[...] + jnp.sum(p, axis=1, keepdims=True)
            acc_s[...] = alpha * acc_s[...] + jnp.dot(p.astype(BF16), kv[:, NOPE:], preferred_element_type=F32)
            m_s[...] = m_new

        def pair(k0, k1):
            first, second = scores(k0, False), scores(k1, False)
            update(*first)
            update(*second)

        _in_pairs(0, qi, pair, lambda ki: update(*scores(ki, False)))
        update(*scores(qi, True))
        o_ref[...] = (acc_s[...] / l_s[...]).astype(o_ref.dtype)
        lse_ref[...] = (m_s[...] + jnp.log2(l_s[...]))[:, :1]

    return pl.pallas_call(
        body, name="flash_fwd", grid=(HEADS, s_len // t),
        in_specs=[pl.BlockSpec((t, HEAD_PAD), lambda h, qi: (qi, h)),
                  pl.BlockSpec((s_len, HEAD_PAD), lambda h, qi: (0, h)),
                  pl.BlockSpec((s_len, LANE), lambda h, qi: (0, 0))],
        out_specs=[pl.BlockSpec((t, VHEAD), lambda h, qi: (qi, h)),
                   pl.BlockSpec((None, t, 1), lambda h, qi: (h, qi, 0))],
        out_shape=[jax.ShapeDtypeStruct((s_len, HEADS * VHEAD), BF16),
                   jax.ShapeDtypeStruct((HEADS, s_len, 1), F32)],
        scratch_shapes=[pltpu.VMEM((t, LANE), F32), pltpu.VMEM((t, LANE), F32), pltpu.VMEM((t, VHEAD), F32)],
        compiler_params=_params(("parallel", "arbitrary")),
    )(qb, kvb, krb)


def _flash_bwd_dq(qb, kvb, krb, dob, lse, delta):
    s_len = qb.shape[0]
    t = ATT_TILE
    nq = s_len // t
    scale = QK_HEAD ** -0.5

    def body(q_ref, kv_ref, kr_ref, do_ref, lse_ref, dl_ref, dq_ref, acc_s):
        qi = pl.program_id(1)
        acc_s[...] = jnp.zeros_like(acc_s)
        q, do = q_ref[...], do_ref[...]
        lse = jnp.broadcast_to(lse_ref[...], (t, LANE))
        dl = jnp.broadcast_to(dl_ref[...], (t, LANE))

        def products(ki, masked):
            rows = pl.ds(pl.multiple_of(ki * t, t), t)
            kv = kv_ref[rows, :]
            s, k = _att_scores(q, kv, kr_ref[rows, :], LOG2_SCALE, masked, False)
            return s, lax.dot_general(do, kv[:, NOPE:], NT, preferred_element_type=F32), k

        def update(s, dp, k):
            p = jnp.exp2(s - jnp.tile(lse, (1, t // LANE)))
            ds = (p * (dp - jnp.tile(dl, (1, t // LANE))) * scale).astype(BF16)
            acc_s[...] += jnp.dot(ds, k, preferred_element_type=F32)

        def pair(k0, k1):
            first, second = products(k0, False), products(k1, False)
            update(*first)
            update(*second)

        _in_pairs(0, qi, pair, lambda ki: update(*products(ki, False)))
        update(*products(qi, True))
        dq_ref[...] = acc_s[...]

    col = pl.BlockSpec((None, t, 1), lambda h, qi: (h, qi, 0))
    return pl.pallas_call(
        body, name="flash_bwd_dq", grid=(HEADS, nq),
        in_specs=[pl.BlockSpec((t, HEAD_PAD), lambda h, qi: (qi, h)),
                  pl.BlockSpec((s_len, HEAD_PAD), lambda h, qi: (0, h)),
                  pl.BlockSpec((s_len, LANE), lambda h, qi: (0, 0)),
                  pl.BlockSpec((t, VHEAD), lambda h, qi: (qi, h)), col, col],
        out_specs=pl.BlockSpec((t, HEAD_PAD), lambda h, qi: (qi, h)),
        out_shape=jax.ShapeDtypeStruct((s_len, HEADS * HEAD_PAD), F32),
        scratch_shapes=[pltpu.VMEM((t, HEAD_PAD), F32)],
        compiler_params=_params(("parallel", "arbitrary")),
    )(qb, kvb, krb, dob, lse, delta)


def _flash_bwd_dkv(qb, kvb, krb, dob, lse_row, delta_row):
    s_len = qb.shape[0]
    t = ATT_TILE
    nq = s_len // t
    scale = QK_HEAD ** -0.5

    def body(q_ref, kv_ref, kr_ref, do_ref, lse_ref, dl_ref, dkv_ref, dkr_ref, dk_s, dv_s):
        ki = pl.program_id(1)
        dk_s[...] = jnp.zeros_like(dk_s)
        dv_s[...] = jnp.zeros_like(dv_s)
        kv, kr = kv_ref[...], kr_ref[...]

        def products(qi, masked):
            rows = pl.ds(pl.multiple_of(qi * t, t), t)
            q, do = q_ref[rows, :], do_ref[rows, :]
            st, _ = _att_scores(q, kv, kr, LOG2_SCALE, masked, True)
            return st, lax.dot_general(kv[:, NOPE:], do, NT, preferred_element_type=F32), q, do, rows

        def update(st, dpt, q, do, rows):
            pt = jnp.exp2(st - lse_ref[:, rows])
            dv_s[...] += jnp.dot(pt.astype(BF16), do, preferred_element_type=F32)
            dst = (pt * (dpt - dl_ref[:, rows]) * scale).astype(BF16)
            dk_s[...] += jnp.dot(dst, q, preferred_element_type=F32)

        def pair(q0, q1):
            first, second = products(q0, False), products(q1, False)
            update(*first)
            update(*second)

        update(*products(ki, True))
        _in_pairs(ki + 1, nq, pair, lambda qi: update(*products(qi, False)))
        dk = dk_s[...]
        dkv_ref[...] = jnp.concatenate([dk[:, :NOPE], dv_s[...]], axis=1).astype(dkv_ref.dtype)
        dkr_ref[...] = dk[:, NOPE:]

    row = pl.BlockSpec((None, 1, s_len), lambda h, ki: (h, 0, 0))
    return pl.pallas_call(
        body, name="flash_bwd_dkv", grid=(HEADS, nq),
        in_specs=[pl.BlockSpec((s_len, HEAD_PAD), lambda h, ki: (0, h)),
                  pl.BlockSpec((t, HEAD_PAD), lambda h, ki: (ki, h)),
                  pl.BlockSpec((t, LANE), lambda h, ki: (ki, 0)),
                  pl.BlockSpec((s_len, VHEAD), lambda h, ki: (0, h)), row, row],
        out_specs=[pl.BlockSpec((t, HEAD_PAD), lambda h, ki: (ki, h)),
                   pl.BlockSpec((t, LANE), lambda h, ki: (ki, h))],
        out_shape=[jax.ShapeDtypeStruct((s_len, HEADS * HEAD_PAD), BF16),
                   jax.ShapeDtypeStruct((s_len, HEADS * LANE), F32)],
        scratch_shapes=[pltpu.VMEM((t, HEAD_PAD), F32), pltpu.VMEM((t, VHEAD), F32)],
        compiler_params=_params(("parallel", "parallel")),
    )(qb, kvb, krb, dob, lse_row, delta_row)


def _att_delta(do, ob):
    s_len = do.shape[0]
    t = ATT_TILE

    def body(do_ref, o_ref, dob_ref, dl_ref):
        d = do_ref[...]
        dob_ref[...] = d.astype(dob_ref.dtype)
        dl_ref[...] = jnp.sum(d * o_ref[...].astype(F32), axis=1, keepdims=True)

    blk = pl.BlockSpec((t, VHEAD), lambda i, h: (i, h))
    return pl.pallas_call(
        body, name="att_delta", grid=(s_len // t, HEADS), in_specs=[blk, blk],
        out_specs=[blk, pl.BlockSpec((None, t, 1), lambda i, h: (h, i, 0))],
        out_shape=[jax.ShapeDtypeStruct(do.shape, BF16), jax.ShapeDtypeStruct((HEADS, s_len, 1), F32)],
        compiler_params=_params(("parallel", "parallel")),
    )(do, ob)


def _tril(w):
    r = lax.broadcasted_iota(jnp.int32, w.shape, 0)
    c = lax.broadcasted_iota(jnp.int32, w.shape, 1)
    return jnp.where(c <= r, w, 0.0)


def _sgu_mix(w_ref, vln, gd):
    wcs = [_tril(w_ref[g]).astype(BF16) for g in range(SGU_GROUPS)]
    mixed = jnp.concatenate(
        [jnp.dot(wcs[g], vln[:, g * gd:(g + 1) * gd], preferred_element_type=F32) for g in range(SGU_GROUPS)], axis=1)
    return wcs, mixed


def _sgu_fwd(zpre, ln_g, ln_b, w_sp, bias_full):
    s_len, two_w = zpre.shape
    width = two_w // 2
    gd = width // SGU_GROUPS
    t = SGU_CHUNK

    def body(z_ref, g_ref, b_ref, w_ref, bias_ref, uv_ref):
        u = _gelu(z_ref[:, :width])
        v = _gelu(z_ref[:, width:])
        d = v - jnp.mean(v, axis=-1, keepdims=True)
        vhat = d * lax.rsqrt(jnp.mean(d * d, axis=-1, keepdims=True) + LN_EPS)
        vln = (vhat * g_ref[...] + b_ref[...]).astype(BF16)
        _, mixed = _sgu_mix(w_ref, vln, gd)
        uv_ref[...] = (u * (mixed + bias_ref[...])).astype(uv_ref.dtype)

    return pl.pallas_call(
        body, name="sgu_fwd", grid=(s_len // t,),
        in_specs=[pl.BlockSpec((t, two_w), lambda i: (i, 0)), pl.BlockSpec((1, width), lambda i: (0, 0)),
                  pl.BlockSpec((1, width), lambda i: (0, 0)), pl.BlockSpec(w_sp.shape, lambda i: (0, 0, 0)),
                  pl.BlockSpec((t, width), lambda i: (0, 0))],
        out_specs=pl.BlockSpec((t, width), lambda i: (i, 0)),
        out_shape=jax.ShapeDtypeStruct((s_len, width), BF16),
        compiler_params=_params(("parallel",)),
    )(zpre, ln_g, ln_b, w_sp, bias_full)


def _sgu_bwd(zpre, duv, ln_g, ln_b, w_sp, bias_full):
    s_len, two_w = zpre.shape
    width = two_w // 2
    gd = width // SGU_GROUPS
    t = SGU_CHUNK

    def body(z_ref, duv_ref, g_ref, b_ref, w_ref, bias_ref, dz_ref, dg_ref, db_ref, dw_ref, dbias_ref):
        first = pl.program_id(0) == 0

        def accumulate(ref, val):
            @pl.when(first)
            def _():
                ref[...] = val

            @pl.when(jnp.logical_not(first))
            def _():
                ref[...] += val

        u, u_grad = _gelu_and_grad(z_ref[:, :width])
        v, v_grad = _gelu_and_grad(z_ref[:, width:])
        d = v - jnp.mean(v, axis=-1, keepdims=True)
        rstd = lax.rsqrt(jnp.mean(d * d, axis=-1, keepdims=True) + LN_EPS)
        vhat = d * rstd
        vln = (vhat * g_ref[...] + b_ref[...]).astype(BF16)
        wcs, mixed = _sgu_mix(w_ref, vln, gd)
        duv_v = duv_ref[...]
        du = duv_v * (mixed + bias_ref[...])
        dmixed = duv_v * u
        dmb = dmixed.astype(BF16)
        dvln = jnp.concatenate(
            [lax.dot_general(wcs[g], dmb[:, g * gd:(g + 1) * gd], TN, preferred_element_type=F32)
             for g in range(SGU_GROUPS)], axis=1)
        for g in range(SGU_GROUPS):
            dw = lax.dot_general(dmb[:, g * gd:(g + 1) * gd], vln[:, g * gd:(g + 1) * gd], NT, preferred_element_type=F32)
            accumulate(dw_ref.at[g], _tril(dw))
        dvhat = dvln * g_ref[...]
        dv0 = rstd * (dvhat - jnp.mean(dvhat, axis=-1, keepdims=True)
                      - vhat * jnp.mean(dvhat * vhat, axis=-1, keepdims=True))
        dz_ref[:, :width] = (du * u_grad).astype(dz_ref.dtype)
        dz_ref[:, width:] = (dv0 * v_grad).astype(dz_ref.dtype)
        accumulate(dg_ref, jnp.sum(dvln * vhat, axis=0, keepdims=True))
        accumulate(db_ref, jnp.sum(dvln, axis=0, keepdims=True))
        accumulate(dbias_ref, dmixed)

    vec = pl.BlockSpec((1, width), lambda i: (0, 0))
    return pl.pallas_call(
        body, name="sgu_bwd", grid=(s_len // t,),
        in_specs=[pl.BlockSpec((t, two_w), lambda i: (i, 0)), pl.BlockSpec((t, width), lambda i: (i, 0)), vec, vec,
                  pl.BlockSpec(w_sp.shape, lambda i: (0, 0, 0)), pl.BlockSpec((t, width), lambda i: (0, 0))],
        out_specs=[pl.BlockSpec((t, two_w), lambda i: (i, 0)), vec, vec,
                   pl.BlockSpec(w_sp.shape, lambda i: (0, 0, 0)), pl.BlockSpec((t, width), lambda i: (0, 0))],
        out_shape=[jax.ShapeDtypeStruct((s_len, two_w), BF16), jax.ShapeDtypeStruct((1, width), F32),
                   jax.ShapeDtypeStruct((1, width), F32), jax.ShapeDtypeStruct(w_sp.shape, F32),
                   jax.ShapeDtypeStruct((t, width), F32)],
        compiler_params=_params(("arbitrary",)),
    )(zpre, duv, ln_g, ln_b, w_sp, bias_full)


def _rope_tables(positions):
    inv_freq = ROPE_THETA ** (-jnp.arange(0, ROPE, 2, dtype=F32) / ROPE)
    ang = positions.astype(F32)[:, None] * inv_freq
    cos, sin = jnp.cos(ang), jnp.sin(ang)
    z32, z64 = jnp.zeros_like(cos), jnp.zeros((cos.shape[0], LANE - ROPE), F32)
    return (jnp.concatenate([cos, cos, z64], axis=1), jnp.concatenate([z32, sin, z64], axis=1),
            jnp.concatenate([-sin, z32, z64], axis=1))


def _q_rope(name, q, tables, transpose):
    rot = _rope_t if transpose else _rope

    def fn(qv, cs, s1, s2):
        parts = []
        for h in range(HEADS):
            parts.append(qv[:, h * HEAD_PAD:h * HEAD_PAD + NOPE])
            parts.append(rot(qv[:, h * HEAD_PAD + NOPE:(h + 1) * HEAD_PAD], cs, s1, s2))
        return (jnp.concatenate(parts, axis=1),)

    return _rowwise(name, fn, [q, *tables], [], [(q.shape[1], BF16)])[0]


def _ffn_fwd(x, g, w_up3, w_down3):
    h2 = _rowwise("ffn_norm", lambda xv, gv: (_rms_fwd(xv, gv),), [x], [g], [(x.shape[1], BF16)])[0]

    def sq_relu(acc):
        r = jnp.maximum(acc, 0.0)
        return (r * r,)

    r = _mm_stacked("ffn_up", h2, w_up3, "col", (BF16,), sq_relu)[0]
    x_out = _mm_stacked("ffn_down", r, w_down3, "row", (F32,), lambda acc, res: (acc + res,), [x])[0]
    return x_out, (x, h2, r)


def _ffn_bwd(dx, dxb, saved, g, w_up3, w_down3):
    x, h2, r = saved
    da = _mm_stacked_nt("ffn_down_dx", dxb, w_down3, "row", (BF16,),
                        lambda acc, rv: (acc * (2.0 * jnp.sqrt(rv.astype(F32))),), [r])[0]
    g_down = _mm_tn_stacked("ffn_down_dw", r, dxb, w_down3.shape, "row")
    dh2 = _mm_stacked_nt("ffn_up_dx", da, w_up3, "col", (F32,))[0]
    g_up = _mm_tn_stacked("ffn_up_dw", h2, da, w_up3.shape, "col")
    dx, dxb, dg = _norm_bwd("ffn_norm_bwd", dh2, x, g, dx)
    return dx, dxb, dg, g_up, g_down


def _norm_bwd(name, dh, x, g, dres):
    def fn(dhv, xv, rv, gv):
        dxv, dg = _rms_bwd(dhv, xv, gv)
        return dxv + rv, dxv + rv, dg

    return _rowwise(name, fn, [dh, x, dres], [g], [(x.shape[1], F32), (x.shape[1], BF16)], [g.shape])


def _mla_fwd(x, g, wdkv, q_norm, kv_norm, wq, wkv, wo, tables):
    d = x.shape[1]
    h = _rowwise("mla_norm", lambda xv, gv: (_rms_fwd(xv, gv),), [x], [g], [(d, BF16)])[0]
    lat = _mm("mla_dkv", h, wdkv)[0]

    def lat_post(lv, cs, s1, s2, qg, kg):
        return (_rms_fwd(lv[:, :Q_RANK], qg), _rms_fwd(lv[:, Q_RANK:Q_RANK + KV_RANK], kg),
                _rope(lv[:, Q_RANK + KV_RANK:], cs, s1, s2))

    cqn, ckvn, krb = _rowwise("mla_lat", lat_post, [lat, *tables], [q_norm, kv_norm],
                              [(Q_RANK, BF16), (KV_RANK, BF16), (LANE, BF16)])
    q = _mm("mla_uq", cqn, wq)[0]
    kvb = _mm("mla_ukv", ckvn, wkv, (BF16,))[0]
    qb = _q_rope("mla_q_rope", q, tables, False)
    ob, lse = _flash_fwd(qb, kvb, krb)
    x_mid = _mm("mla_o", ob, wo, (F32,), lambda acc, res: (acc + res,), [x])[0]
    return x_mid, (x, h, lat, cqn, ckvn, krb, qb, kvb, ob, lse)


def _mla_bwd(dx, dxb, saved, g, wdkv, q_norm, kv_norm, wq, wkv, wo, tables):
    x, h, lat, cqn, ckvn, krb, qb, kvb, ob, lse = saved
    s_len = x.shape[0]
    do = _mm("mla_o_dx", dxb, wo, nt=True)[0]
    g_wo = _mm_tn("mla_o_dw", ob, dxb)[0]
    dob, delta = _att_delta(do, ob)
    dq = _flash_bwd_dq(qb, kvb, krb, dob, lse, delta)
    dkvb, dkr = _flash_bwd_dkv(qb, kvb, krb, dob, lse.reshape(HEADS, 1, s_len), delta.reshape(HEADS, 1, s_len))
    dqb = _q_rope("mla_q_rope_bwd", dq, tables, True)
    dcqn = _mm("mla_uq_dx", dqb, wq, nt=True)[0]
    g_wq = _mm_tn("mla_uq_dw", cqn, dqb)[0]
    dckvn = _mm("mla_ukv_dx", dkvb, wkv, nt=True)[0]
    g_wkv = _mm_tn("mla_ukv_dw", ckvn, dkvb)[0]

    def lat_bwd(dq_v, dkv_v, dkr_v, lv, cs, s1, s2, qg, kg):
        dcq, dqg = _rms_bwd(dq_v, lv[:, :Q_RANK], qg)
        dckv, dkg = _rms_bwd(dkv_v, lv[:, Q_RANK:Q_RANK + KV_RANK], kg)
        dkr_sum = dkr_v[:, :LANE]
        for hd in range(1, HEADS):
            dkr_sum = dkr_sum + dkr_v[:, hd * LANE:(hd + 1) * LANE]
        return jnp.concatenate([dcq, dckv, _rope_t(dkr_sum, cs, s1, s2)], axis=1), dqg, dkg

    dlat, g_qn, g_kvn = _rowwise("mla_lat_bwd", lat_bwd, [dcqn, dckvn, dkr, lat, *tables], [q_norm, kv_norm],
                                 [(LAT_PAD, BF16)], [q_norm.shape, kv_norm.shape])
    dh = _mm("mla_dkv_dx", dlat, wdkv, nt=True)[0]
    g_wdkv = _mm_tn("mla_dkv_dw", h, dlat)[0]
    dx, dxb, dg = _norm_bwd("mla_norm_bwd", dh, x, g, dx)
    return dx, dxb, dg, g_wdkv, g_qn, g_kvn, g_wq, g_wkv, g_wo


def _sgu_layer_fwd(x, g, w_in3, ln_g, ln_b, w_sp, bias_full, w_out3):
    h = _rowwise("sgu_norm", lambda xv, gv: (_rms_fwd(xv, gv),), [x], [g], [(x.shape[1], BF16)])[0]
    zpre = _mm_stacked("sgu_in", h, w_in3, "col")[0]
    uv = _sgu_fwd(zpre, ln_g, ln_b, w_sp, bias_full)
    x_mid = _mm_stacked("sgu_out", uv, w_out3, "row", (F32,), lambda acc, res: (acc + res,), [x])[0]
    return x_mid, (x, h, zpre, uv)


def _sgu_layer_bwd(dx, dxb, saved, g, w_in3, ln_g, ln_b, w_sp, bias_full, w_out3):
    x, h, zpre, uv = saved
    duv = _mm_stacked_nt("sgu_out_dx", dxb, w_out3, "row")[0]
    g_out = _mm_tn_stacked("sgu_out_dw", uv, dxb, w_out3.shape, "row")
    dz, g_lng, g_lnb, g_wsp, g_bias = _sgu_bwd(zpre, duv, ln_g, ln_b, w_sp, bias_full)
    dh = _mm_stacked_nt("sgu_in_dx", dz, w_in3, "col")[0]
    g_in = _mm_tn_stacked("sgu_in_dw", h, dz, w_in3.shape, "col")
    dx, dxb, dg = _norm_bwd("sgu_norm_bwd", dh, x, g, dx)
    return dx, dxb, dg, g_in, g_out, g_lng, g_lnb, g_wsp, g_bias


def _loss_head(x, target, g):
    d = x.shape[1]

    def fn(xv, tv, gv):
        err = _rms_fwd(xv, gv) - tv
        dxv, dg = _rms_bwd(err * (1.0 / d), xv, gv)
        return dxv, dxv, dg, jnp.sum(err * err, axis=0, keepdims=True)

    return _rowwise("loss_head", fn, [x, target], [g], [(d, F32), (d, BF16)], [g.shape, g.shape])


def _mixer_weights(i, stacks):
    by_rows = lambda a: a.reshape(N_SHARDS * a.shape[1], a.shape[2])
    by_cols = lambda a: a.transpose(1, 0, 2).reshape(a.shape[1], N_SHARDS * a.shape[2])
    if i % 2:
        w_in3, w_out3, ln_g, ln_b = stacks
        return w_in3, ln_g.reshape(1, -1), ln_b.reshape(1, -1), w_out3
    wdkv = by_rows(stacks[0])
    wdkv = jnp.pad(wdkv, ((0, 0), (0, LAT_PAD - wdkv.shape[1])))
    wq = jnp.pad(by_cols(stacks[1]).reshape(Q_RANK, HEADS, QK_HEAD), ((0, 0), (0, 0), (0, HEAD_PAD - QK_HEAD)))
    return wdkv, wq.reshape(Q_RANK, HEADS * HEAD_PAD), by_cols(stacks[2]), by_rows(stacks[3])


def _local_step(x, positions, target, norm_mix, norm_ffn, final_norm, q_norm, kv_norm, w_sp, b_sp, mixers, ffn, reducer):
    tables = _rope_tables(positions)
    gd = mixers[1][2].size // SGU_GROUPS
    bias_full = [jnp.repeat(b_sp[j].T, gd, axis=1) for j in range(DEPTH // 2)]
    saved, mla, sgu = [], [None] * (DEPTH // 2), [None] * (DEPTH // 2)
    for i in range(DEPTH):
        j = i // 2
        x, *stacks = lax.optimization_barrier((x, *mixers[i]))
        if i % 2 == 0:
            wdkv, wq, wkv, wo = mla[j] = _mixer_weights(i, stacks)
            x, s_mix = _mla_fwd(x, norm_mix[i:i + 1], wdkv, q_norm[j:j + 1], kv_norm[j:j + 1], wq, wkv, wo, tables)
        else:
            w_in3, ln_g, ln_b, w_out3 = sgu[j] = _mixer_weights(i, stacks)
            x, s_mix = _sgu_layer_fwd(x, norm_mix[i:i + 1], w_in3, ln_g, ln_b, w_sp[j], bias_full[j], w_out3)
        x, s_ffn = _ffn_fwd(x, norm_ffn[i:i + 1], *ffn[i])
        saved.append((s_mix, s_ffn))
    dx, dxb, g_final, sq_cols = _loss_head(x, target, final_norm[None, :])
    loss = 0.5 * jnp.sum(sq_cols) / x.shape[1]

    def pair(g):
        return g, g.astype(BF16)

    g_mix, g_ffn = [None] * DEPTH, [None] * DEPTH
    mla_g, sgu_g = [None] * (DEPTH // 2), [None] * (DEPTH // 2)
    for i in reversed(range(DEPTH)):
        j = i // 2
        s_mix, s_ffn = saved[i]
        dx, dxb, g_ffn[i], g_up, g_down = _ffn_bwd(dx, dxb, s_ffn, norm_ffn[i:i + 1], *ffn[i])
        dxb = reducer.add(f"ffn{i}", i, {"ffn_w_up": g_up, "ffn_w_down": g_down}, dxb)
        dxb = reducer.phase_end(dxb)
        if i % 2 == 0:
            wdkv, wq, wkv, wo = mla[j]
            dx, dxb, g_mix[i], g_wdkv, g_qn, g_kvn, g_wq, g_wkv, g_wo = _mla_bwd(
                dx, dxb, s_mix, norm_mix[i:i + 1], wdkv, q_norm[j:j + 1], kv_norm[j:j + 1], wq, wkv, wo, tables)
            mla_g[j] = (g_qn, g_kvn)
            g_wq = g_wq.reshape(Q_RANK, HEADS, HEAD_PAD)[..., :QK_HEAD].reshape(Q_RANK, N_SHARDS, -1)
            dxb = reducer.add(f"mla{j}", j, {
                "mla_w_dkv": pair(g_wdkv[:, :Q_RANK + KV_RANK + ROPE].reshape(N_SHARDS, -1, Q_RANK + KV_RANK + ROPE)),
                "mla_w_uq": pair(g_wq.transpose(1, 0, 2)),
                "mla_w_ukv": pair(g_wkv.reshape(KV_RANK, N_SHARDS, -1).transpose(1, 0, 2)),
                "mla_w_o": pair(g_wo.reshape(N_SHARDS, -1, g_wo.shape[1]))}, dxb)
        else:
            w_in3, ln_g, ln_b, w_out3 = sgu[j]
            dx, dxb, g_mix[i], g_in, g_out, g_lng, g_lnb, g_wsp, g_bias = _sgu_layer_bwd(
                dx, dxb, s_mix, norm_mix[i:i + 1], w_in3, ln_g, ln_b, w_sp[j], bias_full[j], w_out3)
            sgu_g[j] = (g_wsp, g_bias.reshape(SGU_CHUNK, SGU_GROUPS, gd).sum(axis=-1).T)
            dxb = reducer.add(f"sgu{j}", j, {"sgu_w_in": g_in, "sgu_w_out": g_out,
                                             "sgu_ln_g": pair(g_lng.reshape(N_SHARDS, -1, LANE)),
                                             "sgu_ln_b": pair(g_lnb.reshape(N_SHARDS, -1, LANE))}, dxb)
        dxb = reducer.phase_end(dxb)
    small = dict(
        norm_mix=jnp.concatenate(g_mix, axis=0), norm_ffn=jnp.concatenate(g_ffn, axis=0), final_norm=g_final[0],
        q_norm=jnp.concatenate([m[0] for m in mla_g], axis=0), kv_norm=jnp.concatenate([m[1] for m in mla_g], axis=0),
        w_sp=jnp.stack([s[0] for s in sgu_g]), b_sp=jnp.stack([s[1] for s in sgu_g]))
    return loss, dx, small


HBM_SPEC = pl.BlockSpec(memory_space=pltpu.HBM)


def _place():
    x, y, c = lax.axis_index("x"), lax.axis_index("y"), lax.axis_index("c")
    return x, y, c, [(1 - x, y), (x, 1 - y), (1 - x, 1 - y)]


def _remote(src, dst, send_sems, recv_sems, k, to):
    return pltpu.make_async_remote_copy(src_ref=src, dst_ref=dst, send_sem=send_sems.at[k], recv_sem=recv_sems.at[k],
                                        device_id=to, device_id_type=MESH)


def _gather_layer(tag, shards):
    n = len(shards)
    split = [s.shape[0] >= 16 for s in shards]

    def body(*refs):
        ins, outs = refs[:n], refs[n:2 * n]
        send_sems, recv_sems, local_sems = refs[2 * n:]
        x, y, c, chips = _place()
        mine = 2 * x + y
        barrier = pltpu.get_barrier_semaphore()
        peers = [(x, y, 1 - c)] + [(*chip, c) for chip in chips]
        for peer in peers:
            pl.semaphore_signal(barrier, inc=1, device_id=peer, device_id_type=MESH)
        pl.semaphore_wait(barrier, len(peers))

        def rows(t, half):
            hr = shards[t].shape[0] // 2
            return pl.ds(half * hr, hr) if split[t] else pl.ds(0, shards[t].shape[0])

        local, sent = [], []
        for t in range(n):
            local.append(pltpu.make_async_copy(ins[t], outs[t].at[mine], local_sems.at[t]))
            local[-1].start()
            for j, chip in enumerate(chips):
                cp = _remote(ins[t].at[rows(t, c)], outs[t].at[mine, rows(t, c)], send_sems, recv_sems, 3 * t + j, (*chip, c))
                cp.start()
                sent.append(cp)
        for j, chip in enumerate(chips):
            theirs = 2 * chip[0] + chip[1]
            for t in range(n):
                piece = outs[t].at[theirs, rows(t, c)]
                _remote(piece, piece, send_sems, recv_sems, 3 * t + j, (x, y, c)).wait_recv()
                if split[t]:
                    cp = _remote(piece, piece, send_sems, recv_sems, 3 * n + 3 * t + j, (x, y, 1 - c))
                    cp.start()
                    sent.append(cp)
        for j, chip in enumerate(chips):
            theirs = 2 * chip[0] + chip[1]
            for t in range(n):
                if split[t]:
                    piece = outs[t].at[theirs, rows(t, 1 - c)]
                    _remote(piece, piece, send_sems, recv_sems, 3 * n + 3 * t + j, (x, y, c)).wait_recv()
        for cp in sent:
            cp.wait_send()
        for cp in local:
            cp.wait()

    return pl.kernel(
        body, name=f"gather_{tag}", mesh=plsc.ScalarSubcoreMesh(axis_name="sequencer", num_cores=1),
        out_type=[jax.ShapeDtypeStruct((N_SHARDS, *s.shape), s.dtype) for s in shards],
        scratch_types=[pltpu.SemaphoreType.DMA((6 * n,)), pltpu.SemaphoreType.DMA((6 * n,)), pltpu.SemaphoreType.DMA((n,))],
        compiler_params=pltpu.CompilerParams(collective_id=ID_GATHER),
    )(*shards)


SEQUENCER = dict(axis_name="sequencer", num_cores=1)
ID_GATHER, ID_EXCHANGE, ID_SHARE = 0, 1, 2
MIN_SPLIT_ROWS = 16


def _handshake(peers):
    barrier = pltpu.get_barrier_semaphore()
    for peer in peers:
        pl.semaphore_signal(barrier, inc=1, device_id=peer, device_id_type=MESH)
    pl.semaphore_wait(barrier, len(peers))


def _half_rows(rows, half):
    return pl.ds(half * (rows // 2), rows // 2) if rows >= MIN_SPLIT_ROWS else pl.ds(0, rows)


SEM_SPEC = pl.BlockSpec(memory_space=pltpu.SEMAPHORE)
DATAFLOW = pltpu.SideEffectType.DATAFLOW_SIDE_EFFECTING


def _exchange_copies(shapes, stacks, lands, send_sems, recv_sems):
    x, y, c, chips = _place()
    mine = 2 * x + y
    copies = []
    for t, shape in enumerate(shapes):
        r = shape[1]
        copies.append(_remote(stacks[t].at[mine, _half_rows(r, 1 - c)], lands[t].at[0], send_sems, recv_sems, 7 * t, (x, y, 1 - c)))
        for j, chip in enumerate(chips):
            theirs = 2 * chip[0] + chip[1]
            copies.append(_remote(stacks[t].at[theirs, _half_rows(r, c)], lands[t].at[1 + j], send_sems, recv_sems,
                                  7 * t + 1 + j, (*chip, c)))
            copies.append(_remote(stacks[t].at[theirs, _half_rows(r, 1 - c)], lands[t].at[4 + j], send_sems, recv_sems,
                                  7 * t + 4 + j, (*chip, 1 - c)))
    return copies


def _exchange_start(tag, stacks, carry):
    n = len(stacks)
    shapes = [s.shape for s in stacks]
    lands = [lax.empty((7, s.shape[1] // 2 if s.shape[1] >= MIN_SPLIT_ROWS else s.shape[1], s.shape[2]), s.dtype) for s in stacks]

    def body(*refs):
        for cp in _exchange_copies(shapes, refs[:n], refs[n:2 * n], refs[2 * n + 1], refs[2 * n + 2]):
            cp.start()

    through = (*stacks, *lands, carry)
    out = pl.pallas_call(
        body, name=f"reduce_exchange_start_{tag}",
        out_shape=(pltpu.SemaphoreType.DMA((7 * n,)), pltpu.SemaphoreType.DMA((7 * n,)),
                   *[pltpu.HBM(a.shape, a.dtype) for a in through]),
        in_specs=[HBM_SPEC] * (2 * n + 1),
        out_specs=(SEM_SPEC, SEM_SPEC, *[HBM_SPEC] * (2 * n + 1)),
        input_output_aliases={t: 2 + t for t in range(2 * n + 1)},
        compiler_params=pltpu.CompilerParams(has_side_effects=DATAFLOW),
    )(*[pltpu.with_memory_space_constraint(a, pltpu.HBM) for a in through])
    return out[0], out[1], out[2:2 + n], out[2 + n:2 + 2 * n], out[-1]


def _exchange_wait(tag, send_sems, recv_sems, stacks, lands, after):
    n = len(stacks)
    shapes = [s.shape for s in stacks]

    def body(*refs):
        for cp in _exchange_copies(shapes, refs[:n], refs[n:2 * n], refs[2 * n], refs[2 * n + 1]):
            cp.wait()

    out = pl.pallas_call(
        body, name=f"reduce_exchange_wait_{tag}",
        out_shape=tuple(pltpu.HBM(a.shape, a.dtype) for a in (*stacks, *lands)),
        in_specs=[HBM_SPEC] * (2 * n) + [SEM_SPEC, SEM_SPEC, pl.BlockSpec(memory_space=pl.ANY)],
        out_specs=tuple([HBM_SPEC] * (2 * n)),
        input_output_aliases={t: t for t in range(2 * n)},
        compiler_params=pltpu.CompilerParams(has_side_effects=DATAFLOW),
    )(*stacks, *lands, send_sems, recv_sems, after)
    return out[:n], out[n:]


def _share_halves(tag, halves):
    n = len(halves)

    def body(*refs):
        ins, outs, send_sems, recv_sems = refs[:n], refs[n:2 * n], refs[2 * n], refs[2 * n + 1]
        x, y, c, _ = _place()
        _handshake([(x, y, 1 - c)])
        sent = [_remote(ins[t], outs[t], send_sems, recv_sems, t, (x, y, 1 - c)) for t in range(n)]
        for cp in sent:
            cp.start()
        for cp in sent:
            cp.wait()

    return pl.kernel(
        body, name=f"reduce_share_{tag}", mesh=plsc.ScalarSubcoreMesh(**SEQUENCER),
        out_type=[jax.ShapeDtypeStruct(h.shape, h.dtype) for h in halves],
        scratch_types=[pltpu.SemaphoreType.DMA((n,)), pltpu.SemaphoreType.DMA((n,))],
        compiler_params=pltpu.CompilerParams(collective_id=ID_SHARE),
    )(*halves)


def _all_reduce_small(part):
    rows = part.shape[0]

    def body(p_ref, out_ref, sib_buf, chip_sums, send_sems, recv_sems):
        x, y, c, chips = _place()
        mine = 2 * x + y
        swap = _remote(p_ref, sib_buf, send_sems, recv_sems, 0, (x, y, 1 - c))
        swap.start()
        swap.wait()
        chip_sums[mine] = p_ref[...] + sib_buf[...]
        sent = [_remote(chip_sums.at[mine], chip_sums.at[mine], send_sems, recv_sems, 1 + j, (*chip, c))
                for j, chip in enumerate(chips)]
        for cp in sent:
            cp.start()
        for j, chip in enumerate(chips):
            sent[j].wait_send()
            theirs = chip_sums.at[2 * chip[0] + chip[1]]
            _remote(theirs, theirs, send_sems, recv_sems, 1 + j, (x, y, c)).wait_recv()
        out_ref[...] = ((chip_sums[0] + chip_sums[1]) + chip_sums[2]) + chip_sums[3]

    vmem = pl.BlockSpec(memory_space=pltpu.VMEM)
    return pl.pallas_call(
        body, name="all_reduce_small", in_specs=[vmem], out_specs=vmem, out_shape=jax.ShapeDtypeStruct(part.shape, F32),
        scratch_shapes=[pltpu.VMEM((rows, LANE), F32), pltpu.VMEM((N_SHARDS, rows, LANE), F32),
                        pltpu.SemaphoreType.DMA((4,)), pltpu.SemaphoreType.DMA((4,))],
        compiler_params=pltpu.CompilerParams(vmem_limit_bytes=VMEM_LIMIT_BYTES),
    )(part)


def _sum_partials(g3, others, sel):
    _, rows, c = others.shape
    whole = g3.shape[1] == rows
    tr = _tile(rows, 512)
    nb = rows // tr

    def body(sel_ref, g_ref, *rest):
        same = g_ref[...].astype(F32)
        for ref in rest[1:4]:
            same = same + ref[...].astype(F32)
        other = rest[0][...].astype(F32)
        for ref in rest[4:7]:
            other = other + ref[...].astype(F32)
        rest[7][...] = same + other

    blk = (None, tr, c)
    slots = [pl.BlockSpec(blk, functools.partial(lambda i, sr, k: (k, i, 0), k=k)) for k in range(7)]
    return pl.pallas_call(
        body, name="reduce_sum_partials",
        grid_spec=pltpu.PrefetchScalarGridSpec(
            num_scalar_prefetch=1, grid=(nb,),
            in_specs=[pl.BlockSpec(blk, lambda i, sr: (sr[0], (0 if whole else sr[1] * nb) + i, 0))] + slots,
            out_specs=pl.BlockSpec((tr, c), lambda i, sr: (i, 0))),
        out_shape=jax.ShapeDtypeStruct((rows, c), F32),
        compiler_params=_params(("parallel",)),
    )(sel, g3, *[others] * 7)


def _adamw_math(w, g, m, v):
    nm = ADAM_B1 * m + (1.0 - ADAM_B1) * g
    nv = ADAM_B2 * v + (1.0 - ADAM_B2) * (g * g)
    m_hat = nm / (1.0 - ADAM_B1 ** ADAM_STEP)
    v_hat = nv / (1.0 - ADAM_B2 ** ADAM_STEP)
    return -ADAM_LR * (m_hat / (jnp.sqrt(v_hat) + ADAM_EPS) + ADAM_WD * w), nm, nv


def _adamw_layer(layer, w, m, v, g_mine, g_sibling, sel, prev):
    lyr, r, c = w.shape
    rows = g_mine.shape[0]
    halves = r // rows
    tr = _tile(rows, 512)
    nb = rows // tr
    n_g = 1 if g_sibling is None else 2

    def body(sel_ref, w_ref, m_ref, v_ref, *rest):
        g = rest[0][...]
        if n_g == 2:
            g = jnp.where(pl.program_id(0) == sel_ref[1], g, rest[1][...])
        outs = rest[n_g + (0 if prev is None else 4):]
        d, nm, nv = _adamw_math(w_ref[...], g, m_ref[...], v_ref[...])
        for ref, val in zip(outs, (g, d, nm, nv)):
            ref[...] = val

    full = pl.BlockSpec((None, tr, c), lambda h, i, sr: (layer, h * nb + i, 0))
    part = pl.BlockSpec((tr, c), lambda h, i, sr: (i, 0))
    n_in = 4 + n_g
    return pl.pallas_call(
        body, name="adamw_layer",
        grid_spec=pltpu.PrefetchScalarGridSpec(
            num_scalar_prefetch=1, grid=(halves, nb),
            in_specs=[full] * 3 + [part] * n_g + ([] if prev is None else [pl.BlockSpec(memory_space=pl.ANY)] * 4),
            out_specs=[full] * 4),
        out_shape=[jax.ShapeDtypeStruct(w.shape, F32)] * 4,
        input_output_aliases={} if prev is None else {n_in + k: k for k in range(4)},
        compiler_params=_params(("parallel", "parallel")),
    )(sel, w, m, v, g_mine, *([] if g_sibling is None else [g_sibling]), *([] if prev is None else prev))


def _adamw(w, g, m, v):
    lyr, r, c = w.shape
    tr = _tile(r, 512)

    def body(w_ref, g_ref, m_ref, v_ref, d_ref, nm_ref, nv_ref):
        gv = g_ref[...]
        nm = ADAM_B1 * m_ref[...] + (1.0 - ADAM_B1) * gv
        nv = ADAM_B2 * v_ref[...] + (1.0 - ADAM_B2) * (gv * gv)
        m_hat = nm / (1.0 - ADAM_B1 ** ADAM_STEP)
        v_hat = nv / (1.0 - ADAM_B2 ** ADAM_STEP)
        d_ref[...] = -ADAM_LR * (m_hat / (jnp.sqrt(v_hat) + ADAM_EPS) + ADAM_WD * w_ref[...])
        nm_ref[...] = nm
        nv_ref[...] = nv

    blk = pl.BlockSpec((None, tr, c), lambda l, i: (l, i, 0))
    return pl.pallas_call(
        body, name="adamw", grid=(lyr, r // tr), in_specs=[blk] * 4, out_specs=[blk] * 3,
        out_shape=[jax.ShapeDtypeStruct(w.shape, F32)] * 3,
        compiler_params=_params(("parallel", "parallel")),
    )(w, g, m, v)


SHARDED = ("mla_w_dkv", "mla_w_uq", "mla_w_ukv", "mla_w_o", "sgu_w_in", "sgu_ln_g", "sgu_ln_b", "sgu_w_out",
           "ffn_w_up", "ffn_w_down")
REPLICATED = ("norm_mix", "norm_ffn", "final_norm", "mla_q_norm", "mla_kv_norm", "sgu_w_spatial", "sgu_b_spatial")
WEIGHTS = ("norm_mix", "norm_ffn", "final_norm", "mla_w_dkv", "mla_q_norm", "mla_kv_norm", "mla_w_uq", "mla_w_ukv",
           "mla_w_o", "sgu_w_in", "sgu_ln_g", "sgu_ln_b", "sgu_w_spatial", "sgu_b_spatial", "sgu_w_out", "ffn_w_up",
           "ffn_w_down")


class _Reducer:
    def __init__(self, state, sel):
        self.state, self.sel = state, sel
        self.started, self.travelling, self.summed = [], [], []
        self.done = {}

    def add(self, tag, layer, grads, token):
        names = list(grads)
        token, *tied = lax.optimization_barrier((token, *[a for n in names for a in grads[n]]))
        f32s, bf16s = tied[0::2], tied[1::2]
        *flying, token = _exchange_start(tag, bf16s, token)
        self.started.append((tag, layer, names, f32s, flying))
        return token

    def phase_end(self, token):
        for tag, layer, names, f32s, flying in self.travelling:
            bf16s, received = _exchange_wait(tag, *flying, token)
            own = [g if g.shape[1] >= MIN_SPLIT_ROWS else gb for g, gb in zip(f32s, bf16s)]
            mine = [_sum_partials(g, got, self.sel) for g, got in zip(own, received)]
            token, *mine = lax.optimization_barrier((token, *mine))
            cut = [k for k, g in enumerate(own) if g.shape[1] >= MIN_SPLIT_ROWS]
            theirs = dict(zip(cut, _share_halves(tag, [mine[k] for k in cut])))
            self.summed.append((layer, names, mine, [theirs.get(k) for k in range(len(names))]))
        self.travelling, self.started = self.started, []
        return token

    def update(self, token):
        for layer, names, mine, theirs in self.summed:
            for name, g_mine, g_theirs in zip(names, mine, theirs):
                w, m, v = self.state[name]
                self.done[name] = _adamw_layer(layer, w, m, v, g_mine, g_theirs, self.sel, self.done.get(name))
                token = self.done[name][1]
        self.summed = []
        return token


def _as3d(name, a):
    return a.reshape(a.shape[0], -1, LANE) if name in ("sgu_ln_g", "sgu_ln_b") else a


def _pack(parts):
    flat = jnp.concatenate([p.reshape(-1) for p in parts])
    rows = -(-flat.shape[0] // (8 * LANE)) * 8
    return jnp.pad(flat, (0, rows * LANE - flat.shape[0])).reshape(rows, LANE)


def _unpack(packed, like):
    flat, out, at = packed.reshape(-1), [], 0
    for p in like:
        out.append(flat[at:at + p.size].reshape(p.shape))
        at += p.size
    return out


def kernel(x, positions, norm_mix, norm_ffn, final_norm, mla_w_dkv, mla_q_norm, mla_kv_norm, mla_w_uq, mla_w_ukv, mla_w_o, sgu_w_in, sgu_ln_g, sgu_ln_b, sgu_w_spatial, sgu_b_spatial, sgu_w_out, ffn_w_up, ffn_w_down, loss_target, m_norm_mix, m_norm_ffn, m_final_norm, m_mla_w_dkv, m_mla_q_norm, m_mla_kv_norm, m_mla_w_uq, m_mla_w_ukv, m_mla_w_o, m_sgu_w_in, m_sgu_ln_g, m_sgu_ln_b, m_sgu_w_spatial, m_sgu_b_spatial, m_sgu_w_out, m_ffn_w_up, m_ffn_w_down, v_norm_mix, v_norm_ffn, v_final_norm, v_mla_w_dkv, v_mla_q_norm, v_mla_kv_norm, v_mla_w_uq, v_mla_w_ukv, v_mla_w_o, v_sgu_w_in, v_sgu_ln_g, v_sgu_ln_b, v_sgu_w_spatial, v_sgu_b_spatial, v_sgu_w_out, v_ffn_w_up, v_ffn_w_down):
    given = dict(locals())
    w = {n: given[n] for n in WEIGHTS}
    mom = {n: given["m_" + n] for n in WEIGHTS}
    var = {n: given["v_" + n] for n in WEIGHTS}
    mixers, ffn, token = [], [], None
    for i in range(DEPTH):
        j = i // 2
        if i % 2 == 0:
            mixer = [w[n][j].astype(BF16) for n in ("mla_w_dkv", "mla_w_uq", "mla_w_ukv", "mla_w_o")]
        else:
            mixer = [sgu_w_in[j].astype(BF16), sgu_w_out[j].astype(BF16), sgu_ln_g[j].reshape(-1, LANE),
                     sgu_ln_b[j].reshape(-1, LANE)]
        for tag, shards, into in ((f"mixer{i}", mixer, mixers), (f"ffn{i}", [ffn_w_up[i].astype(BF16), ffn_w_down[i].astype(BF16)], ffn)):
            if token is None:
                token = shards[0]
            else:
                token, *shards = lax.optimization_barrier((token, *shards))
            into.append(_gather_layer(tag, shards))

    x_i, y_i, c_i = lax.axis_index("x"), lax.axis_index("y"), lax.axis_index("c")
    sel = jnp.stack([2 * x_i + y_i, c_i]).astype(jnp.int32)
    reducer = _Reducer({n: tuple(_as3d(n, d[n]) for d in (w, mom, var)) for n in SHARDED}, sel)
    loss, dx, small = _local_step(
        x[0], positions[0], loss_target[0], norm_mix, norm_ffn, final_norm, mla_q_norm, mla_kv_norm, sgu_w_spatial,
        sgu_b_spatial, mixers, ffn, reducer)
    loss = lax.psum(loss, ("x", "y", "c"))

    small_g = [small["norm_mix"], small["norm_ffn"], small["final_norm"], small["q_norm"], small["kv_norm"],
               small["w_sp"], small["b_sp"]]
    like = [w[n] for n in REPLICATED]
    g_small = _all_reduce_small(_pack(small_g))
    packed = [_pack([d[n] for n in REPLICATED])[None] for d in (w, mom, var)]
    upd_small = _adamw(packed[0], g_small[None], packed[1], packed[2])
    grads = dict(zip(REPLICATED, _unpack(g_small, like)))
    delta, new_m, new_v = ({n: a for n, a in zip(REPLICATED, _unpack(u[0], like))} for u in upd_small)

    reducer.phase_end(reducer.update(upd_small[0]))
    reducer.update(None)
    for n in SHARDED:
        grads[n], delta[n], new_m[n], new_v[n] = (a.reshape(w[n].shape) for a in reducer.done[n])

    return (loss, dx[None], *[grads[n] for n in WEIGHTS], *[delta[n] for n in WEIGHTS],
            *[new_m[n] for n in WEIGHTS], *[new_v[n] for n in WEIGHTS])
```

```python
import functools
import math

import jax
import jax.numpy as jnp
from jax import lax
from jax.experimental import pallas as pl
from jax.experimental.pallas import tpu as pltpu
from jax.experimental.pallas import tpu_sc as plsc

F32 = jnp.float32
BF16 = jnp.bfloat16
MESH = pl.DeviceIdType.MESH

DEPTH = 4
HEADS = 8
NOPE = 128
ROPE = 64
VHEAD = 128
QK_HEAD = NOPE + ROPE
Q_RANK = 256
KV_RANK = 128
HEAD_PAD = 256
LAT_PAD = 512
ROPE_THETA = 10000.0
SGU_CHUNK = 128
SGU_GROUPS = 8
NORM_EPS = 1e-6
LN_EPS = 1e-5
ADAM_LR, ADAM_B1, ADAM_B2, ADAM_EPS, ADAM_WD, ADAM_STEP = 0.001, 0.9, 0.999, 1e-08, 0.01, 10

N_SHARDS = 4
LANE = 128
VMEM_LIMIT_BYTES = 56 * 1024 * 1024
ATT_TILE = 512
MM_TILE = 1024
ATT_SCALE = QK_HEAD ** -0.5
LOG2_SCALE = ATT_SCALE * math.log2(math.e)

NN = (((1,), (0,)), ((), ()))
NT = (((1,), (1,)), ((), ()))
TN = (((0,), (0,)), ((), ()))


def _params(sem):
    return pltpu.CompilerParams(dimension_semantics=sem, vmem_limit_bytes=VMEM_LIMIT_BYTES)


def _tile(n, pref):
    t = min(n, pref)
    while n % t:
        t //= 2
    return t


def _matmul(name, a, b, a_spec, b_spec, dims, grid, tile, outs, extras=(), epilogue=None, sums=(), strips=1):
    nk, ne, no = grid[2], len(extras), len(outs)
    b_specs = list(b_spec) if isinstance(b_spec, (list, tuple)) else [b_spec]
    nb = len(b_specs)
    strips = strips if nk == 1 else 1
    width = tile[1] // strips

    def body(a_ref, *rest):
        b_refs, e_refs, o_refs = rest[:nb], rest[nb:nb + ne], rest[nb + ne:nb + ne + no]
        s_refs = rest[nb + ne + no:nb + ne + no + len(sums)]
        kw = a_ref.shape[1] // nb

        def product(cols):
            part = None
            for p, b_ref in enumerate(b_refs):
                a_tile = a_ref[...] if nb == 1 else a_ref[:, p * kw:(p + 1) * kw]
                b_tile = b_ref[...] if strips == 1 else (b_ref[cols, :] if dims == NT else b_ref[:, cols])
                d = lax.dot_general(a_tile.astype(BF16), b_tile.astype(BF16), dims, preferred_element_type=F32)
                part = d if part is None else part + d
            return part

        def finish(acc, cols):
            at = (lambda ref: ref[...]) if strips == 1 else (lambda ref: ref[:, cols])
            vals = (acc,) if epilogue is None else epilogue(acc, *[at(e) for e in e_refs])
            for o_ref, v in zip(o_refs, vals):
                if strips == 1:
                    o_ref[...] = v.astype(o_ref.dtype)
                else:
                    o_ref[:, cols] = v.astype(o_ref.dtype)
            first = pl.program_id(0) == 0
            for s_ref, v in zip(s_refs, vals[no:]):
                @pl.when(first)
                def _():
                    s_ref[...] = v

                @pl.when(jnp.logical_not(first))
                def _():
                    s_ref[...] += v

        if nk == 1:
            for strip in range(strips):
                cols = slice(strip * width, (strip + 1) * width)
                finish(product(cols), cols)
            return
        part = product(None)
        acc_ref, k = rest[-1], pl.program_id(2)

        @pl.when(k == 0)
        def _():
            acc_ref[...] = part

        @pl.when(jnp.logical_and(k > 0, k < nk - 1))
        def _():
            acc_ref[...] += part

        @pl.when(k == nk - 1)
        def _():
            finish(acc_ref[...] + part, None)

    assert not sums or (grid[1] == 1 and nk == 1)
    return pl.pallas_call(
        body, name=name, grid=grid,
        in_specs=[a_spec] + b_specs + [s for _, s in extras],
        out_specs=[s for _, s in outs] + [pl.BlockSpec(s, lambda i, j, k: (0,) * len(s)) for s in sums],
        out_shape=[s for s, _ in outs] + [jax.ShapeDtypeStruct(s, F32) for s in sums],
        scratch_shapes=[pltpu.VMEM(tile, F32)] if nk > 1 else [],
        compiler_params=_params(("arbitrary" if sums else "parallel", "parallel", "arbitrary")),
    )(a, *[b] * nb, *[e for e, _ in extras])


def _epilogue_operands(extras, consts, o_spec):
    return [(e, o_spec) for e in extras] + [(c, pl.BlockSpec(c.shape, lambda i, j, k: (0, 0))) for c in consts]


def _mm(name, a, b, out_dtypes=(F32,), epilogue=None, extras=(), tm=MM_TILE, tn=MM_TILE, tk=MM_TILE, nt=False,
        consts=(), sums=(), strips=1):
    m, kd = a.shape
    n = b.shape[0] if nt else b.shape[1]
    tm, tn, tk = _tile(m, tm), _tile(n, tn), _tile(kd, tk)
    o_spec = pl.BlockSpec((tm, tn), lambda i, j, k: (i, j))
    b_spec = pl.BlockSpec((tn, tk), lambda i, j, k: (j, k)) if nt else pl.BlockSpec((tk, tn), lambda i, j, k: (k, j))
    return _matmul(name, a, b, pl.BlockSpec((tm, tk), lambda i, j, k: (i, k)), b_spec, NT if nt else NN,
                   (m // tm, n // tn, kd // tk), (tm, tn),
                   [(jax.ShapeDtypeStruct((m, n), d), o_spec) for d in out_dtypes],
                   _epilogue_operands(extras, consts, o_spec), epilogue, sums, strips)


def _mm_tn(name, a, b, out_dtypes=(F32,), tm=MM_TILE, tn=MM_TILE, tk=MM_TILE):
    s, m = a.shape
    n = b.shape[1]
    tm, tn, tk = _tile(m, tm), _tile(n, tn), _tile(s, tk)
    o_spec = pl.BlockSpec((tm, tn), lambda i, j, k: (i, j))
    return _matmul(name, a, b, pl.BlockSpec((tk, tm), lambda i, j, k: (k, i)),
                   pl.BlockSpec((tk, tn), lambda i, j, k: (k, j)), TN, (m // tm, n // tn, s // tk), (tm, tn),
                   [(jax.ShapeDtypeStruct((m, n), d), o_spec) for d in out_dtypes])


def _mm_stacked(name, a, w3, mode, out_dtypes=(F32,), epilogue=None, extras=(), tm=MM_TILE, tn=MM_TILE, tk=MM_TILE,
                strips=1):
    m, kd = a.shape
    _, r, c = w3.shape
    n = c if mode == "row" else N_SHARDS * c
    if mode == "row":
        tm, tn, tk = _tile(m, tm // 2), _tile(n, tn), kd
        b_spec = [pl.BlockSpec((None, r, tn), functools.partial(lambda i, j, k, p: (p, 0, j), p=p)) for p in range(N_SHARDS)]
    else:
        tm, tn, tk = _tile(m, tm), _tile(c, tn), _tile(kd, tk)
        per = c // tn
        b_spec = pl.BlockSpec((None, tk, tn), lambda i, j, k: (j // per, k, j % per))
    o_spec = pl.BlockSpec((tm, tn), lambda i, j, k: (i, j))
    return _matmul(name, a, w3, pl.BlockSpec((tm, tk), lambda i, j, k: (i, k)), b_spec, NN,
                   (m // tm, n // tn, kd // tk), (tm, tn),
                   [(jax.ShapeDtypeStruct((m, n), d), o_spec) for d in out_dtypes],
                   [(e, o_spec) for e in extras], epilogue, strips=strips)


def _mm_stacked_nt(name, a, w3, mode, out_dtypes=(F32,), epilogue=None, extras=(), tm=MM_TILE, tn=MM_TILE, tk=MM_TILE,
                   consts=(), sums=(), strips=1):
    m, nd = a.shape
    _, r, c = w3.shape
    kout = N_SHARDS * r if mode == "row" else r
    if mode == "row":
        tm, tn, tk = _tile(m, tm), _tile(r, tn), _tile(c, tk)
        per = r // tn
        b_spec = pl.BlockSpec((None, tn, tk), lambda i, j, k: (j // per, j % per, k))
    else:
        tm, tn, tk = _tile(m, tm // 2), _tile(r, tn), nd
        b_spec = [pl.BlockSpec((None, tn, c), functools.partial(lambda i, j, k, p: (p, j, 0), p=p)) for p in range(N_SHARDS)]
    o_spec = pl.BlockSpec((tm, tn), lambda i, j, k: (i, j))
    return _matmul(name, a, w3, pl.BlockSpec((tm, tk), lambda i, j, k: (i, k)), b_spec, NT,
                   (m // tm, kout // tn, nd // tk), (tm, tn),
                   [(jax.ShapeDtypeStruct((m, kout), d), o_spec) for d in out_dtypes],
                   _epilogue_operands(extras, consts, o_spec), epilogue, sums, strips)


def _mm_tn_stacked(name, a, b, shape3, mode, tm=MM_TILE, tn=MM_TILE, tk=MM_TILE):
    s, m = a.shape
    n = b.shape[1]
    _, r, c = shape3
    tk, tn = s, tn // 2
    if mode == "row":
        tm, tn = _tile(r, tm), _tile(n, tn)
        per = r // tm
        o_spec = pl.BlockSpec((None, tm, tn), lambda i, j, k: (i // per, i % per, j))
    else:
        tm, tn = _tile(m, tm), _tile(c, tn)
        per = c // tn
        o_spec = pl.BlockSpec((None, tm, tn), lambda i, j, k: (j // per, i, j % per))
    outs = [(jax.ShapeDtypeStruct(shape3, F32), o_spec), (jax.ShapeDtypeStruct(shape3, BF16), o_spec)]
    return _matmul(name, a, b, pl.BlockSpec((tk, tm), lambda i, j, k: (k, i)),
                   pl.BlockSpec((tk, tn), lambda i, j, k: (k, j)), TN, (m // tm, n // tn, s // tk), (tm, tn),
                   outs, epilogue=lambda acc: (acc, acc))


def _rowwise(name, fn, rows, consts, out_rows, out_accs=(), tr=256):
    nr, nc, no = len(rows), len(consts), len(out_rows)
    n_rows = rows[0].shape[0]
    tr = _tile(n_rows, tr)

    def body(*refs):
        vals = fn(*[r[...] for r in refs[:nr + nc]])
        o_refs, a_refs = refs[nr + nc:nr + nc + no], refs[nr + nc + no:]
        for ref, v in zip(o_refs, vals[:no]):
            ref[...] = v.astype(ref.dtype)
        first = pl.program_id(0) == 0

        @pl.when(first)
        def _():
            for ref, v in zip(a_refs, vals[no:]):
                ref[...] = v

        @pl.when(jnp.logical_not(first))
        def _():
            for ref, v in zip(a_refs, vals[no:]):
                ref[...] += v

    def whole(shape):
        return pl.BlockSpec(shape, lambda i: (0,) * len(shape))

    return pl.pallas_call(
        body, name=name, grid=(n_rows // tr,),
        in_specs=[pl.BlockSpec((tr, a.shape[1]), lambda i: (i, 0)) for a in rows] + [whole(c.shape) for c in consts],
        out_specs=[pl.BlockSpec((tr, f), lambda i: (i, 0)) for f, _ in out_rows] + [whole(s) for s in out_accs],
        out_shape=[jax.ShapeDtypeStruct((n_rows, f), d) for f, d in out_rows]
        + [jax.ShapeDtypeStruct(s, F32) for s in out_accs],
        compiler_params=_params(("arbitrary",)),
    )(*rows, *consts)


def _rms_fwd(x, g):
    return x * lax.rsqrt(jnp.mean(x * x, axis=-1, keepdims=True) + NORM_EPS) * g


def _rms_bwd(dy, x, g):
    rstd = lax.rsqrt(jnp.mean(x * x, axis=-1, keepdims=True) + NORM_EPS)
    n = x * rstd
    dn = dy * g
    dx = rstd * (dn - n * jnp.mean(dn * n, axis=-1, keepdims=True))
    return dx, jnp.sum(dy * n, axis=0, keepdims=True)


def _rope(x, cs, s1, s2):
    return x * cs + pltpu.roll(x, 32, 1) * s1 + pltpu.roll(x, 96, 1) * s2


def _rope_t(dy, cs, s1, s2):
    return dy * cs + pltpu.roll(dy * s1, 96, 1) + pltpu.roll(dy * s2, 32, 1)


def _gelu(z):
    return 0.5 * z * (1.0 + lax.erf(z * (1.0 / math.sqrt(2.0))))


def _gelu_and_grad(z):
    cdf = 0.5 * (1.0 + lax.erf(z * (1.0 / math.sqrt(2.0))))
    return z * cdf, cdf + z * jnp.exp(-0.5 * z * z) * (1.0 / math.sqrt(2.0 * math.pi))


def _att_scores(q, kv, kr, scale, masked, transposed):
    k = jnp.concatenate([kv[:, :NOPE], kr], axis=1)
    if transposed:
        s = lax.dot_general(k, q, NT, preferred_element_type=F32) * scale
    else:
        s = lax.dot_general(q, k, NT, preferred_element_type=F32) * scale
    if masked:
        r = lax.broadcasted_iota(jnp.int32, s.shape, 0)
        c = lax.broadcasted_iota(jnp.int32, s.shape, 1)
        s = jnp.where((r <= c) if transposed else (c <= r), s, -jnp.inf)
    return s, k


def _in_pairs(lo, hi, pair, single):
    n = hi - lo

    def body(p, carry):
        pair(lo + 2 * p, lo + 2 * p + 1)
        return carry

    lax.fori_loop(0, n // 2, body, 0)

    @pl.when(n % 2 == 1)
    def _():
        single(hi - 1)


def _flash_fwd(qb, kvb, krb):
    s_len = qb.shape[0]
    t = ATT_TILE

    def body(q_ref, kv_ref, kr_ref, o_ref, lse_ref, m_s, l_s, acc_s):
        qi = pl.program_id(1)
        m_s[...] = jnp.full_like(m_s, -jnp.inf)
        l_s[...] = jnp.zeros_like(l_s)
        acc_s[...] = jnp.zeros_like(acc_s)
        q = q_ref[...]

        def scores(ki, masked):
            rows = pl.ds(pl.multiple_of(ki * t, t), t)
            kv = kv_ref[rows, :]
            return _att_scores(q, kv, kr_ref[rows, :], LOG2_SCALE, masked, False)[0], kv

        def update(s, kv):
            m_prev = m_s[...]
            m_new = jnp.maximum(m_prev, jnp.max(s, axis=1, keepdims=True))
            alpha = jnp.exp2(m_prev - m_new)
            p = jnp.exp2(s - jnp.tile(m_new, (1, t // LANE)))
            l_s[...] = alpha * l_s[...] + jnp.sum(p, axis=1, keepdims=True)
            acc_s[...] = alpha * acc_s[...] + jnp.dot(p.astype(BF16), kv[:, NOPE:], preferred_element_type=F32)
            m_s[...] = m_new

        def pair(k0, k1):
            first, second = scores(k0, False), scores(k1, False)
            update(*first)
            update(*second)

        _in_pairs(0, qi, pair, lambda ki: update(*scores(ki, False)))
        update(*scores(qi, True))
        o_ref[...] = (acc_s[...] / l_s[...]).astype(o_ref.dtype)
        lse_ref[...] = (m_s[...] + jnp.log2(l_s[...]))[:, :1]

    return pl.pallas_call(
        body, name="flash_fwd", grid=(HEADS, s_len // t),
        in_specs=[pl.BlockSpec((t, HEAD_PAD), lambda h, qi: (qi, h)),
                  pl.BlockSpec((s_len, HEAD_PAD), lambda h, qi: (0, h)),
                  pl.BlockSpec((s_len, LANE), lambda h, qi: (0, 0))],
        out_specs=[pl.BlockSpec((t, VHEAD), lambda h, qi: (qi, h)),
                   pl.BlockSpec((None, t, 1), lambda h, qi: (h, qi, 0))],
        out_shape=[jax.ShapeDtypeStruct((s_len, HEADS * VHEAD), BF16),
                   jax.ShapeDtypeStruct((HEADS, s_len, 1), F32)],
        scratch_shapes=[pltpu.VMEM((t, LANE), F32), pltpu.VMEM((t, LANE), F32), pltpu.VMEM((t, VHEAD), F32)],
        compiler_params=_params(("parallel", "arbitrary")),
    )(qb, kvb, krb)


def _flash_bwd_dq(qb, kvb, krb, dob, lse, delta):
    s_len = qb.shape[0]
    t = ATT_TILE
    nq = s_len // t
    scale = QK_HEAD ** -0.5

    def body(q_ref, kv_ref, kr_ref, do_ref, lse_ref, dl_ref, dq_ref, acc_s):
        qi = pl.program_id(1)
        acc_s[...] = jnp.zeros_like(acc_s)
        q, do = q_ref[...], do_ref[...]
        lse = jnp.broadcast_to(lse_ref[...], (t, LANE))
        dl = dl_ref[...]

        def products(ki, masked):
            rows = pl.ds(pl.multiple_of(ki * t, t), t)
            kv = kv_ref[rows, :]
            s, k = _att_scores(q, kv, kr_ref[rows, :], LOG2_SCALE, masked, False)
            return s, lax.dot_general(do, kv[:, NOPE:], NT, preferred_element_type=F32), k

        def update(s, dp, k):
            p = jnp.exp2(s - jnp.tile(lse, (1, t // LANE)))
            ds = (p * (dp - jnp.tile(dl, (1, t // LANE))) * scale).astype(BF16)
            acc_s[...] += jnp.dot(ds, k, preferred_element_type=F32)

        def pair(k0, k1):
            first, second = products(k0, False), products(k1, False)
            update(*first)
            update(*second)

        _in_pairs(0, qi, pair, lambda ki: update(*products(ki, False)))
        update(*products(qi, True))
        dq_ref[...] = acc_s[...]

    col = pl.BlockSpec((None, t, 1), lambda h, qi: (h, qi, 0))
    return pl.pallas_call(
        body, name="flash_bwd_dq", grid=(HEADS, nq),
        in_specs=[pl.BlockSpec((t, HEAD_PAD), lambda h, qi: (qi, h)),
                  pl.BlockSpec((s_len, HEAD_PAD), lambda h, qi: (0, h)),
                  pl.BlockSpec((s_len, LANE), lambda h, qi: (0, 0)),
                  pl.BlockSpec((t, VHEAD), lambda h, qi: (qi, h)), col, pl.BlockSpec((t, VHEAD), lambda h, qi: (qi, h))],
        out_specs=pl.BlockSpec((t, HEAD_PAD), lambda h, qi: (qi, h)),
        out_shape=jax.ShapeDtypeStruct((s_len, HEADS * HEAD_PAD), F32),
        scratch_shapes=[pltpu.VMEM((t, HEAD_PAD), F32)],
        compiler_params=_params(("parallel", "arbitrary")),
    )(qb, kvb, krb, dob, lse, delta)


def _flash_bwd_dkv(qb, kvb, krb, dob, lse_row, delta_row):
    s_len = qb.shape[0]
    t = ATT_TILE
    nq = s_len // t
    scale = QK_HEAD ** -0.5

    def body(q_ref, kv_ref, kr_ref, do_ref, lse_ref, dl_ref, dkv_ref, dkr_ref, dk_s, dv_s):
        ki = pl.program_id(1)
        dk_s[...] = jnp.zeros_like(dk_s)
        dv_s[...] = jnp.zeros_like(dv_s)
        kv, kr = kv_ref[...], kr_ref[...]

        def products(qi, masked):
            rows = pl.ds(pl.multiple_of(qi * t, t), t)
            q, do = q_ref[rows, :], do_ref[rows, :]
            st, _ = _att_scores(q, kv, kr, LOG2_SCALE, masked, True)
            return st, lax.dot_general(kv[:, NOPE:], do, NT, preferred_element_type=F32), q, do, rows

        def update(st, dpt, q, do, rows):
            pt = jnp.exp2(st - lse_ref[:, rows])
            dv_s[...] += jnp.dot(pt.astype(BF16), do, preferred_element_type=F32)
            dst = (pt * (dpt - dl_ref[:, rows]) * scale).astype(BF16)
            dk_s[...] += jnp.dot(dst, q, preferred_element_type=F32)

        def pair(q0, q1):
            first, second = products(q0, False), products(q1, False)
            update(*first)
            update(*second)

        update(*products(ki, True))
        _in_pairs(ki + 1, nq, pair, lambda qi: update(*products(qi, False)))
        dk = dk_s[...]
        dkv_ref[...] = jnp.concatenate([dk[:, :NOPE], dv_s[...]], axis=1).astype(dkv_ref.dtype)
        dkr_ref[...] = dk[:, NOPE:]

    row = pl.BlockSpec((None, 1, s_len), lambda h, ki: (h, 0, 0))
    return pl.pallas_call(
        body, name="flash_bwd_dkv", grid=(HEADS, nq),
        in_specs=[pl.BlockSpec((s_len, HEAD_PAD), lambda h, ki: (0, h)),
                  pl.BlockSpec((t, HEAD_PAD), lambda h, ki: (ki, h)),
                  pl.BlockSpec((t, LANE), lambda h, ki: (ki, 0)),
                  pl.BlockSpec((s_len, VHEAD), lambda h, ki: (0, h)), row, row],
        out_specs=[pl.BlockSpec((t, HEAD_PAD), lambda h, ki: (ki, h)),
                   pl.BlockSpec((t, LANE), lambda h, ki: (ki, h))],
        out_shape=[jax.ShapeDtypeStruct((s_len, HEADS * HEAD_PAD), BF16),
                   jax.ShapeDtypeStruct((s_len, HEADS * LANE), F32)],
        scratch_shapes=[pltpu.VMEM((t, HEAD_PAD), F32), pltpu.VMEM((t, VHEAD), F32)],
        compiler_params=_params(("parallel", "parallel")),
    )(qb, kvb, krb, dob, lse_row, delta_row)


def _tril(w):
    r = lax.broadcasted_iota(jnp.int32, w.shape, 0)
    c = lax.broadcasted_iota(jnp.int32, w.shape, 1)
    return jnp.where(c <= r, w, 0.0)


def _sgu_mix(w_ref, vln, gd):
    wcs = [_tril(w_ref[g]).astype(BF16) for g in range(SGU_GROUPS)]
    mixed = jnp.concatenate(
        [jnp.dot(wcs[g], vln[:, g * gd:(g + 1) * gd], preferred_element_type=F32) for g in range(SGU_GROUPS)], axis=1)
    return wcs, mixed


def _sgu_fwd(zpre, ln_g, ln_b, w_sp, bias_full):
    s_len, two_w = zpre.shape
    width = two_w // 2
    gd = width // SGU_GROUPS
    t = SGU_CHUNK

    def body(z_ref, g_ref, b_ref, w_ref, bias_ref, uv_ref):
        u = _gelu(z_ref[:, :width])
        v = _gelu(z_ref[:, width:])
        d = v - jnp.mean(v, axis=-1, keepdims=True)
        vhat = d * lax.rsqrt(jnp.mean(d * d, axis=-1, keepdims=True) + LN_EPS)
        vln = (vhat * g_ref[...] + b_ref[...]).astype(BF16)
        _, mixed = _sgu_mix(w_ref, vln, gd)
        uv_ref[...] = (u * (mixed + bias_ref[...])).astype(uv_ref.dtype)

    return pl.pallas_call(
        body, name="sgu_fwd", grid=(s_len // t,),
        in_specs=[pl.BlockSpec((t, two_w), lambda i: (i, 0)), pl.BlockSpec((1, width), lambda i: (0, 0)),
                  pl.BlockSpec((1, width), lambda i: (0, 0)), pl.BlockSpec(w_sp.shape, lambda i: (0, 0, 0)),
                  pl.BlockSpec((t, width), lambda i: (0, 0))],
        out_specs=pl.BlockSpec((t, width), lambda i: (i, 0)),
        out_shape=jax.ShapeDtypeStruct((s_len, width), BF16),
        compiler_params=_params(("parallel",)),
    )(zpre, ln_g, ln_b, w_sp, bias_full)


def _sgu_bwd(zpre, duv, ln_g, ln_b, w_sp, bias_full):
    s_len, two_w = zpre.shape
    width = two_w // 2
    gd = width // SGU_GROUPS
    t = SGU_CHUNK

    def body(z_ref, duv_ref, g_ref, b_ref, w_ref, bias_ref, dz_ref, dg_ref, db_ref, dw_ref, dbias_ref):
        first = pl.program_id(0) == 0

        def accumulate(ref, val):
            @pl.when(first)
            def _():
                ref[...] = val

            @pl.when(jnp.logical_not(first))
            def _():
                ref[...] += val

        u, u_grad = _gelu_and_grad(z_ref[:, :width])
        v, v_grad = _gelu_and_grad(z_ref[:, width:])
        d = v - jnp.mean(v, axis=-1, keepdims=True)
        rstd = lax.rsqrt(jnp.mean(d * d, axis=-1, keepdims=True) + LN_EPS)
        vhat = d * rstd
        vln = (vhat * g_ref[...] + b_ref[...]).astype(BF16)
        wcs, mixed = _sgu_mix(w_ref, vln, gd)
        duv_v = duv_ref[...]
        du = duv_v * (mixed + bias_ref[...])
        dmixed = duv_v * u
        dmb = dmixed.astype(BF16)
        dvln = jnp.concatenate(
            [lax.dot_general(wcs[g], dmb[:, g * gd:(g + 1) * gd], TN, preferred_element_type=F32)
             for g in range(SGU_GROUPS)], axis=1)
        for g in range(SGU_GROUPS):
            dw = lax.dot_general(dmb[:, g * gd:(g + 1) * gd], vln[:, g * gd:(g + 1) * gd], NT, preferred_element_type=F32)
            accumulate(dw_ref.at[g], _tril(dw))
        dvhat = dvln * g_ref[...]
        dv0 = rstd * (dvhat - jnp.mean(dvhat, axis=-1, keepdims=True)
                      - vhat * jnp.mean(dvhat * vhat, axis=-1, keepdims=True))
        dz_ref[:, :width] = (du * u_grad).astype(dz_ref.dtype)
        dz_ref[:, width:] = (dv0 * v_grad).astype(dz_ref.dtype)
        accumulate(dg_ref, jnp.sum(dvln * vhat, axis=0, keepdims=True))
        accumulate(db_ref, jnp.sum(dvln, axis=0, keepdims=True))
        accumulate(dbias_ref, dmixed)

    vec = pl.BlockSpec((1, width), lambda i: (0, 0))
    return pl.pallas_call(
        body, name="sgu_bwd", grid=(s_len // t,),
        in_specs=[pl.BlockSpec((t, two_w), lambda i: (i, 0)), pl.BlockSpec((t, width), lambda i: (i, 0)), vec, vec,
                  pl.BlockSpec(w_sp.shape, lambda i: (0, 0, 0)), pl.BlockSpec((t, width), lambda i: (0, 0))],
        out_specs=[pl.BlockSpec((t, two_w), lambda i: (i, 0)), vec, vec,
                   pl.BlockSpec(w_sp.shape, lambda i: (0, 0, 0)), pl.BlockSpec((t, width), lambda i: (0, 0))],
        out_shape=[jax.ShapeDtypeStruct((s_len, two_w), BF16), jax.ShapeDtypeStruct((1, width), F32),
                   jax.ShapeDtypeStruct((1, width), F32), jax.ShapeDtypeStruct(w_sp.shape, F32),
                   jax.ShapeDtypeStruct((t, width), F32)],
        compiler_params=_params(("arbitrary",)),
    )(zpre, duv, ln_g, ln_b, w_sp, bias_full)


def _rope_tables(positions):
    inv_freq = ROPE_THETA ** (-jnp.arange(0, ROPE, 2, dtype=F32) / ROPE)
    ang = positions.astype(F32)[:, None] * inv_freq
    cos, sin = jnp.cos(ang), jnp.sin(ang)
    z32, z64 = jnp.zeros_like(cos), jnp.zeros((cos.shape[0], LANE - ROPE), F32)
    return (jnp.concatenate([cos, cos, z64], axis=1), jnp.concatenate([z32, sin, z64], axis=1),
            jnp.concatenate([-sin, z32, z64], axis=1))


def _q_rope(name, q, tables, transpose):
    rot = _rope_t if transpose else _rope

    def fn(qv, cs, s1, s2):
        parts = []
        for h in range(HEADS):
            parts.append(qv[:, h * HEAD_PAD:h * HEAD_PAD + NOPE])
            parts.append(rot(qv[:, h * HEAD_PAD + NOPE:(h + 1) * HEAD_PAD], cs, s1, s2))
        return (jnp.concatenate(parts, axis=1),)

    return _rowwise(name, fn, [q, *tables], [], [(q.shape[1], BF16)])[0]


def _ffn_fwd(x, g, w_up3, w_down3):
    h2 = _rowwise("ffn_norm", lambda xv, gv: (_rms_fwd(xv, gv),), [x], [g], [(x.shape[1], BF16)])[0]

    def sq_relu(acc):
        r = jnp.maximum(acc, 0.0)
        return (r * r,)

    r = _mm_stacked("ffn_up", h2, w_up3, "col", (BF16,), sq_relu, strips=2)[0]
    x_out = _mm_stacked("ffn_down", r, w_down3, "row", (F32,), lambda acc, res: (acc + res,), [x], strips=2)[0]
    return x_out, (x, h2, r)


def _ffn_bwd(dx, dxb, saved, g, w_up3, w_down3):
    x, h2, r = saved
    da = _mm_stacked_nt("ffn_down_dx", dxb, w_down3, "row", (BF16,),
                        lambda acc, rv: (acc * (2.0 * jnp.sqrt(rv.astype(F32))),), [r], strips=2)[0]
    g_down = _mm_tn_stacked("ffn_down_dw", r, dxb, w_down3.shape, "row")
    dx, dxb, dg = _mm_stacked_nt("ffn_up_dx", da, w_up3, "col", **_norm_bwd(x, g, dx))
    g_up = _mm_tn_stacked("ffn_up_dw", h2, da, w_up3.shape, "col")
    return dx, dxb, dg, g_up, g_down


def _norm_bwd(x, g, dres):
    def epilogue(dh, xv, rv, gv):
        dxv, dg = _rms_bwd(dh, xv, gv)
        return dxv + rv, dxv + rv, dg

    return dict(out_dtypes=(F32, BF16), epilogue=epilogue, extras=[x, dres], consts=[g], sums=[g.shape], tm=MM_TILE // 2)


def _mla_fwd(x, g, wdkv, q_norm, kv_norm, wq, wkv, wo, tables):
    d = x.shape[1]
    h = _rowwise("mla_norm", lambda xv, gv: (_rms_fwd(xv, gv),), [x], [g], [(d, BF16)])[0]
    lat = _mm("mla_dkv", h, wdkv)[0]

    def lat_post(lv, cs, s1, s2, qg, kg):
        return (_rms_fwd(lv[:, :Q_RANK], qg), _rms_fwd(lv[:, Q_RANK:Q_RANK + KV_RANK], kg),
                _rope(lv[:, Q_RANK + KV_RANK:], cs, s1, s2))

    cqn, ckvn, krb = _rowwise("mla_lat", lat_post, [lat, *tables], [q_norm, kv_norm],
                              [(Q_RANK, BF16), (KV_RANK, BF16), (LANE, BF16)])
    q = _mm("mla_uq", cqn, wq)[0]
    kvb = _mm("mla_ukv", ckvn, wkv, (BF16,))[0]
    qb = _q_rope("mla_q_rope", q, tables, False)
    ob, lse = _flash_fwd(qb, kvb, krb)
    x_mid = _mm("mla_o", ob, wo, (F32,), lambda acc, res: (acc + res,), [x], strips=2)[0]
    return x_mid, (x, h, lat, cqn, ckvn, krb, qb, kvb, ob, lse)


def _mla_bwd(dx, dxb, saved, g, wdkv, q_norm, kv_norm, wq, wkv, wo, tables):
    x, h, lat, cqn, ckvn, krb, qb, kvb, ob, lse = saved
    s_len = x.shape[0]
    def with_delta(do, ov):
        prod = do * ov.astype(F32)
        sums = [jnp.sum(prod[:, hd * VHEAD:(hd + 1) * VHEAD], axis=1, keepdims=True) for hd in range(HEADS)]
        return do, jnp.concatenate([jnp.broadcast_to(s, (do.shape[0], VHEAD)) for s in sums], axis=1)

    dob, delta = _mm("mla_o_dx", dxb, wo, (BF16, F32), with_delta, [ob], nt=True, tm=MM_TILE // 2)
    g_wo = _mm_tn("mla_o_dw", ob, dxb)[0]
    delta_row = delta.reshape(s_len, HEADS, VHEAD)[:, :, 0].T.reshape(HEADS, 1, s_len)
    dq = _flash_bwd_dq(qb, kvb, krb, dob, lse, delta)
    dkvb, dkr = _flash_bwd_dkv(qb, kvb, krb, dob, lse.reshape(HEADS, 1, s_len), delta_row)
    dqb = _q_rope("mla_q_rope_bwd", dq, tables, True)
    dcqn = _mm("mla_uq_dx", dqb, wq, nt=True)[0]
    g_wq = _mm_tn("mla_uq_dw", cqn, dqb)[0]
    dckvn = _mm("mla_ukv_dx", dkvb, wkv, nt=True)[0]
    g_wkv = _mm_tn("mla_ukv_dw", ckvn, dkvb)[0]

    def lat_bwd(dq_v, dkv_v, dkr_v, lv, cs, s1, s2, qg, kg):
        dcq, dqg = _rms_bwd(dq_v, lv[:, :Q_RANK], qg)
        dckv, dkg = _rms_bwd(dkv_v, lv[:, Q_RANK:Q_RANK + KV_RANK], kg)
        dkr_sum = dkr_v[:, :LANE]
        for hd in range(1, HEADS):
            dkr_sum = dkr_sum + dkr_v[:, hd * LANE:(hd + 1) * LANE]
        return jnp.concatenate([dcq, dckv, _rope_t(dkr_sum, cs, s1, s2)], axis=1), dqg, dkg

    dlat, g_qn, g_kvn = _rowwise("mla_lat_bwd", lat_bwd, [dcqn, dckvn, dkr, lat, *tables], [q_norm, kv_norm],
                                 [(LAT_PAD, BF16)], [q_norm.shape, kv_norm.shape])
    g_wdkv = _mm_tn("mla_dkv_dw", h, dlat)[0]
    dx, dxb, dg = _mm("mla_dkv_dx", dlat, wdkv, nt=True, **_norm_bwd(x, g, dx))
    return dx, dxb, dg, g_wdkv, g_qn, g_kvn, g_wq, g_wkv, g_wo


def _sgu_layer_fwd(x, g, w_in3, ln_g, ln_b, w_sp, bias_full, w_out3):
    h = _rowwise("sgu_norm", lambda xv, gv: (_rms_fwd(xv, gv),), [x], [g], [(x.shape[1], BF16)])[0]
    zpre = _mm_stacked("sgu_in", h, w_in3, "col")[0]
    uv = _sgu_fwd(zpre, ln_g, ln_b, w_sp, bias_full)
    x_mid = _mm_stacked("sgu_out", uv, w_out3, "row", (F32,), lambda acc, res: (acc + res,), [x], strips=2)[0]
    return x_mid, (x, h, zpre, uv)


def _sgu_layer_bwd(dx, dxb, saved, g, w_in3, ln_g, ln_b, w_sp, bias_full, w_out3):
    x, h, zpre, uv = saved
    duv = _mm_stacked_nt("sgu_out_dx", dxb, w_out3, "row")[0]
    g_out = _mm_tn_stacked("sgu_out_dw", uv, dxb, w_out3.shape, "row")
    dz, g_lng, g_lnb, g_wsp, g_bias = _sgu_bwd(zpre, duv, ln_g, ln_b, w_sp, bias_full)
    g_in = _mm_tn_stacked("sgu_in_dw", h, dz, w_in3.shape, "col")
    dx, dxb, dg = _mm_stacked_nt("sgu_in_dx", dz, w_in3, "col", **_norm_bwd(x, g, dx))
    return dx, dxb, dg, g_in, g_out, g_lng, g_lnb, g_wsp, g_bias


def _loss_head(x, target, g):
    d = x.shape[1]

    def fn(xv, tv, gv):
        err = _rms_fwd(xv, gv) - tv
        dxv, dg = _rms_bwd(err * (1.0 / d), xv, gv)
        return dxv, dxv, dg, jnp.sum(err * err, axis=0, keepdims=True)

    return _rowwise("loss_head", fn, [x, target], [g], [(d, F32), (d, BF16)], [g.shape, g.shape])


def _mixer_weights(i, stacks):
    by_rows = lambda a: a.reshape(N_SHARDS * a.shape[1], a.shape[2])
    by_cols = lambda a: a.transpose(1, 0, 2).reshape(a.shape[1], N_SHARDS * a.shape[2])
    if i % 2:
        w_in3, w_out3, ln_g, ln_b = stacks
        return w_in3, ln_g.reshape(1, -1), ln_b.reshape(1, -1), w_out3
    wdkv = by_rows(stacks[0])
    wdkv = jnp.pad(wdkv, ((0, 0), (0, LAT_PAD - wdkv.shape[1])))
    wq = jnp.pad(by_cols(stacks[1]).reshape(Q_RANK, HEADS, QK_HEAD), ((0, 0), (0, 0), (0, HEAD_PAD - QK_HEAD)))
    return wdkv, wq.reshape(Q_RANK, HEADS * HEAD_PAD), by_cols(stacks[2]), by_rows(stacks[3])


def _local_step(x, positions, target, norm_mix, norm_ffn, final_norm, q_norm, kv_norm, w_sp, b_sp, mixers, ffn, reducer):
    tables = _rope_tables(positions)
    gd = mixers[1][2].size // SGU_GROUPS
    bias_full = [jnp.repeat(b_sp[j].T, gd, axis=1) for j in range(DEPTH // 2)]
    saved, mla, sgu = [], [None] * (DEPTH // 2), [None] * (DEPTH // 2)
    for i in range(DEPTH):
        j = i // 2
        x, *stacks = lax.optimization_barrier((x, *mixers[i]))
        if i % 2 == 0:
            wdkv, wq, wkv, wo = mla[j] = _mixer_weights(i, stacks)
            x, s_mix = _mla_fwd(x, norm_mix[i:i + 1], wdkv, q_norm[j:j + 1], kv_norm[j:j + 1], wq, wkv, wo, tables)
        else:
            w_in3, ln_g, ln_b, w_out3 = sgu[j] = _mixer_weights(i, stacks)
            x, s_mix = _sgu_layer_fwd(x, norm_mix[i:i + 1], w_in3, ln_g, ln_b, w_sp[j], bias_full[j], w_out3)
        x, s_ffn = _ffn_fwd(x, norm_ffn[i:i + 1], *ffn[i])
        saved.append((s_mix, s_ffn))
    dx, dxb, g_final, sq_cols = _loss_head(x, target, final_norm[None, :])
    loss = 0.5 * jnp.sum(sq_cols) / x.shape[1]

    def pair(g):
        return g, g.astype(BF16)

    g_mix, g_ffn = [None] * DEPTH, [None] * DEPTH
    mla_g, sgu_g = [None] * (DEPTH // 2), [None] * (DEPTH // 2)
    for i in reversed(range(DEPTH)):
        j = i // 2
        s_mix, s_ffn = saved[i]
        dx, dxb, g_ffn[i], g_up, g_down = _ffn_bwd(dx, dxb, s_ffn, norm_ffn[i:i + 1], *ffn[i])
        dxb = reducer.add(f"ffn{i}", i, {"ffn_w_up": g_up, "ffn_w_down": g_down}, dxb)
        dxb = reducer.phase_end(dxb)
        if i % 2 == 0:
            wdkv, wq, wkv, wo = mla[j]
            dx, dxb, g_mix[i], g_wdkv, g_qn, g_kvn, g_wq, g_wkv, g_wo = _mla_bwd(
                dx, dxb, s_mix, norm_mix[i:i + 1], wdkv, q_norm[j:j + 1], kv_norm[j:j + 1], wq, wkv, wo, tables)
            mla_g[j] = (g_qn, g_kvn)
            g_wq = g_wq.reshape(Q_RANK, HEADS, HEAD_PAD)[..., :QK_HEAD].reshape(Q_RANK, N_SHARDS, -1)
            dxb = reducer.add(f"mla{j}", j, {
                "mla_w_dkv": pair(g_wdkv[:, :Q_RANK + KV_RANK + ROPE].reshape(N_SHARDS, -1, Q_RANK + KV_RANK + ROPE)),
                "mla_w_uq": pair(g_wq.transpose(1, 0, 2)),
                "mla_w_ukv": pair(g_wkv.reshape(KV_RANK, N_SHARDS, -1).transpose(1, 0, 2)),
                "mla_w_o": pair(g_wo.reshape(N_SHARDS, -1, g_wo.shape[1]))}, dxb)
        else:
            w_in3, ln_g, ln_b, w_out3 = sgu[j]
            dx, dxb, g_mix[i], g_in, g_out, g_lng, g_lnb, g_wsp, g_bias = _sgu_layer_bwd(
                dx, dxb, s_mix, norm_mix[i:i + 1], w_in3, ln_g, ln_b, w_sp[j], bias_full[j], w_out3)
            sgu_g[j] = (g_wsp, g_bias.reshape(SGU_CHUNK, SGU_GROUPS, gd).sum(axis=-1).T)
            dxb = reducer.add(f"sgu{j}", j, {"sgu_w_in": g_in, "sgu_w_out": g_out,
                                             "sgu_ln_g": pair(g_lng.reshape(N_SHARDS, -1, LANE)),
                                             "sgu_ln_b": pair(g_lnb.reshape(N_SHARDS, -1, LANE))}, dxb)
        dxb = reducer.phase_end(dxb)
    small = dict(
        norm_mix=jnp.concatenate(g_mix, axis=0), norm_ffn=jnp.concatenate(g_ffn, axis=0), final_norm=g_final[0],
        q_norm=jnp.concatenate([m[0] for m in mla_g], axis=0), kv_norm=jnp.concatenate([m[1] for m in mla_g], axis=0),
        w_sp=jnp.stack([s[0] for s in sgu_g]), b_sp=jnp.stack([s[1] for s in sgu_g]))
    return loss, dx, small


HBM_SPEC = pl.BlockSpec(memory_space=pltpu.HBM)


def _place():
    x, y, c = lax.axis_index("x"), lax.axis_index("y"), lax.axis_index("c")
    return x, y, c, [(1 - x, y), (x, 1 - y), (1 - x, 1 - y)]


def _remote(src, dst, send_sems, recv_sems, k, to):
    return pltpu.make_async_remote_copy(src_ref=src, dst_ref=dst, send_sem=send_sems.at[k], recv_sem=recv_sems.at[k],
                                        device_id=to, device_id_type=MESH)


def _gather_layer(tag, shards):
    n = len(shards)
    split = [s.shape[0] >= 16 for s in shards]

    def body(*refs):
        ins, outs = refs[:n], refs[n:2 * n]
        send_sems, recv_sems, local_sems = refs[2 * n:]
        x, y, c, chips = _place()
        mine = 2 * x + y
        barrier = pltpu.get_barrier_semaphore()
        peers = [(x, y, 1 - c)] + [(*chip, c) for chip in chips]
        for peer in peers:
            pl.semaphore_signal(barrier, inc=1, device_id=peer, device_id_type=MESH)
        pl.semaphore_wait(barrier, len(peers))

        def rows(t, half):
            hr = shards[t].shape[0] // 2
            return pl.ds(half * hr, hr) if split[t] else pl.ds(0, shards[t].shape[0])

        local, sent = [], []
        for t in range(n):
            local.append(pltpu.make_async_copy(ins[t], outs[t].at[mine], local_sems.at[t]))
            local[-1].start()
            for j, chip in enumerate(chips):
                cp = _remote(ins[t].at[rows(t, c)], outs[t].at[mine, rows(t, c)], send_sems, recv_sems, 3 * t + j, (*chip, c))
                cp.start()
                sent.append(cp)
        for j, chip in enumerate(chips):
            theirs = 2 * chip[0] + chip[1]
            for t in range(n):
                piece = outs[t].at[theirs, rows(t, c)]
                _remote(piece, piece, send_sems, recv_sems, 3 * t + j, (x, y, c)).wait_recv()
                if split[t]:
                    cp = _remote(piece, piece, send_sems, recv_sems, 3 * n + 3 * t + j, (x, y, 1 - c))
                    cp.start()
                    sent.append(cp)
        for j, chip in enumerate(chips):
            theirs = 2 * chip[0] + chip[1]
            for t in range(n):
                if split[t]:
                    piece = outs[t].at[theirs, rows(t, 1 - c)]
                    _remote(piece, piece, send_sems, recv_sems, 3 * n + 3 * t + j, (x, y, c)).wait_recv()
        for cp in sent:
            cp.wait_send()
        for cp in local:
            cp.wait()

    return pl.kernel(
        body, name=f"gather_{tag}", mesh=plsc.ScalarSubcoreMesh(axis_name="sequencer", num_cores=1),
        out_type=[jax.ShapeDtypeStruct((N_SHARDS, *s.shape), s.dtype) for s in shards],
        scratch_types=[pltpu.SemaphoreType.DMA((6 * n,)), pltpu.SemaphoreType.DMA((6 * n,)), pltpu.SemaphoreType.DMA((n,))],
        compiler_params=pltpu.CompilerParams(collective_id=ID_GATHER),
    )(*shards)


SEQUENCER = dict(axis_name="sequencer", num_cores=1)
ID_GATHER, ID_EXCHANGE, ID_SHARE = 0, 1, 2
MIN_SPLIT_ROWS = 16


def _handshake(peers):
    barrier = pltpu.get_barrier_semaphore()
    for peer in peers:
        pl.semaphore_signal(barrier, inc=1, device_id=peer, device_id_type=MESH)
    pl.semaphore_wait(barrier, len(peers))


def _half_rows(rows, half):
    return pl.ds(half * (rows // 2), rows // 2) if rows >= MIN_SPLIT_ROWS else pl.ds(0, rows)


SEM_SPEC = pl.BlockSpec(memory_space=pltpu.SEMAPHORE)
DATAFLOW = pltpu.SideEffectType.DATAFLOW_SIDE_EFFECTING


def _exchange_copies(shapes, stacks, lands, send_sems, recv_sems):
    x, y, c, chips = _place()
    mine = 2 * x + y
    copies = []
    for t, shape in enumerate(shapes):
        r = shape[1]
        copies.append(_remote(stacks[t].at[mine, _half_rows(r, 1 - c)], lands[t].at[0], send_sems, recv_sems, 7 * t, (x, y, 1 - c)))
        for j, chip in enumerate(chips):
            theirs = 2 * chip[0] + chip[1]
            copies.append(_remote(stacks[t].at[theirs, _half_rows(r, c)], lands[t].at[1 + j], send_sems, recv_sems,
                                  7 * t + 1 + j, (*chip, c)))
            copies.append(_remote(stacks[t].at[theirs, _half_rows(r, 1 - c)], lands[t].at[4 + j], send_sems, recv_sems,
                                  7 * t + 4 + j, (*chip, 1 - c)))
    return copies


def _exchange_start(tag, stacks, carry):
    n = len(stacks)
    shapes = [s.shape for s in stacks]
    lands = [lax.empty((7, s.shape[1] // 2 if s.shape[1] >= MIN_SPLIT_ROWS else s.shape[1], s.shape[2]), s.dtype) for s in stacks]

    def body(*refs):
        for cp in _exchange_copies(shapes, refs[:n], refs[n:2 * n], refs[2 * n + 1], refs[2 * n + 2]):
            cp.start()

    through = (*stacks, *lands, carry)
    out = pl.pallas_call(
        body, name=f"reduce_exchange_start_{tag}",
        out_shape=(pltpu.SemaphoreType.DMA((7 * n,)), pltpu.SemaphoreType.DMA((7 * n,)),
                   *[pltpu.HBM(a.shape, a.dtype) for a in through]),
        in_specs=[HBM_SPEC] * (2 * n + 1),
        out_specs=(SEM_SPEC, SEM_SPEC, *[HBM_SPEC] * (2 * n + 1)),
        input_output_aliases={t: 2 + t for t in range(2 * n + 1)},
        compiler_params=pltpu.CompilerParams(has_side_effects=DATAFLOW),
    )(*[pltpu.with_memory_space_constraint(a, pltpu.HBM) for a in through])
    return out[0], out[1], out[2:2 + n], out[2 + n:2 + 2 * n], out[-1]


def _exchange_wait(tag, send_sems, recv_sems, stacks, lands, after):
    n = len(stacks)
    shapes = [s.shape for s in stacks]

    def body(*refs):
        for cp in _exchange_copies(shapes, refs[:n], refs[n:2 * n], refs[2 * n], refs[2 * n + 1]):
            cp.wait()

    out = pl.pallas_call(
        body, name=f"reduce_exchange_wait_{tag}",
        out_shape=tuple(pltpu.HBM(a.shape, a.dtype) for a in (*stacks, *lands)),
        in_specs=[HBM_SPEC] * (2 * n) + [SEM_SPEC, SEM_SPEC, pl.BlockSpec(memory_space=pl.ANY)],
        out_specs=tuple([HBM_SPEC] * (2 * n)),
        input_output_aliases={t: t for t in range(2 * n)},
        compiler_params=pltpu.CompilerParams(has_side_effects=DATAFLOW),
    )(*stacks, *lands, send_sems, recv_sems, after)
    return out[:n], out[n:]


def _share_halves(tag, halves):
    n = len(halves)

    def body(*refs):
        ins, outs, send_sems, recv_sems = refs[:n], refs[n:2 * n], refs[2 * n], refs[2 * n + 1]
        x, y, c, _ = _place()
        _handshake([(x, y, 1 - c)])
        sent = [_remote(ins[t], outs[t], send_sems, recv_sems, t, (x, y, 1 - c)) for t in range(n)]
        for cp in sent:
            cp.start()
        for cp in sent:
            cp.wait()

    return pl.kernel(
        body, name=f"reduce_share_{tag}", mesh=plsc.ScalarSubcoreMesh(**SEQUENCER),
        out_type=[jax.ShapeDtypeStruct(h.shape, h.dtype) for h in halves],
        scratch_types=[pltpu.SemaphoreType.DMA((n,)), pltpu.SemaphoreType.DMA((n,))],
        compiler_params=pltpu.CompilerParams(collective_id=ID_SHARE),
    )(*halves)


def _all_reduce_small(part):
    rows = part.shape[0]

    def body(p_ref, out_ref, sib_buf, chip_sums, send_sems, recv_sems):
        x, y, c, chips = _place()
        mine = 2 * x + y
        swap = _remote(p_ref, sib_buf, send_sems, recv_sems, 0, (x, y, 1 - c))
        swap.start()
        swap.wait()
        chip_sums[mine] = p_ref[...] + sib_buf[...]
        sent = [_remote(chip_sums.at[mine], chip_sums.at[mine], send_sems, recv_sems, 1 + j, (*chip, c))
                for j, chip in enumerate(chips)]
        for cp in sent:
            cp.start()
        for j, chip in enumerate(chips):
            sent[j].wait_send()
            theirs = chip_sums.at[2 * chip[0] + chip[1]]
            _remote(theirs, theirs, send_sems, recv_sems, 1 + j, (x, y, c)).wait_recv()
        out_ref[...] = ((chip_sums[0] + chip_sums[1]) + chip_sums[2]) + chip_sums[3]

    vmem = pl.BlockSpec(memory_space=pltpu.VMEM)
    return pl.pallas_call(
        body, name="all_reduce_small", in_specs=[vmem], out_specs=vmem, out_shape=jax.ShapeDtypeStruct(part.shape, F32),
        scratch_shapes=[pltpu.VMEM((rows, LANE), F32), pltpu.VMEM((N_SHARDS, rows, LANE), F32),
                        pltpu.SemaphoreType.DMA((4,)), pltpu.SemaphoreType.DMA((4,))],
        compiler_params=pltpu.CompilerParams(vmem_limit_bytes=VMEM_LIMIT_BYTES),
    )(part)


def _sum_partials(g3, others, sel):
    _, rows, c = others.shape
    whole = g3.shape[1] == rows
    tr = _tile(rows, 512)
    nb = rows // tr

    def body(sel_ref, g_ref, *rest):
        same = g_ref[...].astype(F32)
        for ref in rest[1:4]:
            same = same + ref[...].astype(F32)
        other = rest[0][...].astype(F32)
        for ref in rest[4:7]:
            other = other + ref[...].astype(F32)
        rest[7][...] = same + other

    blk = (None, tr, c)
    slots = [pl.BlockSpec(blk, functools.partial(lambda i, sr, k: (k, i, 0), k=k)) for k in range(7)]
    return pl.pallas_call(
        body, name="reduce_sum_partials",
        grid_spec=pltpu.PrefetchScalarGridSpec(
            num_scalar_prefetch=1, grid=(nb,),
            in_specs=[pl.BlockSpec(blk, lambda i, sr: (sr[0], (0 if whole else sr[1] * nb) + i, 0))] + slots,
            out_specs=pl.BlockSpec((tr, c), lambda i, sr: (i, 0))),
        out_shape=jax.ShapeDtypeStruct((rows, c), F32),
        compiler_params=_params(("parallel",)),
    )(sel, g3, *[others] * 7)


def _adamw_math(w, g, m, v):
    nm = ADAM_B1 * m + (1.0 - ADAM_B1) * g
    nv = ADAM_B2 * v + (1.0 - ADAM_B2) * (g * g)
    m_hat = nm / (1.0 - ADAM_B1 ** ADAM_STEP)
    v_hat = nv / (1.0 - ADAM_B2 ** ADAM_STEP)
    return -ADAM_LR * (m_hat / (jnp.sqrt(v_hat) + ADAM_EPS) + ADAM_WD * w), nm, nv


def _adamw_layer(layer, w, m, v, g_mine, g_sibling, sel, prev):
    lyr, r, c = w.shape
    rows = g_mine.shape[0]
    halves = r // rows
    tr = _tile(rows, 512)
    nb = rows // tr
    n_g = 1 if g_sibling is None else 2

    def body(sel_ref, w_ref, m_ref, v_ref, *rest):
        g = rest[0][...]
        if n_g == 2:
            g = jnp.where(pl.program_id(0) == sel_ref[1], g, rest[1][...])
        outs = rest[n_g + (0 if prev is None else 4):]
        d, nm, nv = _adamw_math(w_ref[...], g, m_ref[...], v_ref[...])
        for ref, val in zip(outs, (g, d, nm, nv)):
            ref[...] = val

    full = pl.BlockSpec((None, tr, c), lambda h, i, sr: (layer, h * nb + i, 0))
    part = pl.BlockSpec((tr, c), lambda h, i, sr: (i, 0))
    n_in = 4 + n_g
    return pl.pallas_call(
        body, name="adamw_layer",
        grid_spec=pltpu.PrefetchScalarGridSpec(
            num_scalar_prefetch=1, grid=(halves, nb),
            in_specs=[full] * 3 + [part] * n_g + ([] if prev is None else [pl.BlockSpec(memory_space=pl.ANY)] * 4),
            out_specs=[full] * 4),
        out_shape=[jax.ShapeDtypeStruct(w.shape, F32)] * 4,
        input_output_aliases={} if prev is None else {n_in + k: k for k in range(4)},
        compiler_params=_params(("parallel", "parallel")),
    )(sel, w, m, v, g_mine, *([] if g_sibling is None else [g_sibling]), *([] if prev is None else prev))


def _adamw(w, g, m, v):
    lyr, r, c = w.shape
    tr = _tile(r, 512)

    def body(w_ref, g_ref, m_ref, v_ref, d_ref, nm_ref, nv_ref):
        gv = g_ref[...]
        nm = ADAM_B1 * m_ref[...] + (1.0 - ADAM_B1) * gv
        nv = ADAM_B2 * v_ref[...] + (1.0 - ADAM_B2) * (gv * gv)
        m_hat = nm / (1.0 - ADAM_B1 ** ADAM_STEP)
        v_hat = nv / (1.0 - ADAM_B2 ** ADAM_STEP)
        d_ref[...] = -ADAM_LR * (m_hat / (jnp.sqrt(v_hat) + ADAM_EPS) + ADAM_WD * w_ref[...])
        nm_ref[...] = nm
        nv_ref[...] = nv

    blk = pl.BlockSpec((None, tr, c), lambda l, i: (l, i, 0))
    return pl.pallas_call(
        body, name="adamw", grid=(lyr, r // tr), in_specs=[blk] * 4, out_specs=[blk] * 3,
        out_shape=[jax.ShapeDtypeStruct(w.shape, F32)] * 3,
        compiler_params=_params(("parallel", "parallel")),
    )(w, g, m, v)


SHARDED = ("mla_w_dkv", "mla_w_uq", "mla_w_ukv", "mla_w_o", "sgu_w_in", "sgu_ln_g", "sgu_ln_b", "sgu_w_out",
           "ffn_w_up", "ffn_w_down")
REPLICATED = ("norm_mix", "norm_ffn", "final_norm", "mla_q_norm", "mla_kv_norm", "sgu_w_spatial", "sgu_b_spatial")
WEIGHTS = ("norm_mix", "norm_ffn", "final_norm", "mla_w_dkv", "mla_q_norm", "mla_kv_norm", "mla_w_uq", "mla_w_ukv",
           "mla_w_o", "sgu_w_in", "sgu_ln_g", "sgu_ln_b", "sgu_w_spatial", "sgu_b_spatial", "sgu_w_out", "ffn_w_up",
           "ffn_w_down")


class _Reducer:
    def __init__(self, state, sel):
        self.state, self.sel = state, sel
        self.started, self.travelling, self.summed = [], [], []
        self.done = {}

    def add(self, tag, layer, grads, token):
        names = list(grads)
        token, *tied = lax.optimization_barrier((token, *[a for n in names for a in grads[n]]))
        f32s, bf16s = tied[0::2], tied[1::2]
        *flying, token = _exchange_start(tag, bf16s, token)
        self.started.append((tag, layer, names, f32s, flying))
        return token

    def phase_end(self, token):
        for tag, layer, names, f32s, flying in self.travelling:
            bf16s, received = _exchange_wait(tag, *flying, token)
            own = [g if g.shape[1] >= MIN_SPLIT_ROWS else gb for g, gb in zip(f32s, bf16s)]
            mine = [_sum_partials(g, got, self.sel) for g, got in zip(own, received)]
            token, *mine = lax.optimization_barrier((token, *mine))
            cut = [k for k, g in enumerate(own) if g.shape[1] >= MIN_SPLIT_ROWS]
            theirs = dict(zip(cut, _share_halves(tag, [mine[k] for k in cut])))
            self.summed.append((layer, names, mine, [theirs.get(k) for k in range(len(names))]))
        self.travelling, self.started = self.started, []
        return token

    def update(self, token):
        for layer, names, mine, theirs in self.summed:
            for name, g_mine, g_theirs in zip(names, mine, theirs):
                w, m, v = self.state[name]
                self.done[name] = _adamw_layer(layer, w, m, v, g_mine, g_theirs, self.sel, self.done.get(name))
                token = self.done[name][1]
        self.summed = []
        return token


def _as3d(name, a):
    return a.reshape(a.shape[0], -1, LANE) if name in ("sgu_ln_g", "sgu_ln_b") else a


def _pack(parts):
    flat = jnp.concatenate([p.reshape(-1) for p in parts])
    rows = -(-flat.shape[0] // (8 * LANE)) * 8
    return jnp.pad(flat, (0, rows * LANE - flat.shape[0])).reshape(rows, LANE)


def _unpack(packed, like):
    flat, out, at = packed.reshape(-1), [], 0
    for p in like:
        out.append(flat[at:at + p.size].reshape(p.shape))
        at += p.size
    return out


def kernel(x, positions, norm_mix, norm_ffn, final_norm, mla_w_dkv, mla_q_norm, mla_kv_norm, mla_w_uq, mla_w_ukv, mla_w_o, sgu_w_in, sgu_ln_g, sgu_ln_b, sgu_w_spatial, sgu_b_spatial, sgu_w_out, ffn_w_up, ffn_w_down, loss_target, m_norm_mix, m_norm_ffn, m_final_norm, m_mla_w_dkv, m_mla_q_norm, m_mla_kv_norm, m_mla_w_uq, m_mla_w_ukv, m_mla_w_o, m_sgu_w_in, m_sgu_ln_g, m_sgu_ln_b, m_sgu_w_spatial, m_sgu_b_spatial, m_sgu_w_out, m_ffn_w_up, m_ffn_w_down, v_norm_mix, v_norm_ffn, v_final_norm, v_mla_w_dkv, v_mla_q_norm, v_mla_kv_norm, v_mla_w_uq, v_mla_w_ukv, v_mla_w_o, v_sgu_w_in, v_sgu_ln_g, v_sgu_ln_b, v_sgu_w_spatial, v_sgu_b_spatial, v_sgu_w_out, v_ffn_w_up, v_ffn_w_down):
    given = dict(locals())
    w = {n: given[n] for n in WEIGHTS}
    mom = {n: given["m_" + n] for n in WEIGHTS}
    var = {n: given["v_" + n] for n in WEIGHTS}
    mixers, ffn, token = [], [], None
    for i in range(DEPTH):
        j = i // 2
        if i % 2 == 0:
            mixer = [w[n][j].astype(BF16) for n in ("mla_w_dkv", "mla_w_uq", "mla_w_ukv", "mla_w_o")]
        else:
            mixer = [sgu_w_in[j].astype(BF16), sgu_w_out[j].astype(BF16), sgu_ln_g[j].reshape(-1, LANE),
                     sgu_ln_b[j].reshape(-1, LANE)]
        for tag, shards, into in ((f"mixer{i}", mixer, mixers), (f"ffn{i}", [ffn_w_up[i].astype(BF16), ffn_w_down[i].astype(BF16)], ffn)):
            if token is None:
                token = shards[0]
            else:
                token, *shards = lax.optimization_barrier((token, *shards))
            into.append(_gather_layer(tag, shards))

    x_i, y_i, c_i = lax.axis_index("x"), lax.axis_index("y"), lax.axis_index("c")
    sel = jnp.stack([2 * x_i + y_i, c_i]).astype(jnp.int32)
    reducer = _Reducer({n: tuple(_as3d(n, d[n]) for d in (w, mom, var)) for n in SHARDED}, sel)
    loss, dx, small = _local_step(
        x[0], positions[0], loss_target[0], norm_mix, norm_ffn, final_norm, mla_q_norm, mla_kv_norm, sgu_w_spatial,
        sgu_b_spatial, mixers, ffn, reducer)
    loss = lax.psum(loss, ("x", "y", "c"))

    small_g = [small["norm_mix"], small["norm_ffn"], small["final_norm"], small["q_norm"], small["kv_norm"],
               small["w_sp"], small["b_sp"]]
    like = [w[n] for n in REPLICATED]
    g_small = _all_reduce_small(_pack(small_g))
    packed = [_pack([d[n] for n in REPLICATED])[None] for d in (w, mom, var)]
    upd_small = _adamw(packed[0], g_small[None], packed[1], packed[2])
    grads = dict(zip(REPLICATED, _unpack(g_small, like)))
    delta, new_m, new_v = ({n: a for n, a in zip(REPLICATED, _unpack(u[0], like))} for u in upd_small)

    reducer.phase_end(reducer.update(upd_small[0]))
    reducer.update(None)
    for n in SHARDED:
        grads[n], delta[n], new_m[n], new_v[n] = (a.reshape(w[n].shape) for a in reducer.done[n])

    return (loss, dx[None], *[grads[n] for n in WEIGHTS], *[delta[n] for n in WEIGHTS],
            *[new_m[n] for n in WEIGHTS], *[new_v[n] for n in WEIGHTS])
```

```python
import functools
import math

import jax
import jax.numpy as jnp
from jax import lax
from jax.experimental import pallas as pl
from jax.experimental.pallas import tpu as pltpu
from jax.experimental.pallas import tpu_sc as plsc

F32 = jnp.float32
BF16 = jnp.bfloat16
MESH = pl.DeviceIdType.MESH

DEPTH = 4
HEADS = 8
NOPE = 128
ROPE = 64
VHEAD = 128
QK_HEAD = NOPE + ROPE
Q_RANK = 256
KV_RANK = 128
HEAD_PAD = 256
LAT_PAD = 512
ROPE_THETA = 10000.0
SGU_CHUNK = 128
SGU_GROUPS = 8
NORM_EPS = 1e-6
LN_EPS = 1e-5
ADAM_LR, ADAM_B1, ADAM_B2, ADAM_EPS, ADAM_WD, ADAM_STEP = 0.001, 0.9, 0.999, 1e-08, 0.01, 10

N_SHARDS = 4
LANE = 128
VMEM_LIMIT_BYTES = 56 * 1024 * 1024
ATT_TILE = 512
MM_TILE = 1024
ATT_SCALE = QK_HEAD ** -0.5
LOG2_SCALE = ATT_SCALE * math.log2(math.e)

NN = (((1,), (0,)), ((), ()))
NT = (((1,), (1,)), ((), ()))
TN = (((0,), (0,)), ((), ()))


def _params(sem):
    return pltpu.CompilerParams(dimension_semantics=sem, vmem_limit_bytes=VMEM_LIMIT_BYTES)


def _tile(n, pref):
    t = min(n, pref)
    while n % t:
        t //= 2
    return t


def _matmul(name, a, b, a_spec, b_spec, dims, grid, tile, outs, extras=(), epilogue=None, sums=()):
    nk, ne, no = grid[2], len(extras), len(outs)
    b_specs = list(b_spec) if isinstance(b_spec, (list, tuple)) else [b_spec]
    nb = len(b_specs)

    def body(a_ref, *rest):
        b_refs, e_refs, o_refs = rest[:nb], rest[nb:nb + ne], rest[nb + ne:nb + ne + no]
        s_refs = rest[nb + ne + no:nb + ne + no + len(sums)]
        kw = a_ref.shape[1] // nb
        part = None
        for p, b_ref in enumerate(b_refs):
            a_tile = a_ref[...] if nb == 1 else a_ref[:, p * kw:(p + 1) * kw]
            d = lax.dot_general(a_tile.astype(BF16), b_ref[...].astype(BF16), dims, preferred_element_type=F32)
            part = d if part is None else part + d

        def finish(acc):
            vals = (acc,) if epilogue is None else epilogue(acc, *[e[...] for e in e_refs])
            for o_ref, v in zip(o_refs, vals):
                o_ref[...] = v.astype(o_ref.dtype)
            first = pl.program_id(0) == 0
            for s_ref, v in zip(s_refs, vals[no:]):
                @pl.when(first)
                def _():
                    s_ref[...] = v

                @pl.when(jnp.logical_not(first))
                def _():
                    s_ref[...] += v

        if nk == 1:
            finish(part)
            return
        acc_ref, k = rest[-1], pl.program_id(2)

        @pl.when(k == 0)
        def _():
            acc_ref[...] = part

        @pl.when(jnp.logical_and(k > 0, k < nk - 1))
        def _():
            acc_ref[...] += part

        @pl.when(k == nk - 1)
        def _():
            finish(acc_ref[...] + part)

    assert not sums or (grid[1] == 1 and nk == 1)
    return pl.pallas_call(
        body, name=name, grid=grid,
        in_specs=[a_spec] + b_specs + [s for _, s in extras],
        out_specs=[s for _, s in outs] + [pl.BlockSpec(s, lambda i, j, k: (0,) * len(s)) for s in sums],
        out_shape=[s for s, _ in outs] + [jax.ShapeDtypeStruct(s, F32) for s in sums],
        scratch_shapes=[pltpu.VMEM(tile, F32)] if nk > 1 else [],
        compiler_params=_params(("arbitrary" if sums else "parallel", "parallel", "arbitrary")),
    )(a, *[b] * nb, *[e for e, _ in extras])


def _epilogue_operands(extras, consts, o_spec):
    return [(e, o_spec) for e in extras] + [(c, pl.BlockSpec(c.shape, lambda i, j, k: (0, 0))) for c in consts]


def _mm(name, a, b, out_dtypes=(F32,), epilogue=None, extras=(), tm=MM_TILE, tn=MM_TILE, tk=MM_TILE, nt=False,
        consts=(), sums=()):
    m, kd = a.shape
    n = b.shape[0] if nt else b.shape[1]
    tm, tn, tk = _tile(m, tm), _tile(n, tn), _tile(kd, tk)
    o_spec = pl.BlockSpec((tm, tn), lambda i, j, k: (i, j))
    b_spec = pl.BlockSpec((tn, tk), lambda i, j, k: (j, k)) if nt else pl.BlockSpec((tk, tn), lambda i, j, k: (k, j))
    return _matmul(name, a, b, pl.BlockSpec((tm, tk), lambda i, j, k: (i, k)), b_spec, NT if nt else NN,
                   (m // tm, n // tn, kd // tk), (tm, tn),
                   [(jax.ShapeDtypeStruct((m, n), d), o_spec) for d in out_dtypes],
                   _epilogue_operands(extras, consts, o_spec), epilogue, sums)


def _mm_tn(name, a, b, out_dtypes=(F32,), tm=MM_TILE, tn=MM_TILE, tk=MM_TILE):
    s, m = a.shape
    n = b.shape[1]
    tm, tn, tk = _tile(m, tm), _tile(n, tn), _tile(s, tk)
    o_spec = pl.BlockSpec((tm, tn), lambda i, j, k: (i, j))
    return _matmul(name, a, b, pl.BlockSpec((tk, tm), lambda i, j, k: (k, i)),
                   pl.BlockSpec((tk, tn), lambda i, j, k: (k, j)), TN, (m // tm, n // tn, s // tk), (tm, tn),
                   [(jax.ShapeDtypeStruct((m, n), d), o_spec) for d in out_dtypes])


def _mm_stacked(name, a, w3, mode, out_dtypes=(F32,), epilogue=None, extras=(), tm=MM_TILE, tn=MM_TILE, tk=MM_TILE):
    m, kd = a.shape
    _, r, c = w3.shape
    n = c if mode == "row" else N_SHARDS * c
    if mode == "row":
        tm, tn, tk = _tile(m, tm // 2), _tile(n, tn), kd
        b_spec = [pl.BlockSpec((None, r, tn), functools.partial(lambda i, j, k, p: (p, 0, j), p=p)) for p in range(N_SHARDS)]
    else:
        tm, tn, tk = _tile(m, tm), _tile(c, tn), _tile(kd, tk)
        per = c // tn
        b_spec = pl.BlockSpec((None, tk, tn), lambda i, j, k: (j // per, k, j % per))
    o_spec = pl.BlockSpec((tm, tn), lambda i, j, k: (i, j))
    return _matmul(name, a, w3, pl.BlockSpec((tm, tk), lambda i, j, k: (i, k)), b_spec, NN,
                   (m // tm, n // tn, kd // tk), (tm, tn),
                   [(jax.ShapeDtypeStruct((m, n), d), o_spec) for d in out_dtypes],
                   [(e, o_spec) for e in extras], epilogue)


def _mm_stacked_nt(name, a, w3, mode, out_dtypes=(F32,), epilogue=None, extras=(), tm=MM_TILE, tn=MM_TILE, tk=MM_TILE,
                   consts=(), sums=()):
    m, nd = a.shape
    _, r, c = w3.shape
    kout = N_SHARDS * r if mode == "row" else r
    if mode == "row":
        tm, tn, tk = _tile(m, tm), _tile(r, tn), _tile(c, tk)
        per = r // tn
        b_spec = pl.BlockSpec((None, tn, tk), lambda i, j, k: (j // per, j % per, k))
    else:
        tm, tn, tk = _tile(m, tm // 2), _tile(r, tn), nd
        b_spec = [pl.BlockSpec((None, tn, c), functools.partial(lambda i, j, k, p: (p, j, 0), p=p)) for p in range(N_SHARDS)]
    o_spec = pl.BlockSpec((tm, tn), lambda i, j, k: (i, j))
    return _matmul(name, a, w3, pl.BlockSpec((tm, tk), lambda i, j, k: (i, k)), b_spec, NT,
                   (m // tm, kout // tn, nd // tk), (tm, tn),
                   [(jax.ShapeDtypeStruct((m, kout), d), o_spec) for d in out_dtypes],
                   _epilogue_operands(extras, consts, o_spec), epilogue, sums)


def _mm_tn_stacked(name, a, b, shape3, mode, tm=MM_TILE, tn=MM_TILE, tk=MM_TILE):
    s, m = a.shape
    n = b.shape[1]
    _, r, c = shape3
    tk, tn = s, tn // 2
    if mode == "row":
        tm, tn = _tile(r, tm), _tile(n, tn)
        per = r // tm
        o_spec = pl.BlockSpec((None, tm, tn), lambda i, j, k: (i // per, i % per, j))
    else:
        tm, tn = _tile(m, tm), _tile(c, tn)
        per = c // tn
        o_spec = pl.BlockSpec((None, tm, tn), lambda i, j, k: (j // per, i, j % per))
    outs = [(jax.ShapeDtypeStruct(shape3, F32), o_spec), (jax.ShapeDtypeStruct(shape3, BF16), o_spec)]
    return _matmul(name, a, b, pl.BlockSpec((tk, tm), lambda i, j, k: (k, i)),
                   pl.BlockSpec((tk, tn), lambda i, j, k: (k, j)), TN, (m // tm, n // tn, s // tk), (tm, tn),
                   outs, epilogue=lambda acc: (acc, acc))


def _rowwise(name, fn, rows, consts, out_rows, out_accs=(), tr=256):
    nr, nc, no = len(rows), len(consts), len(out_rows)
    n_rows = rows[0].shape[0]
    tr = _tile(n_rows, tr)

    def body(*refs):
        vals = fn(*[r[...] for r in refs[:nr + nc]])
        o_refs, a_refs = refs[nr + nc:nr + nc + no], refs[nr + nc + no:]
        for ref, v in zip(o_refs, vals[:no]):
            ref[...] = v.astype(ref.dtype)
        first = pl.program_id(0) == 0

        @pl.when(first)
        def _():
            for ref, v in zip(a_refs, vals[no:]):
                ref[...] = v

        @pl.when(jnp.logical_not(first))
        def _():
            for ref, v in zip(a_refs, vals[no:]):
                ref[...] += v

    def whole(shape):
        return pl.BlockSpec(shape, lambda i: (0,) * len(shape))

    return pl.pallas_call(
        body, name=name, grid=(n_rows // tr,),
        in_specs=[pl.BlockSpec((tr, a.shape[1]), lambda i: (i, 0)) for a in rows] + [whole(c.shape) for c in consts],
        out_specs=[pl.BlockSpec((tr, f), lambda i: (i, 0)) for f, _ in out_rows] + [whole(s) for s in out_accs],
        out_shape=[jax.ShapeDtypeStruct((n_rows, f), d) for f, d in out_rows]
        + [jax.ShapeDtypeStruct(s, F32) for s in out_accs],
        compiler_params=_params(("arbitrary",)),
    )(*rows, *consts)


def _rms_fwd(x, g):
    return x * lax.rsqrt(jnp.mean(x * x, axis=-1, keepdims=True) + NORM_EPS) * g


def _rms_bwd(dy, x, g):
    rstd = lax.rsqrt(jnp.mean(x * x, axis=-1, keepdims=True) + NORM_EPS)
    n = x * rstd
    dn = dy * g
    dx = rstd * (dn - n * jnp.mean(dn * n, axis=-1, keepdims=True))
    return dx, jnp.sum(dy * n, axis=0, keepdims=True)


def _rope(x, cs, s1, s2):
    return x * cs + pltpu.roll(x, 32, 1) * s1 + pltpu.roll(x, 96, 1) * s2


def _rope_t(dy, cs, s1, s2):
    return dy * cs + pltpu.roll(dy * s1, 96, 1) + pltpu.roll(dy * s2, 32, 1)


def _gelu(z):
    return 0.5 * z * (1.0 + lax.erf(z * (1.0 / math.sqrt(2.0))))


def _gelu_and_grad(z):
    cdf = 0.5 * (1.0 + lax.erf(z * (1.0 / math.sqrt(2.0))))
    return z * cdf, cdf + z * jnp.exp(-0.5 * z * z) * (1.0 / math.sqrt(2.0 * math.pi))


def _att_scores(q, kv, kr, scale, masked, transposed):
    k = jnp.concatenate([kv[:, :NOPE], kr], axis=1)
    if transposed:
        s = lax.dot_general(k, q, NT, preferred_element_type=F32) * scale
    else:
        s = lax.dot_general(q, k, NT, preferred_element_type=F32) * scale
    if masked:
        r = lax.broadcasted_iota(jnp.int32, s.shape, 0)
        c = lax.broadcasted_iota(jnp.int32, s.shape, 1)
        s = jnp.where((r <= c) if transposed else (c <= r), s, -jnp.inf)
    return s, k


def _in_pairs(lo, hi, pair, single):
    n = hi - lo

    def body(p, carry):
        pair(lo + 2 * p, lo + 2 * p + 1)
        return carry

    lax.fori_loop(0, n // 2, body, 0)

    @pl.when(n % 2 == 1)
    def _():
        single(hi - 1)


def _causal_tiles(i, pair, single):
    @pl.when(i == 0)
    def _():
        single(i, True)

    @pl.when(i > 0)
    def _():
        _in_pairs(0, i - 1, lambda a, b: pair(a, b, False), lambda a: single(a, False))
        pair(i - 1, i, True)


def _flash_fwd(q, kvb, krb, tables):
    s_len = q.shape[0]
    t = ATT_TILE

    def body(q_ref, cs_ref, s1_ref, s2_ref, kv_ref, kr_ref, o_ref, lse_ref, qb_ref, m_s, l_s, acc_s):
        qi = pl.program_id(1)
        m_s[...] = jnp.full_like(m_s, -jnp.inf)
        l_s[...] = jnp.zeros_like(l_s)
        acc_s[...] = jnp.zeros_like(acc_s)
        qv = q_ref[...]
        q = jnp.concatenate([qv[:, :NOPE], _rope(qv[:, NOPE:], cs_ref[...], s1_ref[...], s2_ref[...])], axis=1).astype(BF16)
        qb_ref[...] = q

        def scores(ki, masked):
            rows = pl.ds(pl.multiple_of(ki * t, t), t)
            kv = kv_ref[rows, :]
            return _att_scores(q, kv, kr_ref[rows, :], LOG2_SCALE, masked, False)[0], kv

        def update(s, kv):
            m_prev = m_s[...]
            m_new = jnp.maximum(m_prev, jnp.max(s, axis=1, keepdims=True))
            alpha = jnp.exp2(m_prev - m_new)
            p = jnp.exp2(s - jnp.tile(m_new, (1, t // LANE)))
            l_s[...] = alpha * l_s[...] + jnp.sum(p, axis=1, keepdims=True)
            acc_s[...] = alpha * acc_s[...] + jnp.dot(p.astype(BF16), kv[:, NOPE:], preferred_element_type=F32)
            m_s[...] = m_new

        def pair(k0, k1, masked):
            first, second = scores(k0, False), scores(k1, masked)
            update(*first)
            update(*second)

        _causal_tiles(qi, pair, lambda ki, masked: update(*scores(ki, masked)))
        o_ref[...] = (acc_s[...] / l_s[...]).astype(o_ref.dtype)
        lse_ref[...] = (m_s[...] + jnp.log2(l_s[...]))[:, :1]

    table = pl.BlockSpec((t, LANE), lambda h, qi: (qi, 0))
    return pl.pallas_call(
        body, name="flash_fwd", grid=(HEADS, s_len // t),
        in_specs=[pl.BlockSpec((t, HEAD_PAD), lambda h, qi: (qi, h)), table, table, table,
                  pl.BlockSpec((s_len, HEAD_PAD), lambda h, qi: (0, h)),
                  pl.BlockSpec((s_len, LANE), lambda h, qi: (0, 0))],
        out_specs=[pl.BlockSpec((t, VHEAD), lambda h, qi: (qi, h)),
                   pl.BlockSpec((None, t, 1), lambda h, qi: (h, qi, 0)),
                   pl.BlockSpec((t, HEAD_PAD), lambda h, qi: (qi, h))],
        out_shape=[jax.ShapeDtypeStruct((s_len, HEADS * VHEAD), BF16),
                   jax.ShapeDtypeStruct((HEADS, s_len, 1), F32),
                   jax.ShapeDtypeStruct((s_len, HEADS * HEAD_PAD), BF16)],
        scratch_shapes=[pltpu.VMEM((t, LANE), F32), pltpu.VMEM((t, LANE), F32), pltpu.VMEM((t, VHEAD), F32)],
        compiler_params=_params(("parallel", "arbitrary")),
    )(q, *tables, kvb, krb)


def _flash_bwd_dq(qb, kvb, krb, dob, lse, delta, tables):
    s_len = qb.shape[0]
    t = ATT_TILE
    nq = s_len // t
    scale = QK_HEAD ** -0.5

    def body(q_ref, kv_ref, kr_ref, do_ref, lse_ref, dl_ref, cs_ref, s1_ref, s2_ref, dq_ref, acc_s):
        qi = pl.program_id(1)
        acc_s[...] = jnp.zeros_like(acc_s)
        q, do = q_ref[...], do_ref[...]
        lse = jnp.broadcast_to(lse_ref[...], (t, LANE))
        dl = dl_ref[...]

        def products(ki, masked):
            rows = pl.ds(pl.multiple_of(ki * t, t), t)
            kv = kv_ref[rows, :]
            s, k = _att_scores(q, kv, kr_ref[rows, :], LOG2_SCALE, masked, False)
            return s, lax.dot_general(do, kv[:, NOPE:], NT, preferred_element_type=F32), k

        def update(s, dp, k):
            p = jnp.exp2(s - jnp.tile(lse, (1, t // LANE)))
            ds = (p * (dp - jnp.tile(dl, (1, t // LANE))) * scale).astype(BF16)
            acc_s[...] += jnp.dot(ds, k, preferred_element_type=F32)

        def pair(k0, k1, masked):
            first, second = products(k0, False), products(k1, masked)
            update(*first)
            update(*second)

        _causal_tiles(qi, pair, lambda ki, masked: update(*products(ki, masked)))
        dq = acc_s[...]
        dq = jnp.concatenate([dq[:, :NOPE], _rope_t(dq[:, NOPE:], cs_ref[...], s1_ref[...], s2_ref[...])], axis=1)
        dq_ref[...] = dq.astype(dq_ref.dtype)

    col = pl.BlockSpec((None, t, 1), lambda h, qi: (h, qi, 0))
    table = pl.BlockSpec((t, LANE), lambda h, qi: (qi, 0))
    return pl.pallas_call(
        body, name="flash_bwd_dq", grid=(HEADS, nq),
        in_specs=[pl.BlockSpec((t, HEAD_PAD), lambda h, qi: (qi, h)),
                  pl.BlockSpec((s_len, HEAD_PAD), lambda h, qi: (0, h)),
                  pl.BlockSpec((s_len, LANE), lambda h, qi: (0, 0)),
                  pl.BlockSpec((t, VHEAD), lambda h, qi: (qi, h)), col, pl.BlockSpec((t, VHEAD), lambda h, qi: (qi, h)),
                  table, table, table],
        out_specs=pl.BlockSpec((t, HEAD_PAD), lambda h, qi: (qi, h)),
        out_shape=jax.ShapeDtypeStruct((s_len, HEADS * HEAD_PAD), BF16),
        scratch_shapes=[pltpu.VMEM((t, HEAD_PAD), F32)],
        compiler_params=_params(("parallel", "arbitrary")),
    )(qb, kvb, krb, dob, lse, delta, *tables)


def _flash_bwd_dkv(qb, kvb, krb, dob, lse_row, delta_row):
    s_len = qb.shape[0]
    t = ATT_TILE
    nq = s_len // t
    scale = QK_HEAD ** -0.5

    def body(q_ref, kv_ref, kr_ref, do_ref, lse_ref, dl_ref, dkv_ref, dkr_ref, dk_s, dv_s):
        ki = pl.program_id(1)
        dk_s[...] = jnp.zeros_like(dk_s)
        dv_s[...] = jnp.zeros_like(dv_s)
        kv, kr = kv_ref[...], kr_ref[...]

        def products(qi, masked):
            rows = pl.ds(pl.multiple_of(qi * t, t), t)
            q, do = q_ref[rows, :], do_ref[rows, :]
            st, _ = _att_scores(q, kv, kr, LOG2_SCALE, masked, True)
            return st, lax.dot_general(kv[:, NOPE:], do, NT, preferred_element_type=F32), q, do, rows

        def update(st, dpt, q, do, rows):
            pt = jnp.exp2(st - lse_ref[:, rows])
            dv_s[...] += jnp.dot(pt.astype(BF16), do, preferred_element_type=F32)
            dst = (pt * (dpt - dl_ref[:, rows]) * scale).astype(BF16)
            dk_s[...] += jnp.dot(dst, q, preferred_element_type=F32)

        def pair(q0, q1, masked):
            first, second = products(q0, masked), products(q1, False)
            update(*first)
            update(*second)

        @pl.when(ki == nq - 1)
        def _():
            update(*products(ki, True))

        @pl.when(ki < nq - 1)
        def _():
            pair(ki, ki + 1, True)
            _in_pairs(ki + 2, nq, lambda a, b: pair(a, b, False), lambda qi: update(*products(qi, False)))

        dk = dk_s[...]
        dkv_ref[...] = jnp.concatenate([dk[:, :NOPE], dv_s[...]], axis=1).astype(dkv_ref.dtype)
        dkr_ref[...] = dk[:, NOPE:]

    row = pl.BlockSpec((None, 1, s_len), lambda h, ki: (h, 0, 0))
    return pl.pallas_call(
        body, name="flash_bwd_dkv", grid=(HEADS, nq),
        in_specs=[pl.BlockSpec((s_len, HEAD_PAD), lambda h, ki: (0, h)),
                  pl.BlockSpec((t, HEAD_PAD), lambda h, ki: (ki, h)),
                  pl.BlockSpec((t, LANE), lambda h, ki: (ki, 0)),
                  pl.BlockSpec((s_len, VHEAD), lambda h, ki: (0, h)), row, row],
        out_specs=[pl.BlockSpec((t, HEAD_PAD), lambda h, ki: (ki, h)),
                   pl.BlockSpec((t, LANE), lambda h, ki: (ki, h))],
        out_shape=[jax.ShapeDtypeStruct((s_len, HEADS * HEAD_PAD), BF16),
                   jax.ShapeDtypeStruct((s_len, HEADS * LANE), F32)],
        scratch_shapes=[pltpu.VMEM((t, HEAD_PAD), F32), pltpu.VMEM((t, VHEAD), F32)],
        compiler_params=_params(("parallel", "parallel")),
    )(qb, kvb, krb, dob, lse_row, delta_row)


def _tril(w):
    r = lax.broadcasted_iota(jnp.int32, w.shape, 0)
    c = lax.broadcasted_iota(jnp.int32, w.shape, 1)
    return jnp.where(c <= r, w, 0.0)


def _sgu_mix(w_ref, vln, gd):
    wcs = [_tril(w_ref[g]).astype(BF16) for g in range(SGU_GROUPS)]
    mixed = jnp.concatenate(
        [jnp.dot(wcs[g], vln[:, g * gd:(g + 1) * gd], preferred_element_type=F32) for g in range(SGU_GROUPS)], axis=1)
    return wcs, mixed


def _sgu_fwd(zpre, ln_g, ln_b, w_sp, bias_full):
    s_len, two_w = zpre.shape
    width = two_w // 2
    gd = width // SGU_GROUPS
    t = SGU_CHUNK

    def body(z_ref, g_ref, b_ref, w_ref, bias_ref, uv_ref):
        u = _gelu(z_ref[:, :width])
        v = _gelu(z_ref[:, width:])
        d = v - jnp.mean(v, axis=-1, keepdims=True)
        vhat = d * lax.rsqrt(jnp.mean(d * d, axis=-1, keepdims=True) + LN_EPS)
        vln = (vhat * g_ref[...] + b_ref[...]).astype(BF16)
        _, mixed = _sgu_mix(w_ref, vln, gd)
        uv_ref[...] = (u * (mixed + bias_ref[...])).astype(uv_ref.dtype)

    return pl.pallas_call(
        body, name="sgu_fwd", grid=(s_len // t,),
        in_specs=[pl.BlockSpec((t, two_w), lambda i: (i, 0)), pl.BlockSpec((1, width), lambda i: (0, 0)),
                  pl.BlockSpec((1, width), lambda i: (0, 0)), pl.BlockSpec(w_sp.shape, lambda i: (0, 0, 0)),
                  pl.BlockSpec((t, width), lambda i: (0, 0))],
        out_specs=pl.BlockSpec((t, width), lambda i: (i, 0)),
        out_shape=jax.ShapeDtypeStruct((s_len, width), BF16),
        compiler_params=_params(("parallel",)),
    )(zpre, ln_g, ln_b, w_sp, bias_full)


def _sgu_bwd(zpre, duv, ln_g, ln_b, w_sp, bias_full):
    s_len, two_w = zpre.shape
    width = two_w // 2
    gd = width // SGU_GROUPS
    t = SGU_CHUNK

    def body(z_ref, duv_ref, g_ref, b_ref, w_ref, bias_ref, dz_ref, dg_ref, db_ref, dw_ref, dbias_ref):
        first = pl.program_id(0) == 0

        def accumulate(ref, val):
            @pl.when(first)
            def _():
                ref[...] = val

            @pl.when(jnp.logical_not(first))
            def _():
                ref[...] += val

        u, u_grad = _gelu_and_grad(z_ref[:, :width])
        v, v_grad = _gelu_and_grad(z_ref[:, width:])
        d = v - jnp.mean(v, axis=-1, keepdims=True)
        rstd = lax.rsqrt(jnp.mean(d * d, axis=-1, keepdims=True) + LN_EPS)
        vhat = d * rstd
        vln = (vhat * g_ref[...] + b_ref[...]).astype(BF16)
        wcs, mixed = _sgu_mix(w_ref, vln, gd)
        duv_v = duv_ref[...]
        du = duv_v * (mixed + bias_ref[...])
        dmixed = duv_v * u
        dmb = dmixed.astype(BF16)
        dvln = jnp.concatenate(
            [lax.dot_general(wcs[g], dmb[:, g * gd:(g + 1) * gd], TN, preferred_element_type=F32)
             for g in range(SGU_GROUPS)], axis=1)
        for g in range(SGU_GROUPS):
            dw = lax.dot_general(dmb[:, g * gd:(g + 1) * gd], vln[:, g * gd:(g + 1) * gd], NT, preferred_element_type=F32)
            accumulate(dw_ref.at[g], _tril(dw))
        dvhat = dvln * g_ref[...]
        dv0 = rstd * (dvhat - jnp.mean(dvhat, axis=-1, keepdims=True)
                      - vhat * jnp.mean(dvhat * vhat, axis=-1, keepdims=True))
        dz_ref[:, :width] = (du * u_grad).astype(dz_ref.dtype)
        dz_ref[:, width:] = (dv0 * v_grad).astype(dz_ref.dtype)
        accumulate(dg_ref, jnp.sum(dvln * vhat, axis=0, keepdims=True))
        accumulate(db_ref, jnp.sum(dvln, axis=0, keepdims=True))
        accumulate(dbias_ref, dmixed)

    vec = pl.BlockSpec((1, width), lambda i: (0, 0))
    return pl.pallas_call(
        body, name="sgu_bwd", grid=(s_len // t,),
        in_specs=[pl.BlockSpec((t, two_w), lambda i: (i, 0)), pl.BlockSpec((t, width), lambda i: (i, 0)), vec, vec,
                  pl.BlockSpec(w_sp.shape, lambda i: (0, 0, 0)), pl.BlockSpec((t, width), lambda i: (0, 0))],
        out_specs=[pl.BlockSpec((t, two_w), lambda i: (i, 0)), vec, vec,
                   pl.BlockSpec(w_sp.shape, lambda i: (0, 0, 0)), pl.BlockSpec((t, width), lambda i: (0, 0))],
        out_shape=[jax.ShapeDtypeStruct((s_len, two_w), BF16), jax.ShapeDtypeStruct((1, width), F32),
                   jax.ShapeDtypeStruct((1, width), F32), jax.ShapeDtypeStruct(w_sp.shape, F32),
                   jax.ShapeDtypeStruct((t, width), F32)],
        compiler_params=_params(("arbitrary",)),
    )(zpre, duv, ln_g, ln_b, w_sp, bias_full)


def _rope_tables(positions):
    inv_freq = ROPE_THETA ** (-jnp.arange(0, ROPE, 2, dtype=F32) / ROPE)
    ang = positions.astype(F32)[:, None] * inv_freq
    cos, sin = jnp.cos(ang), jnp.sin(ang)
    z32, z64 = jnp.zeros_like(cos), jnp.zeros((cos.shape[0], LANE - ROPE), F32)
    return (jnp.concatenate([cos, cos, z64], axis=1), jnp.concatenate([z32, sin, z64], axis=1),
            jnp.concatenate([-sin, z32, z64], axis=1))


def _ffn_fwd(x, g, w_up3, w_down3):
    h2 = _rowwise("ffn_norm", lambda xv, gv: (_rms_fwd(xv, gv),), [x], [g], [(x.shape[1], BF16)])[0]

    def sq_relu(acc):
        r = jnp.maximum(acc, 0.0)
        return r * r, 2.0 * r

    r, r_grad = _mm_stacked("ffn_up", h2, w_up3, "col", (BF16, BF16), sq_relu)
    x_out = _mm_stacked("ffn_down", r, w_down3, "row", (F32,), lambda acc, res: (acc + res,), [x])[0]
    return x_out, (x, h2, r, r_grad)


def _ffn_bwd(dx, dxb, saved, g, w_up3, w_down3):
    x, h2, r, r_grad = saved
    da = _mm_stacked_nt("ffn_down_dx", dxb, w_down3, "row", (BF16,),
                        lambda acc, rg: (acc * rg.astype(F32),), [r_grad])[0]
    g_down = _mm_tn_stacked("ffn_down_dw", r, dxb, w_down3.shape, "row")
    dx, dxb, dg = _mm_stacked_nt("ffn_up_dx", da, w_up3, "col", **_norm_bwd(x, g, dx))
    g_up = _mm_tn_stacked("ffn_up_dw", h2, da, w_up3.shape, "col")
    return dx, dxb, dg, g_up, g_down


def _norm_bwd(x, g, dres):
    def epilogue(dh, xv, rv, gv):
        dxv, dg = _rms_bwd(dh, xv, gv)
        return dxv + rv, dxv + rv, dg

    return dict(out_dtypes=(F32, BF16), epilogue=epilogue, extras=[x, dres], consts=[g], sums=[g.shape], tm=MM_TILE // 2)


def _mla_fwd(x, g, wdkv, q_norm, kv_norm, wq, wkv, wo, tables):
    d = x.shape[1]
    h = _rowwise("mla_norm", lambda xv, gv: (_rms_fwd(xv, gv),), [x], [g], [(d, BF16)])[0]
    lat = _mm("mla_dkv", h, wdkv)[0]

    def lat_post(lv, cs, s1, s2, qg, kg):
        return (_rms_fwd(lv[:, :Q_RANK], qg), _rms_fwd(lv[:, Q_RANK:Q_RANK + KV_RANK], kg),
                _rope(lv[:, Q_RANK + KV_RANK:], cs, s1, s2))

    cqn, ckvn, krb = _rowwise("mla_lat", lat_post, [lat, *tables], [q_norm, kv_norm],
                              [(Q_RANK, BF16), (KV_RANK, BF16), (LANE, BF16)])
    q = _mm("mla_uq", cqn, wq)[0]
    kvb = _mm("mla_ukv", ckvn, wkv, (BF16,))[0]
    ob, lse, qb = _flash_fwd(q, kvb, krb, tables)
    x_mid = _mm("mla_o", ob, wo, (F32,), lambda acc, res: (acc + res,), [x])[0]
    return x_mid, (x, h, lat, cqn, ckvn, krb, qb, kvb, ob, lse)


def _mla_bwd(dx, dxb, saved, g, wdkv, q_norm, kv_norm, wq, wkv, wo, tables):
    x, h, lat, cqn, ckvn, krb, qb, kvb, ob, lse = saved
    s_len = x.shape[0]
    def with_delta(do, ov):
        prod = do * ov.astype(F32)
        sums = [jnp.sum(prod[:, hd * VHEAD:(hd + 1) * VHEAD], axis=1, keepdims=True) for hd in range(HEADS)]
        return do, jnp.concatenate([jnp.broadcast_to(s, (do.shape[0], VHEAD)) for s in sums], axis=1)

    dob, delta = _mm("mla_o_dx", dxb, wo, (BF16, F32), with_delta, [ob], nt=True, tm=MM_TILE // 2)
    g_wo = _mm_tn("mla_o_dw", ob, dxb)[0]
    delta_row = delta.reshape(s_len, HEADS, VHEAD)[:, :, 0].T.reshape(HEADS, 1, s_len)
    dqb = _flash_bwd_dq(qb, kvb, krb, dob, lse, delta, tables)
    dkvb, dkr = _flash_bwd_dkv(qb, kvb, krb, dob, lse.reshape(HEADS, 1, s_len), delta_row)
    dcqn = _mm("mla_uq_dx", dqb, wq, nt=True)[0]
    g_wq = _mm_tn("mla_uq_dw", cqn, dqb)[0]
    dckvn = _mm("mla_ukv_dx", dkvb, wkv, nt=True)[0]
    g_wkv = _mm_tn("mla_ukv_dw", ckvn, dkvb)[0]

    def lat_bwd(dq_v, dkv_v, dkr_v, lv, cs, s1, s2, qg, kg):
        dcq, dqg = _rms_bwd(dq_v, lv[:, :Q_RANK], qg)
        dckv, dkg = _rms_bwd(dkv_v, lv[:, Q_RANK:Q_RANK + KV_RANK], kg)
        dkr_sum = dkr_v[:, :LANE]
        for hd in range(1, HEADS):
            dkr_sum = dkr_sum + dkr_v[:, hd * LANE:(hd + 1) * LANE]
        return jnp.concatenate([dcq, dckv, _rope_t(dkr_sum, cs, s1, s2)], axis=1), dqg, dkg

    dlat, g_qn, g_kvn = _rowwise("mla_lat_bwd", lat_bwd, [dcqn, dckvn, dkr, lat, *tables], [q_norm, kv_norm],
                                 [(LAT_PAD, BF16)], [q_norm.shape, kv_norm.shape])
    g_wdkv = _mm_tn("mla_dkv_dw", h, dlat)[0]
    dx, dxb, dg = _mm("mla_dkv_dx", dlat, wdkv, nt=True, **_norm_bwd(x, g, dx))
    return dx, dxb, dg, g_wdkv, g_qn, g_kvn, g_wq, g_wkv, g_wo


def _sgu_layer_fwd(x, g, w_in3, ln_g, ln_b, w_sp, bias_full, w_out3):
    h = _rowwise("sgu_norm", lambda xv, gv: (_rms_fwd(xv, gv),), [x], [g], [(x.shape[1], BF16)])[0]
    zpre = _mm_stacked("sgu_in", h, w_in3, "col")[0]
    uv = _sgu_fwd(zpre, ln_g, ln_b, w_sp, bias_full)
    x_mid = _mm_stacked("sgu_out", uv, w_out3, "row", (F32,), lambda acc, res: (acc + res,), [x])[0]
    return x_mid, (x, h, zpre, uv)


def _sgu_layer_bwd(dx, dxb, saved, g, w_in3, ln_g, ln_b, w_sp, bias_full, w_out3):
    x, h, zpre, uv = saved
    duv = _mm_stacked_nt("sgu_out_dx", dxb, w_out3, "row")[0]
    g_out = _mm_tn_stacked("sgu_out_dw", uv, dxb, w_out3.shape, "row")
    dz, g_lng, g_lnb, g_wsp, g_bias = _sgu_bwd(zpre, duv, ln_g, ln_b, w_sp, bias_full)
    g_in = _mm_tn_stacked("sgu_in_dw", h, dz, w_in3.shape, "col")
    dx, dxb, dg = _mm_stacked_nt("sgu_in_dx", dz, w_in3, "col", **_norm_bwd(x, g, dx))
    return dx, dxb, dg, g_in, g_out, g_lng, g_lnb, g_wsp, g_bias


def _loss_head(x, target, g):
    d = x.shape[1]

    def fn(xv, tv, gv):
        err = _rms_fwd(xv, gv) - tv
        dxv, dg = _rms_bwd(err * (1.0 / d), xv, gv)
        return dxv, dxv, dg, jnp.sum(err * err, axis=0, keepdims=True)

    return _rowwise("loss_head", fn, [x, target], [g], [(d, F32), (d, BF16)], [g.shape, g.shape])


def _mixer_weights(i, stacks):
    by_rows = lambda a: a.reshape(N_SHARDS * a.shape[1], a.shape[2])
    by_cols = lambda a: a.transpose(1, 0, 2).reshape(a.shape[1], N_SHARDS * a.shape[2])
    if i % 2:
        w_in3, w_out3, ln_g, ln_b = stacks
        return w_in3, ln_g.reshape(1, -1), ln_b.reshape(1, -1), w_out3
    wdkv = by_rows(stacks[0])
    wdkv = jnp.pad(wdkv, ((0, 0), (0, LAT_PAD - wdkv.shape[1])))
    wq = jnp.pad(by_cols(stacks[1]).reshape(Q_RANK, HEADS, QK_HEAD), ((0, 0), (0, 0), (0, HEAD_PAD - QK_HEAD)))
    return wdkv, wq.reshape(Q_RANK, HEADS * HEAD_PAD), by_cols(stacks[2]), by_rows(stacks[3])


def _local_step(x, positions, target, norm_mix, norm_ffn, final_norm, q_norm, kv_norm, w_sp, b_sp, mixers, ffn, reducer):
    tables = _rope_tables(positions)
    gd = mixers[1][2].size // SGU_GROUPS
    bias_full = [jnp.repeat(b_sp[j].T, gd, axis=1) for j in range(DEPTH // 2)]
    saved, mla, sgu = [], [None] * (DEPTH // 2), [None] * (DEPTH // 2)
    for i in range(DEPTH):
        j = i // 2
        x, *stacks = lax.optimization_barrier((x, *mixers[i]))
        if i % 2 == 0:
            wdkv, wq, wkv, wo = mla[j] = _mixer_weights(i, stacks)
            x, s_mix = _mla_fwd(x, norm_mix[i:i + 1], wdkv, q_norm[j:j + 1], kv_norm[j:j + 1], wq, wkv, wo, tables)
        else:
            w_in3, ln_g, ln_b, w_out3 = sgu[j] = _mixer_weights(i, stacks)
            x, s_mix = _sgu_layer_fwd(x, norm_mix[i:i + 1], w_in3, ln_g, ln_b, w_sp[j], bias_full[j], w_out3)
        x, s_ffn = _ffn_fwd(x, norm_ffn[i:i + 1], *ffn[i])
        saved.append((s_mix, s_ffn))
    dx, dxb, g_final, sq_cols = _loss_head(x, target, final_norm[None, :])
    loss = 0.5 * jnp.sum(sq_cols) / x.shape[1]

    def pair(g):
        return g, g.astype(BF16)

    g_mix, g_ffn = [None] * DEPTH, [None] * DEPTH
    mla_g, sgu_g = [None] * (DEPTH // 2), [None] * (DEPTH // 2)
    for i in reversed(range(DEPTH)):
        j = i // 2
        s_mix, s_ffn = saved[i]
        dx, dxb, g_ffn[i], g_up, g_down = _ffn_bwd(dx, dxb, s_ffn, norm_ffn[i:i + 1], *ffn[i])
        dxb = reducer.add(f"ffn{i}", i, {"ffn_w_up": g_up, "ffn_w_down": g_down}, dxb)
        dxb = reducer.phase_end(dxb)
        if i % 2 == 0:
            wdkv, wq, wkv, wo = mla[j]
            dx, dxb, g_mix[i], g_wdkv, g_qn, g_kvn, g_wq, g_wkv, g_wo = _mla_bwd(
                dx, dxb, s_mix, norm_mix[i:i + 1], wdkv, q_norm[j:j + 1], kv_norm[j:j + 1], wq, wkv, wo, tables)
            mla_g[j] = (g_qn, g_kvn)
            g_wq = g_wq.reshape(Q_RANK, HEADS, HEAD_PAD)[..., :QK_HEAD].reshape(Q_RANK, N_SHARDS, -1)
            dxb = reducer.add(f"mla{j}", j, {
                "mla_w_dkv": pair(g_wdkv[:, :Q_RANK + KV_RANK + ROPE].reshape(N_SHARDS, -1, Q_RANK + KV_RANK + ROPE)),
                "mla_w_uq": pair(g_wq.transpose(1, 0, 2)),
                "mla_w_ukv": pair(g_wkv.reshape(KV_RANK, N_SHARDS, -1).transpose(1, 0, 2)),
                "mla_w_o": pair(g_wo.reshape(N_SHARDS, -1, g_wo.shape[1]))}, dxb)
        else:
            w_in3, ln_g, ln_b, w_out3 = sgu[j]
            dx, dxb, g_mix[i], g_in, g_out, g_lng, g_lnb, g_wsp, g_bias = _sgu_layer_bwd(
                dx, dxb, s_mix, norm_mix[i:i + 1], w_in3, ln_g, ln_b, w_sp[j], bias_full[j], w_out3)
            sgu_g[j] = (g_wsp, g_bias.reshape(SGU_CHUNK, SGU_GROUPS, gd).sum(axis=-1).T)
            dxb = reducer.add(f"sgu{j}", j, {"sgu_w_in": g_in, "sgu_w_out": g_out,
                                             "sgu_ln_g": pair(g_lng.reshape(N_SHARDS, -1, LANE)),
                                             "sgu_ln_b": pair(g_lnb.reshape(N_SHARDS, -1, LANE))}, dxb)
        dxb = reducer.phase_end(dxb)
    small = dict(
        norm_mix=jnp.concatenate(g_mix, axis=0), norm_ffn=jnp.concatenate(g_ffn, axis=0), final_norm=g_final[0],
        q_norm=jnp.concatenate([m[0] for m in mla_g], axis=0), kv_norm=jnp.concatenate([m[1] for m in mla_g], axis=0),
        w_sp=jnp.stack([s[0] for s in sgu_g]), b_sp=jnp.stack([s[1] for s in sgu_g]))
    return loss, dx, small


HBM_SPEC = pl.BlockSpec(memory_space=pltpu.HBM)


def _place():
    x, y, c = lax.axis_index("x"), lax.axis_index("y"), lax.axis_index("c")
    return x, y, c, [(1 - x, y), (x, 1 - y), (1 - x, 1 - y)]


def _remote(src, dst, send_sems, recv_sems, k, to):
    return pltpu.make_async_remote_copy(src_ref=src, dst_ref=dst, send_sem=send_sems.at[k], recv_sem=recv_sems.at[k],
                                        device_id=to, device_id_type=MESH)


def _gather_layer(tag, shards):
    n = len(shards)
    split = [s.shape[0] >= 16 for s in shards]

    def body(*refs):
        ins, outs = refs[:n], refs[n:2 * n]
        send_sems, recv_sems, local_sems = refs[2 * n:]
        x, y, c, chips = _place()
        mine = 2 * x + y
        barrier = pltpu.get_barrier_semaphore()
        peers = [(x, y, 1 - c)] + [(*chip, c) for chip in chips]
        for peer in peers:
            pl.semaphore_signal(barrier, inc=1, device_id=peer, device_id_type=MESH)
        pl.semaphore_wait(barrier, len(peers))

        def rows(t, half):
            hr = shards[t].shape[0] // 2
            return pl.ds(half * hr, hr) if split[t] else pl.ds(0, shards[t].shape[0])

        local, sent = [], []
        for t in range(n):
            local.append(pltpu.make_async_copy(ins[t], outs[t].at[mine], local_sems.at[t]))
            local[-1].start()
            for j, chip in enumerate(chips):
                cp = _remote(ins[t].at[rows(t, c)], outs[t].at[mine, rows(t, c)], send_sems, recv_sems, 3 * t + j, (*chip, c))
                cp.start()
                sent.append(cp)
        for j, chip in enumerate(chips):
            theirs = 2 * chip[0] + chip[1]
            for t in range(n):
                piece = outs[t].at[theirs, rows(t, c)]
                _remote(piece, piece, send_sems, recv_sems, 3 * t + j, (x, y, c)).wait_recv()
                if split[t]:
                    cp = _remote(piece, piece, send_sems, recv_sems, 3 * n + 3 * t + j, (x, y, 1 - c))
                    cp.start()
                    sent.append(cp)
        for j, chip in enumerate(chips):
            theirs = 2 * chip[0] + chip[1]
            for t in range(n):
                if split[t]:
                    piece = outs[t].at[theirs, rows(t, 1 - c)]
                    _remote(piece, piece, send_sems, recv_sems, 3 * n + 3 * t + j, (x, y, c)).wait_recv()
        for cp in sent:
            cp.wait_send()
        for cp in local:
            cp.wait()

    return pl.kernel(
        body, name=f"gather_{tag}", mesh=plsc.ScalarSubcoreMesh(axis_name="sequencer", num_cores=1),
        out_type=[jax.ShapeDtypeStruct((N_SHARDS, *s.shape), s.dtype) for s in shards],
        scratch_types=[pltpu.SemaphoreType.DMA((6 * n,)), pltpu.SemaphoreType.DMA((6 * n,)), pltpu.SemaphoreType.DMA((n,))],
        compiler_params=pltpu.CompilerParams(collective_id=ID_GATHER),
    )(*shards)


SEQUENCER = dict(axis_name="sequencer", num_cores=1)
ID_GATHER, ID_EXCHANGE, ID_SHARE = 0, 1, 2
MIN_SPLIT_ROWS = 16


def _handshake(peers):
    barrier = pltpu.get_barrier_semaphore()
    for peer in peers:
        pl.semaphore_signal(barrier, inc=1, device_id=peer, device_id_type=MESH)
    pl.semaphore_wait(barrier, len(peers))


def _half_rows(rows, half):
    return pl.ds(half * (rows // 2), rows // 2) if rows >= MIN_SPLIT_ROWS else pl.ds(0, rows)


SEM_SPEC = pl.BlockSpec(memory_space=pltpu.SEMAPHORE)
DATAFLOW = pltpu.SideEffectType.DATAFLOW_SIDE_EFFECTING


def _exchange_copies(shapes, stacks, lands, send_sems, recv_sems):
    x, y, c, chips = _place()
    mine = 2 * x + y
    copies = []
    for t, shape in enumerate(shapes):
        r = shape[1]
        copies.append(_remote(stacks[t].at[mine, _half_rows(r, 1 - c)], lands[t].at[0], send_sems, recv_sems, 7 * t, (x, y, 1 - c)))
        for j, chip in enumerate(chips):
            theirs = 2 * chip[0] + chip[1]
            copies.append(_remote(stacks[t].at[theirs, _half_rows(r, c)], lands[t].at[1 + j], send_sems, recv_sems,
                                  7 * t + 1 + j, (*chip, c)))
            copies.append(_remote(stacks[t].at[theirs, _half_rows(r, 1 - c)], lands[t].at[4 + j], send_sems, recv_sems,
                                  7 * t + 4 + j, (*chip, 1 - c)))
    return copies


def _exchange_start(tag, stacks, carry):
    n = len(stacks)
    shapes = [s.shape for s in stacks]
    lands = [lax.empty((7, s.shape[1] // 2 if s.shape[1] >= MIN_SPLIT_ROWS else s.shape[1], s.shape[2]), s.dtype) for s in stacks]

    def body(*refs):
        for cp in _exchange_copies(shapes, refs[:n], refs[n:2 * n], refs[2 * n + 1], refs[2 * n + 2]):
            cp.start()

    through = (*stacks, *lands, carry)
    out = pl.pallas_call(
        body, name=f"reduce_exchange_start_{tag}",
        out_shape=(pltpu.SemaphoreType.DMA((7 * n,)), pltpu.SemaphoreType.DMA((7 * n,)),
                   *[pltpu.HBM(a.shape, a.dtype) for a in through]),
        in_specs=[HBM_SPEC] * (2 * n + 1),
        out_specs=(SEM_SPEC, SEM_SPEC, *[HBM_SPEC] * (2 * n + 1)),
        input_output_aliases={t: 2 + t for t in range(2 * n + 1)},
        compiler_params=pltpu.CompilerParams(has_side_effects=DATAFLOW),
    )(*[pltpu.with_memory_space_constraint(a, pltpu.HBM) for a in through])
    return out[0], out[1], out[2:2 + n], out[2 + n:2 + 2 * n], out[-1]


def _exchange_wait(tag, send_sems, recv_sems, stacks, lands, after):
    n = len(stacks)
    shapes = [s.shape for s in stacks]

    def body(*refs):
        for cp in _exchange_copies(shapes, refs[:n], refs[n:2 * n], refs[2 * n], refs[2 * n + 1]):
            cp.wait()

    out = pl.pallas_call(
        body, name=f"reduce_exchange_wait_{tag}",
        out_shape=tuple(pltpu.HBM(a.shape, a.dtype) for a in (*stacks, *lands)),
        in_specs=[HBM_SPEC] * (2 * n) + [SEM_SPEC, SEM_SPEC, pl.BlockSpec(memory_space=pl.ANY)],
        out_specs=tuple([HBM_SPEC] * (2 * n)),
        input_output_aliases={t: t for t in range(2 * n)},
        compiler_params=pltpu.CompilerParams(has_side_effects=DATAFLOW),
    )(*stacks, *lands, send_sems, recv_sems, after)
    return out[:n], out[n:]


def _share_halves(tag, halves):
    n = len(halves)

    def body(*refs):
        ins, outs, send_sems, recv_sems = refs[:n], refs[n:2 * n], refs[2 * n], refs[2 * n + 1]
        x, y, c, _ = _place()
        _handshake([(x, y, 1 - c)])
        sent = [_remote(ins[t], outs[t], send_sems, recv_sems, t, (x, y, 1 - c)) for t in range(n)]
        for cp in sent:
            cp.start()
        for cp in sent:
            cp.wait()

    return pl.kernel(
        body, name=f"reduce_share_{tag}", mesh=plsc.ScalarSubcoreMesh(**SEQUENCER),
        out_type=[jax.ShapeDtypeStruct(h.shape, h.dtype) for h in halves],
        scratch_types=[pltpu.SemaphoreType.DMA((n,)), pltpu.SemaphoreType.DMA((n,))],
        compiler_params=pltpu.CompilerParams(collective_id=ID_SHARE),
    )(*halves)


def _all_reduce_small(part):
    rows = part.shape[0]

    def body(p_ref, out_ref, sib_buf, chip_sums, send_sems, recv_sems):
        x, y, c, chips = _place()
        mine = 2 * x + y
        swap = _remote(p_ref, sib_buf, send_sems, recv_sems, 0, (x, y, 1 - c))
        swap.start()
        swap.wait()
        chip_sums[mine] = p_ref[...] + sib_buf[...]
        sent = [_remote(chip_sums.at[mine], chip_sums.at[mine], send_sems, recv_sems, 1 + j, (*chip, c))
                for j, chip in enumerate(chips)]
        for cp in sent:
            cp.start()
        for j, chip in enumerate(chips):
            sent[j].wait_send()
            theirs = chip_sums.at[2 * chip[0] + chip[1]]
            _remote(theirs, theirs, send_sems, recv_sems, 1 + j, (x, y, c)).wait_recv()
        out_ref[...] = ((chip_sums[0] + chip_sums[1]) + chip_sums[2]) + chip_sums[3]

    vmem = pl.BlockSpec(memory_space=pltpu.VMEM)
    return pl.pallas_call(
        body, name="all_reduce_small", in_specs=[vmem], out_specs=vmem, out_shape=jax.ShapeDtypeStruct(part.shape, F32),
        scratch_shapes=[pltpu.VMEM((rows, LANE), F32), pltpu.VMEM((N_SHARDS, rows, LANE), F32),
                        pltpu.SemaphoreType.DMA((4,)), pltpu.SemaphoreType.DMA((4,))],
        compiler_params=pltpu.CompilerParams(vmem_limit_bytes=VMEM_LIMIT_BYTES),
    )(part)


def _sum_partials(g3, others, sel):
    _, rows, c = others.shape
    whole = g3.shape[1] == rows
    tr = _tile(rows, 512)
    nb = rows // tr

    def body(sel_ref, g_ref, *rest):
        same = g_ref[...].astype(F32)
        for ref in rest[1:4]:
            same = same + ref[...].astype(F32)
        other = rest[0][...].astype(F32)
        for ref in rest[4:7]:
            other = other + ref[...].astype(F32)
        rest[7][...] = same + other

    blk = (None, tr, c)
    slots = [pl.BlockSpec(blk, functools.partial(lambda i, sr, k: (k, i, 0), k=k)) for k in range(7)]
    return pl.pallas_call(
        body, name="reduce_sum_partials",
        grid_spec=pltpu.PrefetchScalarGridSpec(
            num_scalar_prefetch=1, grid=(nb,),
            in_specs=[pl.BlockSpec(blk, lambda i, sr: (sr[0], (0 if whole else sr[1] * nb) + i, 0))] + slots,
            out_specs=pl.BlockSpec((tr, c), lambda i, sr: (i, 0))),
        out_shape=jax.ShapeDtypeStruct((rows, c), F32),
        compiler_params=_params(("parallel",)),
    )(sel, g3, *[others] * 7)


def _adamw_math(w, g, m, v):
    nm = ADAM_B1 * m + (1.0 - ADAM_B1) * g
    nv = ADAM_B2 * v + (1.0 - ADAM_B2) * (g * g)
    m_hat = nm / (1.0 - ADAM_B1 ** ADAM_STEP)
    v_hat = nv / (1.0 - ADAM_B2 ** ADAM_STEP)
    return -ADAM_LR * (m_hat / (jnp.sqrt(v_hat) + ADAM_EPS) + ADAM_WD * w), nm, nv


def _adamw_layer(layer, w, m, v, g_mine, g_sibling, sel, prev):
    lyr, r, c = w.shape
    rows = g_mine.shape[0]
    halves = r // rows
    tr = _tile(rows, 512)
    nb = rows // tr
    n_g = 1 if g_sibling is None else 2

    def body(sel_ref, w_ref, m_ref, v_ref, *rest):
        g = rest[0][...]
        if n_g == 2:
            g = jnp.where(pl.program_id(0) == sel_ref[1], g, rest[1][...])
        outs = rest[n_g + (0 if prev is None else 4):]
        d, nm, nv = _adamw_math(w_ref[...], g, m_ref[...], v_ref[...])
        for ref, val in zip(outs, (g, d, nm, nv)):
            ref[...] = val

    full = pl.BlockSpec((None, tr, c), lambda h, i, sr: (layer, h * nb + i, 0))
    part = pl.BlockSpec((tr, c), lambda h, i, sr: (i, 0))
    n_in = 4 + n_g
    return pl.pallas_call(
        body, name="adamw_layer",
        grid_spec=pltpu.PrefetchScalarGridSpec(
            num_scalar_prefetch=1, grid=(halves, nb),
            in_specs=[full] * 3 + [part] * n_g + ([] if prev is None else [pl.BlockSpec(memory_space=pl.ANY)] * 4),
            out_specs=[full] * 4),
        out_shape=[jax.ShapeDtypeStruct(w.shape, F32)] * 4,
        input_output_aliases={} if prev is None else {n_in + k: k for k in range(4)},
        compiler_params=_params(("parallel", "parallel")),
    )(sel, w, m, v, g_mine, *([] if g_sibling is None else [g_sibling]), *([] if prev is None else prev))


def _adamw(w, g, m, v):
    lyr, r, c = w.shape
    tr = _tile(r, 512)

    def body(w_ref, g_ref, m_ref, v_ref, d_ref, nm_ref, nv_ref):
        gv = g_ref[...]
        nm = ADAM_B1 * m_ref[...] + (1.0 - ADAM_B1) * gv
        nv = ADAM_B2 * v_ref[...] + (1.0 - ADAM_B2) * (gv * gv)
        m_hat = nm / (1.0 - ADAM_B1 ** ADAM_STEP)
        v_hat = nv / (1.0 - ADAM_B2 ** ADAM_STEP)
        d_ref[...] = -ADAM_LR * (m_hat / (jnp.sqrt(v_hat) + ADAM_EPS) + ADAM_WD * w_ref[...])
        nm_ref[...] = nm
        nv_ref[...] = nv

    blk = pl.BlockSpec((None, tr, c), lambda l, i: (l, i, 0))
    return pl.pallas_call(
        body, name="adamw", grid=(lyr, r // tr), in_specs=[blk] * 4, out_specs=[blk] * 3,
        out_shape=[jax.ShapeDtypeStruct(w.shape, F32)] * 3,
        compiler_params=_params(("parallel", "parallel")),
    )(w, g, m, v)


SHARDED = ("mla_w_dkv", "mla_w_uq", "mla_w_ukv", "mla_w_o", "sgu_w_in", "sgu_ln_g", "sgu_ln_b", "sgu_w_out",
           "ffn_w_up", "ffn_w_down")
REPLICATED = ("norm_mix", "norm_ffn", "final_norm", "mla_q_norm", "mla_kv_norm", "sgu_w_spatial", "sgu_b_spatial")
WEIGHTS = ("norm_mix", "norm_ffn", "final_norm", "mla_w_dkv", "mla_q_norm", "mla_kv_norm", "mla_w_uq", "mla_w_ukv",
           "mla_w_o", "sgu_w_in", "sgu_ln_g", "sgu_ln_b", "sgu_w_spatial", "sgu_b_spatial", "sgu_w_out", "ffn_w_up",
           "ffn_w_down")


class _Reducer:
    def __init__(self, state, sel):
        self.state, self.sel = state, sel
        self.started, self.travelling, self.summed = [], [], []
        self.done = {}

    def add(self, tag, layer, grads, token):
        names = list(grads)
        token, *tied = lax.optimization_barrier((token, *[a for n in names for a in grads[n]]))
        f32s, bf16s = tied[0::2], tied[1::2]
        *flying, token = _exchange_start(tag, bf16s, token)
        self.started.append((tag, layer, names, f32s, flying))
        return token

    def phase_end(self, token):
        for tag, layer, names, f32s, flying in self.travelling:
            bf16s, received = _exchange_wait(tag, *flying, token)
            own = [g if g.shape[1] >= MIN_SPLIT_ROWS else gb for g, gb in zip(f32s, bf16s)]
            mine = [_sum_partials(g, got, self.sel) for g, got in zip(own, received)]
            token, *mine = lax.optimization_barrier((token, *mine))
            cut = [k for k, g in enumerate(own) if g.shape[1] >= MIN_SPLIT_ROWS]
            theirs = dict(zip(cut, _share_halves(tag, [mine[k] for k in cut])))
            self.summed.append((layer, names, mine, [theirs.get(k) for k in range(len(names))]))
        self.travelling, self.started = self.started, []
        return token

    def update(self, token):
        for layer, names, mine, theirs in self.summed:
            for name, g_mine, g_theirs in zip(names, mine, theirs):
                w, m, v = self.state[name]
                self.done[name] = _adamw_layer(layer, w, m, v, g_mine, g_theirs, self.sel, self.done.get(name))
                token = self.done[name][1]
        self.summed = []
        return token


def _as3d(name, a):
    return a.reshape(a.shape[0], -1, LANE) if name in ("sgu_ln_g", "sgu_ln_b") else a


def _pack(parts):
    flat = jnp.concatenate([p.reshape(-1) for p in parts])
    rows = -(-flat.shape[0] // (8 * LANE)) * 8
    return jnp.pad(flat, (0, rows * LANE - flat.shape[0])).reshape(rows, LANE)


def _unpack(packed, like):
    flat, out, at = packed.reshape(-1), [], 0
    for p in like:
        out.append(flat[at:at + p.size].reshape(p.shape))
        at += p.size
    return out


def kernel(x, positions, norm_mix, norm_ffn, final_norm, mla_w_dkv, mla_q_norm, mla_kv_norm, mla_w_uq, mla_w_ukv, mla_w_o, sgu_w_in, sgu_ln_g, sgu_ln_b, sgu_w_spatial, sgu_b_spatial, sgu_w_out, ffn_w_up, ffn_w_down, loss_target, m_norm_mix, m_norm_ffn, m_final_norm, m_mla_w_dkv, m_mla_q_norm, m_mla_kv_norm, m_mla_w_uq, m_mla_w_ukv, m_mla_w_o, m_sgu_w_in, m_sgu_ln_g, m_sgu_ln_b, m_sgu_w_spatial, m_sgu_b_spatial, m_sgu_w_out, m_ffn_w_up, m_ffn_w_down, v_norm_mix, v_norm_ffn, v_final_norm, v_mla_w_dkv, v_mla_q_norm, v_mla_kv_norm, v_mla_w_uq, v_mla_w_ukv, v_mla_w_o, v_sgu_w_in, v_sgu_ln_g, v_sgu_ln_b, v_sgu_w_spatial, v_sgu_b_spatial, v_sgu_w_out, v_ffn_w_up, v_ffn_w_down):
    given = dict(locals())
    w = {n: given[n] for n in WEIGHTS}
    mom = {n: given["m_" + n] for n in WEIGHTS}
    var = {n: given["v_" + n] for n in WEIGHTS}
    mixers, ffn, token = [], [], None
    for i in range(DEPTH):
        j = i // 2
        if i % 2 == 0:
            mixer = [w[n][j].astype(BF16) for n in ("mla_w_dkv", "mla_w_uq", "mla_w_ukv", "mla_w_o")]
        else:
            mixer = [sgu_w_in[j].astype(BF16), sgu_w_out[j].astype(BF16), sgu_ln_g[j].reshape(-1, LANE),
                     sgu_ln_b[j].reshape(-1, LANE)]
        for tag, shards, into in ((f"mixer{i}", mixer, mixers), (f"ffn{i}", [ffn_w_up[i].astype(BF16), ffn_w_down[i].astype(BF16)], ffn)):
            if token is None:
                token = shards[0]
            else:
                token, *shards = lax.optimization_barrier((token, *shards))
            into.append(_gather_layer(tag, shards))

    x_i, y_i, c_i = lax.axis_index("x"), lax.axis_index("y"), lax.axis_index("c")
    sel = jnp.stack([2 * x_i + y_i, c_i]).astype(jnp.int32)
    reducer = _Reducer({n: tuple(_as3d(n, d[n]) for d in (w, mom, var)) for n in SHARDED}, sel)
    loss, dx, small = _local_step(
        x[0], positions[0], loss_target[0], norm_mix, norm_ffn, final_norm, mla_q_norm, mla_kv_norm, sgu_w_spatial,
        sgu_b_spatial, mixers, ffn, reducer)
    loss = lax.psum(loss, ("x", "y", "c"))

    small_g = [small["norm_mix"], small["norm_ffn"], small["final_norm"], small["q_norm"], small["kv_norm"],
               small["w_sp"], small["b_sp"]]
    like = [w[n] for n in REPLICATED]
    g_small = _all_reduce_small(_pack(small_g))
    packed = [_pack([d[n] for n in REPLICATED])[None] for d in (w, mom, var)]
    upd_small = _adamw(packed[0], g_small[None], packed[1], packed[2])
    grads = dict(zip(REPLICATED, _unpack(g_small, like)))
    delta, new_m, new_v = ({n: a for n, a in zip(REPLICATED, _unpack(u[0], like))} for u in upd_small)

    reducer.phase_end(reducer.update(upd_small[0]))
    reducer.update(None)
    for n in SHARDED:
        grads[n], delta[n], new_m[n], new_v[n] = (a.reshape(w[n].shape) for a in reducer.done[n])

    return (loss, dx[None], *[grads[n] for n in WEIGHTS], *[delta[n] for n in WEIGHTS],
            *[new_m[n] for n in WEIGHTS], *[new_v[n] for n in WEIGHTS])
```

```python
import functools
import math

import jax
import jax.numpy as jnp
from jax import lax
from jax.experimental import pallas as pl
from jax.experimental.pallas import tpu as pltpu
from jax.experimental.pallas import tpu_sc as plsc

F32 = jnp.float32
BF16 = jnp.bfloat16
MESH = pl.DeviceIdType.MESH

DEPTH = 4
HEADS = 8
NOPE = 128
ROPE = 64
VHEAD = 128
QK_HEAD = NOPE + ROPE
Q_RANK = 256
KV_RANK = 128
HEAD_PAD = 256
LAT_PAD = 512
ROPE_THETA = 10000.0
SGU_CHUNK = 128
SGU_GROUPS = 8
NORM_EPS = 1e-6
LN_EPS = 1e-5
ADAM_LR, ADAM_B1, ADAM_B2, ADAM_EPS, ADAM_WD, ADAM_STEP = 0.001, 0.9, 0.999, 1e-08, 0.01, 10

N_SHARDS = 4
LANE = 128
VMEM_LIMIT_BYTES = 56 * 1024 * 1024
ATT_TILE = 512
MM_TILE = 1024
ATT_SCALE = QK_HEAD ** -0.5
LOG2_SCALE = ATT_SCALE * math.log2(math.e)

NN = (((1,), (0,)), ((), ()))
NT = (((1,), (1,)), ((), ()))
TN = (((0,), (0,)), ((), ()))


def _params(sem):
    return pltpu.CompilerParams(dimension_semantics=sem, vmem_limit_bytes=VMEM_LIMIT_BYTES)


def _tile(n, pref):
    t = min(n, pref)
    while n % t:
        t //= 2
    return t


def _matmul(name, a, b, a_spec, b_spec, dims, grid, tile, outs, extras=(), epilogue=None, sums=()):
    nk, ne, no = grid[2], len(extras), len(outs)
    b_specs = list(b_spec) if isinstance(b_spec, (list, tuple)) else [b_spec]
    nb = len(b_specs)

    def body(a_ref, *rest):
        b_refs, e_refs, o_refs = rest[:nb], rest[nb:nb + ne], rest[nb + ne:nb + ne + no]
        s_refs = rest[nb + ne + no:nb + ne + no + len(sums)]
        kw = a_ref.shape[1] // nb
        part = None
        for p, b_ref in enumerate(b_refs):
            a_tile = a_ref[...] if nb == 1 else a_ref[:, p * kw:(p + 1) * kw]
            d = lax.dot_general(a_tile.astype(BF16), b_ref[...].astype(BF16), dims, preferred_element_type=F32)
            part = d if part is None else part + d

        def finish(acc):
            vals = (acc,) if epilogue is None else epilogue(acc, *[e[...] for e in e_refs])
            for o_ref, v in zip(o_refs, vals):
                o_ref[...] = v.astype(o_ref.dtype)
            first = pl.program_id(0) == 0
            for s_ref, v in zip(s_refs, vals[no:]):
                @pl.when(first)
                def _():
                    s_ref[...] = v

                @pl.when(jnp.logical_not(first))
                def _():
                    s_ref[...] += v

        if nk == 1:
            finish(part)
            return
        acc_ref, k = rest[-1], pl.program_id(2)

        @pl.when(k == 0)
        def _():
            acc_ref[...] = part

        @pl.when(jnp.logical_and(k > 0, k < nk - 1))
        def _():
            acc_ref[...] += part

        @pl.when(k == nk - 1)
        def _():
            finish(acc_ref[...] + part)

    assert not sums or (grid[1] == 1 and nk == 1)
    return pl.pallas_call(
        body, name=name, grid=grid,
        in_specs=[a_spec] + b_specs + [s for _, s in extras],
        out_specs=[s for _, s in outs] + [pl.BlockSpec(s, lambda i, j, k: (0,) * len(s)) for s in sums],
        out_shape=[s for s, _ in outs] + [jax.ShapeDtypeStruct(s, F32) for s in sums],
        scratch_shapes=[pltpu.VMEM(tile, F32)] if nk > 1 else [],
        compiler_params=_params(("arbitrary" if sums else "parallel", "parallel", "arbitrary")),
    )(a, *[b] * nb, *[e for e, _ in extras])


def _epilogue_operands(extras, consts, o_spec):
    return [(e, o_spec) for e in extras] + [(c, pl.BlockSpec(c.shape, lambda i, j, k: (0, 0))) for c in consts]


def _mm(name, a, b, out_dtypes=(F32,), epilogue=None, extras=(), tm=MM_TILE, tn=MM_TILE, tk=MM_TILE, nt=False,
        consts=(), sums=()):
    m, kd = a.shape
    n = b.shape[0] if nt else b.shape[1]
    tm, tn, tk = _tile(m, tm), _tile(n, tn), _tile(kd, tk)
    o_spec = pl.BlockSpec((tm, tn), lambda i, j, k: (i, j))
    b_spec = pl.BlockSpec((tn, tk), lambda i, j, k: (j, k)) if nt else pl.BlockSpec((tk, tn), lambda i, j, k: (k, j))
    return _matmul(name, a, b, pl.BlockSpec((tm, tk), lambda i, j, k: (i, k)), b_spec, NT if nt else NN,
                   (m // tm, n // tn, kd // tk), (tm, tn),
                   [(jax.ShapeDtypeStruct((m, n), d), o_spec) for d in out_dtypes],
                   _epilogue_operands(extras, consts, o_spec), epilogue, sums)


def _mm_tn(name, a, b, out_dtypes=(F32,), tm=MM_TILE, tn=MM_TILE, tk=MM_TILE):
    s, m = a.shape
    n = b.shape[1]
    tm, tn, tk = _tile(m, tm), _tile(n, tn), _tile(s, tk)
    o_spec = pl.BlockSpec((tm, tn), lambda i, j, k: (i, j))
    return _matmul(name, a, b, pl.BlockSpec((tk, tm), lambda i, j, k: (k, i)),
                   pl.BlockSpec((tk, tn), lambda i, j, k: (k, j)), TN, (m // tm, n // tn, s // tk), (tm, tn),
                   [(jax.ShapeDtypeStruct((m, n), d), o_spec) for d in out_dtypes])


def _mm_stacked(name, a, w3, mode, out_dtypes=(F32,), epilogue=None, extras=(), tm=MM_TILE, tn=MM_TILE, tk=MM_TILE):
    m, kd = a.shape
    _, r, c = w3.shape
    n = c if mode == "row" else N_SHARDS * c
    if mode == "row":
        tm, tn, tk = _tile(m, tm // 2), _tile(n, tn), kd
        b_spec = [pl.BlockSpec((None, r, tn), functools.partial(lambda i, j, k, p: (p, 0, j), p=p)) for p in range(N_SHARDS)]
    else:
        tm, tn, tk = _tile(m, tm), _tile(c, tn), _tile(kd, tk)
        per = c // tn
        b_spec = pl.BlockSpec((None, tk, tn), lambda i, j, k: (j // per, k, j % per))
    o_spec = pl.BlockSpec((tm, tn), lambda i, j, k: (i, j))
    return _matmul(name, a, w3, pl.BlockSpec((tm, tk), lambda i, j, k: (i, k)), b_spec, NN,
                   (m // tm, n // tn, kd // tk), (tm, tn),
                   [(jax.ShapeDtypeStruct((m, n), d), o_spec) for d in out_dtypes],
                   [(e, o_spec) for e in extras], epilogue)


def _mm_stacked_nt(name, a, w3, mode, out_dtypes=(F32,), epilogue=None, extras=(), tm=MM_TILE, tn=MM_TILE, tk=MM_TILE,
                   consts=(), sums=()):
    m, nd = a.shape
    _, r, c = w3.shape
    kout = N_SHARDS * r if mode == "row" else r
    if mode == "row":
        tm, tn, tk = _tile(m, tm), _tile(r, tn), _tile(c, tk)
        per = r // tn
        b_spec = pl.BlockSpec((None, tn, tk), lambda i, j, k: (j // per, j % per, k))
    else:
        tm, tn, tk = _tile(m, tm // 2), _tile(r, tn), nd
        b_spec = [pl.BlockSpec((None, tn, c), functools.partial(lambda i, j, k, p: (p, j, 0), p=p)) for p in range(N_SHARDS)]
    o_spec = pl.BlockSpec((tm, tn), lambda i, j, k: (i, j))
    return _matmul(name, a, w3, pl.BlockSpec((tm, tk), lambda i, j, k: (i, k)), b_spec, NT,
                   (m // tm, kout // tn, nd // tk), (tm, tn),
                   [(jax.ShapeDtypeStruct((m, kout), d), o_spec) for d in out_dtypes],
                   _epilogue_operands(extras, consts, o_spec), epilogue, sums)


def _mm_tn_stacked(name, a, b, shape3, mode, tm=MM_TILE, tn=MM_TILE, tk=MM_TILE):
    s, m = a.shape
    n = b.shape[1]
    _, r, c = shape3
    tk, tn = s, tn // 2
    if mode == "row":
        tm, tn = _tile(r, tm), _tile(n, tn)
        per = r // tm
        o_spec = pl.BlockSpec((None, tm, tn), lambda i, j, k: (i // per, i % per, j))
    else:
        tm, tn = _tile(m, tm), _tile(c, tn)
        per = c // tn
        o_spec = pl.BlockSpec((None, tm, tn), lambda i, j, k: (j // per, i, j % per))
    outs = [(jax.ShapeDtypeStruct(shape3, F32), o_spec), (jax.ShapeDtypeStruct(shape3, BF16), o_spec)]
    return _matmul(name, a, b, pl.BlockSpec((tk, tm), lambda i, j, k: (k, i)),
                   pl.BlockSpec((tk, tn), lambda i, j, k: (k, j)), TN, (m // tm, n // tn, s // tk), (tm, tn),
                   outs, epilogue=lambda acc: (acc, acc))


def _rowwise(name, fn, rows, consts, out_rows, out_accs=(), tr=256):
    nr, nc, no = len(rows), len(consts), len(out_rows)
    n_rows = rows[0].shape[0]
    tr = _tile(n_rows, tr)

    def body(*refs):
        vals = fn(*[r[...] for r in refs[:nr + nc]])
        o_refs, a_refs = refs[nr + nc:nr + nc + no], refs[nr + nc + no:]
        for ref, v in zip(o_refs, vals[:no]):
            ref[...] = v.astype(ref.dtype)
        first = pl.program_id(0) == 0

        @pl.when(first)
        def _():
            for ref, v in zip(a_refs, vals[no:]):
                ref[...] = v

        @pl.when(jnp.logical_not(first))
        def _():
            for ref, v in zip(a_refs, vals[no:]):
                ref[...] += v

    def whole(shape):
        return pl.BlockSpec(shape, lambda i: (0,) * len(shape))

    return pl.pallas_call(
        body, name=name, grid=(n_rows // tr,),
        in_specs=[pl.BlockSpec((tr, a.shape[1]), lambda i: (i, 0)) for a in rows] + [whole(c.shape) for c in consts],
        out_specs=[pl.BlockSpec((tr, f), lambda i: (i, 0)) for f, _ in out_rows] + [whole(s) for s in out_accs],
        out_shape=[jax.ShapeDtypeStruct((n_rows, f), d) for f, d in out_rows]
        + [jax.ShapeDtypeStruct(s, F32) for s in out_accs],
        compiler_params=_params(("arbitrary",)),
    )(*rows, *consts)


def _rms_fwd(x, g):
    return x * lax.rsqrt(jnp.mean(x * x, axis=-1, keepdims=True) + NORM_EPS) * g


def _rms_bwd(dy, x, g):
    rstd = lax.rsqrt(jnp.mean(x * x, axis=-1, keepdims=True) + NORM_EPS)
    n = x * rstd
    dn = dy * g
    dx = rstd * (dn - n * jnp.mean(dn * n, axis=-1, keepdims=True))
    return dx, jnp.sum(dy * n, axis=0, keepdims=True)


def _rope(x, cs, s1, s2):
    return x * cs + pltpu.roll(x, 32, 1) * s1 + pltpu.roll(x, 96, 1) * s2


def _rope_t(dy, cs, s1, s2):
    return dy * cs + pltpu.roll(dy * s1, 96, 1) + pltpu.roll(dy * s2, 32, 1)


def _gelu(z):
    return 0.5 * z * (1.0 + lax.erf(z * (1.0 / math.sqrt(2.0))))


def _gelu_and_grad(z):
    cdf = 0.5 * (1.0 + lax.erf(z * (1.0 / math.sqrt(2.0))))
    return z * cdf, cdf + z * jnp.exp(-0.5 * z * z) * (1.0 / math.sqrt(2.0 * math.pi))


def _att_scores(q, kv, kr, scale, masked, transposed):
    k = jnp.concatenate([kv[:, :NOPE], kr], axis=1)
    if transposed:
        s = lax.dot_general(k, q, NT, preferred_element_type=F32) * scale
    else:
        s = lax.dot_general(q, k, NT, preferred_element_type=F32) * scale
    if masked:
        r = lax.broadcasted_iota(jnp.int32, s.shape, 0)
        c = lax.broadcasted_iota(jnp.int32, s.shape, 1)
        s = jnp.where((r <= c) if transposed else (c <= r), s, -jnp.inf)
    return s, k


def _in_pairs(lo, hi, pair, single):
    n = hi - lo

    def body(p, carry):
        pair(lo + 2 * p, lo + 2 * p + 1)
        return carry

    lax.fori_loop(0, n // 2, body, 0)

    @pl.when(n % 2 == 1)
    def _():
        single(hi - 1)


def _causal_tiles(i, pair, single):
    @pl.when(i == 0)
    def _():
        single(i, True)

    @pl.when(i > 0)
    def _():
        _in_pairs(0, i - 1, lambda a, b: pair(a, b, False), lambda a: single(a, False))
        pair(i - 1, i, True)


def _flash_fwd(q, kvb, krb, tables):
    s_len = q.shape[0]
    t = ATT_TILE

    def body(q_ref, cs_ref, s1_ref, s2_ref, kv_ref, kr_ref, o_ref, lse_ref, qb_ref, m_s, l_s, acc_s):
        qi = pl.program_id(1)
        m_s[...] = jnp.full_like(m_s, -jnp.inf)
        l_s[...] = jnp.zeros_like(l_s)
        acc_s[...] = jnp.zeros_like(acc_s)
        qv = q_ref[...]
        q = jnp.concatenate([qv[:, :NOPE], _rope(qv[:, NOPE:], cs_ref[...], s1_ref[...], s2_ref[...])], axis=1).astype(BF16)
        qb_ref[...] = q

        def scores(ki, masked):
            rows = pl.ds(pl.multiple_of(ki * t, t), t)
            kv = kv_ref[rows, :]
            return _att_scores(q, kv, kr_ref[rows, :], LOG2_SCALE, masked, False)[0], kv

        def update(s, kv):
            m_prev = m_s[...]
            m_new = jnp.maximum(m_prev, jnp.max(s, axis=1, keepdims=True))
            alpha = jnp.exp2(m_prev - m_new)
            p = jnp.exp2(s - jnp.tile(m_new, (1, t // LANE)))
            l_s[...] = alpha * l_s[...] + jnp.sum(p, axis=1, keepdims=True)
            acc_s[...] = alpha * acc_s[...] + jnp.dot(p.astype(BF16), kv[:, NOPE:], preferred_element_type=F32)
            m_s[...] = m_new

        def pair(k0, k1, masked):
            first, second = scores(k0, False), scores(k1, masked)
            update(*first)
            update(*second)

        _causal_tiles(qi, pair, lambda ki, masked: update(*scores(ki, masked)))
        o_ref[...] = (acc_s[...] / l_s[...]).astype(o_ref.dtype)
        lse_ref[...] = (m_s[...] + jnp.log2(l_s[...]))[:, :1]

    table = pl.BlockSpec((t, LANE), lambda h, qi: (qi, 0))
    return pl.pallas_call(
        body, name="flash_fwd", grid=(HEADS, s_len // t),
        in_specs=[pl.BlockSpec((t, HEAD_PAD), lambda h, qi: (qi, h)), table, table, table,
                  pl.BlockSpec((s_len, HEAD_PAD), lambda h, qi: (0, h)),
                  pl.BlockSpec((s_len, LANE), lambda h, qi: (0, 0))],
        out_specs=[pl.BlockSpec((t, VHEAD), lambda h, qi: (qi, h)),
                   pl.BlockSpec((None, t, 1), lambda h, qi: (h, qi, 0)),
                   pl.BlockSpec((t, HEAD_PAD), lambda h, qi: (qi, h))],
        out_shape=[jax.ShapeDtypeStruct((s_len, HEADS * VHEAD), BF16),
                   jax.ShapeDtypeStruct((HEADS, s_len, 1), F32),
                   jax.ShapeDtypeStruct((s_len, HEADS * HEAD_PAD), BF16)],
        scratch_shapes=[pltpu.VMEM((t, LANE), F32), pltpu.VMEM((t, LANE), F32), pltpu.VMEM((t, VHEAD), F32)],
        compiler_params=_params(("parallel", "arbitrary")),
    )(q, *tables, kvb, krb)


def _flash_bwd_dq(qb, kvb, krb, dob, lse, delta, tables):
    s_len = qb.shape[0]
    t = ATT_TILE
    nq = s_len // t
    scale = QK_HEAD ** -0.5

    def body(q_ref, kv_ref, kr_ref, do_ref, lse_ref, dl_ref, cs_ref, s1_ref, s2_ref, dq_ref, acc_s):
        qi = pl.program_id(1)
        acc_s[...] = jnp.zeros_like(acc_s)
        q, do = q_ref[...], do_ref[...]
        lse = jnp.broadcast_to(lse_ref[...], (t, LANE))
        dl = dl_ref[...]

        def products(ki, masked):
            rows = pl.ds(pl.multiple_of(ki * t, t), t)
            kv = kv_ref[rows, :]
            s, k = _att_scores(q, kv, kr_ref[rows, :], LOG2_SCALE, masked, False)
            return s, lax.dot_general(do, kv[:, NOPE:], NT, preferred_element_type=F32), k

        def update(s, dp, k):
            p = jnp.exp2(s - jnp.tile(lse, (1, t // LANE)))
            ds = (p * (dp - jnp.tile(dl, (1, t // LANE))) * scale).astype(BF16)
            acc_s[...] += jnp.dot(ds, k, preferred_element_type=F32)

        def pair(k0, k1, masked):
            first, second = products(k0, False), products(k1, masked)
            update(*first)
            update(*second)

        _causal_tiles(qi, pair, lambda ki, masked: update(*products(ki, masked)))
        dq = acc_s[...]
        dq = jnp.concatenate([dq[:, :NOPE], _rope_t(dq[:, NOPE:], cs_ref[...], s1_ref[...], s2_ref[...])], axis=1)
        dq_ref[...] = dq.astype(dq_ref.dtype)

    col = pl.BlockSpec((None, t, 1), lambda h, qi: (h, qi, 0))
    table = pl.BlockSpec((t, LANE), lambda h, qi: (qi, 0))
    return pl.pallas_call(
        body, name="flash_bwd_dq", grid=(HEADS, nq),
        in_specs=[pl.BlockSpec((t, HEAD_PAD), lambda h, qi: (qi, h)),
                  pl.BlockSpec((s_len, HEAD_PAD), lambda h, qi: (0, h)),
                  pl.BlockSpec((s_len, LANE), lambda h, qi: (0, 0)),
                  pl.BlockSpec((t, VHEAD), lambda h, qi: (qi, h)), col, pl.BlockSpec((t, VHEAD), lambda h, qi: (qi, h)),
                  table, table, table],
        out_specs=pl.BlockSpec((t, HEAD_PAD), lambda h, qi: (qi, h)),
        out_shape=jax.ShapeDtypeStruct((s_len, HEADS * HEAD_PAD), BF16),
        scratch_shapes=[pltpu.VMEM((t, HEAD_PAD), F32)],
        compiler_params=_params(("parallel", "arbitrary")),
    )(qb, kvb, krb, dob, lse, delta, *tables)


def _flash_bwd_dkv(qb, kvb, krb, dob, lse_row, delta_row):
    s_len = qb.shape[0]
    t = ATT_TILE
    nq = s_len // t
    scale = QK_HEAD ** -0.5

    def body(q_ref, kv_ref, kr_ref, do_ref, lse_ref, dl_ref, dkv_ref, dkr_ref, dk_s, dv_s):
        ki = pl.program_id(1)
        dk_s[...] = jnp.zeros_like(dk_s)
        dv_s[...] = jnp.zeros_like(dv_s)
        kv, kr = kv_ref[...], kr_ref[...]

        def products(qi, masked):
            rows = pl.ds(pl.multiple_of(qi * t, t), t)
            q, do = q_ref[rows, :], do_ref[rows, :]
            st, _ = _att_scores(q, kv, kr, LOG2_SCALE, masked, True)
            return st, lax.dot_general(kv[:, NOPE:], do, NT, preferred_element_type=F32), q, do, rows

        def update(st, dpt, q, do, rows):
            pt = jnp.exp2(st - lse_ref[:, rows])
            dv_s[...] += jnp.dot(pt.astype(BF16), do, preferred_element_type=F32)
            dst = (pt * (dpt - dl_ref[:, rows]) * scale).astype(BF16)
            dk_s[...] += jnp.dot(dst, q, preferred_element_type=F32)

        def pair(q0, q1, masked):
            first, second = products(q0, masked), products(q1, False)
            update(*first)
            update(*second)

        @pl.when(ki == nq - 1)
        def _():
            update(*products(ki, True))

        @pl.when(ki < nq - 1)
        def _():
            pair(ki, ki + 1, True)
            _in_pairs(ki + 2, nq, lambda a, b: pair(a, b, False), lambda qi: update(*products(qi, False)))

        dk = dk_s[...]
        dkv_ref[...] = jnp.concatenate([dk[:, :NOPE], dv_s[...]], axis=1).astype(dkv_ref.dtype)
        dkr_ref[...] = dk[:, NOPE:]

    row = pl.BlockSpec((None, 1, s_len), lambda h, ki: (h, 0, 0))
    return pl.pallas_call(
        body, name="flash_bwd_dkv", grid=(HEADS, nq),
        in_specs=[pl.BlockSpec((s_len, HEAD_PAD), lambda h, ki: (0, h)),
                  pl.BlockSpec((t, HEAD_PAD), lambda h, ki: (ki, h)),
                  pl.BlockSpec((t, LANE), lambda h, ki: (ki, 0)),
                  pl.BlockSpec((s_len, VHEAD), lambda h, ki: (0, h)), row, row],
        out_specs=[pl.BlockSpec((t, HEAD_PAD), lambda h, ki: (ki, h)),
                   pl.BlockSpec((t, LANE), lambda h, ki: (ki, h))],
        out_shape=[jax.ShapeDtypeStruct((s_len, HEADS * HEAD_PAD), BF16),
                   jax.ShapeDtypeStruct((s_len, HEADS * LANE), F32)],
        scratch_shapes=[pltpu.VMEM((t, HEAD_PAD), F32), pltpu.VMEM((t, VHEAD), F32)],
        compiler_params=_params(("parallel", "parallel")),
    )(qb, kvb, krb, dob, lse_row, delta_row)


def _tril(w):
    r = lax.broadcasted_iota(jnp.int32, w.shape, 0)
    c = lax.broadcasted_iota(jnp.int32, w.shape, 1)
    return jnp.where(c <= r, w, 0.0)


def _sgu_mix(w_ref, vln, gd):
    wcs = [_tril(w_ref[g]).astype(BF16) for g in range(SGU_GROUPS)]
    mixed = jnp.concatenate(
        [jnp.dot(wcs[g], vln[:, g * gd:(g + 1) * gd], preferred_element_type=F32) for g in range(SGU_GROUPS)], axis=1)
    return wcs, mixed


def _sgu_fwd(zpre, ln_g, ln_b, w_sp, bias_full):
    s_len, two_w = zpre.shape
    width = two_w // 2
    gd = width // SGU_GROUPS
    t = SGU_CHUNK

    def body(z_ref, g_ref, b_ref, w_ref, bias_ref, uv_ref):
        u = _gelu(z_ref[:, :width])
        v = _gelu(z_ref[:, width:])
        d = v - jnp.mean(v, axis=-1, keepdims=True)
        vhat = d * lax.rsqrt(jnp.mean(d * d, axis=-1, keepdims=True) + LN_EPS)
        vln = (vhat * g_ref[...] + b_ref[...]).astype(BF16)
        _, mixed = _sgu_mix(w_ref, vln, gd)
        uv_ref[...] = (u * (mixed + bias_ref[...])).astype(uv_ref.dtype)

    return pl.pallas_call(
        body, name="sgu_fwd", grid=(s_len // t,),
        in_specs=[pl.BlockSpec((t, two_w), lambda i: (i, 0)), pl.BlockSpec((1, width), lambda i: (0, 0)),
                  pl.BlockSpec((1, width), lambda i: (0, 0)), pl.BlockSpec(w_sp.shape, lambda i: (0, 0, 0)),
                  pl.BlockSpec((t, width), lambda i: (0, 0))],
        out_specs=pl.BlockSpec((t, width), lambda i: (i, 0)),
        out_shape=jax.ShapeDtypeStruct((s_len, width), BF16),
        compiler_params=_params(("parallel",)),
    )(zpre, ln_g, ln_b, w_sp, bias_full)


def _sgu_bwd(zpre, duv, ln_g, ln_b, w_sp, bias_full):
    s_len, two_w = zpre.shape
    width = two_w // 2
    gd = width // SGU_GROUPS
    t = SGU_CHUNK

    def body(z_ref, duv_ref, g_ref, b_ref, w_ref, bias_ref, dz_ref, dg_ref, db_ref, dw_ref, dbias_ref):
        first = pl.program_id(0) == 0

        def accumulate(ref, val):
            @pl.when(first)
            def _():
                ref[...] = val

            @pl.when(jnp.logical_not(first))
            def _():
                ref[...] += val

        u, u_grad = _gelu_and_grad(z_ref[:, :width])
        v, v_grad = _gelu_and_grad(z_ref[:, width:])
        d = v - jnp.mean(v, axis=-1, keepdims=True)
        rstd = lax.rsqrt(jnp.mean(d * d, axis=-1, keepdims=True) + LN_EPS)
        vhat = d * rstd
        vln = (vhat * g_ref[...] + b_ref[...]).astype(BF16)
        wcs, mixed = _sgu_mix(w_ref, vln, gd)
        duv_v = duv_ref[...]
        du = duv_v * (mixed + bias_ref[...])
        dmixed = duv_v * u
        dmb = dmixed.astype(BF16)
        dvln = jnp.concatenate(
            [lax.dot_general(wcs[g], dmb[:, g * gd:(g + 1) * gd], TN, preferred_element_type=F32)
             for g in range(SGU_GROUPS)], axis=1)
        for g in range(SGU_GROUPS):
            dw = lax.dot_general(dmb[:, g * gd:(g + 1) * gd], vln[:, g * gd:(g + 1) * gd], NT, preferred_element_type=F32)
            accumulate(dw_ref.at[g], _tril(dw))
        dvhat = dvln * g_ref[...]
        dv0 = rstd * (dvhat - jnp.mean(dvhat, axis=-1, keepdims=True)
                      - vhat * jnp.mean(dvhat * vhat, axis=-1, keepdims=True))
        dz_ref[:, :width] = (du * u_grad).astype(dz_ref.dtype)
        dz_ref[:, width:] = (dv0 * v_grad).astype(dz_ref.dtype)
        accumulate(dg_ref, jnp.sum(dvln * vhat, axis=0, keepdims=True))
        accumulate(db_ref, jnp.sum(dvln, axis=0, keepdims=True))
        accumulate(dbias_ref, dmixed)

    vec = pl.BlockSpec((1, width), lambda i: (0, 0))
    return pl.pallas_call(
        body, name="sgu_bwd", grid=(s_len // t,),
        in_specs=[pl.BlockSpec((t, two_w), lambda i: (i, 0)), pl.BlockSpec((t, width), lambda i: (i, 0)), vec, vec,
                  pl.BlockSpec(w_sp.shape, lambda i: (0, 0, 0)), pl.BlockSpec((t, width), lambda i: (0, 0))],
        out_specs=[pl.BlockSpec((t, two_w), lambda i: (i, 0)), vec, vec,
                   pl.BlockSpec(w_sp.shape, lambda i: (0, 0, 0)), pl.BlockSpec((t, width), lambda i: (0, 0))],
        out_shape=[jax.ShapeDtypeStruct((s_len, two_w), BF16), jax.ShapeDtypeStruct((1, width), F32),
                   jax.ShapeDtypeStruct((1, width), F32), jax.ShapeDtypeStruct(w_sp.shape, F32),
                   jax.ShapeDtypeStruct((t, width), F32)],
        compiler_params=_params(("arbitrary",)),
    )(zpre, duv, ln_g, ln_b, w_sp, bias_full)


def _rope_tables(positions):
    inv_freq = ROPE_THETA ** (-jnp.arange(0, ROPE, 2, dtype=F32) / ROPE)
    ang = positions.astype(F32)[:, None] * inv_freq
    cos, sin = jnp.cos(ang), jnp.sin(ang)
    z32, z64 = jnp.zeros_like(cos), jnp.zeros((cos.shape[0], LANE - ROPE), F32)
    return (jnp.concatenate([cos, cos, z64], axis=1), jnp.concatenate([z32, sin, z64], axis=1),
            jnp.concatenate([-sin, z32, z64], axis=1))


def _ffn_fwd(x, g, w_up3, w_down3):
    h2 = _rowwise("ffn_norm", lambda xv, gv: (_rms_fwd(xv, gv),), [x], [g], [(x.shape[1], BF16)])[0]

    def sq_relu(acc):
        r = jnp.maximum(acc, 0.0)
        return r * r, 2.0 * r

    r, r_grad = _mm_stacked("ffn_up", h2, w_up3, "col", (BF16, BF16), sq_relu)
    x_out = _mm_stacked("ffn_down", r, w_down3, "row", (F32,), lambda acc, res: (acc + res,), [x])[0]
    return x_out, (x, h2, r, r_grad)


def _ffn_bwd(dx, dxb, saved, g, w_up3, w_down3):
    x, h2, r, r_grad = saved
    da = _mm_stacked_nt("ffn_down_dx", dxb, w_down3, "row", (BF16,),
                        lambda acc, rg: (acc * rg.astype(F32),), [r_grad])[0]
    g_down = _mm_tn_stacked("ffn_down_dw", r, dxb, w_down3.shape, "row")
    dx, dxb, dg = _mm_stacked_nt("ffn_up_dx", da, w_up3, "col", **_norm_bwd(x, g, dx))
    g_up = _mm_tn_stacked("ffn_up_dw", h2, da, w_up3.shape, "col")
    return dx, dxb, dg, g_up, g_down


def _norm_bwd(x, g, dres):
    def epilogue(dh, xv, rv, gv):
        dxv, dg = _rms_bwd(dh, xv, gv)
        return dxv + rv, dxv + rv, dg

    return dict(out_dtypes=(F32, BF16), epilogue=epilogue, extras=[x, dres], consts=[g], sums=[g.shape], tm=MM_TILE // 2)


def _dot(a, b, dims=NN):
    return lax.dot_general(a.astype(BF16), b.astype(BF16), dims, preferred_element_type=F32)


def _mla_fwd(x, g, wdkv, q_norm, kv_norm, wq, wkv, wo, tables):
    d = x.shape[1]

    def project(xv, cs, s1, s2, gv, wdkv_v, qg, kg, wq_v, wkv_v):
        h = _rms_fwd(xv, gv).astype(BF16)
        lv = _dot(h, wdkv_v)
        cqn = _rms_fwd(lv[:, :Q_RANK], qg).astype(BF16)
        ckvn = _rms_fwd(lv[:, Q_RANK:Q_RANK + KV_RANK], kg).astype(BF16)
        return (h, lv, cqn, ckvn, _rope(lv[:, Q_RANK + KV_RANK:], cs, s1, s2), _dot(cqn, wq_v), _dot(ckvn, wkv_v))

    h, lat, cqn, ckvn, krb, q, kvb = _rowwise(
        "mla_project", project, [x, *tables], [g, wdkv, q_norm, kv_norm, wq, wkv],
        [(d, BF16), (LAT_PAD, F32), (Q_RANK, BF16), (KV_RANK, BF16), (LANE, BF16), (wq.shape[1], F32), (wkv.shape[1], BF16)],
        tr=512)
    ob, lse, qb = _flash_fwd(q, kvb, krb, tables)
    x_mid = _mm("mla_o", ob, wo, (F32,), lambda acc, res: (acc + res,), [x])[0]
    return x_mid, (x, h, lat, cqn, ckvn, krb, qb, kvb, ob, lse)


def _mla_bwd(dx, dxb, saved, g, wdkv, q_norm, kv_norm, wq, wkv, wo, tables):
    x, h, lat, cqn, ckvn, krb, qb, kvb, ob, lse = saved
    s_len = x.shape[0]
    def with_delta(do, ov):
        prod = do * ov.astype(F32)
        sums = [jnp.sum(prod[:, hd * VHEAD:(hd + 1) * VHEAD], axis=1, keepdims=True) for hd in range(HEADS)]
        return do, jnp.concatenate([jnp.broadcast_to(s, (do.shape[0], VHEAD)) for s in sums], axis=1)

    dob, delta = _mm("mla_o_dx", dxb, wo, (BF16, F32), with_delta, [ob], nt=True, tm=MM_TILE // 2)
    g_wo = _mm_tn("mla_o_dw", ob, dxb)[0]
    delta_row = delta.reshape(s_len, HEADS, VHEAD)[:, :, 0].T.reshape(HEADS, 1, s_len)
    dqb = _flash_bwd_dq(qb, kvb, krb, dob, lse, delta, tables)
    dkvb, dkr = _flash_bwd_dkv(qb, kvb, krb, dob, lse.reshape(HEADS, 1, s_len), delta_row)
    def project_bwd(dq_v, dkv_v, dkr_v, lv, cqn_v, ckvn_v, h_v, xv, rv, cs, s1, s2, gv, qg, kg, wq_v, wkv_v, wdkv_v):
        dcq, dqg = _rms_bwd(_dot(dq_v, wq_v, NT), lv[:, :Q_RANK], qg)
        dckv, dkg = _rms_bwd(_dot(dkv_v, wkv_v, NT), lv[:, Q_RANK:Q_RANK + KV_RANK], kg)
        dkr_sum = dkr_v[:, :LANE]
        for hd in range(1, HEADS):
            dkr_sum = dkr_sum + dkr_v[:, hd * LANE:(hd + 1) * LANE]
        dlat = jnp.concatenate([dcq, dckv, _rope_t(dkr_sum, cs, s1, s2)], axis=1).astype(BF16)
        dxv, dg = _rms_bwd(_dot(dlat, wdkv_v, NT), xv, gv)
        return (dxv + rv, dxv + rv, _dot(cqn_v, dq_v, TN), _dot(ckvn_v, dkv_v, TN), _dot(h_v, dlat, TN), dqg, dkg, dg)

    dx, dxb, g_wq, g_wkv, g_wdkv, g_qn, g_kvn, dg = _rowwise(
        "mla_project_bwd", project_bwd, [dqb, dkvb, dkr, lat, cqn, ckvn, h, x, dx, *tables],
        [g, q_norm, kv_norm, wq, wkv, wdkv], [(x.shape[1], F32), (x.shape[1], BF16)],
        [wq.shape, wkv.shape, wdkv.shape, q_norm.shape, kv_norm.shape, g.shape], tr=256)
    return dx, dxb, dg, g_wdkv, g_qn, g_kvn, g_wq, g_wkv, g_wo


def _sgu_layer_fwd(x, g, w_in3, ln_g, ln_b, w_sp, bias_full, w_out3):
    h = _rowwise("sgu_norm", lambda xv, gv: (_rms_fwd(xv, gv),), [x], [g], [(x.shape[1], BF16)])[0]
    zpre = _mm_stacked("sgu_in", h, w_in3, "col")[0]
    uv = _sgu_fwd(zpre, ln_g, ln_b, w_sp, bias_full)
    x_mid = _mm_stacked("sgu_out", uv, w_out3, "row", (F32,), lambda acc, res: (acc + res,), [x])[0]
    return x_mid, (x, h, zpre, uv)


def _sgu_layer_bwd(dx, dxb, saved, g, w_in3, ln_g, ln_b, w_sp, bias_full, w_out3):
    x, h, zpre, uv = saved
    duv = _mm_stacked_nt("sgu_out_dx", dxb, w_out3, "row")[0]
    g_out = _mm_tn_stacked("sgu_out_dw", uv, dxb, w_out3.shape, "row")
    dz, g_lng, g_lnb, g_wsp, g_bias = _sgu_bwd(zpre, duv, ln_g, ln_b, w_sp, bias_full)
    g_in = _mm_tn_stacked("sgu_in_dw", h, dz, w_in3.shape, "col")
    dx, dxb, dg = _mm_stacked_nt("sgu_in_dx", dz, w_in3, "col", **_norm_bwd(x, g, dx))
    return dx, dxb, dg, g_in, g_out, g_lng, g_lnb, g_wsp, g_bias


def _loss_head(x, target, g):
    d = x.shape[1]

    def fn(xv, tv, gv):
        err = _rms_fwd(xv, gv) - tv
        dxv, dg = _rms_bwd(err * (1.0 / d), xv, gv)
        return dxv, dxv, dg, jnp.sum(err * err, axis=0, keepdims=True)

    return _rowwise("loss_head", fn, [x, target], [g], [(d, F32), (d, BF16)], [g.shape, g.shape])


def _mixer_weights(i, stacks):
    by_rows = lambda a: a.reshape(N_SHARDS * a.shape[1], a.shape[2])
    by_cols = lambda a: a.transpose(1, 0, 2).reshape(a.shape[1], N_SHARDS * a.shape[2])
    if i % 2:
        w_in3, w_out3, ln_g, ln_b = stacks
        return w_in3, ln_g.reshape(1, -1), ln_b.reshape(1, -1), w_out3
    wdkv = by_rows(stacks[0])
    wdkv = jnp.pad(wdkv, ((0, 0), (0, LAT_PAD - wdkv.shape[1])))
    wq = jnp.pad(by_cols(stacks[1]).reshape(Q_RANK, HEADS, QK_HEAD), ((0, 0), (0, 0), (0, HEAD_PAD - QK_HEAD)))
    return wdkv, wq.reshape(Q_RANK, HEADS * HEAD_PAD), by_cols(stacks[2]), by_rows(stacks[3])


def _local_step(x, positions, target, norm_mix, norm_ffn, final_norm, q_norm, kv_norm, w_sp, b_sp, mixers, ffn, reducer):
    tables = _rope_tables(positions)
    gd = mixers[1][2].size // SGU_GROUPS
    bias_full = [jnp.repeat(b_sp[j].T, gd, axis=1) for j in range(DEPTH // 2)]
    saved, mla, sgu = [], [None] * (DEPTH // 2), [None] * (DEPTH // 2)
    for i in range(DEPTH):
        j = i // 2
        x, *stacks = lax.optimization_barrier((x, *mixers[i]))
        if i % 2 == 0:
            wdkv, wq, wkv, wo = mla[j] = _mixer_weights(i, stacks)
            x, s_mix = _mla_fwd(x, norm_mix[i:i + 1], wdkv, q_norm[j:j + 1], kv_norm[j:j + 1], wq, wkv, wo, tables)
        else:
            w_in3, ln_g, ln_b, w_out3 = sgu[j] = _mixer_weights(i, stacks)
            x, s_mix = _sgu_layer_fwd(x, norm_mix[i:i + 1], w_in3, ln_g, ln_b, w_sp[j], bias_full[j], w_out3)
        x, s_ffn = _ffn_fwd(x, norm_ffn[i:i + 1], *ffn[i])
        saved.append((s_mix, s_ffn))
    dx, dxb, g_final, sq_cols = _loss_head(x, target, final_norm[None, :])
    loss = 0.5 * jnp.sum(sq_cols) / x.shape[1]

    def pair(g):
        return g, g.astype(BF16)

    g_mix, g_ffn = [None] * DEPTH, [None] * DEPTH
    mla_g, sgu_g = [None] * (DEPTH // 2), [None] * (DEPTH // 2)
    for i in reversed(range(DEPTH)):
        j = i // 2
        s_mix, s_ffn = saved[i]
        dx, dxb, g_ffn[i], g_up, g_down = _ffn_bwd(dx, dxb, s_ffn, norm_ffn[i:i + 1], *ffn[i])
        dxb = reducer.add(f"ffn{i}", i, {"ffn_w_up": g_up, "ffn_w_down": g_down}, dxb)
        dxb = reducer.phase_end(dxb)
        if i % 2 == 0:
            wdkv, wq, wkv, wo = mla[j]
            dx, dxb, g_mix[i], g_wdkv, g_qn, g_kvn, g_wq, g_wkv, g_wo = _mla_bwd(
                dx, dxb, s_mix, norm_mix[i:i + 1], wdkv, q_norm[j:j + 1], kv_norm[j:j + 1], wq, wkv, wo, tables)
            mla_g[j] = (g_qn, g_kvn)
            g_wq = g_wq.reshape(Q_RANK, HEADS, HEAD_PAD)[..., :QK_HEAD].reshape(Q_RANK, N_SHARDS, -1)
            dxb = reducer.add(f"mla{j}", j, {
                "mla_w_dkv": pair(g_wdkv[:, :Q_RANK + KV_RANK + ROPE].reshape(N_SHARDS, -1, Q_RANK + KV_RANK + ROPE)),
                "mla_w_uq": pair(g_wq.transpose(1, 0, 2)),
                "mla_w_ukv": pair(g_wkv.reshape(KV_RANK, N_SHARDS, -1).transpose(1, 0, 2)),
                "mla_w_o": pair(g_wo.reshape(N_SHARDS, -1, g_wo.shape[1]))}, dxb)
        else:
            w_in3, ln_g, ln_b, w_out3 = sgu[j]
            dx, dxb, g_mix[i], g_in, g_out, g_lng, g_lnb, g_wsp, g_bias = _sgu_layer_bwd(
                dx, dxb, s_mix, norm_mix[i:i + 1], w_in3, ln_g, ln_b, w_sp[j], bias_full[j], w_out3)
            sgu_g[j] = (g_wsp, g_bias.reshape(SGU_CHUNK, SGU_GROUPS, gd).sum(axis=-1).T)
            dxb = reducer.add(f"sgu{j}", j, {"sgu_w_in": g_in, "sgu_w_out": g_out,
                                             "sgu_ln_g": pair(g_lng.reshape(N_SHARDS, -1, LANE)),
                                             "sgu_ln_b": pair(g_lnb.reshape(N_SHARDS, -1, LANE))}, dxb)
        dxb = reducer.phase_end(dxb)
    small = dict(
        norm_mix=jnp.concatenate(g_mix, axis=0), norm_ffn=jnp.concatenate(g_ffn, axis=0), final_norm=g_final[0],
        q_norm=jnp.concatenate([m[0] for m in mla_g], axis=0), kv_norm=jnp.concatenate([m[1] for m in mla_g], axis=0),
        w_sp=jnp.stack([s[0] for s in sgu_g]), b_sp=jnp.stack([s[1] for s in sgu_g]))
    return loss, dx, small


HBM_SPEC = pl.BlockSpec(memory_space=pltpu.HBM)


def _place():
    x, y, c = lax.axis_index("x"), lax.axis_index("y"), lax.axis_index("c")
    return x, y, c, [(1 - x, y), (x, 1 - y), (1 - x, 1 - y)]


def _remote(src, dst, send_sems, recv_sems, k, to):
    return pltpu.make_async_remote_copy(src_ref=src, dst_ref=dst, send_sem=send_sems.at[k], recv_sem=recv_sems.at[k],
                                        device_id=to, device_id_type=MESH)


def _gather_layer(tag, shards):
    n = len(shards)
    split = [s.shape[0] >= 16 for s in shards]

    def body(*refs):
        ins, outs = refs[:n], refs[n:2 * n]
        send_sems, recv_sems, local_sems = refs[2 * n:]
        x, y, c, chips = _place()
        mine = 2 * x + y
        barrier = pltpu.get_barrier_semaphore()
        peers = [(x, y, 1 - c)] + [(*chip, c) for chip in chips]
        for peer in peers:
            pl.semaphore_signal(barrier, inc=1, device_id=peer, device_id_type=MESH)
        pl.semaphore_wait(barrier, len(peers))

        def rows(t, half):
            hr = shards[t].shape[0] // 2
            return pl.ds(half * hr, hr) if split[t] else pl.ds(0, shards[t].shape[0])

        local, sent = [], []
        for t in range(n):
            local.append(pltpu.make_async_copy(ins[t], outs[t].at[mine], local_sems.at[t]))
            local[-1].start()
            for j, chip in enumerate(chips):
                cp = _remote(ins[t].at[rows(t, c)], outs[t].at[mine, rows(t, c)], send_sems, recv_sems, 3 * t + j, (*chip, c))
                cp.start()
                sent.append(cp)
        for j, chip in enumerate(chips):
            theirs = 2 * chip[0] + chip[1]
            for t in range(n):
                piece = outs[t].at[theirs, rows(t, c)]
                _remote(piece, piece, send_sems, recv_sems, 3 * t + j, (x, y, c)).wait_recv()
                if split[t]:
                    cp = _remote(piece, piece, send_sems, recv_sems, 3 * n + 3 * t + j, (x, y, 1 - c))
                    cp.start()
                    sent.append(cp)
        for j, chip in enumerate(chips):
            theirs = 2 * chip[0] + chip[1]
            for t in range(n):
                if split[t]:
                    piece = outs[t].at[theirs, rows(t, 1 - c)]
                    _remote(piece, piece, send_sems, recv_sems, 3 * n + 3 * t + j, (x, y, c)).wait_recv()
        for cp in sent:
            cp.wait_send()
        for cp in local:
            cp.wait()

    return pl.kernel(
        body, name=f"gather_{tag}", mesh=plsc.ScalarSubcoreMesh(axis_name="sequencer", num_cores=1),
        out_type=[jax.ShapeDtypeStruct((N_SHARDS, *s.shape), s.dtype) for s in shards],
        scratch_types=[pltpu.SemaphoreType.DMA((6 * n,)), pltpu.SemaphoreType.DMA((6 * n,)), pltpu.SemaphoreType.DMA((n,))],
        compiler_params=pltpu.CompilerParams(collective_id=ID_GATHER),
    )(*shards)


SEQUENCER = dict(axis_name="sequencer", num_cores=1)
ID_GATHER, ID_EXCHANGE, ID_SHARE = 0, 1, 2
MIN_SPLIT_ROWS = 16


def _handshake(peers):
    barrier = pltpu.get_barrier_semaphore()
    for peer in peers:
        pl.semaphore_signal(barrier, inc=1, device_id=peer, device_id_type=MESH)
    pl.semaphore_wait(barrier, len(peers))


def _half_rows(rows, half):
    return pl.ds(half * (rows // 2), rows // 2) if rows >= MIN_SPLIT_ROWS else pl.ds(0, rows)


SEM_SPEC = pl.BlockSpec(memory_space=pltpu.SEMAPHORE)
DATAFLOW = pltpu.SideEffectType.DATAFLOW_SIDE_EFFECTING


def _exchange_copies(shapes, stacks, lands, send_sems, recv_sems):
    x, y, c, chips = _place()
    mine = 2 * x + y
    copies = []
    for t, shape in enumerate(shapes):
        r = shape[1]
        copies.append(_remote(stacks[t].at[mine, _half_rows(r, 1 - c)], lands[t].at[0], send_sems, recv_sems, 7 * t, (x, y, 1 - c)))
        for j, chip in enumerate(chips):
            theirs = 2 * chip[0] + chip[1]
            copies.append(_remote(stacks[t].at[theirs, _half_rows(r, c)], lands[t].at[1 + j], send_sems, recv_sems,
                                  7 * t + 1 + j, (*chip, c)))
            copies.append(_remote(stacks[t].at[theirs, _half_rows(r, 1 - c)], lands[t].at[4 + j], send_sems, recv_sems,
                                  7 * t + 4 + j, (*chip, 1 - c)))
    return copies


def _exchange_start(tag, stacks, carry):
    n = len(stacks)
    shapes = [s.shape for s in stacks]
    lands = [lax.empty((7, s.shape[1] // 2 if s.shape[1] >= MIN_SPLIT_ROWS else s.shape[1], s.shape[2]), s.dtype) for s in stacks]

    def body(*refs):
        for cp in _exchange_copies(shapes, refs[:n], refs[n:2 * n], refs[2 * n + 1], refs[2 * n + 2]):
            cp.start()

    through = (*stacks, *lands, carry)
    out = pl.pallas_call(
        body, name=f"reduce_exchange_start_{tag}",
        out_shape=(pltpu.SemaphoreType.DMA((7 * n,)), pltpu.SemaphoreType.DMA((7 * n,)),
                   *[pltpu.HBM(a.shape, a.dtype) for a in through]),
        in_specs=[HBM_SPEC] * (2 * n + 1),
        out_specs=(SEM_SPEC, SEM_SPEC, *[HBM_SPEC] * (2 * n + 1)),
        input_output_aliases={t: 2 + t for t in range(2 * n + 1)},
        compiler_params=pltpu.CompilerParams(has_side_effects=DATAFLOW),
    )(*[pltpu.with_memory_space_constraint(a, pltpu.HBM) for a in through])
    return out[0], out[1], out[2:2 + n], out[2 + n:2 + 2 * n], out[-1]


def _exchange_wait(tag, send_sems, recv_sems, stacks, lands, after):
    n = len(stacks)
    shapes = [s.shape for s in stacks]

    def body(*refs):
        for cp in _exchange_copies(shapes, refs[:n], refs[n:2 * n], refs[2 * n], refs[2 * n + 1]):
            cp.wait()

    out = pl.pallas_call(
        body, name=f"reduce_exchange_wait_{tag}",
        out_shape=tuple(pltpu.HBM(a.shape, a.dtype) for a in (*stacks, *lands)),
        in_specs=[HBM_SPEC] * (2 * n) + [SEM_SPEC, SEM_SPEC, pl.BlockSpec(memory_space=pl.ANY)],
        out_specs=tuple([HBM_SPEC] * (2 * n)),
        input_output_aliases={t: t for t in range(2 * n)},
        compiler_params=pltpu.CompilerParams(has_side_effects=DATAFLOW),
    )(*stacks, *lands, send_sems, recv_sems, after)
    return out[:n], out[n:]


def _share_halves(tag, halves):
    n = len(halves)

    def body(*refs):
        ins, outs, send_sems, recv_sems = refs[:n], refs[n:2 * n], refs[2 * n], refs[2 * n + 1]
        x, y, c, _ = _place()
        _handshake([(x, y, 1 - c)])
        sent = [_remote(ins[t], outs[t], send_sems, recv_sems, t, (x, y, 1 - c)) for t in range(n)]
        for cp in sent:
            cp.start()
        for cp in sent:
            cp.wait()

    return pl.kernel(
        body, name=f"reduce_share_{tag}", mesh=plsc.ScalarSubcoreMesh(**SEQUENCER),
        out_type=[jax.ShapeDtypeStruct(h.shape, h.dtype) for h in halves],
        scratch_types=[pltpu.SemaphoreType.DMA((n,)), pltpu.SemaphoreType.DMA((n,))],
        compiler_params=pltpu.CompilerParams(collective_id=ID_SHARE),
    )(*halves)


def _all_reduce_small(part):
    rows = part.shape[0]

    def body(p_ref, out_ref, sib_buf, chip_sums, send_sems, recv_sems):
        x, y, c, chips = _place()
        mine = 2 * x + y
        swap = _remote(p_ref, sib_buf, send_sems, recv_sems, 0, (x, y, 1 - c))
        swap.start()
        swap.wait()
        chip_sums[mine] = p_ref[...] + sib_buf[...]
        sent = [_remote(chip_sums.at[mine], chip_sums.at[mine], send_sems, recv_sems, 1 + j, (*chip, c))
                for j, chip in enumerate(chips)]
        for cp in sent:
            cp.start()
        for j, chip in enumerate(chips):
            sent[j].wait_send()
            theirs = chip_sums.at[2 * chip[0] + chip[1]]
            _remote(theirs, theirs, send_sems, recv_sems, 1 + j, (x, y, c)).wait_recv()
        out_ref[...] = ((chip_sums[0] + chip_sums[1]) + chip_sums[2]) + chip_sums[3]

    vmem = pl.BlockSpec(memory_space=pltpu.VMEM)
    return pl.pallas_call(
        body, name="all_reduce_small", in_specs=[vmem], out_specs=vmem, out_shape=jax.ShapeDtypeStruct(part.shape, F32),
        scratch_shapes=[pltpu.VMEM((rows, LANE), F32), pltpu.VMEM((N_SHARDS, rows, LANE), F32),
                        pltpu.SemaphoreType.DMA((4,)), pltpu.SemaphoreType.DMA((4,))],
        compiler_params=pltpu.CompilerParams(vmem_limit_bytes=VMEM_LIMIT_BYTES),
    )(part)


def _sum_partials(g3, others, sel):
    _, rows, c = others.shape
    whole = g3.shape[1] == rows
    tr = _tile(rows, 512)
    nb = rows // tr

    def body(sel_ref, g_ref, *rest):
        same = g_ref[...].astype(F32)
        for ref in rest[1:4]:
            same = same + ref[...].astype(F32)
        other = rest[0][...].astype(F32)
        for ref in rest[4:7]:
            other = other + ref[...].astype(F32)
        rest[7][...] = same + other

    blk = (None, tr, c)
    slots = [pl.BlockSpec(blk, functools.partial(lambda i, sr, k: (k, i, 0), k=k)) for k in range(7)]
    return pl.pallas_call(
        body, name="reduce_sum_partials",
        grid_spec=pltpu.PrefetchScalarGridSpec(
            num_scalar_prefetch=1, grid=(nb,),
            in_specs=[pl.BlockSpec(blk, lambda i, sr: (sr[0], (0 if whole else sr[1] * nb) + i, 0))] + slots,
            out_specs=pl.BlockSpec((tr, c), lambda i, sr: (i, 0))),
        out_shape=jax.ShapeDtypeStruct((rows, c), F32),
        compiler_params=_params(("parallel",)),
    )(sel, g3, *[others] * 7)


def _adamw_math(w, g, m, v):
    nm = ADAM_B1 * m + (1.0 - ADAM_B1) * g
    nv = ADAM_B2 * v + (1.0 - ADAM_B2) * (g * g)
    m_hat = nm / (1.0 - ADAM_B1 ** ADAM_STEP)
    v_hat = nv / (1.0 - ADAM_B2 ** ADAM_STEP)
    return -ADAM_LR * (m_hat / (jnp.sqrt(v_hat) + ADAM_EPS) + ADAM_WD * w), nm, nv


def _adamw_layer(layer, w, m, v, g_mine, g_sibling, sel, prev):
    lyr, r, c = w.shape
    rows = g_mine.shape[0]
    halves = r // rows
    tr = _tile(rows, 512)
    nb = rows // tr
    n_g = 1 if g_sibling is None else 2

    def body(sel_ref, w_ref, m_ref, v_ref, *rest):
        g = rest[0][...]
        if n_g == 2:
            g = jnp.where(pl.program_id(0) == sel_ref[1], g, rest[1][...])
        outs = rest[n_g + (0 if prev is None else 4):]
        d, nm, nv = _adamw_math(w_ref[...], g, m_ref[...], v_ref[...])
        for ref, val in zip(outs, (g, d, nm, nv)):
            ref[...] = val

    full = pl.BlockSpec((None, tr, c), lambda h, i, sr: (layer, h * nb + i, 0))
    part = pl.BlockSpec((tr, c), lambda h, i, sr: (i, 0))
    n_in = 4 + n_g
    return pl.pallas_call(
        body, name="adamw_layer",
        grid_spec=pltpu.PrefetchScalarGridSpec(
            num_scalar_prefetch=1, grid=(halves, nb),
            in_specs=[full] * 3 + [part] * n_g + ([] if prev is None else [pl.BlockSpec(memory_space=pl.ANY)] * 4),
            out_specs=[full] * 4),
        out_shape=[jax.ShapeDtypeStruct(w.shape, F32)] * 4,
        input_output_aliases={} if prev is None else {n_in + k: k for k in range(4)},
        compiler_params=_params(("parallel", "parallel")),
    )(sel, w, m, v, g_mine, *([] if g_sibling is None else [g_sibling]), *([] if prev is None else prev))


def _adamw(w, g, m, v):
    lyr, r, c = w.shape
    tr = _tile(r, 512)

    def body(w_ref, g_ref, m_ref, v_ref, d_ref, nm_ref, nv_ref):
        gv = g_ref[...]
        nm = ADAM_B1 * m_ref[...] + (1.0 - ADAM_B1) * gv
        nv = ADAM_B2 * v_ref[...] + (1.0 - ADAM_B2) * (gv * gv)
        m_hat = nm / (1.0 - ADAM_B1 ** ADAM_STEP)
        v_hat = nv / (1.0 - ADAM_B2 ** ADAM_STEP)
        d_ref[...] = -ADAM_LR * (m_hat / (jnp.sqrt(v_hat) + ADAM_EPS) + ADAM_WD * w_ref[...])
        nm_ref[...] = nm
        nv_ref[...] = nv

    blk = pl.BlockSpec((None, tr, c), lambda l, i: (l, i, 0))
    return pl.pallas_call(
        body, name="adamw", grid=(lyr, r // tr), in_specs=[blk] * 4, out_specs=[blk] * 3,
        out_shape=[jax.ShapeDtypeStruct(w.shape, F32)] * 3,
        compiler_params=_params(("parallel", "parallel")),
    )(w, g, m, v)


SHARDED = ("mla_w_dkv", "mla_w_uq", "mla_w_ukv", "mla_w_o", "sgu_w_in", "sgu_ln_g", "sgu_ln_b", "sgu_w_out",
           "ffn_w_up", "ffn_w_down")
REPLICATED = ("norm_mix", "norm_ffn", "final_norm", "mla_q_norm", "mla_kv_norm", "sgu_w_spatial", "sgu_b_spatial")
WEIGHTS = ("norm_mix", "norm_ffn", "final_norm", "mla_w_dkv", "mla_q_norm", "mla_kv_norm", "mla_w_uq", "mla_w_ukv",
           "mla_w_o", "sgu_w_in", "sgu_ln_g", "sgu_ln_b", "sgu_w_spatial", "sgu_b_spatial", "sgu_w_out", "ffn_w_up",
           "ffn_w_down")


class _Reducer:
    def __init__(self, state, sel):
        self.state, self.sel = state, sel
        self.started, self.travelling, self.summed = [], [], []
        self.done = {}

    def add(self, tag, layer, grads, token):
        names = list(grads)
        token, *tied = lax.optimization_barrier((token, *[a for n in names for a in grads[n]]))
        f32s, bf16s = tied[0::2], tied[1::2]
        *flying, token = _exchange_start(tag, bf16s, token)
        self.started.append((tag, layer, names, f32s, flying))
        return token

    def phase_end(self, token):
        for tag, layer, names, f32s, flying in self.travelling:
            bf16s, received = _exchange_wait(tag, *flying, token)
            own = [g if g.shape[1] >= MIN_SPLIT_ROWS else gb for g, gb in zip(f32s, bf16s)]
            mine = [_sum_partials(g, got, self.sel) for g, got in zip(own, received)]
            token, *mine = lax.optimization_barrier((token, *mine))
            cut = [k for k, g in enumerate(own) if g.shape[1] >= MIN_SPLIT_ROWS]
            theirs = dict(zip(cut, _share_halves(tag, [mine[k] for k in cut])))
            self.summed.append((layer, names, mine, [theirs.get(k) for k in range(len(names))]))
        self.travelling, self.started = self.started, []
        return token

    def update(self, token):
        for layer, names, mine, theirs in self.summed:
            for name, g_mine, g_theirs in zip(names, mine, theirs):
                w, m, v = self.state[name]
                self.done[name] = _adamw_layer(layer, w, m, v, g_mine, g_theirs, self.sel, self.done.get(name))
                token = self.done[name][1]
        self.summed = []
        return token


def _as3d(name, a):
    return a.reshape(a.shape[0], -1, LANE) if name in ("sgu_ln_g", "sgu_ln_b") else a


def _pack(parts):
    flat = jnp.concatenate([p.reshape(-1) for p in parts])
    rows = -(-flat.shape[0] // (8 * LANE)) * 8
    return jnp.pad(flat, (0, rows * LANE - flat.shape[0])).reshape(rows, LANE)


def _unpack(packed, like):
    flat, out, at = packed.reshape(-1), [], 0
    for p in like:
        out.append(flat[at:at + p.size].reshape(p.shape))
        at += p.size
    return out


def kernel(x, positions, norm_mix, norm_ffn, final_norm, mla_w_dkv, mla_q_norm, mla_kv_norm, mla_w_uq, mla_w_ukv, mla_w_o, sgu_w_in, sgu_ln_g, sgu_ln_b, sgu_w_spatial, sgu_b_spatial, sgu_w_out, ffn_w_up, ffn_w_down, loss_target, m_norm_mix, m_norm_ffn, m_final_norm, m_mla_w_dkv, m_mla_q_norm, m_mla_kv_norm, m_mla_w_uq, m_mla_w_ukv, m_mla_w_o, m_sgu_w_in, m_sgu_ln_g, m_sgu_ln_b, m_sgu_w_spatial, m_sgu_b_spatial, m_sgu_w_out, m_ffn_w_up, m_ffn_w_down, v_norm_mix, v_norm_ffn, v_final_norm, v_mla_w_dkv, v_mla_q_norm, v_mla_kv_norm, v_mla_w_uq, v_mla_w_ukv, v_mla_w_o, v_sgu_w_in, v_sgu_ln_g, v_sgu_ln_b, v_sgu_w_spatial, v_sgu_b_spatial, v_sgu_w_out, v_ffn_w_up, v_ffn_w_down):
    given = dict(locals())
    w = {n: given[n] for n in WEIGHTS}
    mom = {n: given["m_" + n] for n in WEIGHTS}
    var = {n: given["v_" + n] for n in WEIGHTS}
    mixers, ffn, token = [], [], None
    for i in range(DEPTH):
        j = i // 2
        if i % 2 == 0:
            mixer = [w[n][j].astype(BF16) for n in ("mla_w_dkv", "mla_w_uq", "mla_w_ukv", "mla_w_o")]
        else:
            mixer = [sgu_w_in[j].astype(BF16), sgu_w_out[j].astype(BF16), sgu_ln_g[j].reshape(-1, LANE),
                     sgu_ln_b[j].reshape(-1, LANE)]
        for tag, shards, into in ((f"mixer{i}", mixer, mixers), (f"ffn{i}", [ffn_w_up[i].astype(BF16), ffn_w_down[i].astype(BF16)], ffn)):
            if token is None:
                token = shards[0]
            else:
                token, *shards = lax.optimization_barrier((token, *shards))
            into.append(_gather_layer(tag, shards))

    x_i, y_i, c_i = lax.axis_index("x"), lax.axis_index("y"), lax.axis_index("c")
    sel = jnp.stack([2 * x_i + y_i, c_i]).astype(jnp.int32)
    reducer = _Reducer({n: tuple(_as3d(n, d[n]) for d in (w, mom, var)) for n in SHARDED}, sel)
    loss, dx, small = _local_step(
        x[0], positions[0], loss_target[0], norm_mix, norm_ffn, final_norm, mla_q_norm, mla_kv_norm, sgu_w_spatial,
        sgu_b_spatial, mixers, ffn, reducer)
    loss = lax.psum(loss, ("x", "y", "c"))

    small_g = [small["norm_mix"], small["norm_ffn"], small["final_norm"], small["q_norm"], small["kv_norm"],
               small["w_sp"], small["b_sp"]]
    like = [w[n] for n in REPLICATED]
    g_small = _all_reduce_small(_pack(small_g))
    packed = [_pack([d[n] for n in REPLICATED])[None] for d in (w, mom, var)]
    upd_small = _adamw(packed[0], g_small[None], packed[1], packed[2])
    grads = dict(zip(REPLICATED, _unpack(g_small, like)))
    delta, new_m, new_v = ({n: a for n, a in zip(REPLICATED, _unpack(u[0], like))} for u in upd_small)

    reducer.phase_end(reducer.update(upd_small[0]))
    reducer.update(None)
    for n in SHARDED:
        grads[n], delta[n], new_m[n], new_v[n] = (a.reshape(w[n].shape) for a in reducer.done[n])

    return (loss, dx[None], *[grads[n] for n in WEIGHTS], *[delta[n] for n in WEIGHTS],
            *[new_m[n] for n in WEIGHTS], *[new_v[n] for n in WEIGHTS])
```

```python
import functools
import math

import jax
import jax.numpy as jnp
from jax import lax
from jax.experimental import pallas as pl
from jax.experimental.pallas import tpu as pltpu
from jax.experimental.pallas import tpu_sc as plsc

F32 = jnp.float32
BF16 = jnp.bfloat16
MESH = pl.DeviceIdType.MESH

DEPTH = 4
HEADS = 8
NOPE = 128
ROPE = 64
VHEAD = 128
QK_HEAD = NOPE + ROPE
Q_RANK = 256
KV_RANK = 128
HEAD_PAD = 256
LAT_PAD = 512
ROPE_THETA = 10000.0
SGU_CHUNK = 128
SGU_GROUPS = 8
NORM_EPS = 1e-6
LN_EPS = 1e-5
ADAM_LR, ADAM_B1, ADAM_B2, ADAM_EPS, ADAM_WD, ADAM_STEP = 0.001, 0.9, 0.999, 1e-08, 0.01, 10

N_SHARDS = 4
LANE = 128
VMEM_LIMIT_BYTES = 56 * 1024 * 1024
ATT_TILE = 512
MM_TILE = 1024
ATT_SCALE = QK_HEAD ** -0.5
LOG2_SCALE = ATT_SCALE * math.log2(math.e)

NN = (((1,), (0,)), ((), ()))
NT = (((1,), (1,)), ((), ()))
TN = (((0,), (0,)), ((), ()))


def _params(sem):
    return pltpu.CompilerParams(dimension_semantics=sem, vmem_limit_bytes=VMEM_LIMIT_BYTES)


def _tile(n, pref):
    t = min(n, pref)
    while n % t:
        t //= 2
    return t


def _matmul(name, a, b, a_spec, b_spec, dims, grid, tile, outs, extras=(), epilogue=None, sums=(), norm_g=None):
    nk, ne, no = grid[2], len(extras), len(outs)
    b_specs = list(b_spec) if isinstance(b_spec, (list, tuple)) else [b_spec]
    nb = len(b_specs)
    normed = norm_g is not None
    assert not normed or (nk == 1 and nb == 1)

    def body(a_ref, *rest):
        b_refs, e_refs, o_refs = rest[:nb], rest[nb:nb + ne], rest[nb + ne + normed:nb + ne + normed + no]
        s_refs = rest[nb + ne + normed + no:nb + ne + normed + no + len(sums)]
        if normed:
            g_ref, h_ref, h_s = rest[nb + ne], rest[nb + ne + 1 + no + len(sums)], rest[-1]

            @pl.when(pl.program_id(1) == 0)
            def _():
                h_s[...] = _rms_fwd(a_ref[...], g_ref[...]).astype(BF16)
                h_ref[...] = h_s[...]

            a_ref = h_s
        kw = a_ref.shape[1] // nb
        part = None
        for p, b_ref in enumerate(b_refs):
            a_tile = a_ref[...] if nb == 1 else a_ref[:, p * kw:(p + 1) * kw]
            d = lax.dot_general(a_tile.astype(BF16), b_ref[...].astype(BF16), dims, preferred_element_type=F32)
            part = d if part is None else part + d

        def finish(acc):
            vals = (acc,) if epilogue is None else epilogue(acc, *[e[...] for e in e_refs])
            for o_ref, v in zip(o_refs, vals):
                o_ref[...] = v.astype(o_ref.dtype)
            first = pl.program_id(0) == 0
            for s_ref, v in zip(s_refs, vals[no:]):
                @pl.when(first)
                def _():
                    s_ref[...] = v

                @pl.when(jnp.logical_not(first))
                def _():
                    s_ref[...] += v

        if nk == 1:
            finish(part)
            return
        acc_ref, k = rest[-1], pl.program_id(2)

        @pl.when(k == 0)
        def _():
            acc_ref[...] = part

        @pl.when(jnp.logical_and(k > 0, k < nk - 1))
        def _():
            acc_ref[...] += part

        @pl.when(k == nk - 1)
        def _():
            finish(acc_ref[...] + part)

    assert not sums or (grid[1] == 1 and nk == 1)
    a_block = a_spec.block_shape
    return pl.pallas_call(
        body, name=name, grid=grid,
        in_specs=[a_spec] + b_specs + [s for _, s in extras]
        + ([pl.BlockSpec(norm_g.shape, lambda i, j, k: (0, 0))] if normed else []),
        out_specs=[s for _, s in outs] + [pl.BlockSpec(s, lambda i, j, k: (0,) * len(s)) for s in sums]
        + ([pl.BlockSpec(a_block, lambda i, j, k: (i, 0))] if normed else []),
        out_shape=[s for s, _ in outs] + [jax.ShapeDtypeStruct(s, F32) for s in sums]
        + ([jax.ShapeDtypeStruct(a.shape, BF16)] if normed else []),
        scratch_shapes=([pltpu.VMEM(tile, F32)] if nk > 1 else []) + ([pltpu.VMEM(a_block, BF16)] if normed else []),
        compiler_params=_params(("arbitrary" if sums else "parallel", "arbitrary" if normed else "parallel", "arbitrary")),
    )(a, *[b] * nb, *[e for e, _ in extras], *([norm_g] if normed else []))


def _epilogue_operands(extras, consts, o_spec):
    return [(e, o_spec) for e in extras] + [(c, pl.BlockSpec(c.shape, lambda i, j, k: (0, 0))) for c in consts]


def _mm(name, a, b, out_dtypes=(F32,), epilogue=None, extras=(), tm=MM_TILE, tn=MM_TILE, tk=MM_TILE, nt=False,
        consts=(), sums=(), narrow=()):
    m, kd = a.shape
    n = b.shape[0] if nt else b.shape[1]
    tm, tn, tk = _tile(m, tm), _tile(n, tn), _tile(kd, tk)
    o_spec = pl.BlockSpec((tm, tn), lambda i, j, k: (i, j))
    b_spec = pl.BlockSpec((tn, tk), lambda i, j, k: (j, k)) if nt else pl.BlockSpec((tk, tn), lambda i, j, k: (k, j))
    assert not narrow or n == tn
    outs = [(jax.ShapeDtypeStruct((m, n), d), o_spec) for d in out_dtypes]
    outs += [(jax.ShapeDtypeStruct((m, w), d), pl.BlockSpec((tm, w), lambda i, j, k: (i, 0))) for w, d in narrow]
    return _matmul(name, a, b, pl.BlockSpec((tm, tk), lambda i, j, k: (i, k)), b_spec, NT if nt else NN,
                   (m // tm, n // tn, kd // tk), (tm, tn), outs,
                   _epilogue_operands(extras, consts, o_spec), epilogue, sums)


def _mm_tn(name, a, b, out_dtypes=(F32,), tm=MM_TILE, tn=MM_TILE, tk=MM_TILE):
    s, m = a.shape
    n = b.shape[1]
    tm, tn, tk = _tile(m, tm), _tile(n, tn), _tile(s, tk)
    o_spec = pl.BlockSpec((tm, tn), lambda i, j, k: (i, j))
    return _matmul(name, a, b, pl.BlockSpec((tk, tm), lambda i, j, k: (k, i)),
                   pl.BlockSpec((tk, tn), lambda i, j, k: (k, j)), TN, (m // tm, n // tn, s // tk), (tm, tn),
                   [(jax.ShapeDtypeStruct((m, n), d), o_spec) for d in out_dtypes])


def _mm_stacked(name, a, w3, mode, out_dtypes=(F32,), epilogue=None, extras=(), tm=MM_TILE, tn=MM_TILE, tk=MM_TILE,
                norm_g=None):
    m, kd = a.shape
    _, r, c = w3.shape
    n = c if mode == "row" else N_SHARDS * c
    if mode == "row":
        tm, tn, tk = _tile(m, tm // 2), _tile(n, tn), kd
        b_spec = [pl.BlockSpec((None, r, tn), functools.partial(lambda i, j, k, p: (p, 0, j), p=p)) for p in range(N_SHARDS)]
    else:
        tm, tn, tk = _tile(m, tm), _tile(c, tn), _tile(kd, tk)
        per = c // tn
        b_spec = pl.BlockSpec((None, tk, tn), lambda i, j, k: (j // per, k, j % per))
    o_spec = pl.BlockSpec((tm, tn), lambda i, j, k: (i, j))
    return _matmul(name, a, w3, pl.BlockSpec((tm, tk), lambda i, j, k: (i, k)), b_spec, NN,
                   (m // tm, n // tn, kd // tk), (tm, tn),
                   [(jax.ShapeDtypeStruct((m, n), d), o_spec) for d in out_dtypes],
                   [(e, o_spec) for e in extras], epilogue, norm_g=norm_g)


def _mm_stacked_nt(name, a, w3, mode, out_dtypes=(F32,), epilogue=None, extras=(), tm=MM_TILE, tn=MM_TILE, tk=MM_TILE,
                   consts=(), sums=()):
    m, nd = a.shape
    _, r, c = w3.shape
    kout = N_SHARDS * r if mode == "row" else r
    if mode == "row":
        tm, tn, tk = _tile(m, tm), _tile(r, tn), _tile(c, tk)
        per = r // tn
        b_spec = pl.BlockSpec((None, tn, tk), lambda i, j, k: (j // per, j % per, k))
    else:
        tm, tn, tk = _tile(m, tm // 2), _tile(r, tn), nd
        b_spec = [pl.BlockSpec((None, tn, c), functools.partial(lambda i, j, k, p: (p, j, 0), p=p)) for p in range(N_SHARDS)]
    o_spec = pl.BlockSpec((tm, tn), lambda i, j, k: (i, j))
    return _matmul(name, a, w3, pl.BlockSpec((tm, tk), lambda i, j, k: (i, k)), b_spec, NT,
                   (m // tm, kout // tn, nd // tk), (tm, tn),
                   [(jax.ShapeDtypeStruct((m, kout), d), o_spec) for d in out_dtypes],
                   _epilogue_operands(extras, consts, o_spec), epilogue, sums)


def _mm_tn_stacked(name, a, b, shape3, mode, tm=MM_TILE, tn=MM_TILE, tk=MM_TILE):
    s, m = a.shape
    n = b.shape[1]
    _, r, c = shape3
    tk, tn = s, tn // 2
    if mode == "row":
        tm, tn = _tile(r, tm), _tile(n, tn)
        per = r // tm
        o_spec = pl.BlockSpec((None, tm, tn), lambda i, j, k: (i // per, i % per, j))
    else:
        tm, tn = _tile(m, tm), _tile(c, tn)
        per = c // tn
        o_spec = pl.BlockSpec((None, tm, tn), lambda i, j, k: (j // per, i, j % per))
    outs = [(jax.ShapeDtypeStruct(shape3, F32), o_spec), (jax.ShapeDtypeStruct(shape3, BF16), o_spec)]
    return _matmul(name, a, b, pl.BlockSpec((tk, tm), lambda i, j, k: (k, i)),
                   pl.BlockSpec((tk, tn), lambda i, j, k: (k, j)), TN, (m // tm, n // tn, s // tk), (tm, tn),
                   outs, epilogue=lambda acc: (acc, acc))


def _rowwise(name, fn, rows, consts, out_rows, out_accs=(), tr=256):
    nr, nc, no = len(rows), len(consts), len(out_rows)
    n_rows = rows[0].shape[0]
    tr = _tile(n_rows, tr)

    def body(*refs):
        vals = fn(*[r[...] for r in refs[:nr + nc]])
        o_refs, a_refs = refs[nr + nc:nr + nc + no], refs[nr + nc + no:]
        for ref, v in zip(o_refs, vals[:no]):
            ref[...] = v.astype(ref.dtype)
        first = pl.program_id(0) == 0

        @pl.when(first)
        def _():
            for ref, v in zip(a_refs, vals[no:]):
                ref[...] = v

        @pl.when(jnp.logical_not(first))
        def _():
            for ref, v in zip(a_refs, vals[no:]):
                ref[...] += v

    def whole(shape):
        return pl.BlockSpec(shape, lambda i: (0,) * len(shape))

    return pl.pallas_call(
        body, name=name, grid=(n_rows // tr,),
        in_specs=[pl.BlockSpec((tr, a.shape[1]), lambda i: (i, 0)) for a in rows] + [whole(c.shape) for c in consts],
        out_specs=[pl.BlockSpec((tr, f), lambda i: (i, 0)) for f, _ in out_rows] + [whole(s) for s in out_accs],
        out_shape=[jax.ShapeDtypeStruct((n_rows, f), d) for f, d in out_rows]
        + [jax.ShapeDtypeStruct(s, F32) for s in out_accs],
        compiler_params=_params(("arbitrary",)),
    )(*rows, *consts)


def _rms_fwd(x, g):
    return x * lax.rsqrt(jnp.mean(x * x, axis=-1, keepdims=True) + NORM_EPS) * g


def _rms_bwd(dy, x, g):
    rstd = lax.rsqrt(jnp.mean(x * x, axis=-1, keepdims=True) + NORM_EPS)
    n = x * rstd
    dn = dy * g
    dx = rstd * (dn - n * jnp.mean(dn * n, axis=-1, keepdims=True))
    return dx, jnp.sum(dy * n, axis=0, keepdims=True)


def _rope(x, cs, s1, s2):
    return x * cs + pltpu.roll(x, 32, 1) * s1 + pltpu.roll(x, 96, 1) * s2


def _rope_t(dy, cs, s1, s2):
    return dy * cs + pltpu.roll(dy * s1, 96, 1) + pltpu.roll(dy * s2, 32, 1)


def _gelu(z):
    return 0.5 * z * (1.0 + lax.erf(z * (1.0 / math.sqrt(2.0))))


def _gelu_and_grad(z):
    cdf = 0.5 * (1.0 + lax.erf(z * (1.0 / math.sqrt(2.0))))
    return z * cdf, cdf + z * jnp.exp(-0.5 * z * z) * (1.0 / math.sqrt(2.0 * math.pi))


def _att_scores(q, kv, kr, scale, masked, transposed):
    k = jnp.concatenate([kv[:, :NOPE], kr], axis=1)
    if transposed:
        s = lax.dot_general(k, q, NT, preferred_element_type=F32) * scale
    else:
        s = lax.dot_general(q, k, NT, preferred_element_type=F32) * scale
    if masked:
        r = lax.broadcasted_iota(jnp.int32, s.shape, 0)
        c = lax.broadcasted_iota(jnp.int32, s.shape, 1)
        s = jnp.where((r <= c) if transposed else (c <= r), s, -jnp.inf)
    return s, k


def _in_pairs(lo, hi, pair, single):
    n = hi - lo

    def body(p, carry):
        pair(lo + 2 * p, lo + 2 * p + 1)
        return carry

    lax.fori_loop(0, n // 2, body, 0)

    @pl.when(n % 2 == 1)
    def _():
        single(hi - 1)


def _causal_tiles(i, pair, single):
    @pl.when(i == 0)
    def _():
        single(i, True)

    @pl.when(i > 0)
    def _():
        _in_pairs(0, i - 1, lambda a, b: pair(a, b, False), lambda a: single(a, False))
        pair(i - 1, i, True)


def _flash_fwd(q, kvb, krb, tables):
    s_len = q.shape[0]
    t = ATT_TILE

    def body(q_ref, cs_ref, s1_ref, s2_ref, kv_ref, kr_ref, o_ref, lse_ref, qb_ref, m_s, l_s, acc_s):
        qi = pl.program_id(1)
        m_s[...] = jnp.full_like(m_s, -jnp.inf)
        l_s[...] = jnp.zeros_like(l_s)
        acc_s[...] = jnp.zeros_like(acc_s)
        qv = q_ref[...]
        q = jnp.concatenate([qv[:, :NOPE], _rope(qv[:, NOPE:], cs_ref[...], s1_ref[...], s2_ref[...])], axis=1).astype(BF16)
        qb_ref[...] = q

        def scores(ki, masked):
            rows = pl.ds(pl.multiple_of(ki * t, t), t)
            kv = kv_ref[rows, :]
            return _att_scores(q, kv, kr_ref[rows, :], LOG2_SCALE, masked, False)[0], kv

        def update(s, kv):
            m_prev = m_s[...]
            m_new = jnp.maximum(m_prev, jnp.max(s, axis=1, keepdims=True))
            alpha = jnp.exp2(m_prev - m_new)
            p = jnp.exp2(s - jnp.tile(m_new, (1, t // LANE)))
            l_s[...] = alpha * l_s[...] + jnp.sum(p, axis=1, keepdims=True)
            acc_s[...] = alpha * acc_s[...] + jnp.dot(p.astype(BF16), kv[:, NOPE:], preferred_element_type=F32)
            m_s[...] = m_new

        def pair(k0, k1, masked):
            first, second = scores(k0, False), scores(k1, masked)
            update(*first)
            update(*second)

        _causal_tiles(qi, pair, lambda ki, masked: update(*scores(ki, masked)))
        o_ref[...] = (acc_s[...] / l_s[...]).astype(o_ref.dtype)
        lse_ref[...] = (m_s[...] + jnp.log2(l_s[...]))[:, :1]

    table = pl.BlockSpec((t, LANE), lambda h, qi: (qi, 0))
    return pl.pallas_call(
        body, name="flash_fwd", grid=(HEADS, s_len // t),
        in_specs=[pl.BlockSpec((t, HEAD_PAD), lambda h, qi: (qi, h)), table, table, table,
                  pl.BlockSpec((s_len, HEAD_PAD), lambda h, qi: (0, h)),
                  pl.BlockSpec((s_len, LANE), lambda h, qi: (0, 0))],
        out_specs=[pl.BlockSpec((t, VHEAD), lambda h, qi: (qi, h)),
                   pl.BlockSpec((None, t, 1), lambda h, qi: (h, qi, 0)),
                   pl.BlockSpec((t, HEAD_PAD), lambda h, qi: (qi, h))],
        out_shape=[jax.ShapeDtypeStruct((s_len, HEADS * VHEAD), BF16),
                   jax.ShapeDtypeStruct((HEADS, s_len, 1), F32),
                   jax.ShapeDtypeStruct((s_len, HEADS * HEAD_PAD), BF16)],
        scratch_shapes=[pltpu.VMEM((t, LANE), F32), pltpu.VMEM((t, LANE), F32), pltpu.VMEM((t, VHEAD), F32)],
        compiler_params=_params(("parallel", "arbitrary")),
    )(q, *tables, kvb, krb)


def _flash_bwd_dq(qb, kvb, krb, dob, lse, delta, tables):
    s_len = qb.shape[0]
    t = ATT_TILE
    nq = s_len // t
    scale = QK_HEAD ** -0.5

    def body(q_ref, kv_ref, kr_ref, do_ref, lse_ref, dl_ref, cs_ref, s1_ref, s2_ref, dq_ref, acc_s):
        qi = pl.program_id(1)
        acc_s[...] = jnp.zeros_like(acc_s)
        q, do = q_ref[...], do_ref[...]
        lse = jnp.broadcast_to(lse_ref[...], (t, LANE))
        dl = dl_ref[...]

        def products(ki, masked):
            rows = pl.ds(pl.multiple_of(ki * t, t), t)
            kv = kv_ref[rows, :]
            s, k = _att_scores(q, kv, kr_ref[rows, :], LOG2_SCALE, masked, False)
            return s, lax.dot_general(do, kv[:, NOPE:], NT, preferred_element_type=F32), k

        def update(s, dp, k):
            p = jnp.exp2(s - jnp.tile(lse, (1, t // LANE)))
            ds = (p * (dp - jnp.tile(dl, (1, t // LANE))) * scale).astype(BF16)
            acc_s[...] += jnp.dot(ds, k, preferred_element_type=F32)

        def pair(k0, k1, masked):
            first, second = products(k0, False), products(k1, masked)
            update(*first)
            update(*second)

        _causal_tiles(qi, pair, lambda ki, masked: update(*products(ki, masked)))
        dq = acc_s[...]
        dq = jnp.concatenate([dq[:, :NOPE], _rope_t(dq[:, NOPE:], cs_ref[...], s1_ref[...], s2_ref[...])], axis=1)
        dq_ref[...] = dq.astype(dq_ref.dtype)

    col = pl.BlockSpec((None, t, 1), lambda h, qi: (h, qi, 0))
    table = pl.BlockSpec((t, LANE), lambda h, qi: (qi, 0))
    return pl.pallas_call(
        body, name="flash_bwd_dq", grid=(HEADS, nq),
        in_specs=[pl.BlockSpec((t, HEAD_PAD), lambda h, qi: (qi, h)),
                  pl.BlockSpec((s_len, HEAD_PAD), lambda h, qi: (0, h)),
                  pl.BlockSpec((s_len, LANE), lambda h, qi: (0, 0)),
                  pl.BlockSpec((t, VHEAD), lambda h, qi: (qi, h)), col, pl.BlockSpec((t, VHEAD), lambda h, qi: (qi, h)),
                  table, table, table],
        out_specs=pl.BlockSpec((t, HEAD_PAD), lambda h, qi: (qi, h)),
        out_shape=jax.ShapeDtypeStruct((s_len, HEADS * HEAD_PAD), BF16),
        scratch_shapes=[pltpu.VMEM((t, HEAD_PAD), F32)],
        compiler_params=_params(("parallel", "arbitrary")),
    )(qb, kvb, krb, dob, lse, delta, *tables)


def _flash_bwd_dkv(qb, kvb, krb, dob, lse_row, delta_row):
    s_len = qb.shape[0]
    t = ATT_TILE
    nq = s_len // t
    scale = QK_HEAD ** -0.5

    def body(q_ref, kv_ref, kr_ref, do_ref, lse_ref, dl_ref, dkv_ref, dkr_ref, dk_s, dv_s):
        ki = pl.program_id(1)
        dk_s[...] = jnp.zeros_like(dk_s)
        dv_s[...] = jnp.zeros_like(dv_s)
        kv, kr = kv_ref[...], kr_ref[...]

        def products(qi, masked):
            rows = pl.ds(pl.multiple_of(qi * t, t), t)
            q, do = q_ref[rows, :], do_ref[rows, :]
            st, _ = _att_scores(q, kv, kr, LOG2_SCALE, masked, True)
            return st, lax.dot_general(kv[:, NOPE:], do, NT, preferred_element_type=F32), q, do, rows

        def update(st, dpt, q, do, rows):
            pt = jnp.exp2(st - lse_ref[:, rows])
            dv_s[...] += jnp.dot(pt.astype(BF16), do, preferred_element_type=F32)
            dst = (pt * (dpt - dl_ref[:, rows]) * scale).astype(BF16)
            dk_s[...] += jnp.dot(dst, q, preferred_element_type=F32)

        def pair(q0, q1, masked):
            first, second = products(q0, masked), products(q1, False)
            update(*first)
            update(*second)

        @pl.when(ki == nq - 1)
        def _():
            update(*products(ki, True))

        @pl.when(ki < nq - 1)
        def _():
            pair(ki, ki + 1, True)
            _in_pairs(ki + 2, nq, lambda a, b: pair(a, b, False), lambda qi: update(*products(qi, False)))

        dk = dk_s[...]
        dkv_ref[...] = jnp.concatenate([dk[:, :NOPE], dv_s[...]], axis=1).astype(dkv_ref.dtype)
        dkr_ref[...] = dk[:, NOPE:]

    row = pl.BlockSpec((None, 1, s_len), lambda h, ki: (h, 0, 0))
    return pl.pallas_call(
        body, name="flash_bwd_dkv", grid=(HEADS, nq),
        in_specs=[pl.BlockSpec((s_len, HEAD_PAD), lambda h, ki: (0, h)),
                  pl.BlockSpec((t, HEAD_PAD), lambda h, ki: (ki, h)),
                  pl.BlockSpec((t, LANE), lambda h, ki: (ki, 0)),
                  pl.BlockSpec((s_len, VHEAD), lambda h, ki: (0, h)), row, row],
        out_specs=[pl.BlockSpec((t, HEAD_PAD), lambda h, ki: (ki, h)),
                   pl.BlockSpec((t, LANE), lambda h, ki: (ki, h))],
        out_shape=[jax.ShapeDtypeStruct((s_len, HEADS * HEAD_PAD), BF16),
                   jax.ShapeDtypeStruct((s_len, HEADS * LANE), F32)],
        scratch_shapes=[pltpu.VMEM((t, HEAD_PAD), F32), pltpu.VMEM((t, VHEAD), F32)],
        compiler_params=_params(("parallel", "parallel")),
    )(qb, kvb, krb, dob, lse_row, delta_row)


def _tril(w):
    r = lax.broadcasted_iota(jnp.int32, w.shape, 0)
    c = lax.broadcasted_iota(jnp.int32, w.shape, 1)
    return jnp.where(c <= r, w, 0.0)


def _sgu_mix(w_ref, vln, gd):
    wcs = [_tril(w_ref[g]).astype(BF16) for g in range(SGU_GROUPS)]
    mixed = jnp.concatenate(
        [jnp.dot(wcs[g], vln[:, g * gd:(g + 1) * gd], preferred_element_type=F32) for g in range(SGU_GROUPS)], axis=1)
    return wcs, mixed


def _sgu_fwd(zpre, ln_g, ln_b, w_sp, bias_full):
    s_len, two_w = zpre.shape
    width = two_w // 2
    gd = width // SGU_GROUPS
    t = SGU_CHUNK

    def body(z_ref, g_ref, b_ref, w_ref, bias_ref, uv_ref):
        u = _gelu(z_ref[:, :width])
        v = _gelu(z_ref[:, width:])
        d = v - jnp.mean(v, axis=-1, keepdims=True)
        vhat = d * lax.rsqrt(jnp.mean(d * d, axis=-1, keepdims=True) + LN_EPS)
        vln = (vhat * g_ref[...] + b_ref[...]).astype(BF16)
        _, mixed = _sgu_mix(w_ref, vln, gd)
        uv_ref[...] = (u * (mixed + bias_ref[...])).astype(uv_ref.dtype)

    return pl.pallas_call(
        body, name="sgu_fwd", grid=(s_len // t,),
        in_specs=[pl.BlockSpec((t, two_w), lambda i: (i, 0)), pl.BlockSpec((1, width), lambda i: (0, 0)),
                  pl.BlockSpec((1, width), lambda i: (0, 0)), pl.BlockSpec(w_sp.shape, lambda i: (0, 0, 0)),
                  pl.BlockSpec((t, width), lambda i: (0, 0))],
        out_specs=pl.BlockSpec((t, width), lambda i: (i, 0)),
        out_shape=jax.ShapeDtypeStruct((s_len, width), BF16),
        compiler_params=_params(("parallel",)),
    )(zpre, ln_g, ln_b, w_sp, bias_full)


def _sgu_bwd(zpre, duv, ln_g, ln_b, w_sp, bias_full):
    s_len, two_w = zpre.shape
    width = two_w // 2
    gd = width // SGU_GROUPS
    t = SGU_CHUNK

    def body(z_ref, duv_ref, g_ref, b_ref, w_ref, bias_ref, dz_ref, dg_ref, db_ref, dw_ref, dbias_ref):
        first = pl.program_id(0) == 0

        def accumulate(ref, val):
            @pl.when(first)
            def _():
                ref[...] = val

            @pl.when(jnp.logical_not(first))
            def _():
                ref[...] += val

        u, u_grad = _gelu_and_grad(z_ref[:, :width])
        v, v_grad = _gelu_and_grad(z_ref[:, width:])
        d = v - jnp.mean(v, axis=-1, keepdims=True)
        rstd = lax.rsqrt(jnp.mean(d * d, axis=-1, keepdims=True) + LN_EPS)
        vhat = d * rstd
        vln = (vhat * g_ref[...] + b_ref[...]).astype(BF16)
        wcs, mixed = _sgu_mix(w_ref, vln, gd)
        duv_v = duv_ref[...]
        du = duv_v * (mixed + bias_ref[...])
        dmixed = duv_v * u
        dmb = dmixed.astype(BF16)
        dvln = jnp.concatenate(
            [lax.dot_general(wcs[g], dmb[:, g * gd:(g + 1) * gd], TN, preferred_element_type=F32)
             for g in range(SGU_GROUPS)], axis=1)
        for g in range(SGU_GROUPS):
            dw = lax.dot_general(dmb[:, g * gd:(g + 1) * gd], vln[:, g * gd:(g + 1) * gd], NT, preferred_element_type=F32)
            accumulate(dw_ref.at[g], _tril(dw))
        dvhat = dvln * g_ref[...]
        dv0 = rstd * (dvhat - jnp.mean(dvhat, axis=-1, keepdims=True)
                      - vhat * jnp.mean(dvhat * vhat, axis=-1, keepdims=True))
        dz_ref[:, :width] = (du * u_grad).astype(dz_ref.dtype)
        dz_ref[:, width:] = (dv0 * v_grad).astype(dz_ref.dtype)
        accumulate(dg_ref, jnp.sum(dvln * vhat, axis=0, keepdims=True))
        accumulate(db_ref, jnp.sum(dvln, axis=0, keepdims=True))
        accumulate(dbias_ref, dmixed)

    vec = pl.BlockSpec((1, width), lambda i: (0, 0))
    return pl.pallas_call(
        body, name="sgu_bwd", grid=(s_len // t,),
        in_specs=[pl.BlockSpec((t, two_w), lambda i: (i, 0)), pl.BlockSpec((t, width), lambda i: (i, 0)), vec, vec,
                  pl.BlockSpec(w_sp.shape, lambda i: (0, 0, 0)), pl.BlockSpec((t, width), lambda i: (0, 0))],
        out_specs=[pl.BlockSpec((t, two_w), lambda i: (i, 0)), vec, vec,
                   pl.BlockSpec(w_sp.shape, lambda i: (0, 0, 0)), pl.BlockSpec((t, width), lambda i: (0, 0))],
        out_shape=[jax.ShapeDtypeStruct((s_len, two_w), BF16), jax.ShapeDtypeStruct((1, width), F32),
                   jax.ShapeDtypeStruct((1, width), F32), jax.ShapeDtypeStruct(w_sp.shape, F32),
                   jax.ShapeDtypeStruct((t, width), F32)],
        compiler_params=_params(("arbitrary",)),
    )(zpre, duv, ln_g, ln_b, w_sp, bias_full)


def _rope_tables(positions):
    inv_freq = ROPE_THETA ** (-jnp.arange(0, ROPE, 2, dtype=F32) / ROPE)
    ang = positions.astype(F32)[:, None] * inv_freq
    cos, sin = jnp.cos(ang), jnp.sin(ang)
    z32, z64 = jnp.zeros_like(cos), jnp.zeros((cos.shape[0], LANE - ROPE), F32)
    return (jnp.concatenate([cos, cos, z64], axis=1), jnp.concatenate([z32, sin, z64], axis=1),
            jnp.concatenate([-sin, z32, z64], axis=1))


def _ffn_fwd(x, g, w_up3, w_down3):
    def sq_relu(acc):
        r = jnp.maximum(acc, 0.0)
        return r * r, 2.0 * r

    r, r_grad, h2 = _mm_stacked("ffn_up", x, w_up3, "col", (BF16, BF16), sq_relu, norm_g=g)
    x_out = _mm_stacked("ffn_down", r, w_down3, "row", (F32,), lambda acc, res: (acc + res,), [x])[0]
    return x_out, (x, h2, r, r_grad)


def _ffn_bwd(dx, dxb, saved, g, w_up3, w_down3):
    x, h2, r, r_grad = saved
    da = _mm_stacked_nt("ffn_down_dx", dxb, w_down3, "row", (BF16,),
                        lambda acc, rg: (acc * rg.astype(F32),), [r_grad])[0]
    g_down = _mm_tn_stacked("ffn_down_dw", r, dxb, w_down3.shape, "row")
    dx, dxb, dg = _mm_stacked_nt("ffn_up_dx", da, w_up3, "col", **_norm_bwd(x, g, dx))
    g_up = _mm_tn_stacked("ffn_up_dw", h2, da, w_up3.shape, "col")
    return dx, dxb, dg, g_up, g_down


def _norm_bwd(x, g, dres):
    def epilogue(dh, xv, rv, gv):
        dxv, dg = _rms_bwd(dh, xv, gv)
        return dxv + rv, dxv + rv, dg

    return dict(out_dtypes=(F32, BF16), epilogue=epilogue, extras=[x, dres], consts=[g], sums=[g.shape], tm=MM_TILE // 2)


def _dot(a, b, dims=NN):
    return lax.dot_general(a.astype(BF16), b.astype(BF16), dims, preferred_element_type=F32)


def _mla_fwd(x, g, wdkv, q_norm, kv_norm, wq, wkv, wo, tables):
    d = x.shape[1]

    def project(xv, cs, s1, s2, gv, wdkv_v, qg, kg, wq_v, wkv_v):
        h = _rms_fwd(xv, gv).astype(BF16)
        lv = _dot(h, wdkv_v)
        cqn = _rms_fwd(lv[:, :Q_RANK], qg).astype(BF16)
        ckvn = _rms_fwd(lv[:, Q_RANK:Q_RANK + KV_RANK], kg).astype(BF16)
        return (h, lv, cqn, ckvn, _rope(lv[:, Q_RANK + KV_RANK:], cs, s1, s2), _dot(cqn, wq_v), _dot(ckvn, wkv_v))

    h, lat, cqn, ckvn, krb, q, kvb = _rowwise(
        "mla_project", project, [x, *tables], [g, wdkv, q_norm, kv_norm, wq, wkv],
        [(d, BF16), (LAT_PAD, F32), (Q_RANK, BF16), (KV_RANK, BF16), (LANE, BF16), (wq.shape[1], F32), (wkv.shape[1], BF16)],
        tr=512)
    ob, lse, qb = _flash_fwd(q, kvb, krb, tables)
    x_mid = _mm("mla_o", ob, wo, (F32,), lambda acc, res: (acc + res,), [x])[0]
    return x_mid, (x, h, lat, cqn, ckvn, krb, qb, kvb, ob, lse)


def _mla_bwd(dx, dxb, saved, g, wdkv, q_norm, kv_norm, wq, wkv, wo, tables):
    x, h, lat, cqn, ckvn, krb, qb, kvb, ob, lse = saved
    s_len = x.shape[0]
    def with_delta(do, ov):
        prod = do * ov.astype(F32)
        sums = [jnp.broadcast_to(jnp.sum(prod[:, hd * VHEAD:(hd + 1) * VHEAD], axis=1, keepdims=True), (do.shape[0], VHEAD))
                for hd in range(HEADS)]
        lane = lax.broadcasted_iota(jnp.int32, sums[0].shape, 1)
        by_lane = sum(jnp.where(lane == hd, s, 0.0) for hd, s in enumerate(sums))
        return do, jnp.concatenate(sums, axis=1), by_lane

    dob, delta, delta_lanes = _mm("mla_o_dx", dxb, wo, (BF16, F32), with_delta, [ob], nt=True, tm=MM_TILE // 2,
                                  narrow=[(LANE, F32)])
    g_wo = _mm_tn("mla_o_dw", ob, dxb)[0]
    delta_row = delta_lanes[:, :HEADS].T.reshape(HEADS, 1, s_len)
    dqb = _flash_bwd_dq(qb, kvb, krb, dob, lse, delta, tables)
    dkvb, dkr = _flash_bwd_dkv(qb, kvb, krb, dob, lse.reshape(HEADS, 1, s_len), delta_row)
    def project_bwd(dq_v, dkv_v, dkr_v, lv, cqn_v, ckvn_v, h_v, xv, rv, cs, s1, s2, gv, qg, kg, wq_v, wkv_v, wdkv_v):
        dcq, dqg = _rms_bwd(_dot(dq_v, wq_v, NT), lv[:, :Q_RANK], qg)
        dckv, dkg = _rms_bwd(_dot(dkv_v, wkv_v, NT), lv[:, Q_RANK:Q_RANK + KV_RANK], kg)
        dkr_sum = dkr_v[:, :LANE]
        for hd in range(1, HEADS):
            dkr_sum = dkr_sum + dkr_v[:, hd * LANE:(hd + 1) * LANE]
        dlat = jnp.concatenate([dcq, dckv, _rope_t(dkr_sum, cs, s1, s2)], axis=1).astype(BF16)
        dxv, dg = _rms_bwd(_dot(dlat, wdkv_v, NT), xv, gv)
        return (dxv + rv, dxv + rv, _dot(cqn_v, dq_v, TN), _dot(ckvn_v, dkv_v, TN), _dot(h_v, dlat, TN), dqg, dkg, dg)

    dx, dxb, g_wq, g_wkv, g_wdkv, g_qn, g_kvn, dg = _rowwise(
        "mla_project_bwd", project_bwd, [dqb, dkvb, dkr, lat, cqn, ckvn, h, x, dx, *tables],
        [g, q_norm, kv_norm, wq, wkv, wdkv], [(x.shape[1], F32), (x.shape[1], BF16)],
        [wq.shape, wkv.shape, wdkv.shape, q_norm.shape, kv_norm.shape, g.shape], tr=256)
    return dx, dxb, dg, g_wdkv, g_qn, g_kvn, g_wq, g_wkv, g_wo


def _sgu_layer_fwd(x, g, w_in3, ln_g, ln_b, w_sp, bias_full, w_out3):
    zpre, h = _mm_stacked("sgu_in", x, w_in3, "col", norm_g=g)
    uv = _sgu_fwd(zpre, ln_g, ln_b, w_sp, bias_full)
    x_mid = _mm_stacked("sgu_out", uv, w_out3, "row", (F32,), lambda acc, res: (acc + res,), [x])[0]
    return x_mid, (x, h, zpre, uv)


def _sgu_layer_bwd(dx, dxb, saved, g, w_in3, ln_g, ln_b, w_sp, bias_full, w_out3):
    x, h, zpre, uv = saved
    duv = _mm_stacked_nt("sgu_out_dx", dxb, w_out3, "row")[0]
    g_out = _mm_tn_stacked("sgu_out_dw", uv, dxb, w_out3.shape, "row")
    dz, g_lng, g_lnb, g_wsp, g_bias = _sgu_bwd(zpre, duv, ln_g, ln_b, w_sp, bias_full)
    g_in = _mm_tn_stacked("sgu_in_dw", h, dz, w_in3.shape, "col")
    dx, dxb, dg = _mm_stacked_nt("sgu_in_dx", dz, w_in3, "col", **_norm_bwd(x, g, dx))
    return dx, dxb, dg, g_in, g_out, g_lng, g_lnb, g_wsp, g_bias


def _loss_head(x, target, g):
    d = x.shape[1]

    def fn(xv, tv, gv):
        err = _rms_fwd(xv, gv) - tv
        dxv, dg = _rms_bwd(err * (1.0 / d), xv, gv)
        return dxv, dxv, dg, jnp.sum(err * err, axis=0, keepdims=True)

    return _rowwise("loss_head", fn, [x, target], [g], [(d, F32), (d, BF16)], [g.shape, g.shape])


def _mixer_weights(i, stacks):
    by_rows = lambda a: a.reshape(N_SHARDS * a.shape[1], a.shape[2])
    by_cols = lambda a: a.transpose(1, 0, 2).reshape(a.shape[1], N_SHARDS * a.shape[2])
    if i % 2:
        w_in3, w_out3, ln_g, ln_b = stacks
        return w_in3, ln_g.reshape(1, -1), ln_b.reshape(1, -1), w_out3
    wdkv = by_rows(stacks[0])
    wdkv = jnp.pad(wdkv, ((0, 0), (0, LAT_PAD - wdkv.shape[1])))
    wq = jnp.pad(by_cols(stacks[1]).reshape(Q_RANK, HEADS, QK_HEAD), ((0, 0), (0, 0), (0, HEAD_PAD - QK_HEAD)))
    return wdkv, wq.reshape(Q_RANK, HEADS * HEAD_PAD), by_cols(stacks[2]), by_rows(stacks[3])


def _local_step(x, positions, target, norm_mix, norm_ffn, final_norm, q_norm, kv_norm, w_sp, b_sp, mixers, ffn, reducer):
    tables = _rope_tables(positions)
    gd = mixers[1][2].size // SGU_GROUPS
    bias_full = [jnp.repeat(b_sp[j].T, gd, axis=1) for j in range(DEPTH // 2)]
    saved, mla, sgu = [], [None] * (DEPTH // 2), [None] * (DEPTH // 2)
    for i in range(DEPTH):
        j = i // 2
        x, *stacks = lax.optimization_barrier((x, *mixers[i]))
        if i % 2 == 0:
            wdkv, wq, wkv, wo = mla[j] = _mixer_weights(i, stacks)
            x, s_mix = _mla_fwd(x, norm_mix[i:i + 1], wdkv, q_norm[j:j + 1], kv_norm[j:j + 1], wq, wkv, wo, tables)
        else:
            w_in3, ln_g, ln_b, w_out3 = sgu[j] = _mixer_weights(i, stacks)
            x, s_mix = _sgu_layer_fwd(x, norm_mix[i:i + 1], w_in3, ln_g, ln_b, w_sp[j], bias_full[j], w_out3)
        x, s_ffn = _ffn_fwd(x, norm_ffn[i:i + 1], *ffn[i])
        saved.append((s_mix, s_ffn))
    dx, dxb, g_final, sq_cols = _loss_head(x, target, final_norm[None, :])
    loss = 0.5 * jnp.sum(sq_cols) / x.shape[1]

    def pair(g):
        return g, g.astype(BF16)

    g_mix, g_ffn = [None] * DEPTH, [None] * DEPTH
    mla_g, sgu_g = [None] * (DEPTH // 2), [None] * (DEPTH // 2)
    for i in reversed(range(DEPTH)):
        j = i // 2
        s_mix, s_ffn = saved[i]
        dx, dxb, g_ffn[i], g_up, g_down = _ffn_bwd(dx, dxb, s_ffn, norm_ffn[i:i + 1], *ffn[i])
        dxb = reducer.add(f"ffn{i}", i, {"ffn_w_up": g_up, "ffn_w_down": g_down}, dxb)
        dxb = reducer.phase_end(dxb)
        if i % 2 == 0:
            wdkv, wq, wkv, wo = mla[j]
            dx, dxb, g_mix[i], g_wdkv, g_qn, g_kvn, g_wq, g_wkv, g_wo = _mla_bwd(
                dx, dxb, s_mix, norm_mix[i:i + 1], wdkv, q_norm[j:j + 1], kv_norm[j:j + 1], wq, wkv, wo, tables)
            mla_g[j] = (g_qn, g_kvn)
            g_wq = g_wq.reshape(Q_RANK, HEADS, HEAD_PAD)[..., :QK_HEAD].reshape(Q_RANK, N_SHARDS, -1)
            dxb = reducer.add(f"mla{j}", j, {
                "mla_w_dkv": pair(g_wdkv[:, :Q_RANK + KV_RANK + ROPE].reshape(N_SHARDS, -1, Q_RANK + KV_RANK + ROPE)),
                "mla_w_uq": pair(g_wq.transpose(1, 0, 2)),
                "mla_w_ukv": pair(g_wkv.reshape(KV_RANK, N_SHARDS, -1).transpose(1, 0, 2)),
                "mla_w_o": pair(g_wo.reshape(N_SHARDS, -1, g_wo.shape[1]))}, dxb)
        else:
            w_in3, ln_g, ln_b, w_out3 = sgu[j]
            dx, dxb, g_mix[i], g_in, g_out, g_lng, g_lnb, g_wsp, g_bias = _sgu_layer_bwd(
                dx, dxb, s_mix, norm_mix[i:i + 1], w_in3, ln_g, ln_b, w_sp[j], bias_full[j], w_out3)
            sgu_g[j] = (g_wsp, g_bias.reshape(SGU_CHUNK, SGU_GROUPS, gd).sum(axis=-1).T)
            dxb = reducer.add(f"sgu{j}", j, {"sgu_w_in": g_in, "sgu_w_out": g_out,
                                             "sgu_ln_g": pair(g_lng.reshape(N_SHARDS, -1, LANE)),
                                             "sgu_ln_b": pair(g_lnb.reshape(N_SHARDS, -1, LANE))}, dxb)
        dxb = reducer.phase_end(dxb)
    small = dict(
        norm_mix=jnp.concatenate(g_mix, axis=0), norm_ffn=jnp.concatenate(g_ffn, axis=0), final_norm=g_final[0],
        q_norm=jnp.concatenate([m[0] for m in mla_g], axis=0), kv_norm=jnp.concatenate([m[1] for m in mla_g], axis=0),
        w_sp=jnp.stack([s[0] for s in sgu_g]), b_sp=jnp.stack([s[1] for s in sgu_g]))
    return loss, dx, small


HBM_SPEC = pl.BlockSpec(memory_space=pltpu.HBM)


def _place():
    x, y, c = lax.axis_index("x"), lax.axis_index("y"), lax.axis_index("c")
    return x, y, c, [(1 - x, y), (x, 1 - y), (1 - x, 1 - y)]


def _remote(src, dst, send_sems, recv_sems, k, to):
    return pltpu.make_async_remote_copy(src_ref=src, dst_ref=dst, send_sem=send_sems.at[k], recv_sem=recv_sems.at[k],
                                        device_id=to, device_id_type=MESH)


def _gather_layer(tag, shards):
    n = len(shards)
    split = [s.shape[0] >= 16 for s in shards]

    def body(*refs):
        ins, outs = refs[:n], refs[n:2 * n]
        send_sems, recv_sems, local_sems = refs[2 * n:]
        x, y, c, chips = _place()
        mine = 2 * x + y
        barrier = pltpu.get_barrier_semaphore()
        peers = [(x, y, 1 - c)] + [(*chip, c) for chip in chips]
        for peer in peers:
            pl.semaphore_signal(barrier, inc=1, device_id=peer, device_id_type=MESH)
        pl.semaphore_wait(barrier, len(peers))

        def rows(t, half):
            hr = shards[t].shape[0] // 2
            return pl.ds(half * hr, hr) if split[t] else pl.ds(0, shards[t].shape[0])

        local, sent = [], []
        for t in range(n):
            local.append(pltpu.make_async_copy(ins[t], outs[t].at[mine], local_sems.at[t]))
            local[-1].start()
            for j, chip in enumerate(chips):
                cp = _remote(ins[t].at[rows(t, c)], outs[t].at[mine, rows(t, c)], send_sems, recv_sems, 3 * t + j, (*chip, c))
                cp.start()
                sent.append(cp)
        for j, chip in enumerate(chips):
            theirs = 2 * chip[0] + chip[1]
            for t in range(n):
                piece = outs[t].at[theirs, rows(t, c)]
                _remote(piece, piece, send_sems, recv_sems, 3 * t + j, (x, y, c)).wait_recv()
                if split[t]:
                    cp = _remote(piece, piece, send_sems, recv_sems, 3 * n + 3 * t + j, (x, y, 1 - c))
                    cp.start()
                    sent.append(cp)
        for j, chip in enumerate(chips):
            theirs = 2 * chip[0] + chip[1]
            for t in range(n):
                if split[t]:
                    piece = outs[t].at[theirs, rows(t, 1 - c)]
                    _remote(piece, piece, send_sems, recv_sems, 3 * n + 3 * t + j, (x, y, c)).wait_recv()
        for cp in sent:
            cp.wait_send()
        for cp in local:
            cp.wait()

    return pl.kernel(
        body, name=f"gather_{tag}", mesh=plsc.ScalarSubcoreMesh(axis_name="sequencer", num_cores=1),
        out_type=[jax.ShapeDtypeStruct((N_SHARDS, *s.shape), s.dtype) for s in shards],
        scratch_types=[pltpu.SemaphoreType.DMA((6 * n,)), pltpu.SemaphoreType.DMA((6 * n,)), pltpu.SemaphoreType.DMA((n,))],
        compiler_params=pltpu.CompilerParams(collective_id=ID_GATHER),
    )(*shards)


SEQUENCER = dict(axis_name="sequencer", num_cores=1)
ID_GATHER, ID_EXCHANGE, ID_SHARE = 0, 1, 2
MIN_SPLIT_ROWS = 16


def _handshake(peers):
    barrier = pltpu.get_barrier_semaphore()
    for peer in peers:
        pl.semaphore_signal(barrier, inc=1, device_id=peer, device_id_type=MESH)
    pl.semaphore_wait(barrier, len(peers))


def _half_rows(rows, half):
    return pl.ds(half * (rows // 2), rows // 2) if rows >= MIN_SPLIT_ROWS else pl.ds(0, rows)


SEM_SPEC = pl.BlockSpec(memory_space=pltpu.SEMAPHORE)
DATAFLOW = pltpu.SideEffectType.DATAFLOW_SIDE_EFFECTING


def _exchange_copies(shapes, stacks, lands, send_sems, recv_sems):
    x, y, c, chips = _place()
    mine = 2 * x + y
    copies = []
    for t, shape in enumerate(shapes):
        r = shape[1]
        copies.append(_remote(stacks[t].at[mine, _half_rows(r, 1 - c)], lands[t].at[0], send_sems, recv_sems, 7 * t, (x, y, 1 - c)))
        for j, chip in enumerate(chips):
            theirs = 2 * chip[0] + chip[1]
            copies.append(_remote(stacks[t].at[theirs, _half_rows(r, c)], lands[t].at[1 + j], send_sems, recv_sems,
                                  7 * t + 1 + j, (*chip, c)))
            copies.append(_remote(stacks[t].at[theirs, _half_rows(r, 1 - c)], lands[t].at[4 + j], send_sems, recv_sems,
                                  7 * t + 4 + j, (*chip, 1 - c)))
    return copies


def _exchange_start(tag, stacks, carry):
    n = len(stacks)
    shapes = [s.shape for s in stacks]
    lands = [lax.empty((7, s.shape[1] // 2 if s.shape[1] >= MIN_SPLIT_ROWS else s.shape[1], s.shape[2]), s.dtype) for s in stacks]

    def body(*refs):
        for cp in _exchange_copies(shapes, refs[:n], refs[n:2 * n], refs[2 * n + 1], refs[2 * n + 2]):
            cp.start()

    through = (*stacks, *lands, carry)
    out = pl.pallas_call(
        body, name=f"reduce_exchange_start_{tag}",
        out_shape=(pltpu.SemaphoreType.DMA((7 * n,)), pltpu.SemaphoreType.DMA((7 * n,)),
                   *[pltpu.HBM(a.shape, a.dtype) for a in through]),
        in_specs=[HBM_SPEC] * (2 * n + 1),
        out_specs=(SEM_SPEC, SEM_SPEC, *[HBM_SPEC] * (2 * n + 1)),
        input_output_aliases={t: 2 + t for t in range(2 * n + 1)},
        compiler_params=pltpu.CompilerParams(has_side_effects=DATAFLOW),
    )(*[pltpu.with_memory_space_constraint(a, pltpu.HBM) for a in through])
    return out[0], out[1], out[2:2 + n], out[2 + n:2 + 2 * n], out[-1]


def _exchange_wait(tag, send_sems, recv_sems, stacks, lands, after):
    n = len(stacks)
    shapes = [s.shape for s in stacks]

    def body(*refs):
        for cp in _exchange_copies(shapes, refs[:n], refs[n:2 * n], refs[2 * n], refs[2 * n + 1]):
            cp.wait()

    out = pl.pallas_call(
        body, name=f"reduce_exchange_wait_{tag}",
        out_shape=tuple(pltpu.HBM(a.shape, a.dtype) for a in (*stacks, *lands)),
        in_specs=[HBM_SPEC] * (2 * n) + [SEM_SPEC, SEM_SPEC, pl.BlockSpec(memory_space=pl.ANY)],
        out_specs=tuple([HBM_SPEC] * (2 * n)),
        input_output_aliases={t: t for t in range(2 * n)},
        compiler_params=pltpu.CompilerParams(has_side_effects=DATAFLOW),
    )(*stacks, *lands, send_sems, recv_sems, after)
    return out[:n], out[n:]


def _share_halves(tag, halves):
    n = len(halves)

    def body(*refs):
        ins, outs, send_sems, recv_sems = refs[:n], refs[n:2 * n], refs[2 * n], refs[2 * n + 1]
        x, y, c, _ = _place()
        _handshake([(x, y, 1 - c)])
        sent = [_remote(ins[t], outs[t], send_sems, recv_sems, t, (x, y, 1 - c)) for t in range(n)]
        for cp in sent:
            cp.start()
        for cp in sent:
            cp.wait()

    return pl.kernel(
        body, name=f"reduce_share_{tag}", mesh=plsc.ScalarSubcoreMesh(**SEQUENCER),
        out_type=[jax.ShapeDtypeStruct(h.shape, h.dtype) for h in halves],
        scratch_types=[pltpu.SemaphoreType.DMA((n,)), pltpu.SemaphoreType.DMA((n,))],
        compiler_params=pltpu.CompilerParams(collective_id=ID_SHARE),
    )(*halves)


def _all_reduce_small(part):
    rows = part.shape[0]

    def body(p_ref, out_ref, sib_buf, chip_sums, send_sems, recv_sems):
        x, y, c, chips = _place()
        mine = 2 * x + y
        swap = _remote(p_ref, sib_buf, send_sems, recv_sems, 0, (x, y, 1 - c))
        swap.start()
        swap.wait()
        chip_sums[mine] = p_ref[...] + sib_buf[...]
        sent = [_remote(chip_sums.at[mine], chip_sums.at[mine], send_sems, recv_sems, 1 + j, (*chip, c))
                for j, chip in enumerate(chips)]
        for cp in sent:
            cp.start()
        for j, chip in enumerate(chips):
            sent[j].wait_send()
            theirs = chip_sums.at[2 * chip[0] + chip[1]]
            _remote(theirs, theirs, send_sems, recv_sems, 1 + j, (x, y, c)).wait_recv()
        out_ref[...] = ((chip_sums[0] + chip_sums[1]) + chip_sums[2]) + chip_sums[3]

    vmem = pl.BlockSpec(memory_space=pltpu.VMEM)
    return pl.pallas_call(
        body, name="all_reduce_small", in_specs=[vmem], out_specs=vmem, out_shape=jax.ShapeDtypeStruct(part.shape, F32),
        scratch_shapes=[pltpu.VMEM((rows, LANE), F32), pltpu.VMEM((N_SHARDS, rows, LANE), F32),
                        pltpu.SemaphoreType.DMA((4,)), pltpu.SemaphoreType.DMA((4,))],
        compiler_params=pltpu.CompilerParams(vmem_limit_bytes=VMEM_LIMIT_BYTES),
    )(part)


def _sum_partials(g3, others, sel):
    _, rows, c = others.shape
    whole = g3.shape[1] == rows
    tr = _tile(rows, 512)
    nb = rows // tr

    def body(sel_ref, g_ref, *rest):
        same = g_ref[...].astype(F32)
        for ref in rest[1:4]:
            same = same + ref[...].astype(F32)
        other = rest[0][...].astype(F32)
        for ref in rest[4:7]:
            other = other + ref[...].astype(F32)
        rest[7][...] = same + other

    blk = (None, tr, c)
    slots = [pl.BlockSpec(blk, functools.partial(lambda i, sr, k: (k, i, 0), k=k)) for k in range(7)]
    return pl.pallas_call(
        body, name="reduce_sum_partials",
        grid_spec=pltpu.PrefetchScalarGridSpec(
            num_scalar_prefetch=1, grid=(nb,),
            in_specs=[pl.BlockSpec(blk, lambda i, sr: (sr[0], (0 if whole else sr[1] * nb) + i, 0))] + slots,
            out_specs=pl.BlockSpec((tr, c), lambda i, sr: (i, 0))),
        out_shape=jax.ShapeDtypeStruct((rows, c), F32),
        compiler_params=_params(("parallel",)),
    )(sel, g3, *[others] * 7)


def _adamw_math(w, g, m, v):
    nm = ADAM_B1 * m + (1.0 - ADAM_B1) * g
    nv = ADAM_B2 * v + (1.0 - ADAM_B2) * (g * g)
    m_hat = nm / (1.0 - ADAM_B1 ** ADAM_STEP)
    v_hat = nv / (1.0 - ADAM_B2 ** ADAM_STEP)
    return -ADAM_LR * (m_hat / (jnp.sqrt(v_hat) + ADAM_EPS) + ADAM_WD * w), nm, nv


def _adamw_layer(layer, w, m, v, g_mine, g_sibling, sel, prev):
    lyr, r, c = w.shape
    rows = g_mine.shape[0]
    halves = r // rows
    tr = _tile(rows, 512)
    nb = rows // tr
    n_g = 1 if g_sibling is None else 2

    def body(sel_ref, w_ref, m_ref, v_ref, *rest):
        g = rest[0][...]
        if n_g == 2:
            g = jnp.where(pl.program_id(0) == sel_ref[1], g, rest[1][...])
        outs = rest[n_g + (0 if prev is None else 4):]
        d, nm, nv = _adamw_math(w_ref[...], g, m_ref[...], v_ref[...])
        for ref, val in zip(outs, (g, d, nm, nv)):
            ref[...] = val

    full = pl.BlockSpec((None, tr, c), lambda h, i, sr: (layer, h * nb + i, 0))
    part = pl.BlockSpec((tr, c), lambda h, i, sr: (i, 0))
    n_in = 4 + n_g
    return pl.pallas_call(
        body, name="adamw_layer",
        grid_spec=pltpu.PrefetchScalarGridSpec(
            num_scalar_prefetch=1, grid=(halves, nb),
            in_specs=[full] * 3 + [part] * n_g + ([] if prev is None else [pl.BlockSpec(memory_space=pl.ANY)] * 4),
            out_specs=[full] * 4),
        out_shape=[jax.ShapeDtypeStruct(w.shape, F32)] * 4,
        input_output_aliases={} if prev is None else {n_in + k: k for k in range(4)},
        compiler_params=_params(("parallel", "parallel")),
    )(sel, w, m, v, g_mine, *([] if g_sibling is None else [g_sibling]), *([] if prev is None else prev))


def _adamw(w, g, m, v):
    lyr, r, c = w.shape
    tr = _tile(r, 512)

    def body(w_ref, g_ref, m_ref, v_ref, d_ref, nm_ref, nv_ref):
        gv = g_ref[...]
        nm = ADAM_B1 * m_ref[...] + (1.0 - ADAM_B1) * gv
        nv = ADAM_B2 * v_ref[...] + (1.0 - ADAM_B2) * (gv * gv)
        m_hat = nm / (1.0 - ADAM_B1 ** ADAM_STEP)
        v_hat = nv / (1.0 - ADAM_B2 ** ADAM_STEP)
        d_ref[...] = -ADAM_LR * (m_hat / (jnp.sqrt(v_hat) + ADAM_EPS) + ADAM_WD * w_ref[...])
        nm_ref[...] = nm
        nv_ref[...] = nv

    blk = pl.BlockSpec((None, tr, c), lambda l, i: (l, i, 0))
    return pl.pallas_call(
        body, name="adamw", grid=(lyr, r // tr), in_specs=[blk] * 4, out_specs=[blk] * 3,
        out_shape=[jax.ShapeDtypeStruct(w.shape, F32)] * 3,
        compiler_params=_params(("parallel", "parallel")),
    )(w, g, m, v)


SHARDED = ("mla_w_dkv", "mla_w_uq", "mla_w_ukv", "mla_w_o", "sgu_w_in", "sgu_ln_g", "sgu_ln_b", "sgu_w_out",
           "ffn_w_up", "ffn_w_down")
REPLICATED = ("norm_mix", "norm_ffn", "final_norm", "mla_q_norm", "mla_kv_norm", "sgu_w_spatial", "sgu_b_spatial")
WEIGHTS = ("norm_mix", "norm_ffn", "final_norm", "mla_w_dkv", "mla_q_norm", "mla_kv_norm", "mla_w_uq", "mla_w_ukv",
           "mla_w_o", "sgu_w_in", "sgu_ln_g", "sgu_ln_b", "sgu_w_spatial", "sgu_b_spatial", "sgu_w_out", "ffn_w_up",
           "ffn_w_down")


class _Reducer:
    def __init__(self, state, sel):
        self.state, self.sel = state, sel
        self.started, self.travelling, self.summed = [], [], []
        self.done = {}

    def add(self, tag, layer, grads, token):
        names = list(grads)
        token, *tied = lax.optimization_barrier((token, *[a for n in names for a in grads[n]]))
        f32s, bf16s = tied[0::2], tied[1::2]
        *flying, token = _exchange_start(tag, bf16s, token)
        self.started.append((tag, layer, names, f32s, flying))
        return token

    def phase_end(self, token):
        for tag, layer, names, f32s, flying in self.travelling:
            bf16s, received = _exchange_wait(tag, *flying, token)
            own = [g if g.shape[1] >= MIN_SPLIT_ROWS else gb for g, gb in zip(f32s, bf16s)]
            mine = [_sum_partials(g, got, self.sel) for g, got in zip(own, received)]
            token, *mine = lax.optimization_barrier((token, *mine))
            cut = [k for k, g in enumerate(own) if g.shape[1] >= MIN_SPLIT_ROWS]
            theirs = dict(zip(cut, _share_halves(tag, [mine[k] for k in cut])))
            self.summed.append((layer, names, mine, [theirs.get(k) for k in range(len(names))]))
        self.travelling, self.started = self.started, []
        return token

    def update(self, token):
        for layer, names, mine, theirs in self.summed:
            for name, g_mine, g_theirs in zip(names, mine, theirs):
                w, m, v = self.state[name]
                self.done[name] = _adamw_layer(layer, w, m, v, g_mine, g_theirs, self.sel, self.done.get(name))
                token = self.done[name][1]
        self.summed = []
        return token


def _as3d(name, a):
    return a.reshape(a.shape[0], -1, LANE) if name in ("sgu_ln_g", "sgu_ln_b") else a


def _pack(parts):
    flat = jnp.concatenate([p.reshape(-1) for p in parts])
    rows = -(-flat.shape[0] // (8 * LANE)) * 8
    return jnp.pad(flat, (0, rows * LANE - flat.shape[0])).reshape(rows, LANE)


def _unpack(packed, like):
    flat, out, at = packed.reshape(-1), [], 0
    for p in like:
        out.append(flat[at:at + p.size].reshape(p.shape))
        at += p.size
    return out


def kernel(x, positions, norm_mix, norm_ffn, final_norm, mla_w_dkv, mla_q_norm, mla_kv_norm, mla_w_uq, mla_w_ukv, mla_w_o, sgu_w_in, sgu_ln_g, sgu_ln_b, sgu_w_spatial, sgu_b_spatial, sgu_w_out, ffn_w_up, ffn_w_down, loss_target, m_norm_mix, m_norm_ffn, m_final_norm, m_mla_w_dkv, m_mla_q_norm, m_mla_kv_norm, m_mla_w_uq, m_mla_w_ukv, m_mla_w_o, m_sgu_w_in, m_sgu_ln_g, m_sgu_ln_b, m_sgu_w_spatial, m_sgu_b_spatial, m_sgu_w_out, m_ffn_w_up, m_ffn_w_down, v_norm_mix, v_norm_ffn, v_final_norm, v_mla_w_dkv, v_mla_q_norm, v_mla_kv_norm, v_mla_w_uq, v_mla_w_ukv, v_mla_w_o, v_sgu_w_in, v_sgu_ln_g, v_sgu_ln_b, v_sgu_w_spatial, v_sgu_b_spatial, v_sgu_w_out, v_ffn_w_up, v_ffn_w_down):
    given = dict(locals())
    w = {n: given[n] for n in WEIGHTS}
    mom = {n: given["m_" + n] for n in WEIGHTS}
    var = {n: given["v_" + n] for n in WEIGHTS}
    mixers, ffn, token = [], [], None
    for i in range(DEPTH):
        j = i // 2
        if i % 2 == 0:
            mixer = [w[n][j].astype(BF16) for n in ("mla_w_dkv", "mla_w_uq", "mla_w_ukv", "mla_w_o")]
        else:
            mixer = [sgu_w_in[j].astype(BF16), sgu_w_out[j].astype(BF16), sgu_ln_g[j].reshape(-1, LANE),
                     sgu_ln_b[j].reshape(-1, LANE)]
        for tag, shards, into in ((f"mixer{i}", mixer, mixers), (f"ffn{i}", [ffn_w_up[i].astype(BF16), ffn_w_down[i].astype(BF16)], ffn)):
            if token is None:
                token = shards[0]
            else:
                token, *shards = lax.optimization_barrier((token, *shards))
            into.append(_gather_layer(tag, shards))

    x_i, y_i, c_i = lax.axis_index("x"), lax.axis_index("y"), lax.axis_index("c")
    sel = jnp.stack([2 * x_i + y_i, c_i]).astype(jnp.int32)
    reducer = _Reducer({n: tuple(_as3d(n, d[n]) for d in (w, mom, var)) for n in SHARDED}, sel)
    loss, dx, small = _local_step(
        x[0], positions[0], loss_target[0], norm_mix, norm_ffn, final_norm, mla_q_norm, mla_kv_norm, sgu_w_spatial,
        sgu_b_spatial, mixers, ffn, reducer)
    loss = lax.psum(loss, ("x", "y", "c"))

    small_g = [small["norm_mix"], small["norm_ffn"], small["final_norm"], small["q_norm"], small["kv_norm"],
               small["w_sp"], small["b_sp"]]
    like = [w[n] for n in REPLICATED]
    g_small = _all_reduce_small(_pack(small_g))
    packed = [_pack([d[n] for n in REPLICATED])[None] for d in (w, mom, var)]
    upd_small = _adamw(packed[0], g_small[None], packed[1], packed[2])
    grads = dict(zip(REPLICATED, _unpack(g_small, like)))
    delta, new_m, new_v = ({n: a for n, a in zip(REPLICATED, _unpack(u[0], like))} for u in upd_small)

    reducer.phase_end(reducer.update(upd_small[0]))
    reducer.update(None)
    for n in SHARDED:
        grads[n], delta[n], new_m[n], new_v[n] = (a.reshape(w[n].shape) for a in reducer.done[n])

    return (loss, dx[None], *[grads[n] for n in WEIGHTS], *[delta[n] for n in WEIGHTS],
            *[new_m[n] for n in WEIGHTS], *[new_v[n] for n in WEIGHTS])
```

```python
import functools
import math

import jax
import jax.numpy as jnp
from jax import lax
from jax.experimental import pallas as pl
from jax.experimental.pallas import tpu as pltpu
from jax.experimental.pallas import tpu_sc as plsc

F32 = jnp.float32
BF16 = jnp.bfloat16
MESH = pl.DeviceIdType.MESH

DEPTH = 4
HEADS = 8
NOPE = 128
ROPE = 64
VHEAD = 128
QK_HEAD = NOPE + ROPE
Q_RANK = 256
KV_RANK = 128
HEAD_PAD = 256
LAT_PAD = 512
ROPE_THETA = 10000.0
SGU_CHUNK = 128
SGU_GROUPS = 8
NORM_EPS = 1e-6
LN_EPS = 1e-5
ADAM_LR, ADAM_B1, ADAM_B2, ADAM_EPS, ADAM_WD, ADAM_STEP = 0.001, 0.9, 0.999, 1e-08, 0.01, 10

N_SHARDS = 4
LANE = 128
VMEM_LIMIT_BYTES = 56 * 1024 * 1024
ATT_TILE = 512
MM_TILE = 1024
ATT_SCALE = QK_HEAD ** -0.5
LOG2_SCALE = ATT_SCALE * math.log2(math.e)

NN = (((1,), (0,)), ((), ()))
NT = (((1,), (1,)), ((), ()))
TN = (((0,), (0,)), ((), ()))


def _params(sem):
    return pltpu.CompilerParams(dimension_semantics=sem, vmem_limit_bytes=VMEM_LIMIT_BYTES)


def _tile(n, pref):
    t = min(n, pref)
    while n % t:
        t //= 2
    return t


def _matmul(name, a, b, a_spec, b_spec, dims, grid, tile, outs, extras=(), epilogue=None, sums=(), norm_g=None):
    nk, ne, no = grid[2], len(extras), len(outs)
    b_specs = list(b_spec) if isinstance(b_spec, (list, tuple)) else [b_spec]
    nb = len(b_specs)
    normed = norm_g is not None
    assert not normed or (nk == 1 and nb == 1)

    def body(a_ref, *rest):
        b_refs, e_refs, o_refs = rest[:nb], rest[nb:nb + ne], rest[nb + ne + normed:nb + ne + normed + no]
        s_refs = rest[nb + ne + normed + no:nb + ne + normed + no + len(sums)]
        if normed:
            g_ref, h_ref, h_s = rest[nb + ne], rest[nb + ne + 1 + no + len(sums)], rest[-1]

            @pl.when(pl.program_id(1) == 0)
            def _():
                h_s[...] = _rms_fwd(a_ref[...], g_ref[...]).astype(BF16)
                h_ref[...] = h_s[...]

            a_ref = h_s
        kw = a_ref.shape[1] // nb
        part = None
        for p, b_ref in enumerate(b_refs):
            a_tile = a_ref[...] if nb == 1 else a_ref[:, p * kw:(p + 1) * kw]
            d = lax.dot_general(a_tile.astype(BF16), b_ref[...].astype(BF16), dims, preferred_element_type=F32)
            part = d if part is None else part + d

        def finish(acc):
            vals = (acc,) if epilogue is None else epilogue(acc, *[e[...] for e in e_refs])
            for o_ref, v in zip(o_refs, vals):
                o_ref[...] = v.astype(o_ref.dtype)
            first = pl.program_id(0) == 0
            for s_ref, v in zip(s_refs, vals[no:]):
                @pl.when(first)
                def _():
                    s_ref[...] = v

                @pl.when(jnp.logical_not(first))
                def _():
                    s_ref[...] += v

        if nk == 1:
            finish(part)
            return
        acc_ref, k = rest[-1], pl.program_id(2)

        @pl.when(k == 0)
        def _():
            acc_ref[...] = part

        @pl.when(jnp.logical_and(k > 0, k < nk - 1))
        def _():
            acc_ref[...] += part

        @pl.when(k == nk - 1)
        def _():
            finish(acc_ref[...] + part)

    assert not sums or (grid[1] == 1 and nk == 1)
    a_block = a_spec.block_shape
    return pl.pallas_call(
        body, name=name, grid=grid,
        in_specs=[a_spec] + b_specs + [s for _, s in extras]
        + ([pl.BlockSpec(norm_g.shape, lambda i, j, k: (0, 0))] if normed else []),
        out_specs=[s for _, s in outs] + [pl.BlockSpec(s, lambda i, j, k: (0,) * len(s)) for s in sums]
        + ([pl.BlockSpec(a_block, lambda i, j, k: (i, 0))] if normed else []),
        out_shape=[s for s, _ in outs] + [jax.ShapeDtypeStruct(s, F32) for s in sums]
        + ([jax.ShapeDtypeStruct(a.shape, BF16)] if normed else []),
        scratch_shapes=([pltpu.VMEM(tile, F32)] if nk > 1 else []) + ([pltpu.VMEM(a_block, BF16)] if normed else []),
        compiler_params=_params(("arbitrary" if sums else "parallel", "arbitrary" if normed else "parallel", "arbitrary")),
    )(a, *[b] * nb, *[e for e, _ in extras], *([norm_g] if normed else []))


def _epilogue_operands(extras, consts, o_spec):
    return [(e, o_spec) for e in extras] + [(c, pl.BlockSpec(c.shape, lambda i, j, k: (0, 0))) for c in consts]


def _mm(name, a, b, out_dtypes=(F32,), epilogue=None, extras=(), tm=MM_TILE, tn=MM_TILE, tk=MM_TILE, nt=False,
        consts=(), sums=(), narrow=()):
    m, kd = a.shape
    n = b.shape[0] if nt else b.shape[1]
    tm, tn, tk = _tile(m, tm), _tile(n, tn), _tile(kd, tk)
    o_spec = pl.BlockSpec((tm, tn), lambda i, j, k: (i, j))
    b_spec = pl.BlockSpec((tn, tk), lambda i, j, k: (j, k)) if nt else pl.BlockSpec((tk, tn), lambda i, j, k: (k, j))
    assert not narrow or n == tn
    outs = [(jax.ShapeDtypeStruct((m, n), d), o_spec) for d in out_dtypes]
    outs += [(jax.ShapeDtypeStruct((m, w), d), pl.BlockSpec((tm, w), lambda i, j, k: (i, 0))) for w, d in narrow]
    return _matmul(name, a, b, pl.BlockSpec((tm, tk), lambda i, j, k: (i, k)), b_spec, NT if nt else NN,
                   (m // tm, n // tn, kd // tk), (tm, tn), outs,
                   _epilogue_operands(extras, consts, o_spec), epilogue, sums)


def _mm_tn(name, a, b, out_dtypes=(F32,), tm=MM_TILE, tn=MM_TILE, tk=MM_TILE):
    s, m = a.shape
    n = b.shape[1]
    tm, tn, tk = _tile(m, tm), _tile(n, tn), _tile(s, tk)
    o_spec = pl.BlockSpec((tm, tn), lambda i, j, k: (i, j))
    return _matmul(name, a, b, pl.BlockSpec((tk, tm), lambda i, j, k: (k, i)),
                   pl.BlockSpec((tk, tn), lambda i, j, k: (k, j)), TN, (m // tm, n // tn, s // tk), (tm, tn),
                   [(jax.ShapeDtypeStruct((m, n), d), o_spec) for d in out_dtypes])


def _mm_stacked(name, a, w3, mode, out_dtypes=(F32,), epilogue=None, extras=(), tm=MM_TILE, tn=MM_TILE, tk=MM_TILE,
                norm_g=None):
    m, kd = a.shape
    _, r, c = w3.shape
    n = c if mode == "row" else N_SHARDS * c
    if mode == "row":
        tm, tn, tk = _tile(m, tm // 2), _tile(n, tn), kd
        b_spec = [pl.BlockSpec((None, r, tn), functools.partial(lambda i, j, k, p: (p, 0, j), p=p)) for p in range(N_SHARDS)]
    else:
        tm, tn, tk = _tile(m, tm), _tile(c, tn), _tile(kd, tk)
        per = c // tn
        b_spec = pl.BlockSpec((None, tk, tn), lambda i, j, k: (j // per, k, j % per))
    o_spec = pl.BlockSpec((tm, tn), lambda i, j, k: (i, j))
    return _matmul(name, a, w3, pl.BlockSpec((tm, tk), lambda i, j, k: (i, k)), b_spec, NN,
                   (m // tm, n // tn, kd // tk), (tm, tn),
                   [(jax.ShapeDtypeStruct((m, n), d), o_spec) for d in out_dtypes],
                   [(e, o_spec) for e in extras], epilogue, norm_g=norm_g)


def _mm_stacked_nt(name, a, w3, mode, out_dtypes=(F32,), epilogue=None, extras=(), tm=MM_TILE, tn=MM_TILE, tk=MM_TILE,
                   consts=(), sums=()):
    m, nd = a.shape
    _, r, c = w3.shape
    kout = N_SHARDS * r if mode == "row" else r
    if mode == "row":
        tm, tn, tk = _tile(m, tm), _tile(r, tn), _tile(c, tk)
        per = r // tn
        b_spec = pl.BlockSpec((None, tn, tk), lambda i, j, k: (j // per, j % per, k))
    else:
        tm, tn, tk = _tile(m, tm // 2), _tile(r, tn), nd
        b_spec = [pl.BlockSpec((None, tn, c), functools.partial(lambda i, j, k, p: (p, j, 0), p=p)) for p in range(N_SHARDS)]
    o_spec = pl.BlockSpec((tm, tn), lambda i, j, k: (i, j))
    return _matmul(name, a, w3, pl.BlockSpec((tm, tk), lambda i, j, k: (i, k)), b_spec, NT,
                   (m // tm, kout // tn, nd // tk), (tm, tn),
                   [(jax.ShapeDtypeStruct((m, kout), d), o_spec) for d in out_dtypes],
                   _epilogue_operands(extras, consts, o_spec), epilogue, sums)


def _mm_tn_stacked(name, a, b, shape3, mode, tm=MM_TILE, tn=MM_TILE, tk=MM_TILE):
    s, m = a.shape
    n = b.shape[1]
    _, r, c = shape3
    tk, tn = s, tn // 2
    if mode == "row":
        tm, tn = _tile(r, tm), _tile(n, tn)
        per = r // tm
        o_spec = pl.BlockSpec((None, tm, tn), lambda i, j, k: (i // per, i % per, j))
    else:
        tm, tn = _tile(m, tm), _tile(c, tn)
        per = c // tn
        o_spec = pl.BlockSpec((None, tm, tn), lambda i, j, k: (j // per, i, j % per))
    outs = [(jax.ShapeDtypeStruct(shape3, F32), o_spec), (jax.ShapeDtypeStruct(shape3, BF16), o_spec)]
    return _matmul(name, a, b, pl.BlockSpec((tk, tm), lambda i, j, k: (k, i)),
                   pl.BlockSpec((tk, tn), lambda i, j, k: (k, j)), TN, (m // tm, n // tn, s // tk), (tm, tn),
                   outs, epilogue=lambda acc: (acc, acc))


def _rowwise(name, fn, rows, consts, out_rows, out_accs=(), tr=256):
    nr, nc, no = len(rows), len(consts), len(out_rows)
    n_rows = rows[0].shape[0]
    tr = _tile(n_rows, tr)

    def body(*refs):
        vals = fn(*[r[...] for r in refs[:nr + nc]])
        o_refs, a_refs = refs[nr + nc:nr + nc + no], refs[nr + nc + no:]
        for ref, v in zip(o_refs, vals[:no]):
            ref[...] = v.astype(ref.dtype)
        first = pl.program_id(0) == 0

        @pl.when(first)
        def _():
            for ref, v in zip(a_refs, vals[no:]):
                ref[...] = v

        @pl.when(jnp.logical_not(first))
        def _():
            for ref, v in zip(a_refs, vals[no:]):
                ref[...] += v

    def whole(shape):
        return pl.BlockSpec(shape, lambda i: (0,) * len(shape))

    return pl.pallas_call(
        body, name=name, grid=(n_rows // tr,),
        in_specs=[pl.BlockSpec((tr, a.shape[1]), lambda i: (i, 0)) for a in rows] + [whole(c.shape) for c in consts],
        out_specs=[pl.BlockSpec((tr, f), lambda i: (i, 0)) for f, _ in out_rows] + [whole(s) for s in out_accs],
        out_shape=[jax.ShapeDtypeStruct((n_rows, f), d) for f, d in out_rows]
        + [jax.ShapeDtypeStruct(s, F32) for s in out_accs],
        compiler_params=_params(("arbitrary",)),
    )(*rows, *consts)


def _rms_fwd(x, g):
    return x * lax.rsqrt(jnp.mean(x * x, axis=-1, keepdims=True) + NORM_EPS) * g


def _rms_bwd(dy, x, g):
    rstd = lax.rsqrt(jnp.mean(x * x, axis=-1, keepdims=True) + NORM_EPS)
    n = x * rstd
    dn = dy * g
    dx = rstd * (dn - n * jnp.mean(dn * n, axis=-1, keepdims=True))
    return dx, jnp.sum(dy * n, axis=0, keepdims=True)


def _rope(x, cs, s1, s2):
    return x * cs + pltpu.roll(x, 32, 1) * s1 + pltpu.roll(x, 96, 1) * s2


def _rope_t(dy, cs, s1, s2):
    return dy * cs + pltpu.roll(dy * s1, 96, 1) + pltpu.roll(dy * s2, 32, 1)


def _gelu(z):
    return 0.5 * z * (1.0 + lax.erf(z * (1.0 / math.sqrt(2.0))))


def _gelu_and_grad(z):
    cdf = 0.5 * (1.0 + lax.erf(z * (1.0 / math.sqrt(2.0))))
    return z * cdf, cdf + z * jnp.exp(-0.5 * z * z) * (1.0 / math.sqrt(2.0 * math.pi))


def _att_scores(q, kv, kr, scale, masked, transposed):
    k = jnp.concatenate([kv[:, :NOPE], kr], axis=1)
    if transposed:
        s = lax.dot_general(k, q, NT, preferred_element_type=F32) * scale
    else:
        s = lax.dot_general(q, k, NT, preferred_element_type=F32) * scale
    if masked:
        r = lax.broadcasted_iota(jnp.int32, s.shape, 0)
        c = lax.broadcasted_iota(jnp.int32, s.shape, 1)
        s = jnp.where((r <= c) if transposed else (c <= r), s, -jnp.inf)
    return s, k


def _in_pairs(lo, hi, pair, single):
    n = hi - lo

    def body(p, carry):
        pair(lo + 2 * p, lo + 2 * p + 1)
        return carry

    lax.fori_loop(0, n // 2, body, 0)

    @pl.when(n % 2 == 1)
    def _():
        single(hi - 1)


def _causal_tiles(i, pair, single):
    @pl.when(i == 0)
    def _():
        single(i, True)

    @pl.when(i > 0)
    def _():
        _in_pairs(0, i - 1, lambda a, b: pair(a, b, False), lambda a: single(a, False))
        pair(i - 1, i, True)


def _flash_fwd(q, kvb, krb, tables):
    s_len = q.shape[0]
    t = ATT_TILE

    def body(q_ref, cs_ref, s1_ref, s2_ref, kv_ref, kr_ref, o_ref, lse_ref, qb_ref, m_s, l_s, acc_s):
        qi = pl.program_id(1)
        m_s[...] = jnp.full_like(m_s, -jnp.inf)
        l_s[...] = jnp.zeros_like(l_s)
        acc_s[...] = jnp.zeros_like(acc_s)
        qv = q_ref[...]
        q = jnp.concatenate([qv[:, :NOPE], _rope(qv[:, NOPE:], cs_ref[...], s1_ref[...], s2_ref[...])], axis=1).astype(BF16)
        qb_ref[...] = q

        def scores(ki, masked):
            rows = pl.ds(pl.multiple_of(ki * t, t), t)
            kv = kv_ref[rows, :]
            return _att_scores(q, kv, kr_ref[rows, :], LOG2_SCALE, masked, False)[0], kv

        def update(s, kv):
            m_prev = m_s[...]
            m_new = jnp.maximum(m_prev, jnp.max(s, axis=1, keepdims=True))
            alpha = jnp.exp2(m_prev - m_new)
            p = jnp.exp2(s - jnp.tile(m_new, (1, t // LANE)))
            l_s[...] = alpha * l_s[...] + jnp.sum(p, axis=1, keepdims=True)
            acc_s[...] = alpha * acc_s[...] + jnp.dot(p.astype(BF16), kv[:, NOPE:], preferred_element_type=F32)
            m_s[...] = m_new

        def pair(k0, k1, masked):
            first, second = scores(k0, False), scores(k1, masked)
            update(*first)
            update(*second)

        _causal_tiles(qi, pair, lambda ki, masked: update(*scores(ki, masked)))
        o_ref[...] = (acc_s[...] / l_s[...]).astype(o_ref.dtype)
        lse_ref[...] = (m_s[...] + jnp.log2(l_s[...]))[:, :1]

    table = pl.BlockSpec((t, LANE), lambda h, qi: (qi, 0))
    return pl.pallas_call(
        body, name="flash_fwd", grid=(HEADS, s_len // t),
        in_specs=[pl.BlockSpec((t, HEAD_PAD), lambda h, qi: (qi, h)), table, table, table,
                  pl.BlockSpec((s_len, HEAD_PAD), lambda h, qi: (0, h)),
                  pl.BlockSpec((s_len, LANE), lambda h, qi: (0, 0))],
        out_specs=[pl.BlockSpec((t, VHEAD), lambda h, qi: (qi, h)),
                   pl.BlockSpec((None, t, 1), lambda h, qi: (h, qi, 0)),
                   pl.BlockSpec((t, HEAD_PAD), lambda h, qi: (qi, h))],
        out_shape=[jax.ShapeDtypeStruct((s_len, HEADS * VHEAD), BF16),
                   jax.ShapeDtypeStruct((HEADS, s_len, 1), F32),
                   jax.ShapeDtypeStruct((s_len, HEADS * HEAD_PAD), BF16)],
        scratch_shapes=[pltpu.VMEM((t, LANE), F32), pltpu.VMEM((t, LANE), F32), pltpu.VMEM((t, VHEAD), F32)],
        compiler_params=_params(("parallel", "arbitrary")),
    )(q, *tables, kvb, krb)


def _flash_bwd_dq(qb, kvb, krb, dob, lse, delta, tables):
    s_len = qb.shape[0]
    t = ATT_TILE
    nq = s_len // t
    scale = QK_HEAD ** -0.5

    def body(q_ref, kv_ref, kr_ref, do_ref, lse_ref, dl_ref, cs_ref, s1_ref, s2_ref, dq_ref, acc_s):
        qi = pl.program_id(1)
        acc_s[...] = jnp.zeros_like(acc_s)
        q, do = q_ref[...], do_ref[...]
        lse = jnp.broadcast_to(lse_ref[...], (t, LANE))
        dl = dl_ref[...]

        def products(ki, masked):
            rows = pl.ds(pl.multiple_of(ki * t, t), t)
            kv = kv_ref[rows, :]
            s, k = _att_scores(q, kv, kr_ref[rows, :], LOG2_SCALE, masked, False)
            return s, lax.dot_general(do, kv[:, NOPE:], NT, preferred_element_type=F32), k

        def update(s, dp, k):
            p = jnp.exp2(s - jnp.tile(lse, (1, t // LANE)))
            ds = (p * (dp - jnp.tile(dl, (1, t // LANE))) * scale).astype(BF16)
            acc_s[...] += jnp.dot(ds, k, preferred_element_type=F32)

        def pair(k0, k1, masked):
            first, second = products(k0, False), products(k1, masked)
            update(*first)
            update(*second)

        _causal_tiles(qi, pair, lambda ki, masked: update(*products(ki, masked)))
        dq = acc_s[...]
        dq = jnp.concatenate([dq[:, :NOPE], _rope_t(dq[:, NOPE:], cs_ref[...], s1_ref[...], s2_ref[...])], axis=1)
        dq_ref[...] = dq.astype(dq_ref.dtype)

    col = pl.BlockSpec((None, t, 1), lambda h, qi: (h, qi, 0))
    table = pl.BlockSpec((t, LANE), lambda h, qi: (qi, 0))
    return pl.pallas_call(
        body, name="flash_bwd_dq", grid=(HEADS, nq),
        in_specs=[pl.BlockSpec((t, HEAD_PAD), lambda h, qi: (qi, h)),
                  pl.BlockSpec((s_len, HEAD_PAD), lambda h, qi: (0, h)),
                  pl.BlockSpec((s_len, LANE), lambda h, qi: (0, 0)),
                  pl.BlockSpec((t, VHEAD), lambda h, qi: (qi, h)), col, pl.BlockSpec((t, VHEAD), lambda h, qi: (qi, h)),
                  table, table, table],
        out_specs=pl.BlockSpec((t, HEAD_PAD), lambda h, qi: (qi, h)),
        out_shape=jax.ShapeDtypeStruct((s_len, HEADS * HEAD_PAD), BF16),
        scratch_shapes=[pltpu.VMEM((t, HEAD_PAD), F32)],
        compiler_params=_params(("parallel", "arbitrary")),
    )(qb, kvb, krb, dob, lse, delta, *tables)


def _flash_bwd_dkv(qb, kvb, krb, dob, lse_row, delta_row):
    s_len = qb.shape[0]
    t = ATT_TILE
    nq = s_len // t
    scale = QK_HEAD ** -0.5

    def body(q_ref, kv_ref, kr_ref, do_ref, lse_ref, dl_ref, dkv_ref, dkr_ref, dk_s, dv_s):
        ki = pl.program_id(1)
        dk_s[...] = jnp.zeros_like(dk_s)
        dv_s[...] = jnp.zeros_like(dv_s)
        kv, kr = kv_ref[...], kr_ref[...]

        def products(qi, masked):
            rows = pl.ds(pl.multiple_of(qi * t, t), t)
            q, do = q_ref[rows, :], do_ref[rows, :]
            st, _ = _att_scores(q, kv, kr, LOG2_SCALE, masked, True)
            return st, lax.dot_general(kv[:, NOPE:], do, NT, preferred_element_type=F32), q, do, rows

        def update(st, dpt, q, do, rows):
            pt = jnp.exp2(st - lse_ref[:, rows])
            dv_s[...] += jnp.dot(pt.astype(BF16), do, preferred_element_type=F32)
            dst = (pt * (dpt - dl_ref[:, rows]) * scale).astype(BF16)
            dk_s[...] += jnp.dot(dst, q, preferred_element_type=F32)

        def pair(q0, q1, masked):
            first, second = products(q0, masked), products(q1, False)
            update(*first)
            update(*second)

        @pl.when(ki == nq - 1)
        def _():
            update(*products(ki, True))

        @pl.when(ki < nq - 1)
        def _():
            pair(ki, ki + 1, True)
            _in_pairs(ki + 2, nq, lambda a, b: pair(a, b, False), lambda qi: update(*products(qi, False)))

        dk = dk_s[...]
        dkv_ref[...] = jnp.concatenate([dk[:, :NOPE], dv_s[...]], axis=1).astype(dkv_ref.dtype)
        dkr_ref[...] = dk[:, NOPE:]

    row = pl.BlockSpec((None, 1, s_len), lambda h, ki: (h, 0, 0))
    return pl.pallas_call(
        body, name="flash_bwd_dkv", grid=(HEADS, nq),
        in_specs=[pl.BlockSpec((s_len, HEAD_PAD), lambda h, ki: (0, h)),
                  pl.BlockSpec((t, HEAD_PAD), lambda h, ki: (ki, h)),
                  pl.BlockSpec((t, LANE), lambda h, ki: (ki, 0)),
                  pl.BlockSpec((s_len, VHEAD), lambda h, ki: (0, h)), row, row],
        out_specs=[pl.BlockSpec((t, HEAD_PAD), lambda h, ki: (ki, h)),
                   pl.BlockSpec((t, LANE), lambda h, ki: (ki, h))],
        out_shape=[jax.ShapeDtypeStruct((s_len, HEADS * HEAD_PAD), BF16),
                   jax.ShapeDtypeStruct((s_len, HEADS * LANE), F32)],
        scratch_shapes=[pltpu.VMEM((t, HEAD_PAD), F32), pltpu.VMEM((t, VHEAD), F32)],
        compiler_params=_params(("parallel", "parallel")),
    )(qb, kvb, krb, dob, lse_row, delta_row)


def _tril(w):
    r = lax.broadcasted_iota(jnp.int32, w.shape, 0)
    c = lax.broadcasted_iota(jnp.int32, w.shape, 1)
    return jnp.where(c <= r, w, 0.0)


def _sgu_row_stats(z_ref, width, gd, v_s, act, extra_s=None):
    total = None
    for g in range(SGU_GROUPS):
        cols = slice(g * gd, (g + 1) * gd)
        v = act(z_ref[:, width + g * gd:width + (g + 1) * gd])
        if extra_s is not None:
            v, extra_s[:, cols] = v
        v_s[:, cols] = v
        part = jnp.sum(v, axis=1, keepdims=True)
        total = part if total is None else total + part
    mean = total * (1.0 / width)
    sq = None
    for g in range(SGU_GROUPS):
        d = v_s[:, g * gd:(g + 1) * gd] - mean
        part = jnp.sum(d * d, axis=1, keepdims=True)
        sq = part if sq is None else sq + part
    return mean, lax.rsqrt(sq * (1.0 / width) + LN_EPS)


def _sgu_fwd(zpre, ln_g, ln_b, w_sp, bias_full):
    s_len, two_w = zpre.shape
    width = two_w // 2
    gd = width // SGU_GROUPS
    t = SGU_CHUNK

    def body(z_ref, g_ref, b_ref, w_ref, bias_ref, uv_ref, v_s):
        mean, rstd = _sgu_row_stats(z_ref, width, gd, v_s, _gelu)
        for g in range(SGU_GROUPS):
            cols = slice(g * gd, (g + 1) * gd)
            vln = ((v_s[:, cols] - mean) * rstd * g_ref[:, cols] + b_ref[:, cols]).astype(BF16)
            mixed = jnp.dot(_tril(w_ref[g]).astype(BF16), vln, preferred_element_type=F32) + bias_ref[:, cols]
            uv_ref[:, cols] = (_gelu(z_ref[:, cols]) * mixed).astype(uv_ref.dtype)

    return pl.pallas_call(
        body, name="sgu_fwd", grid=(s_len // t,),
        in_specs=[pl.BlockSpec((t, two_w), lambda i: (i, 0)), pl.BlockSpec((1, width), lambda i: (0, 0)),
                  pl.BlockSpec((1, width), lambda i: (0, 0)), pl.BlockSpec(w_sp.shape, lambda i: (0, 0, 0)),
                  pl.BlockSpec((t, width), lambda i: (0, 0))],
        out_specs=pl.BlockSpec((t, width), lambda i: (i, 0)),
        out_shape=jax.ShapeDtypeStruct((s_len, width), BF16),
        scratch_shapes=[pltpu.VMEM((t, width), F32)],
        compiler_params=_params(("parallel",)),
    )(zpre, ln_g, ln_b, w_sp, bias_full)


def _sgu_bwd(zpre, duv, ln_g, ln_b, w_sp, bias_full):
    s_len, two_w = zpre.shape
    width = two_w // 2
    gd = width // SGU_GROUPS
    t = SGU_CHUNK

    def body(z_ref, duv_ref, g_ref, b_ref, w_ref, bias_ref, dz_ref, dg_ref, db_ref, dw_ref, dbias_ref, v_s, vgrad_s, dvhat_s):
        @pl.when(pl.program_id(0) == 0)
        def _():
            for ref in (dg_ref, db_ref, dw_ref, dbias_ref):
                ref[...] = jnp.zeros_like(ref)

        def accumulate(ref, val):
            ref[...] += val

        mean, rstd = _sgu_row_stats(z_ref, width, gd, v_s, _gelu_and_grad, vgrad_s)
        sum_dvhat = sum_dvhat_vhat = None
        for g in range(SGU_GROUPS):
            cols = slice(g * gd, (g + 1) * gd)
            vhat = (v_s[:, cols] - mean) * rstd
            vln = (vhat * g_ref[:, cols] + b_ref[:, cols]).astype(BF16)
            wc = _tril(w_ref[g]).astype(BF16)
            mixed = jnp.dot(wc, vln, preferred_element_type=F32) + bias_ref[:, cols]
            u, u_grad = _gelu_and_grad(z_ref[:, cols])
            duv = duv_ref[:, cols]
            dz_ref[:, cols] = (duv * mixed * u_grad).astype(dz_ref.dtype)
            dmixed = duv * u
            dmb = dmixed.astype(BF16)
            dvln = lax.dot_general(wc, dmb, TN, preferred_element_type=F32)
            accumulate(dw_ref.at[g], _tril(lax.dot_general(dmb, vln, NT, preferred_element_type=F32)))
            accumulate(dbias_ref.at[:, cols], dmixed)
            accumulate(dg_ref.at[:, cols], jnp.sum(dvln * vhat, axis=0, keepdims=True))
            accumulate(db_ref.at[:, cols], jnp.sum(dvln, axis=0, keepdims=True))
            dvhat = dvln * g_ref[:, cols]
            dvhat_s[:, cols] = dvhat
            parts = jnp.sum(dvhat, axis=1, keepdims=True), jnp.sum(dvhat * vhat, axis=1, keepdims=True)
            sum_dvhat = parts[0] if sum_dvhat is None else sum_dvhat + parts[0]
            sum_dvhat_vhat = parts[1] if sum_dvhat_vhat is None else sum_dvhat_vhat + parts[1]
        mean_dvhat, mean_dvhat_vhat = sum_dvhat * (1.0 / width), sum_dvhat_vhat * (1.0 / width)
        for g in range(SGU_GROUPS):
            cols = slice(g * gd, (g + 1) * gd)
            vhat = (v_s[:, cols] - mean) * rstd
            dv0 = rstd * (dvhat_s[:, cols] - mean_dvhat - vhat * mean_dvhat_vhat)
            dz_ref[:, width + g * gd:width + (g + 1) * gd] = (dv0 * vgrad_s[:, cols]).astype(dz_ref.dtype)

    vec = pl.BlockSpec((1, width), lambda i: (0, 0))
    return pl.pallas_call(
        body, name="sgu_bwd", grid=(s_len // t,),
        in_specs=[pl.BlockSpec((t, two_w), lambda i: (i, 0)), pl.BlockSpec((t, width), lambda i: (i, 0)), vec, vec,
                  pl.BlockSpec(w_sp.shape, lambda i: (0, 0, 0)), pl.BlockSpec((t, width), lambda i: (0, 0))],
        out_specs=[pl.BlockSpec((t, two_w), lambda i: (i, 0)), vec, vec,
                   pl.BlockSpec(w_sp.shape, lambda i: (0, 0, 0)), pl.BlockSpec((t, width), lambda i: (0, 0))],
        out_shape=[jax.ShapeDtypeStruct((s_len, two_w), BF16), jax.ShapeDtypeStruct((1, width), F32),
                   jax.ShapeDtypeStruct((1, width), F32), jax.ShapeDtypeStruct(w_sp.shape, F32),
                   jax.ShapeDtypeStruct((t, width), F32)],
        scratch_shapes=[pltpu.VMEM((t, width), F32)] * 3,
        compiler_params=_params(("arbitrary",)),
    )(zpre, duv, ln_g, ln_b, w_sp, bias_full)


def _rope_tables(positions):
    inv_freq = ROPE_THETA ** (-jnp.arange(0, ROPE, 2, dtype=F32) / ROPE)
    ang = positions.astype(F32)[:, None] * inv_freq
    cos, sin = jnp.cos(ang), jnp.sin(ang)
    z32, z64 = jnp.zeros_like(cos), jnp.zeros((cos.shape[0], LANE - ROPE), F32)
    return (jnp.concatenate([cos, cos, z64], axis=1), jnp.concatenate([z32, sin, z64], axis=1),
            jnp.concatenate([-sin, z32, z64], axis=1))


def _ffn_fwd(x, g, w_up3, w_down3):
    def sq_relu(acc):
        r = jnp.maximum(acc, 0.0)
        return r * r, 2.0 * r

    r, r_grad, h2 = _mm_stacked("ffn_up", x, w_up3, "col", (BF16, BF16), sq_relu, norm_g=g)
    x_out = _mm_stacked("ffn_down", r, w_down3, "row", (F32,), lambda acc, res: (acc + res,), [x])[0]
    return x_out, (x, h2, r, r_grad)


def _ffn_bwd(dx, dxb, saved, g, w_up3, w_down3):
    x, h2, r, r_grad = saved
    da = _mm_stacked_nt("ffn_down_dx", dxb, w_down3, "row", (BF16,),
                        lambda acc, rg: (acc * rg.astype(F32),), [r_grad])[0]
    g_down = _mm_tn_stacked("ffn_down_dw", r, dxb, w_down3.shape, "row")
    dx, dxb, dg = _mm_stacked_nt("ffn_up_dx", da, w_up3, "col", **_norm_bwd(x, g, dx))
    g_up = _mm_tn_stacked("ffn_up_dw", h2, da, w_up3.shape, "col")
    return dx, dxb, dg, g_up, g_down


def _norm_bwd(x, g, dres):
    def epilogue(dh, xv, rv, gv):
        dxv, dg = _rms_bwd(dh, xv, gv)
        return dxv + rv, dxv + rv, dg

    return dict(out_dtypes=(F32, BF16), epilogue=epilogue, extras=[x, dres], consts=[g], sums=[g.shape], tm=MM_TILE // 2)


def _dot(a, b, dims=NN):
    return lax.dot_general(a.astype(BF16), b.astype(BF16), dims, preferred_element_type=F32)


def _mla_fwd(x, g, wdkv, q_norm, kv_norm, wq, wkv, wo, tables):
    d = x.shape[1]

    def project(xv, cs, s1, s2, gv, wdkv_v, qg, kg, wq_v, wkv_v):
        h = _rms_fwd(xv, gv).astype(BF16)
        lv = _dot(h, wdkv_v)
        cqn = _rms_fwd(lv[:, :Q_RANK], qg).astype(BF16)
        ckvn = _rms_fwd(lv[:, Q_RANK:Q_RANK + KV_RANK], kg).astype(BF16)
        return (h, lv, cqn, ckvn, _rope(lv[:, Q_RANK + KV_RANK:], cs, s1, s2), _dot(cqn, wq_v), _dot(ckvn, wkv_v))

    h, lat, cqn, ckvn, krb, q, kvb = _rowwise(
        "mla_project", project, [x, *tables], [g, wdkv, q_norm, kv_norm, wq, wkv],
        [(d, BF16), (LAT_PAD, F32), (Q_RANK, BF16), (KV_RANK, BF16), (LANE, BF16), (wq.shape[1], F32), (wkv.shape[1], BF16)],
        tr=512)
    ob, lse, qb = _flash_fwd(q, kvb, krb, tables)
    x_mid = _mm("mla_o", ob, wo, (F32,), lambda acc, res: (acc + res,), [x])[0]
    return x_mid, (x, h, lat, cqn, ckvn, krb, qb, kvb, ob, lse)


def _mla_bwd(dx, dxb, saved, g, wdkv, q_norm, kv_norm, wq, wkv, wo, tables):
    x, h, lat, cqn, ckvn, krb, qb, kvb, ob, lse = saved
    s_len = x.shape[0]
    def with_delta(do, ov):
        prod = do * ov.astype(F32)
        sums = [jnp.broadcast_to(jnp.sum(prod[:, hd * VHEAD:(hd + 1) * VHEAD], axis=1, keepdims=True), (do.shape[0], VHEAD))
                for hd in range(HEADS)]
        lane = lax.broadcasted_iota(jnp.int32, sums[0].shape, 1)
        by_lane = sum(jnp.where(lane == hd, s, 0.0) for hd, s in enumerate(sums))
        return do, jnp.concatenate(sums, axis=1), by_lane

    dob, delta, delta_lanes = _mm("mla_o_dx", dxb, wo, (BF16, F32), with_delta, [ob], nt=True, tm=MM_TILE // 2,
                                  narrow=[(LANE, F32)])
    g_wo = _mm_tn("mla_o_dw", ob, dxb)[0]
    delta_row = delta_lanes[:, :HEADS].T.reshape(HEADS, 1, s_len)
    dqb = _flash_bwd_dq(qb, kvb, krb, dob, lse, delta, tables)
    dkvb, dkr = _flash_bwd_dkv(qb, kvb, krb, dob, lse.reshape(HEADS, 1, s_len), delta_row)
    def project_bwd(dq_v, dkv_v, dkr_v, lv, cqn_v, ckvn_v, h_v, xv, rv, cs, s1, s2, gv, qg, kg, wq_v, wkv_v, wdkv_v):
        dcq, dqg = _rms_bwd(_dot(dq_v, wq_v, NT), lv[:, :Q_RANK], qg)
        dckv, dkg = _rms_bwd(_dot(dkv_v, wkv_v, NT), lv[:, Q_RANK:Q_RANK + KV_RANK], kg)
        dkr_sum = dkr_v[:, :LANE]
        for hd in range(1, HEADS):
            dkr_sum = dkr_sum + dkr_v[:, hd * LANE:(hd + 1) * LANE]
        dlat = jnp.concatenate([dcq, dckv, _rope_t(dkr_sum, cs, s1, s2)], axis=1).astype(BF16)
        dxv, dg = _rms_bwd(_dot(dlat, wdkv_v, NT), xv, gv)
        return (dxv + rv, dxv + rv, _dot(cqn_v, dq_v, TN), _dot(ckvn_v, dkv_v, TN), _dot(h_v, dlat, TN), dqg, dkg, dg)

    dx, dxb, g_wq, g_wkv, g_wdkv, g_qn, g_kvn, dg = _rowwise(
        "mla_project_bwd", project_bwd, [dqb, dkvb, dkr, lat, cqn, ckvn, h, x, dx, *tables],
        [g, q_norm, kv_norm, wq, wkv, wdkv], [(x.shape[1], F32), (x.shape[1], BF16)],
        [wq.shape, wkv.shape, wdkv.shape, q_norm.shape, kv_norm.shape, g.shape], tr=256)
    return dx, dxb, dg, g_wdkv, g_qn, g_kvn, g_wq, g_wkv, g_wo


def _sgu_layer_fwd(x, g, w_in3, ln_g, ln_b, w_sp, bias_full, w_out3):
    zpre, h = _mm_stacked("sgu_in", x, w_in3, "col", norm_g=g)
    uv = _sgu_fwd(zpre, ln_g, ln_b, w_sp, bias_full)
    x_mid = _mm_stacked("sgu_out", uv, w_out3, "row", (F32,), lambda acc, res: (acc + res,), [x])[0]
    return x_mid, (x, h, zpre, uv)


def _sgu_layer_bwd(dx, dxb, saved, g, w_in3, ln_g, ln_b, w_sp, bias_full, w_out3):
    x, h, zpre, uv = saved
    duv = _mm_stacked_nt("sgu_out_dx", dxb, w_out3, "row")[0]
    g_out = _mm_tn_stacked("sgu_out_dw", uv, dxb, w_out3.shape, "row")
    dz, g_lng, g_lnb, g_wsp, g_bias = _sgu_bwd(zpre, duv, ln_g, ln_b, w_sp, bias_full)
    g_in = _mm_tn_stacked("sgu_in_dw", h, dz, w_in3.shape, "col")
    dx, dxb, dg = _mm_stacked_nt("sgu_in_dx", dz, w_in3, "col", **_norm_bwd(x, g, dx))
    return dx, dxb, dg, g_in, g_out, g_lng, g_lnb, g_wsp, g_bias


def _loss_head(x, target, g):
    d = x.shape[1]

    def fn(xv, tv, gv):
        err = _rms_fwd(xv, gv) - tv
        dxv, dg = _rms_bwd(err * (1.0 / d), xv, gv)
        return dxv, dxv, dg, jnp.sum(err * err, axis=0, keepdims=True)

    return _rowwise("loss_head", fn, [x, target], [g], [(d, F32), (d, BF16)], [g.shape, g.shape])


def _mixer_weights(i, stacks):
    by_rows = lambda a: a.reshape(N_SHARDS * a.shape[1], a.shape[2])
    by_cols = lambda a: a.transpose(1, 0, 2).reshape(a.shape[1], N_SHARDS * a.shape[2])
    if i % 2:
        w_in3, w_out3, ln_g, ln_b = stacks
        return w_in3, ln_g.reshape(1, -1), ln_b.reshape(1, -1), w_out3
    wdkv = by_rows(stacks[0])
    wdkv = jnp.pad(wdkv, ((0, 0), (0, LAT_PAD - wdkv.shape[1])))
    wq = jnp.pad(by_cols(stacks[1]).reshape(Q_RANK, HEADS, QK_HEAD), ((0, 0), (0, 0), (0, HEAD_PAD - QK_HEAD)))
    return wdkv, wq.reshape(Q_RANK, HEADS * HEAD_PAD), by_cols(stacks[2]), by_rows(stacks[3])


def _local_step(x, positions, target, norm_mix, norm_ffn, final_norm, q_norm, kv_norm, w_sp, b_sp, mixers, ffn, reducer):
    tables = _rope_tables(positions)
    gd = mixers[1][2].size // SGU_GROUPS
    bias_full = [jnp.repeat(b_sp[j].T, gd, axis=1) for j in range(DEPTH // 2)]
    saved, mla, sgu = [], [None] * (DEPTH // 2), [None] * (DEPTH // 2)
    for i in range(DEPTH):
        j = i // 2
        x, *stacks = lax.optimization_barrier((x, *mixers[i]))
        if i % 2 == 0:
            wdkv, wq, wkv, wo = mla[j] = _mixer_weights(i, stacks)
            x, s_mix = _mla_fwd(x, norm_mix[i:i + 1], wdkv, q_norm[j:j + 1], kv_norm[j:j + 1], wq, wkv, wo, tables)
        else:
            w_in3, ln_g, ln_b, w_out3 = sgu[j] = _mixer_weights(i, stacks)
            x, s_mix = _sgu_layer_fwd(x, norm_mix[i:i + 1], w_in3, ln_g, ln_b, w_sp[j], bias_full[j], w_out3)
        x, s_ffn = _ffn_fwd(x, norm_ffn[i:i + 1], *ffn[i])
        saved.append((s_mix, s_ffn))
    dx, dxb, g_final, sq_cols = _loss_head(x, target, final_norm[None, :])
    loss = 0.5 * jnp.sum(sq_cols) / x.shape[1]

    def pair(g):
        return g, g.astype(BF16)

    g_mix, g_ffn = [None] * DEPTH, [None] * DEPTH
    mla_g, sgu_g = [None] * (DEPTH // 2), [None] * (DEPTH // 2)
    for i in reversed(range(DEPTH)):
        j = i // 2
        s_mix, s_ffn = saved[i]
        dx, dxb, g_ffn[i], g_up, g_down = _ffn_bwd(dx, dxb, s_ffn, norm_ffn[i:i + 1], *ffn[i])
        dxb = reducer.add(f"ffn{i}", i, {"ffn_w_up": g_up, "ffn_w_down": g_down}, dxb)
        dxb = reducer.phase_end(dxb)
        if i % 2 == 0:
            wdkv, wq, wkv, wo = mla[j]
            dx, dxb, g_mix[i], g_wdkv, g_qn, g_kvn, g_wq, g_wkv, g_wo = _mla_bwd(
                dx, dxb, s_mix, norm_mix[i:i + 1], wdkv, q_norm[j:j + 1], kv_norm[j:j + 1], wq, wkv, wo, tables)
            mla_g[j] = (g_qn, g_kvn)
            g_wq = g_wq.reshape(Q_RANK, HEADS, HEAD_PAD)[..., :QK_HEAD].reshape(Q_RANK, N_SHARDS, -1)
            dxb = reducer.add(f"mla{j}", j, {
                "mla_w_dkv": pair(g_wdkv[:, :Q_RANK + KV_RANK + ROPE].reshape(N_SHARDS, -1, Q_RANK + KV_RANK + ROPE)),
                "mla_w_uq": pair(g_wq.transpose(1, 0, 2)),
                "mla_w_ukv": pair(g_wkv.reshape(KV_RANK, N_SHARDS, -1).transpose(1, 0, 2)),
                "mla_w_o": pair(g_wo.reshape(N_SHARDS, -1, g_wo.shape[1]))}, dxb)
        else:
            w_in3, ln_g, ln_b, w_out3 = sgu[j]
            dx, dxb, g_mix[i], g_in, g_out, g_lng, g_lnb, g_wsp, g_bias = _sgu_layer_bwd(
                dx, dxb, s_mix, norm_mix[i:i + 1], w_in3, ln_g, ln_b, w_sp[j], bias_full[j], w_out3)
            sgu_g[j] = (g_wsp, g_bias.reshape(SGU_CHUNK, SGU_GROUPS, gd).sum(axis=-1).T)
            dxb = reducer.add(f"sgu{j}", j, {"sgu_w_in": g_in, "sgu_w_out": g_out,
                                             "sgu_ln_g": pair(g_lng.reshape(N_SHARDS, -1, LANE)),
                                             "sgu_ln_b": pair(g_lnb.reshape(N_SHARDS, -1, LANE))}, dxb)
        dxb = reducer.phase_end(dxb)
    small = dict(
        norm_mix=jnp.concatenate(g_mix, axis=0), norm_ffn=jnp.concatenate(g_ffn, axis=0), final_norm=g_final[0],
        q_norm=jnp.concatenate([m[0] for m in mla_g], axis=0), kv_norm=jnp.concatenate([m[1] for m in mla_g], axis=0),
        w_sp=jnp.stack([s[0] for s in sgu_g]), b_sp=jnp.stack([s[1] for s in sgu_g]))
    return loss, dx, small


HBM_SPEC = pl.BlockSpec(memory_space=pltpu.HBM)


def _place():
    x, y, c = lax.axis_index("x"), lax.axis_index("y"), lax.axis_index("c")
    return x, y, c, [(1 - x, y), (x, 1 - y), (1 - x, 1 - y)]


def _remote(src, dst, send_sems, recv_sems, k, to):
    return pltpu.make_async_remote_copy(src_ref=src, dst_ref=dst, send_sem=send_sems.at[k], recv_sem=recv_sems.at[k],
                                        device_id=to, device_id_type=MESH)


def _gather_layer(tag, shards):
    n = len(shards)
    split = [s.shape[0] >= 16 for s in shards]

    def body(*refs):
        ins, outs = refs[:n], refs[n:2 * n]
        send_sems, recv_sems, local_sems = refs[2 * n:]
        x, y, c, chips = _place()
        mine = 2 * x + y
        barrier = pltpu.get_barrier_semaphore()
        peers = [(x, y, 1 - c)] + [(*chip, c) for chip in chips]
        for peer in peers:
            pl.semaphore_signal(barrier, inc=1, device_id=peer, device_id_type=MESH)
        pl.semaphore_wait(barrier, len(peers))

        def rows(t, half):
            hr = shards[t].shape[0] // 2
            return pl.ds(half * hr, hr) if split[t] else pl.ds(0, shards[t].shape[0])

        local, sent = [], []
        for t in range(n):
            local.append(pltpu.make_async_copy(ins[t], outs[t].at[mine], local_sems.at[t]))
            local[-1].start()
            for j, chip in enumerate(chips):
                cp = _remote(ins[t].at[rows(t, c)], outs[t].at[mine, rows(t, c)], send_sems, recv_sems, 3 * t + j, (*chip, c))
                cp.start()
                sent.append(cp)
        for j, chip in enumerate(chips):
            theirs = 2 * chip[0] + chip[1]
            for t in range(n):
                piece = outs[t].at[theirs, rows(t, c)]
                _remote(piece, piece, send_sems, recv_sems, 3 * t + j, (x, y, c)).wait_recv()
                if split[t]:
                    cp = _remote(piece, piece, send_sems, recv_sems, 3 * n + 3 * t + j, (x, y, 1 - c))
                    cp.start()
                    sent.append(cp)
        for j, chip in enumerate(chips):
            theirs = 2 * chip[0] + chip[1]
            for t in range(n):
                if split[t]:
                    piece = outs[t].at[theirs, rows(t, 1 - c)]
                    _remote(piece, piece, send_sems, recv_sems, 3 * n + 3 * t + j, (x, y, c)).wait_recv()
        for cp in sent:
            cp.wait_send()
        for cp in local:
            cp.wait()

    return pl.kernel(
        body, name=f"gather_{tag}", mesh=plsc.ScalarSubcoreMesh(axis_name="sequencer", num_cores=1),
        out_type=[jax.ShapeDtypeStruct((N_SHARDS, *s.shape), s.dtype) for s in shards],
        scratch_types=[pltpu.SemaphoreType.DMA((6 * n,)), pltpu.SemaphoreType.DMA((6 * n,)), pltpu.SemaphoreType.DMA((n,))],
        compiler_params=pltpu.CompilerParams(collective_id=ID_GATHER),
    )(*shards)


SEQUENCER = dict(axis_name="sequencer", num_cores=1)
ID_GATHER, ID_EXCHANGE, ID_SHARE = 0, 1, 2
MIN_SPLIT_ROWS = 16


def _handshake(peers):
    barrier = pltpu.get_barrier_semaphore()
    for peer in peers:
        pl.semaphore_signal(barrier, inc=1, device_id=peer, device_id_type=MESH)
    pl.semaphore_wait(barrier, len(peers))


def _half_rows(rows, half):
    return pl.ds(half * (rows // 2), rows // 2) if rows >= MIN_SPLIT_ROWS else pl.ds(0, rows)


SEM_SPEC = pl.BlockSpec(memory_space=pltpu.SEMAPHORE)
DATAFLOW = pltpu.SideEffectType.DATAFLOW_SIDE_EFFECTING


def _exchange_copies(shapes, stacks, lands, send_sems, recv_sems):
    x, y, c, chips = _place()
    mine = 2 * x + y
    copies = []
    for t, shape in enumerate(shapes):
        r = shape[1]
        copies.append(_remote(stacks[t].at[mine, _half_rows(r, 1 - c)], lands[t].at[0], send_sems, recv_sems, 7 * t, (x, y, 1 - c)))
        for j, chip in enumerate(chips):
            theirs = 2 * chip[0] + chip[1]
            copies.append(_remote(stacks[t].at[theirs, _half_rows(r, c)], lands[t].at[1 + j], send_sems, recv_sems,
                                  7 * t + 1 + j, (*chip, c)))
            copies.append(_remote(stacks[t].at[theirs, _half_rows(r, 1 - c)], lands[t].at[4 + j], send_sems, recv_sems,
                                  7 * t + 4 + j, (*chip, 1 - c)))
    return copies


def _exchange_start(tag, stacks, carry):
    n = len(stacks)
    shapes = [s.shape for s in stacks]
    lands = [lax.empty((7, s.shape[1] // 2 if s.shape[1] >= MIN_SPLIT_ROWS else s.shape[1], s.shape[2]), s.dtype) for s in stacks]

    def body(*refs):
        for cp in _exchange_copies(shapes, refs[:n], refs[n:2 * n], refs[2 * n + 1], refs[2 * n + 2]):
            cp.start()

    through = (*stacks, *lands, carry)
    out = pl.pallas_call(
        body, name=f"reduce_exchange_start_{tag}",
        out_shape=(pltpu.SemaphoreType.DMA((7 * n,)), pltpu.SemaphoreType.DMA((7 * n,)),
                   *[pltpu.HBM(a.shape, a.dtype) for a in through]),
        in_specs=[HBM_SPEC] * (2 * n + 1),
        out_specs=(SEM_SPEC, SEM_SPEC, *[HBM_SPEC] * (2 * n + 1)),
        input_output_aliases={t: 2 + t for t in range(2 * n + 1)},
        compiler_params=pltpu.CompilerParams(has_side_effects=DATAFLOW),
    )(*[pltpu.with_memory_space_constraint(a, pltpu.HBM) for a in through])
    return out[0], out[1], out[2:2 + n], out[2 + n:2 + 2 * n], out[-1]


def _exchange_wait(tag, send_sems, recv_sems, stacks, lands, after):
    n = len(stacks)
    shapes = [s.shape for s in stacks]

    def body(*refs):
        for cp in _exchange_copies(shapes, refs[:n], refs[n:2 * n], refs[2 * n], refs[2 * n + 1]):
            cp.wait()

    out = pl.pallas_call(
        body, name=f"reduce_exchange_wait_{tag}",
        out_shape=tuple(pltpu.HBM(a.shape, a.dtype) for a in (*stacks, *lands)),
        in_specs=[HBM_SPEC] * (2 * n) + [SEM_SPEC, SEM_SPEC, pl.BlockSpec(memory_space=pl.ANY)],
        out_specs=tuple([HBM_SPEC] * (2 * n)),
        input_output_aliases={t: t for t in range(2 * n)},
        compiler_params=pltpu.CompilerParams(has_side_effects=DATAFLOW),
    )(*stacks, *lands, send_sems, recv_sems, after)
    return out[:n], out[n:]


def _share_halves(tag, halves):
    n = len(halves)

    def body(*refs):
        ins, outs, send_sems, recv_sems = refs[:n], refs[n:2 * n], refs[2 * n], refs[2 * n + 1]
        x, y, c, _ = _place()
        _handshake([(x, y, 1 - c)])
        sent = [_remote(ins[t], outs[t], send_sems, recv_sems, t, (x, y, 1 - c)) for t in range(n)]
        for cp in sent:
            cp.start()
        for cp in sent:
            cp.wait()

    return pl.kernel(
        body, name=f"reduce_share_{tag}", mesh=plsc.ScalarSubcoreMesh(**SEQUENCER),
        out_type=[jax.ShapeDtypeStruct(h.shape, h.dtype) for h in halves],
        scratch_types=[pltpu.SemaphoreType.DMA((n,)), pltpu.SemaphoreType.DMA((n,))],
        compiler_params=pltpu.CompilerParams(collective_id=ID_SHARE),
    )(*halves)


def _all_reduce_small(part):
    rows = part.shape[0]

    def body(p_ref, out_ref, sib_buf, chip_sums, send_sems, recv_sems):
        x, y, c, chips = _place()
        mine = 2 * x + y
        swap = _remote(p_ref, sib_buf, send_sems, recv_sems, 0, (x, y, 1 - c))
        swap.start()
        swap.wait()
        chip_sums[mine] = p_ref[...] + sib_buf[...]
        sent = [_remote(chip_sums.at[mine], chip_sums.at[mine], send_sems, recv_sems, 1 + j, (*chip, c))
                for j, chip in enumerate(chips)]
        for cp in sent:
            cp.start()
        for j, chip in enumerate(chips):
            sent[j].wait_send()
            theirs = chip_sums.at[2 * chip[0] + chip[1]]
            _remote(theirs, theirs, send_sems, recv_sems, 1 + j, (x, y, c)).wait_recv()
        out_ref[...] = ((chip_sums[0] + chip_sums[1]) + chip_sums[2]) + chip_sums[3]

    vmem = pl.BlockSpec(memory_space=pltpu.VMEM)
    return pl.pallas_call(
        body, name="all_reduce_small", in_specs=[vmem], out_specs=vmem, out_shape=jax.ShapeDtypeStruct(part.shape, F32),
        scratch_shapes=[pltpu.VMEM((rows, LANE), F32), pltpu.VMEM((N_SHARDS, rows, LANE), F32),
                        pltpu.SemaphoreType.DMA((4,)), pltpu.SemaphoreType.DMA((4,))],
        compiler_params=pltpu.CompilerParams(vmem_limit_bytes=VMEM_LIMIT_BYTES),
    )(part)


def _sum_partials(g3, others, sel):
    _, rows, c = others.shape
    whole = g3.shape[1] == rows
    tr = _tile(rows, 512)
    nb = rows // tr

    def body(sel_ref, g_ref, *rest):
        same = g_ref[...].astype(F32)
        for ref in rest[1:4]:
            same = same + ref[...].astype(F32)
        other = rest[0][...].astype(F32)
        for ref in rest[4:7]:
            other = other + ref[...].astype(F32)
        rest[7][...] = same + other

    blk = (None, tr, c)
    slots = [pl.BlockSpec(blk, functools.partial(lambda i, sr, k: (k, i, 0), k=k)) for k in range(7)]
    return pl.pallas_call(
        body, name="reduce_sum_partials",
        grid_spec=pltpu.PrefetchScalarGridSpec(
            num_scalar_prefetch=1, grid=(nb,),
            in_specs=[pl.BlockSpec(blk, lambda i, sr: (sr[0], (0 if whole else sr[1] * nb) + i, 0))] + slots,
            out_specs=pl.BlockSpec((tr, c), lambda i, sr: (i, 0))),
        out_shape=jax.ShapeDtypeStruct((rows, c), F32),
        compiler_params=_params(("parallel",)),
    )(sel, g3, *[others] * 7)


def _adamw_math(w, g, m, v):
    nm = ADAM_B1 * m + (1.0 - ADAM_B1) * g
    nv = ADAM_B2 * v + (1.0 - ADAM_B2) * (g * g)
    m_hat = nm / (1.0 - ADAM_B1 ** ADAM_STEP)
    v_hat = nv / (1.0 - ADAM_B2 ** ADAM_STEP)
    return -ADAM_LR * (m_hat / (jnp.sqrt(v_hat) + ADAM_EPS) + ADAM_WD * w), nm, nv


def _adamw_layer(layer, w, m, v, g_mine, g_sibling, sel, prev):
    lyr, r, c = w.shape
    rows = g_mine.shape[0]
    halves = r // rows
    tr = _tile(rows, 512)
    nb = rows // tr
    n_g = 1 if g_sibling is None else 2

    def body(sel_ref, w_ref, m_ref, v_ref, *rest):
        g = rest[0][...]
        if n_g == 2:
            g = jnp.where(pl.program_id(0) == sel_ref[1], g, rest[1][...])
        outs = rest[n_g + (0 if prev is None else 4):]
        d, nm, nv = _adamw_math(w_ref[...], g, m_ref[...], v_ref[...])
        for ref, val in zip(outs, (g, d, nm, nv)):
            ref[...] = val

    full = pl.BlockSpec((None, tr, c), lambda h, i, sr: (layer, h * nb + i, 0))
    part = pl.BlockSpec((tr, c), lambda h, i, sr: (i, 0))
    n_in = 4 + n_g
    return pl.pallas_call(
        body, name="adamw_layer",
        grid_spec=pltpu.PrefetchScalarGridSpec(
            num_scalar_prefetch=1, grid=(halves, nb),
            in_specs=[full] * 3 + [part] * n_g + ([] if prev is None else [pl.BlockSpec(memory_space=pl.ANY)] * 4),
            out_specs=[full] * 4),
        out_shape=[jax.ShapeDtypeStruct(w.shape, F32)] * 4,
        input_output_aliases={} if prev is None else {n_in + k: k for k in range(4)},
        compiler_params=_params(("parallel", "parallel")),
    )(sel, w, m, v, g_mine, *([] if g_sibling is None else [g_sibling]), *([] if prev is None else prev))


def _adamw(w, g, m, v):
    lyr, r, c = w.shape
    tr = _tile(r, 512)

    def body(w_ref, g_ref, m_ref, v_ref, d_ref, nm_ref, nv_ref):
        gv = g_ref[...]
        nm = ADAM_B1 * m_ref[...] + (1.0 - ADAM_B1) * gv
        nv = ADAM_B2 * v_ref[...] + (1.0 - ADAM_B2) * (gv * gv)
        m_hat = nm / (1.0 - ADAM_B1 ** ADAM_STEP)
        v_hat = nv / (1.0 - ADAM_B2 ** ADAM_STEP)
        d_ref[...] = -ADAM_LR * (m_hat / (jnp.sqrt(v_hat) + ADAM_EPS) + ADAM_WD * w_ref[...])
        nm_ref[...] = nm
        nv_ref[...] = nv

    blk = pl.BlockSpec((None, tr, c), lambda l, i: (l, i, 0))
    return pl.pallas_call(
        body, name="adamw", grid=(lyr, r // tr), in_specs=[blk] * 4, out_specs=[blk] * 3,
        out_shape=[jax.ShapeDtypeStruct(w.shape, F32)] * 3,
        compiler_params=_params(("parallel", "parallel")),
    )(w, g, m, v)


SHARDED = ("mla_w_dkv", "mla_w_uq", "mla_w_ukv", "mla_w_o", "sgu_w_in", "sgu_ln_g", "sgu_ln_b", "sgu_w_out",
           "ffn_w_up", "ffn_w_down")
REPLICATED = ("norm_mix", "norm_ffn", "final_norm", "mla_q_norm", "mla_kv_norm", "sgu_w_spatial", "sgu_b_spatial")
WEIGHTS = ("norm_mix", "norm_ffn", "final_norm", "mla_w_dkv", "mla_q_norm", "mla_kv_norm", "mla_w_uq", "mla_w_ukv",
           "mla_w_o", "sgu_w_in", "sgu_ln_g", "sgu_ln_b", "sgu_w_spatial", "sgu_b_spatial", "sgu_w_out", "ffn_w_up",
           "ffn_w_down")


class _Reducer:
    def __init__(self, state, sel):
        self.state, self.sel = state, sel
        self.started, self.travelling, self.summed = [], [], []
        self.done = {}

    def add(self, tag, layer, grads, token):
        names = list(grads)
        token, *tied = lax.optimization_barrier((token, *[a for n in names for a in grads[n]]))
        f32s, bf16s = tied[0::2], tied[1::2]
        *flying, token = _exchange_start(tag, bf16s, token)
        self.started.append((tag, layer, names, f32s, flying))
        return token

    def phase_end(self, token):
        for tag, layer, names, f32s, flying in self.travelling:
            bf16s, received = _exchange_wait(tag, *flying, token)
            own = [g if g.shape[1] >= MIN_SPLIT_ROWS else gb for g, gb in zip(f32s, bf16s)]
            mine = [_sum_partials(g, got, self.sel) for g, got in zip(own, received)]
            token, *mine = lax.optimization_barrier((token, *mine))
            cut = [k for k, g in enumerate(own) if g.shape[1] >= MIN_SPLIT_ROWS]
            theirs = dict(zip(cut, _share_halves(tag, [mine[k] for k in cut])))
            self.summed.append((layer, names, mine, [theirs.get(k) for k in range(len(names))]))
        self.travelling, self.started = self.started, []
        return token

    def update(self, token):
        for layer, names, mine, theirs in self.summed:
            for name, g_mine, g_theirs in zip(names, mine, theirs):
                w, m, v = self.state[name]
                self.done[name] = _adamw_layer(layer, w, m, v, g_mine, g_theirs, self.sel, self.done.get(name))
                token = self.done[name][1]
        self.summed = []
        return token


def _as3d(name, a):
    return a.reshape(a.shape[0], -1, LANE) if name in ("sgu_ln_g", "sgu_ln_b") else a


def _pack(parts):
    flat = jnp.concatenate([p.reshape(-1) for p in parts])
    rows = -(-flat.shape[0] // (8 * LANE)) * 8
    return jnp.pad(flat, (0, rows * LANE - flat.shape[0])).reshape(rows, LANE)


def _unpack(packed, like):
    flat, out, at = packed.reshape(-1), [], 0
    for p in like:
        out.append(flat[at:at + p.size].reshape(p.shape))
        at += p.size
    return out


def kernel(x, positions, norm_mix, norm_ffn, final_norm, mla_w_dkv, mla_q_norm, mla_kv_norm, mla_w_uq, mla_w_ukv, mla_w_o, sgu_w_in, sgu_ln_g, sgu_ln_b, sgu_w_spatial, sgu_b_spatial, sgu_w_out, ffn_w_up, ffn_w_down, loss_target, m_norm_mix, m_norm_ffn, m_final_norm, m_mla_w_dkv, m_mla_q_norm, m_mla_kv_norm, m_mla_w_uq, m_mla_w_ukv, m_mla_w_o, m_sgu_w_in, m_sgu_ln_g, m_sgu_ln_b, m_sgu_w_spatial, m_sgu_b_spatial, m_sgu_w_out, m_ffn_w_up, m_ffn_w_down, v_norm_mix, v_norm_ffn, v_final_norm, v_mla_w_dkv, v_mla_q_norm, v_mla_kv_norm, v_mla_w_uq, v_mla_w_ukv, v_mla_w_o, v_sgu_w_in, v_sgu_ln_g, v_sgu_ln_b, v_sgu_w_spatial, v_sgu_b_spatial, v_sgu_w_out, v_ffn_w_up, v_ffn_w_down):
    given = dict(locals())
    w = {n: given[n] for n in WEIGHTS}
    mom = {n: given["m_" + n] for n in WEIGHTS}
    var = {n: given["v_" + n] for n in WEIGHTS}
    mixers, ffn, token = [], [], None
    for i in range(DEPTH):
        j = i // 2
        if i % 2 == 0:
            mixer = [w[n][j].astype(BF16) for n in ("mla_w_dkv", "mla_w_uq", "mla_w_ukv", "mla_w_o")]
        else:
            mixer = [sgu_w_in[j].astype(BF16), sgu_w_out[j].astype(BF16), sgu_ln_g[j].reshape(-1, LANE),
                     sgu_ln_b[j].reshape(-1, LANE)]
        for tag, shards, into in ((f"mixer{i}", mixer, mixers), (f"ffn{i}", [ffn_w_up[i].astype(BF16), ffn_w_down[i].astype(BF16)], ffn)):
            if token is None:
                token = shards[0]
            else:
                token, *shards = lax.optimization_barrier((token, *shards))
            into.append(_gather_layer(tag, shards))

    x_i, y_i, c_i = lax.axis_index("x"), lax.axis_index("y"), lax.axis_index("c")
    sel = jnp.stack([2 * x_i + y_i, c_i]).astype(jnp.int32)
    reducer = _Reducer({n: tuple(_as3d(n, d[n]) for d in (w, mom, var)) for n in SHARDED}, sel)
    loss, dx, small = _local_step(
        x[0], positions[0], loss_target[0], norm_mix, norm_ffn, final_norm, mla_q_norm, mla_kv_norm, sgu_w_spatial,
        sgu_b_spatial, mixers, ffn, reducer)
    loss = lax.psum(loss, ("x", "y", "c"))

    small_g = [small["norm_mix"], small["norm_ffn"], small["final_norm"], small["q_norm"], small["kv_norm"],
               small["w_sp"], small["b_sp"]]
    like = [w[n] for n in REPLICATED]
    g_small = _all_reduce_small(_pack(small_g))
    packed = [_pack([d[n] for n in REPLICATED])[None] for d in (w, mom, var)]
    upd_small = _adamw(packed[0], g_small[None], packed[1], packed[2])
    grads = dict(zip(REPLICATED, _unpack(g_small, like)))
    delta, new_m, new_v = ({n: a for n, a in zip(REPLICATED, _unpack(u[0], like))} for u in upd_small)

    reducer.phase_end(reducer.update(upd_small[0]))
    reducer.update(None)
    for n in SHARDED:
        grads[n], delta[n], new_m[n], new_v[n] = (a.reshape(w[n].shape) for a in reducer.done[n])

    return (loss, dx[None], *[grads[n] for n in WEIGHTS], *[delta[n] for n in WEIGHTS],
            *[new_m[n] for n in WEIGHTS], *[new_v[n] for n in WEIGHTS])
```

```python
import functools
import math

import jax
import jax.numpy as jnp
from jax import lax
from jax.experimental import pallas as pl
from jax.experimental.pallas import tpu as pltpu
from jax.experimental.pallas import tpu_sc as plsc

F32 = jnp.float32
BF16 = jnp.bfloat16
MESH = pl.DeviceIdType.MESH

DEPTH = 4
HEADS = 8
NOPE = 128
ROPE = 64
VHEAD = 128
QK_HEAD = NOPE + ROPE
Q_RANK = 256
KV_RANK = 128
HEAD_PAD = 256
LAT_PAD = 512
ROPE_THETA = 10000.0
SGU_CHUNK = 128
SGU_GROUPS = 8
NORM_EPS = 1e-6
LN_EPS = 1e-5
ADAM_LR, ADAM_B1, ADAM_B2, ADAM_EPS, ADAM_WD, ADAM_STEP = 0.001, 0.9, 0.999, 1e-08, 0.01, 10

N_SHARDS = 4
LANE = 128
VMEM_LIMIT_BYTES = 56 * 1024 * 1024
ATT_TILE = 512
MM_TILE = 1024
UPDATE_ROWS = 128
ATT_SCALE = QK_HEAD ** -0.5
LOG2_SCALE = ATT_SCALE * math.log2(math.e)

NN = (((1,), (0,)), ((), ()))
NT = (((1,), (1,)), ((), ()))
TN = (((0,), (0,)), ((), ()))


def _params(sem):
    return pltpu.CompilerParams(dimension_semantics=sem, vmem_limit_bytes=VMEM_LIMIT_BYTES)


def _tile(n, pref):
    t = min(n, pref)
    while n % t:
        t //= 2
    return t


def _matmul(name, a, b, a_spec, b_spec, dims, grid, tile, outs, extras=(), epilogue=None, sums=(), norm_g=None):
    nk, ne, no = grid[2], len(extras), len(outs)
    b_specs = list(b_spec) if isinstance(b_spec, (list, tuple)) else [b_spec]
    nb = len(b_specs)
    normed = norm_g is not None
    assert not normed or (nk == 1 and nb == 1)

    def body(a_ref, *rest):
        b_refs, e_refs, o_refs = rest[:nb], rest[nb:nb + ne], rest[nb + ne + normed:nb + ne + normed + no]
        s_refs = rest[nb + ne + normed + no:nb + ne + normed + no + len(sums)]
        if normed:
            g_ref, h_ref, h_s = rest[nb + ne], rest[nb + ne + 1 + no + len(sums)], rest[-1]

            @pl.when(pl.program_id(1) == 0)
            def _():
                h_s[...] = _rms_fwd(a_ref[...], g_ref[...]).astype(BF16)
                h_ref[...] = h_s[...]

            a_ref = h_s
        kw = a_ref.shape[1] // nb
        part = None
        for p, b_ref in enumerate(b_refs):
            a_tile = a_ref[...] if nb == 1 else a_ref[:, p * kw:(p + 1) * kw]
            d = lax.dot_general(a_tile.astype(BF16), b_ref[...].astype(BF16), dims, preferred_element_type=F32)
            part = d if part is None else part + d

        def finish(acc):
            vals = (acc,) if epilogue is None else epilogue(acc, *[e[...] for e in e_refs])
            for o_ref, v in zip(o_refs, vals):
                o_ref[...] = v.astype(o_ref.dtype)
            first = pl.program_id(0) == 0
            for s_ref, v in zip(s_refs, vals[no:]):
                @pl.when(first)
                def _():
                    s_ref[...] = v

                @pl.when(jnp.logical_not(first))
                def _():
                    s_ref[...] += v

        if nk == 1:
            finish(part)
            return
        acc_ref, k = rest[-1], pl.program_id(2)

        @pl.when(k == 0)
        def _():
            acc_ref[...] = part

        @pl.when(jnp.logical_and(k > 0, k < nk - 1))
        def _():
            acc_ref[...] += part

        @pl.when(k == nk - 1)
        def _():
            finish(acc_ref[...] + part)

    assert not sums or (grid[1] == 1 and nk == 1)
    a_block = a_spec.block_shape
    return pl.pallas_call(
        body, name=name, grid=grid,
        in_specs=[a_spec] + b_specs + [s for _, s in extras]
        + ([pl.BlockSpec(norm_g.shape, lambda i, j, k: (0, 0))] if normed else []),
        out_specs=[s for _, s in outs] + [pl.BlockSpec(s, lambda i, j, k: (0,) * len(s)) for s in sums]
        + ([pl.BlockSpec(a_block, lambda i, j, k: (i, 0))] if normed else []),
        out_shape=[s for s, _ in outs] + [jax.ShapeDtypeStruct(s, F32) for s in sums]
        + ([jax.ShapeDtypeStruct(a.shape, BF16)] if normed else []),
        scratch_shapes=([pltpu.VMEM(tile, F32)] if nk > 1 else []) + ([pltpu.VMEM(a_block, BF16)] if normed else []),
        compiler_params=_params(("arbitrary" if sums else "parallel", "arbitrary" if normed else "parallel", "arbitrary")),
    )(a, *[b] * nb, *[e for e, _ in extras], *([norm_g] if normed else []))


def _epilogue_operands(extras, consts, o_spec):
    return [(e, o_spec) for e in extras] + [(c, pl.BlockSpec(c.shape, lambda i, j, k: (0, 0))) for c in consts]


def _mm(name, a, b, out_dtypes=(F32,), epilogue=None, extras=(), tm=MM_TILE, tn=MM_TILE, tk=MM_TILE, nt=False,
        consts=(), sums=(), narrow=()):
    m, kd = a.shape
    n = b.shape[0] if nt else b.shape[1]
    tm, tn, tk = _tile(m, tm), _tile(n, tn), _tile(kd, tk)
    o_spec = pl.BlockSpec((tm, tn), lambda i, j, k: (i, j))
    b_spec = pl.BlockSpec((tn, tk), lambda i, j, k: (j, k)) if nt else pl.BlockSpec((tk, tn), lambda i, j, k: (k, j))
    assert not narrow or n == tn
    outs = [(jax.ShapeDtypeStruct((m, n), d), o_spec) for d in out_dtypes]
    outs += [(jax.ShapeDtypeStruct((m, w), d), pl.BlockSpec((tm, w), lambda i, j, k: (i, 0))) for w, d in narrow]
    return _matmul(name, a, b, pl.BlockSpec((tm, tk), lambda i, j, k: (i, k)), b_spec, NT if nt else NN,
                   (m // tm, n // tn, kd // tk), (tm, tn), outs,
                   _epilogue_operands(extras, consts, o_spec), epilogue, sums)


def _mm_tn(name, a, b, out_dtypes=(F32,), tm=MM_TILE, tn=MM_TILE, tk=MM_TILE):
    s, m = a.shape
    n = b.shape[1]
    tm, tn, tk = _tile(m, tm), _tile(n, tn), _tile(s, tk)
    o_spec = pl.BlockSpec((tm, tn), lambda i, j, k: (i, j))
    return _matmul(name, a, b, pl.BlockSpec((tk, tm), lambda i, j, k: (k, i)),
                   pl.BlockSpec((tk, tn), lambda i, j, k: (k, j)), TN, (m // tm, n // tn, s // tk), (tm, tn),
                   [(jax.ShapeDtypeStruct((m, n), d), o_spec) for d in out_dtypes])


def _mm_stacked(name, a, w3, mode, out_dtypes=(F32,), epilogue=None, extras=(), tm=MM_TILE, tn=MM_TILE, tk=MM_TILE,
                norm_g=None):
    m, kd = a.shape
    _, r, c = w3.shape
    n = c if mode == "row" else N_SHARDS * c
    if mode == "row":
        tm, tn, tk = _tile(m, tm // 2), _tile(n, tn), kd
        b_spec = [pl.BlockSpec((None, r, tn), functools.partial(lambda i, j, k, p: (p, 0, j), p=p)) for p in range(N_SHARDS)]
    else:
        tm, tn, tk = _tile(m, tm), _tile(c, tn), _tile(kd, tk)
        per = c // tn
        b_spec = pl.BlockSpec((None, tk, tn), lambda i, j, k: (j // per, k, j % per))
    o_spec = pl.BlockSpec((tm, tn), lambda i, j, k: (i, j))
    return _matmul(name, a, w3, pl.BlockSpec((tm, tk), lambda i, j, k: (i, k)), b_spec, NN,
                   (m // tm, n // tn, kd // tk), (tm, tn),
                   [(jax.ShapeDtypeStruct((m, n), d), o_spec) for d in out_dtypes],
                   [(e, o_spec) for e in extras], epilogue, norm_g=norm_g)


def _mm_stacked_nt(name, a, w3, mode, out_dtypes=(F32,), epilogue=None, extras=(), tm=MM_TILE, tn=MM_TILE, tk=MM_TILE,
                   consts=(), sums=()):
    m, nd = a.shape
    _, r, c = w3.shape
    kout = N_SHARDS * r if mode == "row" else r
    if mode == "row":
        tm, tn, tk = _tile(m, tm), _tile(r, tn), _tile(c, tk)
        per = r // tn
        b_spec = pl.BlockSpec((None, tn, tk), lambda i, j, k: (j // per, j % per, k))
    else:
        tm, tn, tk = _tile(m, tm // 2), _tile(r, tn), nd
        b_spec = [pl.BlockSpec((None, tn, c), functools.partial(lambda i, j, k, p: (p, j, 0), p=p)) for p in range(N_SHARDS)]
    o_spec = pl.BlockSpec((tm, tn), lambda i, j, k: (i, j))
    return _matmul(name, a, w3, pl.BlockSpec((tm, tk), lambda i, j, k: (i, k)), b_spec, NT,
                   (m // tm, kout // tn, nd // tk), (tm, tn),
                   [(jax.ShapeDtypeStruct((m, kout), d), o_spec) for d in out_dtypes],
                   _epilogue_operands(extras, consts, o_spec), epilogue, sums)


def _mm_tn_stacked(name, a, b, shape3, mode, tm=MM_TILE, tn=MM_TILE, tk=MM_TILE):
    s, m = a.shape
    n = b.shape[1]
    _, r, c = shape3
    tk, tn = s, tn // 2
    if mode == "row":
        tm, tn = _tile(r, tm), _tile(n, tn)
        per = r // tm
        o_spec = pl.BlockSpec((None, tm, tn), lambda i, j, k: (i // per, i % per, j))
    else:
        tm, tn = _tile(m, tm), _tile(c, tn)
        per = c // tn
        o_spec = pl.BlockSpec((None, tm, tn), lambda i, j, k: (j // per, i, j % per))
    outs = [(jax.ShapeDtypeStruct(shape3, F32), o_spec), (jax.ShapeDtypeStruct(shape3, BF16), o_spec)]
    return _matmul(name, a, b, pl.BlockSpec((tk, tm), lambda i, j, k: (k, i)),
                   pl.BlockSpec((tk, tn), lambda i, j, k: (k, j)), TN, (m // tm, n // tn, s // tk), (tm, tn),
                   outs, epilogue=lambda acc: (acc, acc))


def _rowwise(name, fn, rows, consts, out_rows, out_accs=(), tr=256):
    nr, nc, no = len(rows), len(consts), len(out_rows)
    n_rows = rows[0].shape[0]
    tr = _tile(n_rows, tr)

    def body(*refs):
        vals = fn(*[r[...] for r in refs[:nr + nc]])
        o_refs, a_refs = refs[nr + nc:nr + nc + no], refs[nr + nc + no:]
        for ref, v in zip(o_refs, vals[:no]):
            ref[...] = v.astype(ref.dtype)
        first = pl.program_id(0) == 0

        @pl.when(first)
        def _():
            for ref, v in zip(a_refs, vals[no:]):
                ref[...] = v

        @pl.when(jnp.logical_not(first))
        def _():
            for ref, v in zip(a_refs, vals[no:]):
                ref[...] += v

    def whole(shape):
        return pl.BlockSpec(shape, lambda i: (0,) * len(shape))

    return pl.pallas_call(
        body, name=name, grid=(n_rows // tr,),
        in_specs=[pl.BlockSpec((tr, a.shape[1]), lambda i: (i, 0)) for a in rows] + [whole(c.shape) for c in consts],
        out_specs=[pl.BlockSpec((tr, f), lambda i: (i, 0)) for f, _ in out_rows] + [whole(s) for s in out_accs],
        out_shape=[jax.ShapeDtypeStruct((n_rows, f), d) for f, d in out_rows]
        + [jax.ShapeDtypeStruct(s, F32) for s in out_accs],
        compiler_params=_params(("arbitrary",)),
    )(*rows, *consts)


def _rms_fwd(x, g):
    return x * lax.rsqrt(jnp.mean(x * x, axis=-1, keepdims=True) + NORM_EPS) * g


def _rms_bwd(dy, x, g):
    rstd = lax.rsqrt(jnp.mean(x * x, axis=-1, keepdims=True) + NORM_EPS)
    n = x * rstd
    dn = dy * g
    dx = rstd * (dn - n * jnp.mean(dn * n, axis=-1, keepdims=True))
    return dx, jnp.sum(dy * n, axis=0, keepdims=True)


def _rope(x, cs, s1, s2):
    return x * cs + pltpu.roll(x, 32, 1) * s1 + pltpu.roll(x, 96, 1) * s2


def _rope_t(dy, cs, s1, s2):
    return dy * cs + pltpu.roll(dy * s1, 96, 1) + pltpu.roll(dy * s2, 32, 1)


def _gelu(z):
    return 0.5 * z * (1.0 + lax.erf(z * (1.0 / math.sqrt(2.0))))


def _gelu_and_grad(z):
    cdf = 0.5 * (1.0 + lax.erf(z * (1.0 / math.sqrt(2.0))))
    return z * cdf, cdf + z * jnp.exp(-0.5 * z * z) * (1.0 / math.sqrt(2.0 * math.pi))


def _att_scores(q, kv, kr, scale, masked, transposed):
    k = jnp.concatenate([kv[:, :NOPE], kr], axis=1)
    if transposed:
        s = lax.dot_general(k, q, NT, preferred_element_type=F32) * scale
    else:
        s = lax.dot_general(q, k, NT, preferred_element_type=F32) * scale
    if masked:
        r = lax.broadcasted_iota(jnp.int32, s.shape, 0)
        c = lax.broadcasted_iota(jnp.int32, s.shape, 1)
        s = jnp.where((r <= c) if transposed else (c <= r), s, -jnp.inf)
    return s, k


def _in_pairs(lo, hi, pair, single):
    n = hi - lo

    def body(p, carry):
        pair(lo + 2 * p, lo + 2 * p + 1)
        return carry

    lax.fori_loop(0, n // 2, body, 0)

    @pl.when(n % 2 == 1)
    def _():
        single(hi - 1)


def _causal_tiles(i, pair, single):
    @pl.when(i == 0)
    def _():
        single(i, True)

    @pl.when(i > 0)
    def _():
        _in_pairs(0, i - 1, lambda a, b: pair(a, b, False), lambda a: single(a, False))
        pair(i - 1, i, True)


def _flash_fwd(q, kvb, krb, tables):
    s_len = q.shape[0]
    t = ATT_TILE

    def body(q_ref, cs_ref, s1_ref, s2_ref, kv_ref, kr_ref, o_ref, lse_ref, qb_ref, m_s, l_s, acc_s):
        qi = pl.program_id(1)
        m_s[...] = jnp.full_like(m_s, -jnp.inf)
        l_s[...] = jnp.zeros_like(l_s)
        acc_s[...] = jnp.zeros_like(acc_s)
        qv = q_ref[...]
        q = jnp.concatenate([qv[:, :NOPE], _rope(qv[:, NOPE:], cs_ref[...], s1_ref[...], s2_ref[...])], axis=1).astype(BF16)
        qb_ref[...] = q

        def scores(ki, masked):
            rows = pl.ds(pl.multiple_of(ki * t, t), t)
            kv = kv_ref[rows, :]
            return _att_scores(q, kv, kr_ref[rows, :], LOG2_SCALE, masked, False)[0], kv

        def update(s, kv):
            m_prev = m_s[...]
            m_new = jnp.maximum(m_prev, jnp.max(s, axis=1, keepdims=True))
            alpha = jnp.exp2(m_prev - m_new)
            p = jnp.exp2(s - jnp.tile(m_new, (1, t // LANE)))
            l_s[...] = alpha * l_s[...] + jnp.sum(p, axis=1, keepdims=True)
            acc_s[...] = alpha * acc_s[...] + jnp.dot(p.astype(BF16), kv[:, NOPE:], preferred_element_type=F32)
            m_s[...] = m_new

        def pair(k0, k1, masked):
            first, second = scores(k0, False), scores(k1, masked)
            update(*first)
            update(*second)

        _causal_tiles(qi, pair, lambda ki, masked: update(*scores(ki, masked)))
        o_ref[...] = (acc_s[...] / l_s[...]).astype(o_ref.dtype)
        lse_ref[...] = (m_s[...] + jnp.log2(l_s[...]))[:, :1]

    table = pl.BlockSpec((t, LANE), lambda h, qi: (qi, 0))
    return pl.pallas_call(
        body, name="flash_fwd", grid=(HEADS, s_len // t),
        in_specs=[pl.BlockSpec((t, HEAD_PAD), lambda h, qi: (qi, h)), table, table, table,
                  pl.BlockSpec((s_len, HEAD_PAD), lambda h, qi: (0, h)),
                  pl.BlockSpec((s_len, LANE), lambda h, qi: (0, 0))],
        out_specs=[pl.BlockSpec((t, VHEAD), lambda h, qi: (qi, h)),
                   pl.BlockSpec((None, t, 1), lambda h, qi: (h, qi, 0)),
                   pl.BlockSpec((t, HEAD_PAD), lambda h, qi: (qi, h))],
        out_shape=[jax.ShapeDtypeStruct((s_len, HEADS * VHEAD), BF16),
                   jax.ShapeDtypeStruct((HEADS, s_len, 1), F32),
                   jax.ShapeDtypeStruct((s_len, HEADS * HEAD_PAD), BF16)],
        scratch_shapes=[pltpu.VMEM((t, LANE), F32), pltpu.VMEM((t, LANE), F32), pltpu.VMEM((t, VHEAD), F32)],
        compiler_params=_params(("parallel", "arbitrary")),
    )(q, *tables, kvb, krb)


def _flash_bwd_dq(qb, kvb, krb, dob, lse, delta, tables):
    s_len = qb.shape[0]
    t = ATT_TILE
    nq = s_len // t
    scale = QK_HEAD ** -0.5

    def body(q_ref, kv_ref, kr_ref, do_ref, lse_ref, dl_ref, cs_ref, s1_ref, s2_ref, dq_ref, acc_s):
        qi = pl.program_id(1)
        acc_s[...] = jnp.zeros_like(acc_s)
        q, do = q_ref[...], do_ref[...]
        lse = jnp.broadcast_to(lse_ref[...], (t, LANE))
        dl = dl_ref[...]

        def products(ki, masked):
            rows = pl.ds(pl.multiple_of(ki * t, t), t)
            kv = kv_ref[rows, :]
            s, k = _att_scores(q, kv, kr_ref[rows, :], LOG2_SCALE, masked, False)
            return s, lax.dot_general(do, kv[:, NOPE:], NT, preferred_element_type=F32), k

        def update(s, dp, k):
            p = jnp.exp2(s - jnp.tile(lse, (1, t // LANE)))
            ds = (p * (dp - jnp.tile(dl, (1, t // LANE))) * scale).astype(BF16)
            acc_s[...] += jnp.dot(ds, k, preferred_element_type=F32)

        def pair(k0, k1, masked):
            first, second = products(k0, False), products(k1, masked)
            update(*first)
            update(*second)

        _causal_tiles(qi, pair, lambda ki, masked: update(*products(ki, masked)))
        dq = acc_s[...]
        dq = jnp.concatenate([dq[:, :NOPE], _rope_t(dq[:, NOPE:], cs_ref[...], s1_ref[...], s2_ref[...])], axis=1)
        dq_ref[...] = dq.astype(dq_ref.dtype)

    col = pl.BlockSpec((None, t, 1), lambda h, qi: (h, qi, 0))
    table = pl.BlockSpec((t, LANE), lambda h, qi: (qi, 0))
    return pl.pallas_call(
        body, name="flash_bwd_dq", grid=(HEADS, nq),
        in_specs=[pl.BlockSpec((t, HEAD_PAD), lambda h, qi: (qi, h)),
                  pl.BlockSpec((s_len, HEAD_PAD), lambda h, qi: (0, h)),
                  pl.BlockSpec((s_len, LANE), lambda h, qi: (0, 0)),
                  pl.BlockSpec((t, VHEAD), lambda h, qi: (qi, h)), col, pl.BlockSpec((t, VHEAD), lambda h, qi: (qi, h)),
                  table, table, table],
        out_specs=pl.BlockSpec((t, HEAD_PAD), lambda h, qi: (qi, h)),
        out_shape=jax.ShapeDtypeStruct((s_len, HEADS * HEAD_PAD), BF16),
        scratch_shapes=[pltpu.VMEM((t, HEAD_PAD), F32)],
        compiler_params=_params(("parallel", "arbitrary")),
    )(qb, kvb, krb, dob, lse, delta, *tables)


def _flash_bwd_dkv(qb, kvb, krb, dob, lse_row, delta_row):
    s_len = qb.shape[0]
    t = ATT_TILE
    nq = s_len // t
    scale = QK_HEAD ** -0.5

    def body(q_ref, kv_ref, kr_ref, do_ref, lse_ref, dl_ref, dkv_ref, dkr_ref, dk_s, dv_s):
        ki = pl.program_id(1)
        dk_s[...] = jnp.zeros_like(dk_s)
        dv_s[...] = jnp.zeros_like(dv_s)
        kv, kr = kv_ref[...], kr_ref[...]

        def products(qi, masked):
            rows = pl.ds(pl.multiple_of(qi * t, t), t)
            q, do = q_ref[rows, :], do_ref[rows, :]
            st, _ = _att_scores(q, kv, kr, LOG2_SCALE, masked, True)
            return st, lax.dot_general(kv[:, NOPE:], do, NT, preferred_element_type=F32), q, do, rows

        def update(st, dpt, q, do, rows):
            pt = jnp.exp2(st - lse_ref[:, rows])
            dv_s[...] += jnp.dot(pt.astype(BF16), do, preferred_element_type=F32)
            dst = (pt * (dpt - dl_ref[:, rows]) * scale).astype(BF16)
            dk_s[...] += jnp.dot(dst, q, preferred_element_type=F32)

        def pair(q0, q1, masked):
            first, second = products(q0, masked), products(q1, False)
            update(*first)
            update(*second)

        @pl.when(ki == nq - 1)
        def _():
            update(*products(ki, True))

        @pl.when(ki < nq - 1)
        def _():
            pair(ki, ki + 1, True)
            _in_pairs(ki + 2, nq, lambda a, b: pair(a, b, False), lambda qi: update(*products(qi, False)))

        dk = dk_s[...]
        dkv_ref[...] = jnp.concatenate([dk[:, :NOPE], dv_s[...]], axis=1).astype(dkv_ref.dtype)
        dkr_ref[...] = dk[:, NOPE:]

    row = pl.BlockSpec((None, 1, s_len), lambda h, ki: (h, 0, 0))
    return pl.pallas_call(
        body, name="flash_bwd_dkv", grid=(HEADS, nq),
        in_specs=[pl.BlockSpec((s_len, HEAD_PAD), lambda h, ki: (0, h)),
                  pl.BlockSpec((t, HEAD_PAD), lambda h, ki: (ki, h)),
                  pl.BlockSpec((t, LANE), lambda h, ki: (ki, 0)),
                  pl.BlockSpec((s_len, VHEAD), lambda h, ki: (0, h)), row, row],
        out_specs=[pl.BlockSpec((t, HEAD_PAD), lambda h, ki: (ki, h)),
                   pl.BlockSpec((t, LANE), lambda h, ki: (ki, h))],
        out_shape=[jax.ShapeDtypeStruct((s_len, HEADS * HEAD_PAD), BF16),
                   jax.ShapeDtypeStruct((s_len, HEADS * LANE), F32)],
        scratch_shapes=[pltpu.VMEM((t, HEAD_PAD), F32), pltpu.VMEM((t, VHEAD), F32)],
        compiler_params=_params(("parallel", "parallel")),
    )(qb, kvb, krb, dob, lse_row, delta_row)


def _tril(w):
    r = lax.broadcasted_iota(jnp.int32, w.shape, 0)
    c = lax.broadcasted_iota(jnp.int32, w.shape, 1)
    return jnp.where(c <= r, w, 0.0)


def _sgu_row_stats(z_ref, width, gd, v_s, act, extra_s=None):
    total = None
    for g in range(SGU_GROUPS):
        cols = slice(g * gd, (g + 1) * gd)
        v = act(z_ref[:, width + g * gd:width + (g + 1) * gd])
        if extra_s is not None:
            v, extra_s[:, cols] = v
        v_s[:, cols] = v
        part = jnp.sum(v, axis=1, keepdims=True)
        total = part if total is None else total + part
    mean = total * (1.0 / width)
    sq = None
    for g in range(SGU_GROUPS):
        d = v_s[:, g * gd:(g + 1) * gd] - mean
        part = jnp.sum(d * d, axis=1, keepdims=True)
        sq = part if sq is None else sq + part
    return mean, lax.rsqrt(sq * (1.0 / width) + LN_EPS)


def _sgu_fwd(zpre, ln_g, ln_b, w_sp, bias_full):
    s_len, two_w = zpre.shape
    width = two_w // 2
    gd = width // SGU_GROUPS
    t = SGU_CHUNK

    def body(z_ref, g_ref, b_ref, w_ref, bias_ref, uv_ref, v_s):
        mean, rstd = _sgu_row_stats(z_ref, width, gd, v_s, _gelu)
        for g in range(SGU_GROUPS):
            cols = slice(g * gd, (g + 1) * gd)
            vln = ((v_s[:, cols] - mean) * rstd * g_ref[:, cols] + b_ref[:, cols]).astype(BF16)
            mixed = jnp.dot(_tril(w_ref[g]).astype(BF16), vln, preferred_element_type=F32) + bias_ref[:, cols]
            uv_ref[:, cols] = (_gelu(z_ref[:, cols]) * mixed).astype(uv_ref.dtype)

    return pl.pallas_call(
        body, name="sgu_fwd", grid=(s_len // t,),
        in_specs=[pl.BlockSpec((t, two_w), lambda i: (i, 0)), pl.BlockSpec((1, width), lambda i: (0, 0)),
                  pl.BlockSpec((1, width), lambda i: (0, 0)), pl.BlockSpec(w_sp.shape, lambda i: (0, 0, 0)),
                  pl.BlockSpec((t, width), lambda i: (0, 0))],
        out_specs=pl.BlockSpec((t, width), lambda i: (i, 0)),
        out_shape=jax.ShapeDtypeStruct((s_len, width), BF16),
        scratch_shapes=[pltpu.VMEM((t, width), F32)],
        compiler_params=_params(("parallel",)),
    )(zpre, ln_g, ln_b, w_sp, bias_full)


def _sgu_bwd(zpre, duv, ln_g, ln_b, w_sp, bias_full):
    s_len, two_w = zpre.shape
    width = two_w // 2
    gd = width // SGU_GROUPS
    t = SGU_CHUNK

    def body(z_ref, duv_ref, g_ref, b_ref, w_ref, bias_ref, dz_ref, dg_ref, db_ref, dw_ref, dbias_ref, v_s, vgrad_s, dvhat_s):
        @pl.when(pl.program_id(0) == 0)
        def _():
            for ref in (dg_ref, db_ref, dw_ref, dbias_ref):
                ref[...] = jnp.zeros_like(ref)

        def accumulate(ref, val):
            ref[...] += val

        mean, rstd = _sgu_row_stats(z_ref, width, gd, v_s, _gelu_and_grad, vgrad_s)
        sum_dvhat = sum_dvhat_vhat = None
        for g in range(SGU_GROUPS):
            cols = slice(g * gd, (g + 1) * gd)
            vhat = (v_s[:, cols] - mean) * rstd
            vln = (vhat * g_ref[:, cols] + b_ref[:, cols]).astype(BF16)
            wc = _tril(w_ref[g]).astype(BF16)
            mixed = jnp.dot(wc, vln, preferred_element_type=F32) + bias_ref[:, cols]
            u, u_grad = _gelu_and_grad(z_ref[:, cols])
            duv = duv_ref[:, cols]
            dz_ref[:, cols] = (duv * mixed * u_grad).astype(dz_ref.dtype)
            dmixed = duv * u
            dmb = dmixed.astype(BF16)
            dvln = lax.dot_general(wc, dmb, TN, preferred_element_type=F32)
            accumulate(dw_ref.at[g], _tril(lax.dot_general(dmb, vln, NT, preferred_element_type=F32)))
            accumulate(dbias_ref.at[:, cols], dmixed)
            accumulate(dg_ref.at[:, cols], jnp.sum(dvln * vhat, axis=0, keepdims=True))
            accumulate(db_ref.at[:, cols], jnp.sum(dvln, axis=0, keepdims=True))
            dvhat = dvln * g_ref[:, cols]
            dvhat_s[:, cols] = dvhat
            parts = jnp.sum(dvhat, axis=1, keepdims=True), jnp.sum(dvhat * vhat, axis=1, keepdims=True)
            sum_dvhat = parts[0] if sum_dvhat is None else sum_dvhat + parts[0]
            sum_dvhat_vhat = parts[1] if sum_dvhat_vhat is None else sum_dvhat_vhat + parts[1]
        mean_dvhat, mean_dvhat_vhat = sum_dvhat * (1.0 / width), sum_dvhat_vhat * (1.0 / width)
        for g in range(SGU_GROUPS):
            cols = slice(g * gd, (g + 1) * gd)
            vhat = (v_s[:, cols] - mean) * rstd
            dv0 = rstd * (dvhat_s[:, cols] - mean_dvhat - vhat * mean_dvhat_vhat)
            dz_ref[:, width + g * gd:width + (g + 1) * gd] = (dv0 * vgrad_s[:, cols]).astype(dz_ref.dtype)

    vec = pl.BlockSpec((1, width), lambda i: (0, 0))
    return pl.pallas_call(
        body, name="sgu_bwd", grid=(s_len // t,),
        in_specs=[pl.BlockSpec((t, two_w), lambda i: (i, 0)), pl.BlockSpec((t, width), lambda i: (i, 0)), vec, vec,
                  pl.BlockSpec(w_sp.shape, lambda i: (0, 0, 0)), pl.BlockSpec((t, width), lambda i: (0, 0))],
        out_specs=[pl.BlockSpec((t, two_w), lambda i: (i, 0)), vec, vec,
                   pl.BlockSpec(w_sp.shape, lambda i: (0, 0, 0)), pl.BlockSpec((t, width), lambda i: (0, 0))],
        out_shape=[jax.ShapeDtypeStruct((s_len, two_w), BF16), jax.ShapeDtypeStruct((1, width), F32),
                   jax.ShapeDtypeStruct((1, width), F32), jax.ShapeDtypeStruct(w_sp.shape, F32),
                   jax.ShapeDtypeStruct((t, width), F32)],
        scratch_shapes=[pltpu.VMEM((t, width), F32)] * 3,
        compiler_params=_params(("arbitrary",)),
    )(zpre, duv, ln_g, ln_b, w_sp, bias_full)


def _rope_tables(positions):
    inv_freq = ROPE_THETA ** (-jnp.arange(0, ROPE, 2, dtype=F32) / ROPE)
    ang = positions.astype(F32)[:, None] * inv_freq
    cos, sin = jnp.cos(ang), jnp.sin(ang)
    z32, z64 = jnp.zeros_like(cos), jnp.zeros((cos.shape[0], LANE - ROPE), F32)
    return (jnp.concatenate([cos, cos, z64], axis=1), jnp.concatenate([z32, sin, z64], axis=1),
            jnp.concatenate([-sin, z32, z64], axis=1))


def _ffn_fwd(x, g, w_up3, w_down3):
    def sq_relu(acc):
        r = jnp.maximum(acc, 0.0)
        return r * r, 2.0 * r

    r, r_grad, h2 = _mm_stacked("ffn_up", x, w_up3, "col", (BF16, BF16), sq_relu, norm_g=g)
    x_out = _mm_stacked("ffn_down", r, w_down3, "row", (F32,), lambda acc, res: (acc + res,), [x])[0]
    return x_out, (x, h2, r, r_grad)


def _ffn_bwd(dx, dxb, saved, g, w_up3, w_down3):
    x, h2, r, r_grad = saved
    da = _mm_stacked_nt("ffn_down_dx", dxb, w_down3, "row", (BF16,),
                        lambda acc, rg: (acc * rg.astype(F32),), [r_grad])[0]
    g_down = _mm_tn_stacked("ffn_down_dw", r, dxb, w_down3.shape, "row")
    dx, dxb, dg = _mm_stacked_nt("ffn_up_dx", da, w_up3, "col", **_norm_bwd(x, g, dx))
    g_up = _mm_tn_stacked("ffn_up_dw", h2, da, w_up3.shape, "col")
    return dx, dxb, dg, g_up, g_down


def _norm_bwd(x, g, dres):
    def epilogue(dh, xv, rv, gv):
        dxv, dg = _rms_bwd(dh, xv, gv)
        return dxv + rv, dxv + rv, dg

    return dict(out_dtypes=(F32, BF16), epilogue=epilogue, extras=[x, dres], consts=[g], sums=[g.shape], tm=MM_TILE // 2)


def _dot(a, b, dims=NN):
    return lax.dot_general(a.astype(BF16), b.astype(BF16), dims, preferred_element_type=F32)


def _mla_fwd(x, g, wdkv, q_norm, kv_norm, wq, wkv, wo, tables):
    d = x.shape[1]

    def project(xv, cs, s1, s2, gv, wdkv_v, qg, kg, wq_v, wkv_v):
        h = _rms_fwd(xv, gv).astype(BF16)
        lv = _dot(h, wdkv_v)
        cqn = _rms_fwd(lv[:, :Q_RANK], qg).astype(BF16)
        ckvn = _rms_fwd(lv[:, Q_RANK:Q_RANK + KV_RANK], kg).astype(BF16)
        return (h, lv, cqn, ckvn, _rope(lv[:, Q_RANK + KV_RANK:], cs, s1, s2), _dot(cqn, wq_v), _dot(ckvn, wkv_v))

    h, lat, cqn, ckvn, krb, q, kvb = _rowwise(
        "mla_project", project, [x, *tables], [g, wdkv, q_norm, kv_norm, wq, wkv],
        [(d, BF16), (LAT_PAD, F32), (Q_RANK, BF16), (KV_RANK, BF16), (LANE, BF16), (wq.shape[1], F32), (wkv.shape[1], BF16)],
        tr=512)
    ob, lse, qb = _flash_fwd(q, kvb, krb, tables)
    x_mid = _mm("mla_o", ob, wo, (F32,), lambda acc, res: (acc + res,), [x])[0]
    return x_mid, (x, h, lat, cqn, ckvn, krb, qb, kvb, ob, lse)


def _mla_bwd(dx, dxb, saved, g, wdkv, q_norm, kv_norm, wq, wkv, wo, tables):
    x, h, lat, cqn, ckvn, krb, qb, kvb, ob, lse = saved
    s_len = x.shape[0]
    def with_delta(do, ov):
        prod = do * ov.astype(F32)
        sums = [jnp.broadcast_to(jnp.sum(prod[:, hd * VHEAD:(hd + 1) * VHEAD], axis=1, keepdims=True), (do.shape[0], VHEAD))
                for hd in range(HEADS)]
        lane = lax.broadcasted_iota(jnp.int32, sums[0].shape, 1)
        by_lane = sum(jnp.where(lane == hd, s, 0.0) for hd, s in enumerate(sums))
        return do, jnp.concatenate(sums, axis=1), by_lane

    dob, delta, delta_lanes = _mm("mla_o_dx", dxb, wo, (BF16, F32), with_delta, [ob], nt=True, tm=MM_TILE // 2,
                                  narrow=[(LANE, F32)])
    g_wo = _mm_tn("mla_o_dw", ob, dxb)[0]
    delta_row = delta_lanes[:, :HEADS].T.reshape(HEADS, 1, s_len)
    dqb = _flash_bwd_dq(qb, kvb, krb, dob, lse, delta, tables)
    dkvb, dkr = _flash_bwd_dkv(qb, kvb, krb, dob, lse.reshape(HEADS, 1, s_len), delta_row)
    def project_bwd(dq_v, dkv_v, dkr_v, lv, cqn_v, ckvn_v, h_v, xv, rv, cs, s1, s2, gv, qg, kg, wq_v, wkv_v, wdkv_v):
        dcq, dqg = _rms_bwd(_dot(dq_v, wq_v, NT), lv[:, :Q_RANK], qg)
        dckv, dkg = _rms_bwd(_dot(dkv_v, wkv_v, NT), lv[:, Q_RANK:Q_RANK + KV_RANK], kg)
        dkr_sum = dkr_v[:, :LANE]
        for hd in range(1, HEADS):
            dkr_sum = dkr_sum + dkr_v[:, hd * LANE:(hd + 1) * LANE]
        dlat = jnp.concatenate([dcq, dckv, _rope_t(dkr_sum, cs, s1, s2)], axis=1).astype(BF16)
        dxv, dg = _rms_bwd(_dot(dlat, wdkv_v, NT), xv, gv)
        return (dxv + rv, dxv + rv, _dot(cqn_v, dq_v, TN), _dot(ckvn_v, dkv_v, TN), _dot(h_v, dlat, TN), dqg, dkg, dg)

    dx, dxb, g_wq, g_wkv, g_wdkv, g_qn, g_kvn, dg = _rowwise(
        "mla_project_bwd", project_bwd, [dqb, dkvb, dkr, lat, cqn, ckvn, h, x, dx, *tables],
        [g, q_norm, kv_norm, wq, wkv, wdkv], [(x.shape[1], F32), (x.shape[1], BF16)],
        [wq.shape, wkv.shape, wdkv.shape, q_norm.shape, kv_norm.shape, g.shape], tr=256)
    return dx, dxb, dg, g_wdkv, g_qn, g_kvn, g_wq, g_wkv, g_wo


def _sgu_layer_fwd(x, g, w_in3, ln_g, ln_b, w_sp, bias_full, w_out3):
    zpre, h = _mm_stacked("sgu_in", x, w_in3, "col", norm_g=g)
    uv = _sgu_fwd(zpre, ln_g, ln_b, w_sp, bias_full)
    x_mid = _mm_stacked("sgu_out", uv, w_out3, "row", (F32,), lambda acc, res: (acc + res,), [x])[0]
    return x_mid, (x, h, zpre, uv)


def _sgu_layer_bwd(dx, dxb, saved, g, w_in3, ln_g, ln_b, w_sp, bias_full, w_out3):
    x, h, zpre, uv = saved
    duv = _mm_stacked_nt("sgu_out_dx", dxb, w_out3, "row")[0]
    g_out = _mm_tn_stacked("sgu_out_dw", uv, dxb, w_out3.shape, "row")
    dz, g_lng, g_lnb, g_wsp, g_bias = _sgu_bwd(zpre, duv, ln_g, ln_b, w_sp, bias_full)
    g_in = _mm_tn_stacked("sgu_in_dw", h, dz, w_in3.shape, "col")
    dx, dxb, dg = _mm_stacked_nt("sgu_in_dx", dz, w_in3, "col", **_norm_bwd(x, g, dx))
    return dx, dxb, dg, g_in, g_out, g_lng, g_lnb, g_wsp, g_bias


def _loss_head(x, target, g):
    d = x.shape[1]

    def fn(xv, tv, gv):
        err = _rms_fwd(xv, gv) - tv
        dxv, dg = _rms_bwd(err * (1.0 / d), xv, gv)
        return dxv, dxv, dg, jnp.sum(err * err, axis=0, keepdims=True)

    return _rowwise("loss_head", fn, [x, target], [g], [(d, F32), (d, BF16)], [g.shape, g.shape])


def _mixer_weights(i, stacks):
    by_rows = lambda a: a.reshape(N_SHARDS * a.shape[1], a.shape[2])
    by_cols = lambda a: a.transpose(1, 0, 2).reshape(a.shape[1], N_SHARDS * a.shape[2])
    if i % 2:
        w_in3, w_out3, ln_g, ln_b = stacks
        return w_in3, ln_g.reshape(1, -1), ln_b.reshape(1, -1), w_out3
    wdkv = by_rows(stacks[0])
    wdkv = jnp.pad(wdkv, ((0, 0), (0, LAT_PAD - wdkv.shape[1])))
    wq = jnp.pad(by_cols(stacks[1]).reshape(Q_RANK, HEADS, QK_HEAD), ((0, 0), (0, 0), (0, HEAD_PAD - QK_HEAD)))
    return wdkv, wq.reshape(Q_RANK, HEADS * HEAD_PAD), by_cols(stacks[2]), by_rows(stacks[3])


def _local_step(x, positions, target, norm_mix, norm_ffn, final_norm, q_norm, kv_norm, w_sp, b_sp, mixers, ffn, reducer):
    tables = _rope_tables(positions)
    gd = mixers[1][2].size // SGU_GROUPS
    bias_full = [jnp.repeat(b_sp[j].T, gd, axis=1) for j in range(DEPTH // 2)]
    saved, mla, sgu = [], [None] * (DEPTH // 2), [None] * (DEPTH // 2)
    for i in range(DEPTH):
        j = i // 2
        x, *stacks = lax.optimization_barrier((x, *mixers[i]))
        if i % 2 == 0:
            wdkv, wq, wkv, wo = mla[j] = _mixer_weights(i, stacks)
            x, s_mix = _mla_fwd(x, norm_mix[i:i + 1], wdkv, q_norm[j:j + 1], kv_norm[j:j + 1], wq, wkv, wo, tables)
        else:
            w_in3, ln_g, ln_b, w_out3 = sgu[j] = _mixer_weights(i, stacks)
            x, s_mix = _sgu_layer_fwd(x, norm_mix[i:i + 1], w_in3, ln_g, ln_b, w_sp[j], bias_full[j], w_out3)
        x, s_ffn = _ffn_fwd(x, norm_ffn[i:i + 1], *ffn[i])
        saved.append((s_mix, s_ffn))
    dx, dxb, g_final, sq_cols = _loss_head(x, target, final_norm[None, :])
    loss = 0.5 * jnp.sum(sq_cols) / x.shape[1]

    def pair(g):
        return g, g.astype(BF16)

    g_mix, g_ffn = [None] * DEPTH, [None] * DEPTH
    mla_g, sgu_g = [None] * (DEPTH // 2), [None] * (DEPTH // 2)
    for i in reversed(range(DEPTH)):
        j = i // 2
        s_mix, s_ffn = saved[i]
        dx, dxb, g_ffn[i], g_up, g_down = _ffn_bwd(dx, dxb, s_ffn, norm_ffn[i:i + 1], *ffn[i])
        dxb = reducer.add(f"ffn{i}", i, {"ffn_w_up": g_up, "ffn_w_down": g_down}, dxb)
        dxb = reducer.phase_end(dxb)
        if i % 2 == 0:
            wdkv, wq, wkv, wo = mla[j]
            dx, dxb, g_mix[i], g_wdkv, g_qn, g_kvn, g_wq, g_wkv, g_wo = _mla_bwd(
                dx, dxb, s_mix, norm_mix[i:i + 1], wdkv, q_norm[j:j + 1], kv_norm[j:j + 1], wq, wkv, wo, tables)
            mla_g[j] = (g_qn, g_kvn)
            g_wq = g_wq.reshape(Q_RANK, HEADS, HEAD_PAD)[..., :QK_HEAD].reshape(Q_RANK, N_SHARDS, -1)
            dxb = reducer.add(f"mla{j}", j, {
                "mla_w_dkv": pair(g_wdkv[:, :Q_RANK + KV_RANK + ROPE].reshape(N_SHARDS, -1, Q_RANK + KV_RANK + ROPE)),
                "mla_w_uq": pair(g_wq.transpose(1, 0, 2)),
                "mla_w_ukv": pair(g_wkv.reshape(KV_RANK, N_SHARDS, -1).transpose(1, 0, 2)),
                "mla_w_o": pair(g_wo.reshape(N_SHARDS, -1, g_wo.shape[1]))}, dxb)
        else:
            w_in3, ln_g, ln_b, w_out3 = sgu[j]
            dx, dxb, g_mix[i], g_in, g_out, g_lng, g_lnb, g_wsp, g_bias = _sgu_layer_bwd(
                dx, dxb, s_mix, norm_mix[i:i + 1], w_in3, ln_g, ln_b, w_sp[j], bias_full[j], w_out3)
            sgu_g[j] = (g_wsp, g_bias.reshape(SGU_CHUNK, SGU_GROUPS, gd).sum(axis=-1).T)
            dxb = reducer.add(f"sgu{j}", j, {"sgu_w_in": g_in, "sgu_w_out": g_out,
                                             "sgu_ln_g": pair(g_lng.reshape(N_SHARDS, -1, LANE)),
                                             "sgu_ln_b": pair(g_lnb.reshape(N_SHARDS, -1, LANE))}, dxb)
        dxb = reducer.phase_end(dxb)
    small = dict(
        norm_mix=jnp.concatenate(g_mix, axis=0), norm_ffn=jnp.concatenate(g_ffn, axis=0), final_norm=g_final[0],
        q_norm=jnp.concatenate([m[0] for m in mla_g], axis=0), kv_norm=jnp.concatenate([m[1] for m in mla_g], axis=0),
        w_sp=jnp.stack([s[0] for s in sgu_g]), b_sp=jnp.stack([s[1] for s in sgu_g]))
    return loss, dx, small


HBM_SPEC = pl.BlockSpec(memory_space=pltpu.HBM)


def _place():
    x, y, c = lax.axis_index("x"), lax.axis_index("y"), lax.axis_index("c")
    return x, y, c, [(1 - x, y), (x, 1 - y), (1 - x, 1 - y)]


def _remote(src, dst, send_sems, recv_sems, k, to):
    return pltpu.make_async_remote_copy(src_ref=src, dst_ref=dst, send_sem=send_sems.at[k], recv_sem=recv_sems.at[k],
                                        device_id=to, device_id_type=MESH)


def _gather_layer(tag, shards):
    n = len(shards)
    split = [s.shape[0] >= 16 for s in shards]

    def body(*refs):
        ins, outs = refs[:n], refs[n:2 * n]
        send_sems, recv_sems, local_sems = refs[2 * n:]
        x, y, c, chips = _place()
        mine = 2 * x + y
        barrier = pltpu.get_barrier_semaphore()
        peers = [(x, y, 1 - c)] + [(*chip, c) for chip in chips]
        for peer in peers:
            pl.semaphore_signal(barrier, inc=1, device_id=peer, device_id_type=MESH)
        pl.semaphore_wait(barrier, len(peers))

        def rows(t, half):
            hr = shards[t].shape[0] // 2
            return pl.ds(half * hr, hr) if split[t] else pl.ds(0, shards[t].shape[0])

        local, sent = [], []
        for t in range(n):
            local.append(pltpu.make_async_copy(ins[t], outs[t].at[mine], local_sems.at[t]))
            local[-1].start()
            for j, chip in enumerate(chips):
                cp = _remote(ins[t].at[rows(t, c)], outs[t].at[mine, rows(t, c)], send_sems, recv_sems, 3 * t + j, (*chip, c))
                cp.start()
                sent.append(cp)
        for j, chip in enumerate(chips):
            theirs = 2 * chip[0] + chip[1]
            for t in range(n):
                piece = outs[t].at[theirs, rows(t, c)]
                _remote(piece, piece, send_sems, recv_sems, 3 * t + j, (x, y, c)).wait_recv()
                if split[t]:
                    cp = _remote(piece, piece, send_sems, recv_sems, 3 * n + 3 * t + j, (x, y, 1 - c))
                    cp.start()
                    sent.append(cp)
        for j, chip in enumerate(chips):
            theirs = 2 * chip[0] + chip[1]
            for t in range(n):
                if split[t]:
                    piece = outs[t].at[theirs, rows(t, 1 - c)]
                    _remote(piece, piece, send_sems, recv_sems, 3 * n + 3 * t + j, (x, y, c)).wait_recv()
        for cp in sent:
            cp.wait_send()
        for cp in local:
            cp.wait()

    return pl.kernel(
        body, name=f"gather_{tag}", mesh=plsc.ScalarSubcoreMesh(axis_name="sequencer", num_cores=1),
        out_type=[jax.ShapeDtypeStruct((N_SHARDS, *s.shape), s.dtype) for s in shards],
        scratch_types=[pltpu.SemaphoreType.DMA((6 * n,)), pltpu.SemaphoreType.DMA((6 * n,)), pltpu.SemaphoreType.DMA((n,))],
        compiler_params=pltpu.CompilerParams(collective_id=ID_GATHER),
    )(*shards)


SEQUENCER = dict(axis_name="sequencer", num_cores=1)
ID_GATHER, ID_EXCHANGE, ID_SHARE = 0, 1, 2
MIN_SPLIT_ROWS = 16


def _handshake(peers):
    barrier = pltpu.get_barrier_semaphore()
    for peer in peers:
        pl.semaphore_signal(barrier, inc=1, device_id=peer, device_id_type=MESH)
    pl.semaphore_wait(barrier, len(peers))


def _half_rows(rows, half):
    return pl.ds(half * (rows // 2), rows // 2) if rows >= MIN_SPLIT_ROWS else pl.ds(0, rows)


SEM_SPEC = pl.BlockSpec(memory_space=pltpu.SEMAPHORE)
DATAFLOW = pltpu.SideEffectType.DATAFLOW_SIDE_EFFECTING


def _exchange_copies(shapes, stacks, lands, send_sems, recv_sems):
    x, y, c, chips = _place()
    mine = 2 * x + y
    copies = []
    for t, shape in enumerate(shapes):
        r = shape[1]
        copies.append(_remote(stacks[t].at[mine, _half_rows(r, 1 - c)], lands[t].at[0], send_sems, recv_sems, 7 * t, (x, y, 1 - c)))
        for j, chip in enumerate(chips):
            theirs = 2 * chip[0] + chip[1]
            copies.append(_remote(stacks[t].at[theirs, _half_rows(r, c)], lands[t].at[1 + j], send_sems, recv_sems,
                                  7 * t + 1 + j, (*chip, c)))
            copies.append(_remote(stacks[t].at[theirs, _half_rows(r, 1 - c)], lands[t].at[4 + j], send_sems, recv_sems,
                                  7 * t + 4 + j, (*chip, 1 - c)))
    return copies


def _exchange_start(tag, stacks, carry):
    n = len(stacks)
    shapes = [s.shape for s in stacks]
    lands = [lax.empty((7, s.shape[1] // 2 if s.shape[1] >= MIN_SPLIT_ROWS else s.shape[1], s.shape[2]), s.dtype) for s in stacks]

    def body(*refs):
        for cp in _exchange_copies(shapes, refs[:n], refs[n:2 * n], refs[2 * n + 1], refs[2 * n + 2]):
            cp.start()

    through = (*stacks, *lands, carry)
    out = pl.pallas_call(
        body, name=f"reduce_exchange_start_{tag}",
        out_shape=(pltpu.SemaphoreType.DMA((7 * n,)), pltpu.SemaphoreType.DMA((7 * n,)),
                   *[pltpu.HBM(a.shape, a.dtype) for a in through]),
        in_specs=[HBM_SPEC] * (2 * n + 1),
        out_specs=(SEM_SPEC, SEM_SPEC, *[HBM_SPEC] * (2 * n + 1)),
        input_output_aliases={t: 2 + t for t in range(2 * n + 1)},
        compiler_params=pltpu.CompilerParams(has_side_effects=DATAFLOW),
    )(*[pltpu.with_memory_space_constraint(a, pltpu.HBM) for a in through])
    return out[0], out[1], out[2:2 + n], out[2 + n:2 + 2 * n], out[-1]


def _exchange_wait(tag, send_sems, recv_sems, stacks, lands, after):
    n = len(stacks)
    shapes = [s.shape for s in stacks]

    def body(*refs):
        for cp in _exchange_copies(shapes, refs[:n], refs[n:2 * n], refs[2 * n], refs[2 * n + 1]):
            cp.wait()

    out = pl.pallas_call(
        body, name=f"reduce_exchange_wait_{tag}",
        out_shape=tuple(pltpu.HBM(a.shape, a.dtype) for a in (*stacks, *lands)),
        in_specs=[HBM_SPEC] * (2 * n) + [SEM_SPEC, SEM_SPEC, pl.BlockSpec(memory_space=pl.ANY)],
        out_specs=tuple([HBM_SPEC] * (2 * n)),
        input_output_aliases={t: t for t in range(2 * n)},
        compiler_params=pltpu.CompilerParams(has_side_effects=DATAFLOW),
    )(*stacks, *lands, send_sems, recv_sems, after)
    return out[:n], out[n:]


def _share_halves(tag, halves):
    n = len(halves)

    def body(*refs):
        ins, outs, send_sems, recv_sems = refs[:n], refs[n:2 * n], refs[2 * n], refs[2 * n + 1]
        x, y, c, _ = _place()
        _handshake([(x, y, 1 - c)])
        sent = [_remote(ins[t], outs[t], send_sems, recv_sems, t, (x, y, 1 - c)) for t in range(n)]
        for cp in sent:
            cp.start()
        for cp in sent:
            cp.wait()

    return pl.kernel(
        body, name=f"reduce_share_{tag}", mesh=plsc.ScalarSubcoreMesh(**SEQUENCER),
        out_type=[jax.ShapeDtypeStruct(h.shape, h.dtype) for h in halves],
        scratch_types=[pltpu.SemaphoreType.DMA((n,)), pltpu.SemaphoreType.DMA((n,))],
        compiler_params=pltpu.CompilerParams(collective_id=ID_SHARE),
    )(*halves)


def _all_reduce_small(part):
    rows = part.shape[0]
    half = rows // 2

    def body(p_ref, out_ref, sib_buf, chip_sums, send_sems, recv_sems):
        x, y, c, chips = _place()
        mine = 2 * x + y
        my_rows = pl.ds(pl.multiple_of(c * half, 8), half)
        swap = _remote(p_ref, sib_buf, send_sems, recv_sems, 0, (x, y, 1 - c))
        swap.start()
        swap.wait()
        chip_sums[mine] = p_ref[...] + sib_buf[...]
        sent = [_remote(chip_sums.at[mine, my_rows], chip_sums.at[mine, my_rows], send_sems, recv_sems, 1 + j, (*chip, c))
                for j, chip in enumerate(chips)]
        for cp in sent:
            cp.start()
        for j, chip in enumerate(chips):
            sent[j].wait_send()
            theirs = chip_sums.at[2 * chip[0] + chip[1], my_rows]
            _remote(theirs, theirs, send_sems, recv_sems, 1 + j, (x, y, c)).wait_recv()
        out_ref[my_rows, :] = ((chip_sums[0, my_rows, :] + chip_sums[1, my_rows, :]) + chip_sums[2, my_rows, :]) + chip_sums[3, my_rows, :]
        share = _remote(out_ref.at[my_rows], out_ref.at[my_rows], send_sems, recv_sems, 4, (x, y, 1 - c))
        share.start()
        share.wait_send()
        other = out_ref.at[pl.ds(pl.multiple_of((1 - c) * half, 8), half)]
        _remote(other, other, send_sems, recv_sems, 4, (x, y, c)).wait_recv()

    vmem = pl.BlockSpec(memory_space=pltpu.VMEM)
    return pl.pallas_call(
        body, name="all_reduce_small", in_specs=[vmem], out_specs=vmem, out_shape=jax.ShapeDtypeStruct(part.shape, F32),
        scratch_shapes=[pltpu.VMEM((rows, LANE), F32), pltpu.VMEM((N_SHARDS, rows, LANE), F32),
                        pltpu.SemaphoreType.DMA((5,)), pltpu.SemaphoreType.DMA((5,))],
        compiler_params=pltpu.CompilerParams(vmem_limit_bytes=VMEM_LIMIT_BYTES),
    )(part)


def _sum_partials(g3, others, sel):
    _, rows, c = others.shape
    whole = g3.shape[1] == rows
    tr = _tile(rows, UPDATE_ROWS)
    nb = rows // tr

    def body(sel_ref, g_ref, *rest):
        same = g_ref[...].astype(F32)
        for ref in rest[1:4]:
            same = same + ref[...].astype(F32)
        other = rest[0][...].astype(F32)
        for ref in rest[4:7]:
            other = other + ref[...].astype(F32)
        rest[7][...] = same + other

    blk = (None, tr, c)
    slots = [pl.BlockSpec(blk, functools.partial(lambda i, sr, k: (k, i, 0), k=k)) for k in range(7)]
    return pl.pallas_call(
        body, name="reduce_sum_partials",
        grid_spec=pltpu.PrefetchScalarGridSpec(
            num_scalar_prefetch=1, grid=(nb,),
            in_specs=[pl.BlockSpec(blk, lambda i, sr: (sr[0], (0 if whole else sr[1] * nb) + i, 0))] + slots,
            out_specs=pl.BlockSpec((tr, c), lambda i, sr: (i, 0))),
        out_shape=jax.ShapeDtypeStruct((rows, c), F32),
        compiler_params=_params(("parallel",)),
    )(sel, g3, *[others] * 7)


def _adamw_math(w, g, m, v):
    nm = ADAM_B1 * m + (1.0 - ADAM_B1) * g
    nv = ADAM_B2 * v + (1.0 - ADAM_B2) * (g * g)
    m_hat = nm / (1.0 - ADAM_B1 ** ADAM_STEP)
    v_hat = nv / (1.0 - ADAM_B2 ** ADAM_STEP)
    return -ADAM_LR * (m_hat / (jnp.sqrt(v_hat) + ADAM_EPS) + ADAM_WD * w), nm, nv


def _adamw_layer(layer, w, m, v, g_mine, g_sibling, sel, prev):
    lyr, r, c = w.shape
    rows = g_mine.shape[0]
    halves = r // rows
    tr = _tile(rows, UPDATE_ROWS)
    nb = rows // tr
    n_g = 1 if g_sibling is None else 2

    def body(sel_ref, w_ref, m_ref, v_ref, *rest):
        g = rest[0][...]
        if n_g == 2:
            g = jnp.where(pl.program_id(0) == sel_ref[1], g, rest[1][...])
        outs = rest[n_g + (0 if prev is None else 4):]
        d, nm, nv = _adamw_math(w_ref[...], g, m_ref[...], v_ref[...])
        for ref, val in zip(outs, (g, d, nm, nv)):
            ref[...] = val

    full = pl.BlockSpec((None, tr, c), lambda h, i, sr: (layer, h * nb + i, 0))
    part = pl.BlockSpec((tr, c), lambda h, i, sr: (i, 0))
    n_in = 4 + n_g
    return pl.pallas_call(
        body, name="adamw_layer",
        grid_spec=pltpu.PrefetchScalarGridSpec(
            num_scalar_prefetch=1, grid=(halves, nb),
            in_specs=[full] * 3 + [part] * n_g + ([] if prev is None else [pl.BlockSpec(memory_space=pl.ANY)] * 4),
            out_specs=[full] * 4),
        out_shape=[jax.ShapeDtypeStruct(w.shape, F32)] * 4,
        input_output_aliases={} if prev is None else {n_in + k: k for k in range(4)},
        compiler_params=_params(("parallel", "parallel")),
    )(sel, w, m, v, g_mine, *([] if g_sibling is None else [g_sibling]), *([] if prev is None else prev))


def _adamw(w, g, m, v):
    lyr, r, c = w.shape
    tr = _tile(r, 256)

    def body(w_ref, g_ref, m_ref, v_ref, d_ref, nm_ref, nv_ref):
        d_ref[...], nm_ref[...], nv_ref[...] = _adamw_math(w_ref[...], g_ref[...], m_ref[...], v_ref[...])

    blk = pl.BlockSpec((None, tr, c), lambda l, i: (l, i, 0))
    return pl.pallas_call(
        body, name="adamw", grid=(lyr, r // tr), in_specs=[blk] * 4, out_specs=[blk] * 3,
        out_shape=[jax.ShapeDtypeStruct(w.shape, F32)] * 3,
        compiler_params=_params(("parallel", "parallel")),
    )(w, g, m, v)


SHARDED = ("mla_w_dkv", "mla_w_uq", "mla_w_ukv", "mla_w_o", "sgu_w_in", "sgu_ln_g", "sgu_ln_b", "sgu_w_out",
           "ffn_w_up", "ffn_w_down")
REPLICATED = ("norm_mix", "norm_ffn", "final_norm", "mla_q_norm", "mla_kv_norm", "sgu_w_spatial", "sgu_b_spatial")
WEIGHTS = ("norm_mix", "norm_ffn", "final_norm", "mla_w_dkv", "mla_q_norm", "mla_kv_norm", "mla_w_uq", "mla_w_ukv",
           "mla_w_o", "sgu_w_in", "sgu_ln_g", "sgu_ln_b", "sgu_w_spatial", "sgu_b_spatial", "sgu_w_out", "ffn_w_up",
           "ffn_w_down")


class _Reducer:
    def __init__(self, state, sel):
        self.state, self.sel = state, sel
        self.started, self.travelling, self.summed = [], [], []
        self.done = {}

    def add(self, tag, layer, grads, token):
        names = list(grads)
        token, *tied = lax.optimization_barrier((token, *[a for n in names for a in grads[n]]))
        f32s, bf16s = tied[0::2], tied[1::2]
        *flying, token = _exchange_start(tag, bf16s, token)
        self.started.append((tag, layer, names, f32s, flying))
        return token

    def phase_end(self, token):
        for tag, layer, names, f32s, flying in self.travelling:
            bf16s, received = _exchange_wait(tag, *flying, token)
            own = [g if g.shape[1] >= MIN_SPLIT_ROWS else gb for g, gb in zip(f32s, bf16s)]
            mine = [_sum_partials(g, got, self.sel) for g, got in zip(own, received)]
            token, *mine = lax.optimization_barrier((token, *mine))
            cut = [k for k, g in enumerate(own) if g.shape[1] >= MIN_SPLIT_ROWS]
            theirs = dict(zip(cut, _share_halves(tag, [mine[k] for k in cut])))
            self.summed.append((layer, names, mine, [theirs.get(k) for k in range(len(names))]))
        self.travelling, self.started = self.started, []
        return token

    def update(self, token):
        for layer, names, mine, theirs in self.summed:
            for name, g_mine, g_theirs in zip(names, mine, theirs):
                w, m, v = self.state[name]
                self.done[name] = _adamw_layer(layer, w, m, v, g_mine, g_theirs, self.sel, self.done.get(name))
                token = self.done[name][1]
        self.summed = []
        return token


def _as3d(name, a):
    return a.reshape(a.shape[0], -1, LANE) if name in ("sgu_ln_g", "sgu_ln_b") else a


def _pack(parts):
    flat = jnp.concatenate([p.reshape(-1) for p in parts])
    rows = -(-flat.shape[0] // (256 * LANE)) * 256
    return jnp.pad(flat, (0, rows * LANE - flat.shape[0])).reshape(rows, LANE)


def _unpack(packed, like):
    flat, out, at = packed.reshape(-1), [], 0
    for p in like:
        out.append(flat[at:at + p.size].reshape(p.shape))
        at += p.size
    return out


def kernel(x, positions, norm_mix, norm_ffn, final_norm, mla_w_dkv, mla_q_norm, mla_kv_norm, mla_w_uq, mla_w_ukv, mla_w_o, sgu_w_in, sgu_ln_g, sgu_ln_b, sgu_w_spatial, sgu_b_spatial, sgu_w_out, ffn_w_up, ffn_w_down, loss_target, m_norm_mix, m_norm_ffn, m_final_norm, m_mla_w_dkv, m_mla_q_norm, m_mla_kv_norm, m_mla_w_uq, m_mla_w_ukv, m_mla_w_o, m_sgu_w_in, m_sgu_ln_g, m_sgu_ln_b, m_sgu_w_spatial, m_sgu_b_spatial, m_sgu_w_out, m_ffn_w_up, m_ffn_w_down, v_norm_mix, v_norm_ffn, v_final_norm, v_mla_w_dkv, v_mla_q_norm, v_mla_kv_norm, v_mla_w_uq, v_mla_w_ukv, v_mla_w_o, v_sgu_w_in, v_sgu_ln_g, v_sgu_ln_b, v_sgu_w_spatial, v_sgu_b_spatial, v_sgu_w_out, v_ffn_w_up, v_ffn_w_down):
    given = dict(locals())
    w = {n: given[n] for n in WEIGHTS}
    mom = {n: given["m_" + n] for n in WEIGHTS}
    var = {n: given["v_" + n] for n in WEIGHTS}
    mixers, ffn, token = [], [], None
    for i in range(DEPTH):
        j = i // 2
        if i % 2 == 0:
            mixer = [w[n][j].astype(BF16) for n in ("mla_w_dkv", "mla_w_uq", "mla_w_ukv", "mla_w_o")]
        else:
            mixer = [sgu_w_in[j].astype(BF16), sgu_w_out[j].astype(BF16), sgu_ln_g[j].reshape(-1, LANE),
                     sgu_ln_b[j].reshape(-1, LANE)]
        for tag, shards, into in ((f"mixer{i}", mixer, mixers), (f"ffn{i}", [ffn_w_up[i].astype(BF16), ffn_w_down[i].astype(BF16)], ffn)):
            if token is None:
                token = shards[0]
            else:
                token, *shards = lax.optimization_barrier((token, *shards))
            into.append(_gather_layer(tag, shards))

    x_i, y_i, c_i = lax.axis_index("x"), lax.axis_index("y"), lax.axis_index("c")
    sel = jnp.stack([2 * x_i + y_i, c_i]).astype(jnp.int32)
    reducer = _Reducer({n: tuple(_as3d(n, d[n]) for d in (w, mom, var)) for n in SHARDED}, sel)
    loss, dx, small = _local_step(
        x[0], positions[0], loss_target[0], norm_mix, norm_ffn, final_norm, mla_q_norm, mla_kv_norm, sgu_w_spatial,
        sgu_b_spatial, mixers, ffn, reducer)
    loss = lax.psum(loss, ("x", "y", "c"))

    small_g = [small["norm_mix"], small["norm_ffn"], small["final_norm"], small["q_norm"], small["kv_norm"],
               small["w_sp"], small["b_sp"]]
    like = [w[n] for n in REPLICATED]
    g_small = _all_reduce_small(_pack(small_g))
    packed = [_pack([d[n] for n in REPLICATED])[None] for d in (w, mom, var)]
    upd_small = _adamw(packed[0], g_small[None], packed[1], packed[2])
    grads = dict(zip(REPLICATED, _unpack(g_small, like)))
    delta, new_m, new_v = ({n: a for n, a in zip(REPLICATED, _unpack(u[0], like))} for u in upd_small)

    reducer.phase_end(reducer.update(upd_small[0]))
    reducer.update(None)
    for n in SHARDED:
        grads[n], delta[n], new_m[n], new_v[n] = (a.reshape(w[n].shape) for a in reducer.done[n])

    return (loss, dx[None], *[grads[n] for n in WEIGHTS], *[delta[n] for n in WEIGHTS],
            *[new_m[n] for n in WEIGHTS], *[new_v[n] for n in WEIGHTS])
```

```python
import functools
import math

import jax
import jax.numpy as jnp
from jax import lax
from jax.experimental import pallas as pl
from jax.experimental.pallas import tpu as pltpu
from jax.experimental.pallas import tpu_sc as plsc

F32 = jnp.float32
BF16 = jnp.bfloat16
MESH = pl.DeviceIdType.MESH

DEPTH = 4
HEADS = 8
NOPE = 128
ROPE = 64
VHEAD = 128
QK_HEAD = NOPE + ROPE
Q_RANK = 256
KV_RANK = 128
HEAD_PAD = 256
LAT_PAD = 512
ROPE_THETA = 10000.0
SGU_CHUNK = 128
SGU_GROUPS = 8
NORM_EPS = 1e-6
LN_EPS = 1e-5
ADAM_LR, ADAM_B1, ADAM_B2, ADAM_EPS, ADAM_WD, ADAM_STEP = 0.001, 0.9, 0.999, 1e-08, 0.01, 10

N_SHARDS = 4
LANE = 128
VMEM_LIMIT_BYTES = 56 * 1024 * 1024
ATT_TILE = 512
MM_TILE = 1024
UPDATE_ROWS = 128
ATT_SCALE = QK_HEAD ** -0.5
LOG2_SCALE = ATT_SCALE * math.log2(math.e)

NN = (((1,), (0,)), ((), ()))
NT = (((1,), (1,)), ((), ()))
TN = (((0,), (0,)), ((), ()))


def _params(sem):
    return pltpu.CompilerParams(dimension_semantics=sem, vmem_limit_bytes=VMEM_LIMIT_BYTES)


def _tile(n, pref):
    t = min(n, pref)
    while n % t:
        t //= 2
    return t


def _matmul(name, a, b, a_spec, b_spec, dims, grid, tile, outs, extras=(), epilogue=None, sums=(), norm_g=None):
    nk, ne, no = grid[2], len(extras), len(outs)
    b_specs = list(b_spec) if isinstance(b_spec, (list, tuple)) else [b_spec]
    nb = len(b_specs)
    normed = norm_g is not None
    assert not normed or (nk == 1 and nb == 1)

    def body(a_ref, *rest):
        b_refs, e_refs, o_refs = rest[:nb], rest[nb:nb + ne], rest[nb + ne + normed:nb + ne + normed + no]
        s_refs = rest[nb + ne + normed + no:nb + ne + normed + no + len(sums)]
        if normed:
            g_ref, h_ref, h_s = rest[nb + ne], rest[nb + ne + 1 + no + len(sums)], rest[-1]

            @pl.when(pl.program_id(1) == 0)
            def _():
                h_s[...] = _rms_fwd(a_ref[...], g_ref[...]).astype(BF16)
                h_ref[...] = h_s[...]

            a_ref = h_s
        kw = a_ref.shape[1] // nb
        part = None
        for p, b_ref in enumerate(b_refs):
            a_tile = a_ref[...] if nb == 1 else a_ref[:, p * kw:(p + 1) * kw]
            d = lax.dot_general(a_tile.astype(BF16), b_ref[...].astype(BF16), dims, preferred_element_type=F32)
            part = d if part is None else part + d

        def finish(acc):
            vals = (acc,) if epilogue is None else epilogue(acc, *[e[...] for e in e_refs])
            for o_ref, v in zip(o_refs, vals):
                o_ref[...] = v.astype(o_ref.dtype)
            first = pl.program_id(0) == 0
            for s_ref, v in zip(s_refs, vals[no:]):
                @pl.when(first)
                def _():
                    s_ref[...] = v

                @pl.when(jnp.logical_not(first))
                def _():
                    s_ref[...] += v

        if nk == 1:
            finish(part)
            return
        acc_ref, k = rest[-1], pl.program_id(2)

        @pl.when(k == 0)
        def _():
            acc_ref[...] = part

        @pl.when(jnp.logical_and(k > 0, k < nk - 1))
        def _():
            acc_ref[...] += part

        @pl.when(k == nk - 1)
        def _():
            finish(acc_ref[...] + part)

    assert not sums or (grid[1] == 1 and nk == 1)
    a_block = a_spec.block_shape
    return pl.pallas_call(
        body, name=name, grid=grid,
        in_specs=[a_spec] + b_specs + [s for _, s in extras]
        + ([pl.BlockSpec(norm_g.shape, lambda i, j, k: (0, 0))] if normed else []),
        out_specs=[s for _, s in outs] + [pl.BlockSpec(s, lambda i, j, k: (0,) * len(s)) for s in sums]
        + ([pl.BlockSpec(a_block, lambda i, j, k: (i, 0))] if normed else []),
        out_shape=[s for s, _ in outs] + [jax.ShapeDtypeStruct(s, F32) for s in sums]
        + ([jax.ShapeDtypeStruct(a.shape, BF16)] if normed else []),
        scratch_shapes=([pltpu.VMEM(tile, F32)] if nk > 1 else []) + ([pltpu.VMEM(a_block, BF16)] if normed else []),
        compiler_params=_params(("arbitrary" if sums else "parallel", "arbitrary" if normed else "parallel", "arbitrary")),
    )(a, *[b] * nb, *[e for e, _ in extras], *([norm_g] if normed else []))


def _epilogue_operands(extras, consts, o_spec):
    return [(e, o_spec) for e in extras] + [(c, pl.BlockSpec(c.shape, lambda i, j, k: (0, 0))) for c in consts]


def _mm(name, a, b, out_dtypes=(F32,), epilogue=None, extras=(), tm=MM_TILE, tn=MM_TILE, tk=MM_TILE, nt=False,
        consts=(), sums=(), narrow=()):
    m, kd = a.shape
    n = b.shape[0] if nt else b.shape[1]
    tm, tn, tk = _tile(m, tm), _tile(n, tn), _tile(kd, tk)
    o_spec = pl.BlockSpec((tm, tn), lambda i, j, k: (i, j))
    b_spec = pl.BlockSpec((tn, tk), lambda i, j, k: (j, k)) if nt else pl.BlockSpec((tk, tn), lambda i, j, k: (k, j))
    assert not narrow or n == tn
    outs = [(jax.ShapeDtypeStruct((m, n), d), o_spec) for d in out_dtypes]
    outs += [(jax.ShapeDtypeStruct((m, w), d), pl.BlockSpec((tm, w), lambda i, j, k: (i, 0))) for w, d in narrow]
    return _matmul(name, a, b, pl.BlockSpec((tm, tk), lambda i, j, k: (i, k)), b_spec, NT if nt else NN,
                   (m // tm, n // tn, kd // tk), (tm, tn), outs,
                   _epilogue_operands(extras, consts, o_spec), epilogue, sums)


def _mm_tn(name, a, b, out_dtypes=(F32,), tm=MM_TILE, tn=MM_TILE, tk=MM_TILE):
    s, m = a.shape
    n = b.shape[1]
    tm, tn, tk = _tile(m, tm), _tile(n, tn), _tile(s, tk)
    o_spec = pl.BlockSpec((tm, tn), lambda i, j, k: (i, j))
    return _matmul(name, a, b, pl.BlockSpec((tk, tm), lambda i, j, k: (k, i)),
                   pl.BlockSpec((tk, tn), lambda i, j, k: (k, j)), TN, (m // tm, n // tn, s // tk), (tm, tn),
                   [(jax.ShapeDtypeStruct((m, n), d), o_spec) for d in out_dtypes])


def _mm_stacked(name, a, w3, mode, out_dtypes=(F32,), epilogue=None, extras=(), tm=MM_TILE, tn=MM_TILE, tk=MM_TILE,
                norm_g=None):
    m, kd = a.shape
    _, r, c = w3.shape
    n = c if mode == "row" else N_SHARDS * c
    if mode == "row":
        tm, tn, tk = _tile(m, tm // 2), _tile(n, tn), kd
        b_spec = [pl.BlockSpec((None, r, tn), functools.partial(lambda i, j, k, p: (p, 0, j), p=p)) for p in range(N_SHARDS)]
    else:
        tm, tn, tk = _tile(m, tm), _tile(c, tn), _tile(kd, tk)
        per = c // tn
        b_spec = pl.BlockSpec((None, tk, tn), lambda i, j, k: (j // per, k, j % per))
    o_spec = pl.BlockSpec((tm, tn), lambda i, j, k: (i, j))
    return _matmul(name, a, w3, pl.BlockSpec((tm, tk), lambda i, j, k: (i, k)), b_spec, NN,
                   (m // tm, n // tn, kd // tk), (tm, tn),
                   [(jax.ShapeDtypeStruct((m, n), d), o_spec) for d in out_dtypes],
                   [(e, o_spec) for e in extras], epilogue, norm_g=norm_g)


def _mm_stacked_nt(name, a, w3, mode, out_dtypes=(F32,), epilogue=None, extras=(), tm=MM_TILE, tn=MM_TILE, tk=MM_TILE,
                   consts=(), sums=()):
    m, nd = a.shape
    _, r, c = w3.shape
    kout = N_SHARDS * r if mode == "row" else r
    if mode == "row":
        tm, tn, tk = _tile(m, tm), _tile(r, tn), _tile(c, tk)
        per = r // tn
        b_spec = pl.BlockSpec((None, tn, tk), lambda i, j, k: (j // per, j % per, k))
    else:
        tm, tn, tk = _tile(m, tm // 2), _tile(r, tn), nd
        b_spec = [pl.BlockSpec((None, tn, c), functools.partial(lambda i, j, k, p: (p, j, 0), p=p)) for p in range(N_SHARDS)]
    o_spec = pl.BlockSpec((tm, tn), lambda i, j, k: (i, j))
    return _matmul(name, a, w3, pl.BlockSpec((tm, tk), lambda i, j, k: (i, k)), b_spec, NT,
                   (m // tm, kout // tn, nd // tk), (tm, tn),
                   [(jax.ShapeDtypeStruct((m, kout), d), o_spec) for d in out_dtypes],
                   _epilogue_operands(extras, consts, o_spec), epilogue, sums)


def _mm_tn_stacked(name, a, b, shape3, mode, tm=MM_TILE, tn=MM_TILE, tk=MM_TILE):
    s, m = a.shape
    n = b.shape[1]
    _, r, c = shape3
    tk, tn = s, tn // 2
    if mode == "row":
        tm, tn = _tile(r, tm), _tile(n, tn)
        per = r // tm
        o_spec = pl.BlockSpec((None, tm, tn), lambda i, j, k: (i // per, i % per, j))
    else:
        tm, tn = _tile(m, tm), _tile(c, tn)
        per = c // tn
        o_spec = pl.BlockSpec((None, tm, tn), lambda i, j, k: (j // per, i, j % per))
    outs = [(jax.ShapeDtypeStruct(shape3, F32), o_spec), (jax.ShapeDtypeStruct(shape3, BF16), o_spec)]
    return _matmul(name, a, b, pl.BlockSpec((tk, tm), lambda i, j, k: (k, i)),
                   pl.BlockSpec((tk, tn), lambda i, j, k: (k, j)), TN, (m // tm, n // tn, s // tk), (tm, tn),
                   outs, epilogue=lambda acc: (acc, acc))


def _rowwise(name, fn, rows, consts, out_rows, out_accs=(), tr=256):
    nr, nc, no = len(rows), len(consts), len(out_rows)
    n_rows = rows[0].shape[0]
    tr = _tile(n_rows, tr)

    def body(*refs):
        vals = fn(*[r[...] for r in refs[:nr + nc]])
        o_refs, a_refs = refs[nr + nc:nr + nc + no], refs[nr + nc + no:]
        for ref, v in zip(o_refs, vals[:no]):
            ref[...] = v.astype(ref.dtype)
        first = pl.program_id(0) == 0

        @pl.when(first)
        def _():
            for ref, v in zip(a_refs, vals[no:]):
                ref[...] = v

        @pl.when(jnp.logical_not(first))
        def _():
            for ref, v in zip(a_refs, vals[no:]):
                ref[...] += v

    def whole(shape):
        return pl.BlockSpec(shape, lambda i: (0,) * len(shape))

    return pl.pallas_call(
        body, name=name, grid=(n_rows // tr,),
        in_specs=[pl.BlockSpec((tr, a.shape[1]), lambda i: (i, 0)) for a in rows] + [whole(c.shape) for c in consts],
        out_specs=[pl.BlockSpec((tr, f), lambda i: (i, 0)) for f, _ in out_rows] + [whole(s) for s in out_accs],
        out_shape=[jax.ShapeDtypeStruct((n_rows, f), d) for f, d in out_rows]
        + [jax.ShapeDtypeStruct(s, F32) for s in out_accs],
        compiler_params=_params(("arbitrary",)),
    )(*rows, *consts)


def _rms_fwd(x, g):
    return x * lax.rsqrt(jnp.mean(x * x, axis=-1, keepdims=True) + NORM_EPS) * g


def _rms_bwd(dy, x, g):
    rstd = lax.rsqrt(jnp.mean(x * x, axis=-1, keepdims=True) + NORM_EPS)
    n = x * rstd
    dn = dy * g
    dx = rstd * (dn - n * jnp.mean(dn * n, axis=-1, keepdims=True))
    return dx, jnp.sum(dy * n, axis=0, keepdims=True)


def _rope(x, cs, s1, s2):
    return x * cs + pltpu.roll(x, 32, 1) * s1 + pltpu.roll(x, 96, 1) * s2


def _rope_t(dy, cs, s1, s2):
    return dy * cs + pltpu.roll(dy * s1, 96, 1) + pltpu.roll(dy * s2, 32, 1)


def _gelu(z):
    return 0.5 * z * (1.0 + lax.erf(z * (1.0 / math.sqrt(2.0))))


def _gelu_and_grad(z):
    cdf = 0.5 * (1.0 + lax.erf(z * (1.0 / math.sqrt(2.0))))
    return z * cdf, cdf + z * jnp.exp(-0.5 * z * z) * (1.0 / math.sqrt(2.0 * math.pi))


def _att_scores(q, kv, kr, scale, masked, transposed):
    k = jnp.concatenate([kv[:, :NOPE], kr], axis=1)
    if transposed:
        s = lax.dot_general(k, q, NT, preferred_element_type=F32) * scale
    else:
        s = lax.dot_general(q, k, NT, preferred_element_type=F32) * scale
    if masked:
        r = lax.broadcasted_iota(jnp.int32, s.shape, 0)
        c = lax.broadcasted_iota(jnp.int32, s.shape, 1)
        s = jnp.where((r <= c) if transposed else (c <= r), s, -jnp.inf)
    return s, k


def _in_pairs(lo, hi, pair, single):
    n = hi - lo

    def body(p, carry):
        pair(lo + 2 * p, lo + 2 * p + 1)
        return carry

    lax.fori_loop(0, n // 2, body, 0)

    @pl.when(n % 2 == 1)
    def _():
        single(hi - 1)


def _causal_tiles(i, pair, single):
    @pl.when(i == 0)
    def _():
        single(i, True)

    @pl.when(i > 0)
    def _():
        _in_pairs(0, i - 1, lambda a, b: pair(a, b, False), lambda a: single(a, False))
        pair(i - 1, i, True)


def _flash_fwd(q, kvb, krb, tables):
    s_len = q.shape[0]
    t = ATT_TILE

    def body(q_ref, cs_ref, s1_ref, s2_ref, kv_ref, kr_ref, o_ref, lse_ref, qb_ref, m_s, l_s, acc_s):
        qi = pl.program_id(1)
        m_s[...] = jnp.full_like(m_s, -jnp.inf)
        l_s[...] = jnp.zeros_like(l_s)
        acc_s[...] = jnp.zeros_like(acc_s)
        qv = q_ref[...]
        q = jnp.concatenate([qv[:, :NOPE], _rope(qv[:, NOPE:], cs_ref[...], s1_ref[...], s2_ref[...])], axis=1).astype(BF16)
        qb_ref[...] = q

        def scores(ki, masked):
            rows = pl.ds(pl.multiple_of(ki * t, t), t)
            kv = kv_ref[rows, :]
            return _att_scores(q, kv, kr_ref[rows, :], LOG2_SCALE, masked, False)[0], kv

        def update(s, kv):
            m_prev = m_s[...]
            m_new = jnp.maximum(m_prev, jnp.max(s, axis=1, keepdims=True))
            alpha = jnp.exp2(m_prev - m_new)
            p = jnp.exp2(s - jnp.tile(m_new, (1, t // LANE)))
            l_s[...] = alpha * l_s[...] + jnp.sum(p, axis=1, keepdims=True)
            acc_s[...] = alpha * acc_s[...] + jnp.dot(p.astype(BF16), kv[:, NOPE:], preferred_element_type=F32)
            m_s[...] = m_new

        def pair(k0, k1, masked):
            first, second = scores(k0, False), scores(k1, masked)
            update(*first)
            update(*second)

        _causal_tiles(qi, pair, lambda ki, masked: update(*scores(ki, masked)))
        o_ref[...] = (acc_s[...] / l_s[...]).astype(o_ref.dtype)
        lse_ref[...] = (m_s[...] + jnp.log2(l_s[...]))[:, :1]

    table = pl.BlockSpec((t, LANE), lambda h, qi: (qi, 0))
    return pl.pallas_call(
        body, name="flash_fwd", grid=(HEADS, s_len // t),
        in_specs=[pl.BlockSpec((t, HEAD_PAD), lambda h, qi: (qi, h)), table, table, table,
                  pl.BlockSpec((s_len, HEAD_PAD), lambda h, qi: (0, h)),
                  pl.BlockSpec((s_len, LANE), lambda h, qi: (0, 0))],
        out_specs=[pl.BlockSpec((t, VHEAD), lambda h, qi: (qi, h)),
                   pl.BlockSpec((None, t, 1), lambda h, qi: (h, qi, 0)),
                   pl.BlockSpec((t, HEAD_PAD), lambda h, qi: (qi, h))],
        out_shape=[jax.ShapeDtypeStruct((s_len, HEADS * VHEAD), BF16),
                   jax.ShapeDtypeStruct((HEADS, s_len, 1), F32),
                   jax.ShapeDtypeStruct((s_len, HEADS * HEAD_PAD), BF16)],
        scratch_shapes=[pltpu.VMEM((t, LANE), F32), pltpu.VMEM((t, LANE), F32), pltpu.VMEM((t, VHEAD), F32)],
        compiler_params=_params(("parallel", "arbitrary")),
    )(q, *tables, kvb, krb)


def _flash_bwd(qb, kvb, krb, dob, lse_row, delta_row):
    s_len = qb.shape[0]
    t = ATT_TILE
    nq = s_len // t
    scale = QK_HEAD ** -0.5

    def body(q_ref, kv_ref, kr_ref, do_ref, lse_ref, dl_ref, dq_ref, dkv_ref, dkr_ref, dk_s, dv_s):
        ki = pl.program_id(1)

        @pl.when(ki == 0)
        def _():
            dq_ref[...] = jnp.zeros_like(dq_ref)

        dk_s[...] = jnp.zeros_like(dk_s)
        dv_s[...] = jnp.zeros_like(dv_s)
        kv, kr = kv_ref[...], kr_ref[...]

        def products(qi, masked):
            rows = pl.ds(pl.multiple_of(qi * t, t), t)
            q, do = q_ref[rows, :], do_ref[rows, :]
            st, k = _att_scores(q, kv, kr, LOG2_SCALE, masked, True)
            return st, lax.dot_general(kv[:, NOPE:], do, NT, preferred_element_type=F32), q, do, rows, k

        def update(st, dpt, q, do, rows, k):
            pt = jnp.exp2(st - lse_ref[:, rows])
            dv_s[...] += jnp.dot(pt.astype(BF16), do, preferred_element_type=F32)
            dst = (pt * (dpt - dl_ref[:, rows]) * scale).astype(BF16)
            dk_s[...] += jnp.dot(dst, q, preferred_element_type=F32)
            dq_ref[rows, :] += lax.dot_general(dst, k, TN, preferred_element_type=F32)

        def pair(q0, q1, masked):
            first, second = products(q0, masked), products(q1, False)
            update(*first)
            update(*second)

        @pl.when(ki == nq - 1)
        def _():
            update(*products(ki, True))

        @pl.when(ki < nq - 1)
        def _():
            pair(ki, ki + 1, True)
            _in_pairs(ki + 2, nq, lambda a, b: pair(a, b, False), lambda qi: update(*products(qi, False)))

        dk = dk_s[...]
        dkv_ref[...] = jnp.concatenate([dk[:, :NOPE], dv_s[...]], axis=1).astype(dkv_ref.dtype)
        dkr_ref[...] = dk[:, NOPE:]

    row = pl.BlockSpec((None, 1, s_len), lambda h, ki: (h, 0, 0))
    return pl.pallas_call(
        body, name="flash_bwd", grid=(HEADS, nq),
        in_specs=[pl.BlockSpec((s_len, HEAD_PAD), lambda h, ki: (0, h)),
                  pl.BlockSpec((t, HEAD_PAD), lambda h, ki: (ki, h)),
                  pl.BlockSpec((t, LANE), lambda h, ki: (ki, 0)),
                  pl.BlockSpec((s_len, VHEAD), lambda h, ki: (0, h)), row, row],
        out_specs=[pl.BlockSpec((s_len, HEAD_PAD), lambda h, ki: (0, h)),
                   pl.BlockSpec((t, HEAD_PAD), lambda h, ki: (ki, h)),
                   pl.BlockSpec((t, LANE), lambda h, ki: (ki, h))],
        out_shape=[jax.ShapeDtypeStruct((s_len, HEADS * HEAD_PAD), F32),
                   jax.ShapeDtypeStruct((s_len, HEADS * HEAD_PAD), BF16),
                   jax.ShapeDtypeStruct((s_len, HEADS * LANE), F32)],
        scratch_shapes=[pltpu.VMEM((t, HEAD_PAD), F32), pltpu.VMEM((t, VHEAD), F32)],
        compiler_params=_params(("parallel", "arbitrary")),
    )(qb, kvb, krb, dob, lse_row, delta_row)


def _tril(w):
    r = lax.broadcasted_iota(jnp.int32, w.shape, 0)
    c = lax.broadcasted_iota(jnp.int32, w.shape, 1)
    return jnp.where(c <= r, w, 0.0)


def _sgu_row_stats(z_ref, width, gd, v_s, act, extra_s=None):
    total = None
    for g in range(SGU_GROUPS):
        cols = slice(g * gd, (g + 1) * gd)
        v = act(z_ref[:, width + g * gd:width + (g + 1) * gd])
        if extra_s is not None:
            v, extra_s[:, cols] = v
        v_s[:, cols] = v
        part = jnp.sum(v, axis=1, keepdims=True)
        total = part if total is None else total + part
    mean = total * (1.0 / width)
    sq = None
    for g in range(SGU_GROUPS):
        d = v_s[:, g * gd:(g + 1) * gd] - mean
        part = jnp.sum(d * d, axis=1, keepdims=True)
        sq = part if sq is None else sq + part
    return mean, lax.rsqrt(sq * (1.0 / width) + LN_EPS)


def _sgu_fwd(zpre, ln_g, ln_b, w_sp, bias_full):
    s_len, two_w = zpre.shape
    width = two_w // 2
    gd = width // SGU_GROUPS
    t = SGU_CHUNK

    def body(z_ref, g_ref, b_ref, w_ref, bias_ref, uv_ref, v_s):
        mean, rstd = _sgu_row_stats(z_ref, width, gd, v_s, _gelu)
        for g in range(SGU_GROUPS):
            cols = slice(g * gd, (g + 1) * gd)
            vln = ((v_s[:, cols] - mean) * rstd * g_ref[:, cols] + b_ref[:, cols]).astype(BF16)
            mixed = jnp.dot(_tril(w_ref[g]).astype(BF16), vln, preferred_element_type=F32) + bias_ref[:, cols]
            uv_ref[:, cols] = (_gelu(z_ref[:, cols]) * mixed).astype(uv_ref.dtype)

    return pl.pallas_call(
        body, name="sgu_fwd", grid=(s_len // t,),
        in_specs=[pl.BlockSpec((t, two_w), lambda i: (i, 0)), pl.BlockSpec((1, width), lambda i: (0, 0)),
                  pl.BlockSpec((1, width), lambda i: (0, 0)), pl.BlockSpec(w_sp.shape, lambda i: (0, 0, 0)),
                  pl.BlockSpec((t, width), lambda i: (0, 0))],
        out_specs=pl.BlockSpec((t, width), lambda i: (i, 0)),
        out_shape=jax.ShapeDtypeStruct((s_len, width), BF16),
        scratch_shapes=[pltpu.VMEM((t, width), F32)],
        compiler_params=_params(("parallel",)),
    )(zpre, ln_g, ln_b, w_sp, bias_full)


def _sgu_bwd(zpre, duv, ln_g, ln_b, w_sp, bias_full):
    s_len, two_w = zpre.shape
    width = two_w // 2
    gd = width // SGU_GROUPS
    t = SGU_CHUNK

    def body(z_ref, duv_ref, g_ref, b_ref, w_ref, bias_ref, dz_ref, dg_ref, db_ref, dw_ref, dbias_ref, v_s, vgrad_s, dvhat_s):
        @pl.when(pl.program_id(0) == 0)
        def _():
            for ref in (dg_ref, db_ref, dw_ref, dbias_ref):
                ref[...] = jnp.zeros_like(ref)

        def accumulate(ref, val):
            ref[...] += val

        mean, rstd = _sgu_row_stats(z_ref, width, gd, v_s, _gelu_and_grad, vgrad_s)
        sum_dvhat = sum_dvhat_vhat = None
        for g in range(SGU_GROUPS):
            cols = slice(g * gd, (g + 1) * gd)
            vhat = (v_s[:, cols] - mean) * rstd
            vln = (vhat * g_ref[:, cols] + b_ref[:, cols]).astype(BF16)
            wc = _tril(w_ref[g]).astype(BF16)
            mixed = jnp.dot(wc, vln, preferred_element_type=F32) + bias_ref[:, cols]
            u, u_grad = _gelu_and_grad(z_ref[:, cols])
            duv = duv_ref[:, cols]
            dz_ref[:, cols] = (duv * mixed * u_grad).astype(dz_ref.dtype)
            dmixed = duv * u
            dmb = dmixed.astype(BF16)
            dvln = lax.dot_general(wc, dmb, TN, preferred_element_type=F32)
            accumulate(dw_ref.at[g], _tril(lax.dot_general(dmb, vln, NT, preferred_element_type=F32)))
            accumulate(dbias_ref.at[:, cols], dmixed)
            accumulate(dg_ref.at[:, cols], jnp.sum(dvln * vhat, axis=0, keepdims=True))
            accumulate(db_ref.at[:, cols], jnp.sum(dvln, axis=0, keepdims=True))
            dvhat = dvln * g_ref[:, cols]
            dvhat_s[:, cols] = dvhat
            parts = jnp.sum(dvhat, axis=1, keepdims=True), jnp.sum(dvhat * vhat, axis=1, keepdims=True)
            sum_dvhat = parts[0] if sum_dvhat is None else sum_dvhat + parts[0]
            sum_dvhat_vhat = parts[1] if sum_dvhat_vhat is None else sum_dvhat_vhat + parts[1]
        mean_dvhat, mean_dvhat_vhat = sum_dvhat * (1.0 / width), sum_dvhat_vhat * (1.0 / width)
        for g in range(SGU_GROUPS):
            cols = slice(g * gd, (g + 1) * gd)
            vhat = (v_s[:, cols] - mean) * rstd
            dv0 = rstd * (dvhat_s[:, cols] - mean_dvhat - vhat * mean_dvhat_vhat)
            dz_ref[:, width + g * gd:width + (g + 1) * gd] = (dv0 * vgrad_s[:, cols]).astype(dz_ref.dtype)

    vec = pl.BlockSpec((1, width), lambda i: (0, 0))
    return pl.pallas_call(
        body, name="sgu_bwd", grid=(s_len // t,),
        in_specs=[pl.BlockSpec((t, two_w), lambda i: (i, 0)), pl.BlockSpec((t, width), lambda i: (i, 0)), vec, vec,
                  pl.BlockSpec(w_sp.shape, lambda i: (0, 0, 0)), pl.BlockSpec((t, width), lambda i: (0, 0))],
        out_specs=[pl.BlockSpec((t, two_w), lambda i: (i, 0)), vec, vec,
                   pl.BlockSpec(w_sp.shape, lambda i: (0, 0, 0)), pl.BlockSpec((t, width), lambda i: (0, 0))],
        out_shape=[jax.ShapeDtypeStruct((s_len, two_w), BF16), jax.ShapeDtypeStruct((1, width), F32),
                   jax.ShapeDtypeStruct((1, width), F32), jax.ShapeDtypeStruct(w_sp.shape, F32),
                   jax.ShapeDtypeStruct((t, width), F32)],
        scratch_shapes=[pltpu.VMEM((t, width), F32)] * 3,
        compiler_params=_params(("arbitrary",)),
    )(zpre, duv, ln_g, ln_b, w_sp, bias_full)


def _rope_tables(positions):
    inv_freq = ROPE_THETA ** (-jnp.arange(0, ROPE, 2, dtype=F32) / ROPE)
    ang = positions.astype(F32)[:, None] * inv_freq
    cos, sin = jnp.cos(ang), jnp.sin(ang)
    z32, z64 = jnp.zeros_like(cos), jnp.zeros((cos.shape[0], LANE - ROPE), F32)
    return (jnp.concatenate([cos, cos, z64], axis=1), jnp.concatenate([z32, sin, z64], axis=1),
            jnp.concatenate([-sin, z32, z64], axis=1))


def _ffn_fwd(x, g, w_up3, w_down3):
    def sq_relu(acc):
        r = jnp.maximum(acc, 0.0)
        return r * r, 2.0 * r

    r, r_grad, h2 = _mm_stacked("ffn_up", x, w_up3, "col", (BF16, BF16), sq_relu, norm_g=g)
    x_out = _mm_stacked("ffn_down", r, w_down3, "row", (F32,), lambda acc, res: (acc + res,), [x])[0]
    return x_out, (x, h2, r, r_grad)


def _ffn_bwd(dx, dxb, saved, g, w_up3, w_down3):
    x, h2, r, r_grad = saved
    da = _mm_stacked_nt("ffn_down_dx", dxb, w_down3, "row", (BF16,),
                        lambda acc, rg: (acc * rg.astype(F32),), [r_grad])[0]
    g_down = _mm_tn_stacked("ffn_down_dw", r, dxb, w_down3.shape, "row")
    dx, dxb, dg = _mm_stacked_nt("ffn_up_dx", da, w_up3, "col", **_norm_bwd(x, g, dx))
    g_up = _mm_tn_stacked("ffn_up_dw", h2, da, w_up3.shape, "col")
    return dx, dxb, dg, g_up, g_down


def _norm_bwd(x, g, dres):
    def epilogue(dh, xv, rv, gv):
        dxv, dg = _rms_bwd(dh, xv, gv)
        return dxv + rv, dxv + rv, dg

    return dict(out_dtypes=(F32, BF16), epilogue=epilogue, extras=[x, dres], consts=[g], sums=[g.shape], tm=MM_TILE // 2)


def _dot(a, b, dims=NN):
    return lax.dot_general(a.astype(BF16), b.astype(BF16), dims, preferred_element_type=F32)


def _mla_fwd(x, g, wdkv, q_norm, kv_norm, wq, wkv, wo, tables):
    d = x.shape[1]

    def project(xv, cs, s1, s2, gv, wdkv_v, qg, kg, wq_v, wkv_v):
        h = _rms_fwd(xv, gv).astype(BF16)
        lv = _dot(h, wdkv_v)
        cqn = _rms_fwd(lv[:, :Q_RANK], qg).astype(BF16)
        ckvn = _rms_fwd(lv[:, Q_RANK:Q_RANK + KV_RANK], kg).astype(BF16)
        return (h, lv, cqn, ckvn, _rope(lv[:, Q_RANK + KV_RANK:], cs, s1, s2), _dot(cqn, wq_v), _dot(ckvn, wkv_v))

    h, lat, cqn, ckvn, krb, q, kvb = _rowwise(
        "mla_project", project, [x, *tables], [g, wdkv, q_norm, kv_norm, wq, wkv],
        [(d, BF16), (LAT_PAD, F32), (Q_RANK, BF16), (KV_RANK, BF16), (LANE, BF16), (wq.shape[1], F32), (wkv.shape[1], BF16)],
        tr=512)
    ob, lse, qb = _flash_fwd(q, kvb, krb, tables)
    x_mid = _mm("mla_o", ob, wo, (F32,), lambda acc, res: (acc + res,), [x])[0]
    return x_mid, (x, h, lat, cqn, ckvn, krb, qb, kvb, ob, lse)


def _mla_bwd(dx, dxb, saved, g, wdkv, q_norm, kv_norm, wq, wkv, wo, tables):
    x, h, lat, cqn, ckvn, krb, qb, kvb, ob, lse = saved
    s_len = x.shape[0]
    def with_delta(do, ov):
        prod = do * ov.astype(F32)
        lane = lax.broadcasted_iota(jnp.int32, (do.shape[0], LANE), 1)
        by_lane = None
        for hd in range(HEADS):
            total = jnp.broadcast_to(jnp.sum(prod[:, hd * VHEAD:(hd + 1) * VHEAD], axis=1, keepdims=True), lane.shape)
            by_lane = jnp.where(lane == hd, total, 0.0 if by_lane is None else by_lane)
        return do, by_lane

    dob, delta_lanes = _mm("mla_o_dx", dxb, wo, (BF16,), with_delta, [ob], nt=True, tm=MM_TILE // 2, narrow=[(LANE, F32)])
    g_wo = _mm_tn("mla_o_dw", ob, dxb)[0]
    delta_row = delta_lanes[:, :HEADS].T.reshape(HEADS, 1, s_len)
    dq, dkvb, dkr = _flash_bwd(qb, kvb, krb, dob, lse.reshape(HEADS, 1, s_len), delta_row)

    def project_bwd(dq_f, dkv_v, dkr_v, lv, cqn_v, ckvn_v, h_v, xv, rv, cs, s1, s2, gv, qg, kg, wq_v, wkv_v, wdkv_v):
        parts = []
        for hd in range(HEADS):
            parts += [dq_f[:, hd * HEAD_PAD:hd * HEAD_PAD + NOPE],
                      _rope_t(dq_f[:, hd * HEAD_PAD + NOPE:(hd + 1) * HEAD_PAD], cs, s1, s2)]
        dq_v = jnp.concatenate(parts, axis=1).astype(BF16)
        dcq, dqg = _rms_bwd(_dot(dq_v, wq_v, NT), lv[:, :Q_RANK], qg)
        dckv, dkg = _rms_bwd(_dot(dkv_v, wkv_v, NT), lv[:, Q_RANK:Q_RANK + KV_RANK], kg)
        dkr_sum = dkr_v[:, :LANE]
        for hd in range(1, HEADS):
            dkr_sum = dkr_sum + dkr_v[:, hd * LANE:(hd + 1) * LANE]
        dlat = jnp.concatenate([dcq, dckv, _rope_t(dkr_sum, cs, s1, s2)], axis=1).astype(BF16)
        dxv, dg = _rms_bwd(_dot(dlat, wdkv_v, NT), xv, gv)
        return (dxv + rv, dxv + rv, _dot(cqn_v, dq_v, TN), _dot(ckvn_v, dkv_v, TN), _dot(h_v, dlat, TN), dqg, dkg, dg)

    dx, dxb, g_wq, g_wkv, g_wdkv, g_qn, g_kvn, dg = _rowwise(
        "mla_project_bwd", project_bwd, [dq, dkvb, dkr, lat, cqn, ckvn, h, x, dx, *tables],
        [g, q_norm, kv_norm, wq, wkv, wdkv], [(x.shape[1], F32), (x.shape[1], BF16)],
        [wq.shape, wkv.shape, wdkv.shape, q_norm.shape, kv_norm.shape, g.shape], tr=256)
    return dx, dxb, dg, g_wdkv, g_qn, g_kvn, g_wq, g_wkv, g_wo


def _sgu_layer_fwd(x, g, w_in3, ln_g, ln_b, w_sp, bias_full, w_out3):
    zpre, h = _mm_stacked("sgu_in", x, w_in3, "col", norm_g=g)
    uv = _sgu_fwd(zpre, ln_g, ln_b, w_sp, bias_full)
    x_mid = _mm_stacked("sgu_out", uv, w_out3, "row", (F32,), lambda acc, res: (acc + res,), [x])[0]
    return x_mid, (x, h, zpre, uv)


def _sgu_layer_bwd(dx, dxb, saved, g, w_in3, ln_g, ln_b, w_sp, bias_full, w_out3):
    x, h, zpre, uv = saved
    duv = _mm_stacked_nt("sgu_out_dx", dxb, w_out3, "row")[0]
    g_out = _mm_tn_stacked("sgu_out_dw", uv, dxb, w_out3.shape, "row")
    dz, g_lng, g_lnb, g_wsp, g_bias = _sgu_bwd(zpre, duv, ln_g, ln_b, w_sp, bias_full)
    g_in = _mm_tn_stacked("sgu_in_dw", h, dz, w_in3.shape, "col")
    dx, dxb, dg = _mm_stacked_nt("sgu_in_dx", dz, w_in3, "col", **_norm_bwd(x, g, dx))
    return dx, dxb, dg, g_in, g_out, g_lng, g_lnb, g_wsp, g_bias


def _loss_head(x, target, g):
    d = x.shape[1]

    def fn(xv, tv, gv):
        err = _rms_fwd(xv, gv) - tv
        dxv, dg = _rms_bwd(err * (1.0 / d), xv, gv)
        return dxv, dxv, dg, jnp.sum(err * err, axis=0, keepdims=True)

    return _rowwise("loss_head", fn, [x, target], [g], [(d, F32), (d, BF16)], [g.shape, g.shape])


def _mixer_weights(i, stacks):
    by_rows = lambda a: a.reshape(N_SHARDS * a.shape[1], a.shape[2])
    by_cols = lambda a: a.transpose(1, 0, 2).reshape(a.shape[1], N_SHARDS * a.shape[2])
    if i % 2:
        w_in3, w_out3, ln_g, ln_b = stacks
        return w_in3, ln_g.reshape(1, -1), ln_b.reshape(1, -1), w_out3
    wdkv = by_rows(stacks[0])
    wdkv = jnp.pad(wdkv, ((0, 0), (0, LAT_PAD - wdkv.shape[1])))
    wq = jnp.pad(by_cols(stacks[1]).reshape(Q_RANK, HEADS, QK_HEAD), ((0, 0), (0, 0), (0, HEAD_PAD - QK_HEAD)))
    return wdkv, wq.reshape(Q_RANK, HEADS * HEAD_PAD), by_cols(stacks[2]), by_rows(stacks[3])


def _local_step(x, positions, target, norm_mix, norm_ffn, final_norm, q_norm, kv_norm, w_sp, b_sp, mixers, ffn, reducer):
    tables = _rope_tables(positions)
    gd = mixers[1][2].size // SGU_GROUPS
    bias_full = [jnp.repeat(b_sp[j].T, gd, axis=1) for j in range(DEPTH // 2)]
    saved, mla, sgu = [], [None] * (DEPTH // 2), [None] * (DEPTH // 2)
    for i in range(DEPTH):
        j = i // 2
        x, *stacks = lax.optimization_barrier((x, *mixers[i]))
        if i % 2 == 0:
            wdkv, wq, wkv, wo = mla[j] = _mixer_weights(i, stacks)
            x, s_mix = _mla_fwd(x, norm_mix[i:i + 1], wdkv, q_norm[j:j + 1], kv_norm[j:j + 1], wq, wkv, wo, tables)
        else:
            w_in3, ln_g, ln_b, w_out3 = sgu[j] = _mixer_weights(i, stacks)
            x, s_mix = _sgu_layer_fwd(x, norm_mix[i:i + 1], w_in3, ln_g, ln_b, w_sp[j], bias_full[j], w_out3)
        x, s_ffn = _ffn_fwd(x, norm_ffn[i:i + 1], *ffn[i])
        saved.append((s_mix, s_ffn))
    dx, dxb, g_final, sq_cols = _loss_head(x, target, final_norm[None, :])
    loss = 0.5 * jnp.sum(sq_cols) / x.shape[1]

    def pair(g):
        return g, g.astype(BF16)

    g_mix, g_ffn = [None] * DEPTH, [None] * DEPTH
    mla_g, sgu_g = [None] * (DEPTH // 2), [None] * (DEPTH // 2)
    for i in reversed(range(DEPTH)):
        j = i // 2
        s_mix, s_ffn = saved[i]
        dx, dxb, g_ffn[i], g_up, g_down = _ffn_bwd(dx, dxb, s_ffn, norm_ffn[i:i + 1], *ffn[i])
        dxb = reducer.add(f"ffn{i}", i, {"ffn_w_up": g_up, "ffn_w_down": g_down}, dxb)
        dxb = reducer.phase_end(dxb)
        if i % 2 == 0:
            wdkv, wq, wkv, wo = mla[j]
            dx, dxb, g_mix[i], g_wdkv, g_qn, g_kvn, g_wq, g_wkv, g_wo = _mla_bwd(
                dx, dxb, s_mix, norm_mix[i:i + 1], wdkv, q_norm[j:j + 1], kv_norm[j:j + 1], wq, wkv, wo, tables)
            mla_g[j] = (g_qn, g_kvn)
            g_wq = g_wq.reshape(Q_RANK, HEADS, HEAD_PAD)[..., :QK_HEAD].reshape(Q_RANK, N_SHARDS, -1)
            dxb = reducer.add(f"mla{j}", j, {
                "mla_w_dkv": pair(g_wdkv[:, :Q_RANK + KV_RANK + ROPE].reshape(N_SHARDS, -1, Q_RANK + KV_RANK + ROPE)),
                "mla_w_uq": pair(g_wq.transpose(1, 0, 2)),
                "mla_w_ukv": pair(g_wkv.reshape(KV_RANK, N_SHARDS, -1).transpose(1, 0, 2)),
                "mla_w_o": pair(g_wo.reshape(N_SHARDS, -1, g_wo.shape[1]))}, dxb)
        else:
            w_in3, ln_g, ln_b, w_out3 = sgu[j]
            dx, dxb, g_mix[i], g_in, g_out, g_lng, g_lnb, g_wsp, g_bias = _sgu_layer_bwd(
                dx, dxb, s_mix, norm_mix[i:i + 1], w_in3, ln_g, ln_b, w_sp[j], bias_full[j], w_out3)
            sgu_g[j] = (g_wsp, g_bias.reshape(SGU_CHUNK, SGU_GROUPS, gd).sum(axis=-1).T)
            dxb = reducer.add(f"sgu{j}", j, {"sgu_w_in": g_in, "sgu_w_out": g_out,
                                             "sgu_ln_g": pair(g_lng.reshape(N_SHARDS, -1, LANE)),
                                             "sgu_ln_b": pair(g_lnb.reshape(N_SHARDS, -1, LANE))}, dxb)
        dxb = reducer.phase_end(dxb)
    small = dict(
        norm_mix=jnp.concatenate(g_mix, axis=0), norm_ffn=jnp.concatenate(g_ffn, axis=0), final_norm=g_final[0],
        q_norm=jnp.concatenate([m[0] for m in mla_g], axis=0), kv_norm=jnp.concatenate([m[1] for m in mla_g], axis=0),
        w_sp=jnp.stack([s[0] for s in sgu_g]), b_sp=jnp.stack([s[1] for s in sgu_g]))
    return loss, dx, small


HBM_SPEC = pl.BlockSpec(memory_space=pltpu.HBM)


def _place():
    x, y, c = lax.axis_index("x"), lax.axis_index("y"), lax.axis_index("c")
    return x, y, c, [(1 - x, y), (x, 1 - y), (1 - x, 1 - y)]


def _remote(src, dst, send_sems, recv_sems, k, to):
    return pltpu.make_async_remote_copy(src_ref=src, dst_ref=dst, send_sem=send_sems.at[k], recv_sem=recv_sems.at[k],
                                        device_id=to, device_id_type=MESH)


def _gather_layer(tag, shards):
    n = len(shards)
    split = [s.shape[0] >= 16 for s in shards]

    def body(*refs):
        ins, outs = refs[:n], refs[n:2 * n]
        send_sems, recv_sems, local_sems = refs[2 * n:]
        x, y, c, chips = _place()
        mine = 2 * x + y
        barrier = pltpu.get_barrier_semaphore()
        peers = [(x, y, 1 - c)] + [(*chip, c) for chip in chips]
        for peer in peers:
            pl.semaphore_signal(barrier, inc=1, device_id=peer, device_id_type=MESH)
        pl.semaphore_wait(barrier, len(peers))

        def rows(t, half):
            hr = shards[t].shape[0] // 2
            return pl.ds(half * hr, hr) if split[t] else pl.ds(0, shards[t].shape[0])

        local, sent = [], []
        for t in range(n):
            local.append(pltpu.make_async_copy(ins[t], outs[t].at[mine], local_sems.at[t]))
            local[-1].start()
            for j, chip in enumerate(chips):
                cp = _remote(ins[t].at[rows(t, c)], outs[t].at[mine, rows(t, c)], send_sems, recv_sems, 3 * t + j, (*chip, c))
                cp.start()
                sent.append(cp)
        for j, chip in enumerate(chips):
            theirs = 2 * chip[0] + chip[1]
            for t in range(n):
                piece = outs[t].at[theirs, rows(t, c)]
                _remote(piece, piece, send_sems, recv_sems, 3 * t + j, (x, y, c)).wait_recv()
                if split[t]:
                    cp = _remote(piece, piece, send_sems, recv_sems, 3 * n + 3 * t + j, (x, y, 1 - c))
                    cp.start()
                    sent.append(cp)
        for j, chip in enumerate(chips):
            theirs = 2 * chip[0] + chip[1]
            for t in range(n):
                if split[t]:
                    piece = outs[t].at[theirs, rows(t, 1 - c)]
                    _remote(piece, piece, send_sems, recv_sems, 3 * n + 3 * t + j, (x, y, c)).wait_recv()
        for cp in sent:
            cp.wait_send()
        for cp in local:
            cp.wait()

    return pl.kernel(
        body, name=f"gather_{tag}", mesh=plsc.ScalarSubcoreMesh(axis_name="sequencer", num_cores=1),
        out_type=[jax.ShapeDtypeStruct((N_SHARDS, *s.shape), s.dtype) for s in shards],
        scratch_types=[pltpu.SemaphoreType.DMA((6 * n,)), pltpu.SemaphoreType.DMA((6 * n,)), pltpu.SemaphoreType.DMA((n,))],
        compiler_params=pltpu.CompilerParams(collective_id=ID_GATHER),
    )(*shards)


SEQUENCER = dict(axis_name="sequencer", num_cores=1)
ID_GATHER, ID_EXCHANGE, ID_SHARE = 0, 1, 2
MIN_SPLIT_ROWS = 16


def _handshake(peers):
    barrier = pltpu.get_barrier_semaphore()
    for peer in peers:
        pl.semaphore_signal(barrier, inc=1, device_id=peer, device_id_type=MESH)
    pl.semaphore_wait(barrier, len(peers))


def _half_rows(rows, half):
    return pl.ds(half * (rows // 2), rows // 2) if rows >= MIN_SPLIT_ROWS else pl.ds(0, rows)


SEM_SPEC = pl.BlockSpec(memory_space=pltpu.SEMAPHORE)
DATAFLOW = pltpu.SideEffectType.DATAFLOW_SIDE_EFFECTING


def _exchange_copies(shapes, stacks, lands, send_sems, recv_sems):
    x, y, c, chips = _place()
    mine = 2 * x + y
    copies = []
    for t, shape in enumerate(shapes):
        r = shape[1]
        copies.append(_remote(stacks[t].at[mine, _half_rows(r, 1 - c)], lands[t].at[0], send_sems, recv_sems, 7 * t, (x, y, 1 - c)))
        for j, chip in enumerate(chips):
            theirs = 2 * chip[0] + chip[1]
            copies.append(_remote(stacks[t].at[theirs, _half_rows(r, c)], lands[t].at[1 + j], send_sems, recv_sems,
                                  7 * t + 1 + j, (*chip, c)))
            copies.append(_remote(stacks[t].at[theirs, _half_rows(r, 1 - c)], lands[t].at[4 + j], send_sems, recv_sems,
                                  7 * t + 4 + j, (*chip, 1 - c)))
    return copies


def _exchange_start(tag, stacks, carry):
    n = len(stacks)
    shapes = [s.shape for s in stacks]
    lands = [lax.empty((7, s.shape[1] // 2 if s.shape[1] >= MIN_SPLIT_ROWS else s.shape[1], s.shape[2]), s.dtype) for s in stacks]

    def body(*refs):
        for cp in _exchange_copies(shapes, refs[:n], refs[n:2 * n], refs[2 * n + 1], refs[2 * n + 2]):
            cp.start()

    through = (*stacks, *lands, carry)
    out = pl.pallas_call(
        body, name=f"reduce_exchange_start_{tag}",
        out_shape=(pltpu.SemaphoreType.DMA((7 * n,)), pltpu.SemaphoreType.DMA((7 * n,)),
                   *[pltpu.HBM(a.shape, a.dtype) for a in through]),
        in_specs=[HBM_SPEC] * (2 * n + 1),
        out_specs=(SEM_SPEC, SEM_SPEC, *[HBM_SPEC] * (2 * n + 1)),
        input_output_aliases={t: 2 + t for t in range(2 * n + 1)},
        compiler_params=pltpu.CompilerParams(has_side_effects=DATAFLOW),
    )(*[pltpu.with_memory_space_constraint(a, pltpu.HBM) for a in through])
    return out[0], out[1], out[2:2 + n], out[2 + n:2 + 2 * n], out[-1]


def _exchange_wait(tag, send_sems, recv_sems, stacks, lands, after):
    n = len(stacks)
    shapes = [s.shape for s in stacks]

    def body(*refs):
        for cp in _exchange_copies(shapes, refs[:n], refs[n:2 * n], refs[2 * n], refs[2 * n + 1]):
            cp.wait()

    out = pl.pallas_call(
        body, name=f"reduce_exchange_wait_{tag}",
        out_shape=tuple(pltpu.HBM(a.shape, a.dtype) for a in (*stacks, *lands)),
        in_specs=[HBM_SPEC] * (2 * n) + [SEM_SPEC, SEM_SPEC, pl.BlockSpec(memory_space=pl.ANY)],
        out_specs=tuple([HBM_SPEC] * (2 * n)),
        input_output_aliases={t: t for t in range(2 * n)},
        compiler_params=pltpu.CompilerParams(has_side_effects=DATAFLOW),
    )(*stacks, *lands, send_sems, recv_sems, after)
    return out[:n], out[n:]


def _share_halves(tag, halves):
    n = len(halves)

    def body(*refs):
        ins, outs, send_sems, recv_sems = refs[:n], refs[n:2 * n], refs[2 * n], refs[2 * n + 1]
        x, y, c, _ = _place()
        _handshake([(x, y, 1 - c)])
        sent = [_remote(ins[t], outs[t], send_sems, recv_sems, t, (x, y, 1 - c)) for t in range(n)]
        for cp in sent:
            cp.start()
        for cp in sent:
            cp.wait()

    return pl.kernel(
        body, name=f"reduce_share_{tag}", mesh=plsc.ScalarSubcoreMesh(**SEQUENCER),
        out_type=[jax.ShapeDtypeStruct(h.shape, h.dtype) for h in halves],
        scratch_types=[pltpu.SemaphoreType.DMA((n,)), pltpu.SemaphoreType.DMA((n,))],
        compiler_params=pltpu.CompilerParams(collective_id=ID_SHARE),
    )(*halves)


def _all_reduce_small(part):
    rows = part.shape[0]
    half = rows // 2

    def body(p_ref, out_ref, sib_buf, chip_sums, send_sems, recv_sems):
        x, y, c, chips = _place()
        mine = 2 * x + y
        my_rows = pl.ds(pl.multiple_of(c * half, 8), half)
        swap = _remote(p_ref, sib_buf, send_sems, recv_sems, 0, (x, y, 1 - c))
        swap.start()
        swap.wait()
        chip_sums[mine] = p_ref[...] + sib_buf[...]
        sent = [_remote(chip_sums.at[mine, my_rows], chip_sums.at[mine, my_rows], send_sems, recv_sems, 1 + j, (*chip, c))
                for j, chip in enumerate(chips)]
        for cp in sent:
            cp.start()
        for j, chip in enumerate(chips):
            sent[j].wait_send()
            theirs = chip_sums.at[2 * chip[0] + chip[1], my_rows]
            _remote(theirs, theirs, send_sems, recv_sems, 1 + j, (x, y, c)).wait_recv()
        out_ref[my_rows, :] = ((chip_sums[0, my_rows, :] + chip_sums[1, my_rows, :]) + chip_sums[2, my_rows, :]) + chip_sums[3, my_rows, :]
        share = _remote(out_ref.at[my_rows], out_ref.at[my_rows], send_sems, recv_sems, 4, (x, y, 1 - c))
        share.start()
        share.wait_send()
        other = out_ref.at[pl.ds(pl.multiple_of((1 - c) * half, 8), half)]
        _remote(other, other, send_sems, recv_sems, 4, (x, y, c)).wait_recv()

    vmem = pl.BlockSpec(memory_space=pltpu.VMEM)
    return pl.pallas_call(
        body, name="all_reduce_small", in_specs=[vmem], out_specs=vmem, out_shape=jax.ShapeDtypeStruct(part.shape, F32),
        scratch_shapes=[pltpu.VMEM((rows, LANE), F32), pltpu.VMEM((N_SHARDS, rows, LANE), F32),
                        pltpu.SemaphoreType.DMA((5,)), pltpu.SemaphoreType.DMA((5,))],
        compiler_params=pltpu.CompilerParams(vmem_limit_bytes=VMEM_LIMIT_BYTES),
    )(part)


def _sum_partials(g3, others, sel):
    _, rows, c = others.shape
    whole = g3.shape[1] == rows
    tr = _tile(rows, UPDATE_ROWS)
    nb = rows // tr

    def body(sel_ref, g_ref, *rest):
        same = g_ref[...].astype(F32)
        for ref in rest[1:4]:
            same = same + ref[...].astype(F32)
        other = rest[0][...].astype(F32)
        for ref in rest[4:7]:
            other = other + ref[...].astype(F32)
        rest[7][...] = same + other

    blk = (None, tr, c)
    slots = [pl.BlockSpec(blk, functools.partial(lambda i, sr, k: (k, i, 0), k=k)) for k in range(7)]
    return pl.pallas_call(
        body, name="reduce_sum_partials",
        grid_spec=pltpu.PrefetchScalarGridSpec(
            num_scalar_prefetch=1, grid=(nb,),
            in_specs=[pl.BlockSpec(blk, lambda i, sr: (sr[0], (0 if whole else sr[1] * nb) + i, 0))] + slots,
            out_specs=pl.BlockSpec((tr, c), lambda i, sr: (i, 0))),
        out_shape=jax.ShapeDtypeStruct((rows, c), F32),
        compiler_params=_params(("parallel",)),
    )(sel, g3, *[others] * 7)


def _adamw_math(w, g, m, v):
    nm = ADAM_B1 * m + (1.0 - ADAM_B1) * g
    nv = ADAM_B2 * v + (1.0 - ADAM_B2) * (g * g)
    m_hat = nm / (1.0 - ADAM_B1 ** ADAM_STEP)
    v_hat = nv / (1.0 - ADAM_B2 ** ADAM_STEP)
    return -ADAM_LR * (m_hat / (jnp.sqrt(v_hat) + ADAM_EPS) + ADAM_WD * w), nm, nv


def _adamw_layer(layer, w, m, v, g_mine, g_sibling, sel, prev):
    lyr, r, c = w.shape
    rows = g_mine.shape[0]
    halves = r // rows
    tr = _tile(rows, 512)
    nb = rows // tr
    n_g = 1 if g_sibling is None else 2

    def body(sel_ref, w_ref, m_ref, v_ref, *rest):
        g = rest[0][...]
        if n_g == 2:
            g = jnp.where(pl.program_id(0) == sel_ref[1], g, rest[1][...])
        outs = rest[n_g + (0 if prev is None else 4):]
        d, nm, nv = _adamw_math(w_ref[...], g, m_ref[...], v_ref[...])
        for ref, val in zip(outs, (g, d, nm, nv)):
            ref[...] = val

    full = pl.BlockSpec((None, tr, c), lambda h, i, sr: (layer, h * nb + i, 0))
    part = pl.BlockSpec((tr, c), lambda h, i, sr: (i, 0))
    n_in = 4 + n_g
    return pl.pallas_call(
        body, name="adamw_layer",
        grid_spec=pltpu.PrefetchScalarGridSpec(
            num_scalar_prefetch=1, grid=(halves, nb),
            in_specs=[full] * 3 + [part] * n_g + ([] if prev is None else [pl.BlockSpec(memory_space=pl.ANY)] * 4),
            out_specs=[full] * 4),
        out_shape=[jax.ShapeDtypeStruct(w.shape, F32)] * 4,
        input_output_aliases={} if prev is None else {n_in + k: k for k in range(4)},
        compiler_params=_params(("parallel", "parallel")),
    )(sel, w, m, v, g_mine, *([] if g_sibling is None else [g_sibling]), *([] if prev is None else prev))


def _adamw(w, g, m, v):
    lyr, r, c = w.shape
    tr = _tile(r, 256)

    def body(w_ref, g_ref, m_ref, v_ref, d_ref, nm_ref, nv_ref):
        d_ref[...], nm_ref[...], nv_ref[...] = _adamw_math(w_ref[...], g_ref[...], m_ref[...], v_ref[...])

    blk = pl.BlockSpec((None, tr, c), lambda l, i: (l, i, 0))
    return pl.pallas_call(
        body, name="adamw", grid=(lyr, r // tr), in_specs=[blk] * 4, out_specs=[blk] * 3,
        out_shape=[jax.ShapeDtypeStruct(w.shape, F32)] * 3,
        compiler_params=_params(("parallel", "parallel")),
    )(w, g, m, v)


SHARDED = ("mla_w_dkv", "mla_w_uq", "mla_w_ukv", "mla_w_o", "sgu_w_in", "sgu_ln_g", "sgu_ln_b", "sgu_w_out",
           "ffn_w_up", "ffn_w_down")
REPLICATED = ("norm_mix", "norm_ffn", "final_norm", "mla_q_norm", "mla_kv_norm", "sgu_w_spatial", "sgu_b_spatial")
WEIGHTS = ("norm_mix", "norm_ffn", "final_norm", "mla_w_dkv", "mla_q_norm", "mla_kv_norm", "mla_w_uq", "mla_w_ukv",
           "mla_w_o", "sgu_w_in", "sgu_ln_g", "sgu_ln_b", "sgu_w_spatial", "sgu_b_spatial", "sgu_w_out", "ffn_w_up",
           "ffn_w_down")


class _Reducer:
    def __init__(self, state, sel):
        self.state, self.sel = state, sel
        self.started, self.travelling, self.summed = [], [], []
        self.done = {}

    def add(self, tag, layer, grads, token):
        names = list(grads)
        token, *tied = lax.optimization_barrier((token, *[a for n in names for a in grads[n]]))
        f32s, bf16s = tied[0::2], tied[1::2]
        *flying, token = _exchange_start(tag, bf16s, token)
        self.started.append((tag, layer, names, f32s, flying))
        return token

    def phase_end(self, token):
        for tag, layer, names, f32s, flying in self.travelling:
            bf16s, received = _exchange_wait(tag, *flying, token)
            own = [g if g.shape[1] >= MIN_SPLIT_ROWS else gb for g, gb in zip(f32s, bf16s)]
            mine = [_sum_partials(g, got, self.sel) for g, got in zip(own, received)]
            token, *mine = lax.optimization_barrier((token, *mine))
            cut = [k for k, g in enumerate(own) if g.shape[1] >= MIN_SPLIT_ROWS]
            theirs = dict(zip(cut, _share_halves(tag, [mine[k] for k in cut])))
            self.summed.append((layer, names, mine, [theirs.get(k) for k in range(len(names))]))
        self.travelling, self.started = self.started, []
        return token

    def update(self, token):
        for layer, names, mine, theirs in self.summed:
            for name, g_mine, g_theirs in zip(names, mine, theirs):
                w, m, v = self.state[name]
                self.done[name] = _adamw_layer(layer, w, m, v, g_mine, g_theirs, self.sel, self.done.get(name))
                token = self.done[name][1]
        self.summed = []
        return token


def _as3d(name, a):
    return a.reshape(a.shape[0], -1, LANE) if name in ("sgu_ln_g", "sgu_ln_b") else a


def _pack(parts):
    flat = jnp.concatenate([p.reshape(-1) for p in parts])
    rows = -(-flat.shape[0] // (256 * LANE)) * 256
    return jnp.pad(flat, (0, rows * LANE - flat.shape[0])).reshape(rows, LANE)


def _unpack(packed, like):
    flat, out, at = packed.reshape(-1), [], 0
    for p in like:
        out.append(flat[at:at + p.size].reshape(p.shape))
        at += p.size
    return out


def kernel(x, positions, norm_mix, norm_ffn, final_norm, mla_w_dkv, mla_q_norm, mla_kv_norm, mla_w_uq, mla_w_ukv, mla_w_o, sgu_w_in, sgu_ln_g, sgu_ln_b, sgu_w_spatial, sgu_b_spatial, sgu_w_out, ffn_w_up, ffn_w_down, loss_target, m_norm_mix, m_norm_ffn, m_final_norm, m_mla_w_dkv, m_mla_q_norm, m_mla_kv_norm, m_mla_w_uq, m_mla_w_ukv, m_mla_w_o, m_sgu_w_in, m_sgu_ln_g, m_sgu_ln_b, m_sgu_w_spatial, m_sgu_b_spatial, m_sgu_w_out, m_ffn_w_up, m_ffn_w_down, v_norm_mix, v_norm_ffn, v_final_norm, v_mla_w_dkv, v_mla_q_norm, v_mla_kv_norm, v_mla_w_uq, v_mla_w_ukv, v_mla_w_o, v_sgu_w_in, v_sgu_ln_g, v_sgu_ln_b, v_sgu_w_spatial, v_sgu_b_spatial, v_sgu_w_out, v_ffn_w_up, v_ffn_w_down):
    given = dict(locals())
    w = {n: given[n] for n in WEIGHTS}
    mom = {n: given["m_" + n] for n in WEIGHTS}
    var = {n: given["v_" + n] for n in WEIGHTS}
    mixers, ffn, token = [], [], None
    for i in range(DEPTH):
        j = i // 2
        if i % 2 == 0:
            mixer = [w[n][j].astype(BF16) for n in ("mla_w_dkv", "mla_w_uq", "mla_w_ukv", "mla_w_o")]
        else:
            mixer = [sgu_w_in[j].astype(BF16), sgu_w_out[j].astype(BF16), sgu_ln_g[j].reshape(-1, LANE),
                     sgu_ln_b[j].reshape(-1, LANE)]
        for tag, shards, into in ((f"mixer{i}", mixer, mixers), (f"ffn{i}", [ffn_w_up[i].astype(BF16), ffn_w_down[i].astype(BF16)], ffn)):
            if token is None:
                token = shards[0]
            else:
                token, *shards = lax.optimization_barrier((token, *shards))
            into.append(_gather_layer(tag, shards))

    x_i, y_i, c_i = lax.axis_index("x"), lax.axis_index("y"), lax.axis_index("c")
    sel = jnp.stack([2 * x_i + y_i, c_i]).astype(jnp.int32)
    reducer = _Reducer({n: tuple(_as3d(n, d[n]) for d in (w, mom, var)) for n in SHARDED}, sel)
    loss, dx, small = _local_step(
        x[0], positions[0], loss_target[0], norm_mix, norm_ffn, final_norm, mla_q_norm, mla_kv_norm, sgu_w_spatial,
        sgu_b_spatial, mixers, ffn, reducer)
    loss = lax.psum(loss, ("x", "y", "c"))

    small_g = [small["norm_mix"], small["norm_ffn"], small["final_norm"], small["q_norm"], small["kv_norm"],
               small["w_sp"], small["b_sp"]]
    like = [w[n] for n in REPLICATED]
    g_small = _all_reduce_small(_pack(small_g))
    packed = [_pack([d[n] for n in REPLICATED])[None] for d in (w, mom, var)]
    upd_small = _adamw(packed[0], g_small[None], packed[1], packed[2])
    grads = dict(zip(REPLICATED, _unpack(g_small, like)))
    delta, new_m, new_v = ({n: a for n, a in zip(REPLICATED, _unpack(u[0], like))} for u in upd_small)

    reducer.phase_end(reducer.update(upd_small[0]))
    reducer.update(None)
    for n in SHARDED:
        grads[n], delta[n], new_m[n], new_v[n] = (a.reshape(w[n].shape) for a in reducer.done[n])

    return (loss, dx[None], *[grads[n] for n in WEIGHTS], *[delta[n] for n in WEIGHTS],
            *[new_m[n] for n in WEIGHTS], *[new_v[n] for n in WEIGHTS])
```

```python
import functools
import math

import jax
import jax.numpy as jnp
from jax import lax
from jax.experimental import pallas as pl
from jax.experimental.pallas import tpu as pltpu
from jax.experimental.pallas import tpu_sc as plsc

F32 = jnp.float32
BF16 = jnp.bfloat16
MESH = pl.DeviceIdType.MESH

DEPTH = 4
HEADS = 8
NOPE = 128
ROPE = 64
VHEAD = 128
QK_HEAD = NOPE + ROPE
Q_RANK = 256
KV_RANK = 128
HEAD_PAD = 256
LAT_PAD = 512
ROPE_THETA = 10000.0
SGU_CHUNK = 128
SGU_GROUPS = 8
NORM_EPS = 1e-6
LN_EPS = 1e-5
ADAM_LR, ADAM_B1, ADAM_B2, ADAM_EPS, ADAM_WD, ADAM_STEP = 0.001, 0.9, 0.999, 1e-08, 0.01, 10

N_SHARDS = 4
LANE = 128
VMEM_LIMIT_BYTES = 56 * 1024 * 1024
ATT_TILE = 512
MM_TILE = 1024
UPDATE_ROWS = 128
ATT_SCALE = QK_HEAD ** -0.5
LOG2_SCALE = ATT_SCALE * math.log2(math.e)

NN = (((1,), (0,)), ((), ()))
NT = (((1,), (1,)), ((), ()))
TN = (((0,), (0,)), ((), ()))


def _params(sem):
    return pltpu.CompilerParams(dimension_semantics=sem, vmem_limit_bytes=VMEM_LIMIT_BYTES)


def _tile(n, pref):
    t = min(n, pref)
    while n % t:
        t //= 2
    return t


def _matmul(name, a, b, a_spec, b_spec, dims, grid, tile, outs, extras=(), epilogue=None, sums=(), norm_g=None):
    nk, ne, no = grid[2], len(extras), len(outs)
    b_specs = list(b_spec) if isinstance(b_spec, (list, tuple)) else [b_spec]
    nb = len(b_specs)
    normed = norm_g is not None
    assert not normed or (nk == 1 and nb == 1)

    def body(a_ref, *rest):
        b_refs, e_refs, o_refs = rest[:nb], rest[nb:nb + ne], rest[nb + ne + normed:nb + ne + normed + no]
        s_refs = rest[nb + ne + normed + no:nb + ne + normed + no + len(sums)]
        if normed:
            g_ref, h_ref, h_s = rest[nb + ne], rest[nb + ne + 1 + no + len(sums)], rest[-1]

            @pl.when(pl.program_id(1) == 0)
            def _():
                h_s[...] = _rms_fwd(a_ref[...], g_ref[...]).astype(BF16)
                h_ref[...] = h_s[...]

            a_ref = h_s
        kw = a_ref.shape[1] // nb
        part = None
        for p, b_ref in enumerate(b_refs):
            a_tile = a_ref[...] if nb == 1 else a_ref[:, p * kw:(p + 1) * kw]
            d = lax.dot_general(a_tile.astype(BF16), b_ref[...].astype(BF16), dims, preferred_element_type=F32)
            part = d if part is None else part + d

        def finish(acc):
            vals = (acc,) if epilogue is None else epilogue(acc, *[e[...] for e in e_refs])
            for o_ref, v in zip(o_refs, vals):
                o_ref[...] = v.astype(o_ref.dtype)
            first = pl.program_id(0) == 0
            for s_ref, v in zip(s_refs, vals[no:]):
                @pl.when(first)
                def _():
                    s_ref[...] = v

                @pl.when(jnp.logical_not(first))
                def _():
                    s_ref[...] += v

        if nk == 1:
            finish(part)
            return
        acc_ref, k = rest[-1], pl.program_id(2)

        @pl.when(k == 0)
        def _():
            acc_ref[...] = part

        @pl.when(jnp.logical_and(k > 0, k < nk - 1))
        def _():
            acc_ref[...] += part

        @pl.when(k == nk - 1)
        def _():
            finish(acc_ref[...] + part)

    assert not sums or (grid[1] == 1 and nk == 1)
    a_block = a_spec.block_shape
    return pl.pallas_call(
        body, name=name, grid=grid,
        in_specs=[a_spec] + b_specs + [s for _, s in extras]
        + ([pl.BlockSpec(norm_g.shape, lambda i, j, k: (0, 0))] if normed else []),
        out_specs=[s for _, s in outs] + [pl.BlockSpec(s, lambda i, j, k: (0,) * len(s)) for s in sums]
        + ([pl.BlockSpec(a_block, lambda i, j, k: (i, 0))] if normed else []),
        out_shape=[s for s, _ in outs] + [jax.ShapeDtypeStruct(s, F32) for s in sums]
        + ([jax.ShapeDtypeStruct(a.shape, BF16)] if normed else []),
        scratch_shapes=([pltpu.VMEM(tile, F32)] if nk > 1 else []) + ([pltpu.VMEM(a_block, BF16)] if normed else []),
        compiler_params=_params(("arbitrary" if sums else "parallel", "arbitrary" if normed else "parallel", "arbitrary")),
    )(a, *[b] * nb, *[e for e, _ in extras], *([norm_g] if normed else []))


def _epilogue_operands(extras, consts, o_spec):
    return [(e, o_spec) for e in extras] + [(c, pl.BlockSpec(c.shape, lambda i, j, k: (0, 0))) for c in consts]


def _mm(name, a, b, out_dtypes=(F32,), epilogue=None, extras=(), tm=MM_TILE, tn=MM_TILE, tk=MM_TILE, nt=False,
        consts=(), sums=(), narrow=()):
    m, kd = a.shape
    n = b.shape[0] if nt else b.shape[1]
    tm, tn, tk = _tile(m, tm), _tile(n, tn), _tile(kd, tk)
    o_spec = pl.BlockSpec((tm, tn), lambda i, j, k: (i, j))
    b_spec = pl.BlockSpec((tn, tk), lambda i, j, k: (j, k)) if nt else pl.BlockSpec((tk, tn), lambda i, j, k: (k, j))
    assert not narrow or n == tn
    outs = [(jax.ShapeDtypeStruct((m, n), d), o_spec) for d in out_dtypes]
    outs += [(jax.ShapeDtypeStruct((m, w), d), pl.BlockSpec((tm, w), lambda i, j, k: (i, 0))) for w, d in narrow]
    return _matmul(name, a, b, pl.BlockSpec((tm, tk), lambda i, j, k: (i, k)), b_spec, NT if nt else NN,
                   (m // tm, n // tn, kd // tk), (tm, tn), outs,
                   _epilogue_operands(extras, consts, o_spec), epilogue, sums)


def _mm_tn(name, a, b, out_dtypes=(F32,), tm=MM_TILE, tn=MM_TILE, tk=MM_TILE):
    s, m = a.shape
    n = b.shape[1]
    tm, tn, tk = _tile(m, tm), _tile(n, tn), _tile(s, tk)
    o_spec = pl.BlockSpec((tm, tn), lambda i, j, k: (i, j))
    return _matmul(name, a, b, pl.BlockSpec((tk, tm), lambda i, j, k: (k, i)),
                   pl.BlockSpec((tk, tn), lambda i, j, k: (k, j)), TN, (m // tm, n // tn, s // tk), (tm, tn),
                   [(jax.ShapeDtypeStruct((m, n), d), o_spec) for d in out_dtypes])


def _mm_stacked(name, a, w3, mode, out_dtypes=(F32,), epilogue=None, extras=(), tm=MM_TILE, tn=MM_TILE, tk=MM_TILE,
                norm_g=None):
    m, kd = a.shape
    _, r, c = w3.shape
    n = c if mode == "row" else N_SHARDS * c
    if mode == "row":
        tm, tn, tk = _tile(m, tm // 2), _tile(n, tn), kd
        b_spec = [pl.BlockSpec((None, r, tn), functools.partial(lambda i, j, k, p: (p, 0, j), p=p)) for p in range(N_SHARDS)]
    else:
        tm, tn, tk = _tile(m, tm), _tile(c, tn), _tile(kd, tk)
        per = c // tn
        b_spec = pl.BlockSpec((None, tk, tn), lambda i, j, k: (j // per, k, j % per))
    o_spec = pl.BlockSpec((tm, tn), lambda i, j, k: (i, j))
    return _matmul(name, a, w3, pl.BlockSpec((tm, tk), lambda i, j, k: (i, k)), b_spec, NN,
                   (m // tm, n // tn, kd // tk), (tm, tn),
                   [(jax.ShapeDtypeStruct((m, n), d), o_spec) for d in out_dtypes],
                   [(e, o_spec) for e in extras], epilogue, norm_g=norm_g)


def _mm_stacked_nt(name, a, w3, mode, out_dtypes=(F32,), epilogue=None, extras=(), tm=MM_TILE, tn=MM_TILE, tk=MM_TILE,
                   consts=(), sums=()):
    m, nd = a.shape
    _, r, c = w3.shape
    kout = N_SHARDS * r if mode == "row" else r
    if mode == "row":
        tm, tn, tk = _tile(m, tm), _tile(r, tn), _tile(c, tk)
        per = r // tn
        b_spec = pl.BlockSpec((None, tn, tk), lambda i, j, k: (j // per, j % per, k))
    else:
        tm, tn, tk = _tile(m, tm // 2), _tile(r, tn), nd
        b_spec = [pl.BlockSpec((None, tn, c), functools.partial(lambda i, j, k, p: (p, j, 0), p=p)) for p in range(N_SHARDS)]
    o_spec = pl.BlockSpec((tm, tn), lambda i, j, k: (i, j))
    return _matmul(name, a, w3, pl.BlockSpec((tm, tk), lambda i, j, k: (i, k)), b_spec, NT,
                   (m // tm, kout // tn, nd // tk), (tm, tn),
                   [(jax.ShapeDtypeStruct((m, kout), d), o_spec) for d in out_dtypes],
                   _epilogue_operands(extras, consts, o_spec), epilogue, sums)


def _mm_tn_stacked(name, a, b, shape3, mode, tm=MM_TILE, tn=MM_TILE, tk=MM_TILE):
    s, m = a.shape
    n = b.shape[1]
    _, r, c = shape3
    tk, tn = s, tn // 2
    if mode == "row":
        tm, tn = _tile(r, tm), _tile(n, tn)
        per = r // tm
        o_spec = pl.BlockSpec((None, tm, tn), lambda i, j, k: (i // per, i % per, j))
    else:
        tm, tn = _tile(m, tm), _tile(c, tn)
        per = c // tn
        o_spec = pl.BlockSpec((None, tm, tn), lambda i, j, k: (j // per, i, j % per))
    outs = [(jax.ShapeDtypeStruct(shape3, F32), o_spec), (jax.ShapeDtypeStruct(shape3, BF16), o_spec)]
    return _matmul(name, a, b, pl.BlockSpec((tk, tm), lambda i, j, k: (k, i)),
                   pl.BlockSpec((tk, tn), lambda i, j, k: (k, j)), TN, (m // tm, n // tn, s // tk), (tm, tn),
                   outs, epilogue=lambda acc: (acc, acc))


def _rowwise(name, fn, rows, consts, out_rows, out_accs=(), tr=256):
    nr, nc, no = len(rows), len(consts), len(out_rows)
    n_rows = rows[0].shape[0]
    tr = _tile(n_rows, tr)

    def body(*refs):
        vals = fn(*[r[...] for r in refs[:nr + nc]])
        o_refs, a_refs = refs[nr + nc:nr + nc + no], refs[nr + nc + no:]
        for ref, v in zip(o_refs, vals[:no]):
            ref[...] = v.astype(ref.dtype)
        first = pl.program_id(0) == 0

        @pl.when(first)
        def _():
            for ref, v in zip(a_refs, vals[no:]):
                ref[...] = v

        @pl.when(jnp.logical_not(first))
        def _():
            for ref, v in zip(a_refs, vals[no:]):
                ref[...] += v

    def whole(shape):
        return pl.BlockSpec(shape, lambda i: (0,) * len(shape))

    return pl.pallas_call(
        body, name=name, grid=(n_rows // tr,),
        in_specs=[pl.BlockSpec((tr, a.shape[1]), lambda i: (i, 0)) for a in rows] + [whole(c.shape) for c in consts],
        out_specs=[pl.BlockSpec((tr, f), lambda i: (i, 0)) for f, _ in out_rows] + [whole(s) for s in out_accs],
        out_shape=[jax.ShapeDtypeStruct((n_rows, f), d) for f, d in out_rows]
        + [jax.ShapeDtypeStruct(s, F32) for s in out_accs],
        compiler_params=_params(("arbitrary",)),
    )(*rows, *consts)


def _rms_fwd(x, g):
    return x * lax.rsqrt(jnp.mean(x * x, axis=-1, keepdims=True) + NORM_EPS) * g


def _rms_bwd(dy, x, g):
    rstd = lax.rsqrt(jnp.mean(x * x, axis=-1, keepdims=True) + NORM_EPS)
    n = x * rstd
    dn = dy * g
    dx = rstd * (dn - n * jnp.mean(dn * n, axis=-1, keepdims=True))
    return dx, jnp.sum(dy * n, axis=0, keepdims=True)


def _rope(x, cs, s1, s2):
    return x * cs + pltpu.roll(x, 32, 1) * s1 + pltpu.roll(x, 96, 1) * s2


def _rope_t(dy, cs, s1, s2):
    return dy * cs + pltpu.roll(dy * s1, 96, 1) + pltpu.roll(dy * s2, 32, 1)


def _gelu(z):
    return 0.5 * z * (1.0 + lax.erf(z * (1.0 / math.sqrt(2.0))))


def _gelu_and_grad(z):
    cdf = 0.5 * (1.0 + lax.erf(z * (1.0 / math.sqrt(2.0))))
    return z * cdf, cdf + z * jnp.exp(-0.5 * z * z) * (1.0 / math.sqrt(2.0 * math.pi))


def _att_scores(q, kv, kr, masked, transposed):
    k = jnp.concatenate([kv[:, :NOPE], kr], axis=1)
    if transposed:
        s = lax.dot_general(k, q, NT, preferred_element_type=F32)
    else:
        s = lax.dot_general(q, k, NT, preferred_element_type=F32)
    if masked:
        r = lax.broadcasted_iota(jnp.int32, s.shape, 0)
        c = lax.broadcasted_iota(jnp.int32, s.shape, 1)
        s = jnp.where((r <= c) if transposed else (c <= r), s, -jnp.inf)
    return s, k


def _in_pairs(lo, hi, pair, single):
    n = hi - lo

    def body(p, carry):
        pair(lo + 2 * p, lo + 2 * p + 1)
        return carry

    lax.fori_loop(0, n // 2, body, 0)

    @pl.when(n % 2 == 1)
    def _():
        single(hi - 1)


def _causal_tiles(i, pair, single):
    @pl.when(i == 0)
    def _():
        single(i, True)

    @pl.when(i > 0)
    def _():
        _in_pairs(0, i - 1, lambda a, b: pair(a, b, False), lambda a: single(a, False))
        pair(i - 1, i, True)


def _flash_fwd(q, kvb, krb, tables):
    s_len = q.shape[0]
    t = ATT_TILE

    def body(q_ref, cs_ref, s1_ref, s2_ref, kv_ref, kr_ref, o_ref, lse_ref, qb_ref, m_s, l_s, acc_s):
        qi = pl.program_id(1)
        m_s[...] = jnp.full_like(m_s, -jnp.inf)
        l_s[...] = jnp.zeros_like(l_s)
        acc_s[...] = jnp.zeros_like(acc_s)
        qv = q_ref[...]
        q = jnp.concatenate([qv[:, :NOPE], _rope(qv[:, NOPE:], cs_ref[...], s1_ref[...], s2_ref[...])], axis=1)
        q = (q * LOG2_SCALE).astype(BF16)
        qb_ref[...] = q

        def scores(ki, masked):
            rows = pl.ds(pl.multiple_of(ki * t, t), t)
            kv = kv_ref[rows, :]
            return _att_scores(q, kv, kr_ref[rows, :], masked, False)[0], kv

        def update(s, kv):
            m_prev = m_s[...]
            m_new = jnp.maximum(m_prev, jnp.max(s, axis=1, keepdims=True))
            alpha = jnp.exp2(m_prev - m_new)
            p = jnp.exp2(s - jnp.tile(m_new, (1, t // LANE)))
            l_s[...] = alpha * l_s[...] + jnp.sum(p, axis=1, keepdims=True)
            acc_s[...] = alpha * acc_s[...] + jnp.dot(p.astype(BF16), kv[:, NOPE:], preferred_element_type=F32)
            m_s[...] = m_new

        def pair(k0, k1, masked):
            first, second = scores(k0, False), scores(k1, masked)
            update(*first)
            update(*second)

        _causal_tiles(qi, pair, lambda ki, masked: update(*scores(ki, masked)))
        o_ref[...] = (acc_s[...] / l_s[...]).astype(o_ref.dtype)
        lse_ref[...] = (m_s[...] + jnp.log2(l_s[...]))[:, :1]

    table = pl.BlockSpec((t, LANE), lambda h, qi: (qi, 0))
    return pl.pallas_call(
        body, name="flash_fwd", grid=(HEADS, s_len // t),
        in_specs=[pl.BlockSpec((t, HEAD_PAD), lambda h, qi: (qi, h)), table, table, table,
                  pl.BlockSpec((s_len, HEAD_PAD), lambda h, qi: (0, h)),
                  pl.BlockSpec((s_len, LANE), lambda h, qi: (0, 0))],
        out_specs=[pl.BlockSpec((t, VHEAD), lambda h, qi: (qi, h)),
                   pl.BlockSpec((None, t, 1), lambda h, qi: (h, qi, 0)),
                   pl.BlockSpec((t, HEAD_PAD), lambda h, qi: (qi, h))],
        out_shape=[jax.ShapeDtypeStruct((s_len, HEADS * VHEAD), BF16),
                   jax.ShapeDtypeStruct((HEADS, s_len, 1), F32),
                   jax.ShapeDtypeStruct((s_len, HEADS * HEAD_PAD), BF16)],
        scratch_shapes=[pltpu.VMEM((t, LANE), F32), pltpu.VMEM((t, LANE), F32), pltpu.VMEM((t, VHEAD), F32)],
        compiler_params=_params(("parallel", "arbitrary")),
    )(q, *tables, kvb, krb)


def _flash_bwd(qb, kvb, krb, dob, lse_row, delta_row):
    s_len = qb.shape[0]
    t = ATT_TILE
    nq = s_len // t
    scale = QK_HEAD ** -0.5

    def body(q_ref, kv_ref, kr_ref, do_ref, lse_ref, dl_ref, dq_ref, dkv_ref, dkr_ref, dk_s, dv_s):
        ki = pl.program_id(1)

        @pl.when(ki == 0)
        def _():
            dq_ref[...] = jnp.zeros_like(dq_ref)

        dk_s[...] = jnp.zeros_like(dk_s)
        dv_s[...] = jnp.zeros_like(dv_s)
        kv, kr = kv_ref[...], kr_ref[...]

        def products(qi, masked):
            rows = pl.ds(pl.multiple_of(qi * t, t), t)
            q, do = q_ref[rows, :], do_ref[rows, :]
            st, k = _att_scores(q, kv, kr, masked, True)
            return st, lax.dot_general(kv[:, NOPE:], do, NT, preferred_element_type=F32), q, do, rows, k

        def update(st, dpt, q, do, rows, k):
            pt = jnp.exp2(st - lse_ref[:, rows])
            dv_s[...] += jnp.dot(pt.astype(BF16), do, preferred_element_type=F32)
            dst = (pt * (dpt - dl_ref[:, rows]) * scale).astype(BF16)
            dk_s[...] += jnp.dot(dst, q, preferred_element_type=F32)
            dq_ref[rows, :] += lax.dot_general(dst, k, TN, preferred_element_type=F32)

        def pair(q0, q1, masked):
            first, second = products(q0, masked), products(q1, False)
            update(*first)
            update(*second)

        @pl.when(ki == nq - 1)
        def _():
            update(*products(ki, True))

        @pl.when(ki < nq - 1)
        def _():
            pair(ki, ki + 1, True)
            _in_pairs(ki + 2, nq, lambda a, b: pair(a, b, False), lambda qi: update(*products(qi, False)))

        dk = dk_s[...] * (1.0 / LOG2_SCALE)
        dkv_ref[...] = jnp.concatenate([dk[:, :NOPE], dv_s[...]], axis=1).astype(dkv_ref.dtype)
        dkr_ref[...] = dk[:, NOPE:]

    row = pl.BlockSpec((None, 1, s_len), lambda h, ki: (h, 0, 0))
    return pl.pallas_call(
        body, name="flash_bwd", grid=(HEADS, nq),
        in_specs=[pl.BlockSpec((s_len, HEAD_PAD), lambda h, ki: (0, h)),
                  pl.BlockSpec((t, HEAD_PAD), lambda h, ki: (ki, h)),
                  pl.BlockSpec((t, LANE), lambda h, ki: (ki, 0)),
                  pl.BlockSpec((s_len, VHEAD), lambda h, ki: (0, h)), row, row],
        out_specs=[pl.BlockSpec((s_len, HEAD_PAD), lambda h, ki: (0, h)),
                   pl.BlockSpec((t, HEAD_PAD), lambda h, ki: (ki, h)),
                   pl.BlockSpec((t, LANE), lambda h, ki: (ki, h))],
        out_shape=[jax.ShapeDtypeStruct((s_len, HEADS * HEAD_PAD), F32),
                   jax.ShapeDtypeStruct((s_len, HEADS * HEAD_PAD), BF16),
                   jax.ShapeDtypeStruct((s_len, HEADS * LANE), F32)],
        scratch_shapes=[pltpu.VMEM((t, HEAD_PAD), F32), pltpu.VMEM((t, VHEAD), F32)],
        compiler_params=_params(("parallel", "arbitrary")),
    )(qb, kvb, krb, dob, lse_row, delta_row)


def _tril(w):
    r = lax.broadcasted_iota(jnp.int32, w.shape, 0)
    c = lax.broadcasted_iota(jnp.int32, w.shape, 1)
    return jnp.where(c <= r, w, 0.0)


def _sgu_row_stats(z_ref, width, gd, v_s, act, extra_s=None):
    total = None
    for g in range(SGU_GROUPS):
        cols = slice(g * gd, (g + 1) * gd)
        v = act(z_ref[:, width + g * gd:width + (g + 1) * gd])
        if extra_s is not None:
            v, extra_s[:, cols] = v
        v_s[:, cols] = v
        part = jnp.sum(v, axis=1, keepdims=True)
        total = part if total is None else total + part
    mean = total * (1.0 / width)
    sq = None
    for g in range(SGU_GROUPS):
        d = v_s[:, g * gd:(g + 1) * gd] - mean
        part = jnp.sum(d * d, axis=1, keepdims=True)
        sq = part if sq is None else sq + part
    return mean, lax.rsqrt(sq * (1.0 / width) + LN_EPS)


def _sgu_fwd(zpre, ln_g, ln_b, w_sp, bias_full):
    s_len, two_w = zpre.shape
    width = two_w // 2
    gd = width // SGU_GROUPS
    t = SGU_CHUNK

    def body(z_ref, g_ref, b_ref, w_ref, bias_ref, uv_ref, v_s):
        mean, rstd = _sgu_row_stats(z_ref, width, gd, v_s, _gelu)
        for g in range(SGU_GROUPS):
            cols = slice(g * gd, (g + 1) * gd)
            vln = ((v_s[:, cols] - mean) * rstd * g_ref[:, cols] + b_ref[:, cols]).astype(BF16)
            mixed = jnp.dot(_tril(w_ref[g]).astype(BF16), vln, preferred_element_type=F32) + bias_ref[:, cols]
            uv_ref[:, cols] = (_gelu(z_ref[:, cols]) * mixed).astype(uv_ref.dtype)

    return pl.pallas_call(
        body, name="sgu_fwd", grid=(s_len // t,),
        in_specs=[pl.BlockSpec((t, two_w), lambda i: (i, 0)), pl.BlockSpec((1, width), lambda i: (0, 0)),
                  pl.BlockSpec((1, width), lambda i: (0, 0)), pl.BlockSpec(w_sp.shape, lambda i: (0, 0, 0)),
                  pl.BlockSpec((t, width), lambda i: (0, 0))],
        out_specs=pl.BlockSpec((t, width), lambda i: (i, 0)),
        out_shape=jax.ShapeDtypeStruct((s_len, width), BF16),
        scratch_shapes=[pltpu.VMEM((t, width), F32)],
        compiler_params=_params(("parallel",)),
    )(zpre, ln_g, ln_b, w_sp, bias_full)


def _sgu_bwd(zpre, duv, ln_g, ln_b, w_sp, bias_full):
    s_len, two_w = zpre.shape
    width = two_w // 2
    gd = width // SGU_GROUPS
    t = SGU_CHUNK

    def body(z_ref, duv_ref, g_ref, b_ref, w_ref, bias_ref, dz_ref, dg_ref, db_ref, dw_ref, dbias_ref, v_s, vgrad_s, dvhat_s):
        @pl.when(pl.program_id(0) == 0)
        def _():
            for ref in (dg_ref, db_ref, dw_ref, dbias_ref):
                ref[...] = jnp.zeros_like(ref)

        def accumulate(ref, val):
            ref[...] += val

        mean, rstd = _sgu_row_stats(z_ref, width, gd, v_s, _gelu_and_grad, vgrad_s)
        sum_dvhat = sum_dvhat_vhat = None
        for g in range(SGU_GROUPS):
            cols = slice(g * gd, (g + 1) * gd)
            vhat = (v_s[:, cols] - mean) * rstd
            vln = (vhat * g_ref[:, cols] + b_ref[:, cols]).astype(BF16)
            wc = _tril(w_ref[g]).astype(BF16)
            mixed = jnp.dot(wc, vln, preferred_element_type=F32) + bias_ref[:, cols]
            u, u_grad = _gelu_and_grad(z_ref[:, cols])
            duv = duv_ref[:, cols]
            dz_ref[:, cols] = (duv * mixed * u_grad).astype(dz_ref.dtype)
            dmixed = duv * u
            dmb = dmixed.astype(BF16)
            dvln = lax.dot_general(wc, dmb, TN, preferred_element_type=F32)
            accumulate(dw_ref.at[g], _tril(lax.dot_general(dmb, vln, NT, preferred_element_type=F32)))
            accumulate(dbias_ref.at[:, cols], dmixed)
            accumulate(dg_ref.at[:, cols], jnp.sum(dvln * vhat, axis=0, keepdims=True))
            accumulate(db_ref.at[:, cols], jnp.sum(dvln, axis=0, keepdims=True))
            dvhat = dvln * g_ref[:, cols]
            dvhat_s[:, cols] = dvhat
            parts = jnp.sum(dvhat, axis=1, keepdims=True), jnp.sum(dvhat * vhat, axis=1, keepdims=True)
            sum_dvhat = parts[0] if sum_dvhat is None else sum_dvhat + parts[0]
            sum_dvhat_vhat = parts[1] if sum_dvhat_vhat is None else sum_dvhat_vhat + parts[1]
        mean_dvhat, mean_dvhat_vhat = sum_dvhat * (1.0 / width), sum_dvhat_vhat * (1.0 / width)
        for g in range(SGU_GROUPS):
            cols = slice(g * gd, (g + 1) * gd)
            vhat = (v_s[:, cols] - mean) * rstd
            dv0 = rstd * (dvhat_s[:, cols] - mean_dvhat - vhat * mean_dvhat_vhat)
            dz_ref[:, width + g * gd:width + (g + 1) * gd] = (dv0 * vgrad_s[:, cols]).astype(dz_ref.dtype)

    vec = pl.BlockSpec((1, width), lambda i: (0, 0))
    return pl.pallas_call(
        body, name="sgu_bwd", grid=(s_len // t,),
        in_specs=[pl.BlockSpec((t, two_w), lambda i: (i, 0)), pl.BlockSpec((t, width), lambda i: (i, 0)), vec, vec,
                  pl.BlockSpec(w_sp.shape, lambda i: (0, 0, 0)), pl.BlockSpec((t, width), lambda i: (0, 0))],
        out_specs=[pl.BlockSpec((t, two_w), lambda i: (i, 0)), vec, vec,
                   pl.BlockSpec(w_sp.shape, lambda i: (0, 0, 0)), pl.BlockSpec((t, width), lambda i: (0, 0))],
        out_shape=[jax.ShapeDtypeStruct((s_len, two_w), BF16), jax.ShapeDtypeStruct((1, width), F32),
                   jax.ShapeDtypeStruct((1, width), F32), jax.ShapeDtypeStruct(w_sp.shape, F32),
                   jax.ShapeDtypeStruct((t, width), F32)],
        scratch_shapes=[pltpu.VMEM((t, width), F32)] * 3,
        compiler_params=_params(("arbitrary",)),
    )(zpre, duv, ln_g, ln_b, w_sp, bias_full)


def _rope_tables(positions):
    inv_freq = ROPE_THETA ** (-jnp.arange(0, ROPE, 2, dtype=F32) / ROPE)
    ang = positions.astype(F32)[:, None] * inv_freq
    cos, sin = jnp.cos(ang), jnp.sin(ang)
    z32, z64 = jnp.zeros_like(cos), jnp.zeros((cos.shape[0], LANE - ROPE), F32)
    return (jnp.concatenate([cos, cos, z64], axis=1), jnp.concatenate([z32, sin, z64], axis=1),
            jnp.concatenate([-sin, z32, z64], axis=1))


def _ffn_fwd(x, g, w_up3, w_down3):
    def sq_relu(acc):
        r = jnp.maximum(acc, 0.0)
        return r * r, 2.0 * r

    r, r_grad, h2 = _mm_stacked("ffn_up", x, w_up3, "col", (BF16, BF16), sq_relu, norm_g=g)
    x_out = _mm_stacked("ffn_down", r, w_down3, "row", (F32,), lambda acc, res: (acc + res,), [x])[0]
    return x_out, (x, h2, r, r_grad)


def _ffn_bwd(dx, dxb, saved, g, w_up3, w_down3):
    x, h2, r, r_grad = saved
    da = _mm_stacked_nt("ffn_down_dx", dxb, w_down3, "row", (BF16,),
                        lambda acc, rg: (acc * rg.astype(F32),), [r_grad])[0]
    g_down = _mm_tn_stacked("ffn_down_dw", r, dxb, w_down3.shape, "row")
    dx, dxb, dg = _mm_stacked_nt("ffn_up_dx", da, w_up3, "col", **_norm_bwd(x, g, dx))
    g_up = _mm_tn_stacked("ffn_up_dw", h2, da, w_up3.shape, "col")
    return dx, dxb, dg, g_up, g_down


def _norm_bwd(x, g, dres):
    def epilogue(dh, xv, rv, gv):
        dxv, dg = _rms_bwd(dh, xv, gv)
        return dxv + rv, dxv + rv, dg

    return dict(out_dtypes=(F32, BF16), epilogue=epilogue, extras=[x, dres], consts=[g], sums=[g.shape], tm=MM_TILE // 2)


def _dot(a, b, dims=NN):
    return lax.dot_general(a.astype(BF16), b.astype(BF16), dims, preferred_element_type=F32)


def _mla_fwd(x, g, wdkv, q_norm, kv_norm, wq, wkv, wo, tables):
    d = x.shape[1]

    def project(xv, cs, s1, s2, gv, wdkv_v, qg, kg, wq_v, wkv_v):
        h = _rms_fwd(xv, gv).astype(BF16)
        lv = _dot(h, wdkv_v)
        cqn = _rms_fwd(lv[:, :Q_RANK], qg).astype(BF16)
        ckvn = _rms_fwd(lv[:, Q_RANK:Q_RANK + KV_RANK], kg).astype(BF16)
        return (h, lv, cqn, ckvn, _rope(lv[:, Q_RANK + KV_RANK:], cs, s1, s2), _dot(cqn, wq_v), _dot(ckvn, wkv_v))

    h, lat, cqn, ckvn, krb, q, kvb = _rowwise(
        "mla_project", project, [x, *tables], [g, wdkv, q_norm, kv_norm, wq, wkv],
        [(d, BF16), (LAT_PAD, F32), (Q_RANK, BF16), (KV_RANK, BF16), (LANE, BF16), (wq.shape[1], F32), (wkv.shape[1], BF16)],
        tr=512)
    ob, lse, qb = _flash_fwd(q, kvb, krb, tables)
    x_mid = _mm("mla_o", ob, wo, (F32,), lambda acc, res: (acc + res,), [x])[0]
    return x_mid, (x, h, lat, cqn, ckvn, krb, qb, kvb, ob, lse)


def _mla_bwd(dx, dxb, saved, g, wdkv, q_norm, kv_norm, wq, wkv, wo, tables):
    x, h, lat, cqn, ckvn, krb, qb, kvb, ob, lse = saved
    s_len = x.shape[0]
    def with_delta(do, ov):
        prod = do * ov.astype(F32)
        lane = lax.broadcasted_iota(jnp.int32, (do.shape[0], LANE), 1)
        by_lane = None
        for hd in range(HEADS):
            total = jnp.broadcast_to(jnp.sum(prod[:, hd * VHEAD:(hd + 1) * VHEAD], axis=1, keepdims=True), lane.shape)
            by_lane = jnp.where(lane == hd, total, 0.0 if by_lane is None else by_lane)
        return do, by_lane

    dob, delta_lanes = _mm("mla_o_dx", dxb, wo, (BF16,), with_delta, [ob], nt=True, tm=MM_TILE // 2, narrow=[(LANE, F32)])
    g_wo = _mm_tn("mla_o_dw", ob, dxb)[0]
    delta_row = delta_lanes[:, :HEADS].T.reshape(HEADS, 1, s_len)
    dq, dkvb, dkr = _flash_bwd(qb, kvb, krb, dob, lse.reshape(HEADS, 1, s_len), delta_row)

    def project_bwd(dq_f, dkv_v, dkr_v, lv, cqn_v, ckvn_v, h_v, xv, rv, cs, s1, s2, gv, qg, kg, wq_v, wkv_v, wdkv_v):
        parts = []
        for hd in range(HEADS):
            parts += [dq_f[:, hd * HEAD_PAD:hd * HEAD_PAD + NOPE],
                      _rope_t(dq_f[:, hd * HEAD_PAD + NOPE:(hd + 1) * HEAD_PAD], cs, s1, s2)]
        dq_v = jnp.concatenate(parts, axis=1).astype(BF16)
        dcq, dqg = _rms_bwd(_dot(dq_v, wq_v, NT), lv[:, :Q_RANK], qg)
        dckv, dkg = _rms_bwd(_dot(dkv_v, wkv_v, NT), lv[:, Q_RANK:Q_RANK + KV_RANK], kg)
        dkr_sum = dkr_v[:, :LANE]
        for hd in range(1, HEADS):
            dkr_sum = dkr_sum + dkr_v[:, hd * LANE:(hd + 1) * LANE]
        dlat = jnp.concatenate([dcq, dckv, _rope_t(dkr_sum, cs, s1, s2)], axis=1).astype(BF16)
        dxv, dg = _rms_bwd(_dot(dlat, wdkv_v, NT), xv, gv)
        return (dxv + rv, dxv + rv, _dot(cqn_v, dq_v, TN), _dot(ckvn_v, dkv_v, TN), _dot(h_v, dlat, TN), dqg, dkg, dg)

    dx, dxb, g_wq, g_wkv, g_wdkv, g_qn, g_kvn, dg = _rowwise(
        "mla_project_bwd", project_bwd, [dq, dkvb, dkr, lat, cqn, ckvn, h, x, dx, *tables],
        [g, q_norm, kv_norm, wq, wkv, wdkv], [(x.shape[1], F32), (x.shape[1], BF16)],
        [wq.shape, wkv.shape, wdkv.shape, q_norm.shape, kv_norm.shape, g.shape], tr=256)
    return dx, dxb, dg, g_wdkv, g_qn, g_kvn, g_wq, g_wkv, g_wo


def _sgu_layer_fwd(x, g, w_in3, ln_g, ln_b, w_sp, bias_full, w_out3):
    zpre, h = _mm_stacked("sgu_in", x, w_in3, "col", norm_g=g)
    uv = _sgu_fwd(zpre, ln_g, ln_b, w_sp, bias_full)
    x_mid = _mm_stacked("sgu_out", uv, w_out3, "row", (F32,), lambda acc, res: (acc + res,), [x])[0]
    return x_mid, (x, h, zpre, uv)


def _sgu_layer_bwd(dx, dxb, saved, g, w_in3, ln_g, ln_b, w_sp, bias_full, w_out3):
    x, h, zpre, uv = saved
    duv = _mm_stacked_nt("sgu_out_dx", dxb, w_out3, "row")[0]
    g_out = _mm_tn_stacked("sgu_out_dw", uv, dxb, w_out3.shape, "row")
    dz, g_lng, g_lnb, g_wsp, g_bias = _sgu_bwd(zpre, duv, ln_g, ln_b, w_sp, bias_full)
    g_in = _mm_tn_stacked("sgu_in_dw", h, dz, w_in3.shape, "col")
    dx, dxb, dg = _mm_stacked_nt("sgu_in_dx", dz, w_in3, "col", **_norm_bwd(x, g, dx))
    return dx, dxb, dg, g_in, g_out, g_lng, g_lnb, g_wsp, g_bias


def _loss_head(x, target, g):
    d = x.shape[1]

    def fn(xv, tv, gv):
        err = _rms_fwd(xv, gv) - tv
        dxv, dg = _rms_bwd(err * (1.0 / d), xv, gv)
        return dxv, dxv, dg, jnp.sum(err * err, axis=0, keepdims=True)

    return _rowwise("loss_head", fn, [x, target], [g], [(d, F32), (d, BF16)], [g.shape, g.shape])


def _mixer_weights(i, stacks):
    by_rows = lambda a: a.reshape(N_SHARDS * a.shape[1], a.shape[2])
    by_cols = lambda a: a.transpose(1, 0, 2).reshape(a.shape[1], N_SHARDS * a.shape[2])
    if i % 2:
        w_in3, w_out3, ln_g, ln_b = stacks
        return w_in3, ln_g.reshape(1, -1), ln_b.reshape(1, -1), w_out3
    wdkv = by_rows(stacks[0])
    wdkv = jnp.pad(wdkv, ((0, 0), (0, LAT_PAD - wdkv.shape[1])))
    wq = jnp.pad(by_cols(stacks[1]).reshape(Q_RANK, HEADS, QK_HEAD), ((0, 0), (0, 0), (0, HEAD_PAD - QK_HEAD)))
    return wdkv, wq.reshape(Q_RANK, HEADS * HEAD_PAD), by_cols(stacks[2]), by_rows(stacks[3])


def _local_step(x, positions, target, norm_mix, norm_ffn, final_norm, q_norm, kv_norm, w_sp, b_sp, mixers, ffn, reducer):
    tables = _rope_tables(positions)
    gd = mixers[1][2].size // SGU_GROUPS
    bias_full = [jnp.repeat(b_sp[j].T, gd, axis=1) for j in range(DEPTH // 2)]
    saved, mla, sgu = [], [None] * (DEPTH // 2), [None] * (DEPTH // 2)
    for i in range(DEPTH):
        j = i // 2
        x, *stacks = lax.optimization_barrier((x, *mixers[i]))
        if i % 2 == 0:
            wdkv, wq, wkv, wo = mla[j] = _mixer_weights(i, stacks)
            x, s_mix = _mla_fwd(x, norm_mix[i:i + 1], wdkv, q_norm[j:j + 1], kv_norm[j:j + 1], wq, wkv, wo, tables)
        else:
            w_in3, ln_g, ln_b, w_out3 = sgu[j] = _mixer_weights(i, stacks)
            x, s_mix = _sgu_layer_fwd(x, norm_mix[i:i + 1], w_in3, ln_g, ln_b, w_sp[j], bias_full[j], w_out3)
        x, s_ffn = _ffn_fwd(x, norm_ffn[i:i + 1], *ffn[i])
        saved.append((s_mix, s_ffn))
    dx, dxb, g_final, sq_cols = _loss_head(x, target, final_norm[None, :])
    loss = 0.5 * jnp.sum(sq_cols) / x.shape[1]

    def pair(g):
        return g, g.astype(BF16)

    g_mix, g_ffn = [None] * DEPTH, [None] * DEPTH
    mla_g, sgu_g = [None] * (DEPTH // 2), [None] * (DEPTH // 2)
    for i in reversed(range(DEPTH)):
        j = i // 2
        s_mix, s_ffn = saved[i]
        dx, dxb, g_ffn[i], g_up, g_down = _ffn_bwd(dx, dxb, s_ffn, norm_ffn[i:i + 1], *ffn[i])
        dxb = reducer.add(f"ffn{i}", i, {"ffn_w_up": g_up, "ffn_w_down": g_down}, dxb)
        dxb = reducer.phase_end(dxb)
        if i % 2 == 0:
            wdkv, wq, wkv, wo = mla[j]
            dx, dxb, g_mix[i], g_wdkv, g_qn, g_kvn, g_wq, g_wkv, g_wo = _mla_bwd(
                dx, dxb, s_mix, norm_mix[i:i + 1], wdkv, q_norm[j:j + 1], kv_norm[j:j + 1], wq, wkv, wo, tables)
            mla_g[j] = (g_qn, g_kvn)
            g_wq = g_wq.reshape(Q_RANK, HEADS, HEAD_PAD)[..., :QK_HEAD].reshape(Q_RANK, N_SHARDS, -1)
            dxb = reducer.add(f"mla{j}", j, {
                "mla_w_dkv": pair(g_wdkv[:, :Q_RANK + KV_RANK + ROPE].reshape(N_SHARDS, -1, Q_RANK + KV_RANK + ROPE)),
                "mla_w_uq": pair(g_wq.transpose(1, 0, 2)),
                "mla_w_ukv": pair(g_wkv.reshape(KV_RANK, N_SHARDS, -1).transpose(1, 0, 2)),
                "mla_w_o": pair(g_wo.reshape(N_SHARDS, -1, g_wo.shape[1]))}, dxb)
        else:
            w_in3, ln_g, ln_b, w_out3 = sgu[j]
            dx, dxb, g_mix[i], g_in, g_out, g_lng, g_lnb, g_wsp, g_bias = _sgu_layer_bwd(
                dx, dxb, s_mix, norm_mix[i:i + 1], w_in3, ln_g, ln_b, w_sp[j], bias_full[j], w_out3)
            sgu_g[j] = (g_wsp, g_bias.reshape(SGU_CHUNK, SGU_GROUPS, gd).sum(axis=-1).T)
            dxb = reducer.add(f"sgu{j}", j, {"sgu_w_in": g_in, "sgu_w_out": g_out,
                                             "sgu_ln_g": pair(g_lng.reshape(N_SHARDS, -1, LANE)),
                                             "sgu_ln_b": pair(g_lnb.reshape(N_SHARDS, -1, LANE))}, dxb)
        dxb = reducer.phase_end(dxb)
    small = dict(
        norm_mix=jnp.concatenate(g_mix, axis=0), norm_ffn=jnp.concatenate(g_ffn, axis=0), final_norm=g_final[0],
        q_norm=jnp.concatenate([m[0] for m in mla_g], axis=0), kv_norm=jnp.concatenate([m[1] for m in mla_g], axis=0),
        w_sp=jnp.stack([s[0] for s in sgu_g]), b_sp=jnp.stack([s[1] for s in sgu_g]))
    return loss, dx, small


HBM_SPEC = pl.BlockSpec(memory_space=pltpu.HBM)


def _place():
    x, y, c = lax.axis_index("x"), lax.axis_index("y"), lax.axis_index("c")
    return x, y, c, [(1 - x, y), (x, 1 - y), (1 - x, 1 - y)]


def _remote(src, dst, send_sems, recv_sems, k, to):
    return pltpu.make_async_remote_copy(src_ref=src, dst_ref=dst, send_sem=send_sems.at[k], recv_sem=recv_sems.at[k],
                                        device_id=to, device_id_type=MESH)


def _gather_layer(tag, shards):
    n = len(shards)
    split = [s.shape[0] >= 16 for s in shards]

    def body(*refs):
        ins, outs = refs[:n], refs[n:2 * n]
        send_sems, recv_sems, local_sems = refs[2 * n:]
        x, y, c, chips = _place()
        mine = 2 * x + y
        barrier = pltpu.get_barrier_semaphore()
        peers = [(x, y, 1 - c)] + [(*chip, c) for chip in chips]
        for peer in peers:
            pl.semaphore_signal(barrier, inc=1, device_id=peer, device_id_type=MESH)
        pl.semaphore_wait(barrier, len(peers))

        def rows(t, half):
            hr = shards[t].shape[0] // 2
            return pl.ds(half * hr, hr) if split[t] else pl.ds(0, shards[t].shape[0])

        local, sent = [], []
        for t in range(n):
            local.append(pltpu.make_async_copy(ins[t], outs[t].at[mine], local_sems.at[t]))
            local[-1].start()
            for j, chip in enumerate(chips):
                cp = _remote(ins[t].at[rows(t, c)], outs[t].at[mine, rows(t, c)], send_sems, recv_sems, 3 * t + j, (*chip, c))
                cp.start()
                sent.append(cp)
        for j, chip in enumerate(chips):
            theirs = 2 * chip[0] + chip[1]
            for t in range(n):
                piece = outs[t].at[theirs, rows(t, c)]
                _remote(piece, piece, send_sems, recv_sems, 3 * t + j, (x, y, c)).wait_recv()
                if split[t]:
                    cp = _remote(piece, piece, send_sems, recv_sems, 3 * n + 3 * t + j, (x, y, 1 - c))
                    cp.start()
                    sent.append(cp)
        for j, chip in enumerate(chips):
            theirs = 2 * chip[0] + chip[1]
            for t in range(n):
                if split[t]:
                    piece = outs[t].at[theirs, rows(t, 1 - c)]
                    _remote(piece, piece, send_sems, recv_sems, 3 * n + 3 * t + j, (x, y, c)).wait_recv()
        for cp in sent:
            cp.wait_send()
        for cp in local:
            cp.wait()

    return pl.kernel(
        body, name=f"gather_{tag}", mesh=plsc.ScalarSubcoreMesh(axis_name="sequencer", num_cores=1),
        out_type=[jax.ShapeDtypeStruct((N_SHARDS, *s.shape), s.dtype) for s in shards],
        scratch_types=[pltpu.SemaphoreType.DMA((6 * n,)), pltpu.SemaphoreType.DMA((6 * n,)), pltpu.SemaphoreType.DMA((n,))],
        compiler_params=pltpu.CompilerParams(collective_id=ID_GATHER),
    )(*shards)


SEQUENCER = dict(axis_name="sequencer", num_cores=1)
ID_GATHER, ID_EXCHANGE, ID_SHARE = 0, 1, 2
MIN_SPLIT_ROWS = 16


def _handshake(peers):
    barrier = pltpu.get_barrier_semaphore()
    for peer in peers:
        pl.semaphore_signal(barrier, inc=1, device_id=peer, device_id_type=MESH)
    pl.semaphore_wait(barrier, len(peers))


def _half_rows(rows, half):
    return pl.ds(half * (rows // 2), rows // 2) if rows >= MIN_SPLIT_ROWS else pl.ds(0, rows)


SEM_SPEC = pl.BlockSpec(memory_space=pltpu.SEMAPHORE)
DATAFLOW = pltpu.SideEffectType.DATAFLOW_SIDE_EFFECTING


def _exchange_copies(shapes, stacks, lands, send_sems, recv_sems):
    x, y, c, chips = _place()
    mine = 2 * x + y
    copies = []
    for t, shape in enumerate(shapes):
        r = shape[1]
        copies.append(_remote(stacks[t].at[mine, _half_rows(r, 1 - c)], lands[t].at[0], send_sems, recv_sems, 7 * t, (x, y, 1 - c)))
        for j, chip in enumerate(chips):
            theirs = 2 * chip[0] + chip[1]
            copies.append(_remote(stacks[t].at[theirs, _half_rows(r, c)], lands[t].at[1 + j], send_sems, recv_sems,
                                  7 * t + 1 + j, (*chip, c)))
            copies.append(_remote(stacks[t].at[theirs, _half_rows(r, 1 - c)], lands[t].at[4 + j], send_sems, recv_sems,
                                  7 * t + 4 + j, (*chip, 1 - c)))
    return copies


def _exchange_start(tag, stacks, carry):
    n = len(stacks)
    shapes = [s.shape for s in stacks]
    lands = [lax.empty((7, s.shape[1] // 2 if s.shape[1] >= MIN_SPLIT_ROWS else s.shape[1], s.shape[2]), s.dtype) for s in stacks]

    def body(*refs):
        for cp in _exchange_copies(shapes, refs[:n], refs[n:2 * n], refs[2 * n + 1], refs[2 * n + 2]):
            cp.start()

    through = (*stacks, *lands, carry)
    out = pl.pallas_call(
        body, name=f"reduce_exchange_start_{tag}",
        out_shape=(pltpu.SemaphoreType.DMA((7 * n,)), pltpu.SemaphoreType.DMA((7 * n,)),
                   *[pltpu.HBM(a.shape, a.dtype) for a in through]),
        in_specs=[HBM_SPEC] * (2 * n + 1),
        out_specs=(SEM_SPEC, SEM_SPEC, *[HBM_SPEC] * (2 * n + 1)),
        input_output_aliases={t: 2 + t for t in range(2 * n + 1)},
        compiler_params=pltpu.CompilerParams(has_side_effects=DATAFLOW),
    )(*[pltpu.with_memory_space_constraint(a, pltpu.HBM) for a in through])
    return out[0], out[1], out[2:2 + n], out[2 + n:2 + 2 * n], out[-1]


def _exchange_wait(tag, send_sems, recv_sems, stacks, lands, after):
    n = len(stacks)
    shapes = [s.shape for s in stacks]

    def body(*refs):
        for cp in _exchange_copies(shapes, refs[:n], refs[n:2 * n], refs[2 * n], refs[2 * n + 1]):
            cp.wait()

    out = pl.pallas_call(
        body, name=f"reduce_exchange_wait_{tag}",
        out_shape=tuple(pltpu.HBM(a.shape, a.dtype) for a in (*stacks, *lands)),
        in_specs=[HBM_SPEC] * (2 * n) + [SEM_SPEC, SEM_SPEC, pl.BlockSpec(memory_space=pl.ANY)],
        out_specs=tuple([HBM_SPEC] * (2 * n)),
        input_output_aliases={t: t for t in range(2 * n)},
        compiler_params=pltpu.CompilerParams(has_side_effects=DATAFLOW),
    )(*stacks, *lands, send_sems, recv_sems, after)
    return out[:n], out[n:]


def _share_halves(tag, halves):
    n = len(halves)

    def body(*refs):
        ins, outs, send_sems, recv_sems = refs[:n], refs[n:2 * n], refs[2 * n], refs[2 * n + 1]
        x, y, c, _ = _place()
        _handshake([(x, y, 1 - c)])
        sent = [_remote(ins[t], outs[t], send_sems, recv_sems, t, (x, y, 1 - c)) for t in range(n)]
        for cp in sent:
            cp.start()
        for cp in sent:
            cp.wait()

    return pl.kernel(
        body, name=f"reduce_share_{tag}", mesh=plsc.ScalarSubcoreMesh(**SEQUENCER),
        out_type=[jax.ShapeDtypeStruct(h.shape, h.dtype) for h in halves],
        scratch_types=[pltpu.SemaphoreType.DMA((n,)), pltpu.SemaphoreType.DMA((n,))],
        compiler_params=pltpu.CompilerParams(collective_id=ID_SHARE),
    )(*halves)


def _all_reduce_small(part):
    rows = part.shape[0]
    half = rows // 2

    def body(p_ref, out_ref, sib_buf, chip_sums, send_sems, recv_sems):
        x, y, c, chips = _place()
        mine = 2 * x + y
        my_rows = pl.ds(pl.multiple_of(c * half, 8), half)
        swap = _remote(p_ref, sib_buf, send_sems, recv_sems, 0, (x, y, 1 - c))
        swap.start()
        swap.wait()
        chip_sums[mine] = p_ref[...] + sib_buf[...]
        sent = [_remote(chip_sums.at[mine, my_rows], chip_sums.at[mine, my_rows], send_sems, recv_sems, 1 + j, (*chip, c))
                for j, chip in enumerate(chips)]
        for cp in sent:
            cp.start()
        for j, chip in enumerate(chips):
            sent[j].wait_send()
            theirs = chip_sums.at[2 * chip[0] + chip[1], my_rows]
            _remote(theirs, theirs, send_sems, recv_sems, 1 + j, (x, y, c)).wait_recv()
        out_ref[my_rows, :] = ((chip_sums[0, my_rows, :] + chip_sums[1, my_rows, :]) + chip_sums[2, my_rows, :]) + chip_sums[3, my_rows, :]
        share = _remote(out_ref.at[my_rows], out_ref.at[my_rows], send_sems, recv_sems, 4, (x, y, 1 - c))
        share.start()
        share.wait_send()
        other = out_ref.at[pl.ds(pl.multiple_of((1 - c) * half, 8), half)]
        _remote(other, other, send_sems, recv_sems, 4, (x, y, c)).wait_recv()

    vmem = pl.BlockSpec(memory_space=pltpu.VMEM)
    return pl.pallas_call(
        body, name="all_reduce_small", in_specs=[vmem], out_specs=vmem, out_shape=jax.ShapeDtypeStruct(part.shape, F32),
        scratch_shapes=[pltpu.VMEM((rows, LANE), F32), pltpu.VMEM((N_SHARDS, rows, LANE), F32),
                        pltpu.SemaphoreType.DMA((5,)), pltpu.SemaphoreType.DMA((5,))],
        compiler_params=pltpu.CompilerParams(vmem_limit_bytes=VMEM_LIMIT_BYTES),
    )(part)


def _sum_partials(g3, others, sel):
    _, rows, c = others.shape
    whole = g3.shape[1] == rows
    tr = _tile(rows, UPDATE_ROWS)
    nb = rows // tr

    def body(sel_ref, g_ref, *rest):
        same = g_ref[...].astype(F32)
        for ref in rest[1:4]:
            same = same + ref[...].astype(F32)
        other = rest[0][...].astype(F32)
        for ref in rest[4:7]:
            other = other + ref[...].astype(F32)
        rest[7][...] = same + other

    blk = (None, tr, c)
    slots = [pl.BlockSpec(blk, functools.partial(lambda i, sr, k: (k, i, 0), k=k)) for k in range(7)]
    return pl.pallas_call(
        body, name="reduce_sum_partials",
        grid_spec=pltpu.PrefetchScalarGridSpec(
            num_scalar_prefetch=1, grid=(nb,),
            in_specs=[pl.BlockSpec(blk, lambda i, sr: (sr[0], (0 if whole else sr[1] * nb) + i, 0))] + slots,
            out_specs=pl.BlockSpec((tr, c), lambda i, sr: (i, 0))),
        out_shape=jax.ShapeDtypeStruct((rows, c), F32),
        compiler_params=_params(("parallel",)),
    )(sel, g3, *[others] * 7)


def _adamw_math(w, g, m, v):
    nm = ADAM_B1 * m + (1.0 - ADAM_B1) * g
    nv = ADAM_B2 * v + (1.0 - ADAM_B2) * (g * g)
    m_hat = nm / (1.0 - ADAM_B1 ** ADAM_STEP)
    v_hat = nv / (1.0 - ADAM_B2 ** ADAM_STEP)
    return -ADAM_LR * (m_hat / (jnp.sqrt(v_hat) + ADAM_EPS) + ADAM_WD * w), nm, nv


def _adamw_layer(layer, w, m, v, g_mine, g_sibling, sel, prev):
    lyr, r, c = w.shape
    rows = g_mine.shape[0]
    halves = r // rows
    tr = _tile(rows, 512)
    nb = rows // tr
    n_g = 1 if g_sibling is None else 2

    def body(sel_ref, w_ref, m_ref, v_ref, *rest):
        g = rest[0][...]
        if n_g == 2:
            g = jnp.where(pl.program_id(0) == sel_ref[1], g, rest[1][...])
        outs = rest[n_g + (0 if prev is None else 4):]
        d, nm, nv = _adamw_math(w_ref[...], g, m_ref[...], v_ref[...])
        for ref, val in zip(outs, (g, d, nm, nv)):
            ref[...] = val

    full = pl.BlockSpec((None, tr, c), lambda h, i, sr: (layer, h * nb + i, 0))
    part = pl.BlockSpec((tr, c), lambda h, i, sr: (i, 0))
    n_in = 4 + n_g
    return pl.pallas_call(
        body, name="adamw_layer",
        grid_spec=pltpu.PrefetchScalarGridSpec(
            num_scalar_prefetch=1, grid=(halves, nb),
            in_specs=[full] * 3 + [part] * n_g + ([] if prev is None else [pl.BlockSpec(memory_space=pl.ANY)] * 4),
            out_specs=[full] * 4),
        out_shape=[jax.ShapeDtypeStruct(w.shape, F32)] * 4,
        input_output_aliases={} if prev is None else {n_in + k: k for k in range(4)},
        compiler_params=_params(("parallel", "parallel")),
    )(sel, w, m, v, g_mine, *([] if g_sibling is None else [g_sibling]), *([] if prev is None else prev))


def _adamw(w, g, m, v):
    lyr, r, c = w.shape
    tr = _tile(r, 256)

    def body(w_ref, g_ref, m_ref, v_ref, d_ref, nm_ref, nv_ref):
        d_ref[...], nm_ref[...], nv_ref[...] = _adamw_math(w_ref[...], g_ref[...], m_ref[...], v_ref[...])

    blk = pl.BlockSpec((None, tr, c), lambda l, i: (l, i, 0))
    return pl.pallas_call(
        body, name="adamw", grid=(lyr, r // tr), in_specs=[blk] * 4, out_specs=[blk] * 3,
        out_shape=[jax.ShapeDtypeStruct(w.shape, F32)] * 3,
        compiler_params=_params(("parallel", "parallel")),
    )(w, g, m, v)


SHARDED = ("mla_w_dkv", "mla_w_uq", "mla_w_ukv", "mla_w_o", "sgu_w_in", "sgu_ln_g", "sgu_ln_b", "sgu_w_out",
           "ffn_w_up", "ffn_w_down")
REPLICATED = ("norm_mix", "norm_ffn", "final_norm", "mla_q_norm", "mla_kv_norm", "sgu_w_spatial", "sgu_b_spatial")
WEIGHTS = ("norm_mix", "norm_ffn", "final_norm", "mla_w_dkv", "mla_q_norm", "mla_kv_norm", "mla_w_uq", "mla_w_ukv",
           "mla_w_o", "sgu_w_in", "sgu_ln_g", "sgu_ln_b", "sgu_w_spatial", "sgu_b_spatial", "sgu_w_out", "ffn_w_up",
           "ffn_w_down")


class _Reducer:
    def __init__(self, state, sel):
        self.state, self.sel = state, sel
        self.started, self.travelling, self.summed = [], [], []
        self.done = {}

    def add(self, tag, layer, grads, token):
        names = list(grads)
        token, *tied = lax.optimization_barrier((token, *[a for n in names for a in grads[n]]))
        f32s, bf16s = tied[0::2], tied[1::2]
        *flying, token = _exchange_start(tag, bf16s, token)
        self.started.append((tag, layer, names, f32s, flying))
        return token

    def phase_end(self, token):
        for tag, layer, names, f32s, flying in self.travelling:
            bf16s, received = _exchange_wait(tag, *flying, token)
            own = [g if g.shape[1] >= MIN_SPLIT_ROWS else gb for g, gb in zip(f32s, bf16s)]
            mine = [_sum_partials(g, got, self.sel) for g, got in zip(own, received)]
            token, *mine = lax.optimization_barrier((token, *mine))
            cut = [k for k, g in enumerate(own) if g.shape[1] >= MIN_SPLIT_ROWS]
            theirs = dict(zip(cut, _share_halves(tag, [mine[k] for k in cut])))
            self.summed.append((layer, names, mine, [theirs.get(k) for k in range(len(names))]))
        self.travelling, self.started = self.started, []
        return token

    def update(self, token):
        for layer, names, mine, theirs in self.summed:
            for name, g_mine, g_theirs in zip(names, mine, theirs):
                w, m, v = self.state[name]
                self.done[name] = _adamw_layer(layer, w, m, v, g_mine, g_theirs, self.sel, self.done.get(name))
                token = self.done[name][1]
        self.summed = []
        return token


def _as3d(name, a):
    return a.reshape(a.shape[0], -1, LANE) if name in ("sgu_ln_g", "sgu_ln_b") else a


def _pack(parts):
    flat = jnp.concatenate([p.reshape(-1) for p in parts])
    rows = -(-flat.shape[0] // (256 * LANE)) * 256
    return jnp.pad(flat, (0, rows * LANE - flat.shape[0])).reshape(rows, LANE)


def _unpack(packed, like):
    flat, out, at = packed.reshape(-1), [], 0
    for p in like:
        out.append(flat[at:at + p.size].reshape(p.shape))
        at += p.size
    return out


def kernel(x, positions, norm_mix, norm_ffn, final_norm, mla_w_dkv, mla_q_norm, mla_kv_norm, mla_w_uq, mla_w_ukv, mla_w_o, sgu_w_in, sgu_ln_g, sgu_ln_b, sgu_w_spatial, sgu_b_spatial, sgu_w_out, ffn_w_up, ffn_w_down, loss_target, m_norm_mix, m_norm_ffn, m_final_norm, m_mla_w_dkv, m_mla_q_norm, m_mla_kv_norm, m_mla_w_uq, m_mla_w_ukv, m_mla_w_o, m_sgu_w_in, m_sgu_ln_g, m_sgu_ln_b, m_sgu_w_spatial, m_sgu_b_spatial, m_sgu_w_out, m_ffn_w_up, m_ffn_w_down, v_norm_mix, v_norm_ffn, v_final_norm, v_mla_w_dkv, v_mla_q_norm, v_mla_kv_norm, v_mla_w_uq, v_mla_w_ukv, v_mla_w_o, v_sgu_w_in, v_sgu_ln_g, v_sgu_ln_b, v_sgu_w_spatial, v_sgu_b_spatial, v_sgu_w_out, v_ffn_w_up, v_ffn_w_down):
    given = dict(locals())
    w = {n: given[n] for n in WEIGHTS}
    mom = {n: given["m_" + n] for n in WEIGHTS}
    var = {n: given["v_" + n] for n in WEIGHTS}
    mixers, ffn, token = [], [], None
    for i in range(DEPTH):
        j = i // 2
        if i % 2 == 0:
            mixer = [w[n][j].astype(BF16) for n in ("mla_w_dkv", "mla_w_uq", "mla_w_ukv", "mla_w_o")]
        else:
            mixer = [sgu_w_in[j].astype(BF16), sgu_w_out[j].astype(BF16), sgu_ln_g[j].reshape(-1, LANE),
                     sgu_ln_b[j].reshape(-1, LANE)]
        for tag, shards, into in ((f"mixer{i}", mixer, mixers), (f"ffn{i}", [ffn_w_up[i].astype(BF16), ffn_w_down[i].astype(BF16)], ffn)):
            if token is None:
                token = shards[0]
            else:
                token, *shards = lax.optimization_barrier((token, *shards))
            into.append(_gather_layer(tag, shards))

    x_i, y_i, c_i = lax.axis_index("x"), lax.axis_index("y"), lax.axis_index("c")
    sel = jnp.stack([2 * x_i + y_i, c_i]).astype(jnp.int32)
    reducer = _Reducer({n: tuple(_as3d(n, d[n]) for d in (w, mom, var)) for n in SHARDED}, sel)
    loss, dx, small = _local_step(
        x[0], positions[0], loss_target[0], norm_mix, norm_ffn, final_norm, mla_q_norm, mla_kv_norm, sgu_w_spatial,
        sgu_b_spatial, mixers, ffn, reducer)
    loss = lax.psum(loss, ("x", "y", "c"))

    small_g = [small["norm_mix"], small["norm_ffn"], small["final_norm"], small["q_norm"], small["kv_norm"],
               small["w_sp"], small["b_sp"]]
    like = [w[n] for n in REPLICATED]
    g_small = _all_reduce_small(_pack(small_g))
    packed = [_pack([d[n] for n in REPLICATED])[None] for d in (w, mom, var)]
    upd_small = _adamw(packed[0], g_small[None], packed[1], packed[2])
    grads = dict(zip(REPLICATED, _unpack(g_small, like)))
    delta, new_m, new_v = ({n: a for n, a in zip(REPLICATED, _unpack(u[0], like))} for u in upd_small)

    reducer.phase_end(reducer.update(upd_small[0]))
    reducer.update(None)
    for n in SHARDED:
        grads[n], delta[n], new_m[n], new_v[n] = (a.reshape(w[n].shape) for a in reducer.done[n])

    return (loss, dx[None], *[grads[n] for n in WEIGHTS], *[delta[n] for n in WEIGHTS],
            *[new_m[n] for n in WEIGHTS], *[new_v[n] for n in WEIGHTS])
```

```python
import functools
import math

import jax
import jax.numpy as jnp
from jax import lax
from jax.experimental import pallas as pl
from jax.experimental.pallas import tpu as pltpu
from jax.experimental.pallas import tpu_sc as plsc

F32 = jnp.float32
BF16 = jnp.bfloat16
MESH = pl.DeviceIdType.MESH

DEPTH = 4
HEADS = 8
NOPE = 128
ROPE = 64
VHEAD = 128
QK_HEAD = NOPE + ROPE
Q_RANK = 256
KV_RANK = 128
HEAD_PAD = 256
LAT_PAD = 512
ROPE_THETA = 10000.0
SGU_CHUNK = 128
SGU_GROUPS = 8
NORM_EPS = 1e-6
LN_EPS = 1e-5
ADAM_LR, ADAM_B1, ADAM_B2, ADAM_EPS, ADAM_WD, ADAM_STEP = 0.001, 0.9, 0.999, 1e-08, 0.01, 10

N_SHARDS = 4
LANE = 128
VMEM_LIMIT_BYTES = 56 * 1024 * 1024
ATT_TILE = 512
MM_TILE = 1024
UPDATE_ROWS = 128
ATT_SCALE = QK_HEAD ** -0.5
LOG2_SCALE = ATT_SCALE * math.log2(math.e)

NN = (((1,), (0,)), ((), ()))
NT = (((1,), (1,)), ((), ()))
TN = (((0,), (0,)), ((), ()))


def _params(sem):
    return pltpu.CompilerParams(dimension_semantics=sem, vmem_limit_bytes=VMEM_LIMIT_BYTES)


def _tile(n, pref):
    t = min(n, pref)
    while n % t:
        t //= 2
    return t


def _matmul(name, a, b, a_spec, b_spec, dims, grid, tile, outs, extras=(), epilogue=None, sums=(), norm_g=None):
    nk, ne, no = grid[2], len(extras), len(outs)
    b_specs = list(b_spec) if isinstance(b_spec, (list, tuple)) else [b_spec]
    nb = len(b_specs)
    normed = norm_g is not None
    assert not normed or (nk == 1 and nb == 1)

    def body(a_ref, *rest):
        b_refs, e_refs, o_refs = rest[:nb], rest[nb:nb + ne], rest[nb + ne + normed:nb + ne + normed + no]
        s_refs = rest[nb + ne + normed + no:nb + ne + normed + no + len(sums)]
        if normed:
            g_ref, h_ref, h_s = rest[nb + ne], rest[nb + ne + 1 + no + len(sums)], rest[-1]

            @pl.when(pl.program_id(1) == 0)
            def _():
                h_s[...] = _rms_fwd(a_ref[...], g_ref[...]).astype(BF16)
                h_ref[...] = h_s[...]

            a_ref = h_s
        kw = a_ref.shape[1] // nb
        part = None
        for p, b_ref in enumerate(b_refs):
            a_tile = a_ref[...] if nb == 1 else a_ref[:, p * kw:(p + 1) * kw]
            d = lax.dot_general(a_tile.astype(BF16), b_ref[...].astype(BF16), dims, preferred_element_type=F32)
            part = d if part is None else part + d

        def finish(acc):
            vals = (acc,) if epilogue is None else epilogue(acc, *[e[...] for e in e_refs])
            for o_ref, v in zip(o_refs, vals):
                o_ref[...] = v.astype(o_ref.dtype)
            first = pl.program_id(0) == 0
            for s_ref, v in zip(s_refs, vals[no:]):
                @pl.when(first)
                def _():
                    s_ref[...] = v

                @pl.when(jnp.logical_not(first))
                def _():
                    s_ref[...] += v

        if nk == 1:
            finish(part)
            return
        acc_ref, k = rest[-1], pl.program_id(2)

        @pl.when(k == 0)
        def _():
            acc_ref[...] = part

        @pl.when(jnp.logical_and(k > 0, k < nk - 1))
        def _():
            acc_ref[...] += part

        @pl.when(k == nk - 1)
        def _():
            finish(acc_ref[...] + part)

    assert not sums or (grid[1] == 1 and nk == 1)
    a_block = a_spec.block_shape
    return pl.pallas_call(
        body, name=name, grid=grid,
        in_specs=[a_spec] + b_specs + [s for _, s in extras]
        + ([pl.BlockSpec(norm_g.shape, lambda i, j, k: (0, 0))] if normed else []),
        out_specs=[s for _, s in outs] + [pl.BlockSpec(s, lambda i, j, k: (0,) * len(s)) for s in sums]
        + ([pl.BlockSpec(a_block, lambda i, j, k: (i, 0))] if normed else []),
        out_shape=[s for s, _ in outs] + [jax.ShapeDtypeStruct(s, F32) for s in sums]
        + ([jax.ShapeDtypeStruct(a.shape, BF16)] if normed else []),
        scratch_shapes=([pltpu.VMEM(tile, F32)] if nk > 1 else []) + ([pltpu.VMEM(a_block, BF16)] if normed else []),
        compiler_params=_params(("arbitrary" if sums else "parallel", "arbitrary" if normed else "parallel", "arbitrary")),
    )(a, *[b] * nb, *[e for e, _ in extras], *([norm_g] if normed else []))


def _epilogue_operands(extras, consts, o_spec):
    return [(e, o_spec) for e in extras] + [(c, pl.BlockSpec(c.shape, lambda i, j, k: (0, 0))) for c in consts]


def _mm(name, a, b, out_dtypes=(F32,), epilogue=None, extras=(), tm=MM_TILE, tn=MM_TILE, tk=MM_TILE, nt=False,
        consts=(), sums=(), narrow=()):
    m, kd = a.shape
    n = b.shape[0] if nt else b.shape[1]
    tm, tn, tk = _tile(m, tm), _tile(n, tn), _tile(kd, tk)
    o_spec = pl.BlockSpec((tm, tn), lambda i, j, k: (i, j))
    b_spec = pl.BlockSpec((tn, tk), lambda i, j, k: (j, k)) if nt else pl.BlockSpec((tk, tn), lambda i, j, k: (k, j))
    assert not narrow or n == tn
    outs = [(jax.ShapeDtypeStruct((m, n), d), o_spec) for d in out_dtypes]
    outs += [(jax.ShapeDtypeStruct((m, w), d), pl.BlockSpec((tm, w), lambda i, j, k: (i, 0))) for w, d in narrow]
    return _matmul(name, a, b, pl.BlockSpec((tm, tk), lambda i, j, k: (i, k)), b_spec, NT if nt else NN,
                   (m // tm, n // tn, kd // tk), (tm, tn), outs,
                   _epilogue_operands(extras, consts, o_spec), epilogue, sums)


def _mm_tn(name, a, b, out_dtypes=(F32,), tm=MM_TILE, tn=MM_TILE, tk=MM_TILE):
    s, m = a.shape
    n = b.shape[1]
    tm, tn, tk = _tile(m, tm), _tile(n, tn), _tile(s, tk)
    o_spec = pl.BlockSpec((tm, tn), lambda i, j, k: (i, j))
    return _matmul(name, a, b, pl.BlockSpec((tk, tm), lambda i, j, k: (k, i)),
                   pl.BlockSpec((tk, tn), lambda i, j, k: (k, j)), TN, (m // tm, n // tn, s // tk), (tm, tn),
                   [(jax.ShapeDtypeStruct((m, n), d), o_spec) for d in out_dtypes])


def _mm_stacked(name, a, w3, mode, out_dtypes=(F32,), epilogue=None, extras=(), tm=MM_TILE, tn=MM_TILE, tk=MM_TILE,
                norm_g=None):
    m, kd = a.shape
    _, r, c = w3.shape
    n = c if mode == "row" else N_SHARDS * c
    if mode == "row":
        tm, tn, tk = _tile(m, tm // 2), _tile(n, tn), kd
        b_spec = [pl.BlockSpec((None, r, tn), functools.partial(lambda i, j, k, p: (p, 0, j), p=p)) for p in range(N_SHARDS)]
    else:
        tm, tn, tk = _tile(m, tm), _tile(c, tn), _tile(kd, tk)
        per = c // tn
        b_spec = pl.BlockSpec((None, tk, tn), lambda i, j, k: (j // per, k, j % per))
    o_spec = pl.BlockSpec((tm, tn), lambda i, j, k: (i, j))
    return _matmul(name, a, w3, pl.BlockSpec((tm, tk), lambda i, j, k: (i, k)), b_spec, NN,
                   (m // tm, n // tn, kd // tk), (tm, tn),
                   [(jax.ShapeDtypeStruct((m, n), d), o_spec) for d in out_dtypes],
                   [(e, o_spec) for e in extras], epilogue, norm_g=norm_g)


def _mm_stacked_nt(name, a, w3, mode, out_dtypes=(F32,), epilogue=None, extras=(), tm=MM_TILE, tn=MM_TILE, tk=MM_TILE,
                   consts=(), sums=()):
    m, nd = a.shape
    _, r, c = w3.shape
    kout = N_SHARDS * r if mode == "row" else r
    if mode == "row":
        tm, tn, tk = _tile(m, tm), _tile(r, tn), _tile(c, tk)
        per = r // tn
        b_spec = pl.BlockSpec((None, tn, tk), lambda i, j, k: (j // per, j % per, k))
    else:
        tm, tn, tk = _tile(m, min(tm, MM_TILE // 2)), _tile(r, tn), nd
        b_spec = [pl.BlockSpec((None, tn, c), functools.partial(lambda i, j, k, p: (p, j, 0), p=p)) for p in range(N_SHARDS)]
    o_spec = pl.BlockSpec((tm, tn), lambda i, j, k: (i, j))
    return _matmul(name, a, w3, pl.BlockSpec((tm, tk), lambda i, j, k: (i, k)), b_spec, NT,
                   (m // tm, kout // tn, nd // tk), (tm, tn),
                   [(jax.ShapeDtypeStruct((m, kout), d), o_spec) for d in out_dtypes],
                   _epilogue_operands(extras, consts, o_spec), epilogue, sums)


def _mm_tn_stacked(name, a, b, shape3, mode, tm=MM_TILE, tn=MM_TILE, tk=MM_TILE):
    s, m = a.shape
    n = b.shape[1]
    _, r, c = shape3
    tk, tn = s, tn // 2
    if mode == "row":
        tm, tn = _tile(r, tm), _tile(n, tn)
        per = r // tm
        o_spec = pl.BlockSpec((None, tm, tn), lambda i, j, k: (i // per, i % per, j))
    else:
        tm, tn = _tile(m, tm), _tile(c, tn)
        per = c // tn
        o_spec = pl.BlockSpec((None, tm, tn), lambda i, j, k: (j // per, i, j % per))
    outs = [(jax.ShapeDtypeStruct(shape3, F32), o_spec), (jax.ShapeDtypeStruct(shape3, BF16), o_spec)]
    return _matmul(name, a, b, pl.BlockSpec((tk, tm), lambda i, j, k: (k, i)),
                   pl.BlockSpec((tk, tn), lambda i, j, k: (k, j)), TN, (m // tm, n // tn, s // tk), (tm, tn),
                   outs, epilogue=lambda acc: (acc, acc))


def _rowwise(name, fn, rows, consts, out_rows, out_accs=(), tr=256):
    nr, nc, no = len(rows), len(consts), len(out_rows)
    n_rows = rows[0].shape[0]
    tr = _tile(n_rows, tr)

    def body(*refs):
        vals = fn(*[r[...] for r in refs[:nr + nc]])
        o_refs, a_refs = refs[nr + nc:nr + nc + no], refs[nr + nc + no:]
        for ref, v in zip(o_refs, vals[:no]):
            ref[...] = v.astype(ref.dtype)
        first = pl.program_id(0) == 0

        @pl.when(first)
        def _():
            for ref, v in zip(a_refs, vals[no:]):
                ref[...] = v

        @pl.when(jnp.logical_not(first))
        def _():
            for ref, v in zip(a_refs, vals[no:]):
                ref[...] += v

    def whole(shape):
        return pl.BlockSpec(shape, lambda i: (0,) * len(shape))

    return pl.pallas_call(
        body, name=name, grid=(n_rows // tr,),
        in_specs=[pl.BlockSpec((tr, a.shape[1]), lambda i: (i, 0)) for a in rows] + [whole(c.shape) for c in consts],
        out_specs=[pl.BlockSpec((tr, f), lambda i: (i, 0)) for f, _ in out_rows] + [whole(s) for s in out_accs],
        out_shape=[jax.ShapeDtypeStruct((n_rows, f), d) for f, d in out_rows]
        + [jax.ShapeDtypeStruct(s, F32) for s in out_accs],
        compiler_params=_params(("arbitrary",)),
    )(*rows, *consts)


def _rms_fwd(x, g):
    return x * lax.rsqrt(jnp.mean(x * x, axis=-1, keepdims=True) + NORM_EPS) * g


def _rms_bwd(dy, x, g):
    rstd = lax.rsqrt(jnp.mean(x * x, axis=-1, keepdims=True) + NORM_EPS)
    n = x * rstd
    dn = dy * g
    dx = rstd * (dn - n * jnp.mean(dn * n, axis=-1, keepdims=True))
    return dx, jnp.sum(dy * n, axis=0, keepdims=True)


def _rope(x, cs, s1, s2):
    return x * cs + pltpu.roll(x, 32, 1) * s1 + pltpu.roll(x, 96, 1) * s2


def _rope_t(dy, cs, s1, s2):
    return dy * cs + pltpu.roll(dy * s1, 96, 1) + pltpu.roll(dy * s2, 32, 1)


def _gelu(z):
    return 0.5 * z * (1.0 + lax.erf(z * (1.0 / math.sqrt(2.0))))


def _gelu_and_grad(z):
    cdf = 0.5 * (1.0 + lax.erf(z * (1.0 / math.sqrt(2.0))))
    return z * cdf, cdf + z * jnp.exp(-0.5 * z * z) * (1.0 / math.sqrt(2.0 * math.pi))


def _att_scores(q, kv, kr, masked, transposed):
    k = jnp.concatenate([kv[:, :NOPE], kr], axis=1)
    if transposed:
        s = lax.dot_general(k, q, NT, preferred_element_type=F32)
    else:
        s = lax.dot_general(q, k, NT, preferred_element_type=F32)
    if masked:
        r = lax.broadcasted_iota(jnp.int32, s.shape, 0)
        c = lax.broadcasted_iota(jnp.int32, s.shape, 1)
        s = jnp.where((r <= c) if transposed else (c <= r), s, -jnp.inf)
    return s, k


def _in_pairs(lo, hi, pair, single):
    n = hi - lo

    def body(p, carry):
        pair(lo + 2 * p, lo + 2 * p + 1)
        return carry

    lax.fori_loop(0, n // 2, body, 0)

    @pl.when(n % 2 == 1)
    def _():
        single(hi - 1)


def _causal_tiles(i, pair, single):
    @pl.when(i == 0)
    def _():
        single(i, True)

    @pl.when(i > 0)
    def _():
        _in_pairs(0, i - 1, lambda a, b: pair(a, b, False), lambda a: single(a, False))
        pair(i - 1, i, True)


def _flash_fwd(q, kvb, krb, tables):
    s_len = q.shape[0]
    t = ATT_TILE

    def body(q_ref, cs_ref, s1_ref, s2_ref, kv_ref, kr_ref, o_ref, lse_ref, qb_ref, m_s, l_s, acc_s):
        qi = pl.program_id(1)
        m_s[...] = jnp.full_like(m_s, -jnp.inf)
        l_s[...] = jnp.zeros_like(l_s)
        acc_s[...] = jnp.zeros_like(acc_s)
        qv = q_ref[...]
        q = jnp.concatenate([qv[:, :NOPE], _rope(qv[:, NOPE:], cs_ref[...], s1_ref[...], s2_ref[...])], axis=1)
        q = (q * LOG2_SCALE).astype(BF16)
        qb_ref[...] = q

        def scores(ki, masked):
            rows = pl.ds(pl.multiple_of(ki * t, t), t)
            kv = kv_ref[rows, :]
            return _att_scores(q, kv, kr_ref[rows, :], masked, False)[0], kv

        def update(s, kv):
            m_prev = m_s[...]
            m_new = jnp.maximum(m_prev, jnp.max(s, axis=1, keepdims=True))
            alpha = jnp.exp2(m_prev - m_new)
            p = jnp.exp2(s - jnp.tile(m_new, (1, t // LANE)))
            l_s[...] = alpha * l_s[...] + jnp.sum(p, axis=1, keepdims=True)
            acc_s[...] = alpha * acc_s[...] + jnp.dot(p.astype(BF16), kv[:, NOPE:], preferred_element_type=F32)
            m_s[...] = m_new

        def pair(k0, k1, masked):
            first, second = scores(k0, False), scores(k1, masked)
            update(*first)
            update(*second)

        _causal_tiles(qi, pair, lambda ki, masked: update(*scores(ki, masked)))
        o_ref[...] = (acc_s[...] / l_s[...]).astype(o_ref.dtype)
        lse_ref[...] = (m_s[...] + jnp.log2(l_s[...]))[:, :1]

    table = pl.BlockSpec((t, LANE), lambda h, qi: (qi, 0))
    return pl.pallas_call(
        body, name="flash_fwd", grid=(HEADS, s_len // t),
        in_specs=[pl.BlockSpec((t, HEAD_PAD), lambda h, qi: (qi, h)), table, table, table,
                  pl.BlockSpec((s_len, HEAD_PAD), lambda h, qi: (0, h)),
                  pl.BlockSpec((s_len, LANE), lambda h, qi: (0, 0))],
        out_specs=[pl.BlockSpec((t, VHEAD), lambda h, qi: (qi, h)),
                   pl.BlockSpec((None, t, 1), lambda h, qi: (h, qi, 0)),
                   pl.BlockSpec((t, HEAD_PAD), lambda h, qi: (qi, h))],
        out_shape=[jax.ShapeDtypeStruct((s_len, HEADS * VHEAD), BF16),
                   jax.ShapeDtypeStruct((HEADS, s_len, 1), F32),
                   jax.ShapeDtypeStruct((s_len, HEADS * HEAD_PAD), BF16)],
        scratch_shapes=[pltpu.VMEM((t, LANE), F32), pltpu.VMEM((t, LANE), F32), pltpu.VMEM((t, VHEAD), F32)],
        compiler_params=_params(("parallel", "arbitrary")),
    )(q, *tables, kvb, krb)


def _flash_bwd(qb, kvb, krb, dob, lse_row, delta_row):
    s_len = qb.shape[0]
    t = ATT_TILE
    nq = s_len // t
    scale = QK_HEAD ** -0.5

    def body(q_ref, kv_ref, kr_ref, do_ref, lse_ref, dl_ref, dq_ref, dkv_ref, dkr_ref, dk_s, dv_s):
        ki = pl.program_id(1)

        @pl.when(ki == 0)
        def _():
            dq_ref[...] = jnp.zeros_like(dq_ref)

        dk_s[...] = jnp.zeros_like(dk_s)
        dv_s[...] = jnp.zeros_like(dv_s)
        kv, kr = kv_ref[...], kr_ref[...]

        def products(qi, masked):
            rows = pl.ds(pl.multiple_of(qi * t, t), t)
            q, do = q_ref[rows, :], do_ref[rows, :]
            st, k = _att_scores(q, kv, kr, masked, True)
            return st, lax.dot_general(kv[:, NOPE:], do, NT, preferred_element_type=F32), q, do, rows, k

        def update(st, dpt, q, do, rows, k):
            pt = jnp.exp2(st - lse_ref[:, rows])
            dv_s[...] += jnp.dot(pt.astype(BF16), do, preferred_element_type=F32)
            dst = (pt * (dpt - dl_ref[:, rows]) * scale).astype(BF16)
            dk_s[...] += jnp.dot(dst, q, preferred_element_type=F32)
            dq_ref[rows, :] += lax.dot_general(dst, k, TN, preferred_element_type=F32)

        def pair(q0, q1, masked):
            first, second = products(q0, masked), products(q1, False)
            update(*first)
            update(*second)

        @pl.when(ki == nq - 1)
        def _():
            update(*products(ki, True))

        @pl.when(ki < nq - 1)
        def _():
            pair(ki, ki + 1, True)
            _in_pairs(ki + 2, nq, lambda a, b: pair(a, b, False), lambda qi: update(*products(qi, False)))

        dk = dk_s[...] * (1.0 / LOG2_SCALE)
        dkv_ref[...] = jnp.concatenate([dk[:, :NOPE], dv_s[...]], axis=1).astype(dkv_ref.dtype)
        dkr_ref[...] = dk[:, NOPE:]

    row = pl.BlockSpec((None, 1, s_len), lambda h, ki: (h, 0, 0))
    return pl.pallas_call(
        body, name="flash_bwd", grid=(HEADS, nq),
        in_specs=[pl.BlockSpec((s_len, HEAD_PAD), lambda h, ki: (0, h)),
                  pl.BlockSpec((t, HEAD_PAD), lambda h, ki: (ki, h)),
                  pl.BlockSpec((t, LANE), lambda h, ki: (ki, 0)),
                  pl.BlockSpec((s_len, VHEAD), lambda h, ki: (0, h)), row, row],
        out_specs=[pl.BlockSpec((s_len, HEAD_PAD), lambda h, ki: (0, h)),
                   pl.BlockSpec((t, HEAD_PAD), lambda h, ki: (ki, h)),
                   pl.BlockSpec((t, LANE), lambda h, ki: (ki, h))],
        out_shape=[jax.ShapeDtypeStruct((s_len, HEADS * HEAD_PAD), F32),
                   jax.ShapeDtypeStruct((s_len, HEADS * HEAD_PAD), BF16),
                   jax.ShapeDtypeStruct((s_len, HEADS * LANE), F32)],
        scratch_shapes=[pltpu.VMEM((t, HEAD_PAD), F32), pltpu.VMEM((t, VHEAD), F32)],
        compiler_params=_params(("parallel", "arbitrary")),
    )(qb, kvb, krb, dob, lse_row, delta_row)


def _tril(w):
    r = lax.broadcasted_iota(jnp.int32, w.shape, 0)
    c = lax.broadcasted_iota(jnp.int32, w.shape, 1)
    return jnp.where(c <= r, w, 0.0)


def _sgu_row_stats(z_ref, width, gd, v_s, act, extra_s=None):
    total = None
    for g in range(SGU_GROUPS):
        cols = slice(g * gd, (g + 1) * gd)
        v = act(z_ref[:, width + g * gd:width + (g + 1) * gd])
        if extra_s is not None:
            v, extra_s[:, cols] = v
        v_s[:, cols] = v
        part = jnp.sum(v, axis=1, keepdims=True)
        total = part if total is None else total + part
    mean = total * (1.0 / width)
    sq = None
    for g in range(SGU_GROUPS):
        d = v_s[:, g * gd:(g + 1) * gd] - mean
        part = jnp.sum(d * d, axis=1, keepdims=True)
        sq = part if sq is None else sq + part
    return mean, lax.rsqrt(sq * (1.0 / width) + LN_EPS)


def _sgu_fwd(zpre, ln_g, ln_b, w_sp, bias_full):
    s_len, two_w = zpre.shape
    width = two_w // 2
    gd = width // SGU_GROUPS
    t = SGU_CHUNK

    def body(z_ref, g_ref, b_ref, w_ref, bias_ref, uv_ref, v_s):
        mean, rstd = _sgu_row_stats(z_ref, width, gd, v_s, _gelu)
        for g in range(SGU_GROUPS):
            cols = slice(g * gd, (g + 1) * gd)
            vln = ((v_s[:, cols] - mean) * rstd * g_ref[:, cols] + b_ref[:, cols]).astype(BF16)
            mixed = jnp.dot(_tril(w_ref[g]).astype(BF16), vln, preferred_element_type=F32) + bias_ref[:, cols]
            uv_ref[:, cols] = (_gelu(z_ref[:, cols]) * mixed).astype(uv_ref.dtype)

    return pl.pallas_call(
        body, name="sgu_fwd", grid=(s_len // t,),
        in_specs=[pl.BlockSpec((t, two_w), lambda i: (i, 0)), pl.BlockSpec((1, width), lambda i: (0, 0)),
                  pl.BlockSpec((1, width), lambda i: (0, 0)), pl.BlockSpec(w_sp.shape, lambda i: (0, 0, 0)),
                  pl.BlockSpec((t, width), lambda i: (0, 0))],
        out_specs=pl.BlockSpec((t, width), lambda i: (i, 0)),
        out_shape=jax.ShapeDtypeStruct((s_len, width), BF16),
        scratch_shapes=[pltpu.VMEM((t, width), F32)],
        compiler_params=_params(("parallel",)),
    )(zpre, ln_g, ln_b, w_sp, bias_full)


def _sgu_bwd(zpre, duv, ln_g, ln_b, w_sp, bias_full):
    s_len, two_w = zpre.shape
    width = two_w // 2
    gd = width // SGU_GROUPS
    t = SGU_CHUNK

    def body(z_ref, duv_ref, g_ref, b_ref, w_ref, bias_ref, dz_ref, dg_ref, db_ref, dw_ref, dbias_ref, v_s, vgrad_s, dvhat_s):
        @pl.when(pl.program_id(0) == 0)
        def _():
            for ref in (dg_ref, db_ref, dw_ref, dbias_ref):
                ref[...] = jnp.zeros_like(ref)

        def accumulate(ref, val):
            ref[...] += val

        mean, rstd = _sgu_row_stats(z_ref, width, gd, v_s, _gelu_and_grad, vgrad_s)
        sum_dvhat = sum_dvhat_vhat = None
        for g in range(SGU_GROUPS):
            cols = slice(g * gd, (g + 1) * gd)
            vhat = (v_s[:, cols] - mean) * rstd
            vln = (vhat * g_ref[:, cols] + b_ref[:, cols]).astype(BF16)
            wc = _tril(w_ref[g]).astype(BF16)
            mixed = jnp.dot(wc, vln, preferred_element_type=F32) + bias_ref[:, cols]
            u, u_grad = _gelu_and_grad(z_ref[:, cols])
            duv = duv_ref[:, cols]
            dz_ref[:, cols] = (duv * mixed * u_grad).astype(dz_ref.dtype)
            dmixed = duv * u
            dmb = dmixed.astype(BF16)
            dvln = lax.dot_general(wc, dmb, TN, preferred_element_type=F32)
            accumulate(dw_ref.at[g], _tril(lax.dot_general(dmb, vln, NT, preferred_element_type=F32)))
            accumulate(dbias_ref.at[:, cols], dmixed)
            accumulate(dg_ref.at[:, cols], jnp.sum(dvln * vhat, axis=0, keepdims=True))
            accumulate(db_ref.at[:, cols], jnp.sum(dvln, axis=0, keepdims=True))
            dvhat = dvln * g_ref[:, cols]
            dvhat_s[:, cols] = dvhat
            parts = jnp.sum(dvhat, axis=1, keepdims=True), jnp.sum(dvhat * vhat, axis=1, keepdims=True)
            sum_dvhat = parts[0] if sum_dvhat is None else sum_dvhat + parts[0]
            sum_dvhat_vhat = parts[1] if sum_dvhat_vhat is None else sum_dvhat_vhat + parts[1]
        mean_dvhat, mean_dvhat_vhat = sum_dvhat * (1.0 / width), sum_dvhat_vhat * (1.0 / width)
        for g in range(SGU_GROUPS):
            cols = slice(g * gd, (g + 1) * gd)
            vhat = (v_s[:, cols] - mean) * rstd
            dv0 = rstd * (dvhat_s[:, cols] - mean_dvhat - vhat * mean_dvhat_vhat)
            dz_ref[:, width + g * gd:width + (g + 1) * gd] = (dv0 * vgrad_s[:, cols]).astype(dz_ref.dtype)

    vec = pl.BlockSpec((1, width), lambda i: (0, 0))
    return pl.pallas_call(
        body, name="sgu_bwd", grid=(s_len // t,),
        in_specs=[pl.BlockSpec((t, two_w), lambda i: (i, 0)), pl.BlockSpec((t, width), lambda i: (i, 0)), vec, vec,
                  pl.BlockSpec(w_sp.shape, lambda i: (0, 0, 0)), pl.BlockSpec((t, width), lambda i: (0, 0))],
        out_specs=[pl.BlockSpec((t, two_w), lambda i: (i, 0)), vec, vec,
                   pl.BlockSpec(w_sp.shape, lambda i: (0, 0, 0)), pl.BlockSpec((t, width), lambda i: (0, 0))],
        out_shape=[jax.ShapeDtypeStruct((s_len, two_w), BF16), jax.ShapeDtypeStruct((1, width), F32),
                   jax.ShapeDtypeStruct((1, width), F32), jax.ShapeDtypeStruct(w_sp.shape, F32),
                   jax.ShapeDtypeStruct((t, width), F32)],
        scratch_shapes=[pltpu.VMEM((t, width), F32)] * 3,
        compiler_params=_params(("arbitrary",)),
    )(zpre, duv, ln_g, ln_b, w_sp, bias_full)


def _rope_tables(positions):
    inv_freq = ROPE_THETA ** (-jnp.arange(0, ROPE, 2, dtype=F32) / ROPE)
    ang = positions.astype(F32)[:, None] * inv_freq
    cos, sin = jnp.cos(ang), jnp.sin(ang)
    z32, z64 = jnp.zeros_like(cos), jnp.zeros((cos.shape[0], LANE - ROPE), F32)
    return (jnp.concatenate([cos, cos, z64], axis=1), jnp.concatenate([z32, sin, z64], axis=1),
            jnp.concatenate([-sin, z32, z64], axis=1))


def _ffn_fwd(x, g, w_up3, w_down3):
    def sq_relu(acc):
        r = jnp.maximum(acc, 0.0)
        return r * r, 2.0 * r

    r, r_grad, h2 = _mm_stacked("ffn_up", x, w_up3, "col", (BF16, BF16), sq_relu, norm_g=g)
    x_out = _mm_stacked("ffn_down", r, w_down3, "row", (F32,), lambda acc, res: (acc + res,), [x])[0]
    return x_out, (x, h2, r, r_grad)


def _ffn_bwd(dx, dxb, saved, g, w_up3, w_down3):
    x, h2, r, r_grad = saved
    da = _mm_stacked_nt("ffn_down_dx", dxb, w_down3, "row", (BF16,),
                        lambda acc, rg: (acc * rg.astype(F32),), [r_grad])[0]
    g_down = _mm_tn_stacked("ffn_down_dw", r, dxb, w_down3.shape, "row")
    dx, dxb, dg = _mm_stacked_nt("ffn_up_dx", da, w_up3, "col", **_norm_bwd(x, g, dx))
    g_up = _mm_tn_stacked("ffn_up_dw", h2, da, w_up3.shape, "col")
    return dx, dxb, dg, g_up, g_down


def _norm_bwd(x, g, dres):
    def epilogue(dh, xv, rv, gv):
        dxv, dg = _rms_bwd(dh, xv, gv)
        return dxv + rv, dxv + rv, dg

    return dict(out_dtypes=(F32, BF16), epilogue=epilogue, extras=[x, dres], consts=[g], sums=[g.shape], tm=MM_TILE // 2)


def _dot(a, b, dims=NN):
    return lax.dot_general(a.astype(BF16), b.astype(BF16), dims, preferred_element_type=F32)


def _mla_fwd(x, g, wdkv, q_norm, kv_norm, wq, wkv, wo, tables):
    d = x.shape[1]

    def project(xv, cs, s1, s2, gv, wdkv_v, qg, kg, wq_v, wkv_v):
        h = _rms_fwd(xv, gv).astype(BF16)
        lv = _dot(h, wdkv_v)
        cqn = _rms_fwd(lv[:, :Q_RANK], qg).astype(BF16)
        ckvn = _rms_fwd(lv[:, Q_RANK:Q_RANK + KV_RANK], kg).astype(BF16)
        return (h, lv, cqn, ckvn, _rope(lv[:, Q_RANK + KV_RANK:], cs, s1, s2), _dot(cqn, wq_v), _dot(ckvn, wkv_v))

    h, lat, cqn, ckvn, krb, q, kvb = _rowwise(
        "mla_project", project, [x, *tables], [g, wdkv, q_norm, kv_norm, wq, wkv],
        [(d, BF16), (LAT_PAD, F32), (Q_RANK, BF16), (KV_RANK, BF16), (LANE, BF16), (wq.shape[1], F32), (wkv.shape[1], BF16)],
        tr=512)
    ob, lse, qb = _flash_fwd(q, kvb, krb, tables)
    x_mid = _mm("mla_o", ob, wo, (F32,), lambda acc, res: (acc + res,), [x])[0]
    return x_mid, (x, h, lat, cqn, ckvn, krb, qb, kvb, ob, lse)


def _mla_bwd(dx, dxb, saved, g, wdkv, q_norm, kv_norm, wq, wkv, wo, tables):
    x, h, lat, cqn, ckvn, krb, qb, kvb, ob, lse = saved
    s_len = x.shape[0]
    def with_delta(do, ov):
        prod = do * ov.astype(F32)
        lane = lax.broadcasted_iota(jnp.int32, (do.shape[0], LANE), 1)
        by_lane = None
        for hd in range(HEADS):
            total = jnp.broadcast_to(jnp.sum(prod[:, hd * VHEAD:(hd + 1) * VHEAD], axis=1, keepdims=True), lane.shape)
            by_lane = jnp.where(lane == hd, total, 0.0 if by_lane is None else by_lane)
        return do, by_lane

    dob, delta_lanes = _mm("mla_o_dx", dxb, wo, (BF16,), with_delta, [ob], nt=True, tm=MM_TILE // 2, narrow=[(LANE, F32)])
    g_wo = _mm_tn("mla_o_dw", ob, dxb)[0]
    delta_row = delta_lanes[:, :HEADS].T.reshape(HEADS, 1, s_len)
    dq, dkvb, dkr = _flash_bwd(qb, kvb, krb, dob, lse.reshape(HEADS, 1, s_len), delta_row)

    def project_bwd(dq_f, dkv_v, dkr_v, lv, cqn_v, ckvn_v, h_v, xv, rv, cs, s1, s2, gv, qg, kg, wq_v, wkv_v, wdkv_v):
        parts = []
        for hd in range(HEADS):
            parts += [dq_f[:, hd * HEAD_PAD:hd * HEAD_PAD + NOPE],
                      _rope_t(dq_f[:, hd * HEAD_PAD + NOPE:(hd + 1) * HEAD_PAD], cs, s1, s2)]
        dq_v = jnp.concatenate(parts, axis=1).astype(BF16)
        dcq, dqg = _rms_bwd(_dot(dq_v, wq_v, NT), lv[:, :Q_RANK], qg)
        dckv, dkg = _rms_bwd(_dot(dkv_v, wkv_v, NT), lv[:, Q_RANK:Q_RANK + KV_RANK], kg)
        dkr_sum = dkr_v[:, :LANE]
        for hd in range(1, HEADS):
            dkr_sum = dkr_sum + dkr_v[:, hd * LANE:(hd + 1) * LANE]
        dlat = jnp.concatenate([dcq, dckv, _rope_t(dkr_sum, cs, s1, s2)], axis=1).astype(BF16)
        dxv, dg = _rms_bwd(_dot(dlat, wdkv_v, NT), xv, gv)
        return (dxv + rv, dxv + rv, _dot(cqn_v, dq_v, TN), _dot(ckvn_v, dkv_v, TN), _dot(h_v, dlat, TN), dqg, dkg, dg)

    dx, dxb, g_wq, g_wkv, g_wdkv, g_qn, g_kvn, dg = _rowwise(
        "mla_project_bwd", project_bwd, [dq, dkvb, dkr, lat, cqn, ckvn, h, x, dx, *tables],
        [g, q_norm, kv_norm, wq, wkv, wdkv], [(x.shape[1], F32), (x.shape[1], BF16)],
        [wq.shape, wkv.shape, wdkv.shape, q_norm.shape, kv_norm.shape, g.shape], tr=256)
    return dx, dxb, dg, g_wdkv, g_qn, g_kvn, g_wq, g_wkv, g_wo


def _sgu_layer_fwd(x, g, w_in3, ln_g, ln_b, w_sp, bias_full, w_out3):
    zpre, h = _mm_stacked("sgu_in", x, w_in3, "col", norm_g=g)
    uv = _sgu_fwd(zpre, ln_g, ln_b, w_sp, bias_full)
    x_mid = _mm_stacked("sgu_out", uv, w_out3, "row", (F32,), lambda acc, res: (acc + res,), [x])[0]
    return x_mid, (x, h, zpre, uv)


def _sgu_layer_bwd(dx, dxb, saved, g, w_in3, ln_g, ln_b, w_sp, bias_full, w_out3):
    x, h, zpre, uv = saved
    duv = _mm_stacked_nt("sgu_out_dx", dxb, w_out3, "row")[0]
    g_out = _mm_tn_stacked("sgu_out_dw", uv, dxb, w_out3.shape, "row")
    dz, g_lng, g_lnb, g_wsp, g_bias = _sgu_bwd(zpre, duv, ln_g, ln_b, w_sp, bias_full)
    g_in = _mm_tn_stacked("sgu_in_dw", h, dz, w_in3.shape, "col")
    dx, dxb, dg = _mm_stacked_nt("sgu_in_dx", dz, w_in3, "col", **_norm_bwd(x, g, dx))
    return dx, dxb, dg, g_in, g_out, g_lng, g_lnb, g_wsp, g_bias


def _loss_head(x, target, g):
    d = x.shape[1]

    def fn(xv, tv, gv):
        err = _rms_fwd(xv, gv) - tv
        dxv, dg = _rms_bwd(err * (1.0 / d), xv, gv)
        return dxv, dxv, dg, jnp.sum(err * err, axis=0, keepdims=True)

    return _rowwise("loss_head", fn, [x, target], [g], [(d, F32), (d, BF16)], [g.shape, g.shape])


def _mixer_weights(i, stacks):
    by_rows = lambda a: a.reshape(N_SHARDS * a.shape[1], a.shape[2])
    by_cols = lambda a: a.transpose(1, 0, 2).reshape(a.shape[1], N_SHARDS * a.shape[2])
    if i % 2:
        w_in3, w_out3, ln_g, ln_b = stacks
        return w_in3, ln_g.reshape(1, -1), ln_b.reshape(1, -1), w_out3
    wdkv = by_rows(stacks[0])
    wdkv = jnp.pad(wdkv, ((0, 0), (0, LAT_PAD - wdkv.shape[1])))
    wq = jnp.pad(by_cols(stacks[1]).reshape(Q_RANK, HEADS, QK_HEAD), ((0, 0), (0, 0), (0, HEAD_PAD - QK_HEAD)))
    return wdkv, wq.reshape(Q_RANK, HEADS * HEAD_PAD), by_cols(stacks[2]), by_rows(stacks[3])


def _local_step(x, positions, target, norm_mix, norm_ffn, final_norm, q_norm, kv_norm, w_sp, b_sp, mixers, ffn, reducer):
    tables = _rope_tables(positions)
    gd = mixers[1][2].size // SGU_GROUPS
    bias_full = [jnp.repeat(b_sp[j].T, gd, axis=1) for j in range(DEPTH // 2)]
    saved, mla, sgu = [], [None] * (DEPTH // 2), [None] * (DEPTH // 2)
    for i in range(DEPTH):
        j = i // 2
        x, *stacks = lax.optimization_barrier((x, *mixers[i]))
        if i % 2 == 0:
            wdkv, wq, wkv, wo = mla[j] = _mixer_weights(i, stacks)
            x, s_mix = _mla_fwd(x, norm_mix[i:i + 1], wdkv, q_norm[j:j + 1], kv_norm[j:j + 1], wq, wkv, wo, tables)
        else:
            w_in3, ln_g, ln_b, w_out3 = sgu[j] = _mixer_weights(i, stacks)
            x, s_mix = _sgu_layer_fwd(x, norm_mix[i:i + 1], w_in3, ln_g, ln_b, w_sp[j], bias_full[j], w_out3)
        x, s_ffn = _ffn_fwd(x, norm_ffn[i:i + 1], *ffn[i])
        saved.append((s_mix, s_ffn))
    dx, dxb, g_final, sq_cols = _loss_head(x, target, final_norm[None, :])
    loss = 0.5 * jnp.sum(sq_cols) / x.shape[1]

    def pair(g):
        return g, g.astype(BF16)

    g_mix, g_ffn = [None] * DEPTH, [None] * DEPTH
    mla_g, sgu_g = [None] * (DEPTH // 2), [None] * (DEPTH // 2)
    for i in reversed(range(DEPTH)):
        j = i // 2
        s_mix, s_ffn = saved[i]
        dx, dxb, g_ffn[i], g_up, g_down = _ffn_bwd(dx, dxb, s_ffn, norm_ffn[i:i + 1], *ffn[i])
        dxb = reducer.add(f"ffn{i}", i, {"ffn_w_up": g_up, "ffn_w_down": g_down}, dxb)
        dxb = reducer.phase_end(dxb)
        if i % 2 == 0:
            wdkv, wq, wkv, wo = mla[j]
            dx, dxb, g_mix[i], g_wdkv, g_qn, g_kvn, g_wq, g_wkv, g_wo = _mla_bwd(
                dx, dxb, s_mix, norm_mix[i:i + 1], wdkv, q_norm[j:j + 1], kv_norm[j:j + 1], wq, wkv, wo, tables)
            mla_g[j] = (g_qn, g_kvn)
            g_wq = g_wq.reshape(Q_RANK, HEADS, HEAD_PAD)[..., :QK_HEAD].reshape(Q_RANK, N_SHARDS, -1)
            dxb = reducer.add(f"mla{j}", j, {
                "mla_w_dkv": pair(g_wdkv[:, :Q_RANK + KV_RANK + ROPE].reshape(N_SHARDS, -1, Q_RANK + KV_RANK + ROPE)),
                "mla_w_uq": pair(g_wq.transpose(1, 0, 2)),
                "mla_w_ukv": pair(g_wkv.reshape(KV_RANK, N_SHARDS, -1).transpose(1, 0, 2)),
                "mla_w_o": pair(g_wo.reshape(N_SHARDS, -1, g_wo.shape[1]))}, dxb)
        else:
            w_in3, ln_g, ln_b, w_out3 = sgu[j]
            dx, dxb, g_mix[i], g_in, g_out, g_lng, g_lnb, g_wsp, g_bias = _sgu_layer_bwd(
                dx, dxb, s_mix, norm_mix[i:i + 1], w_in3, ln_g, ln_b, w_sp[j], bias_full[j], w_out3)
            sgu_g[j] = (g_wsp, g_bias.reshape(SGU_CHUNK, SGU_GROUPS, gd).sum(axis=-1).T)
            dxb = reducer.add(f"sgu{j}", j, {"sgu_w_in": g_in, "sgu_w_out": g_out,
                                             "sgu_ln_g": pair(g_lng.reshape(N_SHARDS, -1, LANE)),
                                             "sgu_ln_b": pair(g_lnb.reshape(N_SHARDS, -1, LANE))}, dxb)
        dxb = reducer.phase_end(dxb)
    small = dict(
        norm_mix=jnp.concatenate(g_mix, axis=0), norm_ffn=jnp.concatenate(g_ffn, axis=0), final_norm=g_final[0],
        q_norm=jnp.concatenate([m[0] for m in mla_g], axis=0), kv_norm=jnp.concatenate([m[1] for m in mla_g], axis=0),
        w_sp=jnp.stack([s[0] for s in sgu_g]), b_sp=jnp.stack([s[1] for s in sgu_g]))
    return loss, dx, small


HBM_SPEC = pl.BlockSpec(memory_space=pltpu.HBM)


def _place():
    x, y, c = lax.axis_index("x"), lax.axis_index("y"), lax.axis_index("c")
    return x, y, c, [(1 - x, y), (x, 1 - y), (1 - x, 1 - y)]


def _remote(src, dst, send_sems, recv_sems, k, to):
    return pltpu.make_async_remote_copy(src_ref=src, dst_ref=dst, send_sem=send_sems.at[k], recv_sem=recv_sems.at[k],
                                        device_id=to, device_id_type=MESH)


def _gather_layer(tag, shards):
    n = len(shards)
    split = [s.shape[0] >= 16 for s in shards]

    def body(*refs):
        ins, outs = refs[:n], refs[n:2 * n]
        send_sems, recv_sems, local_sems = refs[2 * n:]
        x, y, c, chips = _place()
        mine = 2 * x + y
        barrier = pltpu.get_barrier_semaphore()
        peers = [(x, y, 1 - c)] + [(*chip, c) for chip in chips]
        for peer in peers:
            pl.semaphore_signal(barrier, inc=1, device_id=peer, device_id_type=MESH)
        pl.semaphore_wait(barrier, len(peers))

        def rows(t, half):
            hr = shards[t].shape[0] // 2
            return pl.ds(half * hr, hr) if split[t] else pl.ds(0, shards[t].shape[0])

        local, sent = [], []
        for t in range(n):
            local.append(pltpu.make_async_copy(ins[t], outs[t].at[mine], local_sems.at[t]))
            local[-1].start()
            for j, chip in enumerate(chips):
                cp = _remote(ins[t].at[rows(t, c)], outs[t].at[mine, rows(t, c)], send_sems, recv_sems, 3 * t + j, (*chip, c))
                cp.start()
                sent.append(cp)
        for j, chip in enumerate(chips):
            theirs = 2 * chip[0] + chip[1]
            for t in range(n):
                piece = outs[t].at[theirs, rows(t, c)]
                _remote(piece, piece, send_sems, recv_sems, 3 * t + j, (x, y, c)).wait_recv()
                if split[t]:
                    cp = _remote(piece, piece, send_sems, recv_sems, 3 * n + 3 * t + j, (x, y, 1 - c))
                    cp.start()
                    sent.append(cp)
        for j, chip in enumerate(chips):
            theirs = 2 * chip[0] + chip[1]
            for t in range(n):
                if split[t]:
                    piece = outs[t].at[theirs, rows(t, 1 - c)]
                    _remote(piece, piece, send_sems, recv_sems, 3 * n + 3 * t + j, (x, y, c)).wait_recv()
        for cp in sent:
            cp.wait_send()
        for cp in local:
            cp.wait()

    return pl.kernel(
        body, name=f"gather_{tag}", mesh=plsc.ScalarSubcoreMesh(axis_name="sequencer", num_cores=1),
        out_type=[jax.ShapeDtypeStruct((N_SHARDS, *s.shape), s.dtype) for s in shards],
        scratch_types=[pltpu.SemaphoreType.DMA((6 * n,)), pltpu.SemaphoreType.DMA((6 * n,)), pltpu.SemaphoreType.DMA((n,))],
        compiler_params=pltpu.CompilerParams(collective_id=ID_GATHER),
    )(*shards)


SEQUENCER = dict(axis_name="sequencer", num_cores=1)
ID_GATHER, ID_EXCHANGE, ID_SHARE = 0, 1, 2
MIN_SPLIT_ROWS = 16


def _handshake(peers):
    barrier = pltpu.get_barrier_semaphore()
    for peer in peers:
        pl.semaphore_signal(barrier, inc=1, device_id=peer, device_id_type=MESH)
    pl.semaphore_wait(barrier, len(peers))


def _half_rows(rows, half):
    return pl.ds(half * (rows // 2), rows // 2) if rows >= MIN_SPLIT_ROWS else pl.ds(0, rows)


SEM_SPEC = pl.BlockSpec(memory_space=pltpu.SEMAPHORE)
DATAFLOW = pltpu.SideEffectType.DATAFLOW_SIDE_EFFECTING


def _exchange_copies(shapes, stacks, lands, send_sems, recv_sems):
    x, y, c, chips = _place()
    mine = 2 * x + y
    copies = []
    for t, shape in enumerate(shapes):
        r = shape[1]
        copies.append(_remote(stacks[t].at[mine, _half_rows(r, 1 - c)], lands[t].at[0], send_sems, recv_sems, 7 * t, (x, y, 1 - c)))
        for j, chip in enumerate(chips):
            theirs = 2 * chip[0] + chip[1]
            copies.append(_remote(stacks[t].at[theirs, _half_rows(r, c)], lands[t].at[1 + j], send_sems, recv_sems,
                                  7 * t + 1 + j, (*chip, c)))
            copies.append(_remote(stacks[t].at[theirs, _half_rows(r, 1 - c)], lands[t].at[4 + j], send_sems, recv_sems,
                                  7 * t + 4 + j, (*chip, 1 - c)))
    return copies


def _exchange_start(tag, stacks, carry):
    n = len(stacks)
    shapes = [s.shape for s in stacks]
    lands = [lax.empty((7, s.shape[1] // 2 if s.shape[1] >= MIN_SPLIT_ROWS else s.shape[1], s.shape[2]), s.dtype) for s in stacks]

    def body(*refs):
        for cp in _exchange_copies(shapes, refs[:n], refs[n:2 * n], refs[2 * n + 1], refs[2 * n + 2]):
            cp.start()

    through = (*stacks, *lands, carry)
    out = pl.pallas_call(
        body, name=f"reduce_exchange_start_{tag}",
        out_shape=(pltpu.SemaphoreType.DMA((7 * n,)), pltpu.SemaphoreType.DMA((7 * n,)),
                   *[pltpu.HBM(a.shape, a.dtype) for a in through]),
        in_specs=[HBM_SPEC] * (2 * n + 1),
        out_specs=(SEM_SPEC, SEM_SPEC, *[HBM_SPEC] * (2 * n + 1)),
        input_output_aliases={t: 2 + t for t in range(2 * n + 1)},
        compiler_params=pltpu.CompilerParams(has_side_effects=DATAFLOW),
    )(*[pltpu.with_memory_space_constraint(a, pltpu.HBM) for a in through])
    return out[0], out[1], out[2:2 + n], out[2 + n:2 + 2 * n], out[-1]


def _exchange_wait(tag, send_sems, recv_sems, stacks, lands, after):
    n = len(stacks)
    shapes = [s.shape for s in stacks]

    def body(*refs):
        for cp in _exchange_copies(shapes, refs[:n], refs[n:2 * n], refs[2 * n], refs[2 * n + 1]):
            cp.wait()

    out = pl.pallas_call(
        body, name=f"reduce_exchange_wait_{tag}",
        out_shape=tuple(pltpu.HBM(a.shape, a.dtype) for a in (*stacks, *lands)),
        in_specs=[HBM_SPEC] * (2 * n) + [SEM_SPEC, SEM_SPEC, pl.BlockSpec(memory_space=pl.ANY)],
        out_specs=tuple([HBM_SPEC] * (2 * n)),
        input_output_aliases={t: t for t in range(2 * n)},
        compiler_params=pltpu.CompilerParams(has_side_effects=DATAFLOW),
    )(*stacks, *lands, send_sems, recv_sems, after)
    return out[:n], out[n:]


def _share_halves(tag, halves):
    n = len(halves)

    def body(*refs):
        ins, outs, send_sems, recv_sems = refs[:n], refs[n:2 * n], refs[2 * n], refs[2 * n + 1]
        x, y, c, _ = _place()
        _handshake([(x, y, 1 - c)])
        sent = [_remote(ins[t], outs[t], send_sems, recv_sems, t, (x, y, 1 - c)) for t in range(n)]
        for cp in sent:
            cp.start()
        for cp in sent:
            cp.wait()

    return pl.kernel(
        body, name=f"reduce_share_{tag}", mesh=plsc.ScalarSubcoreMesh(**SEQUENCER),
        out_type=[jax.ShapeDtypeStruct(h.shape, h.dtype) for h in halves],
        scratch_types=[pltpu.SemaphoreType.DMA((n,)), pltpu.SemaphoreType.DMA((n,))],
        compiler_params=pltpu.CompilerParams(collective_id=ID_SHARE),
    )(*halves)


def _all_reduce_small(part):
    rows = part.shape[0]
    half = rows // 2

    def body(p_ref, out_ref, sib_buf, chip_sums, send_sems, recv_sems):
        x, y, c, chips = _place()
        mine = 2 * x + y
        my_rows = pl.ds(pl.multiple_of(c * half, 8), half)
        swap = _remote(p_ref, sib_buf, send_sems, recv_sems, 0, (x, y, 1 - c))
        swap.start()
        swap.wait()
        chip_sums[mine] = p_ref[...] + sib_buf[...]
        sent = [_remote(chip_sums.at[mine, my_rows], chip_sums.at[mine, my_rows], send_sems, recv_sems, 1 + j, (*chip, c))
                for j, chip in enumerate(chips)]
        for cp in sent:
            cp.start()
        for j, chip in enumerate(chips):
            sent[j].wait_send()
            theirs = chip_sums.at[2 * chip[0] + chip[1], my_rows]
            _remote(theirs, theirs, send_sems, recv_sems, 1 + j, (x, y, c)).wait_recv()
        out_ref[my_rows, :] = ((chip_sums[0, my_rows, :] + chip_sums[1, my_rows, :]) + chip_sums[2, my_rows, :]) + chip_sums[3, my_rows, :]
        share = _remote(out_ref.at[my_rows], out_ref.at[my_rows], send_sems, recv_sems, 4, (x, y, 1 - c))
        share.start()
        share.wait_send()
        other = out_ref.at[pl.ds(pl.multiple_of((1 - c) * half, 8), half)]
        _remote(other, other, send_sems, recv_sems, 4, (x, y, c)).wait_recv()

    vmem = pl.BlockSpec(memory_space=pltpu.VMEM)
    return pl.pallas_call(
        body, name="all_reduce_small", in_specs=[vmem], out_specs=vmem, out_shape=jax.ShapeDtypeStruct(part.shape, F32),
        scratch_shapes=[pltpu.VMEM((rows, LANE), F32), pltpu.VMEM((N_SHARDS, rows, LANE), F32),
                        pltpu.SemaphoreType.DMA((5,)), pltpu.SemaphoreType.DMA((5,))],
        compiler_params=pltpu.CompilerParams(vmem_limit_bytes=VMEM_LIMIT_BYTES),
    )(part)


def _sum_partials(g3, others, sel):
    _, rows, c = others.shape
    whole = g3.shape[1] == rows
    tr = _tile(rows, UPDATE_ROWS)
    nb = rows // tr

    def body(sel_ref, g_ref, *rest):
        same = g_ref[...].astype(F32)
        for ref in rest[1:4]:
            same = same + ref[...].astype(F32)
        other = rest[0][...].astype(F32)
        for ref in rest[4:7]:
            other = other + ref[...].astype(F32)
        rest[7][...] = same + other

    blk = (None, tr, c)
    slots = [pl.BlockSpec(blk, functools.partial(lambda i, sr, k: (k, i, 0), k=k)) for k in range(7)]
    return pl.pallas_call(
        body, name="reduce_sum_partials",
        grid_spec=pltpu.PrefetchScalarGridSpec(
            num_scalar_prefetch=1, grid=(nb,),
            in_specs=[pl.BlockSpec(blk, lambda i, sr: (sr[0], (0 if whole else sr[1] * nb) + i, 0))] + slots,
            out_specs=pl.BlockSpec((tr, c), lambda i, sr: (i, 0))),
        out_shape=jax.ShapeDtypeStruct((rows, c), F32),
        compiler_params=_params(("parallel",)),
    )(sel, g3, *[others] * 7)


def _adamw_math(w, g, m, v):
    nm = ADAM_B1 * m + (1.0 - ADAM_B1) * g
    nv = ADAM_B2 * v + (1.0 - ADAM_B2) * (g * g)
    m_hat = nm / (1.0 - ADAM_B1 ** ADAM_STEP)
    v_hat = nv / (1.0 - ADAM_B2 ** ADAM_STEP)
    return -ADAM_LR * (m_hat / (jnp.sqrt(v_hat) + ADAM_EPS) + ADAM_WD * w), nm, nv


def _adamw_layer(layer, w, m, v, g_mine, g_sibling, sel, prev):
    lyr, r, c = w.shape
    rows = g_mine.shape[0]
    halves = r // rows
    tr = _tile(rows, 512)
    nb = rows // tr
    n_g = 1 if g_sibling is None else 2

    def body(sel_ref, w_ref, m_ref, v_ref, *rest):
        g = rest[0][...]
        if n_g == 2:
            g = jnp.where(pl.program_id(0) == sel_ref[1], g, rest[1][...])
        outs = rest[n_g + (0 if prev is None else 4):]
        d, nm, nv = _adamw_math(w_ref[...], g, m_ref[...], v_ref[...])
        for ref, val in zip(outs, (g, d, nm, nv)):
            ref[...] = val

    full = pl.BlockSpec((None, tr, c), lambda h, i, sr: (layer, h * nb + i, 0))
    part = pl.BlockSpec((tr, c), lambda h, i, sr: (i, 0))
    n_in = 4 + n_g
    return pl.pallas_call(
        body, name="adamw_layer",
        grid_spec=pltpu.PrefetchScalarGridSpec(
            num_scalar_prefetch=1, grid=(halves, nb),
            in_specs=[full] * 3 + [part] * n_g + ([] if prev is None else [pl.BlockSpec(memory_space=pl.ANY)] * 4),
            out_specs=[full] * 4),
        out_shape=[jax.ShapeDtypeStruct(w.shape, F32)] * 4,
        input_output_aliases={} if prev is None else {n_in + k: k for k in range(4)},
        compiler_params=_params(("parallel", "parallel")),
    )(sel, w, m, v, g_mine, *([] if g_sibling is None else [g_sibling]), *([] if prev is None else prev))


def _adamw(w, g, m, v):
    lyr, r, c = w.shape
    tr = _tile(r, 256)

    def body(w_ref, g_ref, m_ref, v_ref, d_ref, nm_ref, nv_ref):
        d_ref[...], nm_ref[...], nv_ref[...] = _adamw_math(w_ref[...], g_ref[...], m_ref[...], v_ref[...])

    blk = pl.BlockSpec((None, tr, c), lambda l, i: (l, i, 0))
    return pl.pallas_call(
        body, name="adamw", grid=(lyr, r // tr), in_specs=[blk] * 4, out_specs=[blk] * 3,
        out_shape=[jax.ShapeDtypeStruct(w.shape, F32)] * 3,
        compiler_params=_params(("parallel", "parallel")),
    )(w, g, m, v)


SHARDED = ("mla_w_dkv", "mla_w_uq", "mla_w_ukv", "mla_w_o", "sgu_w_in", "sgu_ln_g", "sgu_ln_b", "sgu_w_out",
           "ffn_w_up", "ffn_w_down")
REPLICATED = ("norm_mix", "norm_ffn", "final_norm", "mla_q_norm", "mla_kv_norm", "sgu_w_spatial", "sgu_b_spatial")
WEIGHTS = ("norm_mix", "norm_ffn", "final_norm", "mla_w_dkv", "mla_q_norm", "mla_kv_norm", "mla_w_uq", "mla_w_ukv",
           "mla_w_o", "sgu_w_in", "sgu_ln_g", "sgu_ln_b", "sgu_w_spatial", "sgu_b_spatial", "sgu_w_out", "ffn_w_up",
           "ffn_w_down")


class _Reducer:
    def __init__(self, state, sel):
        self.state, self.sel = state, sel
        self.started, self.travelling, self.summed = [], [], []
        self.done = {}

    def add(self, tag, layer, grads, token):
        names = list(grads)
        token, *tied = lax.optimization_barrier((token, *[a for n in names for a in grads[n]]))
        f32s, bf16s = tied[0::2], tied[1::2]
        *flying, token = _exchange_start(tag, bf16s, token)
        self.started.append((tag, layer, names, f32s, flying))
        return token

    def phase_end(self, token):
        for tag, layer, names, f32s, flying in self.travelling:
            bf16s, received = _exchange_wait(tag, *flying, token)
            own = [g if g.shape[1] >= MIN_SPLIT_ROWS else gb for g, gb in zip(f32s, bf16s)]
            mine = [_sum_partials(g, got, self.sel) for g, got in zip(own, received)]
            token, *mine = lax.optimization_barrier((token, *mine))
            cut = [k for k, g in enumerate(own) if g.shape[1] >= MIN_SPLIT_ROWS]
            theirs = dict(zip(cut, _share_halves(tag, [mine[k] for k in cut])))
            self.summed.append((layer, names, mine, [theirs.get(k) for k in range(len(names))]))
        self.travelling, self.started = self.started, []
        return token

    def update(self, token):
        for layer, names, mine, theirs in self.summed:
            for name, g_mine, g_theirs in zip(names, mine, theirs):
                w, m, v = self.state[name]
                self.done[name] = _adamw_layer(layer, w, m, v, g_mine, g_theirs, self.sel, self.done.get(name))
                token = self.done[name][1]
        self.summed = []
        return token


def _as3d(name, a):
    return a.reshape(a.shape[0], -1, LANE) if name in ("sgu_ln_g", "sgu_ln_b") else a


def _pack(parts):
    flat = jnp.concatenate([p.reshape(-1) for p in parts])
    rows = -(-flat.shape[0] // (256 * LANE)) * 256
    return jnp.pad(flat, (0, rows * LANE - flat.shape[0])).reshape(rows, LANE)


def _unpack(packed, like):
    flat, out, at = packed.reshape(-1), [], 0
    for p in like:
        out.append(flat[at:at + p.size].reshape(p.shape))
        at += p.size
    return out


def kernel(x, positions, norm_mix, norm_ffn, final_norm, mla_w_dkv, mla_q_norm, mla_kv_norm, mla_w_uq, mla_w_ukv, mla_w_o, sgu_w_in, sgu_ln_g, sgu_ln_b, sgu_w_spatial, sgu_b_spatial, sgu_w_out, ffn_w_up, ffn_w_down, loss_target, m_norm_mix, m_norm_ffn, m_final_norm, m_mla_w_dkv, m_mla_q_norm, m_mla_kv_norm, m_mla_w_uq, m_mla_w_ukv, m_mla_w_o, m_sgu_w_in, m_sgu_ln_g, m_sgu_ln_b, m_sgu_w_spatial, m_sgu_b_spatial, m_sgu_w_out, m_ffn_w_up, m_ffn_w_down, v_norm_mix, v_norm_ffn, v_final_norm, v_mla_w_dkv, v_mla_q_norm, v_mla_kv_norm, v_mla_w_uq, v_mla_w_ukv, v_mla_w_o, v_sgu_w_in, v_sgu_ln_g, v_sgu_ln_b, v_sgu_w_spatial, v_sgu_b_spatial, v_sgu_w_out, v_ffn_w_up, v_ffn_w_down):
    given = dict(locals())
    w = {n: given[n] for n in WEIGHTS}
    mom = {n: given["m_" + n] for n in WEIGHTS}
    var = {n: given["v_" + n] for n in WEIGHTS}
    mixers, ffn, token = [], [], None
    for i in range(DEPTH):
        j = i // 2
        if i % 2 == 0:
            mixer = [w[n][j].astype(BF16) for n in ("mla_w_dkv", "mla_w_uq", "mla_w_ukv", "mla_w_o")]
        else:
            mixer = [sgu_w_in[j].astype(BF16), sgu_w_out[j].astype(BF16), sgu_ln_g[j].reshape(-1, LANE),
                     sgu_ln_b[j].reshape(-1, LANE)]
        for tag, shards, into in ((f"mixer{i}", mixer, mixers), (f"ffn{i}", [ffn_w_up[i].astype(BF16), ffn_w_down[i].astype(BF16)], ffn)):
            if token is None:
                token = shards[0]
            else:
                token, *shards = lax.optimization_barrier((token, *shards))
            into.append(_gather_layer(tag, shards))

    x_i, y_i, c_i = lax.axis_index("x"), lax.axis_index("y"), lax.axis_index("c")
    sel = jnp.stack([2 * x_i + y_i, c_i]).astype(jnp.int32)
    reducer = _Reducer({n: tuple(_as3d(n, d[n]) for d in (w, mom, var)) for n in SHARDED}, sel)
    loss, dx, small = _local_step(
        x[0], positions[0], loss_target[0], norm_mix, norm_ffn, final_norm, mla_q_norm, mla_kv_norm, sgu_w_spatial,
        sgu_b_spatial, mixers, ffn, reducer)
    loss = lax.psum(loss, ("x", "y", "c"))

    small_g = [small["norm_mix"], small["norm_ffn"], small["final_norm"], small["q_norm"], small["kv_norm"],
               small["w_sp"], small["b_sp"]]
    like = [w[n] for n in REPLICATED]
    g_small = _all_reduce_small(_pack(small_g))
    packed = [_pack([d[n] for n in REPLICATED])[None] for d in (w, mom, var)]
    upd_small = _adamw(packed[0], g_small[None], packed[1], packed[2])
    grads = dict(zip(REPLICATED, _unpack(g_small, like)))
    delta, new_m, new_v = ({n: a for n, a in zip(REPLICATED, _unpack(u[0], like))} for u in upd_small)

    reducer.phase_end(reducer.update(upd_small[0]))
    reducer.update(None)
    for n in SHARDED:
        grads[n], delta[n], new_m[n], new_v[n] = (a.reshape(w[n].shape) for a in reducer.done[n])

    return (loss, dx[None], *[grads[n] for n in WEIGHTS], *[delta[n] for n in WEIGHTS],
            *[new_m[n] for n in WEIGHTS], *[new_v[n] for n in WEIGHTS])
```

```python
import functools
import math

import jax
import jax.numpy as jnp
from jax import lax
from jax.experimental import pallas as pl
from jax.experimental.pallas import tpu as pltpu
from jax.experimental.pallas import tpu_sc as plsc

F32 = jnp.float32
BF16 = jnp.bfloat16
MESH = pl.DeviceIdType.MESH

DEPTH = 4
HEADS = 8
NOPE = 128
ROPE = 64
VHEAD = 128
QK_HEAD = NOPE + ROPE
Q_RANK = 256
KV_RANK = 128
HEAD_PAD = 256
LAT_PAD = 512
ROPE_THETA = 10000.0
SGU_CHUNK = 128
SGU_GROUPS = 8
NORM_EPS = 1e-6
LN_EPS = 1e-5
ADAM_LR, ADAM_B1, ADAM_B2, ADAM_EPS, ADAM_WD, ADAM_STEP = 0.001, 0.9, 0.999, 1e-08, 0.01, 10

N_SHARDS = 4
LANE = 128
VMEM_LIMIT_BYTES = 56 * 1024 * 1024
ATT_TILE = 512
MM_TILE = 1024
UPDATE_ROWS = 128
ATT_SCALE = QK_HEAD ** -0.5
LOG2_SCALE = ATT_SCALE * math.log2(math.e)

NN = (((1,), (0,)), ((), ()))
NT = (((1,), (1,)), ((), ()))
TN = (((0,), (0,)), ((), ()))


def _params(sem):
    return pltpu.CompilerParams(dimension_semantics=sem, vmem_limit_bytes=VMEM_LIMIT_BYTES)


def _tile(n, pref):
    t = min(n, pref)
    while n % t:
        t //= 2
    return t


def _matmul(name, a, b, a_spec, b_spec, dims, grid, tile, outs, extras=(), epilogue=None, sums=()):
    nk, ne, no = grid[2], len(extras), len(outs)
    b_specs = list(b_spec) if isinstance(b_spec, (list, tuple)) else [b_spec]
    nb = len(b_specs)

    def body(a_ref, *rest):
        b_refs, e_refs, o_refs = rest[:nb], rest[nb:nb + ne], rest[nb + ne:nb + ne + no]
        s_refs = rest[nb + ne + no:nb + ne + no + len(sums)]
        kw = a_ref.shape[1] // nb
        part = None
        for p, b_ref in enumerate(b_refs):
            a_tile = a_ref[...] if nb == 1 else a_ref[:, p * kw:(p + 1) * kw]
            d = lax.dot_general(a_tile.astype(BF16), b_ref[...].astype(BF16), dims, preferred_element_type=F32)
            part = d if part is None else part + d

        def finish(acc):
            vals = (acc,) if epilogue is None else epilogue(acc, *[e[...] for e in e_refs])
            for o_ref, v in zip(o_refs, vals):
                o_ref[...] = v.astype(o_ref.dtype)
            first = pl.program_id(0) == 0
            for s_ref, v in zip(s_refs, vals[no:]):
                @pl.when(first)
                def _():
                    s_ref[...] = v

                @pl.when(jnp.logical_not(first))
                def _():
                    s_ref[...] += v

        if nk == 1:
            finish(part)
            return
        acc_ref, k = rest[-1], pl.program_id(2)

        @pl.when(k == 0)
        def _():
            acc_ref[...] = part

        @pl.when(jnp.logical_and(k > 0, k < nk - 1))
        def _():
            acc_ref[...] += part

        @pl.when(k == nk - 1)
        def _():
            finish(acc_ref[...] + part)

    assert not sums or (grid[1] == 1 and nk == 1)
    return pl.pallas_call(
        body, name=name, grid=grid,
        in_specs=[a_spec] + b_specs + [s for _, s in extras],
        out_specs=[s for _, s in outs] + [pl.BlockSpec(s, lambda i, j, k: (0,) * len(s)) for s in sums],
        out_shape=[s for s, _ in outs] + [jax.ShapeDtypeStruct(s, F32) for s in sums],
        scratch_shapes=[pltpu.VMEM(tile, F32)] if nk > 1 else [],
        compiler_params=_params(("arbitrary" if sums else "parallel", "parallel", "arbitrary")),
    )(a, *[b] * nb, *[e for e, _ in extras])


def _epilogue_operands(extras, consts, o_spec):
    return [(e, o_spec) for e in extras] + [(c, pl.BlockSpec(c.shape, lambda i, j, k: (0, 0))) for c in consts]


def _mm(name, a, b, out_dtypes=(F32,), epilogue=None, extras=(), tm=MM_TILE, tn=MM_TILE, tk=MM_TILE, nt=False,
        consts=(), sums=(), narrow=()):
    m, kd = a.shape
    n = b.shape[0] if nt else b.shape[1]
    tm, tn, tk = _tile(m, tm), _tile(n, tn), _tile(kd, tk)
    o_spec = pl.BlockSpec((tm, tn), lambda i, j, k: (i, j))
    b_spec = pl.BlockSpec((tn, tk), lambda i, j, k: (j, k)) if nt else pl.BlockSpec((tk, tn), lambda i, j, k: (k, j))
    assert not narrow or n == tn
    outs = [(jax.ShapeDtypeStruct((m, n), d), o_spec) for d in out_dtypes]
    outs += [(jax.ShapeDtypeStruct((m, w), d), pl.BlockSpec((tm, w), lambda i, j, k: (i, 0))) for w, d in narrow]
    return _matmul(name, a, b, pl.BlockSpec((tm, tk), lambda i, j, k: (i, k)), b_spec, NT if nt else NN,
                   (m // tm, n // tn, kd // tk), (tm, tn), outs,
                   _epilogue_operands(extras, consts, o_spec), epilogue, sums)


def _mm_tn(name, a, b, out_dtypes=(F32,), tm=MM_TILE, tn=MM_TILE, tk=MM_TILE):
    s, m = a.shape
    n = b.shape[1]
    tm, tn, tk = _tile(m, tm), _tile(n, tn), _tile(s, tk)
    o_spec = pl.BlockSpec((tm, tn), lambda i, j, k: (i, j))
    return _matmul(name, a, b, pl.BlockSpec((tk, tm), lambda i, j, k: (k, i)),
                   pl.BlockSpec((tk, tn), lambda i, j, k: (k, j)), TN, (m // tm, n // tn, s // tk), (tm, tn),
                   [(jax.ShapeDtypeStruct((m, n), d), o_spec) for d in out_dtypes])


def _mm_stacked(name, a, w3, mode, out_dtypes=(F32,), epilogue=None, extras=(), tm=MM_TILE, tn=MM_TILE, tk=MM_TILE,
                consts=()):
    m, kd = a.shape
    _, r, c = w3.shape
    n = c if mode == "row" else N_SHARDS * c
    if mode == "row":
        tm, tn, tk = _tile(m, tm // 2), _tile(n, tn), kd
        b_spec = [pl.BlockSpec((None, r, tn), functools.partial(lambda i, j, k, p: (p, 0, j), p=p)) for p in range(N_SHARDS)]
    else:
        tm, tn, tk = _tile(m, tm), _tile(c, tn), _tile(kd, tk)
        per = c // tn
        b_spec = pl.BlockSpec((None, tk, tn), lambda i, j, k: (j // per, k, j % per))
    o_spec = pl.BlockSpec((tm, tn), lambda i, j, k: (i, j))
    return _matmul(name, a, w3, pl.BlockSpec((tm, tk), lambda i, j, k: (i, k)), b_spec, NN,
                   (m // tm, n // tn, kd // tk), (tm, tn),
                   [(jax.ShapeDtypeStruct((m, n), d), o_spec) for d in out_dtypes],
                   _epilogue_operands(extras, consts, o_spec), epilogue)


def _mm_stacked_nt(name, a, w3, mode, out_dtypes=(F32,), epilogue=None, extras=(), tm=MM_TILE, tn=MM_TILE, tk=MM_TILE,
                   consts=(), sums=()):
    m, nd = a.shape
    _, r, c = w3.shape
    kout = N_SHARDS * r if mode == "row" else r
    if mode == "row":
        tm, tn, tk = _tile(m, tm), _tile(r, tn), _tile(c, tk)
        per = r // tn
        b_spec = pl.BlockSpec((None, tn, tk), lambda i, j, k: (j // per, j % per, k))
    else:
        tm, tn, tk = _tile(m, tm // 2), _tile(r, tn), nd
        b_spec = [pl.BlockSpec((None, tn, c), functools.partial(lambda i, j, k, p: (p, j, 0), p=p)) for p in range(N_SHARDS)]
    o_spec = pl.BlockSpec((tm, tn), lambda i, j, k: (i, j))
    return _matmul(name, a, w3, pl.BlockSpec((tm, tk), lambda i, j, k: (i, k)), b_spec, NT,
                   (m // tm, kout // tn, nd // tk), (tm, tn),
                   [(jax.ShapeDtypeStruct((m, kout), d), o_spec) for d in out_dtypes],
                   _epilogue_operands(extras, consts, o_spec), epilogue, sums)


def _mm_tn_stacked(name, a, b, shape3, mode, tm=MM_TILE, tn=MM_TILE, tk=MM_TILE):
    s, m = a.shape
    n = b.shape[1]
    _, r, c = shape3
    tk, tn = s, tn // 2
    if mode == "row":
        tm, tn = _tile(r, tm), _tile(n, tn)
        per = r // tm
        o_spec = pl.BlockSpec((None, tm, tn), lambda i, j, k: (i // per, i % per, j))
    else:
        tm, tn = _tile(m, tm), _tile(c, tn)
        per = c // tn
        o_spec = pl.BlockSpec((None, tm, tn), lambda i, j, k: (j // per, i, j % per))
    outs = [(jax.ShapeDtypeStruct(shape3, F32), o_spec), (jax.ShapeDtypeStruct(shape3, BF16), o_spec)]
    return _matmul(name, a, b, pl.BlockSpec((tk, tm), lambda i, j, k: (k, i)),
                   pl.BlockSpec((tk, tn), lambda i, j, k: (k, j)), TN, (m // tm, n // tn, s // tk), (tm, tn),
                   outs, epilogue=lambda acc: (acc, acc))


def _rowwise(name, fn, rows, consts, out_rows, out_accs=(), tr=256):
    nr, nc, no = len(rows), len(consts), len(out_rows)
    n_rows = rows[0].shape[0]
    tr = _tile(n_rows, tr)

    def body(*refs):
        vals = fn(*[r[...] for r in refs[:nr + nc]])
        o_refs, a_refs = refs[nr + nc:nr + nc + no], refs[nr + nc + no:]
        for ref, v in zip(o_refs, vals[:no]):
            ref[...] = v.astype(ref.dtype)
        first = pl.program_id(0) == 0

        @pl.when(first)
        def _():
            for ref, v in zip(a_refs, vals[no:]):
                ref[...] = v

        @pl.when(jnp.logical_not(first))
        def _():
            for ref, v in zip(a_refs, vals[no:]):
                ref[...] += v

    def whole(shape):
        return pl.BlockSpec(shape, lambda i: (0,) * len(shape))

    return pl.pallas_call(
        body, name=name, grid=(n_rows // tr,),
        in_specs=[pl.BlockSpec((tr, a.shape[1]), lambda i: (i, 0)) for a in rows] + [whole(c.shape) for c in consts],
        out_specs=[pl.BlockSpec((tr, f), lambda i: (i, 0)) for f, _ in out_rows] + [whole(s) for s in out_accs],
        out_shape=[jax.ShapeDtypeStruct((n_rows, f), d) for f, d in out_rows]
        + [jax.ShapeDtypeStruct(s, F32) for s in out_accs],
        compiler_params=_params(("arbitrary",)),
    )(*rows, *consts)


def _rms_fwd(x, g):
    return x * lax.rsqrt(jnp.mean(x * x, axis=-1, keepdims=True) + NORM_EPS) * g


def _rms_bwd(dy, x, g):
    rstd = lax.rsqrt(jnp.mean(x * x, axis=-1, keepdims=True) + NORM_EPS)
    n = x * rstd
    dn = dy * g
    dx = rstd * (dn - n * jnp.mean(dn * n, axis=-1, keepdims=True))
    return dx, jnp.sum(dy * n, axis=0, keepdims=True)


def _rope(x, cs, s1, s2):
    return x * cs + pltpu.roll(x, 32, 1) * s1 + pltpu.roll(x, 96, 1) * s2


def _rope_t(dy, cs, s1, s2):
    return dy * cs + pltpu.roll(dy * s1, 96, 1) + pltpu.roll(dy * s2, 32, 1)


def _gelu(z):
    return 0.5 * z * (1.0 + lax.erf(z * (1.0 / math.sqrt(2.0))))


def _gelu_and_grad(z):
    cdf = 0.5 * (1.0 + lax.erf(z * (1.0 / math.sqrt(2.0))))
    return z * cdf, cdf + z * jnp.exp(-0.5 * z * z) * (1.0 / math.sqrt(2.0 * math.pi))


def _att_scores(q, kv, kr, masked, transposed):
    k = jnp.concatenate([kv[:, :NOPE], kr], axis=1)
    if transposed:
        s = lax.dot_general(k, q, NT, preferred_element_type=F32)
    else:
        s = lax.dot_general(q, k, NT, preferred_element_type=F32)
    if masked:
        r = lax.broadcasted_iota(jnp.int32, s.shape, 0)
        c = lax.broadcasted_iota(jnp.int32, s.shape, 1)
        s = jnp.where((r <= c) if transposed else (c <= r), s, -jnp.inf)
    return s, k


def _in_pairs(lo, hi, pair, single):
    n = hi - lo

    def body(p, carry):
        pair(lo + 2 * p, lo + 2 * p + 1)
        return carry

    lax.fori_loop(0, n // 2, body, 0)

    @pl.when(n % 2 == 1)
    def _():
        single(hi - 1)


def _causal_tiles(i, pair, single):
    @pl.when(i == 0)
    def _():
        single(i, True)

    @pl.when(i > 0)
    def _():
        _in_pairs(0, i - 1, lambda a, b: pair(a, b, False), lambda a: single(a, False))
        pair(i - 1, i, True)


def _flash_fwd(q, kvb, krb, tables):
    s_len = q.shape[0]
    t = ATT_TILE

    def body(q_ref, cs_ref, s1_ref, s2_ref, kv_ref, kr_ref, o_ref, lse_ref, qb_ref, m_s, l_s, acc_s):
        qi = pl.program_id(1)
        m_s[...] = jnp.full_like(m_s, -jnp.inf)
        l_s[...] = jnp.zeros_like(l_s)
        acc_s[...] = jnp.zeros_like(acc_s)
        qv = q_ref[...]
        q = jnp.concatenate([qv[:, :NOPE], _rope(qv[:, NOPE:], cs_ref[...], s1_ref[...], s2_ref[...])], axis=1)
        q = (q * LOG2_SCALE).astype(BF16)
        qb_ref[...] = q

        def scores(ki, masked):
            rows = pl.ds(pl.multiple_of(ki * t, t), t)
            kv = kv_ref[rows, :]
            return _att_scores(q, kv, kr_ref[rows, :], masked, False)[0], kv

        def update(s, kv):
            m_prev = m_s[...]
            m_new = jnp.maximum(m_prev, jnp.max(s, axis=1, keepdims=True))
            alpha = jnp.exp2(m_prev - m_new)
            p = jnp.exp2(s - jnp.tile(m_new, (1, t // LANE)))
            l_s[...] = alpha * l_s[...] + jnp.sum(p, axis=1, keepdims=True)
            acc_s[...] = alpha * acc_s[...] + jnp.dot(p.astype(BF16), kv[:, NOPE:], preferred_element_type=F32)
            m_s[...] = m_new

        def pair(k0, k1, masked):
            first, second = scores(k0, False), scores(k1, masked)
            update(*first)
            update(*second)

        _causal_tiles(qi, pair, lambda ki, masked: update(*scores(ki, masked)))
        o_ref[...] = (acc_s[...] / l_s[...]).astype(o_ref.dtype)
        lse_ref[...] = (m_s[...] + jnp.log2(l_s[...]))[:, :1]

    table = pl.BlockSpec((t, LANE), lambda h, qi: (qi, 0))
    return pl.pallas_call(
        body, name="flash_fwd", grid=(HEADS, s_len // t),
        in_specs=[pl.BlockSpec((t, HEAD_PAD), lambda h, qi: (qi, h)), table, table, table,
                  pl.BlockSpec((s_len, HEAD_PAD), lambda h, qi: (0, h)),
                  pl.BlockSpec((s_len, LANE), lambda h, qi: (0, 0))],
        out_specs=[pl.BlockSpec((t, VHEAD), lambda h, qi: (qi, h)),
                   pl.BlockSpec((None, t, 1), lambda h, qi: (h, qi, 0)),
                   pl.BlockSpec((t, HEAD_PAD), lambda h, qi: (qi, h))],
        out_shape=[jax.ShapeDtypeStruct((s_len, HEADS * VHEAD), BF16),
                   jax.ShapeDtypeStruct((HEADS, s_len, 1), F32),
                   jax.ShapeDtypeStruct((s_len, HEADS * HEAD_PAD), BF16)],
        scratch_shapes=[pltpu.VMEM((t, LANE), F32), pltpu.VMEM((t, LANE), F32), pltpu.VMEM((t, VHEAD), F32)],
        compiler_params=_params(("parallel", "arbitrary")),
    )(q, *tables, kvb, krb)


def _flash_bwd(qb, kvb, krb, dob, lse_row, delta_row):
    s_len = qb.shape[0]
    t = ATT_TILE
    nq = s_len // t
    scale = QK_HEAD ** -0.5

    def body(q_ref, kv_ref, kr_ref, do_ref, lse_ref, dl_ref, dq_ref, dkv_ref, dkr_ref, dk_s, dv_s):
        ki = pl.program_id(1)

        @pl.when(ki == 0)
        def _():
            dq_ref[...] = jnp.zeros_like(dq_ref)

        dk_s[...] = jnp.zeros_like(dk_s)
        dv_s[...] = jnp.zeros_like(dv_s)
        kv, kr = kv_ref[...], kr_ref[...]

        def products(qi, masked):
            rows = pl.ds(pl.multiple_of(qi * t, t), t)
            q, do = q_ref[rows, :], do_ref[rows, :]
            st, k = _att_scores(q, kv, kr, masked, True)
            return st, lax.dot_general(kv[:, NOPE:], do, NT, preferred_element_type=F32), q, do, rows, k

        def update(st, dpt, q, do, rows, k):
            pt = jnp.exp2(st - lse_ref[:, rows])
            dv_s[...] += jnp.dot(pt.astype(BF16), do, preferred_element_type=F32)
            dst = (pt * (dpt - dl_ref[:, rows]) * scale).astype(BF16)
            dk_s[...] += jnp.dot(dst, q, preferred_element_type=F32)
            dq_ref[rows, :] += lax.dot_general(dst, k, TN, preferred_element_type=F32)

        def pair(q0, q1, masked):
            first, second = products(q0, masked), products(q1, False)
            update(*first)
            update(*second)

        @pl.when(ki == nq - 1)
        def _():
            update(*products(ki, True))

        @pl.when(ki < nq - 1)
        def _():
            pair(ki, ki + 1, True)
            _in_pairs(ki + 2, nq, lambda a, b: pair(a, b, False), lambda qi: update(*products(qi, False)))

        dk = dk_s[...] * (1.0 / LOG2_SCALE)
        dkv_ref[...] = jnp.concatenate([dk[:, :NOPE], dv_s[...]], axis=1).astype(dkv_ref.dtype)
        dkr_ref[...] = dk[:, NOPE:]

    row = pl.BlockSpec((None, 1, s_len), lambda h, ki: (h, 0, 0))
    return pl.pallas_call(
        body, name="flash_bwd", grid=(HEADS, nq),
        in_specs=[pl.BlockSpec((s_len, HEAD_PAD), lambda h, ki: (0, h)),
                  pl.BlockSpec((t, HEAD_PAD), lambda h, ki: (ki, h)),
                  pl.BlockSpec((t, LANE), lambda h, ki: (ki, 0)),
                  pl.BlockSpec((s_len, VHEAD), lambda h, ki: (0, h)), row, row],
        out_specs=[pl.BlockSpec((s_len, HEAD_PAD), lambda h, ki: (0, h)),
                   pl.BlockSpec((t, HEAD_PAD), lambda h, ki: (ki, h)),
                   pl.BlockSpec((t, LANE), lambda h, ki: (ki, h))],
        out_shape=[jax.ShapeDtypeStruct((s_len, HEADS * HEAD_PAD), F32),
                   jax.ShapeDtypeStruct((s_len, HEADS * HEAD_PAD), BF16),
                   jax.ShapeDtypeStruct((s_len, HEADS * LANE), F32)],
        scratch_shapes=[pltpu.VMEM((t, HEAD_PAD), F32), pltpu.VMEM((t, VHEAD), F32)],
        compiler_params=_params(("parallel", "arbitrary")),
    )(qb, kvb, krb, dob, lse_row, delta_row)


def _tril(w):
    r = lax.broadcasted_iota(jnp.int32, w.shape, 0)
    c = lax.broadcasted_iota(jnp.int32, w.shape, 1)
    return jnp.where(c <= r, w, 0.0)


def _sgu_row_stats(z_ref, width, gd, v_s, act, extra_s=None):
    total = None
    for g in range(SGU_GROUPS):
        cols = slice(g * gd, (g + 1) * gd)
        v = act(z_ref[:, width + g * gd:width + (g + 1) * gd])
        if extra_s is not None:
            v, extra_s[:, cols] = v
        v_s[:, cols] = v
        part = jnp.sum(v, axis=1, keepdims=True)
        total = part if total is None else total + part
    mean = total * (1.0 / width)
    sq = None
    for g in range(SGU_GROUPS):
        d = v_s[:, g * gd:(g + 1) * gd] - mean
        part = jnp.sum(d * d, axis=1, keepdims=True)
        sq = part if sq is None else sq + part
    return mean, lax.rsqrt(sq * (1.0 / width) + LN_EPS)


def _sgu_fwd(zpre, ln_g, ln_b, w_sp, bias_full):
    s_len, two_w = zpre.shape
    width = two_w // 2
    gd = width // SGU_GROUPS
    t = SGU_CHUNK

    def body(z_ref, g_ref, b_ref, w_ref, bias_ref, uv_ref, v_s):
        mean, rstd = _sgu_row_stats(z_ref, width, gd, v_s, _gelu)
        for g in range(SGU_GROUPS):
            cols = slice(g * gd, (g + 1) * gd)
            vln = ((v_s[:, cols] - mean) * rstd * g_ref[:, cols] + b_ref[:, cols]).astype(BF16)
            mixed = jnp.dot(_tril(w_ref[g]).astype(BF16), vln, preferred_element_type=F32) + bias_ref[:, cols]
            uv_ref[:, cols] = (_gelu(z_ref[:, cols]) * mixed).astype(uv_ref.dtype)

    return pl.pallas_call(
        body, name="sgu_fwd", grid=(s_len // t,),
        in_specs=[pl.BlockSpec((t, two_w), lambda i: (i, 0)), pl.BlockSpec((1, width), lambda i: (0, 0)),
                  pl.BlockSpec((1, width), lambda i: (0, 0)), pl.BlockSpec(w_sp.shape, lambda i: (0, 0, 0)),
                  pl.BlockSpec((t, width), lambda i: (0, 0))],
        out_specs=pl.BlockSpec((t, width), lambda i: (i, 0)),
        out_shape=jax.ShapeDtypeStruct((s_len, width), BF16),
        scratch_shapes=[pltpu.VMEM((t, width), F32)],
        compiler_params=_params(("parallel",)),
    )(zpre, ln_g, ln_b, w_sp, bias_full)


def _sgu_bwd(zpre, duv, ln_g, ln_b, w_sp, bias_full):
    s_len, two_w = zpre.shape
    width = two_w // 2
    gd = width // SGU_GROUPS
    t = SGU_CHUNK

    def body(z_ref, duv_ref, g_ref, b_ref, w_ref, bias_ref, dz_ref, dg_ref, db_ref, dw_ref, dbias_ref, v_s, vgrad_s, dvhat_s):
        @pl.when(pl.program_id(0) == 0)
        def _():
            for ref in (dg_ref, db_ref, dw_ref, dbias_ref):
                ref[...] = jnp.zeros_like(ref)

        def accumulate(ref, val):
            ref[...] += val

        mean, rstd = _sgu_row_stats(z_ref, width, gd, v_s, _gelu_and_grad, vgrad_s)
        sum_dvhat = sum_dvhat_vhat = None
        for g in range(SGU_GROUPS):
            cols = slice(g * gd, (g + 1) * gd)
            vhat = (v_s[:, cols] - mean) * rstd
            vln = (vhat * g_ref[:, cols] + b_ref[:, cols]).astype(BF16)
            wc = _tril(w_ref[g]).astype(BF16)
            mixed = jnp.dot(wc, vln, preferred_element_type=F32) + bias_ref[:, cols]
            u, u_grad = _gelu_and_grad(z_ref[:, cols])
            duv = duv_ref[:, cols]
            dz_ref[:, cols] = (duv * mixed * u_grad).astype(dz_ref.dtype)
            dmixed = duv * u
            dmb = dmixed.astype(BF16)
            dvln = lax.dot_general(wc, dmb, TN, preferred_element_type=F32)
            accumulate(dw_ref.at[g], _tril(lax.dot_general(dmb, vln, NT, preferred_element_type=F32)))
            accumulate(dbias_ref.at[:, cols], dmixed)
            accumulate(dg_ref.at[:, cols], jnp.sum(dvln * vhat, axis=0, keepdims=True))
            accumulate(db_ref.at[:, cols], jnp.sum(dvln, axis=0, keepdims=True))
            dvhat = dvln * g_ref[:, cols]
            dvhat_s[:, cols] = dvhat
            parts = jnp.sum(dvhat, axis=1, keepdims=True), jnp.sum(dvhat * vhat, axis=1, keepdims=True)
            sum_dvhat = parts[0] if sum_dvhat is None else sum_dvhat + parts[0]
            sum_dvhat_vhat = parts[1] if sum_dvhat_vhat is None else sum_dvhat_vhat + parts[1]
        mean_dvhat, mean_dvhat_vhat = sum_dvhat * (1.0 / width), sum_dvhat_vhat * (1.0 / width)
        for g in range(SGU_GROUPS):
            cols = slice(g * gd, (g + 1) * gd)
            vhat = (v_s[:, cols] - mean) * rstd
            dv0 = rstd * (dvhat_s[:, cols] - mean_dvhat - vhat * mean_dvhat_vhat)
            dz_ref[:, width + g * gd:width + (g + 1) * gd] = (dv0 * vgrad_s[:, cols]).astype(dz_ref.dtype)

    vec = pl.BlockSpec((1, width), lambda i: (0, 0))
    return pl.pallas_call(
        body, name="sgu_bwd", grid=(s_len // t,),
        in_specs=[pl.BlockSpec((t, two_w), lambda i: (i, 0)), pl.BlockSpec((t, width), lambda i: (i, 0)), vec, vec,
                  pl.BlockSpec(w_sp.shape, lambda i: (0, 0, 0)), pl.BlockSpec((t, width), lambda i: (0, 0))],
        out_specs=[pl.BlockSpec((t, two_w), lambda i: (i, 0)), vec, vec,
                   pl.BlockSpec(w_sp.shape, lambda i: (0, 0, 0)), pl.BlockSpec((t, width), lambda i: (0, 0))],
        out_shape=[jax.ShapeDtypeStruct((s_len, two_w), BF16), jax.ShapeDtypeStruct((1, width), F32),
                   jax.ShapeDtypeStruct((1, width), F32), jax.ShapeDtypeStruct(w_sp.shape, F32),
                   jax.ShapeDtypeStruct((t, width), F32)],
        scratch_shapes=[pltpu.VMEM((t, width), F32)] * 3,
        compiler_params=_params(("arbitrary",)),
    )(zpre, duv, ln_g, ln_b, w_sp, bias_full)


def _rope_tables(positions):
    inv_freq = ROPE_THETA ** (-jnp.arange(0, ROPE, 2, dtype=F32) / ROPE)
    ang = positions.astype(F32)[:, None] * inv_freq
    cos, sin = jnp.cos(ang), jnp.sin(ang)
    z32, z64 = jnp.zeros_like(cos), jnp.zeros((cos.shape[0], LANE - ROPE), F32)
    return (jnp.concatenate([cos, cos, z64], axis=1), jnp.concatenate([z32, sin, z64], axis=1),
            jnp.concatenate([-sin, z32, z64], axis=1))


def _residual(g_next):
    if g_next is None:
        return dict(out_dtypes=(F32,), epilogue=lambda acc, res: (acc + res,))

    def epilogue(acc, res, gv):
        x_new = acc + res
        return x_new, _rms_fwd(x_new, gv)

    return dict(out_dtypes=(F32, BF16), epilogue=epilogue, consts=[g_next])


def _ffn_fwd(x, h2, w_up3, w_down3, g_next):
    def sq_relu(acc):
        r = jnp.maximum(acc, 0.0)
        return r * r, 2.0 * r

    r, r_grad = _mm_stacked("ffn_up", h2, w_up3, "col", (BF16, BF16), sq_relu)
    x_out, *h_next = _mm_stacked("ffn_down", r, w_down3, "row", extras=[x], **_residual(g_next))
    return x_out, (x, h2, r, r_grad), (h_next[0] if h_next else None)


def _ffn_bwd(dx, dxb, saved, g, w_up3, w_down3):
    x, h2, r, r_grad = saved
    da = _mm_stacked_nt("ffn_down_dx", dxb, w_down3, "row", (BF16,),
                        lambda acc, rg: (acc * rg.astype(F32),), [r_grad])[0]
    g_down = _mm_tn_stacked("ffn_down_dw", r, dxb, w_down3.shape, "row")
    dx, dxb, dg = _mm_stacked_nt("ffn_up_dx", da, w_up3, "col", **_norm_bwd(x, g, dx))
    g_up = _mm_tn_stacked("ffn_up_dw", h2, da, w_up3.shape, "col")
    return dx, dxb, dg, g_up, g_down


def _norm_bwd(x, g, dres):
    def epilogue(dh, xv, rv, gv):
        dxv, dg = _rms_bwd(dh, xv, gv)
        return dxv + rv, dxv + rv, dg

    return dict(out_dtypes=(F32, BF16), epilogue=epilogue, extras=[x, dres], consts=[g], sums=[g.shape], tm=MM_TILE // 2)


def _dot(a, b, dims=NN):
    return lax.dot_general(a.astype(BF16), b.astype(BF16), dims, preferred_element_type=F32)


def _mla_fwd(x, g, wdkv, q_norm, kv_norm, wq, wkv, wo, tables, g_next):
    d = x.shape[1]

    def project(xv, cs, s1, s2, gv, wdkv_v, qg, kg, wq_v, wkv_v):
        h = _rms_fwd(xv, gv).astype(BF16)
        lv = _dot(h, wdkv_v)
        cqn = _rms_fwd(lv[:, :Q_RANK], qg).astype(BF16)
        ckvn = _rms_fwd(lv[:, Q_RANK:Q_RANK + KV_RANK], kg).astype(BF16)
        return (h, lv, cqn, ckvn, _rope(lv[:, Q_RANK + KV_RANK:], cs, s1, s2), _dot(cqn, wq_v), _dot(ckvn, wkv_v))

    h, lat, cqn, ckvn, krb, q, kvb = _rowwise(
        "mla_project", project, [x, *tables], [g, wdkv, q_norm, kv_norm, wq, wkv],
        [(d, BF16), (LAT_PAD, F32), (Q_RANK, BF16), (KV_RANK, BF16), (LANE, BF16), (wq.shape[1], F32), (wkv.shape[1], BF16)],
        tr=512)
    ob, lse, qb = _flash_fwd(q, kvb, krb, tables)
    x_mid, h2 = _mm("mla_o", ob, wo, extras=[x], **_residual(g_next))
    return x_mid, (x, h, lat, cqn, ckvn, krb, qb, kvb, ob, lse), h2


def _mla_bwd(dx, dxb, saved, g, wdkv, q_norm, kv_norm, wq, wkv, wo, tables):
    x, h, lat, cqn, ckvn, krb, qb, kvb, ob, lse = saved
    s_len = x.shape[0]
    def with_delta(do, ov):
        prod = do * ov.astype(F32)
        lane = lax.broadcasted_iota(jnp.int32, (do.shape[0], LANE), 1)
        by_lane = None
        for hd in range(HEADS):
            total = jnp.broadcast_to(jnp.sum(prod[:, hd * VHEAD:(hd + 1) * VHEAD], axis=1, keepdims=True), lane.shape)
            by_lane = jnp.where(lane == hd, total, 0.0 if by_lane is None else by_lane)
        return do, by_lane

    dob, delta_lanes = _mm("mla_o_dx", dxb, wo, (BF16,), with_delta, [ob], nt=True, tm=MM_TILE // 2, narrow=[(LANE, F32)])
    g_wo = _mm_tn("mla_o_dw", ob, dxb)[0]
    delta_row = delta_lanes[:, :HEADS].T.reshape(HEADS, 1, s_len)
    dq, dkvb, dkr = _flash_bwd(qb, kvb, krb, dob, lse.reshape(HEADS, 1, s_len), delta_row)

    def project_bwd(dq_f, dkv_v, dkr_v, lv, cqn_v, ckvn_v, h_v, xv, rv, cs, s1, s2, gv, qg, kg, wq_v, wkv_v, wdkv_v):
        parts = []
        for hd in range(HEADS):
            parts += [dq_f[:, hd * HEAD_PAD:hd * HEAD_PAD + NOPE],
                      _rope_t(dq_f[:, hd * HEAD_PAD + NOPE:(hd + 1) * HEAD_PAD], cs, s1, s2)]
        dq_v = jnp.concatenate(parts, axis=1).astype(BF16)
        dcq, dqg = _rms_bwd(_dot(dq_v, wq_v, NT), lv[:, :Q_RANK], qg)
        dckv, dkg = _rms_bwd(_dot(dkv_v, wkv_v, NT), lv[:, Q_RANK:Q_RANK + KV_RANK], kg)
        dkr_sum = dkr_v[:, :LANE]
        for hd in range(1, HEADS):
            dkr_sum = dkr_sum + dkr_v[:, hd * LANE:(hd + 1) * LANE]
        dlat = jnp.concatenate([dcq, dckv, _rope_t(dkr_sum, cs, s1, s2)], axis=1).astype(BF16)
        dxv, dg = _rms_bwd(_dot(dlat, wdkv_v, NT), xv, gv)
        return (dxv + rv, dxv + rv, _dot(cqn_v, dq_v, TN), _dot(ckvn_v, dkv_v, TN), _dot(h_v, dlat, TN), dqg, dkg, dg)

    dx, dxb, g_wq, g_wkv, g_wdkv, g_qn, g_kvn, dg = _rowwise(
        "mla_project_bwd", project_bwd, [dq, dkvb, dkr, lat, cqn, ckvn, h, x, dx, *tables],
        [g, q_norm, kv_norm, wq, wkv, wdkv], [(x.shape[1], F32), (x.shape[1], BF16)],
        [wq.shape, wkv.shape, wdkv.shape, q_norm.shape, kv_norm.shape, g.shape], tr=256)
    return dx, dxb, dg, g_wdkv, g_qn, g_kvn, g_wq, g_wkv, g_wo


def _sgu_layer_fwd(x, h, w_in3, ln_g, ln_b, w_sp, bias_full, w_out3, g_next):
    zpre = _mm_stacked("sgu_in", h, w_in3, "col")[0]
    uv = _sgu_fwd(zpre, ln_g, ln_b, w_sp, bias_full)
    x_mid, h2 = _mm_stacked("sgu_out", uv, w_out3, "row", extras=[x], **_residual(g_next))
    return x_mid, (x, h, zpre, uv), h2


def _sgu_layer_bwd(dx, dxb, saved, g, w_in3, ln_g, ln_b, w_sp, bias_full, w_out3):
    x, h, zpre, uv = saved
    duv = _mm_stacked_nt("sgu_out_dx", dxb, w_out3, "row")[0]
    g_out = _mm_tn_stacked("sgu_out_dw", uv, dxb, w_out3.shape, "row")
    dz, g_lng, g_lnb, g_wsp, g_bias = _sgu_bwd(zpre, duv, ln_g, ln_b, w_sp, bias_full)
    g_in = _mm_tn_stacked("sgu_in_dw", h, dz, w_in3.shape, "col")
    dx, dxb, dg = _mm_stacked_nt("sgu_in_dx", dz, w_in3, "col", **_norm_bwd(x, g, dx))
    return dx, dxb, dg, g_in, g_out, g_lng, g_lnb, g_wsp, g_bias


def _loss_head(x, target, g):
    d = x.shape[1]

    def fn(xv, tv, gv):
        err = _rms_fwd(xv, gv) - tv
        dxv, dg = _rms_bwd(err * (1.0 / d), xv, gv)
        return dxv, dxv, dg, jnp.sum(err * err, axis=0, keepdims=True)

    return _rowwise("loss_head", fn, [x, target], [g], [(d, F32), (d, BF16)], [g.shape, g.shape])


def _mixer_weights(i, stacks):
    by_rows = lambda a: a.reshape(N_SHARDS * a.shape[1], a.shape[2])
    by_cols = lambda a: a.transpose(1, 0, 2).reshape(a.shape[1], N_SHARDS * a.shape[2])
    if i % 2:
        w_in3, w_out3, ln_g, ln_b = stacks
        return w_in3, ln_g.reshape(1, -1), ln_b.reshape(1, -1), w_out3
    wdkv = by_rows(stacks[0])
    wdkv = jnp.pad(wdkv, ((0, 0), (0, LAT_PAD - wdkv.shape[1])))
    wq = jnp.pad(by_cols(stacks[1]).reshape(Q_RANK, HEADS, QK_HEAD), ((0, 0), (0, 0), (0, HEAD_PAD - QK_HEAD)))
    return wdkv, wq.reshape(Q_RANK, HEADS * HEAD_PAD), by_cols(stacks[2]), by_rows(stacks[3])


def _local_step(x, positions, target, norm_mix, norm_ffn, final_norm, q_norm, kv_norm, w_sp, b_sp, mixers, ffn, reducer):
    tables = _rope_tables(positions)
    gd = mixers[1][2].size // SGU_GROUPS
    bias_full = [jnp.repeat(b_sp[j].T, gd, axis=1) for j in range(DEPTH // 2)]
    saved, mla, sgu = [], [None] * (DEPTH // 2), [None] * (DEPTH // 2)
    h = None
    for i in range(DEPTH):
        j = i // 2
        x, *stacks = lax.optimization_barrier((x, *mixers[i]))
        if i % 2 == 0:
            wdkv, wq, wkv, wo = mla[j] = _mixer_weights(i, stacks)
            x, s_mix, h2 = _mla_fwd(x, norm_mix[i:i + 1], wdkv, q_norm[j:j + 1], kv_norm[j:j + 1], wq, wkv, wo, tables,
                                    norm_ffn[i:i + 1])
        else:
            w_in3, ln_g, ln_b, w_out3 = sgu[j] = _mixer_weights(i, stacks)
            x, s_mix, h2 = _sgu_layer_fwd(x, h, w_in3, ln_g, ln_b, w_sp[j], bias_full[j], w_out3, norm_ffn[i:i + 1])
        next_is_sgu = i + 1 < DEPTH and (i + 1) % 2 == 1
        x, s_ffn, h = _ffn_fwd(x, h2, *ffn[i], norm_mix[i + 1:i + 2] if next_is_sgu else None)
        saved.append((s_mix, s_ffn))
    dx, dxb, g_final, sq_cols = _loss_head(x, target, final_norm[None, :])
    loss = 0.5 * jnp.sum(sq_cols) / x.shape[1]

    def pair(g):
        return g, g.astype(BF16)

    g_mix, g_ffn = [None] * DEPTH, [None] * DEPTH
    mla_g, sgu_g = [None] * (DEPTH // 2), [None] * (DEPTH // 2)
    for i in reversed(range(DEPTH)):
        j = i // 2
        s_mix, s_ffn = saved[i]
        dx, dxb, g_ffn[i], g_up, g_down = _ffn_bwd(dx, dxb, s_ffn, norm_ffn[i:i + 1], *ffn[i])
        dxb = reducer.add(f"ffn{i}", i, {"ffn_w_up": g_up, "ffn_w_down": g_down}, dxb)
        dxb = reducer.phase_end(dxb)
        if i % 2 == 0:
            wdkv, wq, wkv, wo = mla[j]
            dx, dxb, g_mix[i], g_wdkv, g_qn, g_kvn, g_wq, g_wkv, g_wo = _mla_bwd(
                dx, dxb, s_mix, norm_mix[i:i + 1], wdkv, q_norm[j:j + 1], kv_norm[j:j + 1], wq, wkv, wo, tables)
            mla_g[j] = (g_qn, g_kvn)
            g_wq = g_wq.reshape(Q_RANK, HEADS, HEAD_PAD)[..., :QK_HEAD].reshape(Q_RANK, N_SHARDS, -1)
            dxb = reducer.add(f"mla{j}", j, {
                "mla_w_dkv": pair(g_wdkv[:, :Q_RANK + KV_RANK + ROPE].reshape(N_SHARDS, -1, Q_RANK + KV_RANK + ROPE)),
                "mla_w_uq": pair(g_wq.transpose(1, 0, 2)),
                "mla_w_ukv": pair(g_wkv.reshape(KV_RANK, N_SHARDS, -1).transpose(1, 0, 2)),
                "mla_w_o": pair(g_wo.reshape(N_SHARDS, -1, g_wo.shape[1]))}, dxb)
        else:
            w_in3, ln_g, ln_b, w_out3 = sgu[j]
            dx, dxb, g_mix[i], g_in, g_out, g_lng, g_lnb, g_wsp, g_bias = _sgu_layer_bwd(
                dx, dxb, s_mix, norm_mix[i:i + 1], w_in3, ln_g, ln_b, w_sp[j], bias_full[j], w_out3)
            sgu_g[j] = (g_wsp, g_bias.reshape(SGU_CHUNK, SGU_GROUPS, gd).sum(axis=-1).T)
            dxb = reducer.add(f"sgu{j}", j, {"sgu_w_in": g_in, "sgu_w_out": g_out,
                                             "sgu_ln_g": pair(g_lng.reshape(N_SHARDS, -1, LANE)),
                                             "sgu_ln_b": pair(g_lnb.reshape(N_SHARDS, -1, LANE))}, dxb)
        dxb = reducer.phase_end(dxb)
    small = dict(
        norm_mix=jnp.concatenate(g_mix, axis=0), norm_ffn=jnp.concatenate(g_ffn, axis=0), final_norm=g_final[0],
        q_norm=jnp.concatenate([m[0] for m in mla_g], axis=0), kv_norm=jnp.concatenate([m[1] for m in mla_g], axis=0),
        w_sp=jnp.stack([s[0] for s in sgu_g]), b_sp=jnp.stack([s[1] for s in sgu_g]))
    return loss, dx, small


HBM_SPEC = pl.BlockSpec(memory_space=pltpu.HBM)


def _place():
    x, y, c = lax.axis_index("x"), lax.axis_index("y"), lax.axis_index("c")
    return x, y, c, [(1 - x, y), (x, 1 - y), (1 - x, 1 - y)]


def _remote(src, dst, send_sems, recv_sems, k, to):
    return pltpu.make_async_remote_copy(src_ref=src, dst_ref=dst, send_sem=send_sems.at[k], recv_sem=recv_sems.at[k],
                                        device_id=to, device_id_type=MESH)


def _gather_layer(tag, shards):
    n = len(shards)
    split = [s.shape[0] >= 16 for s in shards]

    def body(*refs):
        ins, outs = refs[:n], refs[n:2 * n]
        send_sems, recv_sems, local_sems = refs[2 * n:]
        x, y, c, chips = _place()
        mine = 2 * x + y
        barrier = pltpu.get_barrier_semaphore()
        peers = [(x, y, 1 - c)] + [(*chip, c) for chip in chips]
        for peer in peers:
            pl.semaphore_signal(barrier, inc=1, device_id=peer, device_id_type=MESH)
        pl.semaphore_wait(barrier, len(peers))

        def rows(t, half):
            hr = shards[t].shape[0] // 2
            return pl.ds(half * hr, hr) if split[t] else pl.ds(0, shards[t].shape[0])

        local, sent = [], []
        for t in range(n):
            local.append(pltpu.make_async_copy(ins[t], outs[t].at[mine], local_sems.at[t]))
            local[-1].start()
            for j, chip in enumerate(chips):
                cp = _remote(ins[t].at[rows(t, c)], outs[t].at[mine, rows(t, c)], send_sems, recv_sems, 3 * t + j, (*chip, c))
                cp.start()
                sent.append(cp)
        for j, chip in enumerate(chips):
            theirs = 2 * chip[0] + chip[1]
            for t in range(n):
                piece = outs[t].at[theirs, rows(t, c)]
                _remote(piece, piece, send_sems, recv_sems, 3 * t + j, (x, y, c)).wait_recv()
                if split[t]:
                    cp = _remote(piece, piece, send_sems, recv_sems, 3 * n + 3 * t + j, (x, y, 1 - c))
                    cp.start()
                    sent.append(cp)
        for j, chip in enumerate(chips):
            theirs = 2 * chip[0] + chip[1]
            for t in range(n):
                if split[t]:
                    piece = outs[t].at[theirs, rows(t, 1 - c)]
                    _remote(piece, piece, send_sems, recv_sems, 3 * n + 3 * t + j, (x, y, c)).wait_recv()
        for cp in sent:
            cp.wait_send()
        for cp in local:
            cp.wait()

    return pl.kernel(
        body, name=f"gather_{tag}", mesh=plsc.ScalarSubcoreMesh(axis_name="sequencer", num_cores=1),
        out_type=[jax.ShapeDtypeStruct((N_SHARDS, *s.shape), s.dtype) for s in shards],
        scratch_types=[pltpu.SemaphoreType.DMA((6 * n,)), pltpu.SemaphoreType.DMA((6 * n,)), pltpu.SemaphoreType.DMA((n,))],
        compiler_params=pltpu.CompilerParams(collective_id=ID_GATHER),
    )(*shards)


SEQUENCER = dict(axis_name="sequencer", num_cores=1)
ID_GATHER, ID_EXCHANGE, ID_SHARE = 0, 1, 2
MIN_SPLIT_ROWS = 16


def _handshake(peers):
    barrier = pltpu.get_barrier_semaphore()
    for peer in peers:
        pl.semaphore_signal(barrier, inc=1, device_id=peer, device_id_type=MESH)
    pl.semaphore_wait(barrier, len(peers))


def _half_rows(rows, half):
    return pl.ds(half * (rows // 2), rows // 2) if rows >= MIN_SPLIT_ROWS else pl.ds(0, rows)


SEM_SPEC = pl.BlockSpec(memory_space=pltpu.SEMAPHORE)
DATAFLOW = pltpu.SideEffectType.DATAFLOW_SIDE_EFFECTING


def _exchange_copies(shapes, stacks, lands, send_sems, recv_sems):
    x, y, c, chips = _place()
    mine = 2 * x + y
    copies = []
    for t, shape in enumerate(shapes):
        r = shape[1]
        copies.append(_remote(stacks[t].at[mine, _half_rows(r, 1 - c)], lands[t].at[0], send_sems, recv_sems, 7 * t, (x, y, 1 - c)))
        for j, chip in enumerate(chips):
            theirs = 2 * chip[0] + chip[1]
            copies.append(_remote(stacks[t].at[theirs, _half_rows(r, c)], lands[t].at[1 + j], send_sems, recv_sems,
                                  7 * t + 1 + j, (*chip, c)))
            copies.append(_remote(stacks[t].at[theirs, _half_rows(r, 1 - c)], lands[t].at[4 + j], send_sems, recv_sems,
                                  7 * t + 4 + j, (*chip, 1 - c)))
    return copies


def _exchange_start(tag, stacks, carry):
    n = len(stacks)
    shapes = [s.shape for s in stacks]
    lands = [lax.empty((7, s.shape[1] // 2 if s.shape[1] >= MIN_SPLIT_ROWS else s.shape[1], s.shape[2]), s.dtype) for s in stacks]

    def body(*refs):
        for cp in _exchange_copies(shapes, refs[:n], refs[n:2 * n], refs[2 * n + 1], refs[2 * n + 2]):
            cp.start()

    through = (*stacks, *lands, carry)
    out = pl.pallas_call(
        body, name=f"reduce_exchange_start_{tag}",
        out_shape=(pltpu.SemaphoreType.DMA((7 * n,)), pltpu.SemaphoreType.DMA((7 * n,)),
                   *[pltpu.HBM(a.shape, a.dtype) for a in through]),
        in_specs=[HBM_SPEC] * (2 * n + 1),
        out_specs=(SEM_SPEC, SEM_SPEC, *[HBM_SPEC] * (2 * n + 1)),
        input_output_aliases={t: 2 + t for t in range(2 * n + 1)},
        compiler_params=pltpu.CompilerParams(has_side_effects=DATAFLOW),
    )(*[pltpu.with_memory_space_constraint(a, pltpu.HBM) for a in through])
    return out[0], out[1], out[2:2 + n], out[2 + n:2 + 2 * n], out[-1]


def _exchange_wait(tag, send_sems, recv_sems, stacks, lands, after):
    n = len(stacks)
    shapes = [s.shape for s in stacks]

    def body(*refs):
        for cp in _exchange_copies(shapes, refs[:n], refs[n:2 * n], refs[2 * n], refs[2 * n + 1]):
            cp.wait()

    out = pl.pallas_call(
        body, name=f"reduce_exchange_wait_{tag}",
        out_shape=tuple(pltpu.HBM(a.shape, a.dtype) for a in (*stacks, *lands)),
        in_specs=[HBM_SPEC] * (2 * n) + [SEM_SPEC, SEM_SPEC, pl.BlockSpec(memory_space=pl.ANY)],
        out_specs=tuple([HBM_SPEC] * (2 * n)),
        input_output_aliases={t: t for t in range(2 * n)},
        compiler_params=pltpu.CompilerParams(has_side_effects=DATAFLOW),
    )(*stacks, *lands, send_sems, recv_sems, after)
    return out[:n], out[n:]


def _share_halves(tag, halves):
    n = len(halves)

    def body(*refs):
        ins, outs, send_sems, recv_sems = refs[:n], refs[n:2 * n], refs[2 * n], refs[2 * n + 1]
        x, y, c, _ = _place()
        _handshake([(x, y, 1 - c)])
        sent = [_remote(ins[t], outs[t], send_sems, recv_sems, t, (x, y, 1 - c)) for t in range(n)]
        for cp in sent:
            cp.start()
        for cp in sent:
            cp.wait()

    return pl.kernel(
        body, name=f"reduce_share_{tag}", mesh=plsc.ScalarSubcoreMesh(**SEQUENCER),
        out_type=[jax.ShapeDtypeStruct(h.shape, h.dtype) for h in halves],
        scratch_types=[pltpu.SemaphoreType.DMA((n,)), pltpu.SemaphoreType.DMA((n,))],
        compiler_params=pltpu.CompilerParams(collective_id=ID_SHARE),
    )(*halves)


def _all_reduce_small(part):
    rows = part.shape[0]
    half = rows // 2

    def body(p_ref, out_ref, sib_buf, chip_sums, send_sems, recv_sems):
        x, y, c, chips = _place()
        mine = 2 * x + y
        my_rows = pl.ds(pl.multiple_of(c * half, 8), half)
        swap = _remote(p_ref, sib_buf, send_sems, recv_sems, 0, (x, y, 1 - c))
        swap.start()
        swap.wait()
        chip_sums[mine] = p_ref[...] + sib_buf[...]
        sent = [_remote(chip_sums.at[mine, my_rows], chip_sums.at[mine, my_rows], send_sems, recv_sems, 1 + j, (*chip, c))
                for j, chip in enumerate(chips)]
        for cp in sent:
            cp.start()
        for j, chip in enumerate(chips):
            sent[j].wait_send()
            theirs = chip_sums.at[2 * chip[0] + chip[1], my_rows]
            _remote(theirs, theirs, send_sems, recv_sems, 1 + j, (x, y, c)).wait_recv()
        out_ref[my_rows, :] = ((chip_sums[0, my_rows, :] + chip_sums[1, my_rows, :]) + chip_sums[2, my_rows, :]) + chip_sums[3, my_rows, :]
        share = _remote(out_ref.at[my_rows], out_ref.at[my_rows], send_sems, recv_sems, 4, (x, y, 1 - c))
        share.start()
        share.wait_send()
        other = out_ref.at[pl.ds(pl.multiple_of((1 - c) * half, 8), half)]
        _remote(other, other, send_sems, recv_sems, 4, (x, y, c)).wait_recv()

    vmem = pl.BlockSpec(memory_space=pltpu.VMEM)
    return pl.pallas_call(
        body, name="all_reduce_small", in_specs=[vmem], out_specs=vmem, out_shape=jax.ShapeDtypeStruct(part.shape, F32),
        scratch_shapes=[pltpu.VMEM((rows, LANE), F32), pltpu.VMEM((N_SHARDS, rows, LANE), F32),
                        pltpu.SemaphoreType.DMA((5,)), pltpu.SemaphoreType.DMA((5,))],
        compiler_params=pltpu.CompilerParams(vmem_limit_bytes=VMEM_LIMIT_BYTES),
    )(part)


def _sum_partials(g3, others, sel):
    _, rows, c = others.shape
    whole = g3.shape[1] == rows
    tr = _tile(rows, UPDATE_ROWS)
    nb = rows // tr

    def body(sel_ref, g_ref, *rest):
        same = g_ref[...].astype(F32)
        for ref in rest[1:4]:
            same = same + ref[...].astype(F32)
        other = rest[0][...].astype(F32)
        for ref in rest[4:7]:
            other = other + ref[...].astype(F32)
        rest[7][...] = same + other

    blk = (None, tr, c)
    slots = [pl.BlockSpec(blk, functools.partial(lambda i, sr, k: (k, i, 0), k=k)) for k in range(7)]
    return pl.pallas_call(
        body, name="reduce_sum_partials",
        grid_spec=pltpu.PrefetchScalarGridSpec(
            num_scalar_prefetch=1, grid=(nb,),
            in_specs=[pl.BlockSpec(blk, lambda i, sr: (sr[0], (0 if whole else sr[1] * nb) + i, 0))] + slots,
            out_specs=pl.BlockSpec((tr, c), lambda i, sr: (i, 0))),
        out_shape=jax.ShapeDtypeStruct((rows, c), F32),
        compiler_params=_params(("parallel",)),
    )(sel, g3, *[others] * 7)


def _adamw_math(w, g, m, v):
    nm = ADAM_B1 * m + (1.0 - ADAM_B1) * g
    nv = ADAM_B2 * v + (1.0 - ADAM_B2) * (g * g)
    m_hat = nm / (1.0 - ADAM_B1 ** ADAM_STEP)
    v_hat = nv / (1.0 - ADAM_B2 ** ADAM_STEP)
    return -ADAM_LR * (m_hat / (jnp.sqrt(v_hat) + ADAM_EPS) + ADAM_WD * w), nm, nv


def _adamw_layer(layer, w, m, v, g_mine, g_sibling, sel, prev):
    lyr, r, c = w.shape
    rows = g_mine.shape[0]
    halves = r // rows
    tr = _tile(rows, 512)
    nb = rows // tr
    n_g = 1 if g_sibling is None else 2

    def body(sel_ref, w_ref, m_ref, v_ref, *rest):
        g = rest[0][...]
        if n_g == 2:
            g = jnp.where(pl.program_id(0) == sel_ref[1], g, rest[1][...])
        outs = rest[n_g + (0 if prev is None else 4):]
        d, nm, nv = _adamw_math(w_ref[...], g, m_ref[...], v_ref[...])
        for ref, val in zip(outs, (g, d, nm, nv)):
            ref[...] = val

    full = pl.BlockSpec((None, tr, c), lambda h, i, sr: (layer, h * nb + i, 0))
    part = pl.BlockSpec((tr, c), lambda h, i, sr: (i, 0))
    n_in = 4 + n_g
    return pl.pallas_call(
        body, name="adamw_layer",
        grid_spec=pltpu.PrefetchScalarGridSpec(
            num_scalar_prefetch=1, grid=(halves, nb),
            in_specs=[full] * 3 + [part] * n_g + ([] if prev is None else [pl.BlockSpec(memory_space=pl.ANY)] * 4),
            out_specs=[full] * 4),
        out_shape=[jax.ShapeDtypeStruct(w.shape, F32)] * 4,
        input_output_aliases={} if prev is None else {n_in + k: k for k in range(4)},
        compiler_params=_params(("parallel", "parallel")),
    )(sel, w, m, v, g_mine, *([] if g_sibling is None else [g_sibling]), *([] if prev is None else prev))


def _adamw(w, g, m, v):
    lyr, r, c = w.shape
    tr = _tile(r, 256)

    def body(w_ref, g_ref, m_ref, v_ref, d_ref, nm_ref, nv_ref):
        d_ref[...], nm_ref[...], nv_ref[...] = _adamw_math(w_ref[...], g_ref[...], m_ref[...], v_ref[...])

    blk = pl.BlockSpec((None, tr, c), lambda l, i: (l, i, 0))
    return pl.pallas_call(
        body, name="adamw", grid=(lyr, r // tr), in_specs=[blk] * 4, out_specs=[blk] * 3,
        out_shape=[jax.ShapeDtypeStruct(w.shape, F32)] * 3,
        compiler_params=_params(("parallel", "parallel")),
    )(w, g, m, v)


SHARDED = ("mla_w_dkv", "mla_w_uq", "mla_w_ukv", "mla_w_o", "sgu_w_in", "sgu_ln_g", "sgu_ln_b", "sgu_w_out",
           "ffn_w_up", "ffn_w_down")
REPLICATED = ("norm_mix", "norm_ffn", "final_norm", "mla_q_norm", "mla_kv_norm", "sgu_w_spatial", "sgu_b_spatial")
WEIGHTS = ("norm_mix", "norm_ffn", "final_norm", "mla_w_dkv", "mla_q_norm", "mla_kv_norm", "mla_w_uq", "mla_w_ukv",
           "mla_w_o", "sgu_w_in", "sgu_ln_g", "sgu_ln_b", "sgu_w_spatial", "sgu_b_spatial", "sgu_w_out", "ffn_w_up",
           "ffn_w_down")


class _Reducer:
    def __init__(self, state, sel):
        self.state, self.sel = state, sel
        self.started, self.travelling, self.summed = [], [], []
        self.done = {}

    def add(self, tag, layer, grads, token):
        names = list(grads)
        token, *tied = lax.optimization_barrier((token, *[a for n in names for a in grads[n]]))
        f32s, bf16s = tied[0::2], tied[1::2]
        *flying, token = _exchange_start(tag, bf16s, token)
        self.started.append((tag, layer, names, f32s, flying))
        return token

    def phase_end(self, token):
        for tag, layer, names, f32s, flying in self.travelling:
            bf16s, received = _exchange_wait(tag, *flying, token)
            own = [g if g.shape[1] >= MIN_SPLIT_ROWS else gb for g, gb in zip(f32s, bf16s)]
            mine = [_sum_partials(g, got, self.sel) for g, got in zip(own, received)]
            token, *mine = lax.optimization_barrier((token, *mine))
            cut = [k for k, g in enumerate(own) if g.shape[1] >= MIN_SPLIT_ROWS]
            theirs = dict(zip(cut, _share_halves(tag, [mine[k] for k in cut])))
            self.summed.append((layer, names, mine, [theirs.get(k) for k in range(len(names))]))
        self.travelling, self.started = self.started, []
        return token

    def update(self, token):
        for layer, names, mine, theirs in self.summed:
            for name, g_mine, g_theirs in zip(names, mine, theirs):
                w, m, v = self.state[name]
                self.done[name] = _adamw_layer(layer, w, m, v, g_mine, g_theirs, self.sel, self.done.get(name))
                token = self.done[name][1]
        self.summed = []
        return token


def _as3d(name, a):
    return a.reshape(a.shape[0], -1, LANE) if name in ("sgu_ln_g", "sgu_ln_b") else a


def _pack(parts):
    flat = jnp.concatenate([p.reshape(-1) for p in parts])
    rows = -(-flat.shape[0] // (256 * LANE)) * 256
    return jnp.pad(flat, (0, rows * LANE - flat.shape[0])).reshape(rows, LANE)


def _unpack(packed, like):
    flat, out, at = packed.reshape(-1), [], 0
    for p in like:
        out.append(flat[at:at + p.size].reshape(p.shape))
        at += p.size
    return out


def kernel(x, positions, norm_mix, norm_ffn, final_norm, mla_w_dkv, mla_q_norm, mla_kv_norm, mla_w_uq, mla_w_ukv, mla_w_o, sgu_w_in, sgu_ln_g, sgu_ln_b, sgu_w_spatial, sgu_b_spatial, sgu_w_out, ffn_w_up, ffn_w_down, loss_target, m_norm_mix, m_norm_ffn, m_final_norm, m_mla_w_dkv, m_mla_q_norm, m_mla_kv_norm, m_mla_w_uq, m_mla_w_ukv, m_mla_w_o, m_sgu_w_in, m_sgu_ln_g, m_sgu_ln_b, m_sgu_w_spatial, m_sgu_b_spatial, m_sgu_w_out, m_ffn_w_up, m_ffn_w_down, v_norm_mix, v_norm_ffn, v_final_norm, v_mla_w_dkv, v_mla_q_norm, v_mla_kv_norm, v_mla_w_uq, v_mla_w_ukv, v_mla_w_o, v_sgu_w_in, v_sgu_ln_g, v_sgu_ln_b, v_sgu_w_spatial, v_sgu_b_spatial, v_sgu_w_out, v_ffn_w_up, v_ffn_w_down):
    given = dict(locals())
    w = {n: given[n] for n in WEIGHTS}
    mom = {n: given["m_" + n] for n in WEIGHTS}
    var = {n: given["v_" + n] for n in WEIGHTS}
    mixers, ffn, token = [], [], None
    for i in range(DEPTH):
        j = i // 2
        if i % 2 == 0:
            mixer = [w[n][j].astype(BF16) for n in ("mla_w_dkv", "mla_w_uq", "mla_w_ukv", "mla_w_o")]
        else:
            mixer = [sgu_w_in[j].astype(BF16), sgu_w_out[j].astype(BF16), sgu_ln_g[j].reshape(-1, LANE),
                     sgu_ln_b[j].reshape(-1, LANE)]
        for tag, shards, into in ((f"mixer{i}", mixer, mixers), (f"ffn{i}", [ffn_w_up[i].astype(BF16), ffn_w_down[i].astype(BF16)], ffn)):
            if token is None:
                token = shards[0]
            else:
                token, *shards = lax.optimization_barrier((token, *shards))
            into.append(_gather_layer(tag, shards))

    x_i, y_i, c_i = lax.axis_index("x"), lax.axis_index("y"), lax.axis_index("c")
    sel = jnp.stack([2 * x_i + y_i, c_i]).astype(jnp.int32)
    reducer = _Reducer({n: tuple(_as3d(n, d[n]) for d in (w, mom, var)) for n in SHARDED}, sel)
    loss, dx, small = _local_step(
        x[0], positions[0], loss_target[0], norm_mix, norm_ffn, final_norm, mla_q_norm, mla_kv_norm, sgu_w_spatial,
        sgu_b_spatial, mixers, ffn, reducer)
    loss = lax.psum(loss, ("x", "y", "c"))

    small_g = [small["norm_mix"], small["norm_ffn"], small["final_norm"], small["q_norm"], small["kv_norm"],
               small["w_sp"], small["b_sp"]]
    like = [w[n] for n in REPLICATED]
    g_small = _all_reduce_small(_pack(small_g))
    packed = [_pack([d[n] for n in REPLICATED])[None] for d in (w, mom, var)]
    upd_small = _adamw(packed[0], g_small[None], packed[1], packed[2])
    grads = dict(zip(REPLICATED, _unpack(g_small, like)))
    delta, new_m, new_v = ({n: a for n, a in zip(REPLICATED, _unpack(u[0], like))} for u in upd_small)

    reducer.phase_end(reducer.update(upd_small[0]))
    reducer.update(None)
    for n in SHARDED:
        grads[n], delta[n], new_m[n], new_v[n] = (a.reshape(w[n].shape) for a in reducer.done[n])

    return (loss, dx[None], *[grads[n] for n in WEIGHTS], *[delta[n] for n in WEIGHTS],
            *[new_m[n] for n in WEIGHTS], *[new_v[n] for n in WEIGHTS])
```

```python
import functools
import math

import jax
import jax.numpy as jnp
from jax import lax
from jax.experimental import pallas as pl
from jax.experimental.pallas import tpu as pltpu
from jax.experimental.pallas import tpu_sc as plsc

F32 = jnp.float32
BF16 = jnp.bfloat16
MESH = pl.DeviceIdType.MESH

DEPTH = 4
HEADS = 8
NOPE = 128
ROPE = 64
VHEAD = 128
QK_HEAD = NOPE + ROPE
Q_RANK = 256
KV_RANK = 128
HEAD_PAD = 256
LAT_PAD = 512
ROPE_THETA = 10000.0
SGU_CHUNK = 128
SGU_GROUPS = 8
NORM_EPS = 1e-6
LN_EPS = 1e-5
ADAM_LR, ADAM_B1, ADAM_B2, ADAM_EPS, ADAM_WD, ADAM_STEP = 0.001, 0.9, 0.999, 1e-08, 0.01, 10

N_SHARDS = 4
LANE = 128
VMEM_LIMIT_BYTES = 56 * 1024 * 1024
ATT_TILE = 512
MM_TILE = 1024
UPDATE_ROWS = 128
ATT_SCALE = QK_HEAD ** -0.5
LOG2_SCALE = ATT_SCALE * math.log2(math.e)

NN = (((1,), (0,)), ((), ()))
NT = (((1,), (1,)), ((), ()))
TN = (((0,), (0,)), ((), ()))


def _params(sem):
    return pltpu.CompilerParams(dimension_semantics=sem, vmem_limit_bytes=VMEM_LIMIT_BYTES)


def _tile(n, pref):
    t = min(n, pref)
    while n % t:
        t //= 2
    return t


def _matmul(name, a, b, a_spec, b_spec, dims, grid, tile, outs, extras=(), epilogue=None, sums=()):
    nk, ne, no = grid[2], len(extras), len(outs)
    b_specs = list(b_spec) if isinstance(b_spec, (list, tuple)) else [b_spec]
    nb = len(b_specs)

    def body(a_ref, *rest):
        b_refs, e_refs, o_refs = rest[:nb], rest[nb:nb + ne], rest[nb + ne:nb + ne + no]
        s_refs = rest[nb + ne + no:nb + ne + no + len(sums)]
        kw = a_ref.shape[1] // nb
        part = None
        for p, b_ref in enumerate(b_refs):
            a_tile = a_ref[...] if nb == 1 else a_ref[:, p * kw:(p + 1) * kw]
            d = lax.dot_general(a_tile.astype(BF16), b_ref[...].astype(BF16), dims, preferred_element_type=F32)
            part = d if part is None else part + d

        def finish(acc):
            vals = (acc,) if epilogue is None else epilogue(acc, *[e[...] for e in e_refs])
            for o_ref, v in zip(o_refs, vals):
                o_ref[...] = v.astype(o_ref.dtype)
            first = pl.program_id(0) == 0
            for s_ref, v in zip(s_refs, vals[no:]):
                @pl.when(first)
                def _():
                    s_ref[...] = v

                @pl.when(jnp.logical_not(first))
                def _():
                    s_ref[...] += v

        if nk == 1:
            finish(part)
            return
        acc_ref, k = rest[-1], pl.program_id(2)

        @pl.when(k == 0)
        def _():
            acc_ref[...] = part

        @pl.when(jnp.logical_and(k > 0, k < nk - 1))
        def _():
            acc_ref[...] += part

        @pl.when(k == nk - 1)
        def _():
            finish(acc_ref[...] + part)

    assert not sums or (grid[1] == 1 and nk == 1)
    return pl.pallas_call(
        body, name=name, grid=grid,
        in_specs=[a_spec] + b_specs + [s for _, s in extras],
        out_specs=[s for _, s in outs] + [pl.BlockSpec(s, lambda i, j, k: (0,) * len(s)) for s in sums],
        out_shape=[s for s, _ in outs] + [jax.ShapeDtypeStruct(s, F32) for s in sums],
        scratch_shapes=[pltpu.VMEM(tile, F32)] if nk > 1 else [],
        compiler_params=_params(("arbitrary" if sums else "parallel", "parallel", "arbitrary")),
    )(a, *[b] * nb, *[e for e, _ in extras])


def _epilogue_operands(extras, consts, o_spec):
    return [(e, o_spec) for e in extras] + [(c, pl.BlockSpec(c.shape, lambda i, j, k: (0, 0))) for c in consts]


def _mm(name, a, b, out_dtypes=(F32,), epilogue=None, extras=(), tm=MM_TILE, tn=MM_TILE, tk=MM_TILE, nt=False,
        consts=(), sums=(), narrow=()):
    m, kd = a.shape
    n = b.shape[0] if nt else b.shape[1]
    tm, tn, tk = _tile(m, tm), _tile(n, tn), _tile(kd, tk)
    o_spec = pl.BlockSpec((tm, tn), lambda i, j, k: (i, j))
    b_spec = pl.BlockSpec((tn, tk), lambda i, j, k: (j, k)) if nt else pl.BlockSpec((tk, tn), lambda i, j, k: (k, j))
    assert not narrow or n == tn
    outs = [(jax.ShapeDtypeStruct((m, n), d), o_spec) for d in out_dtypes]
    outs += [(jax.ShapeDtypeStruct((m, w), d), pl.BlockSpec((tm, w), lambda i, j, k: (i, 0))) for w, d in narrow]
    return _matmul(name, a, b, pl.BlockSpec((tm, tk), lambda i, j, k: (i, k)), b_spec, NT if nt else NN,
                   (m // tm, n // tn, kd // tk), (tm, tn), outs,
                   _epilogue_operands(extras, consts, o_spec), epilogue, sums)


def _mm_tn(name, a, b, out_dtypes=(F32,), tm=MM_TILE, tn=MM_TILE, tk=MM_TILE):
    s, m = a.shape
    n = b.shape[1]
    tm, tn, tk = _tile(m, tm), _tile(n, tn), _tile(s, tk)
    o_spec = pl.BlockSpec((tm, tn), lambda i, j, k: (i, j))
    return _matmul(name, a, b, pl.BlockSpec((tk, tm), lambda i, j, k: (k, i)),
                   pl.BlockSpec((tk, tn), lambda i, j, k: (k, j)), TN, (m // tm, n // tn, s // tk), (tm, tn),
                   [(jax.ShapeDtypeStruct((m, n), d), o_spec) for d in out_dtypes])


def _mm_stacked(name, a, w3, mode, out_dtypes=(F32,), epilogue=None, extras=(), tm=MM_TILE, tn=MM_TILE, tk=MM_TILE,
                consts=()):
    m, kd = a.shape
    _, r, c = w3.shape
    n = c if mode == "row" else N_SHARDS * c
    if mode == "row":
        tm, tn, tk = _tile(m, tm // 2), _tile(n, tn), kd
        b_spec = [pl.BlockSpec((None, r, tn), functools.partial(lambda i, j, k, p: (p, 0, j), p=p)) for p in range(N_SHARDS)]
    else:
        tm, tn, tk = _tile(m, tm), _tile(c, tn), _tile(kd, tk)
        per = c // tn
        b_spec = pl.BlockSpec((None, tk, tn), lambda i, j, k: (j // per, k, j % per))
    o_spec = pl.BlockSpec((tm, tn), lambda i, j, k: (i, j))
    return _matmul(name, a, w3, pl.BlockSpec((tm, tk), lambda i, j, k: (i, k)), b_spec, NN,
                   (m // tm, n // tn, kd // tk), (tm, tn),
                   [(jax.ShapeDtypeStruct((m, n), d), o_spec) for d in out_dtypes],
                   _epilogue_operands(extras, consts, o_spec), epilogue)


def _mm_stacked_nt(name, a, w3, mode, out_dtypes=(F32,), epilogue=None, extras=(), tm=MM_TILE, tn=MM_TILE, tk=MM_TILE,
                   consts=(), sums=()):
    m, nd = a.shape
    _, r, c = w3.shape
    kout = N_SHARDS * r if mode == "row" else r
    if mode == "row":
        tm, tn, tk = _tile(m, tm), _tile(r, tn), _tile(c, tk)
        per = r // tn
        b_spec = pl.BlockSpec((None, tn, tk), lambda i, j, k: (j // per, j % per, k))
    else:
        tm, tn, tk = _tile(m, tm // 2), _tile(r, tn), nd
        b_spec = [pl.BlockSpec((None, tn, c), functools.partial(lambda i, j, k, p: (p, j, 0), p=p)) for p in range(N_SHARDS)]
    o_spec = pl.BlockSpec((tm, tn), lambda i, j, k: (i, j))
    return _matmul(name, a, w3, pl.BlockSpec((tm, tk), lambda i, j, k: (i, k)), b_spec, NT,
                   (m // tm, kout // tn, nd // tk), (tm, tn),
                   [(jax.ShapeDtypeStruct((m, kout), d), o_spec) for d in out_dtypes],
                   _epilogue_operands(extras, consts, o_spec), epilogue, sums)


def _mm_tn_stacked(name, a, b, shape3, mode, tm=MM_TILE, tn=MM_TILE, tk=MM_TILE):
    s, m = a.shape
    n = b.shape[1]
    _, r, c = shape3
    tk, tn = s, tn // 2
    if mode == "row":
        tm, tn = _tile(r, tm), _tile(n, tn)
        per = r // tm
        o_spec = pl.BlockSpec((None, tm, tn), lambda i, j, k: (i // per, i % per, j))
    else:
        tm, tn = _tile(m, tm), _tile(c, tn)
        per = c // tn
        o_spec = pl.BlockSpec((None, tm, tn), lambda i, j, k: (j // per, i, j % per))
    outs = [(jax.ShapeDtypeStruct(shape3, F32), o_spec), (jax.ShapeDtypeStruct(shape3, BF16), o_spec)]
    return _matmul(name, a, b, pl.BlockSpec((tk, tm), lambda i, j, k: (k, i)),
                   pl.BlockSpec((tk, tn), lambda i, j, k: (k, j)), TN, (m // tm, n // tn, s // tk), (tm, tn),
                   outs, epilogue=lambda acc: (acc, acc))


def _rowwise(name, fn, rows, consts, out_rows, out_accs=(), tr=256):
    nr, nc, no = len(rows), len(consts), len(out_rows)
    n_rows = rows[0].shape[0]
    tr = _tile(n_rows, tr)

    def body(*refs):
        vals = fn(*[r[...] for r in refs[:nr + nc]])
        o_refs, a_refs = refs[nr + nc:nr + nc + no], refs[nr + nc + no:]
        for ref, v in zip(o_refs, vals[:no]):
            ref[...] = v.astype(ref.dtype)
        first = pl.program_id(0) == 0

        @pl.when(first)
        def _():
            for ref, v in zip(a_refs, vals[no:]):
                ref[...] = v

        @pl.when(jnp.logical_not(first))
        def _():
            for ref, v in zip(a_refs, vals[no:]):
                ref[...] += v

    def whole(shape):
        return pl.BlockSpec(shape, lambda i: (0,) * len(shape))

    return pl.pallas_call(
        body, name=name, grid=(n_rows // tr,),
        in_specs=[pl.BlockSpec((tr, a.shape[1]), lambda i: (i, 0)) for a in rows] + [whole(c.shape) for c in consts],
        out_specs=[pl.BlockSpec((tr, f), lambda i: (i, 0)) for f, _ in out_rows] + [whole(s) for s in out_accs],
        out_shape=[jax.ShapeDtypeStruct((n_rows, f), d) for f, d in out_rows]
        + [jax.ShapeDtypeStruct(s, F32) for s in out_accs],
        compiler_params=_params(("arbitrary",)),
    )(*rows, *consts)


def _rms_fwd(x, g):
    return x * lax.rsqrt(jnp.mean(x * x, axis=-1, keepdims=True) + NORM_EPS) * g


def _rms_bwd(dy, x, g):
    rstd = lax.rsqrt(jnp.mean(x * x, axis=-1, keepdims=True) + NORM_EPS)
    n = x * rstd
    dn = dy * g
    dx = rstd * (dn - n * jnp.mean(dn * n, axis=-1, keepdims=True))
    return dx, jnp.sum(dy * n, axis=0, keepdims=True)


def _rope(x, cs, s1, s2):
    return x * cs + pltpu.roll(x, 32, 1) * s1 + pltpu.roll(x, 96, 1) * s2


def _rope_t(dy, cs, s1, s2):
    return dy * cs + pltpu.roll(dy * s1, 96, 1) + pltpu.roll(dy * s2, 32, 1)


def _gelu(z):
    return 0.5 * z * (1.0 + lax.erf(z * (1.0 / math.sqrt(2.0))))


def _gelu_and_grad(z):
    cdf = 0.5 * (1.0 + lax.erf(z * (1.0 / math.sqrt(2.0))))
    return z * cdf, cdf + z * jnp.exp(-0.5 * z * z) * (1.0 / math.sqrt(2.0 * math.pi))


def _att_scores(q, kv, kr, masked, transposed):
    k = jnp.concatenate([kv[:, :NOPE], kr], axis=1)
    if transposed:
        s = lax.dot_general(k, q, NT, preferred_element_type=F32)
    else:
        s = lax.dot_general(q, k, NT, preferred_element_type=F32)
    if masked:
        r = lax.broadcasted_iota(jnp.int32, s.shape, 0)
        c = lax.broadcasted_iota(jnp.int32, s.shape, 1)
        s = jnp.where((r <= c) if transposed else (c <= r), s, -jnp.inf)
    return s, k


def _in_pairs(lo, hi, pair, single):
    n = hi - lo

    def body(p, carry):
        pair(lo + 2 * p, lo + 2 * p + 1)
        return carry

    lax.fori_loop(0, n // 2, body, 0)

    @pl.when(n % 2 == 1)
    def _():
        single(hi - 1)


def _causal_tiles(i, pair, single):
    @pl.when(i == 0)
    def _():
        single(i, True)

    @pl.when(i > 0)
    def _():
        _in_pairs(0, i - 1, lambda a, b: pair(a, b, False), lambda a: single(a, False))
        pair(i - 1, i, True)


def _flash_fwd(q, kvb, krb, tables):
    s_len = q.shape[0]
    t = ATT_TILE

    def body(q_ref, cs_ref, s1_ref, s2_ref, kv_ref, kr_ref, o_ref, lse_ref, qb_ref, m_s, l_s, acc_s):
        qi = pl.program_id(1)
        m_s[...] = jnp.full_like(m_s, -jnp.inf)
        l_s[...] = jnp.zeros_like(l_s)
        acc_s[...] = jnp.zeros_like(acc_s)
        qv = q_ref[...]
        q = jnp.concatenate([qv[:, :NOPE], _rope(qv[:, NOPE:], cs_ref[...], s1_ref[...], s2_ref[...])], axis=1)
        q = (q * LOG2_SCALE).astype(BF16)
        qb_ref[...] = q

        def scores(ki, masked):
            rows = pl.ds(pl.multiple_of(ki * t, t), t)
            kv = kv_ref[rows, :]
            return _att_scores(q, kv, kr_ref[rows, :], masked, False)[0], kv

        def update(s, kv):
            m_prev = m_s[...]
            m_new = jnp.maximum(m_prev, jnp.max(s, axis=1, keepdims=True))
            alpha = jnp.exp2(m_prev - m_new)
            p = jnp.exp2(s - jnp.tile(m_new, (1, t // LANE)))
            l_s[...] = alpha * l_s[...] + jnp.sum(p, axis=1, keepdims=True)
            acc_s[...] = alpha * acc_s[...] + jnp.dot(p.astype(BF16), kv[:, NOPE:], preferred_element_type=F32)
            m_s[...] = m_new

        def pair(k0, k1, masked):
            first, second = scores(k0, False), scores(k1, masked)
            update(*first)
            update(*second)

        _causal_tiles(qi, pair, lambda ki, masked: update(*scores(ki, masked)))
        o_ref[...] = (acc_s[...] / l_s[...]).astype(o_ref.dtype)
        lse_ref[...] = (m_s[...] + jnp.log2(l_s[...]))[:, :1]

    table = pl.BlockSpec((t, LANE), lambda h, qi: (qi, 0))
    return pl.pallas_call(
        body, name="flash_fwd", grid=(HEADS, s_len // t),
        in_specs=[pl.BlockSpec((t, HEAD_PAD), lambda h, qi: (qi, h)), table, table, table,
                  pl.BlockSpec((s_len, HEAD_PAD), lambda h, qi: (0, h)),
                  pl.BlockSpec((s_len, LANE), lambda h, qi: (0, 0))],
        out_specs=[pl.BlockSpec((t, VHEAD), lambda h, qi: (qi, h)),
                   pl.BlockSpec((None, t, 1), lambda h, qi: (h, qi, 0)),
                   pl.BlockSpec((t, HEAD_PAD), lambda h, qi: (qi, h))],
        out_shape=[jax.ShapeDtypeStruct((s_len, HEADS * VHEAD), BF16),
                   jax.ShapeDtypeStruct((HEADS, s_len, 1), F32),
                   jax.ShapeDtypeStruct((s_len, HEADS * HEAD_PAD), BF16)],
        scratch_shapes=[pltpu.VMEM((t, LANE), F32), pltpu.VMEM((t, LANE), F32), pltpu.VMEM((t, VHEAD), F32)],
        compiler_params=_params(("parallel", "arbitrary")),
    )(q, *tables, kvb, krb)


def _flash_bwd(qb, kvb, krb, dob, lse_row, delta_row):
    s_len = qb.shape[0]
    t = ATT_TILE
    nq = s_len // t
    scale = QK_HEAD ** -0.5

    def body(q_ref, kv_ref, kr_ref, do_ref, lse_ref, dl_ref, dq_ref, dkv_ref, dkr_ref, dk_s, dv_s):
        ki = pl.program_id(1)

        @pl.when(ki == 0)
        def _():
            dq_ref[...] = jnp.zeros_like(dq_ref)

        dk_s[...] = jnp.zeros_like(dk_s)
        dv_s[...] = jnp.zeros_like(dv_s)
        kv, kr = kv_ref[...], kr_ref[...]

        def products(qi, masked):
            rows = pl.ds(pl.multiple_of(qi * t, t), t)
            q, do = q_ref[rows, :], do_ref[rows, :]
            st, k = _att_scores(q, kv, kr, masked, True)
            return st, lax.dot_general(kv[:, NOPE:], do, NT, preferred_element_type=F32), q, do, rows, k

        def update(st, dpt, q, do, rows, k):
            pt = jnp.exp2(st - lse_ref[:, rows])
            dv_s[...] += jnp.dot(pt.astype(BF16), do, preferred_element_type=F32)
            dst = (pt * (dpt - dl_ref[:, rows]) * scale).astype(BF16)
            dk_s[...] += jnp.dot(dst, q, preferred_element_type=F32)
            dq_ref[rows, :] += lax.dot_general(dst, k, TN, preferred_element_type=F32)

        def pair(q0, q1, masked):
            first, second = products(q0, masked), products(q1, False)
            update(*first)
            update(*second)

        @pl.when(ki == nq - 1)
        def _():
            update(*products(ki, True))

        @pl.when(ki < nq - 1)
        def _():
            pair(ki, ki + 1, True)
            _in_pairs(ki + 2, nq, lambda a, b: pair(a, b, False), lambda qi: update(*products(qi, False)))

        dk = dk_s[...] * (1.0 / LOG2_SCALE)
        dkv_ref[...] = jnp.concatenate([dk[:, :NOPE], dv_s[...]], axis=1).astype(dkv_ref.dtype)
        dkr_ref[...] = dk[:, NOPE:]

    row = pl.BlockSpec((None, 1, s_len), lambda h, ki: (h, 0, 0))
    return pl.pallas_call(
        body, name="flash_bwd", grid=(HEADS, nq),
        in_specs=[pl.BlockSpec((s_len, HEAD_PAD), lambda h, ki: (0, h)),
                  pl.BlockSpec((t, HEAD_PAD), lambda h, ki: (ki, h)),
                  pl.BlockSpec((t, LANE), lambda h, ki: (ki, 0)),
                  pl.BlockSpec((s_len, VHEAD), lambda h, ki: (0, h)), row, row],
        out_specs=[pl.BlockSpec((s_len, HEAD_PAD), lambda h, ki: (0, h)),
                   pl.BlockSpec((t, HEAD_PAD), lambda h, ki: (ki, h)),
                   pl.BlockSpec((t, LANE), lambda h, ki: (ki, h))],
        out_shape=[jax.ShapeDtypeStruct((s_len, HEADS * HEAD_PAD), F32),
                   jax.ShapeDtypeStruct((s_len, HEADS * HEAD_PAD), BF16),
                   jax.ShapeDtypeStruct((s_len, HEADS * LANE), F32)],
        scratch_shapes=[pltpu.VMEM((t, HEAD_PAD), F32), pltpu.VMEM((t, VHEAD), F32)],
        compiler_params=_params(("parallel", "arbitrary")),
    )(qb, kvb, krb, dob, lse_row, delta_row)


def _tril(w):
    r = lax.broadcasted_iota(jnp.int32, w.shape, 0)
    c = lax.broadcasted_iota(jnp.int32, w.shape, 1)
    return jnp.where(c <= r, w, 0.0)


def _sgu_row_stats(z_ref, width, gd, v_s, act, extra_s=None):
    total = None
    for g in range(SGU_GROUPS):
        cols = slice(g * gd, (g + 1) * gd)
        v = act(z_ref[:, width + g * gd:width + (g + 1) * gd])
        if extra_s is not None:
            v, extra_s[:, cols] = v
        v_s[:, cols] = v
        part = jnp.sum(v, axis=1, keepdims=True)
        total = part if total is None else total + part
    mean = total * (1.0 / width)
    sq = None
    for g in range(SGU_GROUPS):
        d = v_s[:, g * gd:(g + 1) * gd] - mean
        part = jnp.sum(d * d, axis=1, keepdims=True)
        sq = part if sq is None else sq + part
    return mean, lax.rsqrt(sq * (1.0 / width) + LN_EPS)


def _sgu_fwd(zpre, ln_g, ln_b, w_sp, bias_full):
    s_len, two_w = zpre.shape
    width = two_w // 2
    gd = width // SGU_GROUPS
    t = SGU_CHUNK

    def body(z_ref, g_ref, b_ref, w_ref, bias_ref, uv_ref, v_s):
        mean, rstd = _sgu_row_stats(z_ref, width, gd, v_s, _gelu)
        for g in range(SGU_GROUPS):
            cols = slice(g * gd, (g + 1) * gd)
            vln = ((v_s[:, cols] - mean) * rstd * g_ref[:, cols] + b_ref[:, cols]).astype(BF16)
            mixed = jnp.dot(_tril(w_ref[g]).astype(BF16), vln, preferred_element_type=F32) + bias_ref[:, cols]
            uv_ref[:, cols] = (_gelu(z_ref[:, cols]) * mixed).astype(uv_ref.dtype)

    return pl.pallas_call(
        body, name="sgu_fwd", grid=(s_len // t,),
        in_specs=[pl.BlockSpec((t, two_w), lambda i: (i, 0)), pl.BlockSpec((1, width), lambda i: (0, 0)),
                  pl.BlockSpec((1, width), lambda i: (0, 0)), pl.BlockSpec(w_sp.shape, lambda i: (0, 0, 0)),
                  pl.BlockSpec((t, width), lambda i: (0, 0))],
        out_specs=pl.BlockSpec((t, width), lambda i: (i, 0)),
        out_shape=jax.ShapeDtypeStruct((s_len, width), BF16),
        scratch_shapes=[pltpu.VMEM((t, width), F32)],
        compiler_params=_params(("parallel",)),
    )(zpre, ln_g, ln_b, w_sp, bias_full)


def _sgu_bwd(zpre, duv, ln_g, ln_b, w_sp, bias_full):
    s_len, two_w = zpre.shape
    width = two_w // 2
    gd = width // SGU_GROUPS
    t = SGU_CHUNK

    def body(z_ref, duv_ref, g_ref, b_ref, w_ref, bias_ref, dz_ref, dg_ref, db_ref, dw_ref, dbias_ref, v_s, vgrad_s, dvhat_s):
        @pl.when(pl.program_id(0) == 0)
        def _():
            for ref in (dg_ref, db_ref, dw_ref, dbias_ref):
                ref[...] = jnp.zeros_like(ref)

        def accumulate(ref, val):
            ref[...] += val

        mean, rstd = _sgu_row_stats(z_ref, width, gd, v_s, _gelu_and_grad, vgrad_s)
        sum_dvhat = sum_dvhat_vhat = None
        for g in range(SGU_GROUPS):
            cols = slice(g * gd, (g + 1) * gd)
            vhat = (v_s[:, cols] - mean) * rstd
            vln = (vhat * g_ref[:, cols] + b_ref[:, cols]).astype(BF16)
            wc = _tril(w_ref[g]).astype(BF16)
            mixed = jnp.dot(wc, vln, preferred_element_type=F32) + bias_ref[:, cols]
            u, u_grad = _gelu_and_grad(z_ref[:, cols])
            duv = duv_ref[:, cols]
            dz_ref[:, cols] = (duv * mixed * u_grad).astype(dz_ref.dtype)
            dmixed = duv * u
            dmb = dmixed.astype(BF16)
            dvln = lax.dot_general(wc, dmb, TN, preferred_element_type=F32)
            accumulate(dw_ref.at[g], _tril(lax.dot_general(dmb, vln, NT, preferred_element_type=F32)))
            accumulate(dbias_ref.at[:, g * LANE:(g + 1) * LANE],
                       jnp.broadcast_to(jnp.sum(dmixed, axis=1, keepdims=True), (t, LANE)))
            accumulate(dg_ref.at[:, cols], jnp.sum(dvln * vhat, axis=0, keepdims=True))
            accumulate(db_ref.at[:, cols], jnp.sum(dvln, axis=0, keepdims=True))
            dvhat = dvln * g_ref[:, cols]
            dvhat_s[:, cols] = dvhat
            parts = jnp.sum(dvhat, axis=1, keepdims=True), jnp.sum(dvhat * vhat, axis=1, keepdims=True)
            sum_dvhat = parts[0] if sum_dvhat is None else sum_dvhat + parts[0]
            sum_dvhat_vhat = parts[1] if sum_dvhat_vhat is None else sum_dvhat_vhat + parts[1]
        mean_dvhat, mean_dvhat_vhat = sum_dvhat * (1.0 / width), sum_dvhat_vhat * (1.0 / width)
        for g in range(SGU_GROUPS):
            cols = slice(g * gd, (g + 1) * gd)
            vhat = (v_s[:, cols] - mean) * rstd
            dv0 = rstd * (dvhat_s[:, cols] - mean_dvhat - vhat * mean_dvhat_vhat)
            dz_ref[:, width + g * gd:width + (g + 1) * gd] = (dv0 * vgrad_s[:, cols]).astype(dz_ref.dtype)

    vec = pl.BlockSpec((1, width), lambda i: (0, 0))
    return pl.pallas_call(
        body, name="sgu_bwd", grid=(s_len // t,),
        in_specs=[pl.BlockSpec((t, two_w), lambda i: (i, 0)), pl.BlockSpec((t, width), lambda i: (i, 0)), vec, vec,
                  pl.BlockSpec(w_sp.shape, lambda i: (0, 0, 0)), pl.BlockSpec((t, width), lambda i: (0, 0))],
        out_specs=[pl.BlockSpec((t, two_w), lambda i: (i, 0)), vec, vec,
                   pl.BlockSpec(w_sp.shape, lambda i: (0, 0, 0)), pl.BlockSpec((t, SGU_GROUPS * LANE), lambda i: (0, 0))],
        out_shape=[jax.ShapeDtypeStruct((s_len, two_w), BF16), jax.ShapeDtypeStruct((1, width), F32),
                   jax.ShapeDtypeStruct((1, width), F32), jax.ShapeDtypeStruct(w_sp.shape, F32),
                   jax.ShapeDtypeStruct((t, SGU_GROUPS * LANE), F32)],
        scratch_shapes=[pltpu.VMEM((t, width), F32)] * 3,
        compiler_params=_params(("arbitrary",)),
    )(zpre, duv, ln_g, ln_b, w_sp, bias_full)


def _rope_tables(positions):
    inv_freq = ROPE_THETA ** (-jnp.arange(0, ROPE, 2, dtype=F32) / ROPE)
    ang = positions.astype(F32)[:, None] * inv_freq
    cos, sin = jnp.cos(ang), jnp.sin(ang)
    z32, z64 = jnp.zeros_like(cos), jnp.zeros((cos.shape[0], LANE - ROPE), F32)
    return (jnp.concatenate([cos, cos, z64], axis=1), jnp.concatenate([z32, sin, z64], axis=1),
            jnp.concatenate([-sin, z32, z64], axis=1))


def _residual(g_next):
    if g_next is None:
        return dict(out_dtypes=(F32,), epilogue=lambda acc, res: (acc + res,))

    def epilogue(acc, res, gv):
        x_new = acc + res
        return x_new, _rms_fwd(x_new, gv)

    return dict(out_dtypes=(F32, BF16), epilogue=epilogue, consts=[g_next])


def _ffn_fwd(x, h2, w_up3, w_down3, g_next):
    def sq_relu(acc):
        r = jnp.maximum(acc, 0.0)
        return r * r, 2.0 * r

    r, r_grad = _mm_stacked("ffn_up", h2, w_up3, "col", (BF16, BF16), sq_relu)
    x_out, *h_next = _mm_stacked("ffn_down", r, w_down3, "row", extras=[x], **_residual(g_next))
    return x_out, (x, h2, r, r_grad), (h_next[0] if h_next else None)


def _ffn_bwd(dx, dxb, saved, g, w_up3, w_down3):
    x, h2, r, r_grad = saved
    da = _mm_stacked_nt("ffn_down_dx", dxb, w_down3, "row", (BF16,),
                        lambda acc, rg: (acc * rg.astype(F32),), [r_grad], tm=2 * MM_TILE)[0]
    g_down = _mm_tn_stacked("ffn_down_dw", r, dxb, w_down3.shape, "row")
    dx, dxb, dg = _mm_stacked_nt("ffn_up_dx", da, w_up3, "col", **_norm_bwd(x, g, dx))
    g_up = _mm_tn_stacked("ffn_up_dw", h2, da, w_up3.shape, "col")
    return dx, dxb, dg, g_up, g_down


def _norm_bwd(x, g, dres):
    def epilogue(dh, xv, rv, gv):
        dxv, dg = _rms_bwd(dh, xv, gv)
        return dxv + rv, dxv + rv, dg

    return dict(out_dtypes=(F32, BF16), epilogue=epilogue, extras=[x, dres], consts=[g], sums=[g.shape], tm=MM_TILE // 2)


def _dot(a, b, dims=NN):
    return lax.dot_general(a.astype(BF16), b.astype(BF16), dims, preferred_element_type=F32)


def _mla_fwd(x, g, wdkv, q_norm, kv_norm, wq, wkv, wo, tables, g_next):
    d = x.shape[1]

    def project(xv, cs, s1, s2, gv, wdkv_v, qg, kg, wq_v, wkv_v):
        h = _rms_fwd(xv, gv).astype(BF16)
        lv = _dot(h, wdkv_v)
        cqn = _rms_fwd(lv[:, :Q_RANK], qg).astype(BF16)
        ckvn = _rms_fwd(lv[:, Q_RANK:Q_RANK + KV_RANK], kg).astype(BF16)
        return (h, lv, cqn, ckvn, _rope(lv[:, Q_RANK + KV_RANK:], cs, s1, s2), _dot(cqn, wq_v), _dot(ckvn, wkv_v))

    h, lat, cqn, ckvn, krb, q, kvb = _rowwise(
        "mla_project", project, [x, *tables], [g, wdkv, q_norm, kv_norm, wq, wkv],
        [(d, BF16), (LAT_PAD, F32), (Q_RANK, BF16), (KV_RANK, BF16), (LANE, BF16), (wq.shape[1], F32), (wkv.shape[1], BF16)],
        tr=512)
    ob, lse, qb = _flash_fwd(q, kvb, krb, tables)
    x_mid, h2 = _mm("mla_o", ob, wo, extras=[x], **_residual(g_next))
    return x_mid, (x, h, lat, cqn, ckvn, krb, qb, kvb, ob, lse), h2


def _mla_bwd(dx, dxb, saved, g, wdkv, q_norm, kv_norm, wq, wkv, wo, tables):
    x, h, lat, cqn, ckvn, krb, qb, kvb, ob, lse = saved
    s_len = x.shape[0]
    def with_delta(do, ov):
        prod = do * ov.astype(F32)
        lane = lax.broadcasted_iota(jnp.int32, (do.shape[0], LANE), 1)
        by_lane = None
        for hd in range(HEADS):
            total = jnp.broadcast_to(jnp.sum(prod[:, hd * VHEAD:(hd + 1) * VHEAD], axis=1, keepdims=True), lane.shape)
            by_lane = jnp.where(lane == hd, total, 0.0 if by_lane is None else by_lane)
        return do, by_lane

    dob, delta_lanes = _mm("mla_o_dx", dxb, wo, (BF16,), with_delta, [ob], nt=True, tm=MM_TILE // 2, narrow=[(LANE, F32)])
    g_wo = _mm_tn("mla_o_dw", ob, dxb)[0]
    delta_row = delta_lanes[:, :HEADS].T.reshape(HEADS, 1, s_len)
    dq, dkvb, dkr = _flash_bwd(qb, kvb, krb, dob, lse.reshape(HEADS, 1, s_len), delta_row)

    def project_bwd(dq_f, dkv_v, dkr_v, lv, cqn_v, ckvn_v, h_v, xv, rv, cs, s1, s2, gv, qg, kg, wq_v, wkv_v, wdkv_v):
        parts = []
        for hd in range(HEADS):
            parts += [dq_f[:, hd * HEAD_PAD:hd * HEAD_PAD + NOPE],
                      _rope_t(dq_f[:, hd * HEAD_PAD + NOPE:(hd + 1) * HEAD_PAD], cs, s1, s2)]
        dq_v = jnp.concatenate(parts, axis=1).astype(BF16)
        dcq, dqg = _rms_bwd(_dot(dq_v, wq_v, NT), lv[:, :Q_RANK], qg)
        dckv, dkg = _rms_bwd(_dot(dkv_v, wkv_v, NT), lv[:, Q_RANK:Q_RANK + KV_RANK], kg)
        dkr_sum = dkr_v[:, :LANE]
        for hd in range(1, HEADS):
            dkr_sum = dkr_sum + dkr_v[:, hd * LANE:(hd + 1) * LANE]
        dlat = jnp.concatenate([dcq, dckv, _rope_t(dkr_sum, cs, s1, s2)], axis=1).astype(BF16)
        dxv, dg = _rms_bwd(_dot(dlat, wdkv_v, NT), xv, gv)
        return (dxv + rv, dxv + rv, _dot(cqn_v, dq_v, TN), _dot(ckvn_v, dkv_v, TN), _dot(h_v, dlat, TN), dqg, dkg, dg)

    dx, dxb, g_wq, g_wkv, g_wdkv, g_qn, g_kvn, dg = _rowwise(
        "mla_project_bwd", project_bwd, [dq, dkvb, dkr, lat, cqn, ckvn, h, x, dx, *tables],
        [g, q_norm, kv_norm, wq, wkv, wdkv], [(x.shape[1], F32), (x.shape[1], BF16)],
        [wq.shape, wkv.shape, wdkv.shape, q_norm.shape, kv_norm.shape, g.shape], tr=256)
    return dx, dxb, dg, g_wdkv, g_qn, g_kvn, g_wq, g_wkv, g_wo


def _sgu_layer_fwd(x, h, w_in3, ln_g, ln_b, w_sp, bias_full, w_out3, g_next):
    zpre = _mm_stacked("sgu_in", h, w_in3, "col")[0]
    uv = _sgu_fwd(zpre, ln_g, ln_b, w_sp, bias_full)
    x_mid, h2 = _mm_stacked("sgu_out", uv, w_out3, "row", extras=[x], **_residual(g_next))
    return x_mid, (x, h, zpre, uv), h2


def _sgu_layer_bwd(dx, dxb, saved, g, w_in3, ln_g, ln_b, w_sp, bias_full, w_out3):
    x, h, zpre, uv = saved
    duv = _mm_stacked_nt("sgu_out_dx", dxb, w_out3, "row")[0]
    g_out = _mm_tn_stacked("sgu_out_dw", uv, dxb, w_out3.shape, "row")
    dz, g_lng, g_lnb, g_wsp, g_bias = _sgu_bwd(zpre, duv, ln_g, ln_b, w_sp, bias_full)
    g_in = _mm_tn_stacked("sgu_in_dw", h, dz, w_in3.shape, "col")
    dx, dxb, dg = _mm_stacked_nt("sgu_in_dx", dz, w_in3, "col", **_norm_bwd(x, g, dx))
    return dx, dxb, dg, g_in, g_out, g_lng, g_lnb, g_wsp, g_bias


def _loss_head(x, target, g):
    d = x.shape[1]

    def fn(xv, tv, gv):
        err = _rms_fwd(xv, gv) - tv
        dxv, dg = _rms_bwd(err * (1.0 / d), xv, gv)
        return dxv, dxv, dg, jnp.sum(err * err, axis=0, keepdims=True)

    return _rowwise("loss_head", fn, [x, target], [g], [(d, F32), (d, BF16)], [g.shape, g.shape])


def _mixer_weights(i, stacks):
    by_rows = lambda a: a.reshape(N_SHARDS * a.shape[1], a.shape[2])
    by_cols = lambda a: a.transpose(1, 0, 2).reshape(a.shape[1], N_SHARDS * a.shape[2])
    if i % 2:
        w_in3, w_out3, ln_g, ln_b = stacks
        return w_in3, ln_g.reshape(1, -1), ln_b.reshape(1, -1), w_out3
    wdkv = by_rows(stacks[0])
    wdkv = jnp.pad(wdkv, ((0, 0), (0, LAT_PAD - wdkv.shape[1])))
    wq = jnp.pad(by_cols(stacks[1]).reshape(Q_RANK, HEADS, QK_HEAD), ((0, 0), (0, 0), (0, HEAD_PAD - QK_HEAD)))
    return wdkv, wq.reshape(Q_RANK, HEADS * HEAD_PAD), by_cols(stacks[2]), by_rows(stacks[3])


def _local_step(x, positions, target, norm_mix, norm_ffn, final_norm, q_norm, kv_norm, w_sp, b_sp, mixers, ffn, reducer):
    tables = _rope_tables(positions)
    gd = mixers[1][2].size // SGU_GROUPS
    bias_full = [jnp.repeat(b_sp[j].T, gd, axis=1) for j in range(DEPTH // 2)]
    saved, mla, sgu = [], [None] * (DEPTH // 2), [None] * (DEPTH // 2)
    h = None
    for i in range(DEPTH):
        j = i // 2
        x, *stacks = lax.optimization_barrier((x, *mixers[i]))
        if i % 2 == 0:
            wdkv, wq, wkv, wo = mla[j] = _mixer_weights(i, stacks)
            x, s_mix, h2 = _mla_fwd(x, norm_mix[i:i + 1], wdkv, q_norm[j:j + 1], kv_norm[j:j + 1], wq, wkv, wo, tables,
                                    norm_ffn[i:i + 1])
        else:
            w_in3, ln_g, ln_b, w_out3 = sgu[j] = _mixer_weights(i, stacks)
            x, s_mix, h2 = _sgu_layer_fwd(x, h, w_in3, ln_g, ln_b, w_sp[j], bias_full[j], w_out3, norm_ffn[i:i + 1])
        next_is_sgu = i + 1 < DEPTH and (i + 1) % 2 == 1
        x, s_ffn, h = _ffn_fwd(x, h2, *ffn[i], norm_mix[i + 1:i + 2] if next_is_sgu else None)
        saved.append((s_mix, s_ffn))
    dx, dxb, g_final, sq_cols = _loss_head(x, target, final_norm[None, :])
    loss = 0.5 * jnp.sum(sq_cols) / x.shape[1]

    def pair(g):
        return g, g.astype(BF16)

    g_mix, g_ffn = [None] * DEPTH, [None] * DEPTH
    mla_g, sgu_g = [None] * (DEPTH // 2), [None] * (DEPTH // 2)
    for i in reversed(range(DEPTH)):
        j = i // 2
        s_mix, s_ffn = saved[i]
        dx, dxb, g_ffn[i], g_up, g_down = _ffn_bwd(dx, dxb, s_ffn, norm_ffn[i:i + 1], *ffn[i])
        dxb = reducer.add(f"ffn{i}", i, {"ffn_w_up": g_up, "ffn_w_down": g_down}, dxb)
        dxb = reducer.phase_end(dxb)
        if i % 2 == 0:
            wdkv, wq, wkv, wo = mla[j]
            dx, dxb, g_mix[i], g_wdkv, g_qn, g_kvn, g_wq, g_wkv, g_wo = _mla_bwd(
                dx, dxb, s_mix, norm_mix[i:i + 1], wdkv, q_norm[j:j + 1], kv_norm[j:j + 1], wq, wkv, wo, tables)
            mla_g[j] = (g_qn, g_kvn)
            g_wq = g_wq.reshape(Q_RANK, HEADS, HEAD_PAD)[..., :QK_HEAD].reshape(Q_RANK, N_SHARDS, -1)
            dxb = reducer.add(f"mla{j}", j, {
                "mla_w_dkv": pair(g_wdkv[:, :Q_RANK + KV_RANK + ROPE].reshape(N_SHARDS, -1, Q_RANK + KV_RANK + ROPE)),
                "mla_w_uq": pair(g_wq.transpose(1, 0, 2)),
                "mla_w_ukv": pair(g_wkv.reshape(KV_RANK, N_SHARDS, -1).transpose(1, 0, 2)),
                "mla_w_o": pair(g_wo.reshape(N_SHARDS, -1, g_wo.shape[1]))}, dxb)
        else:
            w_in3, ln_g, ln_b, w_out3 = sgu[j]
            dx, dxb, g_mix[i], g_in, g_out, g_lng, g_lnb, g_wsp, g_bias = _sgu_layer_bwd(
                dx, dxb, s_mix, norm_mix[i:i + 1], w_in3, ln_g, ln_b, w_sp[j], bias_full[j], w_out3)
            sgu_g[j] = (g_wsp, g_bias.reshape(SGU_CHUNK, SGU_GROUPS, LANE)[:, :, 0].T)
            dxb = reducer.add(f"sgu{j}", j, {"sgu_w_in": g_in, "sgu_w_out": g_out,
                                             "sgu_ln_g": pair(g_lng.reshape(N_SHARDS, -1, LANE)),
                                             "sgu_ln_b": pair(g_lnb.reshape(N_SHARDS, -1, LANE))}, dxb)
        dxb = reducer.phase_end(dxb)
    small = dict(
        norm_mix=jnp.concatenate(g_mix, axis=0), norm_ffn=jnp.concatenate(g_ffn, axis=0), final_norm=g_final[0],
        q_norm=jnp.concatenate([m[0] for m in mla_g], axis=0), kv_norm=jnp.concatenate([m[1] for m in mla_g], axis=0),
        w_sp=jnp.stack([s[0] for s in sgu_g]), b_sp=jnp.stack([s[1] for s in sgu_g]))
    return loss, dx, small


HBM_SPEC = pl.BlockSpec(memory_space=pltpu.HBM)


def _place():
    x, y, c = lax.axis_index("x"), lax.axis_index("y"), lax.axis_index("c")
    return x, y, c, [(1 - x, y), (x, 1 - y), (1 - x, 1 - y)]


def _remote(src, dst, send_sems, recv_sems, k, to):
    return pltpu.make_async_remote_copy(src_ref=src, dst_ref=dst, send_sem=send_sems.at[k], recv_sem=recv_sems.at[k],
                                        device_id=to, device_id_type=MESH)


def _gather_layer(tag, shards):
    n = len(shards)
    split = [s.shape[0] >= 16 for s in shards]

    def body(*refs):
        ins, outs = refs[:n], refs[n:2 * n]
        send_sems, recv_sems, local_sems = refs[2 * n:]
        x, y, c, chips = _place()
        mine = 2 * x + y
        barrier = pltpu.get_barrier_semaphore()
        peers = [(x, y, 1 - c)] + [(*chip, c) for chip in chips]
        for peer in peers:
            pl.semaphore_signal(barrier, inc=1, device_id=peer, device_id_type=MESH)
        pl.semaphore_wait(barrier, len(peers))

        def rows(t, half):
            hr = shards[t].shape[0] // 2
            return pl.ds(half * hr, hr) if split[t] else pl.ds(0, shards[t].shape[0])

        local, sent = [], []
        for t in range(n):
            local.append(pltpu.make_async_copy(ins[t], outs[t].at[mine], local_sems.at[t]))
            local[-1].start()
            for j, chip in enumerate(chips):
                cp = _remote(ins[t].at[rows(t, c)], outs[t].at[mine, rows(t, c)], send_sems, recv_sems, 3 * t + j, (*chip, c))
                cp.start()
                sent.append(cp)
        for j, chip in enumerate(chips):
            theirs = 2 * chip[0] + chip[1]
            for t in range(n):
                piece = outs[t].at[theirs, rows(t, c)]
                _remote(piece, piece, send_sems, recv_sems, 3 * t + j, (x, y, c)).wait_recv()
                if split[t]:
                    cp = _remote(piece, piece, send_sems, recv_sems, 3 * n + 3 * t + j, (x, y, 1 - c))
                    cp.start()
                    sent.append(cp)
        for j, chip in enumerate(chips):
            theirs = 2 * chip[0] + chip[1]
            for t in range(n):
                if split[t]:
                    piece = outs[t].at[theirs, rows(t, 1 - c)]
                    _remote(piece, piece, send_sems, recv_sems, 3 * n + 3 * t + j, (x, y, c)).wait_recv()
        for cp in sent:
            cp.wait_send()
        for cp in local:
            cp.wait()

    return pl.kernel(
        body, name=f"gather_{tag}", mesh=plsc.ScalarSubcoreMesh(axis_name="sequencer", num_cores=1),
        out_type=[jax.ShapeDtypeStruct((N_SHARDS, *s.shape), s.dtype) for s in shards],
        scratch_types=[pltpu.SemaphoreType.DMA((6 * n,)), pltpu.SemaphoreType.DMA((6 * n,)), pltpu.SemaphoreType.DMA((n,))],
        compiler_params=pltpu.CompilerParams(collective_id=ID_GATHER),
    )(*shards)


SEQUENCER = dict(axis_name="sequencer", num_cores=1)
ID_GATHER, ID_EXCHANGE, ID_SHARE = 0, 1, 2
MIN_SPLIT_ROWS = 16


def _handshake(peers):
    barrier = pltpu.get_barrier_semaphore()
    for peer in peers:
        pl.semaphore_signal(barrier, inc=1, device_id=peer, device_id_type=MESH)
    pl.semaphore_wait(barrier, len(peers))


def _half_rows(rows, half):
    return pl.ds(half * (rows // 2), rows // 2) if rows >= MIN_SPLIT_ROWS else pl.ds(0, rows)


SEM_SPEC = pl.BlockSpec(memory_space=pltpu.SEMAPHORE)
DATAFLOW = pltpu.SideEffectType.DATAFLOW_SIDE_EFFECTING


def _exchange_copies(shapes, stacks, lands, send_sems, recv_sems):
    x, y, c, chips = _place()
    mine = 2 * x + y
    copies = []
    for t, shape in enumerate(shapes):
        r = shape[1]
        copies.append(_remote(stacks[t].at[mine, _half_rows(r, 1 - c)], lands[t].at[0], send_sems, recv_sems, 7 * t, (x, y, 1 - c)))
        for j, chip in enumerate(chips):
            theirs = 2 * chip[0] + chip[1]
            copies.append(_remote(stacks[t].at[theirs, _half_rows(r, c)], lands[t].at[1 + j], send_sems, recv_sems,
                                  7 * t + 1 + j, (*chip, c)))
            copies.append(_remote(stacks[t].at[theirs, _half_rows(r, 1 - c)], lands[t].at[4 + j], send_sems, recv_sems,
                                  7 * t + 4 + j, (*chip, 1 - c)))
    return copies


def _exchange_start(tag, stacks, carry):
    n = len(stacks)
    shapes = [s.shape for s in stacks]
    lands = [lax.empty((7, s.shape[1] // 2 if s.shape[1] >= MIN_SPLIT_ROWS else s.shape[1], s.shape[2]), s.dtype) for s in stacks]

    def body(*refs):
        for cp in _exchange_copies(shapes, refs[:n], refs[n:2 * n], refs[2 * n + 1], refs[2 * n + 2]):
            cp.start()

    through = (*stacks, *lands, carry)
    out = pl.pallas_call(
        body, name=f"reduce_exchange_start_{tag}",
        out_shape=(pltpu.SemaphoreType.DMA((7 * n,)), pltpu.SemaphoreType.DMA((7 * n,)),
                   *[pltpu.HBM(a.shape, a.dtype) for a in through]),
        in_specs=[HBM_SPEC] * (2 * n + 1),
        out_specs=(SEM_SPEC, SEM_SPEC, *[HBM_SPEC] * (2 * n + 1)),
        input_output_aliases={t: 2 + t for t in range(2 * n + 1)},
        compiler_params=pltpu.CompilerParams(has_side_effects=DATAFLOW),
    )(*[pltpu.with_memory_space_constraint(a, pltpu.HBM) for a in through])
    return out[0], out[1], out[2:2 + n], out[2 + n:2 + 2 * n], out[-1]


def _exchange_wait(tag, send_sems, recv_sems, stacks, lands, after):
    n = len(stacks)
    shapes = [s.shape for s in stacks]

    def body(*refs):
        for cp in _exchange_copies(shapes, refs[:n], refs[n:2 * n], refs[2 * n], refs[2 * n + 1]):
            cp.wait()

    out = pl.pallas_call(
        body, name=f"reduce_exchange_wait_{tag}",
        out_shape=tuple(pltpu.HBM(a.shape, a.dtype) for a in (*stacks, *lands)),
        in_specs=[HBM_SPEC] * (2 * n) + [SEM_SPEC, SEM_SPEC, pl.BlockSpec(memory_space=pl.ANY)],
        out_specs=tuple([HBM_SPEC] * (2 * n)),
        input_output_aliases={t: t for t in range(2 * n)},
        compiler_params=pltpu.CompilerParams(has_side_effects=DATAFLOW),
    )(*stacks, *lands, send_sems, recv_sems, after)
    return out[:n], out[n:]


def _share_halves(tag, halves):
    n = len(halves)

    def body(*refs):
        ins, outs, send_sems, recv_sems = refs[:n], refs[n:2 * n], refs[2 * n], refs[2 * n + 1]
        x, y, c, _ = _place()
        _handshake([(x, y, 1 - c)])
        sent = [_remote(ins[t], outs[t], send_sems, recv_sems, t, (x, y, 1 - c)) for t in range(n)]
        for cp in sent:
            cp.start()
        for cp in sent:
            cp.wait()

    return pl.kernel(
        body, name=f"reduce_share_{tag}", mesh=plsc.ScalarSubcoreMesh(**SEQUENCER),
        out_type=[jax.ShapeDtypeStruct(h.shape, h.dtype) for h in halves],
        scratch_types=[pltpu.SemaphoreType.DMA((n,)), pltpu.SemaphoreType.DMA((n,))],
        compiler_params=pltpu.CompilerParams(collective_id=ID_SHARE),
    )(*halves)


def _all_reduce_small(part):
    rows = part.shape[0]
    half = rows // 2

    def body(p_ref, out_ref, sib_buf, chip_sums, send_sems, recv_sems):
        x, y, c, chips = _place()
        mine = 2 * x + y
        my_rows = pl.ds(pl.multiple_of(c * half, 8), half)
        swap = _remote(p_ref, sib_buf, send_sems, recv_sems, 0, (x, y, 1 - c))
        swap.start()
        swap.wait()
        chip_sums[mine] = p_ref[...] + sib_buf[...]
        sent = [_remote(chip_sums.at[mine, my_rows], chip_sums.at[mine, my_rows], send_sems, recv_sems, 1 + j, (*chip, c))
                for j, chip in enumerate(chips)]
        for cp in sent:
            cp.start()
        for j, chip in enumerate(chips):
            sent[j].wait_send()
            theirs = chip_sums.at[2 * chip[0] + chip[1], my_rows]
            _remote(theirs, theirs, send_sems, recv_sems, 1 + j, (x, y, c)).wait_recv()
        out_ref[my_rows, :] = ((chip_sums[0, my_rows, :] + chip_sums[1, my_rows, :]) + chip_sums[2, my_rows, :]) + chip_sums[3, my_rows, :]
        share = _remote(out_ref.at[my_rows], out_ref.at[my_rows], send_sems, recv_sems, 4, (x, y, 1 - c))
        share.start()
        share.wait_send()
        other = out_ref.at[pl.ds(pl.multiple_of((1 - c) * half, 8), half)]
        _remote(other, other, send_sems, recv_sems, 4, (x, y, c)).wait_recv()

    vmem = pl.BlockSpec(memory_space=pltpu.VMEM)
    return pl.pallas_call(
        body, name="all_reduce_small", in_specs=[vmem], out_specs=vmem, out_shape=jax.ShapeDtypeStruct(part.shape, F32),
        scratch_shapes=[pltpu.VMEM((rows, LANE), F32), pltpu.VMEM((N_SHARDS, rows, LANE), F32),
                        pltpu.SemaphoreType.DMA((5,)), pltpu.SemaphoreType.DMA((5,))],
        compiler_params=pltpu.CompilerParams(vmem_limit_bytes=VMEM_LIMIT_BYTES),
    )(part)


def _sum_partials(g3, others, sel):
    _, rows, c = others.shape
    whole = g3.shape[1] == rows
    tr = _tile(rows, UPDATE_ROWS)
    nb = rows // tr

    def body(sel_ref, g_ref, *rest):
        same = g_ref[...].astype(F32)
        for ref in rest[1:4]:
            same = same + ref[...].astype(F32)
        other = rest[0][...].astype(F32)
        for ref in rest[4:7]:
            other = other + ref[...].astype(F32)
        rest[7][...] = same + other

    blk = (None, tr, c)
    slots = [pl.BlockSpec(blk, functools.partial(lambda i, sr, k: (k, i, 0), k=k)) for k in range(7)]
    return pl.pallas_call(
        body, name="reduce_sum_partials",
        grid_spec=pltpu.PrefetchScalarGridSpec(
            num_scalar_prefetch=1, grid=(nb,),
            in_specs=[pl.BlockSpec(blk, lambda i, sr: (sr[0], (0 if whole else sr[1] * nb) + i, 0))] + slots,
            out_specs=pl.BlockSpec((tr, c), lambda i, sr: (i, 0))),
        out_shape=jax.ShapeDtypeStruct((rows, c), F32),
        compiler_params=_params(("parallel",)),
    )(sel, g3, *[others] * 7)


def _adamw_math(w, g, m, v):
    nm = ADAM_B1 * m + (1.0 - ADAM_B1) * g
    nv = ADAM_B2 * v + (1.0 - ADAM_B2) * (g * g)
    m_hat = nm / (1.0 - ADAM_B1 ** ADAM_STEP)
    v_hat = nv / (1.0 - ADAM_B2 ** ADAM_STEP)
    return -ADAM_LR * (m_hat / (jnp.sqrt(v_hat) + ADAM_EPS) + ADAM_WD * w), nm, nv


def _adamw_layer(layer, w, m, v, g_mine, g_sibling, sel, prev):
    lyr, r, c = w.shape
    rows = g_mine.shape[0]
    halves = r // rows
    tr = _tile(rows, 512)
    nb = rows // tr
    n_g = 1 if g_sibling is None else 2

    def body(sel_ref, w_ref, m_ref, v_ref, *rest):
        g = rest[0][...]
        if n_g == 2:
            g = jnp.where(pl.program_id(0) == sel_ref[1], g, rest[1][...])
        outs = rest[n_g + (0 if prev is None else 4):]
        d, nm, nv = _adamw_math(w_ref[...], g, m_ref[...], v_ref[...])
        for ref, val in zip(outs, (g, d, nm, nv)):
            ref[...] = val

    full = pl.BlockSpec((None, tr, c), lambda h, i, sr: (layer, h * nb + i, 0))
    part = pl.BlockSpec((tr, c), lambda h, i, sr: (i, 0))
    n_in = 4 + n_g
    return pl.pallas_call(
        body, name="adamw_layer",
        grid_spec=pltpu.PrefetchScalarGridSpec(
            num_scalar_prefetch=1, grid=(halves, nb),
            in_specs=[full] * 3 + [part] * n_g + ([] if prev is None else [pl.BlockSpec(memory_space=pl.ANY)] * 4),
            out_specs=[full] * 4),
        out_shape=[jax.ShapeDtypeStruct(w.shape, F32)] * 4,
        input_output_aliases={} if prev is None else {n_in + k: k for k in range(4)},
        compiler_params=_params(("parallel", "parallel")),
    )(sel, w, m, v, g_mine, *([] if g_sibling is None else [g_sibling]), *([] if prev is None else prev))


def _adamw(w, g, m, v):
    lyr, r, c = w.shape
    tr = _tile(r, 256)

    def body(w_ref, g_ref, m_ref, v_ref, d_ref, nm_ref, nv_ref):
        d_ref[...], nm_ref[...], nv_ref[...] = _adamw_math(w_ref[...], g_ref[...], m_ref[...], v_ref[...])

    blk = pl.BlockSpec((None, tr, c), lambda l, i: (l, i, 0))
    return pl.pallas_call(
        body, name="adamw", grid=(lyr, r // tr), in_specs=[blk] * 4, out_specs=[blk] * 3,
        out_shape=[jax.ShapeDtypeStruct(w.shape, F32)] * 3,
        compiler_params=_params(("parallel", "parallel")),
    )(w, g, m, v)


SHARDED = ("mla_w_dkv", "mla_w_uq", "mla_w_ukv", "mla_w_o", "sgu_w_in", "sgu_ln_g", "sgu_ln_b", "sgu_w_out",
           "ffn_w_up", "ffn_w_down")
REPLICATED = ("norm_mix", "norm_ffn", "final_norm", "mla_q_norm", "mla_kv_norm", "sgu_w_spatial", "sgu_b_spatial")
WEIGHTS = ("norm_mix", "norm_ffn", "final_norm", "mla_w_dkv", "mla_q_norm", "mla_kv_norm", "mla_w_uq", "mla_w_ukv",
           "mla_w_o", "sgu_w_in", "sgu_ln_g", "sgu_ln_b", "sgu_w_spatial", "sgu_b_spatial", "sgu_w_out", "ffn_w_up",
           "ffn_w_down")


class _Reducer:
    def __init__(self, state, sel):
        self.state, self.sel = state, sel
        self.started, self.travelling, self.summed = [], [], []
        self.done = {}

    def add(self, tag, layer, grads, token):
        names = list(grads)
        token, *tied = lax.optimization_barrier((token, *[a for n in names for a in grads[n]]))
        f32s, bf16s = tied[0::2], tied[1::2]
        *flying, token = _exchange_start(tag, bf16s, token)
        self.started.append((tag, layer, names, f32s, flying))
        return token

    def phase_end(self, token):
        for tag, layer, names, f32s, flying in self.travelling:
            bf16s, received = _exchange_wait(tag, *flying, token)
            own = [g if g.shape[1] >= MIN_SPLIT_ROWS else gb for g, gb in zip(f32s, bf16s)]
            mine = [_sum_partials(g, got, self.sel) for g, got in zip(own, received)]
            token, *mine = lax.optimization_barrier((token, *mine))
            cut = [k for k, g in enumerate(own) if g.shape[1] >= MIN_SPLIT_ROWS]
            theirs = dict(zip(cut, _share_halves(tag, [mine[k] for k in cut])))
            self.summed.append((layer, names, mine, [theirs.get(k) for k in range(len(names))]))
        self.travelling, self.started = self.started, []
        return token

    def update(self, token):
        for layer, names, mine, theirs in self.summed:
            for name, g_mine, g_theirs in zip(names, mine, theirs):
                w, m, v = self.state[name]
                self.done[name] = _adamw_layer(layer, w, m, v, g_mine, g_theirs, self.sel, self.done.get(name))
                token = self.done[name][1]
        self.summed = []
        return token


def _as3d(name, a):
    return a.reshape(a.shape[0], -1, LANE) if name in ("sgu_ln_g", "sgu_ln_b") else a


def _pack(parts):
    flat = jnp.concatenate([p.reshape(-1) for p in parts])
    rows = -(-flat.shape[0] // (256 * LANE)) * 256
    return jnp.pad(flat, (0, rows * LANE - flat.shape[0])).reshape(rows, LANE)


def _unpack(packed, like):
    flat, out, at = packed.reshape(-1), [], 0
    for p in like:
        out.append(flat[at:at + p.size].reshape(p.shape))
        at += p.size
    return out


def kernel(x, positions, norm_mix, norm_ffn, final_norm, mla_w_dkv, mla_q_norm, mla_kv_norm, mla_w_uq, mla_w_ukv, mla_w_o, sgu_w_in, sgu_ln_g, sgu_ln_b, sgu_w_spatial, sgu_b_spatial, sgu_w_out, ffn_w_up, ffn_w_down, loss_target, m_norm_mix, m_norm_ffn, m_final_norm, m_mla_w_dkv, m_mla_q_norm, m_mla_kv_norm, m_mla_w_uq, m_mla_w_ukv, m_mla_w_o, m_sgu_w_in, m_sgu_ln_g, m_sgu_ln_b, m_sgu_w_spatial, m_sgu_b_spatial, m_sgu_w_out, m_ffn_w_up, m_ffn_w_down, v_norm_mix, v_norm_ffn, v_final_norm, v_mla_w_dkv, v_mla_q_norm, v_mla_kv_norm, v_mla_w_uq, v_mla_w_ukv, v_mla_w_o, v_sgu_w_in, v_sgu_ln_g, v_sgu_ln_b, v_sgu_w_spatial, v_sgu_b_spatial, v_sgu_w_out, v_ffn_w_up, v_ffn_w_down):
    given = dict(locals())
    w = {n: given[n] for n in WEIGHTS}
    mom = {n: given["m_" + n] for n in WEIGHTS}
    var = {n: given["v_" + n] for n in WEIGHTS}
    mixers, ffn, token = [], [], None
    for i in range(DEPTH):
        j = i // 2
        if i % 2 == 0:
            mixer = [w[n][j].astype(BF16) for n in ("mla_w_dkv", "mla_w_uq", "mla_w_ukv", "mla_w_o")]
        else:
            mixer = [sgu_w_in[j].astype(BF16), sgu_w_out[j].astype(BF16), sgu_ln_g[j].reshape(-1, LANE),
                     sgu_ln_b[j].reshape(-1, LANE)]
        for tag, shards, into in ((f"mixer{i}", mixer, mixers), (f"ffn{i}", [ffn_w_up[i].astype(BF16), ffn_w_down[i].astype(BF16)], ffn)):
            if token is None:
                token = shards[0]
            else:
                token, *shards = lax.optimization_barrier((token, *shards))
            into.append(_gather_layer(tag, shards))

    x_i, y_i, c_i = lax.axis_index("x"), lax.axis_index("y"), lax.axis_index("c")
    sel = jnp.stack([2 * x_i + y_i, c_i]).astype(jnp.int32)
    reducer = _Reducer({n: tuple(_as3d(n, d[n]) for d in (w, mom, var)) for n in SHARDED}, sel)
    loss, dx, small = _local_step(
        x[0], positions[0], loss_target[0], norm_mix, norm_ffn, final_norm, mla_q_norm, mla_kv_norm, sgu_w_spatial,
        sgu_b_spatial, mixers, ffn, reducer)
    loss = lax.psum(loss, ("x", "y", "c"))

    small_g = [small["norm_mix"], small["norm_ffn"], small["final_norm"], small["q_norm"], small["kv_norm"],
               small["w_sp"], small["b_sp"]]
    like = [w[n] for n in REPLICATED]
    g_small = _all_reduce_small(_pack(small_g))
    packed = [_pack([d[n] for n in REPLICATED])[None] for d in (w, mom, var)]
    upd_small = _adamw(packed[0], g_small[None], packed[1], packed[2])
    grads = dict(zip(REPLICATED, _unpack(g_small, like)))
    delta, new_m, new_v = ({n: a for n, a in zip(REPLICATED, _unpack(u[0], like))} for u in upd_small)

    reducer.phase_end(reducer.update(upd_small[0]))
    reducer.update(None)
    for n in SHARDED:
        grads[n], delta[n], new_m[n], new_v[n] = (a.reshape(w[n].shape) for a in reducer.done[n])

    return (loss, dx[None], *[grads[n] for n in WEIGHTS], *[delta[n] for n in WEIGHTS],
            *[new_m[n] for n in WEIGHTS], *[new_v[n] for n in WEIGHTS])
```

```python
import functools
import math

import jax
import jax.numpy as jnp
from jax import lax
from jax.experimental import pallas as pl
from jax.experimental.pallas import tpu as pltpu
from jax.experimental.pallas import tpu_sc as plsc

F32 = jnp.float32
BF16 = jnp.bfloat16
MESH = pl.DeviceIdType.MESH

DEPTH = 4
HEADS = 8
NOPE = 128
ROPE = 64
VHEAD = 128
QK_HEAD = NOPE + ROPE
Q_RANK = 256
KV_RANK = 128
HEAD_PAD = 256
LAT_PAD = 512
ROPE_THETA = 10000.0
SGU_CHUNK = 128
SGU_GROUPS = 8
NORM_EPS = 1e-6
LN_EPS = 1e-5
ADAM_LR, ADAM_B1, ADAM_B2, ADAM_EPS, ADAM_WD, ADAM_STEP = 0.001, 0.9, 0.999, 1e-08, 0.01, 10

N_SHARDS = 4
LANE = 128
VMEM_LIMIT_BYTES = 56 * 1024 * 1024
ATT_TILE = 512
MM_TILE = 1024
UPDATE_ROWS = 128
ATT_SCALE = QK_HEAD ** -0.5
LOG2_SCALE = ATT_SCALE * math.log2(math.e)

NN = (((1,), (0,)), ((), ()))
NT = (((1,), (1,)), ((), ()))
TN = (((0,), (0,)), ((), ()))


def _params(sem):
    return pltpu.CompilerParams(dimension_semantics=sem, vmem_limit_bytes=VMEM_LIMIT_BYTES)


def _tile(n, pref):
    t = min(n, pref)
    while n % t:
        t //= 2
    return t


def _matmul(name, a, b, a_spec, b_spec, dims, grid, tile, outs, extras=(), epilogue=None, sums=()):
    nk, ne, no = grid[2], len(extras), len(outs)
    b_specs = list(b_spec) if isinstance(b_spec, (list, tuple)) else [b_spec]
    nb = len(b_specs)

    def body(a_ref, *rest):
        b_refs, e_refs, o_refs = rest[:nb], rest[nb:nb + ne], rest[nb + ne:nb + ne + no]
        s_refs = rest[nb + ne + no:nb + ne + no + len(sums)]
        kw = a_ref.shape[1] // nb
        part = None
        for p, b_ref in enumerate(b_refs):
            a_tile = a_ref[...] if nb == 1 else a_ref[:, p * kw:(p + 1) * kw]
            d = lax.dot_general(a_tile.astype(BF16), b_ref[...].astype(BF16), dims, preferred_element_type=F32)
            part = d if part is None else part + d

        def finish(acc):
            vals = (acc,) if epilogue is None else epilogue(acc, *[e[...] for e in e_refs])
            for o_ref, v in zip(o_refs, vals):
                o_ref[...] = v.astype(o_ref.dtype)
            first = pl.program_id(0) == 0
            for s_ref, v in zip(s_refs, vals[no:]):
                @pl.when(first)
                def _():
                    s_ref[...] = v

                @pl.when(jnp.logical_not(first))
                def _():
                    s_ref[...] += v

        if nk == 1:
            finish(part)
            return
        acc_ref, k = rest[-1], pl.program_id(2)

        @pl.when(k == 0)
        def _():
            acc_ref[...] = part

        @pl.when(jnp.logical_and(k > 0, k < nk - 1))
        def _():
            acc_ref[...] += part

        @pl.when(k == nk - 1)
        def _():
            finish(acc_ref[...] + part)

    assert not sums or (grid[1] == 1 and nk == 1)
    return pl.pallas_call(
        body, name=name, grid=grid,
        in_specs=[a_spec] + b_specs + [s for _, s in extras],
        out_specs=[s for _, s in outs] + [pl.BlockSpec(s, lambda i, j, k: (0,) * len(s)) for s in sums],
        out_shape=[s for s, _ in outs] + [jax.ShapeDtypeStruct(s, F32) for s in sums],
        scratch_shapes=[pltpu.VMEM(tile, F32)] if nk > 1 else [],
        compiler_params=_params(("arbitrary" if sums else "parallel", "parallel", "arbitrary")),
    )(a, *[b] * nb, *[e for e, _ in extras])


def _epilogue_operands(extras, consts, o_spec):
    return [(e, o_spec) for e in extras] + [(c, pl.BlockSpec(c.shape, lambda i, j, k: (0, 0))) for c in consts]


def _mm(name, a, b, out_dtypes=(F32,), epilogue=None, extras=(), tm=MM_TILE, tn=MM_TILE, tk=MM_TILE, nt=False,
        consts=(), sums=(), narrow=()):
    m, kd = a.shape
    n = b.shape[0] if nt else b.shape[1]
    tm, tn, tk = _tile(m, tm), _tile(n, tn), _tile(kd, tk)
    o_spec = pl.BlockSpec((tm, tn), lambda i, j, k: (i, j))
    b_spec = pl.BlockSpec((tn, tk), lambda i, j, k: (j, k)) if nt else pl.BlockSpec((tk, tn), lambda i, j, k: (k, j))
    assert not narrow or n == tn
    outs = [(jax.ShapeDtypeStruct((m, n), d), o_spec) for d in out_dtypes]
    outs += [(jax.ShapeDtypeStruct((m, w), d), pl.BlockSpec((tm, w), lambda i, j, k: (i, 0))) for w, d in narrow]
    return _matmul(name, a, b, pl.BlockSpec((tm, tk), lambda i, j, k: (i, k)), b_spec, NT if nt else NN,
                   (m // tm, n // tn, kd // tk), (tm, tn), outs,
                   _epilogue_operands(extras, consts, o_spec), epilogue, sums)


def _mm_tn(name, a, b, out_dtypes=(F32,), tm=MM_TILE, tn=MM_TILE, tk=MM_TILE):
    s, m = a.shape
    n = b.shape[1]
    tm, tn, tk = _tile(m, tm), _tile(n, tn), _tile(s, tk)
    o_spec = pl.BlockSpec((tm, tn), lambda i, j, k: (i, j))
    return _matmul(name, a, b, pl.BlockSpec((tk, tm), lambda i, j, k: (k, i)),
                   pl.BlockSpec((tk, tn), lambda i, j, k: (k, j)), TN, (m // tm, n // tn, s // tk), (tm, tn),
                   [(jax.ShapeDtypeStruct((m, n), d), o_spec) for d in out_dtypes])


def _mm_stacked(name, a, w3, mode, out_dtypes=(F32,), epilogue=None, extras=(), tm=MM_TILE, tn=MM_TILE, tk=MM_TILE,
                consts=()):
    m, kd = a.shape
    _, r, c = w3.shape
    n = c if mode == "row" else N_SHARDS * c
    if mode == "row":
        tm, tn, tk = _tile(m, tm // 2), _tile(n, tn), kd
        b_spec = [pl.BlockSpec((None, r, tn), functools.partial(lambda i, j, k, p: (p, 0, j), p=p)) for p in range(N_SHARDS)]
    else:
        tm, tn, tk = _tile(m, tm), _tile(c, tn), _tile(kd, tk)
        per = c // tn
        b_spec = pl.BlockSpec((None, tk, tn), lambda i, j, k: (j // per, k, j % per))
    o_spec = pl.BlockSpec((tm, tn), lambda i, j, k: (i, j))
    return _matmul(name, a, w3, pl.BlockSpec((tm, tk), lambda i, j, k: (i, k)), b_spec, NN,
                   (m // tm, n // tn, kd // tk), (tm, tn),
                   [(jax.ShapeDtypeStruct((m, n), d), o_spec) for d in out_dtypes],
                   _epilogue_operands(extras, consts, o_spec), epilogue)


def _mm_stacked_nt(name, a, w3, mode, out_dtypes=(F32,), epilogue=None, extras=(), tm=MM_TILE, tn=MM_TILE, tk=MM_TILE,
                   consts=(), sums=()):
    m, nd = a.shape
    _, r, c = w3.shape
    kout = N_SHARDS * r if mode == "row" else r
    if mode == "row":
        tm, tn, tk = _tile(m, tm), _tile(r, tn), _tile(c, tk)
        per = r // tn
        b_spec = pl.BlockSpec((None, tn, tk), lambda i, j, k: (j // per, j % per, k))
    else:
        tm, tn, tk = _tile(m, tm // 2), _tile(r, tn), nd
        b_spec = [pl.BlockSpec((None, tn, c), functools.partial(lambda i, j, k, p: (p, j, 0), p=p)) for p in range(N_SHARDS)]
    o_spec = pl.BlockSpec((tm, tn), lambda i, j, k: (i, j))
    return _matmul(name, a, w3, pl.BlockSpec((tm, tk), lambda i, j, k: (i, k)), b_spec, NT,
                   (m // tm, kout // tn, nd // tk), (tm, tn),
                   [(jax.ShapeDtypeStruct((m, kout), d), o_spec) for d in out_dtypes],
                   _epilogue_operands(extras, consts, o_spec), epilogue, sums)


def _mm_tn_stacked(name, a, b, shape3, mode, tm=MM_TILE, tn=MM_TILE, tk=MM_TILE):
    s, m = a.shape
    n = b.shape[1]
    _, r, c = shape3
    tk, tn = s, tn // 2
    if mode == "row":
        tm, tn = _tile(r, tm), _tile(n, tn)
        per = r // tm
        o_spec = pl.BlockSpec((None, tm, tn), lambda i, j, k: (i // per, i % per, j))
    else:
        tm, tn = _tile(m, tm), _tile(c, tn)
        per = c // tn
        o_spec = pl.BlockSpec((None, tm, tn), lambda i, j, k: (j // per, i, j % per))
    outs = [(jax.ShapeDtypeStruct(shape3, F32), o_spec), (jax.ShapeDtypeStruct(shape3, BF16), o_spec)]
    return _matmul(name, a, b, pl.BlockSpec((tk, tm), lambda i, j, k: (k, i)),
                   pl.BlockSpec((tk, tn), lambda i, j, k: (k, j)), TN, (m // tm, n // tn, s // tk), (tm, tn),
                   outs, epilogue=lambda acc: (acc, acc))


def _rowwise(name, fn, rows, consts, out_rows, out_accs=(), tr=256):
    nr, nc, no = len(rows), len(consts), len(out_rows)
    n_rows = rows[0].shape[0]
    tr = _tile(n_rows, tr)

    def body(*refs):
        vals = fn(*[r[...] for r in refs[:nr + nc]])
        o_refs, a_refs = refs[nr + nc:nr + nc + no], refs[nr + nc + no:]
        for ref, v in zip(o_refs, vals[:no]):
            ref[...] = v.astype(ref.dtype)
        first = pl.program_id(0) == 0

        @pl.when(first)
        def _():
            for ref, v in zip(a_refs, vals[no:]):
                ref[...] = v

        @pl.when(jnp.logical_not(first))
        def _():
            for ref, v in zip(a_refs, vals[no:]):
                ref[...] += v

    def whole(shape):
        return pl.BlockSpec(shape, lambda i: (0,) * len(shape))

    return pl.pallas_call(
        body, name=name, grid=(n_rows // tr,),
        in_specs=[pl.BlockSpec((tr, a.shape[1]), lambda i: (i, 0)) for a in rows] + [whole(c.shape) for c in consts],
        out_specs=[pl.BlockSpec((tr, f), lambda i: (i, 0)) for f, _ in out_rows] + [whole(s) for s in out_accs],
        out_shape=[jax.ShapeDtypeStruct((n_rows, f), d) for f, d in out_rows]
        + [jax.ShapeDtypeStruct(s, F32) for s in out_accs],
        compiler_params=_params(("arbitrary",)),
    )(*rows, *consts)


def _rms_fwd(x, g):
    return x * lax.rsqrt(jnp.mean(x * x, axis=-1, keepdims=True) + NORM_EPS) * g


def _rms_bwd(dy, x, g):
    rstd = lax.rsqrt(jnp.mean(x * x, axis=-1, keepdims=True) + NORM_EPS)
    n = x * rstd
    dn = dy * g
    dx = rstd * (dn - n * jnp.mean(dn * n, axis=-1, keepdims=True))
    return dx, jnp.sum(dy * n, axis=0, keepdims=True)


def _rope(x, cs, s1, s2):
    return x * cs + pltpu.roll(x, 32, 1) * s1 + pltpu.roll(x, 96, 1) * s2


def _rope_t(dy, cs, s1, s2):
    return dy * cs + pltpu.roll(dy * s1, 96, 1) + pltpu.roll(dy * s2, 32, 1)


def _gelu(z):
    return 0.5 * z * (1.0 + lax.erf(z * (1.0 / math.sqrt(2.0))))


def _gelu_and_grad(z):
    cdf = 0.5 * (1.0 + lax.erf(z * (1.0 / math.sqrt(2.0))))
    return z * cdf, cdf + z * jnp.exp(-0.5 * z * z) * (1.0 / math.sqrt(2.0 * math.pi))


def _att_scores(q, kv, kr, masked, transposed):
    k = jnp.concatenate([kv[:, :NOPE], kr], axis=1)
    if transposed:
        s = lax.dot_general(k, q, NT, preferred_element_type=F32)
    else:
        s = lax.dot_general(q, k, NT, preferred_element_type=F32)
    if masked:
        r = lax.broadcasted_iota(jnp.int32, s.shape, 0)
        c = lax.broadcasted_iota(jnp.int32, s.shape, 1)
        s = jnp.where((r <= c) if transposed else (c <= r), s, -jnp.inf)
    return s, k


def _in_pairs(lo, hi, pair, single):
    n = hi - lo

    def body(p, carry):
        pair(lo + 2 * p, lo + 2 * p + 1)
        return carry

    lax.fori_loop(0, n // 2, body, 0)

    @pl.when(n % 2 == 1)
    def _():
        single(hi - 1)


def _causal_tiles(i, pair, single):
    @pl.when(i == 0)
    def _():
        single(i, True)

    @pl.when(i > 0)
    def _():
        _in_pairs(0, i - 1, lambda a, b: pair(a, b, False), lambda a: single(a, False))
        pair(i - 1, i, True)


def _flash_fwd(q, kvb, krb, tables):
    s_len = q.shape[0]
    t = ATT_TILE

    def body(q_ref, cs_ref, s1_ref, s2_ref, kv_ref, kr_ref, o_ref, lse_ref, qb_ref, m_s, l_s, acc_s):
        qi = pl.program_id(1)
        m_s[...] = jnp.full_like(m_s, -jnp.inf)
        l_s[...] = jnp.zeros_like(l_s)
        acc_s[...] = jnp.zeros_like(acc_s)
        qv = q_ref[...]
        q = jnp.concatenate([qv[:, :NOPE], _rope(qv[:, NOPE:], cs_ref[...], s1_ref[...], s2_ref[...])], axis=1)
        q = (q * LOG2_SCALE).astype(BF16)
        qb_ref[...] = q

        def scores(ki, masked):
            rows = pl.ds(pl.multiple_of(ki * t, t), t)
            kv = kv_ref[rows, :]
            return _att_scores(q, kv, kr_ref[rows, :], masked, False)[0], kv

        def update(s, kv):
            m_prev = m_s[...]
            m_new = jnp.maximum(m_prev, jnp.max(s, axis=1, keepdims=True))
            alpha = jnp.exp2(m_prev - m_new)
            p = jnp.exp2(s - jnp.tile(m_new, (1, t // LANE)))
            l_s[...] = alpha * l_s[...] + jnp.sum(p, axis=1, keepdims=True)
            acc_s[...] = alpha * acc_s[...] + jnp.dot(p.astype(BF16), kv[:, NOPE:], preferred_element_type=F32)
            m_s[...] = m_new

        def pair(k0, k1, masked):
            first, second = scores(k0, False), scores(k1, masked)
            update(*first)
            update(*second)

        _causal_tiles(qi, pair, lambda ki, masked: update(*scores(ki, masked)))
        o_ref[...] = (acc_s[...] / l_s[...]).astype(o_ref.dtype)
        lse_ref[...] = (m_s[...] + jnp.log2(l_s[...])).T[:1, :]

    table = pl.BlockSpec((t, LANE), lambda h, qi: (qi, 0))
    return pl.pallas_call(
        body, name="flash_fwd", grid=(HEADS, s_len // t),
        in_specs=[pl.BlockSpec((t, HEAD_PAD), lambda h, qi: (qi, h)), table, table, table,
                  pl.BlockSpec((s_len, HEAD_PAD), lambda h, qi: (0, h)),
                  pl.BlockSpec((s_len, LANE), lambda h, qi: (0, 0))],
        out_specs=[pl.BlockSpec((t, VHEAD), lambda h, qi: (qi, h)),
                   pl.BlockSpec((None, 1, t), lambda h, qi: (h, 0, qi)),
                   pl.BlockSpec((t, HEAD_PAD), lambda h, qi: (qi, h))],
        out_shape=[jax.ShapeDtypeStruct((s_len, HEADS * VHEAD), BF16),
                   jax.ShapeDtypeStruct((HEADS, 1, s_len), F32),
                   jax.ShapeDtypeStruct((s_len, HEADS * HEAD_PAD), BF16)],
        scratch_shapes=[pltpu.VMEM((t, LANE), F32), pltpu.VMEM((t, LANE), F32), pltpu.VMEM((t, VHEAD), F32)],
        compiler_params=_params(("parallel", "arbitrary")),
    )(q, *tables, kvb, krb)


def _flash_bwd(qb, kvb, krb, dob, lse_row, delta_row):
    s_len = qb.shape[0]
    t = ATT_TILE
    nq = s_len // t
    scale = QK_HEAD ** -0.5

    def body(q_ref, kv_ref, kr_ref, do_ref, lse_ref, dl_ref, dq_ref, dkv_ref, dkr_ref, dk_s, dv_s):
        ki = pl.program_id(1)

        @pl.when(ki == 0)
        def _():
            dq_ref[...] = jnp.zeros_like(dq_ref)

        dk_s[...] = jnp.zeros_like(dk_s)
        dv_s[...] = jnp.zeros_like(dv_s)
        kv, kr = kv_ref[...], kr_ref[...]

        def products(qi, masked):
            rows = pl.ds(pl.multiple_of(qi * t, t), t)
            q, do = q_ref[rows, :], do_ref[rows, :]
            st, k = _att_scores(q, kv, kr, masked, True)
            return st, lax.dot_general(kv[:, NOPE:], do, NT, preferred_element_type=F32), q, do, rows, k

        def update(st, dpt, q, do, rows, k):
            pt = jnp.exp2(st - lse_ref[:, rows])
            dv_s[...] += jnp.dot(pt.astype(BF16), do, preferred_element_type=F32)
            dst = (pt * (dpt - dl_ref[:, rows]) * scale).astype(BF16)
            dk_s[...] += jnp.dot(dst, q, preferred_element_type=F32)
            dq_ref[rows, :] += lax.dot_general(dst, k, TN, preferred_element_type=F32)

        def pair(q0, q1, masked):
            first, second = products(q0, masked), products(q1, False)
            update(*first)
            update(*second)

        @pl.when(ki == nq - 1)
        def _():
            update(*products(ki, True))

        @pl.when(ki < nq - 1)
        def _():
            pair(ki, ki + 1, True)
            _in_pairs(ki + 2, nq, lambda a, b: pair(a, b, False), lambda qi: update(*products(qi, False)))

        dk = dk_s[...] * (1.0 / LOG2_SCALE)
        dkv_ref[...] = jnp.concatenate([dk[:, :NOPE], dv_s[...]], axis=1).astype(dkv_ref.dtype)
        dkr_ref[...] = dk[:, NOPE:]

    row = pl.BlockSpec((None, 1, s_len), lambda h, ki: (h, 0, 0))
    return pl.pallas_call(
        body, name="flash_bwd", grid=(HEADS, nq),
        in_specs=[pl.BlockSpec((s_len, HEAD_PAD), lambda h, ki: (0, h)),
                  pl.BlockSpec((t, HEAD_PAD), lambda h, ki: (ki, h)),
                  pl.BlockSpec((t, LANE), lambda h, ki: (ki, 0)),
                  pl.BlockSpec((s_len, VHEAD), lambda h, ki: (0, h)), row, row],
        out_specs=[pl.BlockSpec((s_len, HEAD_PAD), lambda h, ki: (0, h)),
                   pl.BlockSpec((t, HEAD_PAD), lambda h, ki: (ki, h)),
                   pl.BlockSpec((t, LANE), lambda h, ki: (ki, h))],
        out_shape=[jax.ShapeDtypeStruct((s_len, HEADS * HEAD_PAD), F32),
                   jax.ShapeDtypeStruct((s_len, HEADS * HEAD_PAD), BF16),
                   jax.ShapeDtypeStruct((s_len, HEADS * LANE), F32)],
        scratch_shapes=[pltpu.VMEM((t, HEAD_PAD), F32), pltpu.VMEM((t, VHEAD), F32)],
        compiler_params=_params(("parallel", "arbitrary")),
    )(qb, kvb, krb, dob, lse_row, delta_row)


def _tril(w):
    r = lax.broadcasted_iota(jnp.int32, w.shape, 0)
    c = lax.broadcasted_iota(jnp.int32, w.shape, 1)
    return jnp.where(c <= r, w, 0.0)


def _sgu_row_stats(z_ref, width, gd, v_s, act, extra_s=None):
    total = None
    for g in range(SGU_GROUPS):
        cols = slice(g * gd, (g + 1) * gd)
        v = act(z_ref[:, width + g * gd:width + (g + 1) * gd])
        if extra_s is not None:
            v, extra_s[:, cols] = v
        v_s[:, cols] = v
        part = jnp.sum(v, axis=1, keepdims=True)
        total = part if total is None else total + part
    mean = total * (1.0 / width)
    sq = None
    for g in range(SGU_GROUPS):
        d = v_s[:, g * gd:(g + 1) * gd] - mean
        part = jnp.sum(d * d, axis=1, keepdims=True)
        sq = part if sq is None else sq + part
    return mean, lax.rsqrt(sq * (1.0 / width) + LN_EPS)


def _sgu_fwd(zpre, ln_g, ln_b, w_sp, bias_full):
    s_len, two_w = zpre.shape
    width = two_w // 2
    gd = width // SGU_GROUPS
    t = SGU_CHUNK

    def body(z_ref, g_ref, b_ref, w_ref, bias_ref, uv_ref, v_s):
        mean, rstd = _sgu_row_stats(z_ref, width, gd, v_s, _gelu)
        for g in range(SGU_GROUPS):
            cols = slice(g * gd, (g + 1) * gd)
            vln = ((v_s[:, cols] - mean) * rstd * g_ref[:, cols] + b_ref[:, cols]).astype(BF16)
            mixed = jnp.dot(_tril(w_ref[g]).astype(BF16), vln, preferred_element_type=F32) + bias_ref[:, cols]
            uv_ref[:, cols] = (_gelu(z_ref[:, cols]) * mixed).astype(uv_ref.dtype)

    return pl.pallas_call(
        body, name="sgu_fwd", grid=(s_len // t,),
        in_specs=[pl.BlockSpec((t, two_w), lambda i: (i, 0)), pl.BlockSpec((1, width), lambda i: (0, 0)),
                  pl.BlockSpec((1, width), lambda i: (0, 0)), pl.BlockSpec(w_sp.shape, lambda i: (0, 0, 0)),
                  pl.BlockSpec((t, width), lambda i: (0, 0))],
        out_specs=pl.BlockSpec((t, width), lambda i: (i, 0)),
        out_shape=jax.ShapeDtypeStruct((s_len, width), BF16),
        scratch_shapes=[pltpu.VMEM((t, width), F32)],
        compiler_params=_params(("parallel",)),
    )(zpre, ln_g, ln_b, w_sp, bias_full)


def _sgu_bwd(zpre, duv, ln_g, ln_b, w_sp, bias_full):
    s_len, two_w = zpre.shape
    width = two_w // 2
    gd = width // SGU_GROUPS
    t = SGU_CHUNK

    def body(z_ref, duv_ref, g_ref, b_ref, w_ref, bias_ref, dz_ref, dg_ref, db_ref, dw_ref, dbias_ref, v_s, vgrad_s, dvhat_s):
        @pl.when(pl.program_id(0) == 0)
        def _():
            for ref in (dg_ref, db_ref, dw_ref, dbias_ref):
                ref[...] = jnp.zeros_like(ref)

        def accumulate(ref, val):
            ref[...] += val

        mean, rstd = _sgu_row_stats(z_ref, width, gd, v_s, _gelu_and_grad, vgrad_s)
        sum_dvhat = sum_dvhat_vhat = None
        for g in range(SGU_GROUPS):
            cols = slice(g * gd, (g + 1) * gd)
            vhat = (v_s[:, cols] - mean) * rstd
            vln = (vhat * g_ref[:, cols] + b_ref[:, cols]).astype(BF16)
            wc = _tril(w_ref[g]).astype(BF16)
            mixed = jnp.dot(wc, vln, preferred_element_type=F32) + bias_ref[:, cols]
            u, u_grad = _gelu_and_grad(z_ref[:, cols])
            duv = duv_ref[:, cols]
            dz_ref[:, cols] = (duv * mixed * u_grad).astype(dz_ref.dtype)
            dmixed = duv * u
            dmb = dmixed.astype(BF16)
            dvln = lax.dot_general(wc, dmb, TN, preferred_element_type=F32)
            accumulate(dw_ref.at[g], _tril(lax.dot_general(dmb, vln, NT, preferred_element_type=F32)))
            accumulate(dbias_ref.at[:, g * LANE:(g + 1) * LANE],
                       jnp.broadcast_to(jnp.sum(dmixed, axis=1, keepdims=True), (t, LANE)))
            accumulate(dg_ref.at[:, cols], jnp.sum(dvln * vhat, axis=0, keepdims=True))
            accumulate(db_ref.at[:, cols], jnp.sum(dvln, axis=0, keepdims=True))
            dvhat = dvln * g_ref[:, cols]
            dvhat_s[:, cols] = dvhat
            parts = jnp.sum(dvhat, axis=1, keepdims=True), jnp.sum(dvhat * vhat, axis=1, keepdims=True)
            sum_dvhat = parts[0] if sum_dvhat is None else sum_dvhat + parts[0]
            sum_dvhat_vhat = parts[1] if sum_dvhat_vhat is None else sum_dvhat_vhat + parts[1]
        mean_dvhat, mean_dvhat_vhat = sum_dvhat * (1.0 / width), sum_dvhat_vhat * (1.0 / width)
        for g in range(SGU_GROUPS):
            cols = slice(g * gd, (g + 1) * gd)
            vhat = (v_s[:, cols] - mean) * rstd
            dv0 = rstd * (dvhat_s[:, cols] - mean_dvhat - vhat * mean_dvhat_vhat)
            dz_ref[:, width + g * gd:width + (g + 1) * gd] = (dv0 * vgrad_s[:, cols]).astype(dz_ref.dtype)

    vec = pl.BlockSpec((1, width), lambda i: (0, 0))
    return pl.pallas_call(
        body, name="sgu_bwd", grid=(s_len // t,),
        in_specs=[pl.BlockSpec((t, two_w), lambda i: (i, 0)), pl.BlockSpec((t, width), lambda i: (i, 0)), vec, vec,
                  pl.BlockSpec(w_sp.shape, lambda i: (0, 0, 0)), pl.BlockSpec((t, width), lambda i: (0, 0))],
        out_specs=[pl.BlockSpec((t, two_w), lambda i: (i, 0)), vec, vec,
                   pl.BlockSpec(w_sp.shape, lambda i: (0, 0, 0)), pl.BlockSpec((t, SGU_GROUPS * LANE), lambda i: (0, 0))],
        out_shape=[jax.ShapeDtypeStruct((s_len, two_w), BF16), jax.ShapeDtypeStruct((1, width), F32),
                   jax.ShapeDtypeStruct((1, width), F32), jax.ShapeDtypeStruct(w_sp.shape, F32),
                   jax.ShapeDtypeStruct((t, SGU_GROUPS * LANE), F32)],
        scratch_shapes=[pltpu.VMEM((t, width), F32)] * 3,
        compiler_params=_params(("arbitrary",)),
    )(zpre, duv, ln_g, ln_b, w_sp, bias_full)


def _rope_tables(positions):
    inv_freq = ROPE_THETA ** (-jnp.arange(0, ROPE, 2, dtype=F32) / ROPE)
    ang = positions.astype(F32)[:, None] * inv_freq
    cos, sin = jnp.cos(ang), jnp.sin(ang)
    z32, z64 = jnp.zeros_like(cos), jnp.zeros((cos.shape[0], LANE - ROPE), F32)
    return (jnp.concatenate([cos, cos, z64], axis=1), jnp.concatenate([z32, sin, z64], axis=1),
            jnp.concatenate([-sin, z32, z64], axis=1))


def _residual(g_next):
    if g_next is None:
        return dict(out_dtypes=(F32,), epilogue=lambda acc, res: (acc + res,))

    def epilogue(acc, res, gv):
        x_new = acc + res
        return x_new, _rms_fwd(x_new, gv)

    return dict(out_dtypes=(F32, BF16), epilogue=epilogue, consts=[g_next])


def _ffn_fwd(x, h2, w_up3, w_down3, g_next):
    def sq_relu(acc):
        r = jnp.maximum(acc, 0.0)
        return r * r, 2.0 * r

    r, r_grad = _mm_stacked("ffn_up", h2, w_up3, "col", (BF16, BF16), sq_relu)
    x_out, *h_next = _mm_stacked("ffn_down", r, w_down3, "row", extras=[x], **_residual(g_next))
    return x_out, (x, h2, r, r_grad), (h_next[0] if h_next else None)


def _ffn_bwd(dx, dxb, saved, g, w_up3, w_down3):
    x, h2, r, r_grad = saved
    da = _mm_stacked_nt("ffn_down_dx", dxb, w_down3, "row", (BF16,),
                        lambda acc, rg: (acc * rg.astype(F32),), [r_grad], tm=2 * MM_TILE)[0]
    g_down = _mm_tn_stacked("ffn_down_dw", r, dxb, w_down3.shape, "row")
    dx, dxb, dg = _mm_stacked_nt("ffn_up_dx", da, w_up3, "col", **_norm_bwd(x, g, dx))
    g_up = _mm_tn_stacked("ffn_up_dw", h2, da, w_up3.shape, "col")
    return dx, dxb, dg, g_up, g_down


def _norm_bwd(x, g, dres):
    def epilogue(dh, xv, rv, gv):
        dxv, dg = _rms_bwd(dh, xv, gv)
        return dxv + rv, dxv + rv, dg

    return dict(out_dtypes=(F32, BF16), epilogue=epilogue, extras=[x, dres], consts=[g], sums=[g.shape], tm=MM_TILE // 2)


def _dot(a, b, dims=NN):
    return lax.dot_general(a.astype(BF16), b.astype(BF16), dims, preferred_element_type=F32)


def _mla_fwd(x, g, wdkv, q_norm, kv_norm, wq, wkv, wo, tables, g_next):
    d = x.shape[1]

    def project(xv, cs, s1, s2, gv, wdkv_v, qg, kg, wq_v, wkv_v):
        h = _rms_fwd(xv, gv).astype(BF16)
        lv = _dot(h, wdkv_v)
        cqn = _rms_fwd(lv[:, :Q_RANK], qg).astype(BF16)
        ckvn = _rms_fwd(lv[:, Q_RANK:Q_RANK + KV_RANK], kg).astype(BF16)
        return (h, lv, cqn, ckvn, _rope(lv[:, Q_RANK + KV_RANK:], cs, s1, s2), _dot(cqn, wq_v), _dot(ckvn, wkv_v))

    h, lat, cqn, ckvn, krb, q, kvb = _rowwise(
        "mla_project", project, [x, *tables], [g, wdkv, q_norm, kv_norm, wq, wkv],
        [(d, BF16), (LAT_PAD, F32), (Q_RANK, BF16), (KV_RANK, BF16), (LANE, BF16), (wq.shape[1], F32), (wkv.shape[1], BF16)],
        tr=512)
    ob, lse, qb = _flash_fwd(q, kvb, krb, tables)
    x_mid, h2 = _mm("mla_o", ob, wo, extras=[x], **_residual(g_next))
    return x_mid, (x, h, lat, cqn, ckvn, krb, qb, kvb, ob, lse), h2


def _mla_bwd(dx, dxb, saved, g, wdkv, q_norm, kv_norm, wq, wkv, wo, tables):
    x, h, lat, cqn, ckvn, krb, qb, kvb, ob, lse = saved
    s_len = x.shape[0]
    def with_delta(do, ov):
        prod = do * ov.astype(F32)
        lane = lax.broadcasted_iota(jnp.int32, (do.shape[0], LANE), 1)
        by_lane = None
        for hd in range(HEADS):
            total = jnp.broadcast_to(jnp.sum(prod[:, hd * VHEAD:(hd + 1) * VHEAD], axis=1, keepdims=True), lane.shape)
            by_lane = jnp.where(lane == hd, total, 0.0 if by_lane is None else by_lane)
        return do, by_lane

    dob, delta_lanes = _mm("mla_o_dx", dxb, wo, (BF16,), with_delta, [ob], nt=True, tm=MM_TILE // 2, narrow=[(LANE, F32)])
    g_wo = _mm_tn("mla_o_dw", ob, dxb)[0]
    delta_row = delta_lanes[:, :HEADS].T.reshape(HEADS, 1, s_len)
    dq, dkvb, dkr = _flash_bwd(qb, kvb, krb, dob, lse, delta_row)

    def project_bwd(dq_f, dkv_v, dkr_v, lv, cqn_v, ckvn_v, h_v, xv, rv, cs, s1, s2, gv, qg, kg, wq_v, wkv_v, wdkv_v):
        parts = []
        for hd in range(HEADS):
            parts += [dq_f[:, hd * HEAD_PAD:hd * HEAD_PAD + NOPE],
                      _rope_t(dq_f[:, hd * HEAD_PAD + NOPE:(hd + 1) * HEAD_PAD], cs, s1, s2)]
        dq_v = jnp.concatenate(parts, axis=1).astype(BF16)
        dcq, dqg = _rms_bwd(_dot(dq_v, wq_v, NT), lv[:, :Q_RANK], qg)
        dckv, dkg = _rms_bwd(_dot(dkv_v, wkv_v, NT), lv[:, Q_RANK:Q_RANK + KV_RANK], kg)
        dkr_sum = dkr_v[:, :LANE]
        for hd in range(1, HEADS):
            dkr_sum = dkr_sum + dkr_v[:, hd * LANE:(hd + 1) * LANE]
        dlat = jnp.concatenate([dcq, dckv, _rope_t(dkr_sum, cs, s1, s2)], axis=1).astype(BF16)
        dxv, dg = _rms_bwd(_dot(dlat, wdkv_v, NT), xv, gv)
        return (dxv + rv, dxv + rv, _dot(cqn_v, dq_v, TN), _dot(ckvn_v, dkv_v, TN), _dot(h_v, dlat, TN), dqg, dkg, dg)

    dx, dxb, g_wq, g_wkv, g_wdkv, g_qn, g_kvn, dg = _rowwise(
        "mla_project_bwd", project_bwd, [dq, dkvb, dkr, lat, cqn, ckvn, h, x, dx, *tables],
        [g, q_norm, kv_norm, wq, wkv, wdkv], [(x.shape[1], F32), (x.shape[1], BF16)],
        [wq.shape, wkv.shape, wdkv.shape, q_norm.shape, kv_norm.shape, g.shape], tr=256)
    return dx, dxb, dg, g_wdkv, g_qn, g_kvn, g_wq, g_wkv, g_wo


def _sgu_layer_fwd(x, h, w_in3, ln_g, ln_b, w_sp, bias_full, w_out3, g_next):
    zpre = _mm_stacked("sgu_in", h, w_in3, "col")[0]
    uv = _sgu_fwd(zpre, ln_g, ln_b, w_sp, bias_full)
    x_mid, h2 = _mm_stacked("sgu_out", uv, w_out3, "row", extras=[x], **_residual(g_next))
    return x_mid, (x, h, zpre, uv), h2


def _sgu_layer_bwd(dx, dxb, saved, g, w_in3, ln_g, ln_b, w_sp, bias_full, w_out3):
    x, h, zpre, uv = saved
    duv = _mm_stacked_nt("sgu_out_dx", dxb, w_out3, "row")[0]
    g_out = _mm_tn_stacked("sgu_out_dw", uv, dxb, w_out3.shape, "row")
    dz, g_lng, g_lnb, g_wsp, g_bias = _sgu_bwd(zpre, duv, ln_g, ln_b, w_sp, bias_full)
    g_in = _mm_tn_stacked("sgu_in_dw", h, dz, w_in3.shape, "col")
    dx, dxb, dg = _mm_stacked_nt("sgu_in_dx", dz, w_in3, "col", **_norm_bwd(x, g, dx))
    return dx, dxb, dg, g_in, g_out, g_lng, g_lnb, g_wsp, g_bias


def _loss_head(x, target, g):
    d = x.shape[1]

    def fn(xv, tv, gv):
        err = _rms_fwd(xv, gv) - tv
        dxv, dg = _rms_bwd(err * (1.0 / d), xv, gv)
        return dxv, dxv, dg, jnp.sum(err * err, axis=0, keepdims=True)

    return _rowwise("loss_head", fn, [x, target], [g], [(d, F32), (d, BF16)], [g.shape, g.shape])


def _mixer_weights(i, stacks):
    by_rows = lambda a: a.reshape(N_SHARDS * a.shape[1], a.shape[2])
    by_cols = lambda a: a.transpose(1, 0, 2).reshape(a.shape[1], N_SHARDS * a.shape[2])
    if i % 2:
        w_in3, w_out3, ln_g, ln_b = stacks
        return w_in3, ln_g.reshape(1, -1), ln_b.reshape(1, -1), w_out3
    wdkv = by_rows(stacks[0])
    wdkv = jnp.pad(wdkv, ((0, 0), (0, LAT_PAD - wdkv.shape[1])))
    wq = jnp.pad(by_cols(stacks[1]).reshape(Q_RANK, HEADS, QK_HEAD), ((0, 0), (0, 0), (0, HEAD_PAD - QK_HEAD)))
    return wdkv, wq.reshape(Q_RANK, HEADS * HEAD_PAD), by_cols(stacks[2]), by_rows(stacks[3])


def _local_step(x, positions, target, norm_mix, norm_ffn, final_norm, q_norm, kv_norm, w_sp, b_sp, mixers, ffn, reducer):
    tables = _rope_tables(positions)
    gd = mixers[1][2].size // SGU_GROUPS
    bias_full = [jnp.repeat(b_sp[j].T, gd, axis=1) for j in range(DEPTH // 2)]
    saved, mla, sgu = [], [None] * (DEPTH // 2), [None] * (DEPTH // 2)
    h = None
    for i in range(DEPTH):
        j = i // 2
        x, *stacks = lax.optimization_barrier((x, *mixers[i]))
        if i % 2 == 0:
            wdkv, wq, wkv, wo = mla[j] = _mixer_weights(i, stacks)
            x, s_mix, h2 = _mla_fwd(x, norm_mix[i:i + 1], wdkv, q_norm[j:j + 1], kv_norm[j:j + 1], wq, wkv, wo, tables,
                                    norm_ffn[i:i + 1])
        else:
            w_in3, ln_g, ln_b, w_out3 = sgu[j] = _mixer_weights(i, stacks)
            x, s_mix, h2 = _sgu_layer_fwd(x, h, w_in3, ln_g, ln_b, w_sp[j], bias_full[j], w_out3, norm_ffn[i:i + 1])
        next_is_sgu = i + 1 < DEPTH and (i + 1) % 2 == 1
        x, s_ffn, h = _ffn_fwd(x, h2, *ffn[i], norm_mix[i + 1:i + 2] if next_is_sgu else None)
        saved.append((s_mix, s_ffn))
    dx, dxb, g_final, sq_cols = _loss_head(x, target, final_norm[None, :])
    loss = 0.5 * jnp.sum(sq_cols) / x.shape[1]

    def pair(g):
        return g, g.astype(BF16)

    g_mix, g_ffn = [None] * DEPTH, [None] * DEPTH
    mla_g, sgu_g = [None] * (DEPTH // 2), [None] * (DEPTH // 2)
    for i in reversed(range(DEPTH)):
        j = i // 2
        s_mix, s_ffn = saved[i]
        dx, dxb, g_ffn[i], g_up, g_down = _ffn_bwd(dx, dxb, s_ffn, norm_ffn[i:i + 1], *ffn[i])
        dxb = reducer.add(f"ffn{i}", i, {"ffn_w_up": g_up, "ffn_w_down": g_down}, dxb)
        dxb = reducer.phase_end(dxb)
        if i % 2 == 0:
            wdkv, wq, wkv, wo = mla[j]
            dx, dxb, g_mix[i], g_wdkv, g_qn, g_kvn, g_wq, g_wkv, g_wo = _mla_bwd(
                dx, dxb, s_mix, norm_mix[i:i + 1], wdkv, q_norm[j:j + 1], kv_norm[j:j + 1], wq, wkv, wo, tables)
            mla_g[j] = (g_qn, g_kvn)
            g_wq = g_wq.reshape(Q_RANK, HEADS, HEAD_PAD)[..., :QK_HEAD].reshape(Q_RANK, N_SHARDS, -1)
            dxb = reducer.add(f"mla{j}", j, {
                "mla_w_dkv": pair(g_wdkv[:, :Q_RANK + KV_RANK + ROPE].reshape(N_SHARDS, -1, Q_RANK + KV_RANK + ROPE)),
                "mla_w_uq": pair(g_wq.transpose(1, 0, 2)),
                "mla_w_ukv": pair(g_wkv.reshape(KV_RANK, N_SHARDS, -1).transpose(1, 0, 2)),
                "mla_w_o": pair(g_wo.reshape(N_SHARDS, -1, g_wo.shape[1]))}, dxb)
        else:
            w_in3, ln_g, ln_b, w_out3 = sgu[j]
            dx, dxb, g_mix[i], g_in, g_out, g_lng, g_lnb, g_wsp, g_bias = _sgu_layer_bwd(
                dx, dxb, s_mix, norm_mix[i:i + 1], w_in3, ln_g, ln_b, w_sp[j], bias_full[j], w_out3)
            sgu_g[j] = (g_wsp, g_bias.reshape(SGU_CHUNK, SGU_GROUPS, LANE)[:, :, 0].T)
            dxb = reducer.add(f"sgu{j}", j, {"sgu_w_in": g_in, "sgu_w_out": g_out,
                                             "sgu_ln_g": pair(g_lng.reshape(N_SHARDS, -1, LANE)),
                                             "sgu_ln_b": pair(g_lnb.reshape(N_SHARDS, -1, LANE))}, dxb)
        dxb = reducer.phase_end(dxb)
    small = dict(
        norm_mix=jnp.concatenate(g_mix, axis=0), norm_ffn=jnp.concatenate(g_ffn, axis=0), final_norm=g_final[0],
        q_norm=jnp.concatenate([m[0] for m in mla_g], axis=0), kv_norm=jnp.concatenate([m[1] for m in mla_g], axis=0),
        w_sp=jnp.stack([s[0] for s in sgu_g]), b_sp=jnp.stack([s[1] for s in sgu_g]))
    return loss, dx, small


HBM_SPEC = pl.BlockSpec(memory_space=pltpu.HBM)


def _place():
    x, y, c = lax.axis_index("x"), lax.axis_index("y"), lax.axis_index("c")
    return x, y, c, [(1 - x, y), (x, 1 - y), (1 - x, 1 - y)]


def _remote(src, dst, send_sems, recv_sems, k, to):
    return pltpu.make_async_remote_copy(src_ref=src, dst_ref=dst, send_sem=send_sems.at[k], recv_sem=recv_sems.at[k],
                                        device_id=to, device_id_type=MESH)


def _gather_layer(tag, shards):
    n = len(shards)
    split = [s.shape[0] >= 16 for s in shards]

    def body(*refs):
        ins, outs = refs[:n], refs[n:2 * n]
        send_sems, recv_sems, local_sems = refs[2 * n:]
        x, y, c, chips = _place()
        mine = 2 * x + y
        barrier = pltpu.get_barrier_semaphore()
        peers = [(x, y, 1 - c)] + [(*chip, c) for chip in chips]
        for peer in peers:
            pl.semaphore_signal(barrier, inc=1, device_id=peer, device_id_type=MESH)
        pl.semaphore_wait(barrier, len(peers))

        def rows(t, half):
            hr = shards[t].shape[0] // 2
            return pl.ds(half * hr, hr) if split[t] else pl.ds(0, shards[t].shape[0])

        local, sent = [], []
        for t in range(n):
            local.append(pltpu.make_async_copy(ins[t], outs[t].at[mine], local_sems.at[t]))
            local[-1].start()
            for j, chip in enumerate(chips):
                cp = _remote(ins[t].at[rows(t, c)], outs[t].at[mine, rows(t, c)], send_sems, recv_sems, 3 * t + j, (*chip, c))
                cp.start()
                sent.append(cp)
        for j, chip in enumerate(chips):
            theirs = 2 * chip[0] + chip[1]
            for t in range(n):
                piece = outs[t].at[theirs, rows(t, c)]
                _remote(piece, piece, send_sems, recv_sems, 3 * t + j, (x, y, c)).wait_recv()
                if split[t]:
                    cp = _remote(piece, piece, send_sems, recv_sems, 3 * n + 3 * t + j, (x, y, 1 - c))
                    cp.start()
                    sent.append(cp)
        for j, chip in enumerate(chips):
            theirs = 2 * chip[0] + chip[1]
            for t in range(n):
                if split[t]:
                    piece = outs[t].at[theirs, rows(t, 1 - c)]
                    _remote(piece, piece, send_sems, recv_sems, 3 * n + 3 * t + j, (x, y, c)).wait_recv()
        for cp in sent:
            cp.wait_send()
        for cp in local:
            cp.wait()

    return pl.kernel(
        body, name=f"gather_{tag}", mesh=plsc.ScalarSubcoreMesh(axis_name="sequencer", num_cores=1),
        out_type=[jax.ShapeDtypeStruct((N_SHARDS, *s.shape), s.dtype) for s in shards],
        scratch_types=[pltpu.SemaphoreType.DMA((6 * n,)), pltpu.SemaphoreType.DMA((6 * n,)), pltpu.SemaphoreType.DMA((n,))],
        compiler_params=pltpu.CompilerParams(collective_id=ID_GATHER),
    )(*shards)


SEQUENCER = dict(axis_name="sequencer", num_cores=1)
ID_GATHER, ID_EXCHANGE, ID_SHARE = 0, 1, 2
MIN_SPLIT_ROWS = 16


def _handshake(peers):
    barrier = pltpu.get_barrier_semaphore()
    for peer in peers:
        pl.semaphore_signal(barrier, inc=1, device_id=peer, device_id_type=MESH)
    pl.semaphore_wait(barrier, len(peers))


def _half_rows(rows, half):
    return pl.ds(half * (rows // 2), rows // 2) if rows >= MIN_SPLIT_ROWS else pl.ds(0, rows)


SEM_SPEC = pl.BlockSpec(memory_space=pltpu.SEMAPHORE)
DATAFLOW = pltpu.SideEffectType.DATAFLOW_SIDE_EFFECTING


def _exchange_copies(shapes, stacks, lands, send_sems, recv_sems):
    x, y, c, chips = _place()
    mine = 2 * x + y
    copies = []
    for t, shape in enumerate(shapes):
        r = shape[1]
        copies.append(_remote(stacks[t].at[mine, _half_rows(r, 1 - c)], lands[t].at[0], send_sems, recv_sems, 7 * t, (x, y, 1 - c)))
        for j, chip in enumerate(chips):
            theirs = 2 * chip[0] + chip[1]
            copies.append(_remote(stacks[t].at[theirs, _half_rows(r, c)], lands[t].at[1 + j], send_sems, recv_sems,
                                  7 * t + 1 + j, (*chip, c)))
            copies.append(_remote(stacks[t].at[theirs, _half_rows(r, 1 - c)], lands[t].at[4 + j], send_sems, recv_sems,
                                  7 * t + 4 + j, (*chip, 1 - c)))
    return copies


def _exchange_start(tag, stacks, carry):
    n = len(stacks)
    shapes = [s.shape for s in stacks]
    lands = [lax.empty((7, s.shape[1] // 2 if s.shape[1] >= MIN_SPLIT_ROWS else s.shape[1], s.shape[2]), s.dtype) for s in stacks]

    def body(*refs):
        for cp in _exchange_copies(shapes, refs[:n], refs[n:2 * n], refs[2 * n + 1], refs[2 * n + 2]):
            cp.start()

    through = (*stacks, *lands, carry)
    out = pl.pallas_call(
        body, name=f"reduce_exchange_start_{tag}",
        out_shape=(pltpu.SemaphoreType.DMA((7 * n,)), pltpu.SemaphoreType.DMA((7 * n,)),
                   *[pltpu.HBM(a.shape, a.dtype) for a in through]),
        in_specs=[HBM_SPEC] * (2 * n + 1),
        out_specs=(SEM_SPEC, SEM_SPEC, *[HBM_SPEC] * (2 * n + 1)),
        input_output_aliases={t: 2 + t for t in range(2 * n + 1)},
        compiler_params=pltpu.CompilerParams(has_side_effects=DATAFLOW),
    )(*[pltpu.with_memory_space_constraint(a, pltpu.HBM) for a in through])
    return out[0], out[1], out[2:2 + n], out[2 + n:2 + 2 * n], out[-1]


def _exchange_wait(tag, send_sems, recv_sems, stacks, lands, after):
    n = len(stacks)
    shapes = [s.shape for s in stacks]

    def body(*refs):
        for cp in _exchange_copies(shapes, refs[:n], refs[n:2 * n], refs[2 * n], refs[2 * n + 1]):
            cp.wait()

    out = pl.pallas_call(
        body, name=f"reduce_exchange_wait_{tag}",
        out_shape=tuple(pltpu.HBM(a.shape, a.dtype) for a in (*stacks, *lands)),
        in_specs=[HBM_SPEC] * (2 * n) + [SEM_SPEC, SEM_SPEC, pl.BlockSpec(memory_space=pl.ANY)],
        out_specs=tuple([HBM_SPEC] * (2 * n)),
        input_output_aliases={t: t for t in range(2 * n)},
        compiler_params=pltpu.CompilerParams(has_side_effects=DATAFLOW),
    )(*stacks, *lands, send_sems, recv_sems, after)
    return out[:n], out[n:]


def _share_halves(tag, halves):
    n = len(halves)

    def body(*refs):
        ins, outs, send_sems, recv_sems = refs[:n], refs[n:2 * n], refs[2 * n], refs[2 * n + 1]
        x, y, c, _ = _place()
        _handshake([(x, y, 1 - c)])
        sent = [_remote(ins[t], outs[t], send_sems, recv_sems, t, (x, y, 1 - c)) for t in range(n)]
        for cp in sent:
            cp.start()
        for cp in sent:
            cp.wait()

    return pl.kernel(
        body, name=f"reduce_share_{tag}", mesh=plsc.ScalarSubcoreMesh(**SEQUENCER),
        out_type=[jax.ShapeDtypeStruct(h.shape, h.dtype) for h in halves],
        scratch_types=[pltpu.SemaphoreType.DMA((n,)), pltpu.SemaphoreType.DMA((n,))],
        compiler_params=pltpu.CompilerParams(collective_id=ID_SHARE),
    )(*halves)


def _all_reduce_small(part):
    rows = part.shape[0]
    half = rows // 2

    def body(p_ref, out_ref, sib_buf, chip_sums, send_sems, recv_sems):
        x, y, c, chips = _place()
        mine = 2 * x + y
        my_rows = pl.ds(pl.multiple_of(c * half, 8), half)
        swap = _remote(p_ref, sib_buf, send_sems, recv_sems, 0, (x, y, 1 - c))
        swap.start()
        swap.wait()
        chip_sums[mine] = p_ref[...] + sib_buf[...]
        sent = [_remote(chip_sums.at[mine, my_rows], chip_sums.at[mine, my_rows], send_sems, recv_sems, 1 + j, (*chip, c))
                for j, chip in enumerate(chips)]
        for cp in sent:
            cp.start()
        for j, chip in enumerate(chips):
            sent[j].wait_send()
            theirs = chip_sums.at[2 * chip[0] + chip[1], my_rows]
            _remote(theirs, theirs, send_sems, recv_sems, 1 + j, (x, y, c)).wait_recv()
        out_ref[my_rows, :] = ((chip_sums[0, my_rows, :] + chip_sums[1, my_rows, :]) + chip_sums[2, my_rows, :]) + chip_sums[3, my_rows, :]
        share = _remote(out_ref.at[my_rows], out_ref.at[my_rows], send_sems, recv_sems, 4, (x, y, 1 - c))
        share.start()
        share.wait_send()
        other = out_ref.at[pl.ds(pl.multiple_of((1 - c) * half, 8), half)]
        _remote(other, other, send_sems, recv_sems, 4, (x, y, c)).wait_recv()

    vmem = pl.BlockSpec(memory_space=pltpu.VMEM)
    return pl.pallas_call(
        body, name="all_reduce_small", in_specs=[vmem], out_specs=vmem, out_shape=jax.ShapeDtypeStruct(part.shape, F32),
        scratch_shapes=[pltpu.VMEM((rows, LANE), F32), pltpu.VMEM((N_SHARDS, rows, LANE), F32),
                        pltpu.SemaphoreType.DMA((5,)), pltpu.SemaphoreType.DMA((5,))],
        compiler_params=pltpu.CompilerParams(vmem_limit_bytes=VMEM_LIMIT_BYTES),
    )(part)


def _sum_partials(g3, others, sel):
    _, rows, c = others.shape
    whole = g3.shape[1] == rows
    tr = _tile(rows, UPDATE_ROWS)
    nb = rows // tr

    def body(sel_ref, g_ref, *rest):
        same = g_ref[...].astype(F32)
        for ref in rest[1:4]:
            same = same + ref[...].astype(F32)
        other = rest[0][...].astype(F32)
        for ref in rest[4:7]:
            other = other + ref[...].astype(F32)
        rest[7][...] = same + other

    blk = (None, tr, c)
    slots = [pl.BlockSpec(blk, functools.partial(lambda i, sr, k: (k, i, 0), k=k)) for k in range(7)]
    return pl.pallas_call(
        body, name="reduce_sum_partials",
        grid_spec=pltpu.PrefetchScalarGridSpec(
            num_scalar_prefetch=1, grid=(nb,),
            in_specs=[pl.BlockSpec(blk, lambda i, sr: (sr[0], (0 if whole else sr[1] * nb) + i, 0))] + slots,
            out_specs=pl.BlockSpec((tr, c), lambda i, sr: (i, 0))),
        out_shape=jax.ShapeDtypeStruct((rows, c), F32),
        compiler_params=_params(("parallel",)),
    )(sel, g3, *[others] * 7)


def _adamw_math(w, g, m, v):
    nm = ADAM_B1 * m + (1.0 - ADAM_B1) * g
    nv = ADAM_B2 * v + (1.0 - ADAM_B2) * (g * g)
    m_hat = nm / (1.0 - ADAM_B1 ** ADAM_STEP)
    v_hat = nv / (1.0 - ADAM_B2 ** ADAM_STEP)
    return -ADAM_LR * (m_hat / (jnp.sqrt(v_hat) + ADAM_EPS) + ADAM_WD * w), nm, nv


def _adamw_layer(layer, w, m, v, g_mine, g_sibling, sel, prev):
    lyr, r, c = w.shape
    rows = g_mine.shape[0]
    halves = r // rows
    tr = _tile(rows, 512)
    nb = rows // tr
    n_g = 1 if g_sibling is None else 2

    def body(sel_ref, w_ref, m_ref, v_ref, *rest):
        g = rest[0][...]
        if n_g == 2:
            g = jnp.where(pl.program_id(0) == sel_ref[1], g, rest[1][...])
        outs = rest[n_g + (0 if prev is None else 4):]
        d, nm, nv = _adamw_math(w_ref[...], g, m_ref[...], v_ref[...])
        for ref, val in zip(outs, (g, d, nm, nv)):
            ref[...] = val

    full = pl.BlockSpec((None, tr, c), lambda h, i, sr: (layer, h * nb + i, 0))
    part = pl.BlockSpec((tr, c), lambda h, i, sr: (i, 0))
    n_in = 4 + n_g
    return pl.pallas_call(
        body, name="adamw_layer",
        grid_spec=pltpu.PrefetchScalarGridSpec(
            num_scalar_prefetch=1, grid=(halves, nb),
            in_specs=[full] * 3 + [part] * n_g + ([] if prev is None else [pl.BlockSpec(memory_space=pl.ANY)] * 4),
            out_specs=[full] * 4),
        out_shape=[jax.ShapeDtypeStruct(w.shape, F32)] * 4,
        input_output_aliases={} if prev is None else {n_in + k: k for k in range(4)},
        compiler_params=_params(("parallel", "parallel")),
    )(sel, w, m, v, g_mine, *([] if g_sibling is None else [g_sibling]), *([] if prev is None else prev))


def _adamw(w, g, m, v):
    lyr, r, c = w.shape
    tr = _tile(r, 256)

    def body(w_ref, g_ref, m_ref, v_ref, d_ref, nm_ref, nv_ref):
        d_ref[...], nm_ref[...], nv_ref[...] = _adamw_math(w_ref[...], g_ref[...], m_ref[...], v_ref[...])

    blk = pl.BlockSpec((None, tr, c), lambda l, i: (l, i, 0))
    return pl.pallas_call(
        body, name="adamw", grid=(lyr, r // tr), in_specs=[blk] * 4, out_specs=[blk] * 3,
        out_shape=[jax.ShapeDtypeStruct(w.shape, F32)] * 3,
        compiler_params=_params(("parallel", "parallel")),
    )(w, g, m, v)


SHARDED = ("mla_w_dkv", "mla_w_uq", "mla_w_ukv", "mla_w_o", "sgu_w_in", "sgu_ln_g", "sgu_ln_b", "sgu_w_out",
           "ffn_w_up", "ffn_w_down")
REPLICATED = ("norm_mix", "norm_ffn", "final_norm", "mla_q_norm", "mla_kv_norm", "sgu_w_spatial", "sgu_b_spatial")
WEIGHTS = ("norm_mix", "norm_ffn", "final_norm", "mla_w_dkv", "mla_q_norm", "mla_kv_norm", "mla_w_uq", "mla_w_ukv",
           "mla_w_o", "sgu_w_in", "sgu_ln_g", "sgu_ln_b", "sgu_w_spatial", "sgu_b_spatial", "sgu_w_out", "ffn_w_up",
           "ffn_w_down")


class _Reducer:
    def __init__(self, state, sel):
        self.state, self.sel = state, sel
        self.started, self.travelling, self.summed = [], [], []
        self.done = {}

    def add(self, tag, layer, grads, token):
        names = list(grads)
        token, *tied = lax.optimization_barrier((token, *[a for n in names for a in grads[n]]))
        f32s, bf16s = tied[0::2], tied[1::2]
        *flying, token = _exchange_start(tag, bf16s, token)
        self.started.append((tag, layer, names, f32s, flying))
        return token

    def phase_end(self, token):
        for tag, layer, names, f32s, flying in self.travelling:
            bf16s, received = _exchange_wait(tag, *flying, token)
            own = [g if g.shape[1] >= MIN_SPLIT_ROWS else gb for g, gb in zip(f32s, bf16s)]
            mine = [_sum_partials(g, got, self.sel) for g, got in zip(own, received)]
            token, *mine = lax.optimization_barrier((token, *mine))
            cut = [k for k, g in enumerate(own) if g.shape[1] >= MIN_SPLIT_ROWS]
            theirs = dict(zip(cut, _share_halves(tag, [mine[k] for k in cut])))
            self.summed.append((layer, names, mine, [theirs.get(k) for k in range(len(names))]))
        self.travelling, self.started = self.started, []
        return token

    def update(self, token):
        for layer, names, mine, theirs in self.summed:
            for name, g_mine, g_theirs in zip(names, mine, theirs):
                w, m, v = self.state[name]
                self.done[name] = _adamw_layer(layer, w, m, v, g_mine, g_theirs, self.sel, self.done.get(name))
                token = self.done[name][1]
        self.summed = []
        return token


def _as3d(name, a):
    return a.reshape(a.shape[0], -1, LANE) if name in ("sgu_ln_g", "sgu_ln_b") else a


def _pack(parts):
    flat = jnp.concatenate([p.reshape(-1) for p in parts])
    rows = -(-flat.shape[0] // (256 * LANE)) * 256
    return jnp.pad(flat, (0, rows * LANE - flat.shape[0])).reshape(rows, LANE)


def _unpack(packed, like):
    flat, out, at = packed.reshape(-1), [], 0
    for p in like:
        out.append(flat[at:at + p.size].reshape(p.shape))
        at += p.size
    return out


def kernel(x, positions, norm_mix, norm_ffn, final_norm, mla_w_dkv, mla_q_norm, mla_kv_norm, mla_w_uq, mla_w_ukv, mla_w_o, sgu_w_in, sgu_ln_g, sgu_ln_b, sgu_w_spatial, sgu_b_spatial, sgu_w_out, ffn_w_up, ffn_w_down, loss_target, m_norm_mix, m_norm_ffn, m_final_norm, m_mla_w_dkv, m_mla_q_norm, m_mla_kv_norm, m_mla_w_uq, m_mla_w_ukv, m_mla_w_o, m_sgu_w_in, m_sgu_ln_g, m_sgu_ln_b, m_sgu_w_spatial, m_sgu_b_spatial, m_sgu_w_out, m_ffn_w_up, m_ffn_w_down, v_norm_mix, v_norm_ffn, v_final_norm, v_mla_w_dkv, v_mla_q_norm, v_mla_kv_norm, v_mla_w_uq, v_mla_w_ukv, v_mla_w_o, v_sgu_w_in, v_sgu_ln_g, v_sgu_ln_b, v_sgu_w_spatial, v_sgu_b_spatial, v_sgu_w_out, v_ffn_w_up, v_ffn_w_down):
    given = dict(locals())
    w = {n: given[n] for n in WEIGHTS}
    mom = {n: given["m_" + n] for n in WEIGHTS}
    var = {n: given["v_" + n] for n in WEIGHTS}
    mixers, ffn, token = [], [], None
    for i in range(DEPTH):
        j = i // 2
        if i % 2 == 0:
            mixer = [w[n][j].astype(BF16) for n in ("mla_w_dkv", "mla_w_uq", "mla_w_ukv", "mla_w_o")]
        else:
            mixer = [sgu_w_in[j].astype(BF16), sgu_w_out[j].astype(BF16), sgu_ln_g[j].reshape(-1, LANE),
                     sgu_ln_b[j].reshape(-1, LANE)]
        for tag, shards, into in ((f"mixer{i}", mixer, mixers), (f"ffn{i}", [ffn_w_up[i].astype(BF16), ffn_w_down[i].astype(BF16)], ffn)):
            if token is None:
                token = shards[0]
            else:
                token, *shards = lax.optimization_barrier((token, *shards))
            into.append(_gather_layer(tag, shards))

    x_i, y_i, c_i = lax.axis_index("x"), lax.axis_index("y"), lax.axis_index("c")
    sel = jnp.stack([2 * x_i + y_i, c_i]).astype(jnp.int32)
    reducer = _Reducer({n: tuple(_as3d(n, d[n]) for d in (w, mom, var)) for n in SHARDED}, sel)
    loss, dx, small = _local_step(
        x[0], positions[0], loss_target[0], norm_mix, norm_ffn, final_norm, mla_q_norm, mla_kv_norm, sgu_w_spatial,
        sgu_b_spatial, mixers, ffn, reducer)
    loss = lax.psum(loss, ("x", "y", "c"))

    small_g = [small["norm_mix"], small["norm_ffn"], small["final_norm"], small["q_norm"], small["kv_norm"],
               small["w_sp"], small["b_sp"]]
    like = [w[n] for n in REPLICATED]
    g_small = _all_reduce_small(_pack(small_g))
    packed = [_pack([d[n] for n in REPLICATED])[None] for d in (w, mom, var)]
    upd_small = _adamw(packed[0], g_small[None], packed[1], packed[2])
    grads = dict(zip(REPLICATED, _unpack(g_small, like)))
    delta, new_m, new_v = ({n: a for n, a in zip(REPLICATED, _unpack(u[0], like))} for u in upd_small)

    reducer.phase_end(reducer.update(upd_small[0]))
    reducer.update(None)
    for n in SHARDED:
        grads[n], delta[n], new_m[n], new_v[n] = (a.reshape(w[n].shape) for a in reducer.done[n])

    return (loss, dx[None], *[grads[n] for n in WEIGHTS], *[delta[n] for n in WEIGHTS],
            *[new_m[n] for n in WEIGHTS], *[new_v[n] for n in WEIGHTS])
```

```python
import functools
import math

import jax
import jax.numpy as jnp
from jax import lax
from jax.experimental import pallas as pl
from jax.experimental.pallas import tpu as pltpu
from jax.experimental.pallas import tpu_sc as plsc

F32 = jnp.float32
BF16 = jnp.bfloat16
MESH = pl.DeviceIdType.MESH

DEPTH = 4
HEADS = 8
NOPE = 128
ROPE = 64
VHEAD = 128
QK_HEAD = NOPE + ROPE
Q_RANK = 256
KV_RANK = 128
HEAD_PAD = 256
LAT_PAD = 512
ROPE_THETA = 10000.0
SGU_CHUNK = 128
SGU_GROUPS = 8
NORM_EPS = 1e-6
LN_EPS = 1e-5
ADAM_LR, ADAM_B1, ADAM_B2, ADAM_EPS, ADAM_WD, ADAM_STEP = 0.001, 0.9, 0.999, 1e-08, 0.01, 10

N_SHARDS = 4
LANE = 128
VMEM_LIMIT_BYTES = 56 * 1024 * 1024
ATT_TILE = 512
MM_TILE = 1024
UPDATE_ROWS = 128
ATT_SCALE = QK_HEAD ** -0.5
LOG2_SCALE = ATT_SCALE * math.log2(math.e)

NN = (((1,), (0,)), ((), ()))
NT = (((1,), (1,)), ((), ()))
TN = (((0,), (0,)), ((), ()))


def _params(sem):
    return pltpu.CompilerParams(dimension_semantics=sem, vmem_limit_bytes=VMEM_LIMIT_BYTES)


def _tile(n, pref):
    t = min(n, pref)
    while n % t:
        t //= 2
    return t


def _matmul(name, a, b, a_spec, b_spec, dims, grid, tile, outs, extras=(), epilogue=None, sums=()):
    nk, ne, no = grid[2], len(extras), len(outs)
    b_specs = list(b_spec) if isinstance(b_spec, (list, tuple)) else [b_spec]
    nb = len(b_specs)

    def body(a_ref, *rest):
        b_refs, e_refs, o_refs = rest[:nb], rest[nb:nb + ne], rest[nb + ne:nb + ne + no]
        s_refs = rest[nb + ne + no:nb + ne + no + len(sums)]
        kw = a_ref.shape[1] // nb
        part = None
        for p, b_ref in enumerate(b_refs):
            a_tile = a_ref[...] if nb == 1 else a_ref[:, p * kw:(p + 1) * kw]
            d = lax.dot_general(a_tile.astype(BF16), b_ref[...].astype(BF16), dims, preferred_element_type=F32)
            part = d if part is None else part + d

        def finish(acc):
            vals = (acc,) if epilogue is None else epilogue(acc, *[e[...] for e in e_refs])
            for o_ref, v in zip(o_refs, vals):
                o_ref[...] = v.astype(o_ref.dtype)
            first = pl.program_id(0) == 0
            for s_ref, v in zip(s_refs, vals[no:]):
                @pl.when(first)
                def _():
                    s_ref[...] = v

                @pl.when(jnp.logical_not(first))
                def _():
                    s_ref[...] += v

        if nk == 1:
            finish(part)
            return
        acc_ref, k = rest[-1], pl.program_id(2)

        @pl.when(k == 0)
        def _():
            acc_ref[...] = part

        @pl.when(jnp.logical_and(k > 0, k < nk - 1))
        def _():
            acc_ref[...] += part

        @pl.when(k == nk - 1)
        def _():
            finish(acc_ref[...] + part)

    assert not sums or (grid[1] == 1 and nk == 1)
    return pl.pallas_call(
        body, name=name, grid=grid,
        in_specs=[a_spec] + b_specs + [s for _, s in extras],
        out_specs=[s for _, s in outs] + [pl.BlockSpec(s, lambda i, j, k: (0,) * len(s)) for s in sums],
        out_shape=[s for s, _ in outs] + [jax.ShapeDtypeStruct(s, F32) for s in sums],
        scratch_shapes=[pltpu.VMEM(tile, F32)] if nk > 1 else [],
        compiler_params=_params(("arbitrary" if sums else "parallel", "parallel", "arbitrary")),
    )(a, *[b] * nb, *[e for e, _ in extras])


def _epilogue_operands(extras, consts, o_spec):
    return [(e, o_spec) for e in extras] + [(c, pl.BlockSpec(c.shape, lambda i, j, k: (0, 0))) for c in consts]


def _mm(name, a, b, out_dtypes=(F32,), epilogue=None, extras=(), tm=MM_TILE, tn=MM_TILE, tk=MM_TILE, nt=False,
        consts=(), sums=(), narrow=()):
    m, kd = a.shape
    n = b.shape[0] if nt else b.shape[1]
    tm, tn, tk = _tile(m, tm), _tile(n, tn), _tile(kd, tk)
    o_spec = pl.BlockSpec((tm, tn), lambda i, j, k: (i, j))
    b_spec = pl.BlockSpec((tn, tk), lambda i, j, k: (j, k)) if nt else pl.BlockSpec((tk, tn), lambda i, j, k: (k, j))
    assert not narrow or n == tn
    outs = [(jax.ShapeDtypeStruct((m, n), d), o_spec) for d in out_dtypes]
    outs += [(jax.ShapeDtypeStruct((m, w), d), pl.BlockSpec((tm, w), lambda i, j, k: (i, 0))) for w, d in narrow]
    return _matmul(name, a, b, pl.BlockSpec((tm, tk), lambda i, j, k: (i, k)), b_spec, NT if nt else NN,
                   (m // tm, n // tn, kd // tk), (tm, tn), outs,
                   _epilogue_operands(extras, consts, o_spec), epilogue, sums)


def _mm_tn(name, a, b, out_dtypes=(F32,), tm=MM_TILE, tn=MM_TILE, tk=MM_TILE):
    s, m = a.shape
    n = b.shape[1]
    tm, tn, tk = _tile(m, tm), _tile(n, tn), _tile(s, tk)
    o_spec = pl.BlockSpec((tm, tn), lambda i, j, k: (i, j))
    return _matmul(name, a, b, pl.BlockSpec((tk, tm), lambda i, j, k: (k, i)),
                   pl.BlockSpec((tk, tn), lambda i, j, k: (k, j)), TN, (m // tm, n // tn, s // tk), (tm, tn),
                   [(jax.ShapeDtypeStruct((m, n), d), o_spec) for d in out_dtypes])


def _mm_stacked(name, a, w3, mode, out_dtypes=(F32,), epilogue=None, extras=(), tm=MM_TILE, tn=MM_TILE, tk=MM_TILE,
                consts=()):
    m, kd = a.shape
    _, r, c = w3.shape
    n = c if mode == "row" else N_SHARDS * c
    if mode == "row":
        tm, tn, tk = _tile(m, tm // 2), _tile(n, tn), kd
        b_spec = [pl.BlockSpec((None, r, tn), functools.partial(lambda i, j, k, p: (p, 0, j), p=p)) for p in range(N_SHARDS)]
    else:
        tm, tn, tk = _tile(m, tm), _tile(c, tn), _tile(kd, tk)
        per = c // tn
        b_spec = pl.BlockSpec((None, tk, tn), lambda i, j, k: (j // per, k, j % per))
    o_spec = pl.BlockSpec((tm, tn), lambda i, j, k: (i, j))
    return _matmul(name, a, w3, pl.BlockSpec((tm, tk), lambda i, j, k: (i, k)), b_spec, NN,
                   (m // tm, n // tn, kd // tk), (tm, tn),
                   [(jax.ShapeDtypeStruct((m, n), d), o_spec) for d in out_dtypes],
                   _epilogue_operands(extras, consts, o_spec), epilogue)


def _mm_stacked_nt(name, a, w3, mode, out_dtypes=(F32,), epilogue=None, extras=(), tm=MM_TILE, tn=MM_TILE, tk=MM_TILE,
                   consts=(), sums=()):
    m, nd = a.shape
    _, r, c = w3.shape
    kout = N_SHARDS * r if mode == "row" else r
    if mode == "row":
        tm, tn, tk = _tile(m, tm), _tile(r, tn), _tile(c, tk)
        per = r // tn
        b_spec = pl.BlockSpec((None, tn, tk), lambda i, j, k: (j // per, j % per, k))
    else:
        tm, tn, tk = _tile(m, tm // 2), _tile(r, tn), nd
        b_spec = [pl.BlockSpec((None, tn, c), functools.partial(lambda i, j, k, p: (p, j, 0), p=p)) for p in range(N_SHARDS)]
    o_spec = pl.BlockSpec((tm, tn), lambda i, j, k: (i, j))
    return _matmul(name, a, w3, pl.BlockSpec((tm, tk), lambda i, j, k: (i, k)), b_spec, NT,
                   (m // tm, kout // tn, nd // tk), (tm, tn),
                   [(jax.ShapeDtypeStruct((m, kout), d), o_spec) for d in out_dtypes],
                   _epilogue_operands(extras, consts, o_spec), epilogue, sums)


def _mm_tn_stacked(name, a, b, shape3, mode, tm=MM_TILE, tn=MM_TILE, tk=MM_TILE):
    s, m = a.shape
    n = b.shape[1]
    _, r, c = shape3
    tk, tn = s, tn // 2
    if mode == "row":
        tm, tn = _tile(r, tm), _tile(n, tn)
        per = r // tm
        o_spec = pl.BlockSpec((None, tm, tn), lambda i, j, k: (i // per, i % per, j))
    else:
        tm, tn = _tile(m, tm), _tile(c, tn)
        per = c // tn
        o_spec = pl.BlockSpec((None, tm, tn), lambda i, j, k: (j // per, i, j % per))
    outs = [(jax.ShapeDtypeStruct(shape3, F32), o_spec), (jax.ShapeDtypeStruct(shape3, BF16), o_spec)]
    return _matmul(name, a, b, pl.BlockSpec((tk, tm), lambda i, j, k: (k, i)),
                   pl.BlockSpec((tk, tn), lambda i, j, k: (k, j)), TN, (m // tm, n // tn, s // tk), (tm, tn),
                   outs, epilogue=lambda acc: (acc, acc))


def _rowwise(name, fn, rows, consts, out_rows, out_accs=(), tr=256):
    nr, nc, no = len(rows), len(consts), len(out_rows)
    n_rows = rows[0].shape[0]
    tr = _tile(n_rows, tr)

    def body(*refs):
        vals = fn(*[r[...] for r in refs[:nr + nc]])
        o_refs, a_refs = refs[nr + nc:nr + nc + no], refs[nr + nc + no:]
        for ref, v in zip(o_refs, vals[:no]):
            ref[...] = v.astype(ref.dtype)
        first = pl.program_id(0) == 0

        @pl.when(first)
        def _():
            for ref, v in zip(a_refs, vals[no:]):
                ref[...] = v

        @pl.when(jnp.logical_not(first))
        def _():
            for ref, v in zip(a_refs, vals[no:]):
                ref[...] += v

    def whole(shape):
        return pl.BlockSpec(shape, lambda i: (0,) * len(shape))

    return pl.pallas_call(
        body, name=name, grid=(n_rows // tr,),
        in_specs=[pl.BlockSpec((tr, a.shape[1]), lambda i: (i, 0)) for a in rows] + [whole(c.shape) for c in consts],
        out_specs=[pl.BlockSpec((tr, f), lambda i: (i, 0)) for f, _ in out_rows] + [whole(s) for s in out_accs],
        out_shape=[jax.ShapeDtypeStruct((n_rows, f), d) for f, d in out_rows]
        + [jax.ShapeDtypeStruct(s, F32) for s in out_accs],
        compiler_params=_params(("arbitrary",)),
    )(*rows, *consts)


def _rms_fwd(x, g):
    return x * lax.rsqrt(jnp.mean(x * x, axis=-1, keepdims=True) + NORM_EPS) * g


def _rms_bwd(dy, x, g):
    rstd = lax.rsqrt(jnp.mean(x * x, axis=-1, keepdims=True) + NORM_EPS)
    n = x * rstd
    dn = dy * g
    dx = rstd * (dn - n * jnp.mean(dn * n, axis=-1, keepdims=True))
    return dx, jnp.sum(dy * n, axis=0, keepdims=True)


def _rope(x, cs, s1, s2):
    return x * cs + pltpu.roll(x, 32, 1) * s1 + pltpu.roll(x, 96, 1) * s2


def _rope_t(dy, cs, s1, s2):
    return dy * cs + pltpu.roll(dy * s1, 96, 1) + pltpu.roll(dy * s2, 32, 1)


def _gelu(z):
    return 0.5 * z * (1.0 + lax.erf(z * (1.0 / math.sqrt(2.0))))


def _gelu_and_grad(z):
    cdf = 0.5 * (1.0 + lax.erf(z * (1.0 / math.sqrt(2.0))))
    return z * cdf, cdf + z * jnp.exp(-0.5 * z * z) * (1.0 / math.sqrt(2.0 * math.pi))


def _att_scores(q, kv, kr, masked, transposed):
    k = jnp.concatenate([kv[:, :NOPE], kr], axis=1)
    if transposed:
        s = lax.dot_general(k, q, NT, preferred_element_type=F32)
    else:
        s = lax.dot_general(q, k, NT, preferred_element_type=F32)
    if masked:
        r = lax.broadcasted_iota(jnp.int32, s.shape, 0)
        c = lax.broadcasted_iota(jnp.int32, s.shape, 1)
        s = jnp.where((r <= c) if transposed else (c <= r), s, -jnp.inf)
    return s, k


def _in_pairs(lo, hi, pair, single):
    n = hi - lo

    def body(p, carry):
        pair(lo + 2 * p, lo + 2 * p + 1)
        return carry

    lax.fori_loop(0, n // 2, body, 0)

    @pl.when(n % 2 == 1)
    def _():
        single(hi - 1)


def _causal_tiles(i, pair, single):
    @pl.when(i == 0)
    def _():
        single(i, True)

    @pl.when(i > 0)
    def _():
        _in_pairs(0, i - 1, lambda a, b: pair(a, b, False), lambda a: single(a, False))
        pair(i - 1, i, True)


def _flash_fwd(q, kvb, krb, tables):
    s_len = q.shape[0]
    t = ATT_TILE

    def body(q_ref, cs_ref, s1_ref, s2_ref, kv_ref, kr_ref, o_ref, lse_ref, qb_ref, m_s, l_s, acc_s):
        qi = pl.program_id(1)
        m_s[...] = jnp.full_like(m_s, -jnp.inf)
        l_s[...] = jnp.zeros_like(l_s)
        acc_s[...] = jnp.zeros_like(acc_s)
        qv = q_ref[...]
        q = jnp.concatenate([qv[:, :NOPE], _rope(qv[:, NOPE:], cs_ref[...], s1_ref[...], s2_ref[...])], axis=1)
        q = (q * LOG2_SCALE).astype(BF16)
        qb_ref[...] = q

        def scores(ki, masked):
            rows = pl.ds(pl.multiple_of(ki * t, t), t)
            kv = kv_ref[rows, :]
            return _att_scores(q, kv, kr_ref[rows, :], masked, False)[0], kv

        def update(s, kv):
            m_prev = m_s[...]
            m_new = jnp.maximum(m_prev, jnp.max(s, axis=1, keepdims=True))
            alpha = jnp.exp2(m_prev - m_new)
            p = jnp.exp2(s - jnp.tile(m_new, (1, t // LANE)))
            l_s[...] = alpha * l_s[...] + jnp.sum(p, axis=1, keepdims=True)
            acc_s[...] = alpha * acc_s[...] + jnp.dot(p.astype(BF16), kv[:, NOPE:], preferred_element_type=F32)
            m_s[...] = m_new

        def pair(k0, k1, masked):
            first, second = scores(k0, False), scores(k1, masked)
            update(*first)
            update(*second)

        _causal_tiles(qi, pair, lambda ki, masked: update(*scores(ki, masked)))
        o_ref[...] = (acc_s[...] / l_s[...]).astype(o_ref.dtype)
        lse_ref[...] = (m_s[...] + jnp.log2(l_s[...])).T[:1, :]

    table = pl.BlockSpec((t, LANE), lambda h, qi: (qi, 0))
    return pl.pallas_call(
        body, name="flash_fwd", grid=(HEADS, s_len // t),
        in_specs=[pl.BlockSpec((t, HEAD_PAD), lambda h, qi: (qi, h)), table, table, table,
                  pl.BlockSpec((s_len, HEAD_PAD), lambda h, qi: (0, h)),
                  pl.BlockSpec((s_len, LANE), lambda h, qi: (0, 0))],
        out_specs=[pl.BlockSpec((t, VHEAD), lambda h, qi: (qi, h)),
                   pl.BlockSpec((None, 1, t), lambda h, qi: (h, 0, qi)),
                   pl.BlockSpec((t, HEAD_PAD), lambda h, qi: (qi, h))],
        out_shape=[jax.ShapeDtypeStruct((s_len, HEADS * VHEAD), BF16),
                   jax.ShapeDtypeStruct((HEADS, 1, s_len), F32),
                   jax.ShapeDtypeStruct((s_len, HEADS * HEAD_PAD), BF16)],
        scratch_shapes=[pltpu.VMEM((t, LANE), F32), pltpu.VMEM((t, LANE), F32), pltpu.VMEM((t, VHEAD), F32)],
        compiler_params=_params(("parallel", "arbitrary")),
    )(q, *tables, kvb, krb)


def _flash_bwd(qb, kvb, krb, dob, lse_row, delta_row):
    s_len = qb.shape[0]
    t = ATT_TILE
    nq = s_len // t
    scale = QK_HEAD ** -0.5

    def body(q_ref, kv_ref, kr_ref, do_ref, lse_ref, dl_ref, dq_ref, dkv_ref, dkr_ref, dk_s, dv_s):
        ki = pl.program_id(1)

        @pl.when(ki == 0)
        def _():
            dq_ref[...] = jnp.zeros_like(dq_ref)

        dk_s[...] = jnp.zeros_like(dk_s)
        dv_s[...] = jnp.zeros_like(dv_s)
        kv, kr = kv_ref[...], kr_ref[...]

        def products(qi, masked):
            rows = pl.ds(pl.multiple_of(qi * t, t), t)
            q, do = q_ref[rows, :], do_ref[rows, :]
            st, k = _att_scores(q, kv, kr, masked, True)
            return st, lax.dot_general(kv[:, NOPE:], do, NT, preferred_element_type=F32), q, do, rows, k

        def update(st, dpt, q, do, rows, k):
            pt = jnp.exp2(st - lse_ref[:, rows])
            dv_s[...] += jnp.dot(pt.astype(BF16), do, preferred_element_type=F32)
            dst = (pt * (dpt - dl_ref[:, rows]) * scale).astype(BF16)
            dk_s[...] += jnp.dot(dst, q, preferred_element_type=F32)
            dq_ref[rows, :] += lax.dot_general(dst, k, TN, preferred_element_type=F32)

        def pair(q0, q1, masked):
            first, second = products(q0, masked), products(q1, False)
            update(*first)
            update(*second)

        @pl.when(ki == nq - 1)
        def _():
            update(*products(ki, True))

        @pl.when(ki < nq - 1)
        def _():
            pair(ki, ki + 1, True)
            _in_pairs(ki + 2, nq, lambda a, b: pair(a, b, False), lambda qi: update(*products(qi, False)))

        dk = dk_s[...] * (1.0 / LOG2_SCALE)
        dkv_ref[...] = jnp.concatenate([dk[:, :NOPE], dv_s[...]], axis=1).astype(dkv_ref.dtype)
        dkr_ref[...] = dk[:, NOPE:]

    row = pl.BlockSpec((None, 1, s_len), lambda h, ki: (h, 0, 0))
    return pl.pallas_call(
        body, name="flash_bwd", grid=(HEADS, nq),
        in_specs=[pl.BlockSpec((s_len, HEAD_PAD), lambda h, ki: (0, h)),
                  pl.BlockSpec((t, HEAD_PAD), lambda h, ki: (ki, h)),
                  pl.BlockSpec((t, LANE), lambda h, ki: (ki, 0)),
                  pl.BlockSpec((s_len, VHEAD), lambda h, ki: (0, h)), row, row],
        out_specs=[pl.BlockSpec((s_len, HEAD_PAD), lambda h, ki: (0, h)),
                   pl.BlockSpec((t, HEAD_PAD), lambda h, ki: (ki, h)),
                   pl.BlockSpec((t, LANE), lambda h, ki: (ki, h))],
        out_shape=[jax.ShapeDtypeStruct((s_len, HEADS * HEAD_PAD), F32),
                   jax.ShapeDtypeStruct((s_len, HEADS * HEAD_PAD), BF16),
                   jax.ShapeDtypeStruct((s_len, HEADS * LANE), F32)],
        scratch_shapes=[pltpu.VMEM((t, HEAD_PAD), F32), pltpu.VMEM((t, VHEAD), F32)],
        compiler_params=_params(("parallel", "arbitrary")),
    )(qb, kvb, krb, dob, lse_row, delta_row)


def _tril(w):
    r = lax.broadcasted_iota(jnp.int32, w.shape, 0)
    c = lax.broadcasted_iota(jnp.int32, w.shape, 1)
    return jnp.where(c <= r, w, 0.0)


def _sgu_row_stats(z_ref, width, gd, v_s, act, extra_s=None):
    total = None
    for g in range(SGU_GROUPS):
        cols = slice(g * gd, (g + 1) * gd)
        v = act(z_ref[:, width + g * gd:width + (g + 1) * gd])
        if extra_s is not None:
            v, extra_s[:, cols] = v
        v_s[:, cols] = v
        part = jnp.sum(v, axis=1, keepdims=True)
        total = part if total is None else total + part
    mean = total * (1.0 / width)
    sq = None
    for g in range(SGU_GROUPS):
        d = v_s[:, g * gd:(g + 1) * gd] - mean
        part = jnp.sum(d * d, axis=1, keepdims=True)
        sq = part if sq is None else sq + part
    return mean, lax.rsqrt(sq * (1.0 / width) + LN_EPS)


def _sgu_fwd(zpre, ln_g, ln_b, w_sp, bias_full):
    s_len, two_w = zpre.shape
    width = two_w // 2
    gd = width // SGU_GROUPS
    t = SGU_CHUNK

    def body(z_ref, g_ref, b_ref, w_ref, bias_ref, uv_ref, v_s):
        mean, rstd = _sgu_row_stats(z_ref, width, gd, v_s, _gelu)
        for g in range(SGU_GROUPS):
            cols = slice(g * gd, (g + 1) * gd)
            vln = ((v_s[:, cols] - mean) * rstd * g_ref[:, cols] + b_ref[:, cols]).astype(BF16)
            mixed = jnp.dot(_tril(w_ref[g]).astype(BF16), vln, preferred_element_type=F32) + bias_ref[:, cols]
            uv_ref[:, cols] = (_gelu(z_ref[:, cols]) * mixed).astype(uv_ref.dtype)

    return pl.pallas_call(
        body, name="sgu_fwd", grid=(s_len // t,),
        in_specs=[pl.BlockSpec((t, two_w), lambda i: (i, 0)), pl.BlockSpec((1, width), lambda i: (0, 0)),
                  pl.BlockSpec((1, width), lambda i: (0, 0)), pl.BlockSpec(w_sp.shape, lambda i: (0, 0, 0)),
                  pl.BlockSpec((t, width), lambda i: (0, 0))],
        out_specs=pl.BlockSpec((t, width), lambda i: (i, 0)),
        out_shape=jax.ShapeDtypeStruct((s_len, width), BF16),
        scratch_shapes=[pltpu.VMEM((t, width), F32)],
        compiler_params=_params(("parallel",)),
    )(zpre, ln_g, ln_b, w_sp, bias_full)


def _sgu_bwd(zpre, duv, ln_g, ln_b, w_sp, bias_full):
    s_len, two_w = zpre.shape
    width = two_w // 2
    gd = width // SGU_GROUPS
    t = SGU_CHUNK

    def body(z_ref, duv_ref, g_ref, b_ref, w_ref, bias_ref, dz_ref, dg_ref, db_ref, dw_ref, dbias_ref, v_s, vgrad_s, dvhat_s):
        @pl.when(pl.program_id(0) == 0)
        def _():
            for ref in (dg_ref, db_ref, dw_ref, dbias_ref):
                ref[...] = jnp.zeros_like(ref)

        def accumulate(ref, val):
            ref[...] += val

        mean, rstd = _sgu_row_stats(z_ref, width, gd, v_s, _gelu_and_grad, vgrad_s)
        sum_dvhat = sum_dvhat_vhat = None
        for g in range(SGU_GROUPS):
            cols = slice(g * gd, (g + 1) * gd)
            vhat = (v_s[:, cols] - mean) * rstd
            vln = (vhat * g_ref[:, cols] + b_ref[:, cols]).astype(BF16)
            wc = _tril(w_ref[g]).astype(BF16)
            mixed = jnp.dot(wc, vln, preferred_element_type=F32) + bias_ref[:, cols]
            u, u_grad = _gelu_and_grad(z_ref[:, cols])
            duv = duv_ref[:, cols]
            dz_ref[:, cols] = (duv * mixed * u_grad).astype(dz_ref.dtype)
            dmixed = duv * u
            dmb = dmixed.astype(BF16)
            dvln = lax.dot_general(wc, dmb, TN, preferred_element_type=F32)
            accumulate(dw_ref.at[g], _tril(lax.dot_general(dmb, vln, NT, preferred_element_type=F32)))
            accumulate(dbias_ref.at[:, g * LANE:(g + 1) * LANE],
                       jnp.broadcast_to(jnp.sum(dmixed, axis=1, keepdims=True), (t, LANE)))
            accumulate(dg_ref.at[:, cols], jnp.sum(dvln * vhat, axis=0, keepdims=True))
            accumulate(db_ref.at[:, cols], jnp.sum(dvln, axis=0, keepdims=True))
            dvhat = dvln * g_ref[:, cols]
            dvhat_s[:, cols] = dvhat
            parts = jnp.sum(dvhat, axis=1, keepdims=True), jnp.sum(dvhat * vhat, axis=1, keepdims=True)
            sum_dvhat = parts[0] if sum_dvhat is None else sum_dvhat + parts[0]
            sum_dvhat_vhat = parts[1] if sum_dvhat_vhat is None else sum_dvhat_vhat + parts[1]
        mean_dvhat, mean_dvhat_vhat = sum_dvhat * (1.0 / width), sum_dvhat_vhat * (1.0 / width)
        for g in range(SGU_GROUPS):
            cols = slice(g * gd, (g + 1) * gd)
            vhat = (v_s[:, cols] - mean) * rstd
            dv0 = rstd * (dvhat_s[:, cols] - mean_dvhat - vhat * mean_dvhat_vhat)
            dz_ref[:, width + g * gd:width + (g + 1) * gd] = (dv0 * vgrad_s[:, cols]).astype(dz_ref.dtype)

    vec = pl.BlockSpec((1, width), lambda i: (0, 0))
    return pl.pallas_call(
        body, name="sgu_bwd", grid=(s_len // t,),
        in_specs=[pl.BlockSpec((t, two_w), lambda i: (i, 0)), pl.BlockSpec((t, width), lambda i: (i, 0)), vec, vec,
                  pl.BlockSpec(w_sp.shape, lambda i: (0, 0, 0)), pl.BlockSpec((t, width), lambda i: (0, 0))],
        out_specs=[pl.BlockSpec((t, two_w), lambda i: (i, 0)), vec, vec,
                   pl.BlockSpec(w_sp.shape, lambda i: (0, 0, 0)), pl.BlockSpec((t, SGU_GROUPS * LANE), lambda i: (0, 0))],
        out_shape=[jax.ShapeDtypeStruct((s_len, two_w), BF16), jax.ShapeDtypeStruct((1, width), F32),
                   jax.ShapeDtypeStruct((1, width), F32), jax.ShapeDtypeStruct(w_sp.shape, F32),
                   jax.ShapeDtypeStruct((t, SGU_GROUPS * LANE), F32)],
        scratch_shapes=[pltpu.VMEM((t, width), F32)] * 3,
        compiler_params=_params(("arbitrary",)),
    )(zpre, duv, ln_g, ln_b, w_sp, bias_full)


def _rope_tables(positions):
    inv_freq = ROPE_THETA ** (-jnp.arange(0, ROPE, 2, dtype=F32) / ROPE)
    ang = positions.astype(F32)[:, None] * inv_freq
    cos, sin = jnp.cos(ang), jnp.sin(ang)
    z32, z64 = jnp.zeros_like(cos), jnp.zeros((cos.shape[0], LANE - ROPE), F32)
    return (jnp.concatenate([cos, cos, z64], axis=1), jnp.concatenate([z32, sin, z64], axis=1),
            jnp.concatenate([-sin, z32, z64], axis=1))


def _residual(g_next):
    if g_next is None:
        return dict(out_dtypes=(F32,), epilogue=lambda acc, res: (acc + res,))

    def epilogue(acc, res, gv):
        x_new = acc + res
        return x_new, _rms_fwd(x_new, gv)

    return dict(out_dtypes=(F32, BF16), epilogue=epilogue, consts=[g_next])


def _ffn_fwd(x, h2, w_up3, w_down3, g_next):
    def sq_relu(acc):
        r = jnp.maximum(acc, 0.0)
        return r * r, 2.0 * r

    r, r_grad = _mm_stacked("ffn_up", h2, w_up3, "col", (BF16, BF16), sq_relu)
    x_out, *h_next = _mm_stacked("ffn_down", r, w_down3, "row", extras=[x], **_residual(g_next))
    return x_out, (x, h2, r, r_grad), (h_next[0] if h_next else None)


def _ffn_bwd(dx, dxb, saved, g, w_up3, w_down3):
    x, h2, r, r_grad = saved
    da = _mm_stacked_nt("ffn_down_dx", dxb, w_down3, "row", (BF16,),
                        lambda acc, rg: (acc * rg.astype(F32),), [r_grad], tm=2 * MM_TILE)[0]
    g_down = _mm_tn_stacked("ffn_down_dw", r, dxb, w_down3.shape, "row")
    dx, dxb, dg = _mm_stacked_nt("ffn_up_dx", da, w_up3, "col", **_norm_bwd(x, g, dx))
    g_up = _mm_tn_stacked("ffn_up_dw", h2, da, w_up3.shape, "col")
    return dx, dxb, dg, g_up, g_down


def _norm_bwd(x, g, dres):
    def epilogue(dh, xv, rv, gv):
        dxv, dg = _rms_bwd(dh, xv, gv)
        return dxv + rv, dxv + rv, dg

    return dict(out_dtypes=(F32, BF16), epilogue=epilogue, extras=[x, dres], consts=[g], sums=[g.shape], tm=MM_TILE // 2)


def _dot(a, b, dims=NN):
    return lax.dot_general(a.astype(BF16), b.astype(BF16), dims, preferred_element_type=F32)


def _mla_fwd(x, g, wdkv, q_norm, kv_norm, wq, wkv, wo, tables, g_next):
    d = x.shape[1]

    def project(xv, cs, s1, s2, gv, wdkv_v, qg, kg, wq_v, wkv_v):
        h = _rms_fwd(xv, gv).astype(BF16)
        lv = _dot(h, wdkv_v)
        cqn = _rms_fwd(lv[:, :Q_RANK], qg).astype(BF16)
        ckvn = _rms_fwd(lv[:, Q_RANK:Q_RANK + KV_RANK], kg).astype(BF16)
        return (h, lv, cqn, ckvn, _rope(lv[:, Q_RANK + KV_RANK:], cs, s1, s2), _dot(cqn, wq_v), _dot(ckvn, wkv_v))

    h, lat, cqn, ckvn, krb, q, kvb = _rowwise(
        "mla_project", project, [x, *tables], [g, wdkv, q_norm, kv_norm, wq, wkv],
        [(d, BF16), (LAT_PAD, F32), (Q_RANK, BF16), (KV_RANK, BF16), (LANE, BF16), (wq.shape[1], F32), (wkv.shape[1], BF16)],
        tr=512)
    ob, lse, qb = _flash_fwd(q, kvb, krb, tables)
    x_mid, h2 = _mm("mla_o", ob, wo, extras=[x], **_residual(g_next))
    return x_mid, (x, h, lat, cqn, ckvn, krb, qb, kvb, ob, lse), h2


def _mla_bwd(dx, dxb, saved, g, wdkv, q_norm, kv_norm, wq, wkv, wo, tables):
    x, h, lat, cqn, ckvn, krb, qb, kvb, ob, lse = saved
    s_len = x.shape[0]
    def with_delta(do, ov):
        prod = do * ov.astype(F32)
        lane = lax.broadcasted_iota(jnp.int32, (do.shape[0], LANE), 1)
        by_lane = None
        for hd in range(HEADS):
            total = jnp.broadcast_to(jnp.sum(prod[:, hd * VHEAD:(hd + 1) * VHEAD], axis=1, keepdims=True), lane.shape)
            by_lane = jnp.where(lane == hd, total, 0.0 if by_lane is None else by_lane)
        return do, by_lane

    dob, delta_lanes = _mm("mla_o_dx", dxb, wo, (BF16,), with_delta, [ob], nt=True, tm=MM_TILE // 2, narrow=[(LANE, F32)])
    g_wo = _mm_tn("mla_o_dw", ob, dxb)[0]
    delta_row = delta_lanes[:, :HEADS].T.reshape(HEADS, 1, s_len)
    dq, dkvb, dkr = _flash_bwd(qb, kvb, krb, dob, lse, delta_row)

    def project_bwd(dq_f, dkv_v, dkr_v, lv, cqn_v, ckvn_v, h_v, xv, rv, cs, s1, s2, gv, qg, kg, wq_v, wkv_v, wdkv_v):
        parts = []
        for hd in range(HEADS):
            parts += [dq_f[:, hd * HEAD_PAD:hd * HEAD_PAD + NOPE],
                      _rope_t(dq_f[:, hd * HEAD_PAD + NOPE:(hd + 1) * HEAD_PAD], cs, s1, s2)]
        dq_v = jnp.concatenate(parts, axis=1).astype(BF16)
        dcq, dqg = _rms_bwd(_dot(dq_v, wq_v, NT), lv[:, :Q_RANK], qg)
        dckv, dkg = _rms_bwd(_dot(dkv_v, wkv_v, NT), lv[:, Q_RANK:Q_RANK + KV_RANK], kg)
        dkr_sum = dkr_v[:, :LANE]
        for hd in range(1, HEADS):
            dkr_sum = dkr_sum + dkr_v[:, hd * LANE:(hd + 1) * LANE]
        dlat = jnp.concatenate([dcq, dckv, _rope_t(dkr_sum, cs, s1, s2)], axis=1).astype(BF16)
        dxv, dg = _rms_bwd(_dot(dlat, wdkv_v, NT), xv, gv)
        return (dxv + rv, dxv + rv, _dot(cqn_v, dq_v, TN), _dot(ckvn_v, dkv_v, TN), _dot(h_v, dlat, TN), dqg, dkg, dg)

    dx, dxb, g_wq, g_wkv, g_wdkv, g_qn, g_kvn, dg = _rowwise(
        "mla_project_bwd", project_bwd, [dq, dkvb, dkr, lat, cqn, ckvn, h, x, dx, *tables],
        [g, q_norm, kv_norm, wq, wkv, wdkv], [(x.shape[1], F32), (x.shape[1], BF16)],
        [wq.shape, wkv.shape, wdkv.shape, q_norm.shape, kv_norm.shape, g.shape], tr=256)
    return dx, dxb, dg, g_wdkv, g_qn, g_kvn, g_wq, g_wkv, g_wo


def _sgu_layer_fwd(x, h, w_in3, ln_g, ln_b, w_sp, bias_full, w_out3, g_next):
    zpre = _mm_stacked("sgu_in", h, w_in3, "col")[0]
    uv = _sgu_fwd(zpre, ln_g, ln_b, w_sp, bias_full)
    x_mid, h2 = _mm_stacked("sgu_out", uv, w_out3, "row", extras=[x], **_residual(g_next))
    return x_mid, (x, h, zpre, uv), h2


def _sgu_layer_bwd(dx, dxb, saved, g, w_in3, ln_g, ln_b, w_sp, bias_full, w_out3):
    x, h, zpre, uv = saved
    duv = _mm_stacked_nt("sgu_out_dx", dxb, w_out3, "row")[0]
    g_out = _mm_tn_stacked("sgu_out_dw", uv, dxb, w_out3.shape, "row")
    dz, g_lng, g_lnb, g_wsp, g_bias = _sgu_bwd(zpre, duv, ln_g, ln_b, w_sp, bias_full)
    g_in = _mm_tn_stacked("sgu_in_dw", h, dz, w_in3.shape, "col")
    dx, dxb, dg = _mm_stacked_nt("sgu_in_dx", dz, w_in3, "col", **_norm_bwd(x, g, dx))
    return dx, dxb, dg, g_in, g_out, g_lng, g_lnb, g_wsp, g_bias


def _loss_head(x, target, g):
    d = x.shape[1]

    def fn(xv, tv, gv):
        err = _rms_fwd(xv, gv) - tv
        dxv, dg = _rms_bwd(err * (1.0 / d), xv, gv)
        return dxv, dxv, dg, jnp.sum(err * err, axis=0, keepdims=True)

    return _rowwise("loss_head", fn, [x, target], [g], [(d, F32), (d, BF16)], [g.shape, g.shape])


def _mixer_weights(i, stacks):
    by_rows = lambda a: a.reshape(N_SHARDS * a.shape[1], a.shape[2])
    by_cols = lambda a: a.transpose(1, 0, 2).reshape(a.shape[1], N_SHARDS * a.shape[2])
    if i % 2:
        w_in3, w_out3, ln_g, ln_b = stacks
        return w_in3, ln_g.reshape(1, -1), ln_b.reshape(1, -1), w_out3
    wdkv = by_rows(stacks[0])
    wdkv = jnp.pad(wdkv, ((0, 0), (0, LAT_PAD - wdkv.shape[1])))
    wq = jnp.pad(by_cols(stacks[1]).reshape(Q_RANK, HEADS, QK_HEAD), ((0, 0), (0, 0), (0, HEAD_PAD - QK_HEAD)))
    return wdkv, wq.reshape(Q_RANK, HEADS * HEAD_PAD), by_cols(stacks[2]), by_rows(stacks[3])


def _local_step(x, positions, target, norm_mix, norm_ffn, final_norm, q_norm, kv_norm, w_sp, b_sp, mixers, ffn, reducer):
    tables = _rope_tables(positions)
    gd = mixers[1][2].size // SGU_GROUPS
    bias_full = [jnp.repeat(b_sp[j].T, gd, axis=1) for j in range(DEPTH // 2)]
    saved, mla, sgu = [], [None] * (DEPTH // 2), [None] * (DEPTH // 2)
    h = None
    for i in range(DEPTH):
        j = i // 2
        x, *stacks = lax.optimization_barrier((x, *mixers[i]))
        if i % 2 == 0:
            wdkv, wq, wkv, wo = mla[j] = _mixer_weights(i, stacks)
            x, s_mix, h2 = _mla_fwd(x, norm_mix[i:i + 1], wdkv, q_norm[j:j + 1], kv_norm[j:j + 1], wq, wkv, wo, tables,
                                    norm_ffn[i:i + 1])
        else:
            w_in3, ln_g, ln_b, w_out3 = sgu[j] = _mixer_weights(i, stacks)
            x, s_mix, h2 = _sgu_layer_fwd(x, h, w_in3, ln_g, ln_b, w_sp[j], bias_full[j], w_out3, norm_ffn[i:i + 1])
        next_is_sgu = i + 1 < DEPTH and (i + 1) % 2 == 1
        x, s_ffn, h = _ffn_fwd(x, h2, *ffn[i], norm_mix[i + 1:i + 2] if next_is_sgu else None)
        saved.append((s_mix, s_ffn))
    dx, dxb, g_final, sq_cols = _loss_head(x, target, final_norm[None, :])
    loss = 0.5 * jnp.sum(sq_cols) / x.shape[1]

    def pair(g):
        return g, g.astype(BF16)

    g_mix, g_ffn = [None] * DEPTH, [None] * DEPTH
    mla_g, sgu_g = [None] * (DEPTH // 2), [None] * (DEPTH // 2)
    for i in reversed(range(DEPTH)):
        j = i // 2
        s_mix, s_ffn = saved[i]
        dx, dxb, g_ffn[i], g_up, g_down = _ffn_bwd(dx, dxb, s_ffn, norm_ffn[i:i + 1], *ffn[i])
        dxb = reducer.add(f"ffn{i}", i, {"ffn_w_up": g_up, "ffn_w_down": g_down}, dxb)
        dxb = reducer.phase_end(dxb)
        if i % 2 == 0:
            wdkv, wq, wkv, wo = mla[j]
            dx, dxb, g_mix[i], g_wdkv, g_qn, g_kvn, g_wq, g_wkv, g_wo = _mla_bwd(
                dx, dxb, s_mix, norm_mix[i:i + 1], wdkv, q_norm[j:j + 1], kv_norm[j:j + 1], wq, wkv, wo, tables)
            mla_g[j] = (g_qn, g_kvn)
            g_wq = g_wq.reshape(Q_RANK, HEADS, HEAD_PAD)[..., :QK_HEAD].reshape(Q_RANK, N_SHARDS, -1)
            dxb = reducer.add(f"mla{j}", j, {
                "mla_w_dkv": pair(g_wdkv[:, :Q_RANK + KV_RANK + ROPE].reshape(N_SHARDS, -1, Q_RANK + KV_RANK + ROPE)),
                "mla_w_uq": pair(g_wq.transpose(1, 0, 2)),
                "mla_w_ukv": pair(g_wkv.reshape(KV_RANK, N_SHARDS, -1).transpose(1, 0, 2)),
                "mla_w_o": pair(g_wo.reshape(N_SHARDS, -1, g_wo.shape[1]))}, dxb)
        else:
            w_in3, ln_g, ln_b, w_out3 = sgu[j]
            dx, dxb, g_mix[i], g_in, g_out, g_lng, g_lnb, g_wsp, g_bias = _sgu_layer_bwd(
                dx, dxb, s_mix, norm_mix[i:i + 1], w_in3, ln_g, ln_b, w_sp[j], bias_full[j], w_out3)
            sgu_g[j] = (g_wsp, g_bias.reshape(SGU_CHUNK, SGU_GROUPS, LANE)[:, :, 0].T)
            dxb = reducer.add(f"sgu{j}", j, {"sgu_w_in": g_in, "sgu_w_out": g_out,
                                             "sgu_ln_g": pair(g_lng.reshape(N_SHARDS, -1, LANE)),
                                             "sgu_ln_b": pair(g_lnb.reshape(N_SHARDS, -1, LANE))}, dxb)
        dxb = reducer.phase_end(dxb)
    small = dict(
        norm_mix=jnp.concatenate(g_mix, axis=0), norm_ffn=jnp.concatenate(g_ffn, axis=0), final_norm=g_final[0],
        q_norm=jnp.concatenate([m[0] for m in mla_g], axis=0), kv_norm=jnp.concatenate([m[1] for m in mla_g], axis=0),
        w_sp=jnp.stack([s[0] for s in sgu_g]), b_sp=jnp.stack([s[1] for s in sgu_g]))
    return loss, dx, small


HBM_SPEC = pl.BlockSpec(memory_space=pltpu.HBM)


def _place():
    x, y, c = lax.axis_index("x"), lax.axis_index("y"), lax.axis_index("c")
    return x, y, c, [(1 - x, y), (x, 1 - y), (1 - x, 1 - y)]


def _remote(src, dst, send_sems, recv_sems, k, to):
    return pltpu.make_async_remote_copy(src_ref=src, dst_ref=dst, send_sem=send_sems.at[k], recv_sem=recv_sems.at[k],
                                        device_id=to, device_id_type=MESH)


def _gather_layer(tag, shards):
    n = len(shards)
    split = [s.shape[0] >= 16 for s in shards]

    def body(*refs):
        ins, outs = refs[:n], refs[n:2 * n]
        send_sems, recv_sems, local_sems = refs[2 * n:]
        x, y, c, chips = _place()
        mine = 2 * x + y
        barrier = pltpu.get_barrier_semaphore()
        peers = [(x, y, 1 - c)] + [(*chip, c) for chip in chips]
        for peer in peers:
            pl.semaphore_signal(barrier, inc=1, device_id=peer, device_id_type=MESH)
        pl.semaphore_wait(barrier, len(peers))

        def rows(t, half):
            hr = shards[t].shape[0] // 2
            return pl.ds(half * hr, hr) if split[t] else pl.ds(0, shards[t].shape[0])

        local, sent = [], []
        for t in range(n):
            local.append(pltpu.make_async_copy(ins[t], outs[t].at[mine], local_sems.at[t]))
            local[-1].start()
            for j, chip in enumerate(chips):
                cp = _remote(ins[t].at[rows(t, c)], outs[t].at[mine, rows(t, c)], send_sems, recv_sems, 3 * t + j, (*chip, c))
                cp.start()
                sent.append(cp)
        for j, chip in enumerate(chips):
            theirs = 2 * chip[0] + chip[1]
            for t in range(n):
                piece = outs[t].at[theirs, rows(t, c)]
                _remote(piece, piece, send_sems, recv_sems, 3 * t + j, (x, y, c)).wait_recv()
                if split[t]:
                    cp = _remote(piece, piece, send_sems, recv_sems, 3 * n + 3 * t + j, (x, y, 1 - c))
                    cp.start()
                    sent.append(cp)
        for j, chip in enumerate(chips):
            theirs = 2 * chip[0] + chip[1]
            for t in range(n):
                if split[t]:
                    piece = outs[t].at[theirs, rows(t, 1 - c)]
                    _remote(piece, piece, send_sems, recv_sems, 3 * n + 3 * t + j, (x, y, c)).wait_recv()
        for cp in sent:
            cp.wait_send()
        for cp in local:
            cp.wait()

    return pl.kernel(
        body, name=f"gather_{tag}", mesh=plsc.ScalarSubcoreMesh(axis_name="sequencer", num_cores=1),
        out_type=[jax.ShapeDtypeStruct((N_SHARDS, *s.shape), s.dtype) for s in shards],
        scratch_types=[pltpu.SemaphoreType.DMA((6 * n,)), pltpu.SemaphoreType.DMA((6 * n,)), pltpu.SemaphoreType.DMA((n,))],
        compiler_params=pltpu.CompilerParams(collective_id=ID_GATHER),
    )(*shards)


SEQUENCER = dict(axis_name="sequencer", num_cores=1)
ID_GATHER, ID_SHARE = 0, 1
MIN_SPLIT_ROWS = 16


def _handshake(peers):
    barrier = pltpu.get_barrier_semaphore()
    for peer in peers:
        pl.semaphore_signal(barrier, inc=1, device_id=peer, device_id_type=MESH)
    pl.semaphore_wait(barrier, len(peers))


def _half_rows(rows, half):
    return pl.ds(half * (rows // 2), rows // 2) if rows >= MIN_SPLIT_ROWS else pl.ds(0, rows)


SEM_SPEC = pl.BlockSpec(memory_space=pltpu.SEMAPHORE)
DATAFLOW = pltpu.SideEffectType.DATAFLOW_SIDE_EFFECTING


def _exchange_copies(shapes, stacks, lands, send_sems, recv_sems):
    x, y, c, chips = _place()
    mine = 2 * x + y
    copies = []
    for t, shape in enumerate(shapes):
        r = shape[1]
        copies.append(_remote(stacks[t].at[mine, _half_rows(r, 1 - c)], lands[t].at[0], send_sems, recv_sems, 7 * t, (x, y, 1 - c)))
        for j, chip in enumerate(chips):
            theirs = 2 * chip[0] + chip[1]
            copies.append(_remote(stacks[t].at[theirs, _half_rows(r, c)], lands[t].at[1 + j], send_sems, recv_sems,
                                  7 * t + 1 + j, (*chip, c)))
            copies.append(_remote(stacks[t].at[theirs, _half_rows(r, 1 - c)], lands[t].at[4 + j], send_sems, recv_sems,
                                  7 * t + 4 + j, (*chip, 1 - c)))
    return copies


def _exchange_start(tag, stacks, carry):
    n = len(stacks)
    shapes = [s.shape for s in stacks]
    lands = [lax.empty((7, s.shape[1] // 2 if s.shape[1] >= MIN_SPLIT_ROWS else s.shape[1], s.shape[2]), s.dtype) for s in stacks]

    def body(*refs):
        for cp in _exchange_copies(shapes, refs[:n], refs[n:2 * n], refs[2 * n + 1], refs[2 * n + 2]):
            cp.start()

    through = (*stacks, *lands, carry)
    out = pl.pallas_call(
        body, name=f"reduce_exchange_start_{tag}",
        out_shape=(pltpu.SemaphoreType.DMA((7 * n,)), pltpu.SemaphoreType.DMA((7 * n,)),
                   *[pltpu.HBM(a.shape, a.dtype) for a in through]),
        in_specs=[HBM_SPEC] * (2 * n + 1),
        out_specs=(SEM_SPEC, SEM_SPEC, *[HBM_SPEC] * (2 * n + 1)),
        input_output_aliases={t: 2 + t for t in range(2 * n + 1)},
        compiler_params=pltpu.CompilerParams(has_side_effects=DATAFLOW),
    )(*[pltpu.with_memory_space_constraint(a, pltpu.HBM) for a in through])
    return out[0], out[1], out[2:2 + n], out[2 + n:2 + 2 * n], out[-1]


def _exchange_wait(tag, send_sems, recv_sems, stacks, lands, after):
    n = len(stacks)
    shapes = [s.shape for s in stacks]

    def body(*refs):
        for cp in _exchange_copies(shapes, refs[:n], refs[n:2 * n], refs[2 * n], refs[2 * n + 1]):
            cp.wait()

    out = pl.pallas_call(
        body, name=f"reduce_exchange_wait_{tag}",
        out_shape=tuple(pltpu.HBM(a.shape, a.dtype) for a in (*stacks, *lands)),
        in_specs=[HBM_SPEC] * (2 * n) + [SEM_SPEC, SEM_SPEC, pl.BlockSpec(memory_space=pl.ANY)],
        out_specs=tuple([HBM_SPEC] * (2 * n)),
        input_output_aliases={t: t for t in range(2 * n)},
        compiler_params=pltpu.CompilerParams(has_side_effects=DATAFLOW),
    )(*stacks, *lands, send_sems, recv_sems, after)
    return out[:n], out[n:]


def _share_halves(tag, halves):
    n = len(halves)

    def body(*refs):
        ins, outs, send_sems, recv_sems = refs[:n], refs[n:2 * n], refs[2 * n], refs[2 * n + 1]
        x, y, c, _ = _place()
        _handshake([(x, y, 1 - c)])
        sent = [_remote(ins[t], outs[t], send_sems, recv_sems, t, (x, y, 1 - c)) for t in range(n)]
        for cp in sent:
            cp.start()
        for cp in sent:
            cp.wait()

    return pl.kernel(
        body, name=f"reduce_share_{tag}", mesh=plsc.ScalarSubcoreMesh(**SEQUENCER),
        out_type=[jax.ShapeDtypeStruct(h.shape, h.dtype) for h in halves],
        scratch_types=[pltpu.SemaphoreType.DMA((n,)), pltpu.SemaphoreType.DMA((n,))],
        compiler_params=pltpu.CompilerParams(collective_id=ID_SHARE),
    )(*halves)


def _all_reduce_small(part):
    rows = part.shape[0]
    half = rows // 2

    def body(p_ref, out_ref, sib_buf, chip_sums, send_sems, recv_sems):
        x, y, c, chips = _place()
        mine = 2 * x + y
        my_rows = pl.ds(pl.multiple_of(c * half, 8), half)
        swap = _remote(p_ref, sib_buf, send_sems, recv_sems, 0, (x, y, 1 - c))
        swap.start()
        swap.wait()
        chip_sums[mine] = p_ref[...] + sib_buf[...]
        sent = [_remote(chip_sums.at[mine, my_rows], chip_sums.at[mine, my_rows], send_sems, recv_sems, 1 + j, (*chip, c))
                for j, chip in enumerate(chips)]
        for cp in sent:
            cp.start()
        for j, chip in enumerate(chips):
            sent[j].wait_send()
            theirs = chip_sums.at[2 * chip[0] + chip[1], my_rows]
            _remote(theirs, theirs, send_sems, recv_sems, 1 + j, (x, y, c)).wait_recv()
        out_ref[my_rows, :] = ((chip_sums[0, my_rows, :] + chip_sums[1, my_rows, :]) + chip_sums[2, my_rows, :]) + chip_sums[3, my_rows, :]
        share = _remote(out_ref.at[my_rows], out_ref.at[my_rows], send_sems, recv_sems, 4, (x, y, 1 - c))
        share.start()
        share.wait_send()
        other = out_ref.at[pl.ds(pl.multiple_of((1 - c) * half, 8), half)]
        _remote(other, other, send_sems, recv_sems, 4, (x, y, c)).wait_recv()

    vmem = pl.BlockSpec(memory_space=pltpu.VMEM)
    return pl.pallas_call(
        body, name="all_reduce_small", in_specs=[vmem], out_specs=vmem, out_shape=jax.ShapeDtypeStruct(part.shape, F32),
        scratch_shapes=[pltpu.VMEM((rows, LANE), F32), pltpu.VMEM((N_SHARDS, rows, LANE), F32),
                        pltpu.SemaphoreType.DMA((5,)), pltpu.SemaphoreType.DMA((5,))],
        compiler_params=pltpu.CompilerParams(vmem_limit_bytes=VMEM_LIMIT_BYTES),
    )(part)


def _sum_partials(g3, others, sel):
    _, rows, c = others.shape
    whole = g3.shape[1] == rows
    tr = _tile(rows, UPDATE_ROWS)
    nb = rows // tr

    def body(sel_ref, g_ref, *rest):
        same = g_ref[...].astype(F32)
        for ref in rest[1:4]:
            same = same + ref[...].astype(F32)
        other = rest[0][...].astype(F32)
        for ref in rest[4:7]:
            other = other + ref[...].astype(F32)
        rest[7][...] = same + other

    blk = (None, tr, c)
    slots = [pl.BlockSpec(blk, functools.partial(lambda i, sr, k: (k, i, 0), k=k)) for k in range(7)]
    return pl.pallas_call(
        body, name="reduce_sum_partials",
        grid_spec=pltpu.PrefetchScalarGridSpec(
            num_scalar_prefetch=1, grid=(nb,),
            in_specs=[pl.BlockSpec(blk, lambda i, sr: (sr[0], (0 if whole else sr[1] * nb) + i, 0))] + slots,
            out_specs=pl.BlockSpec((tr, c), lambda i, sr: (i, 0))),
        out_shape=jax.ShapeDtypeStruct((rows, c), F32),
        compiler_params=_params(("parallel",)),
    )(sel, g3, *[others] * 7)


def _adamw_math(w, g, m, v):
    nm = ADAM_B1 * m + (1.0 - ADAM_B1) * g
    nv = ADAM_B2 * v + (1.0 - ADAM_B2) * (g * g)
    m_hat = nm / (1.0 - ADAM_B1 ** ADAM_STEP)
    v_hat = nv / (1.0 - ADAM_B2 ** ADAM_STEP)
    return -ADAM_LR * (m_hat / (jnp.sqrt(v_hat) + ADAM_EPS) + ADAM_WD * w), nm, nv


def _adamw_layer(layer, w, m, v, g_mine, g_sibling, sel, prev):
    lyr, r, c = w.shape
    rows = g_mine.shape[0]
    halves = r // rows
    tr = _tile(rows, 512)
    nb = rows // tr
    n_g = 1 if g_sibling is None else 2

    def body(sel_ref, w_ref, m_ref, v_ref, *rest):
        g = rest[0][...]
        if n_g == 2:
            g = jnp.where(pl.program_id(0) == sel_ref[1], g, rest[1][...])
        outs = rest[n_g + (0 if prev is None else 4):]
        d, nm, nv = _adamw_math(w_ref[...], g, m_ref[...], v_ref[...])
        for ref, val in zip(outs, (g, d, nm, nv)):
            ref[...] = val

    full = pl.BlockSpec((None, tr, c), lambda h, i, sr: (layer, h * nb + i, 0))
    part = pl.BlockSpec((tr, c), lambda h, i, sr: (i, 0))
    n_in = 4 + n_g
    return pl.pallas_call(
        body, name="adamw_layer",
        grid_spec=pltpu.PrefetchScalarGridSpec(
            num_scalar_prefetch=1, grid=(halves, nb),
            in_specs=[full] * 3 + [part] * n_g + ([] if prev is None else [pl.BlockSpec(memory_space=pl.ANY)] * 4),
            out_specs=[full] * 4),
        out_shape=[jax.ShapeDtypeStruct(w.shape, F32)] * 4,
        input_output_aliases={} if prev is None else {n_in + k: k for k in range(4)},
        compiler_params=_params(("parallel", "parallel")),
    )(sel, w, m, v, g_mine, *([] if g_sibling is None else [g_sibling]), *([] if prev is None else prev))


def _adamw(w, g, m, v):
    lyr, r, c = w.shape
    tr = _tile(r, 256)

    def body(w_ref, g_ref, m_ref, v_ref, d_ref, nm_ref, nv_ref):
        d_ref[...], nm_ref[...], nv_ref[...] = _adamw_math(w_ref[...], g_ref[...], m_ref[...], v_ref[...])

    blk = pl.BlockSpec((None, tr, c), lambda l, i: (l, i, 0))
    return pl.pallas_call(
        body, name="adamw", grid=(lyr, r // tr), in_specs=[blk] * 4, out_specs=[blk] * 3,
        out_shape=[jax.ShapeDtypeStruct(w.shape, F32)] * 3,
        compiler_params=_params(("parallel", "parallel")),
    )(w, g, m, v)


SHARDED = ("mla_w_dkv", "mla_w_uq", "mla_w_ukv", "mla_w_o", "sgu_w_in", "sgu_ln_g", "sgu_ln_b", "sgu_w_out",
           "ffn_w_up", "ffn_w_down")
REPLICATED = ("norm_mix", "norm_ffn", "final_norm", "mla_q_norm", "mla_kv_norm", "sgu_w_spatial", "sgu_b_spatial")
WEIGHTS = ("norm_mix", "norm_ffn", "final_norm", "mla_w_dkv", "mla_q_norm", "mla_kv_norm", "mla_w_uq", "mla_w_ukv",
           "mla_w_o", "sgu_w_in", "sgu_ln_g", "sgu_ln_b", "sgu_w_spatial", "sgu_b_spatial", "sgu_w_out", "ffn_w_up",
           "ffn_w_down")


class _Reducer:
    def __init__(self, state, sel):
        self.state, self.sel = state, sel
        self.started, self.travelling, self.summed = [], [], []
        self.done = {}

    def add(self, tag, layer, grads, token):
        names = list(grads)
        token, *tied = lax.optimization_barrier((token, *[a for n in names for a in grads[n]]))
        f32s, bf16s = tied[0::2], tied[1::2]
        *flying, token = _exchange_start(tag, bf16s, token)
        self.started.append((tag, layer, names, f32s, flying))
        return token

    def phase_end(self, token):
        for tag, layer, names, f32s, flying in self.travelling:
            bf16s, received = _exchange_wait(tag, *flying, token)
            own = [g if g.shape[1] >= MIN_SPLIT_ROWS else gb for g, gb in zip(f32s, bf16s)]
            mine = [_sum_partials(g, got, self.sel) for g, got in zip(own, received)]
            token, *mine = lax.optimization_barrier((token, *mine))
            cut = [k for k, g in enumerate(own) if g.shape[1] >= MIN_SPLIT_ROWS]
            theirs = dict(zip(cut, _share_halves(tag, [mine[k] for k in cut])))
            self.summed.append((layer, names, mine, [theirs.get(k) for k in range(len(names))]))
        self.travelling, self.started = self.started, []
        return token

    def update(self, token):
        for layer, names, mine, theirs in self.summed:
            for name, g_mine, g_theirs in zip(names, mine, theirs):
                w, m, v = self.state[name]
                self.done[name] = _adamw_layer(layer, w, m, v, g_mine, g_theirs, self.sel, self.done.get(name))
                token = self.done[name][1]
        self.summed = []
        return token


def _as3d(name, a):
    return a.reshape(a.shape[0], -1, LANE) if name in ("sgu_ln_g", "sgu_ln_b") else a


def _pack(parts):
    flat = jnp.concatenate([p.reshape(-1) for p in parts])
    rows = -(-flat.shape[0] // (256 * LANE)) * 256
    return jnp.pad(flat, (0, rows * LANE - flat.shape[0])).reshape(rows, LANE)


def _unpack(packed, like):
    flat, out, at = packed.reshape(-1), [], 0
    for p in like:
        out.append(flat[at:at + p.size].reshape(p.shape))
        at += p.size
    return out


def kernel(x, positions, norm_mix, norm_ffn, final_norm, mla_w_dkv, mla_q_norm, mla_kv_norm, mla_w_uq, mla_w_ukv, mla_w_o, sgu_w_in, sgu_ln_g, sgu_ln_b, sgu_w_spatial, sgu_b_spatial, sgu_w_out, ffn_w_up, ffn_w_down, loss_target, m_norm_mix, m_norm_ffn, m_final_norm, m_mla_w_dkv, m_mla_q_norm, m_mla_kv_norm, m_mla_w_uq, m_mla_w_ukv, m_mla_w_o, m_sgu_w_in, m_sgu_ln_g, m_sgu_ln_b, m_sgu_w_spatial, m_sgu_b_spatial, m_sgu_w_out, m_ffn_w_up, m_ffn_w_down, v_norm_mix, v_norm_ffn, v_final_norm, v_mla_w_dkv, v_mla_q_norm, v_mla_kv_norm, v_mla_w_uq, v_mla_w_ukv, v_mla_w_o, v_sgu_w_in, v_sgu_ln_g, v_sgu_ln_b, v_sgu_w_spatial, v_sgu_b_spatial, v_sgu_w_out, v_ffn_w_up, v_ffn_w_down):
    given = dict(locals())
    w = {n: given[n] for n in WEIGHTS}
    mom = {n: given["m_" + n] for n in WEIGHTS}
    var = {n: given["v_" + n] for n in WEIGHTS}
    mixers, ffn, token = [], [], None
    for i in range(DEPTH):
        j = i // 2
        if i % 2 == 0:
            mixer = [w[n][j].astype(BF16) for n in ("mla_w_dkv", "mla_w_uq", "mla_w_ukv", "mla_w_o")]
        else:
            mixer = [sgu_w_in[j].astype(BF16), sgu_w_out[j].astype(BF16), sgu_ln_g[j].reshape(-1, LANE),
                     sgu_ln_b[j].reshape(-1, LANE)]
        for tag, shards, into in ((f"mixer{i}", mixer, mixers), (f"ffn{i}", [ffn_w_up[i].astype(BF16), ffn_w_down[i].astype(BF16)], ffn)):
            if token is None:
                token = shards[0]
            else:
                token, *shards = lax.optimization_barrier((token, *shards))
            into.append(_gather_layer(tag, shards))

    x_i, y_i, c_i = lax.axis_index("x"), lax.axis_index("y"), lax.axis_index("c")
    sel = jnp.stack([2 * x_i + y_i, c_i]).astype(jnp.int32)
    reducer = _Reducer({n: tuple(_as3d(n, d[n]) for d in (w, mom, var)) for n in SHARDED}, sel)
    loss, dx, small = _local_step(
        x[0], positions[0], loss_target[0], norm_mix, norm_ffn, final_norm, mla_q_norm, mla_kv_norm, sgu_w_spatial,
        sgu_b_spatial, mixers, ffn, reducer)
    loss = lax.psum(loss, ("x", "y", "c"))

    small_g = [small["norm_mix"], small["norm_ffn"], small["final_norm"], small["q_norm"], small["kv_norm"],
               small["w_sp"], small["b_sp"]]
    like = [w[n] for n in REPLICATED]
    g_small = _all_reduce_small(_pack(small_g))
    packed = [_pack([d[n] for n in REPLICATED])[None] for d in (w, mom, var)]
    upd_small = _adamw(packed[0], g_small[None], packed[1], packed[2])
    grads = dict(zip(REPLICATED, _unpack(g_small, like)))
    delta, new_m, new_v = ({n: a for n, a in zip(REPLICATED, _unpack(u[0], like))} for u in upd_small)

    reducer.phase_end(reducer.update(upd_small[0]))
    reducer.update(None)
    for n in SHARDED:
        grads[n], delta[n], new_m[n], new_v[n] = (a.reshape(w[n].shape) for a in reducer.done[n])

    return (loss, dx[None], *[grads[n] for n in WEIGHTS], *[delta[n] for n in WEIGHTS],
            *[new_m[n] for n in WEIGHTS], *[new_v[n] for n in WEIGHTS])
```

```python
import functools
import math

import jax
import jax.numpy as jnp
from jax import lax
from jax.experimental import pallas as pl
from jax.experimental.pallas import tpu as pltpu
from jax.experimental.pallas import tpu_sc as plsc

F32 = jnp.float32
BF16 = jnp.bfloat16
MESH = pl.DeviceIdType.MESH

DEPTH = 4
HEADS = 8
NOPE = 128
ROPE = 64
VHEAD = 128
QK_HEAD = NOPE + ROPE
Q_RANK = 256
KV_RANK = 128
HEAD_PAD = 256
LAT_PAD = 512
ROPE_THETA = 10000.0
SGU_CHUNK = 128
SGU_GROUPS = 8
NORM_EPS = 1e-6
LN_EPS = 1e-5
ADAM_LR, ADAM_B1, ADAM_B2, ADAM_EPS, ADAM_WD, ADAM_STEP = 0.001, 0.9, 0.999, 1e-08, 0.01, 10

N_SHARDS = 4
LANE = 128
VMEM_LIMIT_BYTES = 56 * 1024 * 1024
ATT_TILE = 512
MM_TILE = 1024
UPDATE_ROWS = 256
ATT_SCALE = QK_HEAD ** -0.5
LOG2_SCALE = ATT_SCALE * math.log2(math.e)

NN = (((1,), (0,)), ((), ()))
NT = (((1,), (1,)), ((), ()))
TN = (((0,), (0,)), ((), ()))


def _params(sem):
    return pltpu.CompilerParams(dimension_semantics=sem, vmem_limit_bytes=VMEM_LIMIT_BYTES)


def _tile(n, pref):
    t = min(n, pref)
    while n % t:
        t //= 2
    return t


def _matmul(name, a, b, a_spec, b_spec, dims, grid, tile, outs, extras=(), epilogue=None, sums=()):
    nk, ne, no = grid[2], len(extras), len(outs)
    b_specs = list(b_spec) if isinstance(b_spec, (list, tuple)) else [b_spec]
    nb = len(b_specs)

    def body(a_ref, *rest):
        b_refs, e_refs, o_refs = rest[:nb], rest[nb:nb + ne], rest[nb + ne:nb + ne + no]
        s_refs = rest[nb + ne + no:nb + ne + no + len(sums)]
        kw = a_ref.shape[1] // nb
        part = None
        for p, b_ref in enumerate(b_refs):
            a_tile = a_ref[...] if nb == 1 else a_ref[:, p * kw:(p + 1) * kw]
            d = lax.dot_general(a_tile.astype(BF16), b_ref[...].astype(BF16), dims, preferred_element_type=F32)
            part = d if part is None else part + d

        def finish(acc):
            vals = (acc,) if epilogue is None else epilogue(acc, *[e[...] for e in e_refs])
            for o_ref, v in zip(o_refs, vals):
                o_ref[...] = v.astype(o_ref.dtype)
            first = pl.program_id(0) == 0
            for s_ref, v in zip(s_refs, vals[no:]):
                @pl.when(first)
                def _():
                    s_ref[...] = v

                @pl.when(jnp.logical_not(first))
                def _():
                    s_ref[...] += v

        if nk == 1:
            finish(part)
            return
        acc_ref, k = rest[-1], pl.program_id(2)

        @pl.when(k == 0)
        def _():
            acc_ref[...] = part

        @pl.when(jnp.logical_and(k > 0, k < nk - 1))
        def _():
            acc_ref[...] += part

        @pl.when(k == nk - 1)
        def _():
            finish(acc_ref[...] + part)

    assert not sums or (grid[1] == 1 and nk == 1)
    return pl.pallas_call(
        body, name=name, grid=grid,
        in_specs=[a_spec] + b_specs + [s for _, s in extras],
        out_specs=[s for _, s in outs] + [pl.BlockSpec(s, lambda i, j, k: (0,) * len(s)) for s in sums],
        out_shape=[s for s, _ in outs] + [jax.ShapeDtypeStruct(s, F32) for s in sums],
        scratch_shapes=[pltpu.VMEM(tile, F32)] if nk > 1 else [],
        compiler_params=_params(("arbitrary" if sums else "parallel", "parallel", "arbitrary")),
    )(a, *[b] * nb, *[e for e, _ in extras])


def _epilogue_operands(extras, consts, o_spec):
    return [(e, o_spec) for e in extras] + [(c, pl.BlockSpec(c.shape, lambda i, j, k: (0, 0))) for c in consts]


def _mm(name, a, b, out_dtypes=(F32,), epilogue=None, extras=(), tm=MM_TILE, tn=MM_TILE, tk=MM_TILE, nt=False,
        consts=(), sums=(), narrow=()):
    m, kd = a.shape
    n = b.shape[0] if nt else b.shape[1]
    tm, tn, tk = _tile(m, tm), _tile(n, tn), _tile(kd, tk)
    o_spec = pl.BlockSpec((tm, tn), lambda i, j, k: (i, j))
    b_spec = pl.BlockSpec((tn, tk), lambda i, j, k: (j, k)) if nt else pl.BlockSpec((tk, tn), lambda i, j, k: (k, j))
    assert not narrow or n == tn
    outs = [(jax.ShapeDtypeStruct((m, n), d), o_spec) for d in out_dtypes]
    outs += [(jax.ShapeDtypeStruct((m, w), d), pl.BlockSpec((tm, w), lambda i, j, k: (i, 0))) for w, d in narrow]
    return _matmul(name, a, b, pl.BlockSpec((tm, tk), lambda i, j, k: (i, k)), b_spec, NT if nt else NN,
                   (m // tm, n // tn, kd // tk), (tm, tn), outs,
                   _epilogue_operands(extras, consts, o_spec), epilogue, sums)


def _mm_tn(name, a, b, out_dtypes=(F32,), tm=MM_TILE, tn=MM_TILE, tk=MM_TILE):
    s, m = a.shape
    n = b.shape[1]
    tm, tn, tk = _tile(m, tm), _tile(n, tn), _tile(s, tk)
    o_spec = pl.BlockSpec((tm, tn), lambda i, j, k: (i, j))
    return _matmul(name, a, b, pl.BlockSpec((tk, tm), lambda i, j, k: (k, i)),
                   pl.BlockSpec((tk, tn), lambda i, j, k: (k, j)), TN, (m // tm, n // tn, s // tk), (tm, tn),
                   [(jax.ShapeDtypeStruct((m, n), d), o_spec) for d in out_dtypes])


def _mm_stacked(name, a, w3, mode, out_dtypes=(F32,), epilogue=None, extras=(), tm=MM_TILE, tn=MM_TILE, tk=MM_TILE,
                consts=()):
    m, kd = a.shape
    _, r, c = w3.shape
    n = c if mode == "row" else N_SHARDS * c
    if mode == "row":
        tm, tn, tk = _tile(m, tm // 2), _tile(n, tn), kd
        b_spec = [pl.BlockSpec((None, r, tn), functools.partial(lambda i, j, k, p: (p, 0, j), p=p)) for p in range(N_SHARDS)]
    else:
        tm, tn, tk = _tile(m, tm), _tile(c, tn), _tile(kd, tk)
        per = c // tn
        b_spec = pl.BlockSpec((None, tk, tn), lambda i, j, k: (j // per, k, j % per))
    o_spec = pl.BlockSpec((tm, tn), lambda i, j, k: (i, j))
    return _matmul(name, a, w3, pl.BlockSpec((tm, tk), lambda i, j, k: (i, k)), b_spec, NN,
                   (m // tm, n // tn, kd // tk), (tm, tn),
                   [(jax.ShapeDtypeStruct((m, n), d), o_spec) for d in out_dtypes],
                   _epilogue_operands(extras, consts, o_spec), epilogue)


def _mm_stacked_nt(name, a, w3, mode, out_dtypes=(F32,), epilogue=None, extras=(), tm=MM_TILE, tn=MM_TILE, tk=MM_TILE,
                   consts=(), sums=()):
    m, nd = a.shape
    _, r, c = w3.shape
    kout = N_SHARDS * r if mode == "row" else r
    if mode == "row":
        tm, tn, tk = _tile(m, tm), _tile(r, tn), _tile(c, tk)
        per = r // tn
        b_spec = pl.BlockSpec((None, tn, tk), lambda i, j, k: (j // per, j % per, k))
    else:
        tm, tn, tk = _tile(m, tm // 2), _tile(r, tn), nd
        b_spec = [pl.BlockSpec((None, tn, c), functools.partial(lambda i, j, k, p: (p, j, 0), p=p)) for p in range(N_SHARDS)]
    o_spec = pl.BlockSpec((tm, tn), lambda i, j, k: (i, j))
    return _matmul(name, a, w3, pl.BlockSpec((tm, tk), lambda i, j, k: (i, k)), b_spec, NT,
                   (m // tm, kout // tn, nd // tk), (tm, tn),
                   [(jax.ShapeDtypeStruct((m, kout), d), o_spec) for d in out_dtypes],
                   _epilogue_operands(extras, consts, o_spec), epilogue, sums)


def _mm_tn_stacked(name, a, b, shape3, mode, tm=MM_TILE, tn=MM_TILE, tk=MM_TILE):
    s, m = a.shape
    n = b.shape[1]
    _, r, c = shape3
    tk, tn = s, tn // 2
    if mode == "row":
        tm, tn = _tile(r, tm), _tile(n, tn)
        per = r // tm
        o_spec = pl.BlockSpec((None, tm, tn), lambda i, j, k: (i // per, i % per, j))
    else:
        tm, tn = _tile(m, tm), _tile(c, tn)
        per = c // tn
        o_spec = pl.BlockSpec((None, tm, tn), lambda i, j, k: (j // per, i, j % per))
    outs = [(jax.ShapeDtypeStruct(shape3, F32), o_spec), (jax.ShapeDtypeStruct(shape3, BF16), o_spec)]
    return _matmul(name, a, b, pl.BlockSpec((tk, tm), lambda i, j, k: (k, i)),
                   pl.BlockSpec((tk, tn), lambda i, j, k: (k, j)), TN, (m // tm, n // tn, s // tk), (tm, tn),
                   outs, epilogue=lambda acc: (acc, acc))


def _rowwise(name, fn, rows, consts, out_rows, out_accs=(), tr=256):
    nr, nc, no = len(rows), len(consts), len(out_rows)
    n_rows = rows[0].shape[0]
    tr = _tile(n_rows, tr)

    def body(*refs):
        vals = fn(*[r[...] for r in refs[:nr + nc]])
        o_refs, a_refs = refs[nr + nc:nr + nc + no], refs[nr + nc + no:]
        for ref, v in zip(o_refs, vals[:no]):
            ref[...] = v.astype(ref.dtype)
        first = pl.program_id(0) == 0

        @pl.when(first)
        def _():
            for ref, v in zip(a_refs, vals[no:]):
                ref[...] = v

        @pl.when(jnp.logical_not(first))
        def _():
            for ref, v in zip(a_refs, vals[no:]):
                ref[...] += v

    def whole(shape):
        return pl.BlockSpec(shape, lambda i: (0,) * len(shape))

    return pl.pallas_call(
        body, name=name, grid=(n_rows // tr,),
        in_specs=[pl.BlockSpec((tr, a.shape[1]), lambda i: (i, 0)) for a in rows] + [whole(c.shape) for c in consts],
        out_specs=[pl.BlockSpec((tr, f), lambda i: (i, 0)) for f, _ in out_rows] + [whole(s) for s in out_accs],
        out_shape=[jax.ShapeDtypeStruct((n_rows, f), d) for f, d in out_rows]
        + [jax.ShapeDtypeStruct(s, F32) for s in out_accs],
        compiler_params=_params(("arbitrary",)),
    )(*rows, *consts)


def _rms_fwd(x, g):
    return x * lax.rsqrt(jnp.mean(x * x, axis=-1, keepdims=True) + NORM_EPS) * g


def _rms_bwd(dy, x, g):
    rstd = lax.rsqrt(jnp.mean(x * x, axis=-1, keepdims=True) + NORM_EPS)
    n = x * rstd
    dn = dy * g
    dx = rstd * (dn - n * jnp.mean(dn * n, axis=-1, keepdims=True))
    return dx, jnp.sum(dy * n, axis=0, keepdims=True)


def _rope(x, cs, s1, s2):
    return x * cs + pltpu.roll(x, 32, 1) * s1 + pltpu.roll(x, 96, 1) * s2


def _rope_t(dy, cs, s1, s2):
    return dy * cs + pltpu.roll(dy * s1, 96, 1) + pltpu.roll(dy * s2, 32, 1)


def _gelu(z):
    return 0.5 * z * (1.0 + lax.erf(z * (1.0 / math.sqrt(2.0))))


def _gelu_and_grad(z):
    cdf = 0.5 * (1.0 + lax.erf(z * (1.0 / math.sqrt(2.0))))
    return z * cdf, cdf + z * jnp.exp(-0.5 * z * z) * (1.0 / math.sqrt(2.0 * math.pi))


def _att_scores(q, kv, kr, masked, transposed):
    k = jnp.concatenate([kv[:, :NOPE], kr], axis=1)
    if transposed:
        s = lax.dot_general(k, q, NT, preferred_element_type=F32)
    else:
        s = lax.dot_general(q, k, NT, preferred_element_type=F32)
    if masked:
        r = lax.broadcasted_iota(jnp.int32, s.shape, 0)
        c = lax.broadcasted_iota(jnp.int32, s.shape, 1)
        s = jnp.where((r <= c) if transposed else (c <= r), s, -jnp.inf)
    return s, k


def _in_pairs(lo, hi, pair, single):
    n = hi - lo

    def body(p, carry):
        pair(lo + 2 * p, lo + 2 * p + 1)
        return carry

    lax.fori_loop(0, n // 2, body, 0)

    @pl.when(n % 2 == 1)
    def _():
        single(hi - 1)


def _causal_tiles(i, pair, single):
    @pl.when(i == 0)
    def _():
        single(i, True)

    @pl.when(i > 0)
    def _():
        _in_pairs(0, i - 1, lambda a, b: pair(a, b, False), lambda a: single(a, False))
        pair(i - 1, i, True)


def _flash_fwd(q, kvb, krb, tables):
    s_len = q.shape[0]
    t = ATT_TILE

    def body(q_ref, cs_ref, s1_ref, s2_ref, kv_ref, kr_ref, o_ref, lse_ref, qb_ref, m_s, l_s, acc_s):
        qi = pl.program_id(1)
        m_s[...] = jnp.full_like(m_s, -jnp.inf)
        l_s[...] = jnp.zeros_like(l_s)
        acc_s[...] = jnp.zeros_like(acc_s)
        qv = q_ref[...]
        q = jnp.concatenate([qv[:, :NOPE], _rope(qv[:, NOPE:], cs_ref[...], s1_ref[...], s2_ref[...])], axis=1)
        q = (q * LOG2_SCALE).astype(BF16)
        qb_ref[...] = q

        def scores(ki, masked):
            rows = pl.ds(pl.multiple_of(ki * t, t), t)
            kv = kv_ref[rows, :]
            return _att_scores(q, kv, kr_ref[rows, :], masked, False)[0], kv

        def update(s, kv):
            m_prev = m_s[...]
            m_new = jnp.maximum(m_prev, jnp.max(s, axis=1, keepdims=True))
            alpha = jnp.exp2(m_prev - m_new)
            p = jnp.exp2(s - jnp.tile(m_new, (1, t // LANE)))
            l_s[...] = alpha * l_s[...] + jnp.sum(p, axis=1, keepdims=True)
            acc_s[...] = alpha * acc_s[...] + jnp.dot(p.astype(BF16), kv[:, NOPE:], preferred_element_type=F32)
            m_s[...] = m_new

        def pair(k0, k1, masked):
            first, second = scores(k0, False), scores(k1, masked)
            update(*first)
            update(*second)

        _causal_tiles(qi, pair, lambda ki, masked: update(*scores(ki, masked)))
        o_ref[...] = (acc_s[...] / l_s[...]).astype(o_ref.dtype)
        lse_ref[...] = (m_s[...] + jnp.log2(l_s[...])).T[:1, :]

    table = pl.BlockSpec((t, LANE), lambda h, qi: (qi, 0))
    return pl.pallas_call(
        body, name="flash_fwd", grid=(HEADS, s_len // t),
        in_specs=[pl.BlockSpec((t, HEAD_PAD), lambda h, qi: (qi, h)), table, table, table,
                  pl.BlockSpec((s_len, HEAD_PAD), lambda h, qi: (0, h)),
                  pl.BlockSpec((s_len, LANE), lambda h, qi: (0, 0))],
        out_specs=[pl.BlockSpec((t, VHEAD), lambda h, qi: (qi, h)),
                   pl.BlockSpec((None, 1, t), lambda h, qi: (h, 0, qi)),
                   pl.BlockSpec((t, HEAD_PAD), lambda h, qi: (qi, h))],
        out_shape=[jax.ShapeDtypeStruct((s_len, HEADS * VHEAD), BF16),
                   jax.ShapeDtypeStruct((HEADS, 1, s_len), F32),
                   jax.ShapeDtypeStruct((s_len, HEADS * HEAD_PAD), BF16)],
        scratch_shapes=[pltpu.VMEM((t, LANE), F32), pltpu.VMEM((t, LANE), F32), pltpu.VMEM((t, VHEAD), F32)],
        compiler_params=_params(("parallel", "arbitrary")),
    )(q, *tables, kvb, krb)


def _flash_bwd(qb, kvb, krb, dob, lse_row, delta_row):
    s_len = qb.shape[0]
    t = ATT_TILE
    nq = s_len // t
    scale = QK_HEAD ** -0.5

    def body(q_ref, kv_ref, kr_ref, do_ref, lse_ref, dl_ref, dq_ref, dkv_ref, dkr_ref, dk_s, dv_s):
        ki = pl.program_id(1)

        @pl.when(ki == 0)
        def _():
            dq_ref[...] = jnp.zeros_like(dq_ref)

        dk_s[...] = jnp.zeros_like(dk_s)
        dv_s[...] = jnp.zeros_like(dv_s)
        kv, kr = kv_ref[...], kr_ref[...]

        def products(qi, masked):
            rows = pl.ds(pl.multiple_of(qi * t, t), t)
            q, do = q_ref[rows, :], do_ref[rows, :]
            st, k = _att_scores(q, kv, kr, masked, True)
            return st, lax.dot_general(kv[:, NOPE:], do, NT, preferred_element_type=F32), q, do, rows, k

        def update(st, dpt, q, do, rows, k):
            pt = jnp.exp2(st - lse_ref[:, rows])
            dv_s[...] += jnp.dot(pt.astype(BF16), do, preferred_element_type=F32)
            dst = (pt * (dpt - dl_ref[:, rows]) * scale).astype(BF16)
            dk_s[...] += jnp.dot(dst, q, preferred_element_type=F32)
            dq_ref[rows, :] += lax.dot_general(dst, k, TN, preferred_element_type=F32)

        def pair(q0, q1, masked):
            first, second = products(q0, masked), products(q1, False)
            update(*first)
            update(*second)

        @pl.when(ki == nq - 1)
        def _():
            update(*products(ki, True))

        @pl.when(ki < nq - 1)
        def _():
            pair(ki, ki + 1, True)
            _in_pairs(ki + 2, nq, lambda a, b: pair(a, b, False), lambda qi: update(*products(qi, False)))

        dk = dk_s[...] * (1.0 / LOG2_SCALE)
        dkv_ref[...] = jnp.concatenate([dk[:, :NOPE], dv_s[...]], axis=1).astype(dkv_ref.dtype)
        dkr_ref[...] = dk[:, NOPE:]

    row = pl.BlockSpec((None, 1, s_len), lambda h, ki: (h, 0, 0))
    return pl.pallas_call(
        body, name="flash_bwd", grid=(HEADS, nq),
        in_specs=[pl.BlockSpec((s_len, HEAD_PAD), lambda h, ki: (0, h)),
                  pl.BlockSpec((t, HEAD_PAD), lambda h, ki: (ki, h)),
                  pl.BlockSpec((t, LANE), lambda h, ki: (ki, 0)),
                  pl.BlockSpec((s_len, VHEAD), lambda h, ki: (0, h)), row, row],
        out_specs=[pl.BlockSpec((s_len, HEAD_PAD), lambda h, ki: (0, h)),
                   pl.BlockSpec((t, HEAD_PAD), lambda h, ki: (ki, h)),
                   pl.BlockSpec((t, LANE), lambda h, ki: (ki, h))],
        out_shape=[jax.ShapeDtypeStruct((s_len, HEADS * HEAD_PAD), F32),
                   jax.ShapeDtypeStruct((s_len, HEADS * HEAD_PAD), BF16),
                   jax.ShapeDtypeStruct((s_len, HEADS * LANE), F32)],
        scratch_shapes=[pltpu.VMEM((t, HEAD_PAD), F32), pltpu.VMEM((t, VHEAD), F32)],
        compiler_params=_params(("parallel", "arbitrary")),
    )(qb, kvb, krb, dob, lse_row, delta_row)


def _tril(w):
    r = lax.broadcasted_iota(jnp.int32, w.shape, 0)
    c = lax.broadcasted_iota(jnp.int32, w.shape, 1)
    return jnp.where(c <= r, w, 0.0)


def _sgu_row_stats(z_ref, width, gd, v_s, act, extra_s=None):
    total = None
    for g in range(SGU_GROUPS):
        cols = slice(g * gd, (g + 1) * gd)
        v = act(z_ref[:, width + g * gd:width + (g + 1) * gd])
        if extra_s is not None:
            v, extra_s[:, cols] = v
        v_s[:, cols] = v
        part = jnp.sum(v, axis=1, keepdims=True)
        total = part if total is None else total + part
    mean = total * (1.0 / width)
    sq = None
    for g in range(SGU_GROUPS):
        d = v_s[:, g * gd:(g + 1) * gd] - mean
        part = jnp.sum(d * d, axis=1, keepdims=True)
        sq = part if sq is None else sq + part
    return mean, lax.rsqrt(sq * (1.0 / width) + LN_EPS)


def _sgu_fwd(zpre, ln_g, ln_b, w_sp, bias_full):
    s_len, two_w = zpre.shape
    width = two_w // 2
    gd = width // SGU_GROUPS
    t = SGU_CHUNK

    def body(z_ref, g_ref, b_ref, w_ref, bias_ref, uv_ref, v_s):
        mean, rstd = _sgu_row_stats(z_ref, width, gd, v_s, _gelu)
        for g in range(SGU_GROUPS):
            cols = slice(g * gd, (g + 1) * gd)
            vln = ((v_s[:, cols] - mean) * rstd * g_ref[:, cols] + b_ref[:, cols]).astype(BF16)
            mixed = jnp.dot(_tril(w_ref[g]).astype(BF16), vln, preferred_element_type=F32) + bias_ref[:, cols]
            uv_ref[:, cols] = (_gelu(z_ref[:, cols]) * mixed).astype(uv_ref.dtype)

    return pl.pallas_call(
        body, name="sgu_fwd", grid=(s_len // t,),
        in_specs=[pl.BlockSpec((t, two_w), lambda i: (i, 0)), pl.BlockSpec((1, width), lambda i: (0, 0)),
                  pl.BlockSpec((1, width), lambda i: (0, 0)), pl.BlockSpec(w_sp.shape, lambda i: (0, 0, 0)),
                  pl.BlockSpec((t, width), lambda i: (0, 0))],
        out_specs=pl.BlockSpec((t, width), lambda i: (i, 0)),
        out_shape=jax.ShapeDtypeStruct((s_len, width), BF16),
        scratch_shapes=[pltpu.VMEM((t, width), F32)],
        compiler_params=_params(("parallel",)),
    )(zpre, ln_g, ln_b, w_sp, bias_full)


def _sgu_bwd(zpre, duv, ln_g, ln_b, w_sp, bias_full):
    s_len, two_w = zpre.shape
    width = two_w // 2
    gd = width // SGU_GROUPS
    t = SGU_CHUNK

    def body(z_ref, duv_ref, g_ref, b_ref, w_ref, bias_ref, dz_ref, dg_ref, db_ref, dw_ref, dbias_ref, v_s, vgrad_s, dvhat_s):
        @pl.when(pl.program_id(0) == 0)
        def _():
            for ref in (dg_ref, db_ref, dw_ref, dbias_ref):
                ref[...] = jnp.zeros_like(ref)

        def accumulate(ref, val):
            ref[...] += val

        mean, rstd = _sgu_row_stats(z_ref, width, gd, v_s, _gelu_and_grad, vgrad_s)
        sum_dvhat = sum_dvhat_vhat = None
        for g in range(SGU_GROUPS):
            cols = slice(g * gd, (g + 1) * gd)
            vhat = (v_s[:, cols] - mean) * rstd
            vln = (vhat * g_ref[:, cols] + b_ref[:, cols]).astype(BF16)
            wc = _tril(w_ref[g]).astype(BF16)
            mixed = jnp.dot(wc, vln, preferred_element_type=F32) + bias_ref[:, cols]
            u, u_grad = _gelu_and_grad(z_ref[:, cols])
            duv = duv_ref[:, cols]
            dz_ref[:, cols] = (duv * mixed * u_grad).astype(dz_ref.dtype)
            dmixed = duv * u
            dmb = dmixed.astype(BF16)
            dvln = lax.dot_general(wc, dmb, TN, preferred_element_type=F32)
            accumulate(dw_ref.at[g], _tril(lax.dot_general(dmb, vln, NT, preferred_element_type=F32)))
            accumulate(dbias_ref.at[:, g * LANE:(g + 1) * LANE],
                       jnp.broadcast_to(jnp.sum(dmixed, axis=1, keepdims=True), (t, LANE)))
            accumulate(dg_ref.at[:, cols], jnp.sum(dvln * vhat, axis=0, keepdims=True))
            accumulate(db_ref.at[:, cols], jnp.sum(dvln, axis=0, keepdims=True))
            dvhat = dvln * g_ref[:, cols]
            dvhat_s[:, cols] = dvhat
            parts = jnp.sum(dvhat, axis=1, keepdims=True), jnp.sum(dvhat * vhat, axis=1, keepdims=True)
            sum_dvhat = parts[0] if sum_dvhat is None else sum_dvhat + parts[0]
            sum_dvhat_vhat = parts[1] if sum_dvhat_vhat is None else sum_dvhat_vhat + parts[1]
        mean_dvhat, mean_dvhat_vhat = sum_dvhat * (1.0 / width), sum_dvhat_vhat * (1.0 / width)
        for g in range(SGU_GROUPS):
            cols = slice(g * gd, (g + 1) * gd)
            vhat = (v_s[:, cols] - mean) * rstd
            dv0 = rstd * (dvhat_s[:, cols] - mean_dvhat - vhat * mean_dvhat_vhat)
            dz_ref[:, width + g * gd:width + (g + 1) * gd] = (dv0 * vgrad_s[:, cols]).astype(dz_ref.dtype)

    vec = pl.BlockSpec((1, width), lambda i: (0, 0))
    return pl.pallas_call(
        body, name="sgu_bwd", grid=(s_len // t,),
        in_specs=[pl.BlockSpec((t, two_w), lambda i: (i, 0)), pl.BlockSpec((t, width), lambda i: (i, 0)), vec, vec,
                  pl.BlockSpec(w_sp.shape, lambda i: (0, 0, 0)), pl.BlockSpec((t, width), lambda i: (0, 0))],
        out_specs=[pl.BlockSpec((t, two_w), lambda i: (i, 0)), vec, vec,
                   pl.BlockSpec(w_sp.shape, lambda i: (0, 0, 0)), pl.BlockSpec((t, SGU_GROUPS * LANE), lambda i: (0, 0))],
        out_shape=[jax.ShapeDtypeStruct((s_len, two_w), BF16), jax.ShapeDtypeStruct((1, width), F32),
                   jax.ShapeDtypeStruct((1, width), F32), jax.ShapeDtypeStruct(w_sp.shape, F32),
                   jax.ShapeDtypeStruct((t, SGU_GROUPS * LANE), F32)],
        scratch_shapes=[pltpu.VMEM((t, width), F32)] * 3,
        compiler_params=_params(("arbitrary",)),
    )(zpre, duv, ln_g, ln_b, w_sp, bias_full)


def _rope_tables(positions):
    inv_freq = ROPE_THETA ** (-jnp.arange(0, ROPE, 2, dtype=F32) / ROPE)
    ang = positions.astype(F32)[:, None] * inv_freq
    cos, sin = jnp.cos(ang), jnp.sin(ang)
    z32, z64 = jnp.zeros_like(cos), jnp.zeros((cos.shape[0], LANE - ROPE), F32)
    return (jnp.concatenate([cos, cos, z64], axis=1), jnp.concatenate([z32, sin, z64], axis=1),
            jnp.concatenate([-sin, z32, z64], axis=1))


def _residual(g_next):
    if g_next is None:
        return dict(out_dtypes=(F32,), epilogue=lambda acc, res: (acc + res,))

    def epilogue(acc, res, gv):
        x_new = acc + res
        return x_new, _rms_fwd(x_new, gv)

    return dict(out_dtypes=(F32, BF16), epilogue=epilogue, consts=[g_next])


def _ffn_fwd(x, h2, w_up3, w_down3, g_next):
    def sq_relu(acc):
        r = jnp.maximum(acc, 0.0)
        return r * r, 2.0 * r

    r, r_grad = _mm_stacked("ffn_up", h2, w_up3, "col", (BF16, BF16), sq_relu)
    x_out, *h_next = _mm_stacked("ffn_down", r, w_down3, "row", extras=[x], **_residual(g_next))
    return x_out, (x, h2, r, r_grad), (h_next[0] if h_next else None)


def _ffn_bwd(dx, dxb, saved, g, w_up3, w_down3):
    x, h2, r, r_grad = saved
    da = _mm_stacked_nt("ffn_down_dx", dxb, w_down3, "row", (BF16,),
                        lambda acc, rg: (acc * rg.astype(F32),), [r_grad], tm=2 * MM_TILE)[0]
    g_down = _mm_tn_stacked("ffn_down_dw", r, dxb, w_down3.shape, "row")
    dx, dxb, dg = _mm_stacked_nt("ffn_up_dx", da, w_up3, "col", **_norm_bwd(x, g, dx))
    g_up = _mm_tn_stacked("ffn_up_dw", h2, da, w_up3.shape, "col")
    return dx, dxb, dg, g_up, g_down


def _norm_bwd(x, g, dres):
    def epilogue(dh, xv, rv, gv):
        dxv, dg = _rms_bwd(dh, xv, gv)
        return dxv + rv, dxv + rv, dg

    return dict(out_dtypes=(F32, BF16), epilogue=epilogue, extras=[x, dres], consts=[g], sums=[g.shape], tm=MM_TILE // 2)


def _dot(a, b, dims=NN):
    return lax.dot_general(a.astype(BF16), b.astype(BF16), dims, preferred_element_type=F32)


def _mla_fwd(x, g, wdkv, q_norm, kv_norm, wq, wkv, wo, tables, g_next):
    d = x.shape[1]

    def project(xv, cs, s1, s2, gv, wdkv_v, qg, kg, wq_v, wkv_v):
        h = _rms_fwd(xv, gv).astype(BF16)
        lv = _dot(h, wdkv_v)
        cqn = _rms_fwd(lv[:, :Q_RANK], qg).astype(BF16)
        ckvn = _rms_fwd(lv[:, Q_RANK:Q_RANK + KV_RANK], kg).astype(BF16)
        return (h, lv, cqn, ckvn, _rope(lv[:, Q_RANK + KV_RANK:], cs, s1, s2), _dot(cqn, wq_v), _dot(ckvn, wkv_v))

    h, lat, cqn, ckvn, krb, q, kvb = _rowwise(
        "mla_project", project, [x, *tables], [g, wdkv, q_norm, kv_norm, wq, wkv],
        [(d, BF16), (LAT_PAD, F32), (Q_RANK, BF16), (KV_RANK, BF16), (LANE, BF16), (wq.shape[1], F32), (wkv.shape[1], BF16)],
        tr=512)
    ob, lse, qb = _flash_fwd(q, kvb, krb, tables)
    x_mid, h2 = _mm("mla_o", ob, wo, extras=[x], **_residual(g_next))
    return x_mid, (x, h, lat, cqn, ckvn, krb, qb, kvb, ob, lse), h2


def _mla_bwd(dx, dxb, saved, g, wdkv, q_norm, kv_norm, wq, wkv, wo, tables):
    x, h, lat, cqn, ckvn, krb, qb, kvb, ob, lse = saved
    s_len = x.shape[0]
    def with_delta(do, ov):
        prod = do * ov.astype(F32)
        lane = lax.broadcasted_iota(jnp.int32, (do.shape[0], LANE), 1)
        by_lane = None
        for hd in range(HEADS):
            total = jnp.broadcast_to(jnp.sum(prod[:, hd * VHEAD:(hd + 1) * VHEAD], axis=1, keepdims=True), lane.shape)
            by_lane = jnp.where(lane == hd, total, 0.0 if by_lane is None else by_lane)
        return do, by_lane

    dob, delta_lanes = _mm("mla_o_dx", dxb, wo, (BF16,), with_delta, [ob], nt=True, tm=MM_TILE // 2, narrow=[(LANE, F32)])
    g_wo = _mm_tn("mla_o_dw", ob, dxb)[0]
    delta_row = delta_lanes[:, :HEADS].T.reshape(HEADS, 1, s_len)
    dq, dkvb, dkr = _flash_bwd(qb, kvb, krb, dob, lse, delta_row)

    def project_bwd(dq_f, dkv_v, dkr_v, lv, cqn_v, ckvn_v, h_v, xv, rv, cs, s1, s2, gv, qg, kg, wq_v, wkv_v, wdkv_v):
        parts = []
        for hd in range(HEADS):
            parts += [dq_f[:, hd * HEAD_PAD:hd * HEAD_PAD + NOPE],
                      _rope_t(dq_f[:, hd * HEAD_PAD + NOPE:(hd + 1) * HEAD_PAD], cs, s1, s2)]
        dq_v = jnp.concatenate(parts, axis=1).astype(BF16)
        dcq, dqg = _rms_bwd(_dot(dq_v, wq_v, NT), lv[:, :Q_RANK], qg)
        dckv, dkg = _rms_bwd(_dot(dkv_v, wkv_v, NT), lv[:, Q_RANK:Q_RANK + KV_RANK], kg)
        dkr_sum = dkr_v[:, :LANE]
        for hd in range(1, HEADS):
            dkr_sum = dkr_sum + dkr_v[:, hd * LANE:(hd + 1) * LANE]
        dlat = jnp.concatenate([dcq, dckv, _rope_t(dkr_sum, cs, s1, s2)], axis=1).astype(BF16)
        dxv, dg = _rms_bwd(_dot(dlat, wdkv_v, NT), xv, gv)
        return (dxv + rv, dxv + rv, _dot(cqn_v, dq_v, TN), _dot(ckvn_v, dkv_v, TN), _dot(h_v, dlat, TN), dqg, dkg, dg)

    dx, dxb, g_wq, g_wkv, g_wdkv, g_qn, g_kvn, dg = _rowwise(
        "mla_project_bwd", project_bwd, [dq, dkvb, dkr, lat, cqn, ckvn, h, x, dx, *tables],
        [g, q_norm, kv_norm, wq, wkv, wdkv], [(x.shape[1], F32), (x.shape[1], BF16)],
        [wq.shape, wkv.shape, wdkv.shape, q_norm.shape, kv_norm.shape, g.shape], tr=256)
    return dx, dxb, dg, g_wdkv, g_qn, g_kvn, g_wq, g_wkv, g_wo


def _sgu_layer_fwd(x, h, w_in3, ln_g, ln_b, w_sp, bias_full, w_out3, g_next):
    zpre = _mm_stacked("sgu_in", h, w_in3, "col")[0]
    uv = _sgu_fwd(zpre, ln_g, ln_b, w_sp, bias_full)
    x_mid, h2 = _mm_stacked("sgu_out", uv, w_out3, "row", extras=[x], **_residual(g_next))
    return x_mid, (x, h, zpre, uv), h2


def _sgu_layer_bwd(dx, dxb, saved, g, w_in3, ln_g, ln_b, w_sp, bias_full, w_out3):
    x, h, zpre, uv = saved
    duv = _mm_stacked_nt("sgu_out_dx", dxb, w_out3, "row")[0]
    g_out = _mm_tn_stacked("sgu_out_dw", uv, dxb, w_out3.shape, "row")
    dz, g_lng, g_lnb, g_wsp, g_bias = _sgu_bwd(zpre, duv, ln_g, ln_b, w_sp, bias_full)
    g_in = _mm_tn_stacked("sgu_in_dw", h, dz, w_in3.shape, "col")
    dx, dxb, dg = _mm_stacked_nt("sgu_in_dx", dz, w_in3, "col", **_norm_bwd(x, g, dx))
    return dx, dxb, dg, g_in, g_out, g_lng, g_lnb, g_wsp, g_bias


def _loss_head(x, target, g):
    d = x.shape[1]

    def fn(xv, tv, gv):
        err = _rms_fwd(xv, gv) - tv
        dxv, dg = _rms_bwd(err * (1.0 / d), xv, gv)
        return dxv, dxv, dg, jnp.sum(err * err, axis=0, keepdims=True)

    return _rowwise("loss_head", fn, [x, target], [g], [(d, F32), (d, BF16)], [g.shape, g.shape])


def _mixer_weights(i, stacks):
    by_rows = lambda a: a.reshape(N_SHARDS * a.shape[1], a.shape[2])
    by_cols = lambda a: a.transpose(1, 0, 2).reshape(a.shape[1], N_SHARDS * a.shape[2])
    if i % 2:
        w_in3, w_out3, ln_g, ln_b = stacks
        return w_in3, ln_g.reshape(1, -1), ln_b.reshape(1, -1), w_out3
    wdkv = by_rows(stacks[0])
    wdkv = jnp.pad(wdkv, ((0, 0), (0, LAT_PAD - wdkv.shape[1])))
    wq = jnp.pad(by_cols(stacks[1]).reshape(Q_RANK, HEADS, QK_HEAD), ((0, 0), (0, 0), (0, HEAD_PAD - QK_HEAD)))
    return wdkv, wq.reshape(Q_RANK, HEADS * HEAD_PAD), by_cols(stacks[2]), by_rows(stacks[3])


def _local_step(x, positions, target, norm_mix, norm_ffn, final_norm, q_norm, kv_norm, w_sp, b_sp, mixers, ffn, reducer):
    tables = _rope_tables(positions)
    gd = mixers[1][2].size // SGU_GROUPS
    bias_full = [jnp.repeat(b_sp[j].T, gd, axis=1) for j in range(DEPTH // 2)]
    saved, mla, sgu = [], [None] * (DEPTH // 2), [None] * (DEPTH // 2)
    h = None
    for i in range(DEPTH):
        j = i // 2
        x, *stacks = lax.optimization_barrier((x, *mixers[i]))
        if i % 2 == 0:
            wdkv, wq, wkv, wo = mla[j] = _mixer_weights(i, stacks)
            x, s_mix, h2 = _mla_fwd(x, norm_mix[i:i + 1], wdkv, q_norm[j:j + 1], kv_norm[j:j + 1], wq, wkv, wo, tables,
                                    norm_ffn[i:i + 1])
        else:
            w_in3, ln_g, ln_b, w_out3 = sgu[j] = _mixer_weights(i, stacks)
            x, s_mix, h2 = _sgu_layer_fwd(x, h, w_in3, ln_g, ln_b, w_sp[j], bias_full[j], w_out3, norm_ffn[i:i + 1])
        next_is_sgu = i + 1 < DEPTH and (i + 1) % 2 == 1
        x, s_ffn, h = _ffn_fwd(x, h2, *ffn[i], norm_mix[i + 1:i + 2] if next_is_sgu else None)
        saved.append((s_mix, s_ffn))
    dx, dxb, g_final, sq_cols = _loss_head(x, target, final_norm[None, :])
    loss = 0.5 * jnp.sum(sq_cols) / x.shape[1]

    def pair(g):
        return g, g.astype(BF16)

    g_mix, g_ffn = [None] * DEPTH, [None] * DEPTH
    mla_g, sgu_g = [None] * (DEPTH // 2), [None] * (DEPTH // 2)
    for i in reversed(range(DEPTH)):
        j = i // 2
        s_mix, s_ffn = saved[i]
        dx, dxb, g_ffn[i], g_up, g_down = _ffn_bwd(dx, dxb, s_ffn, norm_ffn[i:i + 1], *ffn[i])
        dxb = reducer.add(f"ffn{i}", i, {"ffn_w_up": g_up, "ffn_w_down": g_down}, dxb)
        dxb = reducer.phase_end(dxb)
        if i % 2 == 0:
            wdkv, wq, wkv, wo = mla[j]
            dx, dxb, g_mix[i], g_wdkv, g_qn, g_kvn, g_wq, g_wkv, g_wo = _mla_bwd(
                dx, dxb, s_mix, norm_mix[i:i + 1], wdkv, q_norm[j:j + 1], kv_norm[j:j + 1], wq, wkv, wo, tables)
            mla_g[j] = (g_qn, g_kvn)
            g_wq = g_wq.reshape(Q_RANK, HEADS, HEAD_PAD)[..., :QK_HEAD].reshape(Q_RANK, N_SHARDS, -1)
            dxb = reducer.add(f"mla{j}", j, {
                "mla_w_dkv": pair(g_wdkv[:, :Q_RANK + KV_RANK + ROPE].reshape(N_SHARDS, -1, Q_RANK + KV_RANK + ROPE)),
                "mla_w_uq": pair(g_wq.transpose(1, 0, 2)),
                "mla_w_ukv": pair(g_wkv.reshape(KV_RANK, N_SHARDS, -1).transpose(1, 0, 2)),
                "mla_w_o": pair(g_wo.reshape(N_SHARDS, -1, g_wo.shape[1]))}, dxb)
        else:
            w_in3, ln_g, ln_b, w_out3 = sgu[j]
            dx, dxb, g_mix[i], g_in, g_out, g_lng, g_lnb, g_wsp, g_bias = _sgu_layer_bwd(
                dx, dxb, s_mix, norm_mix[i:i + 1], w_in3, ln_g, ln_b, w_sp[j], bias_full[j], w_out3)
            sgu_g[j] = (g_wsp, g_bias.reshape(SGU_CHUNK, SGU_GROUPS, LANE)[:, :, 0].T)
            dxb = reducer.add(f"sgu{j}", j, {"sgu_w_in": g_in, "sgu_w_out": g_out,
                                             "sgu_ln_g": pair(g_lng.reshape(N_SHARDS, -1, LANE)),
                                             "sgu_ln_b": pair(g_lnb.reshape(N_SHARDS, -1, LANE))}, dxb)
        dxb = reducer.phase_end(dxb)
    small = dict(
        norm_mix=jnp.concatenate(g_mix, axis=0), norm_ffn=jnp.concatenate(g_ffn, axis=0), final_norm=g_final[0],
        q_norm=jnp.concatenate([m[0] for m in mla_g], axis=0), kv_norm=jnp.concatenate([m[1] for m in mla_g], axis=0),
        w_sp=jnp.stack([s[0] for s in sgu_g]), b_sp=jnp.stack([s[1] for s in sgu_g]))
    return loss, dx, small


HBM_SPEC = pl.BlockSpec(memory_space=pltpu.HBM)


def _place():
    x, y, c = lax.axis_index("x"), lax.axis_index("y"), lax.axis_index("c")
    return x, y, c, [(1 - x, y), (x, 1 - y), (1 - x, 1 - y)]


def _remote(src, dst, send_sems, recv_sems, k, to):
    return pltpu.make_async_remote_copy(src_ref=src, dst_ref=dst, send_sem=send_sems.at[k], recv_sem=recv_sems.at[k],
                                        device_id=to, device_id_type=MESH)


def _gather_layer(tag, shards):
    n = len(shards)
    split = [s.shape[0] >= 16 for s in shards]

    def body(*refs):
        ins, outs = refs[:n], refs[n:2 * n]
        send_sems, recv_sems, local_sems = refs[2 * n:]
        x, y, c, chips = _place()
        mine = 2 * x + y
        barrier = pltpu.get_barrier_semaphore()
        peers = [(x, y, 1 - c)] + [(*chip, c) for chip in chips]
        for peer in peers:
            pl.semaphore_signal(barrier, inc=1, device_id=peer, device_id_type=MESH)
        pl.semaphore_wait(barrier, len(peers))

        def rows(t, half):
            hr = shards[t].shape[0] // 2
            return pl.ds(half * hr, hr) if split[t] else pl.ds(0, shards[t].shape[0])

        local, sent = [], []
        for t in range(n):
            local.append(pltpu.make_async_copy(ins[t], outs[t].at[mine], local_sems.at[t]))
            local[-1].start()
            for j, chip in enumerate(chips):
                cp = _remote(ins[t].at[rows(t, c)], outs[t].at[mine, rows(t, c)], send_sems, recv_sems, 3 * t + j, (*chip, c))
                cp.start()
                sent.append(cp)
        for j, chip in enumerate(chips):
            theirs = 2 * chip[0] + chip[1]
            for t in range(n):
                piece = outs[t].at[theirs, rows(t, c)]
                _remote(piece, piece, send_sems, recv_sems, 3 * t + j, (x, y, c)).wait_recv()
                if split[t]:
                    cp = _remote(piece, piece, send_sems, recv_sems, 3 * n + 3 * t + j, (x, y, 1 - c))
                    cp.start()
                    sent.append(cp)
        for j, chip in enumerate(chips):
            theirs = 2 * chip[0] + chip[1]
            for t in range(n):
                if split[t]:
                    piece = outs[t].at[theirs, rows(t, 1 - c)]
                    _remote(piece, piece, send_sems, recv_sems, 3 * n + 3 * t + j, (x, y, c)).wait_recv()
        for cp in sent:
            cp.wait_send()
        for cp in local:
            cp.wait()

    return pl.kernel(
        body, name=f"gather_{tag}", mesh=plsc.ScalarSubcoreMesh(axis_name="sequencer", num_cores=1),
        out_type=[jax.ShapeDtypeStruct((N_SHARDS, *s.shape), s.dtype) for s in shards],
        scratch_types=[pltpu.SemaphoreType.DMA((6 * n,)), pltpu.SemaphoreType.DMA((6 * n,)), pltpu.SemaphoreType.DMA((n,))],
        compiler_params=pltpu.CompilerParams(collective_id=ID_GATHER),
    )(*shards)


SEQUENCER = dict(axis_name="sequencer", num_cores=1)
ID_GATHER, ID_SHARE = 0, 1
MIN_SPLIT_ROWS = 16


def _handshake(peers):
    barrier = pltpu.get_barrier_semaphore()
    for peer in peers:
        pl.semaphore_signal(barrier, inc=1, device_id=peer, device_id_type=MESH)
    pl.semaphore_wait(barrier, len(peers))


def _half_rows(rows, half):
    return pl.ds(half * (rows // 2), rows // 2) if rows >= MIN_SPLIT_ROWS else pl.ds(0, rows)


SEM_SPEC = pl.BlockSpec(memory_space=pltpu.SEMAPHORE)
DATAFLOW = pltpu.SideEffectType.DATAFLOW_SIDE_EFFECTING


def _exchange_copies(shapes, stacks, lands, send_sems, recv_sems):
    x, y, c, chips = _place()
    mine = 2 * x + y
    copies = []
    for t, shape in enumerate(shapes):
        r = shape[1]
        copies.append(_remote(stacks[t].at[mine, _half_rows(r, 1 - c)], lands[t].at[0], send_sems, recv_sems, 7 * t, (x, y, 1 - c)))
        for j, chip in enumerate(chips):
            theirs = 2 * chip[0] + chip[1]
            copies.append(_remote(stacks[t].at[theirs, _half_rows(r, c)], lands[t].at[1 + j], send_sems, recv_sems,
                                  7 * t + 1 + j, (*chip, c)))
            copies.append(_remote(stacks[t].at[theirs, _half_rows(r, 1 - c)], lands[t].at[4 + j], send_sems, recv_sems,
                                  7 * t + 4 + j, (*chip, 1 - c)))
    return copies


def _exchange_start(tag, stacks, carry):
    n = len(stacks)
    shapes = [s.shape for s in stacks]
    lands = [lax.empty((7, s.shape[1] // 2 if s.shape[1] >= MIN_SPLIT_ROWS else s.shape[1], s.shape[2]), s.dtype) for s in stacks]

    def body(*refs):
        for cp in _exchange_copies(shapes, refs[:n], refs[n:2 * n], refs[2 * n + 1], refs[2 * n + 2]):
            cp.start()

    through = (*stacks, *lands, carry)
    out = pl.pallas_call(
        body, name=f"reduce_exchange_start_{tag}",
        out_shape=(pltpu.SemaphoreType.DMA((7 * n,)), pltpu.SemaphoreType.DMA((7 * n,)),
                   *[pltpu.HBM(a.shape, a.dtype) for a in through]),
        in_specs=[HBM_SPEC] * (2 * n + 1),
        out_specs=(SEM_SPEC, SEM_SPEC, *[HBM_SPEC] * (2 * n + 1)),
        input_output_aliases={t: 2 + t for t in range(2 * n + 1)},
        compiler_params=pltpu.CompilerParams(has_side_effects=DATAFLOW),
    )(*[pltpu.with_memory_space_constraint(a, pltpu.HBM) for a in through])
    return out[0], out[1], out[2:2 + n], out[2 + n:2 + 2 * n], out[-1]


def _exchange_wait(tag, send_sems, recv_sems, stacks, lands, after):
    n = len(stacks)
    shapes = [s.shape for s in stacks]

    def body(*refs):
        for cp in _exchange_copies(shapes, refs[:n], refs[n:2 * n], refs[2 * n], refs[2 * n + 1]):
            cp.wait()

    out = pl.pallas_call(
        body, name=f"reduce_exchange_wait_{tag}",
        out_shape=tuple(pltpu.HBM(a.shape, a.dtype) for a in (*stacks, *lands)),
        in_specs=[HBM_SPEC] * (2 * n) + [SEM_SPEC, SEM_SPEC, pl.BlockSpec(memory_space=pl.ANY)],
        out_specs=tuple([HBM_SPEC] * (2 * n)),
        input_output_aliases={t: t for t in range(2 * n)},
        compiler_params=pltpu.CompilerParams(has_side_effects=DATAFLOW),
    )(*stacks, *lands, send_sems, recv_sems, after)
    return out[:n], out[n:]


def _share_halves(tag, halves):
    n = len(halves)

    def body(*refs):
        ins, outs, send_sems, recv_sems = refs[:n], refs[n:2 * n], refs[2 * n], refs[2 * n + 1]
        x, y, c, _ = _place()
        _handshake([(x, y, 1 - c)])
        sent = [_remote(ins[t], outs[t], send_sems, recv_sems, t, (x, y, 1 - c)) for t in range(n)]
        for cp in sent:
            cp.start()
        for cp in sent:
            cp.wait()

    return pl.kernel(
        body, name=f"reduce_share_{tag}", mesh=plsc.ScalarSubcoreMesh(**SEQUENCER),
        out_type=[jax.ShapeDtypeStruct(h.shape, h.dtype) for h in halves],
        scratch_types=[pltpu.SemaphoreType.DMA((n,)), pltpu.SemaphoreType.DMA((n,))],
        compiler_params=pltpu.CompilerParams(collective_id=ID_SHARE),
    )(*halves)


def _all_reduce_small(part):
    rows = part.shape[0]
    half = rows // 2

    def body(p_ref, out_ref, sib_buf, chip_sums, send_sems, recv_sems):
        x, y, c, chips = _place()
        mine = 2 * x + y
        my_rows = pl.ds(pl.multiple_of(c * half, 8), half)
        swap = _remote(p_ref, sib_buf, send_sems, recv_sems, 0, (x, y, 1 - c))
        swap.start()
        swap.wait()
        chip_sums[mine] = p_ref[...] + sib_buf[...]
        sent = [_remote(chip_sums.at[mine, my_rows], chip_sums.at[mine, my_rows], send_sems, recv_sems, 1 + j, (*chip, c))
                for j, chip in enumerate(chips)]
        for cp in sent:
            cp.start()
        for j, chip in enumerate(chips):
            sent[j].wait_send()
            theirs = chip_sums.at[2 * chip[0] + chip[1], my_rows]
            _remote(theirs, theirs, send_sems, recv_sems, 1 + j, (x, y, c)).wait_recv()
        out_ref[my_rows, :] = ((chip_sums[0, my_rows, :] + chip_sums[1, my_rows, :]) + chip_sums[2, my_rows, :]) + chip_sums[3, my_rows, :]
        share = _remote(out_ref.at[my_rows], out_ref.at[my_rows], send_sems, recv_sems, 4, (x, y, 1 - c))
        share.start()
        share.wait_send()
        other = out_ref.at[pl.ds(pl.multiple_of((1 - c) * half, 8), half)]
        _remote(other, other, send_sems, recv_sems, 4, (x, y, c)).wait_recv()

    vmem = pl.BlockSpec(memory_space=pltpu.VMEM)
    return pl.pallas_call(
        body, name="all_reduce_small", in_specs=[vmem], out_specs=vmem, out_shape=jax.ShapeDtypeStruct(part.shape, F32),
        scratch_shapes=[pltpu.VMEM((rows, LANE), F32), pltpu.VMEM((N_SHARDS, rows, LANE), F32),
                        pltpu.SemaphoreType.DMA((5,)), pltpu.SemaphoreType.DMA((5,))],
        compiler_params=pltpu.CompilerParams(vmem_limit_bytes=VMEM_LIMIT_BYTES),
    )(part)


def _sum_partials(g3, others, sel):
    _, rows, c = others.shape
    whole = g3.shape[1] == rows
    tr = _tile(rows, UPDATE_ROWS)
    nb = rows // tr

    def body(sel_ref, g_ref, slots_ref, out_ref):
        same = g_ref[...].astype(F32)
        for k in (1, 2, 3):
            same = same + slots_ref[k].astype(F32)
        other = slots_ref[0].astype(F32)
        for k in (4, 5, 6):
            other = other + slots_ref[k].astype(F32)
        out_ref[...] = same + other

    return pl.pallas_call(
        body, name="reduce_sum_partials",
        grid_spec=pltpu.PrefetchScalarGridSpec(
            num_scalar_prefetch=1, grid=(nb,),
            in_specs=[pl.BlockSpec((None, tr, c), lambda i, sr: (sr[0], (0 if whole else sr[1] * nb) + i, 0)),
                      pl.BlockSpec((7, tr, c), lambda i, sr: (0, i, 0))],
            out_specs=pl.BlockSpec((tr, c), lambda i, sr: (i, 0))),
        out_shape=jax.ShapeDtypeStruct((rows, c), F32),
        compiler_params=_params(("parallel",)),
    )(sel, g3, others)


def _adamw_math(w, g, m, v):
    nm = ADAM_B1 * m + (1.0 - ADAM_B1) * g
    nv = ADAM_B2 * v + (1.0 - ADAM_B2) * (g * g)
    m_hat = nm / (1.0 - ADAM_B1 ** ADAM_STEP)
    v_hat = nv / (1.0 - ADAM_B2 ** ADAM_STEP)
    return -ADAM_LR * (m_hat / (jnp.sqrt(v_hat) + ADAM_EPS) + ADAM_WD * w), nm, nv


def _adamw_layer(layer, w, m, v, g_mine, g_sibling, sel, prev):
    lyr, r, c = w.shape
    rows = g_mine.shape[0]
    halves = r // rows
    tr = _tile(rows, 512)
    nb = rows // tr
    n_g = 1 if g_sibling is None else 2

    def body(sel_ref, w_ref, m_ref, v_ref, *rest):
        g = rest[0][...]
        if n_g == 2:
            g = jnp.where(pl.program_id(0) == sel_ref[1], g, rest[1][...])
        outs = rest[n_g + (0 if prev is None else 4):]
        d, nm, nv = _adamw_math(w_ref[...], g, m_ref[...], v_ref[...])
        for ref, val in zip(outs, (g, d, nm, nv)):
            ref[...] = val

    full = pl.BlockSpec((None, tr, c), lambda h, i, sr: (layer, h * nb + i, 0))
    part = pl.BlockSpec((tr, c), lambda h, i, sr: (i, 0))
    n_in = 4 + n_g
    return pl.pallas_call(
        body, name="adamw_layer",
        grid_spec=pltpu.PrefetchScalarGridSpec(
            num_scalar_prefetch=1, grid=(halves, nb),
            in_specs=[full] * 3 + [part] * n_g + ([] if prev is None else [pl.BlockSpec(memory_space=pl.ANY)] * 4),
            out_specs=[full] * 4),
        out_shape=[jax.ShapeDtypeStruct(w.shape, F32)] * 4,
        input_output_aliases={} if prev is None else {n_in + k: k for k in range(4)},
        compiler_params=_params(("parallel", "parallel")),
    )(sel, w, m, v, g_mine, *([] if g_sibling is None else [g_sibling]), *([] if prev is None else prev))


def _adamw(w, g, m, v):
    lyr, r, c = w.shape
    tr = _tile(r, 256)

    def body(w_ref, g_ref, m_ref, v_ref, d_ref, nm_ref, nv_ref):
        d_ref[...], nm_ref[...], nv_ref[...] = _adamw_math(w_ref[...], g_ref[...], m_ref[...], v_ref[...])

    blk = pl.BlockSpec((None, tr, c), lambda l, i: (l, i, 0))
    return pl.pallas_call(
        body, name="adamw", grid=(lyr, r // tr), in_specs=[blk] * 4, out_specs=[blk] * 3,
        out_shape=[jax.ShapeDtypeStruct(w.shape, F32)] * 3,
        compiler_params=_params(("parallel", "parallel")),
    )(w, g, m, v)


SHARDED = ("mla_w_dkv", "mla_w_uq", "mla_w_ukv", "mla_w_o", "sgu_w_in", "sgu_ln_g", "sgu_ln_b", "sgu_w_out",
           "ffn_w_up", "ffn_w_down")
REPLICATED = ("norm_mix", "norm_ffn", "final_norm", "mla_q_norm", "mla_kv_norm", "sgu_w_spatial", "sgu_b_spatial")
WEIGHTS = ("norm_mix", "norm_ffn", "final_norm", "mla_w_dkv", "mla_q_norm", "mla_kv_norm", "mla_w_uq", "mla_w_ukv",
           "mla_w_o", "sgu_w_in", "sgu_ln_g", "sgu_ln_b", "sgu_w_spatial", "sgu_b_spatial", "sgu_w_out", "ffn_w_up",
           "ffn_w_down")


class _Reducer:
    def __init__(self, state, sel):
        self.state, self.sel = state, sel
        self.started, self.travelling, self.summed = [], [], []
        self.done = {}

    def add(self, tag, layer, grads, token):
        names = list(grads)
        token, *tied = lax.optimization_barrier((token, *[a for n in names for a in grads[n]]))
        f32s, bf16s = tied[0::2], tied[1::2]
        *flying, token = _exchange_start(tag, bf16s, token)
        self.started.append((tag, layer, names, f32s, flying))
        return token

    def phase_end(self, token):
        for tag, layer, names, f32s, flying in self.travelling:
            bf16s, received = _exchange_wait(tag, *flying, token)
            own = [g if g.shape[1] >= MIN_SPLIT_ROWS else gb for g, gb in zip(f32s, bf16s)]
            mine = [_sum_partials(g, got, self.sel) for g, got in zip(own, received)]
            token, *mine = lax.optimization_barrier((token, *mine))
            cut = [k for k, g in enumerate(own) if g.shape[1] >= MIN_SPLIT_ROWS]
            theirs = dict(zip(cut, _share_halves(tag, [mine[k] for k in cut])))
            self.summed.append((layer, names, mine, [theirs.get(k) for k in range(len(names))]))
        self.travelling, self.started = self.started, []
        return token

    def update(self, token):
        for layer, names, mine, theirs in self.summed:
            for name, g_mine, g_theirs in zip(names, mine, theirs):
                w, m, v = self.state[name]
                self.done[name] = _adamw_layer(layer, w, m, v, g_mine, g_theirs, self.sel, self.done.get(name))
                token = self.done[name][1]
        self.summed = []
        return token


def _as3d(name, a):
    return a.reshape(a.shape[0], -1, LANE) if name in ("sgu_ln_g", "sgu_ln_b") else a


def _pack(parts):
    flat = jnp.concatenate([p.reshape(-1) for p in parts])
    rows = -(-flat.shape[0] // (256 * LANE)) * 256
    return jnp.pad(flat, (0, rows * LANE - flat.shape[0])).reshape(rows, LANE)


def _unpack(packed, like):
    flat, out, at = packed.reshape(-1), [], 0
    for p in like:
        out.append(flat[at:at + p.size].reshape(p.shape))
        at += p.size
    return out


def kernel(x, positions, norm_mix, norm_ffn, final_norm, mla_w_dkv, mla_q_norm, mla_kv_norm, mla_w_uq, mla_w_ukv, mla_w_o, sgu_w_in, sgu_ln_g, sgu_ln_b, sgu_w_spatial, sgu_b_spatial, sgu_w_out, ffn_w_up, ffn_w_down, loss_target, m_norm_mix, m_norm_ffn, m_final_norm, m_mla_w_dkv, m_mla_q_norm, m_mla_kv_norm, m_mla_w_uq, m_mla_w_ukv, m_mla_w_o, m_sgu_w_in, m_sgu_ln_g, m_sgu_ln_b, m_sgu_w_spatial, m_sgu_b_spatial, m_sgu_w_out, m_ffn_w_up, m_ffn_w_down, v_norm_mix, v_norm_ffn, v_final_norm, v_mla_w_dkv, v_mla_q_norm, v_mla_kv_norm, v_mla_w_uq, v_mla_w_ukv, v_mla_w_o, v_sgu_w_in, v_sgu_ln_g, v_sgu_ln_b, v_sgu_w_spatial, v_sgu_b_spatial, v_sgu_w_out, v_ffn_w_up, v_ffn_w_down):
    given = dict(locals())
    w = {n: given[n] for n in WEIGHTS}
    mom = {n: given["m_" + n] for n in WEIGHTS}
    var = {n: given["v_" + n] for n in WEIGHTS}
    mixers, ffn, token = [], [], None
    for i in range(DEPTH):
        j = i // 2
        if i % 2 == 0:
            mixer = [w[n][j].astype(BF16) for n in ("mla_w_dkv", "mla_w_uq", "mla_w_ukv", "mla_w_o")]
        else:
            mixer = [sgu_w_in[j].astype(BF16), sgu_w_out[j].astype(BF16), sgu_ln_g[j].reshape(-1, LANE),
                     sgu_ln_b[j].reshape(-1, LANE)]
        for tag, shards, into in ((f"mixer{i}", mixer, mixers), (f"ffn{i}", [ffn_w_up[i].astype(BF16), ffn_w_down[i].astype(BF16)], ffn)):
            if token is None:
                token = shards[0]
            else:
                token, *shards = lax.optimization_barrier((token, *shards))
            into.append(_gather_layer(tag, shards))

    x_i, y_i, c_i = lax.axis_index("x"), lax.axis_index("y"), lax.axis_index("c")
    sel = jnp.stack([2 * x_i + y_i, c_i]).astype(jnp.int32)
    reducer = _Reducer({n: tuple(_as3d(n, d[n]) for d in (w, mom, var)) for n in SHARDED}, sel)
    loss, dx, small = _local_step(
        x[0], positions[0], loss_target[0], norm_mix, norm_ffn, final_norm, mla_q_norm, mla_kv_norm, sgu_w_spatial,
        sgu_b_spatial, mixers, ffn, reducer)
    loss = lax.psum(loss, ("x", "y", "c"))

    small_g = [small["norm_mix"], small["norm_ffn"], small["final_norm"], small["q_norm"], small["kv_norm"],
               small["w_sp"], small["b_sp"]]
    like = [w[n] for n in REPLICATED]
    g_small = _all_reduce_small(_pack(small_g))
    packed = [_pack([d[n] for n in REPLICATED])[None] for d in (w, mom, var)]
    upd_small = _adamw(packed[0], g_small[None], packed[1], packed[2])
    grads = dict(zip(REPLICATED, _unpack(g_small, like)))
    delta, new_m, new_v = ({n: a for n, a in zip(REPLICATED, _unpack(u[0], like))} for u in upd_small)

    reducer.phase_end(reducer.update(upd_small[0]))
    reducer.update(None)
    for n in SHARDED:
        grads[n], delta[n], new_m[n], new_v[n] = (a.reshape(w[n].shape) for a in reducer.done[n])

    return (loss, dx[None], *[grads[n] for n in WEIGHTS], *[delta[n] for n in WEIGHTS],
            *[new_m[n] for n in WEIGHTS], *[new_v[n] for n in WEIGHTS])
```

```python
import functools
import math

import jax
import jax.numpy as jnp
from jax import lax
from jax.experimental import pallas as pl
from jax.experimental.pallas import tpu as pltpu
from jax.experimental.pallas import tpu_sc as plsc

F32 = jnp.float32
BF16 = jnp.bfloat16
MESH = pl.DeviceIdType.MESH

DEPTH = 4
HEADS = 8
NOPE = 128
ROPE = 64
VHEAD = 128
QK_HEAD = NOPE + ROPE
Q_RANK = 256
KV_RANK = 128
HEAD_PAD = 256
LAT_PAD = 512
ROPE_THETA = 10000.0
SGU_CHUNK = 128
SGU_GROUPS = 8
NORM_EPS = 1e-6
LN_EPS = 1e-5
ADAM_LR, ADAM_B1, ADAM_B2, ADAM_EPS, ADAM_WD, ADAM_STEP = 0.001, 0.9, 0.999, 1e-08, 0.01, 10

N_SHARDS = 4
LANE = 128
VMEM_LIMIT_BYTES = 56 * 1024 * 1024
ATT_TILE = 512
MM_TILE = 1024
UPDATE_ROWS = 256
ATT_SCALE = QK_HEAD ** -0.5
LOG2_SCALE = ATT_SCALE * math.log2(math.e)

NN = (((1,), (0,)), ((), ()))
NT = (((1,), (1,)), ((), ()))
TN = (((0,), (0,)), ((), ()))


def _params(sem):
    return pltpu.CompilerParams(dimension_semantics=sem, vmem_limit_bytes=VMEM_LIMIT_BYTES)


def _tile(n, pref):
    t = min(n, pref)
    while n % t:
        t //= 2
    return t


def _matmul(name, a, b, a_spec, b_spec, dims, grid, tile, outs, extras=(), epilogue=None, sums=()):
    nk, ne, no = grid[2], len(extras), len(outs)
    b_specs = list(b_spec) if isinstance(b_spec, (list, tuple)) else [b_spec]
    nb = len(b_specs)

    def body(a_ref, *rest):
        b_refs, e_refs, o_refs = rest[:nb], rest[nb:nb + ne], rest[nb + ne:nb + ne + no]
        s_refs = rest[nb + ne + no:nb + ne + no + len(sums)]
        kw = a_ref.shape[1] // nb
        part = None
        for p, b_ref in enumerate(b_refs):
            a_tile = a_ref[...] if nb == 1 else a_ref[:, p * kw:(p + 1) * kw]
            d = lax.dot_general(a_tile.astype(BF16), b_ref[...].astype(BF16), dims, preferred_element_type=F32)
            part = d if part is None else part + d

        def finish(acc):
            vals = (acc,) if epilogue is None else epilogue(acc, *[e[...] for e in e_refs])
            for o_ref, v in zip(o_refs, vals):
                o_ref[...] = v.astype(o_ref.dtype)
            first = pl.program_id(0) == 0
            for s_ref, v in zip(s_refs, vals[no:]):
                @pl.when(first)
                def _():
                    s_ref[...] = v

                @pl.when(jnp.logical_not(first))
                def _():
                    s_ref[...] += v

        if nk == 1:
            finish(part)
            return
        acc_ref, k = rest[-1], pl.program_id(2)

        @pl.when(k == 0)
        def _():
            acc_ref[...] = part

        @pl.when(jnp.logical_and(k > 0, k < nk - 1))
        def _():
            acc_ref[...] += part

        @pl.when(k == nk - 1)
        def _():
            finish(acc_ref[...] + part)

    assert not sums or (grid[1] == 1 and nk == 1)
    return pl.pallas_call(
        body, name=name, grid=grid,
        in_specs=[a_spec] + b_specs + [s for _, s in extras],
        out_specs=[s for _, s in outs] + [pl.BlockSpec(s, lambda i, j, k: (0,) * len(s)) for s in sums],
        out_shape=[s for s, _ in outs] + [jax.ShapeDtypeStruct(s, F32) for s in sums],
        scratch_shapes=[pltpu.VMEM(tile, F32)] if nk > 1 else [],
        compiler_params=_params(("arbitrary" if sums else "parallel", "parallel", "arbitrary")),
    )(a, *[b] * nb, *[e for e, _ in extras])


def _epilogue_operands(extras, consts, o_spec):
    return [(e, o_spec) for e in extras] + [(c, pl.BlockSpec(c.shape, lambda i, j, k: (0, 0))) for c in consts]


def _mm(name, a, b, out_dtypes=(F32,), epilogue=None, extras=(), tm=MM_TILE, tn=MM_TILE, tk=MM_TILE, nt=False,
        consts=(), sums=(), narrow=()):
    m, kd = a.shape
    n = b.shape[0] if nt else b.shape[1]
    tm, tn, tk = _tile(m, tm), _tile(n, tn), _tile(kd, tk)
    o_spec = pl.BlockSpec((tm, tn), lambda i, j, k: (i, j))
    b_spec = pl.BlockSpec((tn, tk), lambda i, j, k: (j, k)) if nt else pl.BlockSpec((tk, tn), lambda i, j, k: (k, j))
    assert not narrow or n == tn
    outs = [(jax.ShapeDtypeStruct((m, n), d), o_spec) for d in out_dtypes]
    outs += [(jax.ShapeDtypeStruct((m, w), d), pl.BlockSpec((tm, w), lambda i, j, k: (i, 0))) for w, d in narrow]
    return _matmul(name, a, b, pl.BlockSpec((tm, tk), lambda i, j, k: (i, k)), b_spec, NT if nt else NN,
                   (m // tm, n // tn, kd // tk), (tm, tn), outs,
                   _epilogue_operands(extras, consts, o_spec), epilogue, sums)


def _mm_tn(name, a, b, out_dtypes=(F32,), tm=MM_TILE, tn=MM_TILE, tk=MM_TILE):
    s, m = a.shape
    n = b.shape[1]
    tm, tn, tk = _tile(m, tm), _tile(n, tn), _tile(s, tk)
    o_spec = pl.BlockSpec((tm, tn), lambda i, j, k: (i, j))
    return _matmul(name, a, b, pl.BlockSpec((tk, tm), lambda i, j, k: (k, i)),
                   pl.BlockSpec((tk, tn), lambda i, j, k: (k, j)), TN, (m // tm, n // tn, s // tk), (tm, tn),
                   [(jax.ShapeDtypeStruct((m, n), d), o_spec) for d in out_dtypes])


def _mm_stacked(name, a, w3, mode, out_dtypes=(F32,), epilogue=None, extras=(), tm=MM_TILE, tn=MM_TILE, tk=MM_TILE,
                consts=()):
    m, kd = a.shape
    _, r, c = w3.shape
    n = c if mode == "row" else N_SHARDS * c
    if mode == "row":
        tm, tn, tk = _tile(m, tm // 2), _tile(n, tn), kd
        b_spec = [pl.BlockSpec((None, r, tn), functools.partial(lambda i, j, k, p: (p, 0, j), p=p)) for p in range(N_SHARDS)]
    else:
        tm, tn, tk = _tile(m, tm), _tile(c, tn), _tile(kd, tk)
        per = c // tn
        b_spec = pl.BlockSpec((None, tk, tn), lambda i, j, k: (j // per, k, j % per))
    o_spec = pl.BlockSpec((tm, tn), lambda i, j, k: (i, j))
    return _matmul(name, a, w3, pl.BlockSpec((tm, tk), lambda i, j, k: (i, k)), b_spec, NN,
                   (m // tm, n // tn, kd // tk), (tm, tn),
                   [(jax.ShapeDtypeStruct((m, n), d), o_spec) for d in out_dtypes],
                   _epilogue_operands(extras, consts, o_spec), epilogue)


def _mm_stacked_nt(name, a, w3, mode, out_dtypes=(F32,), epilogue=None, extras=(), tm=MM_TILE, tn=MM_TILE, tk=MM_TILE,
                   consts=(), sums=()):
    m, nd = a.shape
    _, r, c = w3.shape
    kout = N_SHARDS * r if mode == "row" else r
    if mode == "row":
        tm, tn, tk = _tile(m, tm), _tile(r, tn), _tile(c, tk)
        per = r // tn
        b_spec = pl.BlockSpec((None, tn, tk), lambda i, j, k: (j // per, j % per, k))
    else:
        tm, tn, tk = _tile(m, tm // 2), _tile(r, tn), nd
        b_spec = [pl.BlockSpec((None, tn, c), functools.partial(lambda i, j, k, p: (p, j, 0), p=p)) for p in range(N_SHARDS)]
    o_spec = pl.BlockSpec((tm, tn), lambda i, j, k: (i, j))
    return _matmul(name, a, w3, pl.BlockSpec((tm, tk), lambda i, j, k: (i, k)), b_spec, NT,
                   (m // tm, kout // tn, nd // tk), (tm, tn),
                   [(jax.ShapeDtypeStruct((m, kout), d), o_spec) for d in out_dtypes],
                   _epilogue_operands(extras, consts, o_spec), epilogue, sums)


def _mm_tn_stacked(name, a, b, shape3, mode, tm=MM_TILE, tn=MM_TILE, tk=MM_TILE):
    s, m = a.shape
    n = b.shape[1]
    _, r, c = shape3
    tk, tn = s, tn // 2
    if mode == "row":
        tm, tn = _tile(r, tm), _tile(n, tn)
        per = r // tm
        o_spec = pl.BlockSpec((None, tm, tn), lambda i, j, k: (i // per, i % per, j))
    else:
        tm, tn = _tile(m, tm), _tile(c, tn)
        per = c // tn
        o_spec = pl.BlockSpec((None, tm, tn), lambda i, j, k: (j // per, i, j % per))
    outs = [(jax.ShapeDtypeStruct(shape3, F32), o_spec), (jax.ShapeDtypeStruct(shape3, BF16), o_spec)]
    return _matmul(name, a, b, pl.BlockSpec((tk, tm), lambda i, j, k: (k, i)),
                   pl.BlockSpec((tk, tn), lambda i, j, k: (k, j)), TN, (m // tm, n // tn, s // tk), (tm, tn),
                   outs, epilogue=lambda acc: (acc, acc))


def _rowwise(name, fn, rows, consts, out_rows, out_accs=(), tr=256):
    nr, nc, no = len(rows), len(consts), len(out_rows)
    n_rows = rows[0].shape[0]
    tr = _tile(n_rows, tr)

    def body(*refs):
        vals = fn(*[r[...] for r in refs[:nr + nc]])
        o_refs, a_refs = refs[nr + nc:nr + nc + no], refs[nr + nc + no:]
        for ref, v in zip(o_refs, vals[:no]):
            ref[...] = v.astype(ref.dtype)
        first = pl.program_id(0) == 0

        @pl.when(first)
        def _():
            for ref, v in zip(a_refs, vals[no:]):
                ref[...] = v

        @pl.when(jnp.logical_not(first))
        def _():
            for ref, v in zip(a_refs, vals[no:]):
                ref[...] += v

    def whole(shape):
        return pl.BlockSpec(shape, lambda i: (0,) * len(shape))

    return pl.pallas_call(
        body, name=name, grid=(n_rows // tr,),
        in_specs=[pl.BlockSpec((tr, a.shape[1]), lambda i: (i, 0)) for a in rows] + [whole(c.shape) for c in consts],
        out_specs=[pl.BlockSpec((tr, f), lambda i: (i, 0)) for f, _ in out_rows] + [whole(s) for s in out_accs],
        out_shape=[jax.ShapeDtypeStruct((n_rows, f), d) for f, d in out_rows]
        + [jax.ShapeDtypeStruct(s, F32) for s in out_accs],
        compiler_params=_params(("arbitrary",)),
    )(*rows, *consts)


def _rms_fwd(x, g):
    return x * lax.rsqrt(jnp.mean(x * x, axis=-1, keepdims=True) + NORM_EPS) * g


def _rms_bwd(dy, x, g):
    rstd = lax.rsqrt(jnp.mean(x * x, axis=-1, keepdims=True) + NORM_EPS)
    n = x * rstd
    dn = dy * g
    dx = rstd * (dn - n * jnp.mean(dn * n, axis=-1, keepdims=True))
    return dx, jnp.sum(dy * n, axis=0, keepdims=True)


def _rope(x, cs, s1, s2):
    return x * cs + pltpu.roll(x, 32, 1) * s1 + pltpu.roll(x, 96, 1) * s2


def _rope_t(dy, cs, s1, s2):
    return dy * cs + pltpu.roll(dy * s1, 96, 1) + pltpu.roll(dy * s2, 32, 1)


def _gelu(z):
    return 0.5 * z * (1.0 + lax.erf(z * (1.0 / math.sqrt(2.0))))


def _gelu_and_grad(z):
    cdf = 0.5 * (1.0 + lax.erf(z * (1.0 / math.sqrt(2.0))))
    return z * cdf, cdf + z * jnp.exp(-0.5 * z * z) * (1.0 / math.sqrt(2.0 * math.pi))


def _att_scores(q, kv, kr, masked, transposed):
    k = jnp.concatenate([kv[:, :NOPE], kr], axis=1)
    if transposed:
        s = lax.dot_general(k, q, NT, preferred_element_type=F32)
    else:
        s = lax.dot_general(q, k, NT, preferred_element_type=F32)
    if masked:
        r = lax.broadcasted_iota(jnp.int32, s.shape, 0)
        c = lax.broadcasted_iota(jnp.int32, s.shape, 1)
        s = jnp.where((r <= c) if transposed else (c <= r), s, -jnp.inf)
    return s, k


def _in_pairs(lo, hi, pair, single):
    n = hi - lo

    def body(p, carry):
        pair(lo + 2 * p, lo + 2 * p + 1)
        return carry

    lax.fori_loop(0, n // 2, body, 0)

    @pl.when(n % 2 == 1)
    def _():
        single(hi - 1)


def _causal_tiles(i, pair, single):
    @pl.when(i == 0)
    def _():
        single(i, True)

    @pl.when(i > 0)
    def _():
        _in_pairs(0, i - 1, lambda a, b: pair(a, b, False), lambda a: single(a, False))
        pair(i - 1, i, True)


def _flash_fwd(q, kvb, krb, tables):
    s_len = q.shape[0]
    t = ATT_TILE

    def body(q_ref, cs_ref, s1_ref, s2_ref, kv_ref, kr_ref, o_ref, lse_ref, qb_ref, m_s, l_s, acc_s):
        qi = pl.program_id(1)
        m_s[...] = jnp.full_like(m_s, -jnp.inf)
        l_s[...] = jnp.zeros_like(l_s)
        acc_s[...] = jnp.zeros_like(acc_s)
        qv = q_ref[...]
        q = jnp.concatenate([qv[:, :NOPE], _rope(qv[:, NOPE:], cs_ref[...], s1_ref[...], s2_ref[...])], axis=1)
        q = (q * LOG2_SCALE).astype(BF16)
        qb_ref[...] = q

        def scores(ki, masked):
            rows = pl.ds(pl.multiple_of(ki * t, t), t)
            kv = kv_ref[rows, :]
            return _att_scores(q, kv, kr_ref[rows, :], masked, False)[0], kv

        def update(s, kv):
            m_prev = m_s[...]
            m_new = jnp.maximum(m_prev, jnp.max(s, axis=1, keepdims=True))
            alpha = jnp.exp2(m_prev - m_new)
            p = jnp.exp2(s - jnp.tile(m_new, (1, t // LANE)))
            l_s[...] = alpha * l_s[...] + jnp.sum(p, axis=1, keepdims=True)
            acc_s[...] = alpha * acc_s[...] + jnp.dot(p.astype(BF16), kv[:, NOPE:], preferred_element_type=F32)
            m_s[...] = m_new

        def pair(k0, k1, masked):
            first, second = scores(k0, False), scores(k1, masked)
            update(*first)
            update(*second)

        _causal_tiles(qi, pair, lambda ki, masked: update(*scores(ki, masked)))
        o_ref[...] = (acc_s[...] / l_s[...]).astype(o_ref.dtype)
        lse_ref[...] = (m_s[...] + jnp.log2(l_s[...])).T[:1, :]

    table = pl.BlockSpec((t, LANE), lambda h, qi: (qi, 0))
    return pl.pallas_call(
        body, name="flash_fwd", grid=(HEADS, s_len // t),
        in_specs=[pl.BlockSpec((t, HEAD_PAD), lambda h, qi: (qi, h)), table, table, table,
                  pl.BlockSpec((s_len, HEAD_PAD), lambda h, qi: (0, h)),
                  pl.BlockSpec((s_len, LANE), lambda h, qi: (0, 0))],
        out_specs=[pl.BlockSpec((t, VHEAD), lambda h, qi: (qi, h)),
                   pl.BlockSpec((None, 1, t), lambda h, qi: (h, 0, qi)),
                   pl.BlockSpec((t, HEAD_PAD), lambda h, qi: (qi, h))],
        out_shape=[jax.ShapeDtypeStruct((s_len, HEADS * VHEAD), BF16),
                   jax.ShapeDtypeStruct((HEADS, 1, s_len), F32),
                   jax.ShapeDtypeStruct((s_len, HEADS * HEAD_PAD), BF16)],
        scratch_shapes=[pltpu.VMEM((t, LANE), F32), pltpu.VMEM((t, LANE), F32), pltpu.VMEM((t, VHEAD), F32)],
        compiler_params=_params(("parallel", "arbitrary")),
    )(q, *tables, kvb, krb)


def _flash_bwd(qb, kvb, krb, dob, lse_row, delta_row):
    s_len = qb.shape[0]
    t = ATT_TILE
    nq = s_len // t
    scale = QK_HEAD ** -0.5

    def body(q_ref, kv_ref, kr_ref, do_ref, lse_ref, dl_ref, dq_ref, dkv_ref, dkr_ref, dk_s, dv_s):
        ki = pl.program_id(1)

        @pl.when(ki == 0)
        def _():
            dq_ref[...] = jnp.zeros_like(dq_ref)

        dk_s[...] = jnp.zeros_like(dk_s)
        dv_s[...] = jnp.zeros_like(dv_s)
        kv, kr = kv_ref[...], kr_ref[...]

        def products(qi, masked):
            rows = pl.ds(pl.multiple_of(qi * t, t), t)
            q, do = q_ref[rows, :], do_ref[rows, :]
            st, k = _att_scores(q, kv, kr, masked, True)
            return st, lax.dot_general(kv[:, NOPE:], do, NT, preferred_element_type=F32), q, do, rows, k

        def update(st, dpt, q, do, rows, k):
            pt = jnp.exp2(st - lse_ref[:, rows])
            dv_s[...] += jnp.dot(pt.astype(BF16), do, preferred_element_type=F32)
            dst = (pt * (dpt - dl_ref[:, rows]) * scale).astype(BF16)
            dk_s[...] += jnp.dot(dst, q, preferred_element_type=F32)
            dq_ref[rows, :] += lax.dot_general(dst, k, TN, preferred_element_type=F32)

        def pair(q0, q1, masked):
            first, second = products(q0, masked), products(q1, False)
            update(*first)
            update(*second)

        @pl.when(ki == nq - 1)
        def _():
            update(*products(ki, True))

        @pl.when(ki < nq - 1)
        def _():
            pair(ki, ki + 1, True)
            _in_pairs(ki + 2, nq, lambda a, b: pair(a, b, False), lambda qi: update(*products(qi, False)))

        dk = dk_s[...] * (1.0 / LOG2_SCALE)
        dkv_ref[...] = jnp.concatenate([dk[:, :NOPE], dv_s[...]], axis=1).astype(dkv_ref.dtype)
        dkr_ref[...] = dk[:, NOPE:]

    row = pl.BlockSpec((None, 1, s_len), lambda h, ki: (h, 0, 0))
    return pl.pallas_call(
        body, name="flash_bwd", grid=(HEADS, nq),
        in_specs=[pl.BlockSpec((s_len, HEAD_PAD), lambda h, ki: (0, h)),
                  pl.BlockSpec((t, HEAD_PAD), lambda h, ki: (ki, h)),
                  pl.BlockSpec((t, LANE), lambda h, ki: (ki, 0)),
                  pl.BlockSpec((s_len, VHEAD), lambda h, ki: (0, h)), row, row],
        out_specs=[pl.BlockSpec((s_len, HEAD_PAD), lambda h, ki: (0, h)),
                   pl.BlockSpec((t, HEAD_PAD), lambda h, ki: (ki, h)),
                   pl.BlockSpec((t, LANE), lambda h, ki: (ki, h))],
        out_shape=[jax.ShapeDtypeStruct((s_len, HEADS * HEAD_PAD), F32),
                   jax.ShapeDtypeStruct((s_len, HEADS * HEAD_PAD), BF16),
                   jax.ShapeDtypeStruct((s_len, HEADS * LANE), F32)],
        scratch_shapes=[pltpu.VMEM((t, HEAD_PAD), F32), pltpu.VMEM((t, VHEAD), F32)],
        compiler_params=_params(("parallel", "arbitrary")),
    )(qb, kvb, krb, dob, lse_row, delta_row)


def _tril(w):
    r = lax.broadcasted_iota(jnp.int32, w.shape, 0)
    c = lax.broadcasted_iota(jnp.int32, w.shape, 1)
    return jnp.where(c <= r, w, 0.0)


def _sgu_row_stats(z_ref, width, gd, v_s, act, extra_s=None):
    total = None
    for g in range(SGU_GROUPS):
        cols = slice(g * gd, (g + 1) * gd)
        v = act(z_ref[:, width + g * gd:width + (g + 1) * gd])
        if extra_s is not None:
            v, extra_s[:, cols] = v
        v_s[:, cols] = v
        part = jnp.sum(v, axis=1, keepdims=True)
        total = part if total is None else total + part
    mean = total * (1.0 / width)
    sq = None
    for g in range(SGU_GROUPS):
        d = v_s[:, g * gd:(g + 1) * gd] - mean
        part = jnp.sum(d * d, axis=1, keepdims=True)
        sq = part if sq is None else sq + part
    return mean, lax.rsqrt(sq * (1.0 / width) + LN_EPS)


def _sgu_fwd(zpre, ln_g, ln_b, w_sp, bias_full):
    s_len, two_w = zpre.shape
    width = two_w // 2
    gd = width // SGU_GROUPS
    t = SGU_CHUNK

    per_step = 2 if s_len % (2 * t) == 0 else 1

    def body(z_all, g_ref, b_ref, w_ref, bias_ref, uv_all, v_s):
        for ch in range(per_step):
            z_ref, uv_ref = z_all.at[ch * t:(ch + 1) * t], uv_all.at[ch * t:(ch + 1) * t]
            mean, rstd = _sgu_row_stats(z_ref, width, gd, v_s, _gelu)
            for g in range(SGU_GROUPS):
                cols = slice(g * gd, (g + 1) * gd)
                vln = ((v_s[:, cols] - mean) * rstd * g_ref[:, cols] + b_ref[:, cols]).astype(BF16)
                mixed = jnp.dot(_tril(w_ref[g]).astype(BF16), vln, preferred_element_type=F32) + bias_ref[:, cols]
                uv_ref[:, cols] = (_gelu(z_ref[:, cols]) * mixed).astype(uv_ref.dtype)

    return pl.pallas_call(
        body, name="sgu_fwd", grid=(s_len // (per_step * t),),
        in_specs=[pl.BlockSpec((per_step * t, two_w), lambda i: (i, 0)), pl.BlockSpec((1, width), lambda i: (0, 0)),
                  pl.BlockSpec((1, width), lambda i: (0, 0)), pl.BlockSpec(w_sp.shape, lambda i: (0, 0, 0)),
                  pl.BlockSpec((t, width), lambda i: (0, 0))],
        out_specs=pl.BlockSpec((per_step * t, width), lambda i: (i, 0)),
        out_shape=jax.ShapeDtypeStruct((s_len, width), BF16),
        scratch_shapes=[pltpu.VMEM((t, width), F32)],
        compiler_params=_params(("parallel",)),
    )(zpre, ln_g, ln_b, w_sp, bias_full)


def _sgu_bwd(zpre, duv, ln_g, ln_b, w_sp, bias_full):
    s_len, two_w = zpre.shape
    width = two_w // 2
    gd = width // SGU_GROUPS
    t = SGU_CHUNK

    def body(z_ref, duv_ref, g_ref, b_ref, w_ref, bias_ref, dz_ref, dg_ref, db_ref, dw_ref, dbias_ref, v_s, vgrad_s, dvhat_s):
        @pl.when(pl.program_id(0) == 0)
        def _():
            for ref in (dg_ref, db_ref, dw_ref, dbias_ref):
                ref[...] = jnp.zeros_like(ref)

        def accumulate(ref, val):
            ref[...] += val

        mean, rstd = _sgu_row_stats(z_ref, width, gd, v_s, _gelu_and_grad, vgrad_s)
        sum_dvhat = sum_dvhat_vhat = None
        for g in range(SGU_GROUPS):
            cols = slice(g * gd, (g + 1) * gd)
            vhat = (v_s[:, cols] - mean) * rstd
            vln = (vhat * g_ref[:, cols] + b_ref[:, cols]).astype(BF16)
            wc = _tril(w_ref[g]).astype(BF16)
            mixed = jnp.dot(wc, vln, preferred_element_type=F32) + bias_ref[:, cols]
            u, u_grad = _gelu_and_grad(z_ref[:, cols])
            duv = duv_ref[:, cols]
            dz_ref[:, cols] = (duv * mixed * u_grad).astype(dz_ref.dtype)
            dmixed = duv * u
            dmb = dmixed.astype(BF16)
            dvln = lax.dot_general(wc, dmb, TN, preferred_element_type=F32)
            accumulate(dw_ref.at[g], _tril(lax.dot_general(dmb, vln, NT, preferred_element_type=F32)))
            accumulate(dbias_ref.at[:, g * LANE:(g + 1) * LANE],
                       jnp.broadcast_to(jnp.sum(dmixed, axis=1, keepdims=True), (t, LANE)))
            accumulate(dg_ref.at[:, cols], jnp.sum(dvln * vhat, axis=0, keepdims=True))
            accumulate(db_ref.at[:, cols], jnp.sum(dvln, axis=0, keepdims=True))
            dvhat = dvln * g_ref[:, cols]
            dvhat_s[:, cols] = dvhat
            parts = jnp.sum(dvhat, axis=1, keepdims=True), jnp.sum(dvhat * vhat, axis=1, keepdims=True)
            sum_dvhat = parts[0] if sum_dvhat is None else sum_dvhat + parts[0]
            sum_dvhat_vhat = parts[1] if sum_dvhat_vhat is None else sum_dvhat_vhat + parts[1]
        mean_dvhat, mean_dvhat_vhat = sum_dvhat * (1.0 / width), sum_dvhat_vhat * (1.0 / width)
        for g in range(SGU_GROUPS):
            cols = slice(g * gd, (g + 1) * gd)
            vhat = (v_s[:, cols] - mean) * rstd
            dv0 = rstd * (dvhat_s[:, cols] - mean_dvhat - vhat * mean_dvhat_vhat)
            dz_ref[:, width + g * gd:width + (g + 1) * gd] = (dv0 * vgrad_s[:, cols]).astype(dz_ref.dtype)

    vec = pl.BlockSpec((1, width), lambda i: (0, 0))
    return pl.pallas_call(
        body, name="sgu_bwd", grid=(s_len // t,),
        in_specs=[pl.BlockSpec((t, two_w), lambda i: (i, 0)), pl.BlockSpec((t, width), lambda i: (i, 0)), vec, vec,
                  pl.BlockSpec(w_sp.shape, lambda i: (0, 0, 0)), pl.BlockSpec((t, width), lambda i: (0, 0))],
        out_specs=[pl.BlockSpec((t, two_w), lambda i: (i, 0)), vec, vec,
                   pl.BlockSpec(w_sp.shape, lambda i: (0, 0, 0)), pl.BlockSpec((t, SGU_GROUPS * LANE), lambda i: (0, 0))],
        out_shape=[jax.ShapeDtypeStruct((s_len, two_w), BF16), jax.ShapeDtypeStruct((1, width), F32),
                   jax.ShapeDtypeStruct((1, width), F32), jax.ShapeDtypeStruct(w_sp.shape, F32),
                   jax.ShapeDtypeStruct((t, SGU_GROUPS * LANE), F32)],
        scratch_shapes=[pltpu.VMEM((t, width), F32)] * 3,
        compiler_params=_params(("arbitrary",)),
    )(zpre, duv, ln_g, ln_b, w_sp, bias_full)


def _rope_tables(positions):
    inv_freq = ROPE_THETA ** (-jnp.arange(0, ROPE, 2, dtype=F32) / ROPE)
    ang = positions.astype(F32)[:, None] * inv_freq
    cos, sin = jnp.cos(ang), jnp.sin(ang)
    z32, z64 = jnp.zeros_like(cos), jnp.zeros((cos.shape[0], LANE - ROPE), F32)
    return (jnp.concatenate([cos, cos, z64], axis=1), jnp.concatenate([z32, sin, z64], axis=1),
            jnp.concatenate([-sin, z32, z64], axis=1))


def _residual(g_next):
    if g_next is None:
        return dict(out_dtypes=(F32,), epilogue=lambda acc, res: (acc + res,))

    def epilogue(acc, res, gv):
        x_new = acc + res
        return x_new, _rms_fwd(x_new, gv)

    return dict(out_dtypes=(F32, BF16), epilogue=epilogue, consts=[g_next])


def _ffn_fwd(x, h2, w_up3, w_down3, g_next):
    def sq_relu(acc):
        r = jnp.maximum(acc, 0.0)
        return r * r, 2.0 * r

    r, r_grad = _mm_stacked("ffn_up", h2, w_up3, "col", (BF16, BF16), sq_relu)
    x_out, *h_next = _mm_stacked("ffn_down", r, w_down3, "row", extras=[x], **_residual(g_next))
    return x_out, (x, h2, r, r_grad), (h_next[0] if h_next else None)


def _ffn_bwd(dx, dxb, saved, g, w_up3, w_down3):
    x, h2, r, r_grad = saved
    da = _mm_stacked_nt("ffn_down_dx", dxb, w_down3, "row", (BF16,),
                        lambda acc, rg: (acc * rg.astype(F32),), [r_grad], tm=2 * MM_TILE)[0]
    g_down = _mm_tn_stacked("ffn_down_dw", r, dxb, w_down3.shape, "row")
    dx, dxb, dg = _mm_stacked_nt("ffn_up_dx", da, w_up3, "col", **_norm_bwd(x, g, dx))
    g_up = _mm_tn_stacked("ffn_up_dw", h2, da, w_up3.shape, "col")
    return dx, dxb, dg, g_up, g_down


def _norm_bwd(x, g, dres):
    def epilogue(dh, xv, rv, gv):
        dxv, dg = _rms_bwd(dh, xv, gv)
        return dxv + rv, dxv + rv, dg

    return dict(out_dtypes=(F32, BF16), epilogue=epilogue, extras=[x, dres], consts=[g], sums=[g.shape], tm=MM_TILE // 2)


def _dot(a, b, dims=NN):
    return lax.dot_general(a.astype(BF16), b.astype(BF16), dims, preferred_element_type=F32)


def _mla_fwd(x, g, wdkv, q_norm, kv_norm, wq, wkv, wo, tables, g_next):
    d = x.shape[1]

    def project(xv, cs, s1, s2, gv, wdkv_v, qg, kg, wq_v, wkv_v):
        h = _rms_fwd(xv, gv).astype(BF16)
        lv = _dot(h, wdkv_v)
        cqn = _rms_fwd(lv[:, :Q_RANK], qg).astype(BF16)
        ckvn = _rms_fwd(lv[:, Q_RANK:Q_RANK + KV_RANK], kg).astype(BF16)
        return (h, lv, cqn, ckvn, _rope(lv[:, Q_RANK + KV_RANK:], cs, s1, s2), _dot(cqn, wq_v), _dot(ckvn, wkv_v))

    h, lat, cqn, ckvn, krb, q, kvb = _rowwise(
        "mla_project", project, [x, *tables], [g, wdkv, q_norm, kv_norm, wq, wkv],
        [(d, BF16), (LAT_PAD, F32), (Q_RANK, BF16), (KV_RANK, BF16), (LANE, BF16), (wq.shape[1], F32), (wkv.shape[1], BF16)],
        tr=512)
    ob, lse, qb = _flash_fwd(q, kvb, krb, tables)
    x_mid, h2 = _mm("mla_o", ob, wo, extras=[x], **_residual(g_next))
    return x_mid, (x, h, lat, cqn, ckvn, krb, qb, kvb, ob, lse), h2


def _mla_bwd(dx, dxb, saved, g, wdkv, q_norm, kv_norm, wq, wkv, wo, tables):
    x, h, lat, cqn, ckvn, krb, qb, kvb, ob, lse = saved
    s_len = x.shape[0]
    def with_delta(do, ov):
        prod = do * ov.astype(F32)
        lane = lax.broadcasted_iota(jnp.int32, (do.shape[0], LANE), 1)
        by_lane = None
        for hd in range(HEADS):
            total = jnp.broadcast_to(jnp.sum(prod[:, hd * VHEAD:(hd + 1) * VHEAD], axis=1, keepdims=True), lane.shape)
            by_lane = jnp.where(lane == hd, total, 0.0 if by_lane is None else by_lane)
        return do, by_lane

    dob, delta_lanes = _mm("mla_o_dx", dxb, wo, (BF16,), with_delta, [ob], nt=True, tm=MM_TILE // 2, narrow=[(LANE, F32)])
    g_wo = _mm_tn("mla_o_dw", ob, dxb)[0]
    delta_row = delta_lanes[:, :HEADS].T.reshape(HEADS, 1, s_len)
    dq, dkvb, dkr = _flash_bwd(qb, kvb, krb, dob, lse, delta_row)

    def project_bwd(dq_f, dkv_v, dkr_v, lv, cqn_v, ckvn_v, h_v, xv, rv, cs, s1, s2, gv, qg, kg, wq_v, wkv_v, wdkv_v):
        parts = []
        for hd in range(HEADS):
            parts += [dq_f[:, hd * HEAD_PAD:hd * HEAD_PAD + NOPE],
                      _rope_t(dq_f[:, hd * HEAD_PAD + NOPE:(hd + 1) * HEAD_PAD], cs, s1, s2)]
        dq_v = jnp.concatenate(parts, axis=1).astype(BF16)
        dcq, dqg = _rms_bwd(_dot(dq_v, wq_v, NT), lv[:, :Q_RANK], qg)
        dckv, dkg = _rms_bwd(_dot(dkv_v, wkv_v, NT), lv[:, Q_RANK:Q_RANK + KV_RANK], kg)
        dkr_sum = dkr_v[:, :LANE]
        for hd in range(1, HEADS):
            dkr_sum = dkr_sum + dkr_v[:, hd * LANE:(hd + 1) * LANE]
        dlat = jnp.concatenate([dcq, dckv, _rope_t(dkr_sum, cs, s1, s2)], axis=1).astype(BF16)
        dxv, dg = _rms_bwd(_dot(dlat, wdkv_v, NT), xv, gv)
        return (dxv + rv, dxv + rv, _dot(cqn_v, dq_v, TN), _dot(ckvn_v, dkv_v, TN), _dot(h_v, dlat, TN), dqg, dkg, dg)

    dx, dxb, g_wq, g_wkv, g_wdkv, g_qn, g_kvn, dg = _rowwise(
        "mla_project_bwd", project_bwd, [dq, dkvb, dkr, lat, cqn, ckvn, h, x, dx, *tables],
        [g, q_norm, kv_norm, wq, wkv, wdkv], [(x.shape[1], F32), (x.shape[1], BF16)],
        [wq.shape, wkv.shape, wdkv.shape, q_norm.shape, kv_norm.shape, g.shape], tr=256)
    return dx, dxb, dg, g_wdkv, g_qn, g_kvn, g_wq, g_wkv, g_wo


def _sgu_layer_fwd(x, h, w_in3, ln_g, ln_b, w_sp, bias_full, w_out3, g_next):
    zpre = _mm_stacked("sgu_in", h, w_in3, "col")[0]
    uv = _sgu_fwd(zpre, ln_g, ln_b, w_sp, bias_full)
    x_mid, h2 = _mm_stacked("sgu_out", uv, w_out3, "row", extras=[x], **_residual(g_next))
    return x_mid, (x, h, zpre, uv), h2


def _sgu_layer_bwd(dx, dxb, saved, g, w_in3, ln_g, ln_b, w_sp, bias_full, w_out3):
    x, h, zpre, uv = saved
    duv = _mm_stacked_nt("sgu_out_dx", dxb, w_out3, "row")[0]
    g_out = _mm_tn_stacked("sgu_out_dw", uv, dxb, w_out3.shape, "row")
    dz, g_lng, g_lnb, g_wsp, g_bias = _sgu_bwd(zpre, duv, ln_g, ln_b, w_sp, bias_full)
    g_in = _mm_tn_stacked("sgu_in_dw", h, dz, w_in3.shape, "col")
    dx, dxb, dg = _mm_stacked_nt("sgu_in_dx", dz, w_in3, "col", **_norm_bwd(x, g, dx))
    return dx, dxb, dg, g_in, g_out, g_lng, g_lnb, g_wsp, g_bias


def _loss_head(x, target, g):
    d = x.shape[1]

    def fn(xv, tv, gv):
        err = _rms_fwd(xv, gv) - tv
        dxv, dg = _rms_bwd(err * (1.0 / d), xv, gv)
        return dxv, dxv, dg, jnp.sum(err * err, axis=0, keepdims=True)

    return _rowwise("loss_head", fn, [x, target], [g], [(d, F32), (d, BF16)], [g.shape, g.shape])


def _mixer_weights(i, stacks):
    by_rows = lambda a: a.reshape(N_SHARDS * a.shape[1], a.shape[2])
    by_cols = lambda a: a.transpose(1, 0, 2).reshape(a.shape[1], N_SHARDS * a.shape[2])
    if i % 2:
        w_in3, w_out3, ln_g, ln_b = stacks
        return w_in3, ln_g.reshape(1, -1), ln_b.reshape(1, -1), w_out3
    wdkv = by_rows(stacks[0])
    wdkv = jnp.pad(wdkv, ((0, 0), (0, LAT_PAD - wdkv.shape[1])))
    wq = jnp.pad(by_cols(stacks[1]).reshape(Q_RANK, HEADS, QK_HEAD), ((0, 0), (0, 0), (0, HEAD_PAD - QK_HEAD)))
    return wdkv, wq.reshape(Q_RANK, HEADS * HEAD_PAD), by_cols(stacks[2]), by_rows(stacks[3])


def _local_step(x, positions, target, norm_mix, norm_ffn, final_norm, q_norm, kv_norm, w_sp, b_sp, mixers, ffn, reducer):
    tables = _rope_tables(positions)
    gd = mixers[1][2].size // SGU_GROUPS
    bias_full = [jnp.repeat(b_sp[j].T, gd, axis=1) for j in range(DEPTH // 2)]
    saved, mla, sgu = [], [None] * (DEPTH // 2), [None] * (DEPTH // 2)
    h = None
    for i in range(DEPTH):
        j = i // 2
        x, *stacks = lax.optimization_barrier((x, *mixers[i]))
        if i % 2 == 0:
            wdkv, wq, wkv, wo = mla[j] = _mixer_weights(i, stacks)
            x, s_mix, h2 = _mla_fwd(x, norm_mix[i:i + 1], wdkv, q_norm[j:j + 1], kv_norm[j:j + 1], wq, wkv, wo, tables,
                                    norm_ffn[i:i + 1])
        else:
            w_in3, ln_g, ln_b, w_out3 = sgu[j] = _mixer_weights(i, stacks)
            x, s_mix, h2 = _sgu_layer_fwd(x, h, w_in3, ln_g, ln_b, w_sp[j], bias_full[j], w_out3, norm_ffn[i:i + 1])
        next_is_sgu = i + 1 < DEPTH and (i + 1) % 2 == 1
        x, s_ffn, h = _ffn_fwd(x, h2, *ffn[i], norm_mix[i + 1:i + 2] if next_is_sgu else None)
        saved.append((s_mix, s_ffn))
    dx, dxb, g_final, sq_cols = _loss_head(x, target, final_norm[None, :])
    loss = 0.5 * jnp.sum(sq_cols) / x.shape[1]

    def pair(g):
        return g, g.astype(BF16)

    g_mix, g_ffn = [None] * DEPTH, [None] * DEPTH
    mla_g, sgu_g = [None] * (DEPTH // 2), [None] * (DEPTH // 2)
    for i in reversed(range(DEPTH)):
        j = i // 2
        s_mix, s_ffn = saved[i]
        dx, dxb, g_ffn[i], g_up, g_down = _ffn_bwd(dx, dxb, s_ffn, norm_ffn[i:i + 1], *ffn[i])
        dxb = reducer.add(f"ffn{i}", i, {"ffn_w_up": g_up, "ffn_w_down": g_down}, dxb)
        dxb = reducer.phase_end(dxb)
        if i % 2 == 0:
            wdkv, wq, wkv, wo = mla[j]
            dx, dxb, g_mix[i], g_wdkv, g_qn, g_kvn, g_wq, g_wkv, g_wo = _mla_bwd(
                dx, dxb, s_mix, norm_mix[i:i + 1], wdkv, q_norm[j:j + 1], kv_norm[j:j + 1], wq, wkv, wo, tables)
            mla_g[j] = (g_qn, g_kvn)
            g_wq = g_wq.reshape(Q_RANK, HEADS, HEAD_PAD)[..., :QK_HEAD].reshape(Q_RANK, N_SHARDS, -1)
            dxb = reducer.add(f"mla{j}", j, {
                "mla_w_dkv": pair(g_wdkv[:, :Q_RANK + KV_RANK + ROPE].reshape(N_SHARDS, -1, Q_RANK + KV_RANK + ROPE)),
                "mla_w_uq": pair(g_wq.transpose(1, 0, 2)),
                "mla_w_ukv": pair(g_wkv.reshape(KV_RANK, N_SHARDS, -1).transpose(1, 0, 2)),
                "mla_w_o": pair(g_wo.reshape(N_SHARDS, -1, g_wo.shape[1]))}, dxb)
        else:
            w_in3, ln_g, ln_b, w_out3 = sgu[j]
            dx, dxb, g_mix[i], g_in, g_out, g_lng, g_lnb, g_wsp, g_bias = _sgu_layer_bwd(
                dx, dxb, s_mix, norm_mix[i:i + 1], w_in3, ln_g, ln_b, w_sp[j], bias_full[j], w_out3)
            sgu_g[j] = (g_wsp, g_bias.reshape(SGU_CHUNK, SGU_GROUPS, LANE)[:, :, 0].T)
            dxb = reducer.add(f"sgu{j}", j, {"sgu_w_in": g_in, "sgu_w_out": g_out,
                                             "sgu_ln_g": pair(g_lng.reshape(N_SHARDS, -1, LANE)),
                                             "sgu_ln_b": pair(g_lnb.reshape(N_SHARDS, -1, LANE))}, dxb)
        dxb = reducer.phase_end(dxb)
    small = dict(
        norm_mix=jnp.concatenate(g_mix, axis=0), norm_ffn=jnp.concatenate(g_ffn, axis=0), final_norm=g_final[0],
        q_norm=jnp.concatenate([m[0] for m in mla_g], axis=0), kv_norm=jnp.concatenate([m[1] for m in mla_g], axis=0),
        w_sp=jnp.stack([s[0] for s in sgu_g]), b_sp=jnp.stack([s[1] for s in sgu_g]))
    return loss, dx, small


HBM_SPEC = pl.BlockSpec(memory_space=pltpu.HBM)


def _place():
    x, y, c = lax.axis_index("x"), lax.axis_index("y"), lax.axis_index("c")
    return x, y, c, [(1 - x, y), (x, 1 - y), (1 - x, 1 - y)]


def _remote(src, dst, send_sems, recv_sems, k, to):
    return pltpu.make_async_remote_copy(src_ref=src, dst_ref=dst, send_sem=send_sems.at[k], recv_sem=recv_sems.at[k],
                                        device_id=to, device_id_type=MESH)


def _gather_layer(tag, shards):
    n = len(shards)
    split = [s.shape[0] >= 16 for s in shards]

    def body(*refs):
        ins, outs = refs[:n], refs[n:2 * n]
        send_sems, recv_sems, local_sems = refs[2 * n:]
        x, y, c, chips = _place()
        mine = 2 * x + y
        barrier = pltpu.get_barrier_semaphore()
        peers = [(x, y, 1 - c)] + [(*chip, c) for chip in chips]
        for peer in peers:
            pl.semaphore_signal(barrier, inc=1, device_id=peer, device_id_type=MESH)
        pl.semaphore_wait(barrier, len(peers))

        def rows(t, half):
            hr = shards[t].shape[0] // 2
            return pl.ds(half * hr, hr) if split[t] else pl.ds(0, shards[t].shape[0])

        local, sent = [], []
        for t in range(n):
            local.append(pltpu.make_async_copy(ins[t], outs[t].at[mine], local_sems.at[t]))
            local[-1].start()
            for j, chip in enumerate(chips):
                cp = _remote(ins[t].at[rows(t, c)], outs[t].at[mine, rows(t, c)], send_sems, recv_sems, 3 * t + j, (*chip, c))
                cp.start()
                sent.append(cp)
        for j, chip in enumerate(chips):
            theirs = 2 * chip[0] + chip[1]
            for t in range(n):
                piece = outs[t].at[theirs, rows(t, c)]
                _remote(piece, piece, send_sems, recv_sems, 3 * t + j, (x, y, c)).wait_recv()
                if split[t]:
                    cp = _remote(piece, piece, send_sems, recv_sems, 3 * n + 3 * t + j, (x, y, 1 - c))
                    cp.start()
                    sent.append(cp)
        for j, chip in enumerate(chips):
            theirs = 2 * chip[0] + chip[1]
            for t in range(n):
                if split[t]:
                    piece = outs[t].at[theirs, rows(t, 1 - c)]
                    _remote(piece, piece, send_sems, recv_sems, 3 * n + 3 * t + j, (x, y, c)).wait_recv()
        for cp in sent:
            cp.wait_send()
        for cp in local:
            cp.wait()

    return pl.kernel(
        body, name=f"gather_{tag}", mesh=plsc.ScalarSubcoreMesh(axis_name="sequencer", num_cores=1),
        out_type=[jax.ShapeDtypeStruct((N_SHARDS, *s.shape), s.dtype) for s in shards],
        scratch_types=[pltpu.SemaphoreType.DMA((6 * n,)), pltpu.SemaphoreType.DMA((6 * n,)), pltpu.SemaphoreType.DMA((n,))],
        compiler_params=pltpu.CompilerParams(collective_id=ID_GATHER),
    )(*shards)


SEQUENCER = dict(axis_name="sequencer", num_cores=1)
ID_GATHER, ID_SHARE = 0, 1
MIN_SPLIT_ROWS = 16


def _handshake(peers):
    barrier = pltpu.get_barrier_semaphore()
    for peer in peers:
        pl.semaphore_signal(barrier, inc=1, device_id=peer, device_id_type=MESH)
    pl.semaphore_wait(barrier, len(peers))


def _half_rows(rows, half):
    return pl.ds(half * (rows // 2), rows // 2) if rows >= MIN_SPLIT_ROWS else pl.ds(0, rows)


SEM_SPEC = pl.BlockSpec(memory_space=pltpu.SEMAPHORE)
DATAFLOW = pltpu.SideEffectType.DATAFLOW_SIDE_EFFECTING


def _exchange_copies(shapes, stacks, lands, send_sems, recv_sems):
    x, y, c, chips = _place()
    mine = 2 * x + y
    copies = []
    for t, shape in enumerate(shapes):
        r = shape[1]
        copies.append(_remote(stacks[t].at[mine, _half_rows(r, 1 - c)], lands[t].at[0], send_sems, recv_sems, 7 * t, (x, y, 1 - c)))
        for j, chip in enumerate(chips):
            theirs = 2 * chip[0] + chip[1]
            copies.append(_remote(stacks[t].at[theirs, _half_rows(r, c)], lands[t].at[1 + j], send_sems, recv_sems,
                                  7 * t + 1 + j, (*chip, c)))
            copies.append(_remote(stacks[t].at[theirs, _half_rows(r, 1 - c)], lands[t].at[4 + j], send_sems, recv_sems,
                                  7 * t + 4 + j, (*chip, 1 - c)))
    return copies


def _exchange_start(tag, stacks, carry):
    n = len(stacks)
    shapes = [s.shape for s in stacks]
    lands = [lax.empty((7, s.shape[1] // 2 if s.shape[1] >= MIN_SPLIT_ROWS else s.shape[1], s.shape[2]), s.dtype) for s in stacks]

    def body(*refs):
        for cp in _exchange_copies(shapes, refs[:n], refs[n:2 * n], refs[2 * n + 1], refs[2 * n + 2]):
            cp.start()

    through = (*stacks, *lands, carry)
    out = pl.pallas_call(
        body, name=f"reduce_exchange_start_{tag}",
        out_shape=(pltpu.SemaphoreType.DMA((7 * n,)), pltpu.SemaphoreType.DMA((7 * n,)),
                   *[pltpu.HBM(a.shape, a.dtype) for a in through]),
        in_specs=[HBM_SPEC] * (2 * n + 1),
        out_specs=(SEM_SPEC, SEM_SPEC, *[HBM_SPEC] * (2 * n + 1)),
        input_output_aliases={t: 2 + t for t in range(2 * n + 1)},
        compiler_params=pltpu.CompilerParams(has_side_effects=DATAFLOW),
    )(*[pltpu.with_memory_space_constraint(a, pltpu.HBM) for a in through])
    return out[0], out[1], out[2:2 + n], out[2 + n:2 + 2 * n], out[-1]


def _exchange_wait(tag, send_sems, recv_sems, stacks, lands, after):
    n = len(stacks)
    shapes = [s.shape for s in stacks]

    def body(*refs):
        for cp in _exchange_copies(shapes, refs[:n], refs[n:2 * n], refs[2 * n], refs[2 * n + 1]):
            cp.wait()

    out = pl.pallas_call(
        body, name=f"reduce_exchange_wait_{tag}",
        out_shape=tuple(pltpu.HBM(a.shape, a.dtype) for a in (*stacks, *lands)),
        in_specs=[HBM_SPEC] * (2 * n) + [SEM_SPEC, SEM_SPEC, pl.BlockSpec(memory_space=pl.ANY)],
        out_specs=tuple([HBM_SPEC] * (2 * n)),
        input_output_aliases={t: t for t in range(2 * n)},
        compiler_params=pltpu.CompilerParams(has_side_effects=DATAFLOW),
    )(*stacks, *lands, send_sems, recv_sems, after)
    return out[:n], out[n:]


def _share_halves(tag, halves):
    n = len(halves)

    def body(*refs):
        ins, outs, send_sems, recv_sems = refs[:n], refs[n:2 * n], refs[2 * n], refs[2 * n + 1]
        x, y, c, _ = _place()
        _handshake([(x, y, 1 - c)])
        sent = [_remote(ins[t], outs[t], send_sems, recv_sems, t, (x, y, 1 - c)) for t in range(n)]
        for cp in sent:
            cp.start()
        for cp in sent:
            cp.wait()

    return pl.kernel(
        body, name=f"reduce_share_{tag}", mesh=plsc.ScalarSubcoreMesh(**SEQUENCER),
        out_type=[jax.ShapeDtypeStruct(h.shape, h.dtype) for h in halves],
        scratch_types=[pltpu.SemaphoreType.DMA((n,)), pltpu.SemaphoreType.DMA((n,))],
        compiler_params=pltpu.CompilerParams(collective_id=ID_SHARE),
    )(*halves)


def _all_reduce_small(part):
    rows = part.shape[0]
    half = rows // 2

    def body(p_ref, out_ref, sib_buf, chip_sums, send_sems, recv_sems):
        x, y, c, chips = _place()
        mine = 2 * x + y
        my_rows = pl.ds(pl.multiple_of(c * half, 8), half)
        swap = _remote(p_ref, sib_buf, send_sems, recv_sems, 0, (x, y, 1 - c))
        swap.start()
        swap.wait()
        chip_sums[mine] = p_ref[...] + sib_buf[...]
        sent = [_remote(chip_sums.at[mine, my_rows], chip_sums.at[mine, my_rows], send_sems, recv_sems, 1 + j, (*chip, c))
                for j, chip in enumerate(chips)]
        for cp in sent:
            cp.start()
        for j, chip in enumerate(chips):
            sent[j].wait_send()
            theirs = chip_sums.at[2 * chip[0] + chip[1], my_rows]
            _remote(theirs, theirs, send_sems, recv_sems, 1 + j, (x, y, c)).wait_recv()
        out_ref[my_rows, :] = ((chip_sums[0, my_rows, :] + chip_sums[1, my_rows, :]) + chip_sums[2, my_rows, :]) + chip_sums[3, my_rows, :]
        share = _remote(out_ref.at[my_rows], out_ref.at[my_rows], send_sems, recv_sems, 4, (x, y, 1 - c))
        share.start()
        share.wait_send()
        other = out_ref.at[pl.ds(pl.multiple_of((1 - c) * half, 8), half)]
        _remote(other, other, send_sems, recv_sems, 4, (x, y, c)).wait_recv()

    vmem = pl.BlockSpec(memory_space=pltpu.VMEM)
    return pl.pallas_call(
        body, name="all_reduce_small", in_specs=[vmem], out_specs=vmem, out_shape=jax.ShapeDtypeStruct(part.shape, F32),
        scratch_shapes=[pltpu.VMEM((rows, LANE), F32), pltpu.VMEM((N_SHARDS, rows, LANE), F32),
                        pltpu.SemaphoreType.DMA((5,)), pltpu.SemaphoreType.DMA((5,))],
        compiler_params=pltpu.CompilerParams(vmem_limit_bytes=VMEM_LIMIT_BYTES),
    )(part)


def _sum_partials(g3, others, sel):
    _, rows, c = others.shape
    whole = g3.shape[1] == rows
    tr = _tile(rows, UPDATE_ROWS)
    nb = rows // tr

    def body(sel_ref, g_ref, slots_ref, out_ref):
        same = g_ref[...].astype(F32)
        for k in (1, 2, 3):
            same = same + slots_ref[k].astype(F32)
        other = slots_ref[0].astype(F32)
        for k in (4, 5, 6):
            other = other + slots_ref[k].astype(F32)
        out_ref[...] = same + other

    return pl.pallas_call(
        body, name="reduce_sum_partials",
        grid_spec=pltpu.PrefetchScalarGridSpec(
            num_scalar_prefetch=1, grid=(nb,),
            in_specs=[pl.BlockSpec((None, tr, c), lambda i, sr: (sr[0], (0 if whole else sr[1] * nb) + i, 0)),
                      pl.BlockSpec((7, tr, c), lambda i, sr: (0, i, 0))],
            out_specs=pl.BlockSpec((tr, c), lambda i, sr: (i, 0))),
        out_shape=jax.ShapeDtypeStruct((rows, c), F32),
        compiler_params=_params(("parallel",)),
    )(sel, g3, others)


def _adamw_math(w, g, m, v):
    nm = ADAM_B1 * m + (1.0 - ADAM_B1) * g
    nv = ADAM_B2 * v + (1.0 - ADAM_B2) * (g * g)
    m_hat = nm / (1.0 - ADAM_B1 ** ADAM_STEP)
    v_hat = nv / (1.0 - ADAM_B2 ** ADAM_STEP)
    return -ADAM_LR * (m_hat / (jnp.sqrt(v_hat) + ADAM_EPS) + ADAM_WD * w), nm, nv


def _adamw_layer(layer, w, m, v, g_mine, g_sibling, sel, prev):
    lyr, r, c = w.shape
    rows = g_mine.shape[0]
    halves = r // rows
    tr = _tile(rows, 512)
    nb = rows // tr
    n_g = 1 if g_sibling is None else 2

    def body(sel_ref, w_ref, m_ref, v_ref, *rest):
        g = rest[0][...]
        if n_g == 2:
            g = jnp.where(pl.program_id(0) == sel_ref[1], g, rest[1][...])
        outs = rest[n_g + (0 if prev is None else 4):]
        d, nm, nv = _adamw_math(w_ref[...], g, m_ref[...], v_ref[...])
        for ref, val in zip(outs, (g, d, nm, nv)):
            ref[...] = val

    full = pl.BlockSpec((None, tr, c), lambda h, i, sr: (layer, h * nb + i, 0))
    part = pl.BlockSpec((tr, c), lambda h, i, sr: (i, 0))
    n_in = 4 + n_g
    return pl.pallas_call(
        body, name="adamw_layer",
        grid_spec=pltpu.PrefetchScalarGridSpec(
            num_scalar_prefetch=1, grid=(halves, nb),
            in_specs=[full] * 3 + [part] * n_g + ([] if prev is None else [pl.BlockSpec(memory_space=pl.ANY)] * 4),
            out_specs=[full] * 4),
        out_shape=[jax.ShapeDtypeStruct(w.shape, F32)] * 4,
        input_output_aliases={} if prev is None else {n_in + k: k for k in range(4)},
        compiler_params=_params(("parallel", "parallel")),
    )(sel, w, m, v, g_mine, *([] if g_sibling is None else [g_sibling]), *([] if prev is None else prev))


def _adamw(w, g, m, v):
    lyr, r, c = w.shape
    tr = _tile(r, 256)

    def body(w_ref, g_ref, m_ref, v_ref, d_ref, nm_ref, nv_ref):
        d_ref[...], nm_ref[...], nv_ref[...] = _adamw_math(w_ref[...], g_ref[...], m_ref[...], v_ref[...])

    blk = pl.BlockSpec((None, tr, c), lambda l, i: (l, i, 0))
    return pl.pallas_call(
        body, name="adamw", grid=(lyr, r // tr), in_specs=[blk] * 4, out_specs=[blk] * 3,
        out_shape=[jax.ShapeDtypeStruct(w.shape, F32)] * 3,
        compiler_params=_params(("parallel", "parallel")),
    )(w, g, m, v)


SHARDED = ("mla_w_dkv", "mla_w_uq", "mla_w_ukv", "mla_w_o", "sgu_w_in", "sgu_ln_g", "sgu_ln_b", "sgu_w_out",
           "ffn_w_up", "ffn_w_down")
REPLICATED = ("norm_mix", "norm_ffn", "final_norm", "mla_q_norm", "mla_kv_norm", "sgu_w_spatial", "sgu_b_spatial")
WEIGHTS = ("norm_mix", "norm_ffn", "final_norm", "mla_w_dkv", "mla_q_norm", "mla_kv_norm", "mla_w_uq", "mla_w_ukv",
           "mla_w_o", "sgu_w_in", "sgu_ln_g", "sgu_ln_b", "sgu_w_spatial", "sgu_b_spatial", "sgu_w_out", "ffn_w_up",
           "ffn_w_down")


class _Reducer:
    def __init__(self, state, sel):
        self.state, self.sel = state, sel
        self.started, self.travelling, self.summed = [], [], []
        self.done = {}

    def add(self, tag, layer, grads, token):
        names = list(grads)
        token, *tied = lax.optimization_barrier((token, *[a for n in names for a in grads[n]]))
        f32s, bf16s = tied[0::2], tied[1::2]
        *flying, token = _exchange_start(tag, bf16s, token)
        self.started.append((tag, layer, names, f32s, flying))
        return token

    def phase_end(self, token):
        for tag, layer, names, f32s, flying in self.travelling:
            bf16s, received = _exchange_wait(tag, *flying, token)
            own = [g if g.shape[1] >= MIN_SPLIT_ROWS else gb for g, gb in zip(f32s, bf16s)]
            mine = [_sum_partials(g, got, self.sel) for g, got in zip(own, received)]
            token, *mine = lax.optimization_barrier((token, *mine))
            cut = [k for k, g in enumerate(own) if g.shape[1] >= MIN_SPLIT_ROWS]
            theirs = dict(zip(cut, _share_halves(tag, [mine[k] for k in cut])))
            self.summed.append((layer, names, mine, [theirs.get(k) for k in range(len(names))]))
        self.travelling, self.started = self.started, []
        return token

    def update(self, token):
        for layer, names, mine, theirs in self.summed:
            for name, g_mine, g_theirs in zip(names, mine, theirs):
                w, m, v = self.state[name]
                self.done[name] = _adamw_layer(layer, w, m, v, g_mine, g_theirs, self.sel, self.done.get(name))
                token = self.done[name][1]
        self.summed = []
        return token


def _as3d(name, a):
    return a.reshape(a.shape[0], -1, LANE) if name in ("sgu_ln_g", "sgu_ln_b") else a


def _pack(parts):
    flat = jnp.concatenate([p.reshape(-1) for p in parts])
    rows = -(-flat.shape[0] // (256 * LANE)) * 256
    return jnp.pad(flat, (0, rows * LANE - flat.shape[0])).reshape(rows, LANE)


def _unpack(packed, like):
    flat, out, at = packed.reshape(-1), [], 0
    for p in like:
        out.append(flat[at:at + p.size].reshape(p.shape))
        at += p.size
    return out


def kernel(x, positions, norm_mix, norm_ffn, final_norm, mla_w_dkv, mla_q_norm, mla_kv_norm, mla_w_uq, mla_w_ukv, mla_w_o, sgu_w_in, sgu_ln_g, sgu_ln_b, sgu_w_spatial, sgu_b_spatial, sgu_w_out, ffn_w_up, ffn_w_down, loss_target, m_norm_mix, m_norm_ffn, m_final_norm, m_mla_w_dkv, m_mla_q_norm, m_mla_kv_norm, m_mla_w_uq, m_mla_w_ukv, m_mla_w_o, m_sgu_w_in, m_sgu_ln_g, m_sgu_ln_b, m_sgu_w_spatial, m_sgu_b_spatial, m_sgu_w_out, m_ffn_w_up, m_ffn_w_down, v_norm_mix, v_norm_ffn, v_final_norm, v_mla_w_dkv, v_mla_q_norm, v_mla_kv_norm, v_mla_w_uq, v_mla_w_ukv, v_mla_w_o, v_sgu_w_in, v_sgu_ln_g, v_sgu_ln_b, v_sgu_w_spatial, v_sgu_b_spatial, v_sgu_w_out, v_ffn_w_up, v_ffn_w_down):
    given = dict(locals())
    w = {n: given[n] for n in WEIGHTS}
    mom = {n: given["m_" + n] for n in WEIGHTS}
    var = {n: given["v_" + n] for n in WEIGHTS}
    mixers, ffn, token = [], [], None
    for i in range(DEPTH):
        j = i // 2
        if i % 2 == 0:
            mixer = [w[n][j].astype(BF16) for n in ("mla_w_dkv", "mla_w_uq", "mla_w_ukv", "mla_w_o")]
        else:
            mixer = [sgu_w_in[j].astype(BF16), sgu_w_out[j].astype(BF16), sgu_ln_g[j].reshape(-1, LANE),
                     sgu_ln_b[j].reshape(-1, LANE)]
        for tag, shards, into in ((f"mixer{i}", mixer, mixers), (f"ffn{i}", [ffn_w_up[i].astype(BF16), ffn_w_down[i].astype(BF16)], ffn)):
            if token is None:
                token = shards[0]
            else:
                token, *shards = lax.optimization_barrier((token, *shards))
            into.append(_gather_layer(tag, shards))

    x_i, y_i, c_i = lax.axis_index("x"), lax.axis_index("y"), lax.axis_index("c")
    sel = jnp.stack([2 * x_i + y_i, c_i]).astype(jnp.int32)
    reducer = _Reducer({n: tuple(_as3d(n, d[n]) for d in (w, mom, var)) for n in SHARDED}, sel)
    loss, dx, small = _local_step(
        x[0], positions[0], loss_target[0], norm_mix, norm_ffn, final_norm, mla_q_norm, mla_kv_norm, sgu_w_spatial,
        sgu_b_spatial, mixers, ffn, reducer)
    loss = lax.psum(loss, ("x", "y", "c"))

    small_g = [small["norm_mix"], small["norm_ffn"], small["final_norm"], small["q_norm"], small["kv_norm"],
               small["w_sp"], small["b_sp"]]
    like = [w[n] for n in REPLICATED]
    g_small = _all_reduce_small(_pack(small_g))
    packed = [_pack([d[n] for n in REPLICATED])[None] for d in (w, mom, var)]
    upd_small = _adamw(packed[0], g_small[None], packed[1], packed[2])
    grads = dict(zip(REPLICATED, _unpack(g_small, like)))
    delta, new_m, new_v = ({n: a for n, a in zip(REPLICATED, _unpack(u[0], like))} for u in upd_small)

    reducer.phase_end(reducer.update(upd_small[0]))
    reducer.update(None)
    for n in SHARDED:
        grads[n], delta[n], new_m[n], new_v[n] = (a.reshape(w[n].shape) for a in reducer.done[n])

    return (loss, dx[None], *[grads[n] for n in WEIGHTS], *[delta[n] for n in WEIGHTS],
            *[new_m[n] for n in WEIGHTS], *[new_v[n] for n in WEIGHTS])
```
